```python
import math
import jax, jax.numpy as jnp
from jax import lax
import numpy as np

D_MODEL = 2048
BATCH = 8
SEQ = 2048
DEPTH = 1

CTX_LEN = 256
GRID_W = 64
EPS = 1e-6
S5_WIDTH = D_MODEL // 2
S5_GROUP = 16
S5_GROUPS = S5_WIDTH // S5_GROUP
S5_STATE = 64
MLA_HEADS = 8
QK_NOPE = 128
QK_ROPE = 64
V_DIM = 128
Q_RANK = 512
KV_RANK = 256
ROPE_BASE = 10000.0
Q_BLOCK = 128
ATTN_SCALE = (QK_NOPE + QK_ROPE) ** -0.5
N_BRANCH = 2
D_FF = -(-8 * D_MODEL // (3 * 256)) * 256
IN_COLS = S5_WIDTH + Q_RANK + KV_RANK + QK_ROPE + N_BRANCH * D_MODEL

kernel_name = 'hybrid_s5_mla_dit_block'


def rmsnorm(x, g):
    xf = x.astype(jnp.float32)
    y = xf * lax.rsqrt(jnp.mean(xf * xf, axis=-1, keepdims=True) + EPS)
    return (y * g.astype(jnp.float32)).astype(x.dtype)


def ada(cvec, w_mod, b_mod):
    m = jax.nn.silu(cvec) @ w_mod + b_mod
    return m.reshape(m.shape[:-1] + (6, D_MODEL))


def rope2d_tables(n_tokens):
    rows = n_tokens // GRID_W
    row = jnp.repeat(jnp.arange(rows, dtype=jnp.float32), GRID_W)
    col = jnp.tile(jnp.arange(GRID_W, dtype=jnp.float32), rows)
    n_freq = QK_ROPE // 4
    inv = ROPE_BASE ** (-jnp.arange(n_freq, dtype=jnp.float32) / n_freq)
    ang = jnp.stack([row[:, None] * inv, col[:, None] * inv], axis=1)
    return jnp.cos(ang), jnp.sin(ang)


def apply_rope2d(x, cos, sin):
    xs = x.reshape(x.shape[:-1] + (2, 2, QK_ROPE // 4))
    x1, x2 = xs[..., 0, :], xs[..., 1, :]
    c = cos[None, :, None].astype(x.dtype)
    s = sin[None, :, None].astype(x.dtype)
    out = jnp.stack([x1 * c - x2 * s, x2 * c + x1 * s], axis=-2)
    return out.reshape(x.shape)


def split_in(h):
    o = S5_WIDTH
    u = h[..., :o]
    cq = h[..., o:o + Q_RANK]
    o += Q_RANK
    ckv = h[..., o:o + KV_RANK]
    o += KV_RANK
    kr = h[..., o:o + QK_ROPE]
    o += QK_ROPE
    return u, cq, ckv, kr, h[..., o:]


def s5_discretize(a_re, a_im, log_dt, b_re, b_im):
    f32 = jnp.float32
    dt = jnp.exp(log_dt.astype(f32))[:, None]
    lr, li = a_re.astype(f32), a_im.astype(f32)
    mag = jnp.exp(lr * dt)
    ab_re, ab_im = mag * jnp.cos(li * dt), mag * jnp.sin(li * dt)
    den = lr * lr + li * li
    nr, ni = ab_re - 1.0, ab_im
    co_re = (nr * lr + ni * li) / den
    co_im = (ni * lr - nr * li) / den
    br, bi = b_re.astype(f32), b_im.astype(f32)
    bb_re = co_re[..., None] * br - co_im[..., None] * bi
    bb_im = co_re[..., None] * bi + co_im[..., None] * br
    return ab_re, ab_im, bb_re, bb_im


def _ssm_combine(e1, e2):
    a1r, a1i, b1r, b1i = e1
    a2r, a2i, b2r, b2i = e2
    return (a2r * a1r - a2i * a1i, a2r * a1i + a2i * a1r,
            a2r * b1r - a2i * b1i + b2r, a2r * b1i + a2i * b1r + b2i)


def s5_scan(u, disc, h0, reverse):
    ab_re, ab_im, bb_re, bb_im = disc
    bu_re = jnp.einsum('blgp,gnp->blgn', u, bb_re)
    bu_im = jnp.einsum('blgp,gnp->blgn', u, bb_im)
    if h0 is not None:
        idx = -1 if reverse else 0
        h_re, h_im = h0
        bu_re = bu_re.at[:, idx].add(ab_re * h_re - ab_im * h_im)
        bu_im = bu_im.at[:, idx].add(ab_re * h_im + ab_im * h_re)
    a_re = jnp.broadcast_to(ab_re, bu_re.shape)
    a_im = jnp.broadcast_to(ab_im, bu_re.shape)
    _, _, h_re, h_im = lax.associative_scan(_ssm_combine, (a_re, a_im, bu_re, bu_im),
                                            reverse=reverse, axis=1)
    return h_re, h_im


def s5_readout(h, c_re, c_im):
    h_re, h_im = h
    return (jnp.einsum('blgn,gpn->blgp', h_re, c_re)
            - jnp.einsum('blgn,gpn->blgp', h_im, c_im))


def s5_mixer(u_ctx, u_lat, p, need_ctx_out):
    f32 = jnp.float32
    B, L = u_lat.shape[:2]
    Lc = u_ctx.shape[1]
    uc = u_ctx.astype(f32).reshape(B, Lc, S5_GROUPS, S5_GROUP)
    ul = u_lat.astype(f32).reshape(B, L, S5_GROUPS, S5_GROUP)
    d_skip = p['s5_d'].astype(f32)
    y_lat = d_skip * ul
    y_ctx = d_skip * uc if need_ctx_out else None
    for d, rev in enumerate((False, True)):
        disc = s5_discretize(p['s5_a_re'][d], p['s5_a_im'][d], p['s5_log_dt'][d],
                             p['s5_b_re'][d], p['s5_b_im'][d])
        c_re, c_im = p['s5_c_re'][d].astype(f32), p['s5_c_im'][d].astype(f32)
        hc = s5_scan(uc, disc, None, rev)
        last = 0 if rev else -1
        hl = s5_scan(ul, disc, (hc[0][:, last], hc[1][:, last]), rev)
        y_lat = y_lat + s5_readout(hl, c_re, c_im)
        if need_ctx_out:
            y_ctx = y_ctx + s5_readout(hc, c_re, c_im)
    y_lat = y_lat.reshape(B, L, S5_WIDTH).astype(u_lat.dtype)
    if need_ctx_out:
        y_ctx = y_ctx.reshape(B, Lc, S5_WIDTH).astype(u_ctx.dtype)
    return y_lat, y_ctx


def mla_qkv(cq, ckv, kr, p, rope):
    B, L = cq.shape[:2]
    q = (rmsnorm(cq, p['q_norm']) @ p['w_uq']).reshape(B, L, MLA_HEADS, QK_NOPE + QK_ROPE)
    kv = (rmsnorm(ckv, p['kv_norm']) @ p['w_ukv']).reshape(B, L, MLA_HEADS, QK_NOPE + V_DIM)
    q_nope, q_rope = q[..., :QK_NOPE], q[..., QK_NOPE:]
    k_nope, v = kv[..., :QK_NOPE], kv[..., QK_NOPE:]
    k_rope = kr[:, :, None, :]
    if rope is not None:
        cos, sin = rope
        q_rope = apply_rope2d(q_rope, cos, sin)
        k_rope = apply_rope2d(k_rope, cos, sin)
    q = jnp.concatenate([q_nope, q_rope], axis=-1)
    k = jnp.concatenate([k_nope, jnp.broadcast_to(k_rope, (B, L, MLA_HEADS, QK_ROPE))], axis=-1)
    return q, k, v


def attend(q, k, v):
    s = jnp.einsum('bqhd,bkhd->bhqk', q, k, preferred_element_type=jnp.float32) * ATTN_SCALE
    pr = jax.nn.softmax(s, axis=-1).astype(v.dtype)
    return jnp.einsum('bhqk,bkhd->bqhd', pr, v)


def blocked_attend(q, k, v):
    B, L, H, dk = q.shape
    nb = L // Q_BLOCK
    qb = q.reshape(B, nb, Q_BLOCK, H, dk).transpose(1, 0, 2, 3, 4)
    ob = lax.map(lambda qi: attend(qi, k, v), qb)
    return ob.transpose(1, 0, 2, 3, 4).reshape(B, L, H, v.shape[-1])


def merge_branches(y5, o_mla, gate_cols, p):
    z = jax.nn.gelu(y5)
    a, b = jnp.split(z @ p['w_glu'], 2, axis=-1)
    br_s5 = a * jax.nn.sigmoid(b)
    br_mla = o_mla.reshape(o_mla.shape[:2] + (MLA_HEADS * V_DIM,)) @ p['w_mla_o']
    g_s5, g_mla = jnp.split(jax.nn.sigmoid(gate_cols), 2, axis=-1)
    return (g_s5 * br_s5 + g_mla * br_mla) @ p['w_out']


def swiglu(h, p):
    a, b = jnp.split(h @ p['w_ffn_in'], 2, axis=-1)
    return (jax.nn.silu(a) * b) @ p['w_ffn_out']


def layer(x, xc, m_lat, m_ctx, cos, sin, p, need_ctx_out):
    sh1, sc1, g1, sh2, sc2, g2 = (m_lat[..., i, :] for i in range(6))
    csh1, csc1, cg1, csh2, csc2, cg2 = (m_ctx[..., i, :] for i in range(6))
    hl = (rmsnorm(x, p['norm1']) * (1.0 + sc1) + sh1) @ p['w_in']
    hc = (rmsnorm(xc, p['norm1']) * (1.0 + csc1) + csh1) @ p['w_in']
    ul, cql, ckvl, krl, gl = split_in(hl)
    uc, cqc, ckvc, krc, gc = split_in(hc)
    y5_lat, y5_ctx = s5_mixer(uc, ul, p, need_ctx_out)
    qc, kc, vc = mla_qkv(cqc, ckvc, krc, p, None)
    ql, kl, vl = mla_qkv(cql, ckvl, krl, p, (cos, sin))
    k_all = jnp.concatenate([kl, kc], axis=1)
    v_all = jnp.concatenate([vl, vc], axis=1)
    ol = blocked_attend(ql, k_all, v_all)
    x = x + g1 * merge_branches(y5_lat, ol, gl, p)
    x = x + g2 * swiglu(rmsnorm(x, p['norm2']) * (1.0 + sc2) + sh2, p)
    if need_ctx_out:
        oc = attend(qc, kc, vc)
        xc = xc + cg1 * merge_branches(y5_ctx, oc, gc, p)
        xc = xc + cg2 * swiglu(rmsnorm(xc, p['norm2']) * (1.0 + csc2) + csh2, p)
    return x, xc


def _fwd_setup_inputs(seed: int = 0) -> dict:
    key = jax.random.key(seed)
    ks = jax.random.split(key, 32)
    f32 = jnp.float32

    def nrm(k, shape, scale):
        return jax.random.normal(k, shape, f32) * scale

    G, N, P = S5_GROUPS, S5_STATE, S5_GROUP
    n_idx = jnp.arange(N, dtype=f32)
    return {
        'x': nrm(ks[0], (BATCH, SEQ, D_MODEL), 1.0),
        'c': nrm(ks[1], (BATCH, D_MODEL), 1.0),
        'ctx': nrm(ks[2], (BATCH, CTX_LEN, D_MODEL), 1.0),
        'c_ctx': nrm(ks[3], (D_MODEL,), 1.0),
        'w_mod': nrm(ks[4], (DEPTH, D_MODEL, 6 * D_MODEL), 0.3 * D_MODEL ** -0.5),
        'b_mod': nrm(ks[5], (DEPTH, 6 * D_MODEL), 0.02),
        'norm1': 1.0 + nrm(ks[6], (DEPTH, D_MODEL), 0.01),
        'norm2': 1.0 + nrm(ks[7], (DEPTH, D_MODEL), 0.01),
        'w_in': nrm(ks[8], (DEPTH, D_MODEL, IN_COLS), D_MODEL ** -0.5),
        's5_a_re': -0.5 + nrm(ks[9], (DEPTH, 2, G, N), 0.01),
        's5_a_im': math.pi * n_idx + nrm(ks[10], (DEPTH, 2, G, N), 0.01),
        's5_log_dt': jax.random.uniform(ks[11], (DEPTH, 2, G), f32, math.log(1e-3), math.log(1e-1)),
        's5_b_re': nrm(ks[12], (DEPTH, 2, G, N, P), (2 * P) ** -0.5),
        's5_b_im': nrm(ks[13], (DEPTH, 2, G, N, P), (2 * P) ** -0.5),
        's5_c_re': nrm(ks[14], (DEPTH, 2, G, P, N), N ** -0.5),
        's5_c_im': nrm(ks[15], (DEPTH, 2, G, P, N), N ** -0.5),
        's5_d': nrm(ks[16], (DEPTH, G, P), 0.5),
        'w_glu': nrm(ks[17], (DEPTH, S5_WIDTH, 2 * D_MODEL), S5_WIDTH ** -0.5),
        'q_norm': 1.0 + nrm(ks[18], (DEPTH, Q_RANK), 0.01),
        'kv_norm': 1.0 + nrm(ks[19], (DEPTH, KV_RANK), 0.01),
        'w_uq': nrm(ks[20], (DEPTH, Q_RANK, MLA_HEADS * (QK_NOPE + QK_ROPE)), Q_RANK ** -0.5),
        'w_ukv': nrm(ks[21], (DEPTH, KV_RANK, MLA_HEADS * (QK_NOPE + V_DIM)), KV_RANK ** -0.5),
        'w_mla_o': nrm(ks[22], (DEPTH, MLA_HEADS * V_DIM, D_MODEL), (MLA_HEADS * V_DIM) ** -0.5),
        'w_out': nrm(ks[23], (DEPTH, D_MODEL, D_MODEL), D_MODEL ** -0.5),
        'w_ffn_in': nrm(ks[24], (DEPTH, D_MODEL, 2 * D_FF), D_MODEL ** -0.5),
        'w_ffn_out': nrm(ks[25], (DEPTH, D_FF, D_MODEL), D_FF ** -0.5),
        'norm_f': 1.0 + nrm(ks[26], (D_MODEL,), 0.01),
    }


def _fwd_reference(x, c, ctx, c_ctx, w_mod, b_mod, norm1, norm2, w_in, s5_a_re, s5_a_im, s5_log_dt,
              s5_b_re, s5_b_im, s5_c_re, s5_c_im, s5_d, w_glu, q_norm, kv_norm, w_uq, w_ukv,
              w_mla_o, w_out, w_ffn_in, w_ffn_out, norm_f):
    cos, sin = rope2d_tables(x.shape[1])
    xc = ctx
    for l in range(DEPTH):
        p = {
            'norm1': norm1[l], 'norm2': norm2[l], 'w_in': w_in[l],
            's5_a_re': s5_a_re[l], 's5_a_im': s5_a_im[l], 's5_log_dt': s5_log_dt[l],
            's5_b_re': s5_b_re[l], 's5_b_im': s5_b_im[l], 's5_c_re': s5_c_re[l], 's5_c_im': s5_c_im[l],
            's5_d': s5_d[l], 'w_glu': w_glu[l], 'q_norm': q_norm[l], 'kv_norm': kv_norm[l],
            'w_uq': w_uq[l], 'w_ukv': w_ukv[l], 'w_mla_o': w_mla_o[l], 'w_out': w_out[l],
            'w_ffn_in': w_ffn_in[l], 'w_ffn_out': w_ffn_out[l],
        }
        m_lat = ada(c, w_mod[l], b_mod[l])[:, None]
        m_ctx = ada(c_ctx, w_mod[l], b_mod[l])
        x, xc = layer(x, xc, m_lat, m_ctx, cos, sin, p, l < DEPTH - 1)
    return rmsnorm(x, norm_f)


import jax as _jax
import jax.numpy as _jnp

TWIN_FORMAT = 'train_step'
FWD_PARAMS = ['x', 'c', 'ctx', 'c_ctx', 'w_mod', 'b_mod', 'norm1', 'norm2', 'w_in', 's5_a_re', 's5_a_im', 's5_log_dt', 's5_b_re', 's5_b_im', 's5_c_re', 's5_c_im', 's5_d', 'w_glu', 'q_norm', 'kv_norm', 'w_uq', 'w_ukv', 'w_mla_o', 'w_out', 'w_ffn_in', 'w_ffn_out', 'norm_f']
TWIN_WEIGHTS = ['c_ctx', 'w_mod', 'b_mod', 'norm1', 'norm2', 'w_in', 's5_a_re', 's5_a_im', 's5_log_dt', 's5_b_re', 's5_b_im', 's5_c_re', 's5_c_im', 's5_d', 'w_glu', 'q_norm', 'kv_norm', 'w_uq', 'w_ukv', 'w_mla_o', 'w_out', 'w_ffn_in', 'w_ffn_out', 'norm_f']
TWIN_DIFF_INPUT = 'x'
TWIN_INPUTS = ['x', 'c', 'ctx', 'c_ctx', 'w_mod', 'b_mod', 'norm1', 'norm2', 'w_in', 's5_a_re', 's5_a_im', 's5_log_dt', 's5_b_re', 's5_b_im', 's5_c_re', 's5_c_im', 's5_d', 'w_glu', 'q_norm', 'kv_norm', 'w_uq', 'w_ukv', 'w_mla_o', 'w_out', 'w_ffn_in', 'w_ffn_out', 'norm_f', 'loss_target', 'm_c_ctx', 'm_w_mod', 'm_b_mod', 'm_norm1', 'm_norm2', 'm_w_in', 'm_s5_a_re', 'm_s5_a_im', 'm_s5_log_dt', 'm_s5_b_re', 'm_s5_b_im', 'm_s5_c_re', 'm_s5_c_im', 'm_s5_d', 'm_w_glu', 'm_q_norm', 'm_kv_norm', 'm_w_uq', 'm_w_ukv', 'm_w_mla_o', 'm_w_out', 'm_w_ffn_in', 'm_w_ffn_out', 'm_norm_f', 'v_c_ctx', 'v_w_mod', 'v_b_mod', 'v_norm1', 'v_norm2', 'v_w_in', 'v_s5_a_re', 'v_s5_a_im', 'v_s5_log_dt', 'v_s5_b_re', 'v_s5_b_im', 'v_s5_c_re', 'v_s5_c_im', 'v_s5_d', 'v_w_glu', 'v_q_norm', 'v_kv_norm', 'v_w_uq', 'v_w_ukv', 'v_w_mla_o', 'v_w_out', 'v_w_ffn_in', 'v_w_ffn_out', 'v_norm_f']
TWIN_OUTPUTS = ['loss', 'grad_x', 'grad_c_ctx', 'grad_w_mod', 'grad_b_mod', 'grad_norm1', 'grad_norm2', 'grad_w_in', 'grad_s5_a_re', 'grad_s5_a_im', 'grad_s5_log_dt', 'grad_s5_b_re', 'grad_s5_b_im', 'grad_s5_c_re', 'grad_s5_c_im', 'grad_s5_d', 'grad_w_glu', 'grad_q_norm', 'grad_kv_norm', 'grad_w_uq', 'grad_w_ukv', 'grad_w_mla_o', 'grad_w_out', 'grad_w_ffn_in', 'grad_w_ffn_out', 'grad_norm_f', 'delta_c_ctx', 'delta_w_mod', 'delta_b_mod', 'delta_norm1', 'delta_norm2', 'delta_w_in', 'delta_s5_a_re', 'delta_s5_a_im', 'delta_s5_log_dt', 'delta_s5_b_re', 'delta_s5_b_im', 'delta_s5_c_re', 'delta_s5_c_im', 'delta_s5_d', 'delta_w_glu', 'delta_q_norm', 'delta_kv_norm', 'delta_w_uq', 'delta_w_ukv', 'delta_w_mla_o', 'delta_w_out', 'delta_w_ffn_in', 'delta_w_ffn_out', 'delta_norm_f', 'new_m_c_ctx', 'new_m_w_mod', 'new_m_b_mod', 'new_m_norm1', 'new_m_norm2', 'new_m_w_in', 'new_m_s5_a_re', 'new_m_s5_a_im', 'new_m_s5_log_dt', 'new_m_s5_b_re', 'new_m_s5_b_im', 'new_m_s5_c_re', 'new_m_s5_c_im', 'new_m_s5_d', 'new_m_w_glu', 'new_m_q_norm', 'new_m_kv_norm', 'new_m_w_uq', 'new_m_w_ukv', 'new_m_w_mla_o', 'new_m_w_out', 'new_m_w_ffn_in', 'new_m_w_ffn_out', 'new_m_norm_f', 'new_v_c_ctx', 'new_v_w_mod', 'new_v_b_mod', 'new_v_norm1', 'new_v_norm2', 'new_v_w_in', 'new_v_s5_a_re', 'new_v_s5_a_im', 'new_v_s5_log_dt', 'new_v_s5_b_re', 'new_v_s5_b_im', 'new_v_s5_c_re', 'new_v_s5_c_im', 'new_v_s5_d', 'new_v_w_glu', 'new_v_q_norm', 'new_v_kv_norm', 'new_v_w_uq', 'new_v_w_ukv', 'new_v_w_mla_o', 'new_v_w_out', 'new_v_w_ffn_in', 'new_v_w_ffn_out', 'new_v_norm_f']
TWIN_LEAF_KINDS = {'loss': 'loss', 'grad_x': 'grad_x', 'grad_c_ctx': 'grad_w', 'grad_w_mod': 'grad_w', 'grad_b_mod': 'grad_w', 'grad_norm1': 'grad_w', 'grad_norm2': 'grad_w', 'grad_w_in': 'grad_w', 'grad_s5_a_re': 'grad_w', 'grad_s5_a_im': 'grad_w', 'grad_s5_log_dt': 'grad_w', 'grad_s5_b_re': 'grad_w', 'grad_s5_b_im': 'grad_w', 'grad_s5_c_re': 'grad_w', 'grad_s5_c_im': 'grad_w', 'grad_s5_d': 'grad_w', 'grad_w_glu': 'grad_w', 'grad_q_norm': 'grad_w', 'grad_kv_norm': 'grad_w', 'grad_w_uq': 'grad_w', 'grad_w_ukv': 'grad_w', 'grad_w_mla_o': 'grad_w', 'grad_w_out': 'grad_w', 'grad_w_ffn_in': 'grad_w', 'grad_w_ffn_out': 'grad_w', 'grad_norm_f': 'grad_w', 'delta_c_ctx': 'delta_w', 'delta_w_mod': 'delta_w', 'delta_b_mod': 'delta_w', 'delta_norm1': 'delta_w', 'delta_norm2': 'delta_w', 'delta_w_in': 'delta_w', 'delta_s5_a_re': 'delta_w', 'delta_s5_a_im': 'delta_w', 'delta_s5_log_dt': 'delta_w', 'delta_s5_b_re': 'delta_w', 'delta_s5_b_im': 'delta_w', 'delta_s5_c_re': 'delta_w', 'delta_s5_c_im': 'delta_w', 'delta_s5_d': 'delta_w', 'delta_w_glu': 'delta_w', 'delta_q_norm': 'delta_w', 'delta_kv_norm': 'delta_w', 'delta_w_uq': 'delta_w', 'delta_w_ukv': 'delta_w', 'delta_w_mla_o': 'delta_w', 'delta_w_out': 'delta_w', 'delta_w_ffn_in': 'delta_w', 'delta_w_ffn_out': 'delta_w', 'delta_norm_f': 'delta_w', 'new_m_c_ctx': 'new_m', 'new_m_w_mod': 'new_m', 'new_m_b_mod': 'new_m', 'new_m_norm1': 'new_m', 'new_m_norm2': 'new_m', 'new_m_w_in': 'new_m', 'new_m_s5_a_re': 'new_m', 'new_m_s5_a_im': 'new_m', 'new_m_s5_log_dt': 'new_m', 'new_m_s5_b_re': 'new_m', 'new_m_s5_b_im': 'new_m', 'new_m_s5_c_re': 'new_m', 'new_m_s5_c_im': 'new_m', 'new_m_s5_d': 'new_m', 'new_m_w_glu': 'new_m', 'new_m_q_norm': 'new_m', 'new_m_kv_norm': 'new_m', 'new_m_w_uq': 'new_m', 'new_m_w_ukv': 'new_m', 'new_m_w_mla_o': 'new_m', 'new_m_w_out': 'new_m', 'new_m_w_ffn_in': 'new_m', 'new_m_w_ffn_out': 'new_m', 'new_m_norm_f': 'new_m', 'new_v_c_ctx': 'new_v', 'new_v_w_mod': 'new_v', 'new_v_b_mod': 'new_v', 'new_v_norm1': 'new_v', 'new_v_norm2': 'new_v', 'new_v_w_in': 'new_v', 'new_v_s5_a_re': 'new_v', 'new_v_s5_a_im': 'new_v', 'new_v_s5_log_dt': 'new_v', 'new_v_s5_b_re': 'new_v', 'new_v_s5_b_im': 'new_v', 'new_v_s5_c_re': 'new_v', 'new_v_s5_c_im': 'new_v', 'new_v_s5_d': 'new_v', 'new_v_w_glu': 'new_v', 'new_v_q_norm': 'new_v', 'new_v_kv_norm': 'new_v', 'new_v_w_uq': 'new_v', 'new_v_w_ukv': 'new_v', 'new_v_w_mla_o': 'new_v', 'new_v_w_out': 'new_v', 'new_v_w_ffn_in': 'new_v', 'new_v_w_ffn_out': 'new_v', 'new_v_norm_f': 'new_v'}


def _forward(args):
    return _fwd_reference(*[args[k] for k in FWD_PARAMS])


def _output_shape():
    out = _jax.eval_shape(lambda: _forward(_fwd_setup_inputs(0)))
    return out.shape, out.dtype

N_MICROBATCH = 1
ADAM_LR = 0.001
ADAM_B1 = 0.9
ADAM_B2 = 0.999
ADAM_EPS = 1e-08
ADAM_WD = 0.01
ADAM_STEP = 10
PER_EXAMPLE_BATCH_AXIS = {'x': 0, 'c': 0, 'ctx': 0, 'loss_target': 0}
SHARED_INPUTS = []
_WEIGHT_DTYPES = {'c_ctx': _jnp.float32, 'w_mod': _jnp.float32, 'b_mod': _jnp.float32, 'norm1': _jnp.float32, 'norm2': _jnp.float32, 'w_in': _jnp.float32, 's5_a_re': _jnp.float32, 's5_a_im': _jnp.float32, 's5_log_dt': _jnp.float32, 's5_b_re': _jnp.float32, 's5_b_im': _jnp.float32, 's5_c_re': _jnp.float32, 's5_c_im': _jnp.float32, 's5_d': _jnp.float32, 'w_glu': _jnp.float32, 'q_norm': _jnp.float32, 'kv_norm': _jnp.float32, 'w_uq': _jnp.float32, 'w_ukv': _jnp.float32, 'w_mla_o': _jnp.float32, 'w_out': _jnp.float32, 'w_ffn_in': _jnp.float32, 'w_ffn_out': _jnp.float32, 'norm_f': _jnp.float32}
MOMENT_SCALE = {'c_ctx': 5.735779e-04, 'w_mod': 1.062243e-02, 'b_mod': 1.749736e-02, 'norm1': 1.476617e-03, 'norm2': 1.040410e-02, 'w_in': 9.855108e-04, 's5_a_re': 1.912659e-04, 's5_a_im': 2.244191e-04, 's5_log_dt': 1.140971e-01, 's5_b_re': 1.347589e-04, 's5_b_im': 1.360536e-04, 's5_c_re': 1.995935e-04, 's5_c_im': 1.998131e-04, 's5_d': 2.931246e-03, 'w_glu': 7.369404e-04, 'q_norm': 1.115861e-03, 'kv_norm': 3.395469e-03, 'w_uq': 6.446537e-04, 'w_ukv': 1.097133e-03, 'w_mla_o': 1.074803e-03, 'w_out': 1.493024e-03, 'w_ffn_in': 4.428994e-03, 'w_ffn_out': 7.230367e-03, 'norm_f': 7.996321e+00}


def _to_microbatches(a, axis):
    t = _jnp.moveaxis(a, axis, 0)
    t = t.reshape((N_MICROBATCH, t.shape[0] // N_MICROBATCH) + t.shape[1:])
    return _jnp.moveaxis(t, 1, axis + 1)


def setup_inputs(seed: int = 0) -> dict:
    inp = _fwd_setup_inputs(seed)
    key = _jax.random.fold_in(_jax.random.key(seed), 7919)
    shape, _ = _output_shape()
    out = dict(inp)
    out["loss_target"] = _jax.random.normal(_jax.random.fold_in(key, 0), shape, _jnp.float32)
    for i, name in enumerate(TWIN_WEIGHTS):
        w = inp[name].astype(_jnp.float32)
        if MOMENT_SCALE is None:
            s = _jnp.sqrt(_jnp.mean(_jnp.square(w)) + 1e-30)
        else:
            s = MOMENT_SCALE[name]
        km, kv = _jax.random.split(_jax.random.fold_in(key, i + 1))
        out[name] = w
        out["m_" + name] = s * _jax.random.normal(km, w.shape, _jnp.float32)
        out["v_" + name] = (s * s) * _jax.random.uniform(kv, w.shape, _jnp.float32, 0.5, 1.5)
    if N_MICROBATCH > 1:
        for name, axis in PER_EXAMPLE_BATCH_AXIS.items():
            out[name] = _to_microbatches(out[name], axis)
    return {'x': out['x'], 'c': out['c'], 'ctx': out['ctx'], 'c_ctx': out['c_ctx'], 'w_mod': out['w_mod'], 'b_mod': out['b_mod'], 'norm1': out['norm1'], 'norm2': out['norm2'], 'w_in': out['w_in'], 's5_a_re': out['s5_a_re'], 's5_a_im': out['s5_a_im'], 's5_log_dt': out['s5_log_dt'], 's5_b_re': out['s5_b_re'], 's5_b_im': out['s5_b_im'], 's5_c_re': out['s5_c_re'], 's5_c_im': out['s5_c_im'], 's5_d': out['s5_d'], 'w_glu': out['w_glu'], 'q_norm': out['q_norm'], 'kv_norm': out['kv_norm'], 'w_uq': out['w_uq'], 'w_ukv': out['w_ukv'], 'w_mla_o': out['w_mla_o'], 'w_out': out['w_out'], 'w_ffn_in': out['w_ffn_in'], 'w_ffn_out': out['w_ffn_out'], 'norm_f': out['norm_f'], 'loss_target': out['loss_target'], 'm_c_ctx': out['m_c_ctx'], 'm_w_mod': out['m_w_mod'], 'm_b_mod': out['m_b_mod'], 'm_norm1': out['m_norm1'], 'm_norm2': out['m_norm2'], 'm_w_in': out['m_w_in'], 'm_s5_a_re': out['m_s5_a_re'], 'm_s5_a_im': out['m_s5_a_im'], 'm_s5_log_dt': out['m_s5_log_dt'], 'm_s5_b_re': out['m_s5_b_re'], 'm_s5_b_im': out['m_s5_b_im'], 'm_s5_c_re': out['m_s5_c_re'], 'm_s5_c_im': out['m_s5_c_im'], 'm_s5_d': out['m_s5_d'], 'm_w_glu': out['m_w_glu'], 'm_q_norm': out['m_q_norm'], 'm_kv_norm': out['m_kv_norm'], 'm_w_uq': out['m_w_uq'], 'm_w_ukv': out['m_w_ukv'], 'm_w_mla_o': out['m_w_mla_o'], 'm_w_out': out['m_w_out'], 'm_w_ffn_in': out['m_w_ffn_in'], 'm_w_ffn_out': out['m_w_ffn_out'], 'm_norm_f': out['m_norm_f'], 'v_c_ctx': out['v_c_ctx'], 'v_w_mod': out['v_w_mod'], 'v_b_mod': out['v_b_mod'], 'v_norm1': out['v_norm1'], 'v_norm2': out['v_norm2'], 'v_w_in': out['v_w_in'], 'v_s5_a_re': out['v_s5_a_re'], 'v_s5_a_im': out['v_s5_a_im'], 'v_s5_log_dt': out['v_s5_log_dt'], 'v_s5_b_re': out['v_s5_b_re'], 'v_s5_b_im': out['v_s5_b_im'], 'v_s5_c_re': out['v_s5_c_re'], 'v_s5_c_im': out['v_s5_c_im'], 'v_s5_d': out['v_s5_d'], 'v_w_glu': out['v_w_glu'], 'v_q_norm': out['v_q_norm'], 'v_kv_norm': out['v_kv_norm'], 'v_w_uq': out['v_w_uq'], 'v_w_ukv': out['v_w_ukv'], 'v_w_mla_o': out['v_w_mla_o'], 'v_w_out': out['v_w_out'], 'v_w_ffn_in': out['v_w_ffn_in'], 'v_w_ffn_out': out['v_w_ffn_out'], 'v_norm_f': out['v_norm_f']}


def _loss(weights, diff, rest, loss_target):
    with _jax.named_scope("forward"):
        args = {**rest, TWIN_DIFF_INPUT: diff, **{k: w.astype(_WEIGHT_DTYPES[k]) for k, w in weights.items()}}
        y = _forward(args)
    with _jax.named_scope("loss_head"):
        err = _jnp.square(y.astype(_jnp.float32) - loss_target)
        return 0.5 * _jnp.sum(_jnp.mean(err, axis=-1)) if err.ndim else 0.5 * err


def _adamw(w, g, m, v):
    m = ADAM_B1 * m + (1.0 - ADAM_B1) * g
    v = ADAM_B2 * v + (1.0 - ADAM_B2) * _jnp.square(g)
    m_hat = m / (1.0 - ADAM_B1 ** ADAM_STEP)
    v_hat = v / (1.0 - ADAM_B2 ** ADAM_STEP)
    delta = -ADAM_LR * (m_hat / (_jnp.sqrt(v_hat) + ADAM_EPS) + ADAM_WD * w)
    return delta, m, v


def reference(x, c, ctx, c_ctx, w_mod, b_mod, norm1, norm2, w_in, s5_a_re, s5_a_im, s5_log_dt, s5_b_re, s5_b_im, s5_c_re, s5_c_im, s5_d, w_glu, q_norm, kv_norm, w_uq, w_ukv, w_mla_o, w_out, w_ffn_in, w_ffn_out, norm_f, loss_target, m_c_ctx, m_w_mod, m_b_mod, m_norm1, m_norm2, m_w_in, m_s5_a_re, m_s5_a_im, m_s5_log_dt, m_s5_b_re, m_s5_b_im, m_s5_c_re, m_s5_c_im, m_s5_d, m_w_glu, m_q_norm, m_kv_norm, m_w_uq, m_w_ukv, m_w_mla_o, m_w_out, m_w_ffn_in, m_w_ffn_out, m_norm_f, v_c_ctx, v_w_mod, v_b_mod, v_norm1, v_norm2, v_w_in, v_s5_a_re, v_s5_a_im, v_s5_log_dt, v_s5_b_re, v_s5_b_im, v_s5_c_re, v_s5_c_im, v_s5_d, v_w_glu, v_q_norm, v_kv_norm, v_w_uq, v_w_ukv, v_w_mla_o, v_w_out, v_w_ffn_in, v_w_ffn_out, v_norm_f):
    given = dict(x=x, c=c, ctx=ctx, c_ctx=c_ctx, w_mod=w_mod, b_mod=b_mod, norm1=norm1, norm2=norm2, w_in=w_in, s5_a_re=s5_a_re, s5_a_im=s5_a_im, s5_log_dt=s5_log_dt, s5_b_re=s5_b_re, s5_b_im=s5_b_im, s5_c_re=s5_c_re, s5_c_im=s5_c_im, s5_d=s5_d, w_glu=w_glu, q_norm=q_norm, kv_norm=kv_norm, w_uq=w_uq, w_ukv=w_ukv, w_mla_o=w_mla_o, w_out=w_out, w_ffn_in=w_ffn_in, w_ffn_out=w_ffn_out, norm_f=norm_f, loss_target=loss_target, m_c_ctx=m_c_ctx, m_w_mod=m_w_mod, m_b_mod=m_b_mod, m_norm1=m_norm1, m_norm2=m_norm2, m_w_in=m_w_in, m_s5_a_re=m_s5_a_re, m_s5_a_im=m_s5_a_im, m_s5_log_dt=m_s5_log_dt, m_s5_b_re=m_s5_b_re, m_s5_b_im=m_s5_b_im, m_s5_c_re=m_s5_c_re, m_s5_c_im=m_s5_c_im, m_s5_d=m_s5_d, m_w_glu=m_w_glu, m_q_norm=m_q_norm, m_kv_norm=m_kv_norm, m_w_uq=m_w_uq, m_w_ukv=m_w_ukv, m_w_mla_o=m_w_mla_o, m_w_out=m_w_out, m_w_ffn_in=m_w_ffn_in, m_w_ffn_out=m_w_ffn_out, m_norm_f=m_norm_f, v_c_ctx=v_c_ctx, v_w_mod=v_w_mod, v_b_mod=v_b_mod, v_norm1=v_norm1, v_norm2=v_norm2, v_w_in=v_w_in, v_s5_a_re=v_s5_a_re, v_s5_a_im=v_s5_a_im, v_s5_log_dt=v_s5_log_dt, v_s5_b_re=v_s5_b_re, v_s5_b_im=v_s5_b_im, v_s5_c_re=v_s5_c_re, v_s5_c_im=v_s5_c_im, v_s5_d=v_s5_d, v_w_glu=v_w_glu, v_q_norm=v_q_norm, v_kv_norm=v_kv_norm, v_w_uq=v_w_uq, v_w_ukv=v_w_ukv, v_w_mla_o=v_w_mla_o, v_w_out=v_w_out, v_w_ffn_in=v_w_ffn_in, v_w_ffn_out=v_w_ffn_out, v_norm_f=v_norm_f)
    weights = {n: given[n] for n in TWIN_WEIGHTS}
    shared = {n: given[n] for n in SHARED_INPUTS}
    per_example = {n: given[n] for n in ['x', 'c', 'ctx']}
    grad_fn = _jax.value_and_grad(_loss, argnums=(0, 1))

    def one_microbatch(ex, loss_target):
        ex = dict(ex)
        diff = ex.pop(TWIN_DIFF_INPUT)
        return grad_fn(weights, diff, {**shared, **ex}, loss_target)

    if N_MICROBATCH == 1:
        loss, (grad_w, grad_x) = one_microbatch(per_example, given["loss_target"])
    else:
        def body(carry, xs):
            loss_sum, grad_sum = carry
            l_k, (gw_k, gx_k) = one_microbatch(xs[0], xs[1])
            with _jax.named_scope("update"):
                return (loss_sum + l_k, _jax.tree.map(_jnp.add, grad_sum, gw_k)), gx_k

        init = (_jnp.zeros((), _jnp.float32), _jax.tree.map(_jnp.zeros_like, weights))
        (loss, grad_w), grad_x = _jax.lax.scan(body, init, (per_example, given["loss_target"]))
    with _jax.named_scope("update"):
        delta_w, new_m, new_v = {}, {}, {}
        for n in TWIN_WEIGHTS:
            delta_w[n], new_m[n], new_v[n] = _adamw(weights[n], grad_w[n], given["m_" + n], given["v_" + n])
    return (loss, grad_x, *[grad_w[n] for n in TWIN_WEIGHTS], *[delta_w[n] for n in TWIN_WEIGHTS],
            *[new_m[n] for n in TWIN_WEIGHTS], *[new_v[n] for n in TWIN_WEIGHTS])
```

```python
import functools
import math

import jax
import jax.numpy as jnp
from jax import lax
from jax.experimental import pallas as pl
from jax.experimental.pallas import tpu as pltpu

F32 = jnp.float32
BF16 = jnp.bfloat16

N_DEV = 8
EPS = 1e-6
GRID_W = 64
S5_GROUP = 16
QK_NOPE, QK_ROPE, V_DIM = 128, 64, 128
ROPE_BASE = 10000.0
ADAM_LR, ADAM_B1, ADAM_B2, ADAM_EPS, ADAM_WD, ADAM_STEP = 0.001, 0.9, 0.999, 1e-08, 0.01, 10

LANE = 128
SUB = 8
PACK_W = 1024
PACK_ROWS = 32
VMEM_LIMIT = 48 << 20
ROWMAP_TILE_BYTES = 10 << 20
MESH = pl.DeviceIdType.MESH


def _pick(dim, cands):
    for c in cands:
        if dim % c == 0:
            return c
    return dim


def _cparams(sem):
    return pltpu.CompilerParams(dimension_semantics=sem, vmem_limit_bytes=VMEM_LIMIT)


def _mm(a, b, dims, out_dtype, name):
    a = a.astype(BF16)
    b = b.astype(BF16)
    if dims == "nn":
        (M, K), N = a.shape, b.shape[1]
    elif dims == "nt":
        (M, K), N = a.shape, b.shape[0]
    else:
        (K, M), N = a.shape, b.shape[1]
    tm = _pick(M, (1024, 768, 512, 256, 128, 64, 32, 16))
    tn = _pick(N, (512, 256, 128))
    tk = _pick(K, (2048, 1536, 1024, 768, 512, 256, 128, 64, 32, 16))
    nk = K // tk
    if dims == "nn":
        a_spec = pl.BlockSpec((tm, tk), lambda i, j, k: (i, k))
        b_spec = pl.BlockSpec((tk, tn), lambda i, j, k: (k, j))
        dn = (((1,), (0,)), ((), ()))
    elif dims == "nt":
        a_spec = pl.BlockSpec((tm, tk), lambda i, j, k: (i, k))
        b_spec = pl.BlockSpec((tn, tk), lambda i, j, k: (j, k))
        dn = (((1,), (1,)), ((), ()))
    else:
        a_spec = pl.BlockSpec((tk, tm), lambda i, j, k: (k, i))
        b_spec = pl.BlockSpec((tk, tn), lambda i, j, k: (k, j))
        dn = (((0,), (0,)), ((), ()))

    def body(a_ref, b_ref, o_ref, acc_ref):
        k = pl.program_id(2)

        @pl.when(k == 0)
        def _():
            acc_ref[...] = jnp.zeros_like(acc_ref)

        acc_ref[...] += lax.dot_general(a_ref[...], b_ref[...], dn, preferred_element_type=F32)

        @pl.when(k == nk - 1)
        def _():
            o_ref[...] = acc_ref[...].astype(o_ref.dtype)

    return pl.pallas_call(
        body, name=name, grid=(M // tm, N // tn, nk),
        in_specs=[a_spec, b_spec], out_specs=pl.BlockSpec((tm, tn), lambda i, j, k: (i, j)),
        out_shape=jax.ShapeDtypeStruct((M, N), out_dtype),
        scratch_shapes=[pltpu.VMEM((tm, tn), F32)],
        compiler_params=_cparams(("parallel", "parallel", "arbitrary")),
    )(a, b)


def _rowmap(fn, name, M, row_ins, bc_ins, row_outs, acc_outs=()):
    row_ins = [r if isinstance(r, tuple) else (r, r.shape[1], 0) for r in row_ins]
    row_bytes = sum(w * a.dtype.itemsize for a, w, _ in row_ins) + sum(w * jnp.dtype(d).itemsize for w, d in row_outs)
    widest = max([w for _, w, _ in row_ins] + [w for w, _ in row_outs])
    row_bytes = 2 * row_bytes + 6 * 4 * widest
    tm = _pick(M, [t for t in (512, 256, 128, 64, 32, 16) if t * row_bytes <= ROWMAP_TILE_BYTES] + [16])
    n_in, n_row, n_acc = len(row_ins) + len(bc_ins), len(row_outs), len(acc_outs)

    def body(*refs):
        res = fn(*[r[...] for r in refs[:n_in]])
        res = res if isinstance(res, (tuple, list)) else (res,)
        outs = refs[n_in:]
        for k in range(n_row):
            outs[k][...] = res[k].astype(outs[k].dtype)
        if n_acc:
            @pl.when(pl.program_id(0) == 0)
            def _():
                for k in range(n_acc):
                    outs[n_row + k][...] = jnp.zeros_like(outs[n_row + k])

            for k in range(n_acc):
                outs[n_row + k][...] += res[n_row + k].astype(F32)

    in_specs = [pl.BlockSpec((tm, w), functools.partial(lambda i, blk: (i, blk), blk=blk)) for _, w, blk in row_ins]
    in_specs += [pl.BlockSpec(b.shape, lambda i: (0, 0)) for b in bc_ins]
    out_specs = [pl.BlockSpec((tm, w), lambda i: (i, 0)) for w, _ in row_outs]
    out_specs += [pl.BlockSpec((1, w), lambda i: (0, 0)) for w in acc_outs]
    out_shape = [jax.ShapeDtypeStruct((M, w), d) for w, d in row_outs]
    out_shape += [jax.ShapeDtypeStruct((1, w), F32) for w in acc_outs]
    return pl.pallas_call(
        body, name=name, grid=(M // tm,), in_specs=in_specs, out_specs=out_specs, out_shape=out_shape,
        compiler_params=_cparams(("arbitrary",) if n_acc else ("parallel",)),
    )(*[a for a, _, _ in row_ins], *bc_ins)


def _rms(x, g):
    return x * lax.rsqrt(jnp.mean(x * x, axis=-1, keepdims=True) + EPS) * g


def _normmod(x, g, sc, sh):
    return _rms(x, g) * (1.0 + sc) + sh


def _swap16(v):
    w = v.shape[1]
    lane = lax.broadcasted_iota(jnp.int32, v.shape, 1)
    return jnp.where((lane // 16) % 2 == 0, pltpu.roll(v, w - 16, 1), pltpu.roll(v, 16, 1))


def _rope(v, cos, sin_signed):
    return v * cos + _swap16(v) * sin_signed


def _rope_bwd(d, cos, sin_signed):
    return d * cos + _swap16(d * sin_signed)


def _mesh_pos():
    return lax.axis_index("x"), lax.axis_index("y"), lax.axis_index("c")


def _all_gather(xs, name):
    R, C = xs.shape

    def body(x_ref, out_ref, send_sems, recv_sems, local_sem):
        x, y, c = _mesh_pos()
        me, sibling = (x, y, c), (x, y, 1 - c)
        chips = [(1 - x, y), (x, 1 - y), (1 - x, 1 - y)]

        def slot(px, py, pc):
            return out_ref.at[4 * px + 2 * py + pc]

        def copy(k, block, to, src=None):
            return pltpu.make_async_remote_copy(
                src_ref=slot(*block) if src is None else src, dst_ref=slot(*block),
                send_sem=send_sems.at[k], recv_sem=recv_sems.at[k], device_id=to, device_id_type=MESH)

        mine = pltpu.make_async_copy(x_ref, slot(*me), local_sem)
        mine.start()
        first = [copy(0, me, sibling, src=x_ref)]
        first += [copy(1 + j, me, (*chip, c), src=x_ref) for j, chip in enumerate(chips)]
        for cp in first:
            cp.start()
        passed = [copy(4 + j, (*chip, c), sibling) for j, chip in enumerate(chips)]
        for j, chip in enumerate(chips):
            copy(1 + j, (*chip, c), me).wait_recv()
            passed[j].start()
        copy(0, sibling, me).wait_recv()
        for j, chip in enumerate(chips):
            copy(4 + j, (*chip, 1 - c), me).wait_recv()
        for cp in first + passed:
            cp.wait_send()
        mine.wait()

    return pl.pallas_call(
        body, name=name, out_shape=jax.ShapeDtypeStruct((N_DEV, R, C), xs.dtype),
        in_specs=[pl.BlockSpec(memory_space=pl.ANY)], out_specs=pl.BlockSpec(memory_space=pl.ANY),
        scratch_shapes=[pltpu.SemaphoreType.DMA((7,)), pltpu.SemaphoreType.DMA((7,)), pltpu.SemaphoreType.DMA(())],
    )(xs)


def _exchange(p, name):
    _, R, C = p.shape

    def body(p_ref, out_ref, send_sems, recv_sems, local_sem):
        x, y, c = _mesh_pos()
        me_idx = 4 * x + 2 * y + c
        mine = pltpu.make_async_copy(p_ref.at[me_idx], out_ref.at[me_idx], local_sem)
        mine.start()
        sends, recvs = [], []
        for r in range(1, N_DEV):
            px = 1 - x if r & 4 else x
            py = 1 - y if r & 2 else y
            pc = 1 - c if r & 1 else c
            peer_idx = 4 * px + 2 * py + pc
            sends.append(pltpu.make_async_remote_copy(
                src_ref=p_ref.at[peer_idx], dst_ref=out_ref.at[me_idx],
                send_sem=send_sems.at[r - 1], recv_sem=recv_sems.at[r - 1], device_id=(px, py, pc), device_id_type=MESH))
            recvs.append(pltpu.make_async_remote_copy(
                src_ref=p_ref.at[peer_idx], dst_ref=out_ref.at[peer_idx],
                send_sem=send_sems.at[r - 1], recv_sem=recv_sems.at[r - 1], device_id=(px, py, pc), device_id_type=MESH))
        for cp in sends:
            cp.start()
        for cp in recvs:
            cp.wait_recv()
        for cp in sends:
            cp.wait_send()
        mine.wait()

    return pl.pallas_call(
        body, name=name, out_shape=jax.ShapeDtypeStruct(p.shape, p.dtype),
        in_specs=[pl.BlockSpec(memory_space=pl.ANY)], out_specs=pl.BlockSpec(memory_space=pl.ANY),
        scratch_shapes=[pltpu.SemaphoreType.DMA((7,)), pltpu.SemaphoreType.DMA((7,)), pltpu.SemaphoreType.DMA(())],
    )(p)


def _sum_slots(g8, name):
    _, R, C = g8.shape
    tr = _pick(R, (256, 128, 64, 32, 16))

    def body(g_ref, o_ref):
        acc = g_ref[0].astype(F32)
        for j in range(1, N_DEV):
            acc = acc + g_ref[j].astype(F32)
        o_ref[...] = acc

    return pl.pallas_call(
        body, name=name, grid=(R // tr,),
        in_specs=[pl.BlockSpec((N_DEV, tr, C), lambda i: (0, i, 0))], out_specs=pl.BlockSpec((tr, C), lambda i: (i, 0)),
        out_shape=jax.ShapeDtypeStruct((R, C), F32), compiler_params=_cparams(("parallel",)),
    )(g8)


def _pack_rows(arrs, dtype):
    parts = []
    for a in arrs:
        flat = a.reshape(-1).astype(dtype)
        pad = (-flat.shape[0]) % (PACK_W * 16)
        parts.append(jnp.pad(flat, (0, pad)).reshape(-1, PACK_W))
    out = jnp.concatenate(parts, axis=0)
    return jnp.pad(out, ((0, (-out.shape[0]) % PACK_ROWS), (0, 0)))


def _packed_rows(shape):
    n = math.prod(shape)
    return (n + PACK_W * 16 - 1) // (PACK_W * 16) * 16


def _unpack_rows(packed, shapes):
    out, r0 = [], 0
    lead = packed.shape[:-2]
    for s in shapes:
        rows, n = _packed_rows(s), math.prod(s)
        out.append(packed[..., r0:r0 + rows, :].reshape(lead + (rows * PACK_W,))[..., :n].reshape(lead + tuple(s)))
        r0 += rows
    return out


def _adamw_math(w, g, m, v):
    m = ADAM_B1 * m + (1.0 - ADAM_B1) * g
    v = ADAM_B2 * v + (1.0 - ADAM_B2) * (g * g)
    m_hat = m / (1.0 - ADAM_B1 ** ADAM_STEP)
    v_hat = v / (1.0 - ADAM_B2 ** ADAM_STEP)
    delta = -ADAM_LR * (m_hat / (jnp.sqrt(v_hat) + ADAM_EPS) + ADAM_WD * w)
    return delta, m, v


def _adamw(w, g, m, v, name):
    R, C = w.shape
    return _rowmap(_adamw_math, name, R, [w, g, m, v], [], [(C, F32)] * 3)


def _s5_disc_math(lr, li, ldt, br, bi):
    dt = jnp.exp(ldt)
    mag = jnp.exp(lr * dt)
    ab_re, ab_im = mag * jnp.cos(li * dt), mag * jnp.sin(li * dt)
    den = lr * lr + li * li
    nr, ni = ab_re - 1.0, ab_im
    co_re = (nr * lr + ni * li) / den
    co_im = (ni * lr - nr * li) / den
    bb_re = co_re[None] * br - co_im[None] * bi
    bb_im = co_re[None] * bi + co_im[None] * br
    return ab_re, ab_im, bb_re, bb_im


def _s5_tables(a_re, a_im, ldt, b_re, b_im, c_re, c_im):
    _, P, G, N = b_re.shape

    def body(lr_ref, li_ref, ldt_ref, br_ref, bi_ref, cr_ref, ci_ref,
             wre, wim, vre, vim, pfr, pfi, pbr, pbi, bbr, bbi):
        ar, ai, bb_re, bb_im = _s5_disc_math(lr_ref[0], li_ref[0], ldt_ref[0], br_ref[0], bi_ref[0])
        cr, ci = cr_ref[0], ci_ref[0]
        bbr[0], bbi[0] = bb_re, bb_im
        pr, pi = jnp.ones_like(ar), jnp.zeros_like(ar)
        pows = []
        for j in range(SUB + 1):
            pows.append((pr, pi))
            pr, pi = pr * ar - pi * ai, pr * ai + pi * ar
        for j in range(SUB):
            pr, pi = pows[j]
            wre[0, j] = bb_re * pr[None] - bb_im * pi[None]
            wim[0, j] = bb_re * pi[None] + bb_im * pr[None]
            vre[0, j] = cr * pr[None] - ci * pi[None]
            vim[0, j] = -(cr * pi[None] + ci * pr[None])
            pfr[0, j], pfi[0, j] = pows[j + 1]
            pbr[0, j], pbi[0, j] = pows[SUB - j][0], -pows[SUB - j][1]

    gn = pl.BlockSpec((1, G, N), lambda d: (d, 0, 0))
    pgn = pl.BlockSpec((1, P, G, N), lambda d: (d, 0, 0, 0))
    jpgn = pl.BlockSpec((1, SUB, P, G, N), lambda d: (d, 0, 0, 0, 0))
    jgn = pl.BlockSpec((1, SUB, G, N), lambda d: (d, 0, 0, 0))
    s_jpgn = jax.ShapeDtypeStruct((2, SUB, P, G, N), F32)
    s_jgn = jax.ShapeDtypeStruct((2, SUB, G, N), F32)
    s_pgn = jax.ShapeDtypeStruct((2, P, G, N), F32)
    return pl.pallas_call(
        body, name="s5_tables", grid=(2,),
        in_specs=[gn, gn, pl.BlockSpec((1, G, 1), lambda d: (d, 0, 0)), pgn, pgn, pgn, pgn],
        out_specs=[jpgn] * 4 + [jgn] * 4 + [pgn] * 2,
        out_shape=[s_jpgn] * 4 + [s_jgn] * 4 + [s_pgn] * 2,
        compiler_params=_cparams(("parallel",)),
    )(a_re, a_im, ldt, b_re, b_im, c_re, c_im)


def _s5_param_bwd(a_re, a_im, ldt, b_re, b_im, da_re, da_im, dbb_re, dbb_im):
    _, P, G, N = b_re.shape

    def body(lr_ref, li_ref, ldt_ref, br_ref, bi_ref, dar, dai, dbr, dbi, o_lr, o_li, o_ldt, o_br, o_bi):
        _, vjp = jax.vjp(_s5_disc_math, lr_ref[0], li_ref[0], ldt_ref[0], br_ref[0], bi_ref[0])
        o_lr[0], o_li[0], o_ldt[0], o_br[0], o_bi[0] = vjp((dar[0], dai[0], dbr[0], dbi[0]))

    gn = pl.BlockSpec((1, G, N), lambda d: (d, 0, 0))
    g1 = pl.BlockSpec((1, G, 1), lambda d: (d, 0, 0))
    pgn = pl.BlockSpec((1, P, G, N), lambda d: (d, 0, 0, 0))
    s_gn, s_g1, s_pgn = (jax.ShapeDtypeStruct(s, F32) for s in ((2, G, N), (2, G, 1), (2, P, G, N)))
    return pl.pallas_call(
        body, name="s5_param_bwd", grid=(2,),
        in_specs=[gn, gn, g1, pgn, pgn, gn, gn, pgn, pgn], out_specs=[gn, gn, g1, pgn, pgn],
        out_shape=[s_gn, s_gn, s_g1, s_pgn, s_pgn], compiler_params=_cparams(("parallel",)),
    )(a_re, a_im, ldt, b_re, b_im, da_re, da_im, dbb_re, dbb_im)


def _blockdiag(val, nch):
    _, J, P, G, N = val.shape
    v = val.reshape(2, J, P, nch, 8, N).transpose(0, 3, 1, 4, 2, 5)
    eye = jnp.eye(8, dtype=val.dtype)
    out = v[:, :, :, :, :, None, :] * eye[None, None, None, :, None, :, None]
    return out.reshape(2, nch, J * 8 * P, 8 * N)


def _blockdiag_t(val, nch):
    _, P, G, N = val.shape
    v = val.reshape(2, P, nch, 8, N).transpose(0, 2, 3, 4, 1)
    eye = jnp.eye(8, dtype=val.dtype)
    out = v[:, :, :, :, None, :] * eye[None, None, :, None, :, None]
    return out.reshape(2, nch, 8 * N, 8 * P)


def _chunk_rows(val, nch):
    _, J, G, N = val.shape
    return val.reshape(2, J, nch, 8, N).transpose(0, 2, 1, 3, 4).reshape(2, nch, J, 8 * N)


def _shift_stack(u, back):
    tb = u.shape[0]
    tau = lax.broadcasted_iota(jnp.int32, u.shape, 0) % SUB
    parts = [u]
    for j in range(1, SUB):
        if back:
            parts.append(jnp.where(tau >= j, pltpu.roll(u, j, 0), 0.0))
        else:
            parts.append(jnp.where(tau <= SUB - 1 - j, pltpu.roll(u, tb - j, 0), 0.0))
    return jnp.concatenate(parts, axis=1).astype(BF16)


def _cmul_add(tile, pw, carry, sw):
    pr, pi, cr, ci = pw[:, :sw], pw[:, sw:], carry[:, :sw], carry[:, sw:]
    return tile + jnp.concatenate([pr * cr - pi * ci, pr * ci + pi * cr], axis=1)


def _s5_fwd(useq, waug, cmat, powf):
    _, S, W = useq.shape
    nch, sw2 = waug.shape[1], waug.shape[3]
    sw = sw2 // 2
    tb = _pick(S, (256, 128, 64, 32, 16))
    ntile = tb // SUB

    def body(u_ref, w_ref, c_ref, p_ref, y_ref, h_ref, hblk, carry):
        @pl.when(pl.program_id(2) == 0)
        def _():
            carry[...] = jnp.zeros_like(carry)

        hblk[...] = jnp.dot(_shift_stack(u_ref[0], True), w_ref[0, 0], preferred_element_type=F32)
        pw = p_ref[0, 0]

        def step(i, c):
            r = pl.multiple_of(i * SUB, SUB)
            tile = _cmul_add(hblk[pl.ds(r, SUB), :], pw, c, sw)
            hblk[pl.ds(r, SUB), :] = tile
            return tile[SUB - 1:SUB, :]

        carry[...] = lax.fori_loop(0, ntile, step, carry[...])
        hb = hblk[...].astype(BF16)
        h_ref[0] = hb
        y_ref[0] = jnp.dot(hb, c_ref[0, 0], preferred_element_type=F32)

    return pl.pallas_call(
        body, name="s5_scan_fwd", grid=(2, nch, S // tb),
        in_specs=[pl.BlockSpec((1, tb, LANE), lambda d, s, t: (d, t, s)),
                  pl.BlockSpec((1, 1, SUB * LANE, sw2), lambda d, s, t: (d, s, 0, 0)),
                  pl.BlockSpec((1, 1, sw2, LANE), lambda d, s, t: (d, s, 0, 0)),
                  pl.BlockSpec((1, 1, SUB, sw2), lambda d, s, t: (d, s, 0, 0))],
        out_specs=[pl.BlockSpec((1, tb, LANE), lambda d, s, t: (d, t, s)),
                   pl.BlockSpec((1, tb, sw2), lambda d, s, t: (d, t, s))],
        out_shape=[jax.ShapeDtypeStruct((2, S, W), F32), jax.ShapeDtypeStruct((2, S, nch * sw2), BF16)],
        scratch_shapes=[pltpu.VMEM((tb, sw2), F32), pltpu.VMEM((1, sw2), F32)],
        compiler_params=_cparams(("parallel", "parallel", "arbitrary")),
    )(useq, waug, cmat, powf)


def _s5_bwd(dyseq, useq, hseq, vaug, bt, powb):
    _, S, W = useq.shape
    nch, sw2 = vaug.shape[1], vaug.shape[3]
    sw = sw2 // 2
    tb = _pick(S, (256, 128, 64, 32, 16))
    ntile, nt = tb // SUB, S // tb

    def body(dy_ref, u_ref, h_ref, v_ref, b_ref, p_ref, du_ref, dbb_ref, dc_ref, da_ref, lam):
        @pl.when(pl.program_id(2) == 0)
        def _():
            lam[pl.ds(tb, SUB), :] = jnp.zeros((SUB, sw2), F32)
            dbb_ref[...] = jnp.zeros_like(dbb_ref)
            dc_ref[...] = jnp.zeros_like(dc_ref)
            da_ref[...] = jnp.zeros_like(da_ref)

        dy = dy_ref[0]
        lam[pl.ds(0, tb), :] = jnp.dot(_shift_stack(dy, False), v_ref[0, 0], preferred_element_type=F32)
        pw = p_ref[0, 0]

        def step(k, c):
            r = pl.multiple_of((ntile - 1 - k) * SUB, SUB)
            tile = _cmul_add(lam[pl.ds(r, SUB), :], pw, c, sw)
            lam[pl.ds(r, SUB), :] = tile
            return tile[0:1, :]

        lax.fori_loop(0, ntile, step, lam[pl.ds(tb, 1), :])
        lb = lam[pl.ds(0, tb), :].astype(BF16)
        du_ref[0] = jnp.dot(lb, b_ref[0, 0], preferred_element_type=F32)
        tn = (((0,), (0,)), ((), ()))
        dbb_ref[0, 0] += lax.dot_general(u_ref[0].astype(BF16), lb, tn, preferred_element_type=F32)
        dc_ref[0, 0] += lax.dot_general(h_ref[0], dy.astype(BF16), tn, preferred_element_type=F32)
        h = h_ref[0].astype(F32)
        ln = lam[pl.ds(1, tb), :]
        hr, hi, lr, li = h[:, :sw], h[:, sw:], ln[:, :sw], ln[:, sw:]
        da_ref[0, 0] += jnp.concatenate([jnp.sum(hr * lr + hi * li, axis=0, keepdims=True),
                                         jnp.sum(hr * li - hi * lr, axis=0, keepdims=True)], axis=1)
        lam[pl.ds(tb, SUB), :] = lam[pl.ds(0, SUB), :]

    rev = lambda d, s, t: (d, nt - 1 - t, s)
    fixed4 = lambda d, s, t: (d, s, 0, 0)
    return pl.pallas_call(
        body, name="s5_scan_bwd", grid=(2, nch, nt),
        in_specs=[pl.BlockSpec((1, tb, LANE), rev), pl.BlockSpec((1, tb, LANE), rev), pl.BlockSpec((1, tb, sw2), rev),
                  pl.BlockSpec((1, 1, SUB * LANE, sw2), fixed4), pl.BlockSpec((1, 1, sw2, LANE), fixed4),
                  pl.BlockSpec((1, 1, SUB, sw2), fixed4)],
        out_specs=[pl.BlockSpec((1, tb, LANE), rev), pl.BlockSpec((1, 1, LANE, sw2), fixed4),
                   pl.BlockSpec((1, 1, sw2, LANE), fixed4), pl.BlockSpec((1, 1, 1, sw2), fixed4)],
        out_shape=[jax.ShapeDtypeStruct((2, S, W), F32), jax.ShapeDtypeStruct((2, nch, LANE, sw2), F32),
                   jax.ShapeDtypeStruct((2, nch, sw2, LANE), F32), jax.ShapeDtypeStruct((2, nch, 1, sw2), F32)],
        scratch_shapes=[pltpu.VMEM((tb + SUB, sw2), F32)],
        compiler_params=_cparams(("parallel", "parallel", "arbitrary")),
    )(dyseq, useq, hseq, vaug, bt, powb)


def _scan_order(lat, ctx):
    return jnp.stack([jnp.concatenate([ctx, lat], 0), jnp.concatenate([ctx[::-1], lat[::-1]], 0)])


def _attn_fwd(q3, k3, v3, scale):
    H, T, dk = q3.shape
    S, dv = k3.shape[1], v3.shape[2]
    tq = _pick(T, (256, 128, 64, 32, 16))

    def body(q_ref, k_ref, v_ref, o_ref, lse_ref):
        s = lax.dot_general(q_ref[0], k_ref[0], (((1,), (1,)), ((), ())), preferred_element_type=F32) * scale
        m = jnp.max(s, axis=1, keepdims=True)
        p = jnp.exp(s - m)
        l = jnp.sum(p, axis=1, keepdims=True)
        o_ref[...] = jnp.dot((p / l).astype(BF16), v_ref[0], preferred_element_type=F32).astype(o_ref.dtype)
        lse_ref[0] = m + jnp.log(l)

    return pl.pallas_call(
        body, name="attn_fwd", grid=(H, T // tq),
        in_specs=[pl.BlockSpec((1, tq, dk), lambda h, i: (h, i, 0)), pl.BlockSpec((1, S, dk), lambda h, i: (h, 0, 0)),
                  pl.BlockSpec((1, S, dv), lambda h, i: (h, 0, 0))],
        out_specs=[pl.BlockSpec((tq, dv), lambda h, i: (i, h)), pl.BlockSpec((1, tq, 1), lambda h, i: (h, i, 0))],
        out_shape=[jax.ShapeDtypeStruct((T, H * dv), BF16), jax.ShapeDtypeStruct((H, T, 1), F32)],
        compiler_params=_cparams(("parallel", "parallel")),
    )(q3, k3, v3)


def _attn_bwd(q3, k3, v3, do, lse, scale):
    H, T, dk = q3.shape
    S, dv = k3.shape[1], v3.shape[2]
    tq = _pick(T, (256, 128, 64, 32, 16))
    tn = (((0,), (0,)), ((), ()))
    nt = (((1,), (1,)), ((), ()))

    def body(q_ref, k_ref, v_ref, do_ref, lse_ref, dq_ref, dk_ref, dv_ref):
        @pl.when(pl.program_id(1) == 0)
        def _():
            dk_ref[...] = jnp.zeros_like(dk_ref)
            dv_ref[...] = jnp.zeros_like(dv_ref)

        q, k, v, d_o = q_ref[0], k_ref[0], v_ref[0], do_ref[...]
        s = lax.dot_general(q, k, nt, preferred_element_type=F32) * scale
        p = jnp.exp(s - lse_ref[0])
        dv_ref[0] += lax.dot_general(p.astype(BF16), d_o, tn, preferred_element_type=F32)
        dp = lax.dot_general(d_o, v, nt, preferred_element_type=F32)
        ds = (p * (dp - jnp.sum(p * dp, axis=1, keepdims=True)) * scale).astype(BF16)
        dq_ref[0] = jnp.dot(ds, k, preferred_element_type=F32)
        dk_ref[0] += lax.dot_general(ds, q, tn, preferred_element_type=F32)

    return pl.pallas_call(
        body, name="attn_bwd", grid=(H, T // tq),
        in_specs=[pl.BlockSpec((1, tq, dk), lambda h, i: (h, i, 0)), pl.BlockSpec((1, S, dk), lambda h, i: (h, 0, 0)),
                  pl.BlockSpec((1, S, dv), lambda h, i: (h, 0, 0)), pl.BlockSpec((tq, dv), lambda h, i: (i, h)),
                  pl.BlockSpec((1, tq, 1), lambda h, i: (h, i, 0))],
        out_specs=[pl.BlockSpec((1, tq, dk), lambda h, i: (h, i, 0)), pl.BlockSpec((1, S, dk), lambda h, i: (h, 0, 0)),
                   pl.BlockSpec((1, S, dv), lambda h, i: (h, 0, 0))],
        out_shape=[jax.ShapeDtypeStruct((H, T, dk), F32), jax.ShapeDtypeStruct((H, S, dk), F32),
                   jax.ShapeDtypeStruct((H, S, dv), F32)],
        compiler_params=_cparams(("parallel", "arbitrary")),
    )(q3, k3, v3, do, lse)


def _rope_tables(T, heads):
    rows = T // GRID_W
    row = jnp.repeat(jnp.arange(rows, dtype=F32), GRID_W)
    col = jnp.tile(jnp.arange(GRID_W, dtype=F32), rows)
    n_freq = QK_ROPE // 4
    inv = ROPE_BASE ** (-jnp.arange(n_freq, dtype=F32) / n_freq)
    ar, ac = row[:, None] * inv, col[:, None] * inv
    cos = jnp.concatenate([jnp.cos(ar), jnp.cos(ar), jnp.cos(ac), jnp.cos(ac)], axis=1)
    sin = jnp.concatenate([-jnp.sin(ar), jnp.sin(ar), -jnp.sin(ac), jnp.sin(ac)], axis=1)
    return jnp.tile(cos, (1, heads)), jnp.tile(sin, (1, heads))


def _local_step(x, ctx, tgt, m_lat, m_ctx, p, W):
    T, D = x.shape
    Tc = ctx.shape[0]
    S = T + Tc
    S5W = p["s5_d"].shape[1]
    QR, KVR = p["q_norm"].shape[1], p["kv_norm"].shape[1]
    H = W["w_uq"].shape[1] // (QK_NOPE + QK_ROPE)
    FF = W["w_ffn_out"].shape[0]
    G, N = p["s5_a_re"].shape[1:]
    P = S5_GROUP
    nch = G // 8
    goff = W["goff"]
    o_cq, o_ckv, o_kr = S5W, S5W + QR, S5W + QR + KVR
    assert o_cq % QR == 0 and o_ckv % KVR == 0 and o_kr % LANE == 0 and goff % D == 0 and S5W % LANE == 0 and G % 8 == 0
    assert 8 * P == LANE and FF % LANE == 0
    row = lambda k, m: m[k:k + 1]
    sh1, sc1, g1, sh2, sc2, g2 = (row(k, m_lat) for k in range(6))
    csh1, csc1 = row(0, m_ctx), row(1, m_ctx)
    n1, n2, nf = p["norm1"], p["norm2"], p["norm_f"]

    (xm_lat,) = _rowmap(_normmod, "norm1_lat", T, [x], [n1, sc1, sh1], [(D, BF16)])
    (xm_ctx,) = _rowmap(_normmod, "norm1_ctx", Tc, [ctx], [n1, csc1, csh1], [(D, BF16)])
    xm_all = jnp.concatenate([xm_lat, xm_ctx], axis=0)
    h_all = _mm(xm_all, W["w_in"], "nn", F32, "mm_in")

    to_pgn = lambda a: a.transpose(0, 3, 1, 2)
    b_re, b_im = to_pgn(p["s5_b_re"]), to_pgn(p["s5_b_im"])
    c_re, c_im = p["s5_c_re"].transpose(0, 2, 1, 3), p["s5_c_im"].transpose(0, 2, 1, 3)
    ldt = p["s5_log_dt"][:, :, None]
    wre, wim, vre, vim, pfr, pfi, pbr, pbi, bbr, bbi = _s5_tables(p["s5_a_re"], p["s5_a_im"], ldt, b_re, b_im, c_re, c_im)
    waug = jnp.concatenate([_blockdiag(wre, nch), _blockdiag(wim, nch)], axis=-1).astype(BF16)
    vaug = jnp.concatenate([_blockdiag(vre, nch), _blockdiag(vim, nch)], axis=-1).astype(BF16)
    cmat = jnp.concatenate([_blockdiag_t(c_re, nch), -_blockdiag_t(c_im, nch)], axis=2).astype(BF16)
    bt = jnp.concatenate([_blockdiag_t(bbr, nch), _blockdiag_t(bbi, nch)], axis=2).astype(BF16)
    powf = jnp.concatenate([_chunk_rows(pfr, nch), _chunk_rows(pfi, nch)], axis=-1)
    powb = jnp.concatenate([_chunk_rows(pbr, nch), _chunk_rows(pbi, nch)], axis=-1)
    u_all = h_all[:, :S5W]
    useq = _scan_order(u_all[:T], u_all[T:])
    yseq, hseq = _s5_fwd(useq, waug, cmat, powf)
    y_f, y_r = yseq[0, Tc:], yseq[1, Tc:][::-1]

    def s5_combine(u, yf, yr, dskip):
        y5 = dskip * u + yf + yr
        return y5, jax.nn.gelu(y5)

    y5, z = _rowmap(s5_combine, "s5_combine", T, [(h_all, S5W, 0), y_f, y_r], [p["s5_d"]], [(S5W, F32), (S5W, BF16)])

    (qn,) = _rowmap(_rms, "q_norm", T, [(h_all, QR, o_cq // QR)], [p["q_norm"]], [(QR, BF16)])
    (kvn,) = _rowmap(_rms, "kv_norm", S, [(h_all, KVR, o_ckv // KVR)], [p["kv_norm"]], [(KVR, BF16)])
    qraw = _mm(qn, W["w_uq"], "nn", F32, "mm_uq")
    kvraw = _mm(kvn, W["w_ukv"], "nn", BF16, "mm_ukv")
    cos_q, sin_q = _rope_tables(T, H)
    padl = lambda t: jnp.pad(t[:, :QK_ROPE], ((0, Tc), (0, LANE - QK_ROPE)))
    cos_k = padl(cos_q) + jnp.pad(jnp.ones((Tc, LANE), F32), ((T, 0), (0, 0)))
    sin_k = padl(sin_q)
    hn = H * QK_NOPE

    def q_post(q, cos, sin):
        return q[:, :hn], _rope(q[:, hn:], cos, sin)

    q_nope, q_rope = _rowmap(q_post, "q_rope", T, [qraw, cos_q, sin_q], [], [(hn, BF16), (H * QK_ROPE, BF16)])
    (kr,) = _rowmap(_rope, "k_rope", S, [(h_all, LANE, o_kr // LANE), cos_k, sin_k], [], [(LANE, BF16)])
    q3 = jnp.concatenate([q_nope.reshape(T, H, QK_NOPE), q_rope.reshape(T, H, QK_ROPE)], axis=-1).transpose(1, 0, 2)
    k3 = jnp.concatenate([kvraw[:, :hn].reshape(S, H, QK_NOPE),
                          jnp.broadcast_to(kr[:, None, :QK_ROPE], (S, H, QK_ROPE))], axis=-1).transpose(1, 0, 2)
    v3 = kvraw[:, hn:].reshape(S, H, V_DIM).transpose(1, 0, 2)
    scale = (QK_NOPE + QK_ROPE) ** -0.5
    o, lse = _attn_fwd(q3, k3, v3, scale)

    zz = _mm(z, W["w_glu"], "nn", F32, "mm_glu")
    br_mla = _mm(o, W["w_mla_o"], "nn", F32, "mm_mla_o")

    def merge(zz, brm, gs, gm):
        a, b = zz[:, :D], zz[:, D:]
        return jax.nn.sigmoid(gs) * (a * jax.nn.sigmoid(b)) + jax.nn.sigmoid(gm) * brm

    gb = goff // D
    merge_ins = [zz, br_mla, (h_all, D, gb), (h_all, D, gb + 1)]
    (mix,) = _rowmap(merge, "merge", T, merge_ins, [], [(D, BF16)])
    out1 = _mm(mix, W["w_out"], "nn", F32, "mm_out")

    def resid_norm2(x, out1, g1, n2, sc2, sh2):
        x1 = x + g1 * out1
        return x1, _normmod(x1, n2, sc2, sh2)

    x1, hm = _rowmap(resid_norm2, "resid_norm2", T, [x, out1], [g1, n2, sc2, sh2], [(D, F32), (D, BF16)])

    ab = _mm(hm, W["w_ffn_in"], "nn", F32, "mm_ffn_in")

    def swiglu_act(a, b):
        return jax.nn.silu(a) * b

    (f,) = _rowmap(swiglu_act, "ffn_act", T, [(ab, FF, 0), (ab, FF, 1)], [], [(FF, BF16)])
    out2 = _mm(f, W["w_ffn_out"], "nn", F32, "mm_ffn_out")

    def loss_rows(x1, out2, g2, nf, tgt):
        y = _rms(x1 + g2 * out2, nf)
        return 0.5 * jnp.sum(jnp.mean(jnp.square(y - tgt), axis=-1))

    def final(x1, out2, tgt, g2, nf):
        val, (dx1, dout2, dg2, dnf) = jax.value_and_grad(loss_rows, argnums=(0, 1, 2, 3))(x1, out2, g2, nf, tgt)
        return dx1, dout2, jnp.full((1, LANE), val, F32), dg2, dnf

    dx2, dout2, loss_acc, dg2, dnf = _rowmap(final, "final_loss", T, [x1, out2, tgt], [g2, nf],
                                             [(D, F32), (D, BF16)], [LANE, D, D])

    gW = {}
    df = _mm(dout2, W["w_ffn_out"], "nt", F32, "mm_ffn_out_dx")
    gW["w_ffn_out"] = _mm(f, dout2, "tn", F32, "mm_ffn_out_dw")

    def swiglu_bwd(a, b, df):
        _, vjp = jax.vjp(swiglu_act, a, b)
        da, db = vjp(df)
        return jnp.concatenate([da, db], axis=1)

    (dab,) = _rowmap(swiglu_bwd, "ffn_act_bwd", T, [(ab, FF, 0), (ab, FF, 1), df], [], [(2 * FF, BF16)])
    dhm = _mm(dab, W["w_ffn_in"], "nt", F32, "mm_ffn_in_dx")
    gW["w_ffn_in"] = _mm(hm, dab, "tn", F32, "mm_ffn_in_dw")

    def resid_norm2_bwd(x, out1, dx2, dhm, g1, n2, sc2, sh2):
        _, vjp = jax.vjp(resid_norm2, x, out1, g1, n2, sc2, sh2)
        dx, dout1, dg1, dn2, dsc2, dsh2 = vjp((dx2, dhm))
        return dx, dout1, dg1, dn2, dsc2, dsh2

    dx1, dout1, dg1, dn2, dsc2, dsh2 = _rowmap(resid_norm2_bwd, "resid_norm2_bwd", T, [x, out1, dx2, dhm],
                                               [g1, n2, sc2, sh2], [(D, F32), (D, BF16)], [D, D, D, D])

    dmix = _mm(dout1, W["w_out"], "nt", F32, "mm_out_dx")
    gW["w_out"] = _mm(mix, dout1, "tn", F32, "mm_out_dw")

    def merge_bwd(zz, brm, gs, gm, dmix):
        _, vjp = jax.vjp(merge, zz, brm, gs, gm)
        dzz, dbrm, dgs, dgm = vjp(dmix)
        return dzz, dbrm, jnp.concatenate([dgs, dgm], axis=1)

    dzz, dbrm, dgates = _rowmap(merge_bwd, "merge_bwd", T, merge_ins + [dmix], [],
                                [(2 * D, BF16), (D, BF16), (2 * D, BF16)])
    do = _mm(dbrm, W["w_mla_o"], "nt", BF16, "mm_mla_o_dx")
    gW["w_mla_o"] = _mm(o, dbrm, "tn", F32, "mm_mla_o_dw")
    dz = _mm(dzz, W["w_glu"], "nt", F32, "mm_glu_dx")
    gW["w_glu"] = _mm(z, dzz, "tn", F32, "mm_glu_dw")

    def s5_combine_bwd(u, y5, dz, dskip):
        _, vjp = jax.vjp(lambda y: jax.nn.gelu(y), y5)
        (dy5,) = vjp(dz)
        return dy5, jnp.sum(dy5 * u, axis=0, keepdims=True)

    dy5, d_skip = _rowmap(s5_combine_bwd, "s5_combine_bwd", T, [(h_all, S5W, 0), y5, dz], [p["s5_d"]], [(S5W, F32)], [S5W])

    dq3, dk3, dv3 = _attn_bwd(q3, k3, v3, do, lse, scale)
    dq_t = dq3.transpose(1, 0, 2)
    dq_cat = jnp.concatenate([dq_t[:, :, :QK_NOPE].reshape(T, hn), dq_t[:, :, QK_NOPE:].reshape(T, H * QK_ROPE)], axis=1)

    def q_post_bwd(dq, cos, sin):
        return jnp.concatenate([dq[:, :hn], _rope_bwd(dq[:, hn:], cos, sin)], axis=1)

    (dqraw,) = _rowmap(q_post_bwd, "q_rope_bwd", T, [dq_cat, cos_q, sin_q], [], [(H * (QK_NOPE + QK_ROPE), BF16)])
    dk_t = dk3.transpose(1, 0, 2)
    dkvraw = jnp.concatenate([dk_t[:, :, :QK_NOPE].reshape(S, hn), dv3.transpose(1, 0, 2).reshape(S, H * V_DIM)], axis=1)
    dkr_heads = jnp.pad(dk_t[:, :, QK_NOPE:], ((0, 0), (0, 0), (0, LANE - QK_ROPE))).reshape(S, H * LANE)

    def k_rope_bwd(dkh, cos, sin):
        d = dkh[:, :LANE]
        for h in range(1, H):
            d = d + dkh[:, h * LANE:(h + 1) * LANE]
        return _rope_bwd(d, cos, sin)

    (dkr,) = _rowmap(k_rope_bwd, "k_rope_bwd", S, [dkr_heads, cos_k, sin_k], [], [(LANE, BF16)])
    dqn = _mm(dqraw, W["w_uq"], "nt", F32, "mm_uq_dx")
    gW["w_uq"] = _mm(qn, dqraw, "tn", F32, "mm_uq_dw")
    dkvn = _mm(dkvraw, W["w_ukv"], "nt", F32, "mm_ukv_dx")
    gW["w_ukv"] = _mm(kvn, dkvraw, "tn", F32, "mm_ukv_dw")

    def rms_bwd(cx, dn, g):
        _, vjp = jax.vjp(_rms, cx, g)
        return vjp(dn)

    dcq, dq_norm = _rowmap(rms_bwd, "q_norm_bwd", T, [(h_all, QR, o_cq // QR), dqn], [p["q_norm"]], [(QR, BF16)], [QR])
    dckv, dkv_norm = _rowmap(rms_bwd, "kv_norm_bwd", S, [(h_all, KVR, o_ckv // KVR), dkvn], [p["kv_norm"]],
                             [(KVR, BF16)], [KVR])

    zc = jnp.zeros((Tc, S5W), F32)
    dyseq = _scan_order(dy5, zc)
    duseq, dbb, dcm, dab_s5 = _s5_bwd(dyseq, useq, hseq, vaug, bt, powb)

    def du_combine(a, b, dy5, dskip):
        return a + b + dskip * dy5

    (du_lat,) = _rowmap(du_combine, "s5_du_lat", T, [duseq[0, Tc:], duseq[1, Tc:][::-1], dy5], [p["s5_d"]], [(S5W, BF16)])
    (du_ctx,) = _rowmap(lambda a, b: a + b, "s5_du_ctx", Tc, [duseq[0, :Tc], duseq[1, :Tc][::-1]], [], [(S5W, BF16)])
    dbb = jnp.einsum("dsgpcgn->dcpsgn", dbb.reshape(2, nch, 8, P, 2, 8, N)).reshape(2, 2, P, G, N)
    dcm = jnp.einsum("dscgngp->dcsgpn", dcm.reshape(2, nch, 2, 8, N, 8, P)).reshape(2, 2, G, P, N)
    da = dab_s5.reshape(2, nch, 2, 8, N).transpose(0, 2, 1, 3, 4).reshape(2, 2, G, N)
    d_lr, d_li, d_ldt, d_br, d_bi = _s5_param_bwd(p["s5_a_re"], p["s5_a_im"], ldt, b_re, b_im,
                                                  da[:, 0], da[:, 1], dbb[:, 0], dbb[:, 1])
    from_pgn = lambda a: a.transpose(0, 2, 3, 1)

    zt = lambda w: jnp.zeros((Tc, w), BF16)
    dh_lat = jnp.concatenate([du_lat, dcq, dckv[:T], dkr[:T], jnp.zeros((T, goff - o_kr - LANE), BF16), dgates], axis=1)
    dh_ctx = jnp.concatenate([du_ctx, zt(QR), dckv[T:], dkr[T:], zt(goff - o_kr - LANE + 2 * D)], axis=1)
    dh_all = jnp.concatenate([dh_lat, dh_ctx], axis=0)
    dxm = _mm(dh_all, W["w_in"], "nt", F32, "mm_in_dx")
    gW["w_in"] = _mm(xm_all, dh_all, "tn", F32, "mm_in_dw")

    def norm1_bwd(x, dxm, dx1, n1, sc, sh):
        _, vjp = jax.vjp(_normmod, x, n1, sc, sh)
        dx, dn, dsc, dsh = vjp(dxm)
        return dx + dx1, dn, dsc, dsh

    grad_x, dn1_l, dsc1, dsh1 = _rowmap(norm1_bwd, "norm1_lat_bwd", T, [x, dxm, dx1], [n1, sc1, sh1], [(D, F32)], [D, D, D])

    def norm1_ctx_bwd(x, dxm, n1, sc, sh):
        _, vjp = jax.vjp(_normmod, x, n1, sc, sh)
        return vjp(dxm)[1:]

    dn1_c, dcsc1, dcsh1 = _rowmap(norm1_ctx_bwd, "norm1_ctx_bwd", Tc, [ctx, dxm[T:]], [n1, csc1, csh1], [], [D, D, D])

    zero = jnp.zeros((1, D), F32)
    dm_lat = jnp.concatenate([dsh1, dsc1, dg1, dsh2, dsc2, dg2], axis=0)
    dm_ctx = jnp.concatenate([dcsh1, dcsc1, zero, zero, zero, zero], axis=0)
    small = {
        "norm1": dn1_l + dn1_c, "norm2": dn2, "norm_f": dnf, "q_norm": dq_norm, "kv_norm": dkv_norm, "s5_d": d_skip,
        "s5_a_re": d_lr, "s5_a_im": d_li, "s5_log_dt": d_ldt[:, :, 0], "s5_b_re": from_pgn(d_br), "s5_b_im": from_pgn(d_bi),
        "s5_c_re": dcm[:, 0], "s5_c_im": -dcm[:, 1],
    }
    return loss_acc[:, :1], grad_x, small, dm_lat, dm_ctx, gW


BIG = ("w_in", "w_glu", "w_uq", "w_ukv", "w_mla_o", "w_out", "w_ffn_in", "w_ffn_out")
ROW_SHARDED = ("w_out", "w_ffn_out")
SMALL = ("c_ctx", "b_mod", "norm1", "norm2", "s5_a_re", "s5_a_im", "s5_log_dt", "s5_b_re", "s5_b_im", "s5_c_re",
         "s5_c_im", "s5_d", "q_norm", "kv_norm", "norm_f")
WEIGHTS = ("c_ctx", "w_mod", "b_mod", "norm1", "norm2", "w_in", "s5_a_re", "s5_a_im", "s5_log_dt", "s5_b_re", "s5_b_im",
           "s5_c_re", "s5_c_im", "s5_d", "w_glu", "q_norm", "kv_norm", "w_uq", "w_ukv", "w_mla_o", "w_out", "w_ffn_in",
           "w_ffn_out", "norm_f")


def _heads_split(w, heads, first):
    k = w.shape[0]
    w3 = w.reshape(k, heads, -1)
    return jnp.concatenate([w3[:, :, :first].reshape(k, -1), w3[:, :, first:].reshape(k, -1)], axis=1)


def _heads_merge(w, heads, first):
    k = w.shape[0]
    a, b = w[:, :heads * first].reshape(k, heads, first), w[:, heads * first:].reshape(k, heads, -1)
    return jnp.concatenate([a, b], axis=2).reshape(k, -1)


def _model_weights(full, D):
    w_in = full["w_in"]
    n_front = w_in.shape[1] - 2 * D
    goff = -(-n_front // D) * D
    W = dict(full)
    W["w_in"] = jnp.concatenate([w_in[:, :n_front], jnp.zeros((D, goff - n_front), w_in.dtype), w_in[:, n_front:]], axis=1)
    W["goff"] = goff
    heads = full["w_uq"].shape[1] // (QK_NOPE + QK_ROPE)
    W["w_uq"] = _heads_split(full["w_uq"], heads, QK_NOPE)
    W["w_ukv"] = _heads_split(full["w_ukv"], heads, QK_NOPE)
    return W


def _to_slots(g, name):
    if name in ROW_SHARDED:
        return g.reshape(N_DEV, g.shape[0] // N_DEV, g.shape[1])
    return g.reshape(g.shape[0], N_DEV, g.shape[1] // N_DEV).transpose(1, 0, 2)


def _from_slots(w8, name):
    if name in ROW_SHARDED:
        return w8.reshape(-1, w8.shape[-1])
    return w8.transpose(1, 0, 2).reshape(w8.shape[1], -1)


def _pack_slots(arrs, dtype):
    parts = []
    for a in arrs:
        flat = a.reshape(N_DEV, -1).astype(dtype)
        pad = (-flat.shape[1]) % (PACK_W * 16)
        parts.append(jnp.pad(flat, ((0, 0), (0, pad))).reshape(N_DEV, -1, PACK_W))
    out = jnp.concatenate(parts, axis=1)
    return jnp.pad(out, ((0, 0), (0, (-out.shape[1]) % PACK_ROWS), (0, 0)))


def kernel(x, c, ctx, c_ctx, w_mod, b_mod, norm1, norm2, w_in, s5_a_re, s5_a_im, s5_log_dt, s5_b_re, s5_b_im, s5_c_re, s5_c_im, s5_d, w_glu, q_norm, kv_norm, w_uq, w_ukv, w_mla_o, w_out, w_ffn_in, w_ffn_out, norm_f, loss_target, m_c_ctx, m_w_mod, m_b_mod, m_norm1, m_norm2, m_w_in, m_s5_a_re, m_s5_a_im, m_s5_log_dt, m_s5_b_re, m_s5_b_im, m_s5_c_re, m_s5_c_im, m_s5_d, m_w_glu, m_q_norm, m_kv_norm, m_w_uq, m_w_ukv, m_w_mla_o, m_w_out, m_w_ffn_in, m_w_ffn_out, m_norm_f, v_c_ctx, v_w_mod, v_b_mod, v_norm1, v_norm2, v_w_in, v_s5_a_re, v_s5_a_im, v_s5_log_dt, v_s5_b_re, v_s5_b_im, v_s5_c_re, v_s5_c_im, v_s5_d, v_w_glu, v_q_norm, v_kv_norm, v_w_uq, v_w_ukv, v_w_mla_o, v_w_out, v_w_ffn_in, v_w_ffn_out, v_norm_f):
    a = dict(locals())
    D = x.shape[-1]
    me = 4 * lax.axis_index("x") + 2 * lax.axis_index("y") + lax.axis_index("c")

    shard = {n: a[n][0] for n in BIG}
    wg = _all_gather(_pack_rows([shard[n] for n in BIG], BF16), "ag_weights")
    full = {n: _from_slots(w8, n) for n, w8 in zip(BIG, _unpack_rows(wg, [shard[n].shape for n in BIG]))}
    W = _model_weights(full, D)
    goff = W["goff"]

    wm = w_mod[0]
    ncol = wm.shape[1]
    cg = _all_gather(jnp.broadcast_to(c, (8, D)), "ag_c")
    c16 = jnp.concatenate([cg[:, 0, :], c_ctx[None], jnp.zeros((7, D), F32)], axis=0)
    (s16,) = _rowmap(jax.nn.silu, "mod_silu", 16, [c16], [], [(D, BF16)])
    m_cols = _mm(s16, wm, "nn", F32, "mm_mod")
    mg = _all_gather(m_cols, "ag_mod").transpose(1, 0, 2).reshape(16, N_DEV * ncol)
    (m16,) = _rowmap(lambda m, b: m + b, "mod_bias", 16, [mg], [b_mod], [(N_DEV * ncol, F32)])
    m_lat = lax.dynamic_slice(m16, (me, 0), (1, 6 * D)).reshape(6, D)
    m_ctx = m16[8].reshape(6, D)

    p = {n: a[n][0] for n in ("norm1", "norm2", "s5_a_re", "s5_a_im", "s5_log_dt", "s5_b_re", "s5_b_im", "s5_c_re",
                              "s5_c_im", "q_norm", "kv_norm")}
    p = {k: (v[None] if v.ndim == 1 else v) for k, v in p.items()}
    p["s5_d"] = s5_d.reshape(1, -1)
    p["norm_f"] = norm_f[None]
    loss_part, grad_x, small, dm_lat, dm_ctx, gW = _local_step(x[0], ctx[0], loss_target[0], m_lat, m_ctx, p, W)
    loss = lax.psum(loss_part[0, 0], ("x", "y", "c"))

    dm16 = jnp.concatenate([dm_lat.reshape(1, -1), dm_ctx.reshape(1, -1), jnp.zeros((14, 6 * D), F32)], axis=0)
    dmg = _all_gather(dm16, "ag_dmod")
    dm_sum = _sum_slots(dmg, "sum_dmod")
    dM16 = jnp.concatenate([dmg[:, 0, :], dm_sum[1:2], jnp.zeros((7, 6 * D), F32)], axis=0)
    (g_b_mod,) = _rowmap(lambda d: jnp.sum(d, axis=0, keepdims=True), "b_mod_grad", 16, [dM16], [], [], [6 * D])
    dM_loc = lax.dynamic_slice(dM16, (0, me * ncol), (16, ncol))
    g_w_mod = _mm(s16, dM_loc, "tn", F32, "mm_mod_dw")
    ds16_part = _mm(dM_loc, wm, "nt", F32, "mm_mod_dx")

    small_names = [n for n in SMALL if n not in ("c_ctx", "b_mod")]
    small_shapes = [small[n].shape for n in small_names] + [(1, D)]
    sg = _all_gather(_pack_rows([small[n] for n in small_names] + [ds16_part[8:9]], F32), "ag_small")
    parts = _unpack_rows(_sum_slots(sg, "sum_small"), small_shapes)
    grads = dict(zip(small_names, parts[:-1]))

    def silu_bwd(cc, ds):
        _, vjp = jax.vjp(jax.nn.silu, cc)
        return vjp(ds)[0]

    (g_c_ctx,) = _rowmap(silu_bwd, "c_ctx_grad", 1, [c_ctx[None], parts[-1]], [], [(D, F32)])
    grads["c_ctx"], grads["b_mod"] = g_c_ctx, g_b_mod

    gW = dict(gW)
    n_front = w_in.shape[-1] * N_DEV - 2 * D
    gW["w_in"] = jnp.concatenate([gW["w_in"][:, :n_front], gW["w_in"][:, goff:]], axis=1)
    heads = gW["w_uq"].shape[1] // (QK_NOPE + QK_ROPE)
    gW["w_uq"] = _heads_merge(gW["w_uq"], heads, QK_NOPE)
    gW["w_ukv"] = _heads_merge(gW["w_ukv"], heads, QK_NOPE)
    ex = _exchange(_pack_slots([_to_slots(gW[n], n) for n in BIG], BF16), "exchange_grads")
    for n, g in zip(BIG, _unpack_rows(_sum_slots(ex, "sum_grads"), [shard[n].shape for n in BIG])):
        grads[n] = g
    grads["w_mod"] = g_w_mod

    out = {}
    for n in BIG + ("w_mod",):
        w2 = a[n][0]
        d, nm, nv = _adamw(w2, grads[n], a["m_" + n][0], a["v_" + n][0], "adamw_" + n)
        for k, val in (("grad_", grads[n]), ("delta_", d), ("new_m_", nm), ("new_v_", nv)):
            out[k + n] = val.reshape(a[n].shape)
    packs = [_pack_rows([t[n] for n in SMALL], F32) for t in (
        {n: a[n] for n in SMALL}, {n: grads[n] for n in SMALL}, {n: a["m_" + n] for n in SMALL}, {n: a["v_" + n] for n in SMALL})]
    res = _adamw(*packs, "adamw_small")
    shapes = [a[n].shape for n in SMALL]
    for k, packed in (("grad_", packs[1]), ("delta_", res[0]), ("new_m_", res[1]), ("new_v_", res[2])):
        for n, val in zip(SMALL, _unpack_rows(packed, shapes)):
            out[k + n] = val
    return (loss, grad_x[None]) + tuple(out[k + n] for k in ("grad_", "delta_", "new_m_", "new_v_") for n in WEIGHTS)
```

```python
import functools
import math

import jax
import jax.numpy as jnp
from jax import lax
from jax.experimental import pallas as pl
from jax.experimental.pallas import tpu as pltpu

F32 = jnp.float32
BF16 = jnp.bfloat16

N_DEV = 8
N_CHIP = 4
EPS = 1e-6
GRID_W = 64
S5_GROUP = 16
QK_NOPE, QK_ROPE, V_DIM = 128, 64, 128
ROPE_BASE = 10000.0
ADAM_LR, ADAM_B1, ADAM_B2, ADAM_EPS, ADAM_WD, ADAM_STEP = 0.001, 0.9, 0.999, 1e-08, 0.01, 10

LANE = 128
SUB = 8
PACK_W = 1024
PACK_ROWS = 32
VMEM_LIMIT = 48 << 20
ROWMAP_TILE_BYTES = 10 << 20
MESH = pl.DeviceIdType.MESH
_NT = (((1,), (1,)), ((), ()))
_TN = (((0,), (0,)), ((), ()))


def _pick(dim, cands):
    for c in cands:
        if dim % c == 0:
            return c
    return dim


def _cparams(sem):
    return pltpu.CompilerParams(dimension_semantics=sem, vmem_limit_bytes=VMEM_LIMIT)


def _mm(a, b, dims, out_dtype, name, out_slots=None):
    a = a.astype(BF16)
    b = b.astype(BF16)
    b3 = b.ndim == 3
    if dims == "nn":
        (M, K), N = a.shape, (b.shape[0] * b.shape[2] if b3 else b.shape[1])
    elif dims == "nt":
        M, N = a.shape[0], b.shape[-2]
        K = b.shape[0] * b.shape[2] if b3 else b.shape[1]
    else:
        (K, M), N = a.shape, b.shape[1]
    unit_n = b.shape[2] if (b3 and dims == "nn") else (N // out_slots if out_slots else N)
    unit_k = b.shape[2] if (b3 and dims == "nt") else K
    tn = _pick(unit_n, (512, 256) + ((unit_n,) if unit_n <= 1536 else ()) + (128,))
    tk = _pick(unit_k, ((2048, 1536) if dims != "tn" else ()) + (1024, 768, 512, 256) + ((unit_k,) if unit_k <= 1536 else ())
               + (128, 64, 32, 16))
    tm = _pick(M, ((1024, 768) if tn <= 512 else ()) + (512, 256, 128, 64, 32, 16))
    nk, npt, kpt = K // tk, unit_n // tn, unit_k // tk
    if dims == "nn":
        a_spec = pl.BlockSpec((tm, tk), lambda i, j, k: (i, k))
        b_spec = (pl.BlockSpec((None, tk, tn), lambda i, j, k: (j // npt, k, j % npt)) if b3
                  else pl.BlockSpec((tk, tn), lambda i, j, k: (k, j)))
        dn = (((1,), (0,)), ((), ()))
    elif dims == "nt":
        a_spec = pl.BlockSpec((tm, tk), lambda i, j, k: (i, k))
        b_spec = (pl.BlockSpec((None, tn, tk), lambda i, j, k: (k // kpt, j, k % kpt)) if b3
                  else pl.BlockSpec((tn, tk), lambda i, j, k: (j, k)))
        dn = _NT
    else:
        a_spec = pl.BlockSpec((tk, tm), lambda i, j, k: (k, i))
        b_spec = pl.BlockSpec((tk, tn), lambda i, j, k: (k, j))
        dn = _TN
    if out_slots:
        out_spec = pl.BlockSpec((None, tm, tn), lambda i, j, k: (j // npt, i, j % npt))
        out_shape = jax.ShapeDtypeStruct((out_slots, M, unit_n), out_dtype)
    else:
        out_spec = pl.BlockSpec((tm, tn), lambda i, j, k: (i, j))
        out_shape = jax.ShapeDtypeStruct((M, N), out_dtype)

    def body(a_ref, b_ref, o_ref, acc_ref):
        k = pl.program_id(2)

        @pl.when(k == 0)
        def _():
            acc_ref[...] = jnp.zeros_like(acc_ref)

        acc_ref[...] += lax.dot_general(a_ref[...], b_ref[...], dn, preferred_element_type=F32)

        @pl.when(k == nk - 1)
        def _():
            o_ref[...] = acc_ref[...].astype(o_ref.dtype)

    return pl.pallas_call(
        body, name=name, grid=(M // tm, N // tn, nk),
        in_specs=[a_spec, b_spec], out_specs=out_spec, out_shape=out_shape,
        scratch_shapes=[pltpu.VMEM((tm, tn), F32)],
        compiler_params=_cparams(("parallel", "parallel", "arbitrary")),
    )(a, b)


def _rowmap(fn, name, M, row_ins, bc_ins, row_outs, acc_outs=()):
    row_ins = [r if isinstance(r, tuple) else (r, r.shape[1], 0) for r in row_ins]
    row_bytes = sum(w * a.dtype.itemsize for a, w, _ in row_ins) + sum(w * jnp.dtype(d).itemsize for w, d in row_outs)
    widest = max([w for _, w, _ in row_ins] + [w for w, _ in row_outs])
    row_bytes = 2 * row_bytes + 6 * 4 * widest
    tm = _pick(M, [t for t in (512, 256, 128, 64, 32, 16) if t * row_bytes <= ROWMAP_TILE_BYTES] + [16])
    n_in, n_row, n_acc = len(row_ins) + len(bc_ins), len(row_outs), len(acc_outs)

    def body(*refs):
        res = fn(*[r[...] for r in refs[:n_in]])
        res = res if isinstance(res, (tuple, list)) else (res,)
        outs = refs[n_in:]
        for k in range(n_row):
            outs[k][...] = res[k].astype(outs[k].dtype)
        if n_acc:
            @pl.when(pl.program_id(0) == 0)
            def _():
                for k in range(n_acc):
                    outs[n_row + k][...] = jnp.zeros_like(outs[n_row + k])

            for k in range(n_acc):
                outs[n_row + k][...] += res[n_row + k].astype(F32)

    in_specs = [pl.BlockSpec((tm, w), functools.partial(lambda i, blk: (i, blk), blk=blk)) for _, w, blk in row_ins]
    in_specs += [pl.BlockSpec(b.shape, lambda i: (0, 0)) for b in bc_ins]
    out_specs = [pl.BlockSpec((tm, w), lambda i: (i, 0)) for w, _ in row_outs]
    out_specs += [pl.BlockSpec((1, w), lambda i: (0, 0)) for w in acc_outs]
    out_shape = [jax.ShapeDtypeStruct((M, w), d) for w, d in row_outs]
    out_shape += [jax.ShapeDtypeStruct((1, w), F32) for w in acc_outs]
    return pl.pallas_call(
        body, name=name, grid=(M // tm,), in_specs=in_specs, out_specs=out_specs, out_shape=out_shape,
        compiler_params=_cparams(("arbitrary",) if n_acc else ("parallel",)),
    )(*[a for a, _, _ in row_ins], *bc_ins)


def _rms(x, g):
    return x * lax.rsqrt(jnp.mean(x * x, axis=-1, keepdims=True) + EPS) * g


def _normmod(x, g, sc, sh):
    return _rms(x, g) * (1.0 + sc) + sh


def _swap16(v):
    w = v.shape[1]
    lane = lax.broadcasted_iota(jnp.int32, v.shape, 1)
    return jnp.where((lane // 16) % 2 == 0, pltpu.roll(v, w - 16, 1), pltpu.roll(v, 16, 1))


def _rope(v, cos, sin_signed):
    return v * cos + _swap16(v) * sin_signed


def _rope_bwd(d, cos, sin_signed):
    return d * cos + _swap16(d * sin_signed)


def _mesh_pos():
    return lax.axis_index("x"), lax.axis_index("y"), lax.axis_index("c")


def _hbm_call(body, name, ins, out_shapes, n_sems):
    any_spec = pl.BlockSpec(memory_space=pl.ANY)
    return pl.pallas_call(
        body, name=name, out_shape=out_shapes, in_specs=[any_spec] * len(ins), out_specs=[any_spec] * len(out_shapes),
        scratch_shapes=[pltpu.SemaphoreType.DMA((n_sems,)), pltpu.SemaphoreType.DMA((n_sems,)),
                        pltpu.SemaphoreType.DMA((len(ins),))],
    )(*ins)


def _all_gather(xs, name):
    n = len(xs)

    def body(*refs):
        x_refs, out_refs, (send_sems, recv_sems, local_sems) = refs[:n], refs[n:2 * n], refs[2 * n:]
        x, y, c = _mesh_pos()
        me, sibling = (x, y, c), (x, y, 1 - c)
        chips = [(1 - x, y), (x, 1 - y), (1 - x, 1 - y)]
        locals_, first, passed, arrivals = [], [], [], []
        for a in range(n):
            def slot(px, py, pc, a=a):
                return out_refs[a].at[4 * px + 2 * py + pc]

            def copy(k, block, to, src=None, a=a, slot=slot):
                return pltpu.make_async_remote_copy(
                    src_ref=slot(*block) if src is None else src, dst_ref=slot(*block),
                    send_sem=send_sems.at[7 * a + k], recv_sem=recv_sems.at[7 * a + k], device_id=to, device_id_type=MESH)

            locals_.append(pltpu.make_async_copy(x_refs[a], slot(*me), local_sems.at[a]))
            first.append(copy(0, me, sibling, src=x_refs[a]))
            first += [copy(1 + j, me, (*chip, c), src=x_refs[a]) for j, chip in enumerate(chips)]
            passed.append([copy(4 + j, (*chip, c), sibling) for j, chip in enumerate(chips)])
            arrivals.append([copy(1 + j, (*chip, c), me) for j, chip in enumerate(chips)]
                            + [copy(0, sibling, me)] + [copy(4 + j, (*chip, 1 - c), me) for j, chip in enumerate(chips)])
        for cp in locals_ + first:
            cp.start()
        for j in range(3):
            for a in range(n):
                arrivals[a][j].wait_recv()
                passed[a][j].start()
        for a in range(n):
            for cp in arrivals[a][3:]:
                cp.wait_recv()
        for cp in first + [p for ps in passed for p in ps]:
            cp.wait_send()
        for cp in locals_:
            cp.wait()

    return _hbm_call(body, name, xs, [jax.ShapeDtypeStruct((N_DEV,) + x.shape, x.dtype) for x in xs], 7 * n)


def _rs_pair(ps, name):
    n = len(ps)

    def body(*refs):
        p_refs, out_refs, (send_sems, recv_sems, _) = refs[:n], refs[n:2 * n], refs[2 * n:]
        x, y, c = _mesh_pos()
        sends, recvs = [], []
        for a in range(n):
            for q in range(N_CHIP):
                sem = dict(send_sem=send_sems.at[4 * a + q], recv_sem=recv_sems.at[4 * a + q],
                           device_id=(x, y, 1 - c), device_id_type=MESH)
                sends.append(pltpu.make_async_remote_copy(src_ref=p_refs[a].at[2 * q + 1 - c], dst_ref=out_refs[a].at[q], **sem))
                recvs.append(pltpu.make_async_remote_copy(src_ref=p_refs[a].at[2 * q + c], dst_ref=out_refs[a].at[q], **sem))
        for cp in sends:
            cp.start()
        for cp in recvs:
            cp.wait_recv()
        for cp in sends:
            cp.wait_send()

    return _hbm_call(body, name, ps, [jax.ShapeDtypeStruct((N_CHIP,) + p.shape[1:], p.dtype) for p in ps], 4 * n)


def _rs_chips(qs, name):
    n = len(qs)

    def body(*refs):
        q_refs, out_refs, (send_sems, recv_sems, local_sems) = refs[:n], refs[n:2 * n], refs[2 * n:]
        x, y, c = _mesh_pos()
        mine = 2 * x + y
        locals_, sends, recvs = [], [], []
        for a in range(n):
            locals_.append(pltpu.make_async_copy(q_refs[a].at[mine], out_refs[a].at[mine], local_sems.at[a]))
            for r in range(1, N_CHIP):
                px = 1 - x if r & 2 else x
                py = 1 - y if r & 1 else y
                peer = 2 * px + py
                sem = dict(send_sem=send_sems.at[3 * a + r - 1], recv_sem=recv_sems.at[3 * a + r - 1],
                           device_id=(px, py, c), device_id_type=MESH)
                sends.append(pltpu.make_async_remote_copy(src_ref=q_refs[a].at[peer], dst_ref=out_refs[a].at[mine], **sem))
                recvs.append(pltpu.make_async_remote_copy(src_ref=q_refs[a].at[peer], dst_ref=out_refs[a].at[peer], **sem))
        for cp in locals_ + sends:
            cp.start()
        for cp in recvs:
            cp.wait_recv()
        for cp in sends:
            cp.wait_send()
        for cp in locals_:
            cp.wait()

    return _hbm_call(body, name, qs, [jax.ShapeDtypeStruct(q.shape, q.dtype) for q in qs], 3 * n)


def _add_pair(p, r, name):
    _, R, C = p.shape
    tr = _pick(R, (512, 256, 128, 64, 32, 16))

    def body(c_ref, p_ref, r_ref, o_ref):
        o_ref[...] = (p_ref[...].astype(F32) + r_ref[...].astype(F32)).astype(o_ref.dtype)

    return pl.pallas_call(
        body, name=name, out_shape=jax.ShapeDtypeStruct((N_CHIP, R, C), p.dtype),
        grid_spec=pltpu.PrefetchScalarGridSpec(
            num_scalar_prefetch=1, grid=(N_CHIP, R // tr),
            in_specs=[pl.BlockSpec((None, None, tr, C), lambda q, i, c_ref: (q, c_ref[0], i, 0)),
                      pl.BlockSpec((None, tr, C), lambda q, i, c_ref: (q, i, 0))],
            out_specs=pl.BlockSpec((None, tr, C), lambda q, i, c_ref: (q, i, 0))),
        compiler_params=_cparams(("parallel", "parallel")),
    )(lax.axis_index("c").reshape(1).astype(jnp.int32), p.reshape(N_CHIP, 2, R, C), r)


def _sum_slots(g, name):
    ns, R, C = g.shape
    tr = _pick(R, (256, 128, 64, 32, 16))

    def body(g_ref, o_ref):
        acc = g_ref[0].astype(F32)
        for j in range(1, ns):
            acc = acc + g_ref[j].astype(F32)
        o_ref[...] = acc

    return pl.pallas_call(
        body, name=name, grid=(R // tr,),
        in_specs=[pl.BlockSpec((ns, tr, C), lambda i: (0, i, 0))], out_specs=pl.BlockSpec((tr, C), lambda i: (i, 0)),
        out_shape=jax.ShapeDtypeStruct((R, C), F32), compiler_params=_cparams(("parallel",)),
    )(g)


def _pack_rows(arrs, dtype):
    parts = []
    for a in arrs:
        flat = a.reshape(-1).astype(dtype)
        pad = (-flat.shape[0]) % (PACK_W * 16)
        parts.append(jnp.pad(flat, (0, pad)).reshape(-1, PACK_W))
    out = jnp.concatenate(parts, axis=0)
    return jnp.pad(out, ((0, (-out.shape[0]) % PACK_ROWS), (0, 0)))


def _packed_rows(shape):
    n = math.prod(shape)
    return (n + PACK_W * 16 - 1) // (PACK_W * 16) * 16


def _unpack_rows(packed, shapes):
    out, r0 = [], 0
    for s in shapes:
        rows, n = _packed_rows(s), math.prod(s)
        out.append(packed[r0:r0 + rows].reshape(rows * PACK_W)[:n].reshape(s))
        r0 += rows
    return out


def _adamw_math(w, g, m, v):
    m = ADAM_B1 * m + (1.0 - ADAM_B1) * g
    v = ADAM_B2 * v + (1.0 - ADAM_B2) * (g * g)
    m_hat = m / (1.0 - ADAM_B1 ** ADAM_STEP)
    v_hat = v / (1.0 - ADAM_B2 ** ADAM_STEP)
    delta = -ADAM_LR * (m_hat / (jnp.sqrt(v_hat) + ADAM_EPS) + ADAM_WD * w)
    return delta, m, v


def _adamw(w, g, m, v, name):
    R, C = w.shape
    return _rowmap(_adamw_math, name, R, [w, g, m, v], [], [(C, F32)] * 3)


def _s5_disc_math(lr, li, ldt, br, bi):
    dt = jnp.exp(ldt)
    mag = jnp.exp(lr * dt)
    ab_re, ab_im = mag * jnp.cos(li * dt), mag * jnp.sin(li * dt)
    den = lr * lr + li * li
    nr, ni = ab_re - 1.0, ab_im
    co_re = (nr * lr + ni * li) / den
    co_im = (ni * lr - nr * li) / den
    bb_re = co_re * br - co_im * bi
    bb_im = co_re * bi + co_im * br
    return ab_re, ab_im, bb_re, bb_im


def _s5_tables(a_re, a_im, ldt, b_re, b_im, c_re, c_im):
    _, G, P, N = b_re.shape
    nch = G // 8

    def body(lr_ref, li_ref, ldt_ref, br_ref, bi_ref, cr_ref, ci_ref, wre, wim, vre, vim, pwr, pwi):
        ar, ai, bb_re, bb_im = _s5_disc_math(lr_ref[0], li_ref[0], ldt_ref[0], br_ref[0], bi_ref[0])
        cr, ci = cr_ref[0], ci_ref[0]
        pr, pi = jnp.ones_like(ar), jnp.zeros_like(ar)
        for j in range(SUB + 1):
            pwr[0, j], pwi[0, j] = pr, pi
            if j < SUB:
                tabs = ((wre, bb_re * pr - bb_im * pi), (wim, bb_re * pi + bb_im * pr),
                        (vre, cr * pr - ci * pi), (vim, -(cr * pi + ci * pr)))
                for ref, val in tabs:
                    for s in range(nch):
                        ref[0, s, pl.ds(j * LANE, LANE), :] = val[s * 8:(s + 1) * 8].reshape(LANE, N).astype(BF16)
            pr, pi = pr * ar - pi * ai, pr * ai + pi * ar

    g1n = pl.BlockSpec((1, G, 1, N), lambda d: (d, 0, 0, 0))
    gpn = pl.BlockSpec((1, G, P, N), lambda d: (d, 0, 0, 0))
    tab = pl.BlockSpec((1, nch, SUB * LANE, N), lambda d: (d, 0, 0, 0))
    pw = pl.BlockSpec((1, SUB + 1, G, 1, N), lambda d: (d, 0, 0, 0, 0))
    s_tab = jax.ShapeDtypeStruct((2, nch, SUB * LANE, N), BF16)
    s_pw = jax.ShapeDtypeStruct((2, SUB + 1, G, 1, N), F32)
    return pl.pallas_call(
        body, name="s5_tables", grid=(2,),
        in_specs=[g1n, g1n, pl.BlockSpec((1, G, 1, 1), lambda d: (d, 0, 0, 0)), gpn, gpn, gpn, gpn],
        out_specs=[tab] * 4 + [pw] * 2, out_shape=[s_tab] * 4 + [s_pw] * 2,
        compiler_params=_cparams(("parallel",)),
    )(a_re, a_im, ldt, b_re, b_im, c_re, c_im)


def _s5_expand(t_re, t_im, name):
    _, nch, R, N = t_re.shape
    sw = 8 * N

    def body(re_ref, im_ref, o_ref):
        spread = (lax.broadcasted_iota(jnp.int32, (N, sw), 1) % N == lax.broadcasted_iota(jnp.int32, (N, sw), 0)).astype(BF16)
        row_g = (lax.broadcasted_iota(jnp.int32, (R, sw), 0) % LANE) // S5_GROUP
        keep = row_g == lax.broadcasted_iota(jnp.int32, (R, sw), 1) // N
        for half, ref in enumerate((re_ref, im_ref)):
            t = jnp.dot(ref[0, 0], spread, preferred_element_type=F32)
            o_ref[0, 0, :, pl.ds(half * sw, sw)] = jnp.where(keep, t, 0.0).astype(BF16)

    spec = pl.BlockSpec((1, 1, R, N), lambda d, s: (d, s, 0, 0))
    return pl.pallas_call(
        body, name=name, grid=(2, nch), in_specs=[spec, spec],
        out_specs=pl.BlockSpec((1, 1, R, 2 * sw), lambda d, s: (d, s, 0, 0)),
        out_shape=jax.ShapeDtypeStruct((2, nch, R, 2 * sw), BF16), compiler_params=_cparams(("parallel", "parallel")),
    )(t_re, t_im)


def _s5_param_bwd(a_re, a_im, ldt, b_re, b_im, da_re, da_im, dbb_re, dbb_im):
    _, G, P, N = b_re.shape

    def body(lr_ref, li_ref, ldt_ref, br_ref, bi_ref, dar, dai, dbr, dbi, o_lr, o_li, o_ldt, o_br, o_bi):
        _, vjp = jax.vjp(_s5_disc_math, lr_ref[0], li_ref[0], ldt_ref[0], br_ref[0], bi_ref[0])
        o_lr[0], o_li[0], o_ldt[0], o_br[0], o_bi[0] = vjp((dar[0], dai[0], dbr[0], dbi[0]))

    g1n = pl.BlockSpec((1, G, 1, N), lambda d: (d, 0, 0, 0))
    g11 = pl.BlockSpec((1, G, 1, 1), lambda d: (d, 0, 0, 0))
    gpn = pl.BlockSpec((1, G, P, N), lambda d: (d, 0, 0, 0))
    s_g1n, s_g11, s_gpn = (jax.ShapeDtypeStruct(s, F32) for s in ((2, G, 1, N), (2, G, 1, 1), (2, G, P, N)))
    return pl.pallas_call(
        body, name="s5_param_bwd", grid=(2,),
        in_specs=[g1n, g1n, g11, gpn, gpn, g1n, g1n, gpn, gpn], out_specs=[g1n, g1n, g11, gpn, gpn],
        out_shape=[s_g1n, s_g1n, s_g11, s_gpn, s_gpn], compiler_params=_cparams(("parallel",)),
    )(a_re, a_im, ldt, b_re, b_im, da_re, da_im, dbb_re, dbb_im)


def _shift_stack(u, back):
    tb = u.shape[0]
    tau = lax.broadcasted_iota(jnp.int32, u.shape, 0) % SUB
    parts = [u]
    for j in range(1, SUB):
        if back:
            parts.append(jnp.where(tau >= j, pltpu.roll(u, j, 0), 0.0))
        else:
            parts.append(jnp.where(tau <= SUB - 1 - j, pltpu.roll(u, tb - j, 0), 0.0))
    return jnp.concatenate(parts, axis=1).astype(BF16)


def _cmul_add(tile, pw, carry, sw):
    pr, pi, cr, ci = pw[:, :sw], pw[:, sw:], carry[:, :sw], carry[:, sw:]
    return tile + jnp.concatenate([pr * cr - pi * ci, pr * ci + pi * cr], axis=1)


def _tile_scan(buf, base, ntile, pw, carry, sw, causal):
    def step(k, c):
        i = k if causal else ntile - 1 - k
        r = pl.multiple_of(base + i * SUB, SUB)
        tile = _cmul_add(buf[pl.ds(r, SUB), :], pw, c, sw)
        buf[pl.ds(r, SUB), :] = tile
        return tile[SUB - 1:SUB, :] if causal else tile[0:1, :]

    return lax.fori_loop(0, ntile, step, carry)


def _s5_fwd(h_all, waug, vaug, pw, S5W, T, causal, name):
    S = h_all.shape[0]
    nch, _, sw2 = waug.shape
    sw = sw2 // 2
    tb = _pick(math.gcd(T, S - T), (256, 128, 64, 32, 16))
    ntile, nt, off = tb // SUB, S // tb, T // tb
    rb = (lambda s, t: ((t + off) % nt, s)) if causal else (lambda s, t: (nt - 1 - t, s))

    def body(u_ref, w_ref, v_ref, p_ref, y_ref, h_ref, hblk, carry):
        @pl.when(pl.program_id(1) == 0)
        def _():
            carry[...] = jnp.zeros_like(carry)

        hblk[...] = jnp.dot(_shift_stack(u_ref[...], causal), w_ref[...], preferred_element_type=F32)
        carry[...] = _tile_scan(hblk, 0, ntile, p_ref[...], carry[...], sw, causal)
        hb = hblk[...].astype(BF16)
        h_ref[...] = hb
        y_ref[...] = lax.dot_general(hb, v_ref[...], _NT, preferred_element_type=F32)

    return pl.pallas_call(
        body, name=name, grid=(nch, nt),
        in_specs=[pl.BlockSpec((tb, LANE), rb),
                  pl.BlockSpec((None, SUB * LANE, sw2), lambda s, t: (s, 0, 0)),
                  pl.BlockSpec((None, LANE, sw2), lambda s, t: (s, 0, 0)),
                  pl.BlockSpec((None, SUB, sw2), lambda s, t: (s, 0, 0))],
        out_specs=[pl.BlockSpec((tb, LANE), rb), pl.BlockSpec((tb, sw2), rb)],
        out_shape=[jax.ShapeDtypeStruct((S, S5W), F32), jax.ShapeDtypeStruct((S, nch * sw2), BF16)],
        scratch_shapes=[pltpu.VMEM((tb, sw2), F32), pltpu.VMEM((1, sw2), F32)],
        compiler_params=_cparams(("parallel", "arbitrary")),
    )(h_all, waug, vaug, pw)


def _s5_bwd(dy_all, h_all, hs, waug, vaug, pwc, S5W, T, causal, name):
    S = h_all.shape[0]
    nch, _, sw2 = waug.shape
    sw = sw2 // 2
    tb = _pick(math.gcd(T, S - T), (256, 128, 64, 32, 16))
    ntile, nt, off = tb // SUB, S // tb, T // tb
    rb = (lambda s, t: ((nt - 1 - t + off) % nt, s)) if causal else (lambda s, t: (t, s))
    adj_causal = not causal
    edge = SUB - 1 if adj_causal else SUB + tb
    keep_src, keep_dst = (tb, 0) if adj_causal else (SUB, SUB + tb)

    def body(dy_ref, u_ref, h_ref, w_ref, v_ref, p_ref, du_ref, dbb_ref, dc_ref, da_ref, lam):
        @pl.when(pl.program_id(1) == 0)
        def _():
            lam[pl.ds(0, SUB), :] = jnp.zeros((SUB, sw2), F32)
            lam[pl.ds(SUB + tb, SUB), :] = jnp.zeros((SUB, sw2), F32)
            dbb_ref[...] = jnp.zeros_like(dbb_ref)
            dc_ref[...] = jnp.zeros_like(dc_ref)
            da_ref[...] = jnp.zeros_like(da_ref)

        dy = dy_ref[...]
        lam[pl.ds(SUB, tb), :] = jnp.dot(_shift_stack(dy, adj_causal), v_ref[...], preferred_element_type=F32)
        _tile_scan(lam, SUB, ntile, p_ref[...], lam[pl.ds(edge, 1), :], sw, adj_causal)
        lb = lam[pl.ds(SUB, tb), :].astype(BF16)
        du_ref[...] = lax.dot_general(lb, w_ref[...], _NT, preferred_element_type=F32)
        dbb_ref[...] += lax.dot_general(u_ref[...].astype(BF16), lb, _TN, preferred_element_type=F32)
        dc_ref[...] += lax.dot_general(h_ref[...], dy.astype(BF16), _TN, preferred_element_type=F32)
        h = h_ref[...].astype(F32)
        ln = lam[pl.ds(SUB + 1 if causal else SUB - 1, tb), :]
        hr, hi, lr, li = h[:, :sw], h[:, sw:], ln[:, :sw], ln[:, sw:]
        da_ref[...] += jnp.concatenate([jnp.sum(hr * lr + hi * li, axis=0, keepdims=True),
                                        jnp.sum(hr * li - hi * lr, axis=0, keepdims=True)], axis=1)
        lam[pl.ds(keep_dst, SUB), :] = lam[pl.ds(keep_src, SUB), :]

    fixed = lambda s, t: (s, 0, 0)
    return pl.pallas_call(
        body, name=name, grid=(nch, nt),
        in_specs=[pl.BlockSpec((tb, LANE), rb), pl.BlockSpec((tb, LANE), rb), pl.BlockSpec((tb, sw2), rb),
                  pl.BlockSpec((None, LANE, sw2), fixed), pl.BlockSpec((None, SUB * LANE, sw2), fixed),
                  pl.BlockSpec((None, SUB, sw2), fixed)],
        out_specs=[pl.BlockSpec((tb, LANE), rb), pl.BlockSpec((None, LANE, sw2), fixed),
                   pl.BlockSpec((None, sw2, LANE), fixed), pl.BlockSpec((None, 1, sw2), fixed)],
        out_shape=[jax.ShapeDtypeStruct((S, S5W), F32), jax.ShapeDtypeStruct((nch, LANE, sw2), F32),
                   jax.ShapeDtypeStruct((nch, sw2, LANE), F32), jax.ShapeDtypeStruct((nch, 1, sw2), F32)],
        scratch_shapes=[pltpu.VMEM((tb + 2 * SUB, sw2), F32)],
        compiler_params=_cparams(("parallel", "arbitrary")),
    )(dy_all, h_all, hs, waug, vaug, pwc)


def _attn_fwd(q3, k3, v3, scale):
    H, T, dk = q3.shape
    S, dv = k3.shape[1], v3.shape[2]
    tq = _pick(T, (256, 128, 64, 32, 16))

    def body(q_ref, k_ref, v_ref, o_ref, lse_ref):
        s = lax.dot_general(q_ref[0], k_ref[0], _NT, preferred_element_type=F32) * scale
        m = jnp.max(s, axis=1, keepdims=True)
        p = jnp.exp(s - m)
        l = jnp.sum(p, axis=1, keepdims=True)
        o_ref[...] = jnp.dot((p / l).astype(BF16), v_ref[0], preferred_element_type=F32).astype(o_ref.dtype)
        lse_ref[0] = m + jnp.log(l)

    return pl.pallas_call(
        body, name="attn_fwd", grid=(H, T // tq),
        in_specs=[pl.BlockSpec((1, tq, dk), lambda h, i: (h, i, 0)), pl.BlockSpec((1, S, dk), lambda h, i: (h, 0, 0)),
                  pl.BlockSpec((1, S, dv), lambda h, i: (h, 0, 0))],
        out_specs=[pl.BlockSpec((tq, dv), lambda h, i: (i, h)), pl.BlockSpec((1, tq, 1), lambda h, i: (h, i, 0))],
        out_shape=[jax.ShapeDtypeStruct((T, H * dv), BF16), jax.ShapeDtypeStruct((H, T, 1), F32)],
        compiler_params=_cparams(("parallel", "parallel")),
    )(q3, k3, v3)


def _attn_bwd(q3, k3, v3, do, lse, scale):
    H, T, dk = q3.shape
    S, dv = k3.shape[1], v3.shape[2]
    tq = _pick(T, (256, 128, 64, 32, 16))

    def body(q_ref, k_ref, v_ref, do_ref, lse_ref, dq_ref, dk_ref, dv_ref):
        @pl.when(pl.program_id(1) == 0)
        def _():
            dk_ref[...] = jnp.zeros_like(dk_ref)
            dv_ref[...] = jnp.zeros_like(dv_ref)

        q, k, v, d_o = q_ref[0], k_ref[0], v_ref[0], do_ref[...]
        s = lax.dot_general(q, k, _NT, preferred_element_type=F32) * scale
        p = jnp.exp(s - lse_ref[0])
        dv_ref[0] += lax.dot_general(p.astype(BF16), d_o, _TN, preferred_element_type=F32)
        dp = lax.dot_general(d_o, v, _NT, preferred_element_type=F32)
        ds = (p * (dp - jnp.sum(p * dp, axis=1, keepdims=True)) * scale).astype(BF16)
        dq_ref[0] = jnp.dot(ds, k, preferred_element_type=F32)
        dk_ref[0] += lax.dot_general(ds, q, _TN, preferred_element_type=F32)

    return pl.pallas_call(
        body, name="attn_bwd", grid=(H, T // tq),
        in_specs=[pl.BlockSpec((1, tq, dk), lambda h, i: (h, i, 0)), pl.BlockSpec((1, S, dk), lambda h, i: (h, 0, 0)),
                  pl.BlockSpec((1, S, dv), lambda h, i: (h, 0, 0)), pl.BlockSpec((tq, dv), lambda h, i: (i, h)),
                  pl.BlockSpec((1, tq, 1), lambda h, i: (h, i, 0))],
        out_specs=[pl.BlockSpec((1, tq, dk), lambda h, i: (h, i, 0)), pl.BlockSpec((1, S, dk), lambda h, i: (h, 0, 0)),
                   pl.BlockSpec((1, S, dv), lambda h, i: (h, 0, 0))],
        out_shape=[jax.ShapeDtypeStruct((H, T, dk), F32), jax.ShapeDtypeStruct((H, S, dk), F32),
                   jax.ShapeDtypeStruct((H, S, dv), F32)],
        compiler_params=_cparams(("parallel", "arbitrary")),
    )(q3, k3, v3, do, lse)


def _rope_tables(T, heads):
    rows = T // GRID_W
    row = jnp.repeat(jnp.arange(rows, dtype=F32), GRID_W)
    col = jnp.tile(jnp.arange(GRID_W, dtype=F32), rows)
    n_freq = QK_ROPE // 4
    inv = ROPE_BASE ** (-jnp.arange(n_freq, dtype=F32) / n_freq)
    ar, ac = row[:, None] * inv, col[:, None] * inv
    cos = jnp.concatenate([jnp.cos(ar), jnp.cos(ar), jnp.cos(ac), jnp.cos(ac)], axis=1)
    sin = jnp.concatenate([-jnp.sin(ar), jnp.sin(ar), -jnp.sin(ac), jnp.sin(ac)], axis=1)
    return jnp.tile(cos, (1, heads)), jnp.tile(sin, (1, heads))


def _dw(a, dy, w, name):
    return _mm(a, dy, "tn", BF16, name, out_slots=w.shape[0] if w.ndim == 3 else None)


def _local_step(x, ctx, tgt, m_lat, m_ctx, p, W, goff):
    T, D = x.shape
    Tc = ctx.shape[0]
    S = T + Tc
    S5W = p["s5_d"].shape[1]
    QR, KVR = p["q_norm"].shape[1], p["kv_norm"].shape[1]
    H = W["w_uq"].shape[1] // (QK_NOPE + QK_ROPE)
    FF = W["w_ffn_out"].shape[0]
    G, N = p["s5_a_re"].shape[1:]
    P = S5_GROUP
    nch = G // 8
    o_cq, o_ckv, o_kr = S5W, S5W + QR, S5W + QR + KVR
    assert o_cq % QR == 0 and o_ckv % KVR == 0 and o_kr % LANE == 0 and goff % D == 0 and S5W % LANE == 0 and G % 8 == 0
    assert 8 * P == LANE and FF % LANE == 0
    row = lambda k, m: m[k:k + 1]
    sh1, sc1, g1, sh2, sc2, g2 = (row(k, m_lat) for k in range(6))
    csh1, csc1 = row(0, m_ctx), row(1, m_ctx)
    n1, n2, nf = p["norm1"], p["norm2"], p["norm_f"]

    (xm_lat,) = _rowmap(_normmod, "norm1_lat", T, [x], [n1, sc1, sh1], [(D, BF16)])
    (xm_ctx,) = _rowmap(_normmod, "norm1_ctx", Tc, [ctx], [n1, csc1, csh1], [(D, BF16)])
    xm_all = jnp.concatenate([xm_lat, xm_ctx], axis=0)
    h_all = _mm(xm_all, W["w_in"], "nn", F32, "mm_in")

    a_re, a_im = p["s5_a_re"][:, :, None, :], p["s5_a_im"][:, :, None, :]
    ldt = p["s5_log_dt"][:, :, None, None]
    b_re, b_im = p["s5_b_re"].transpose(0, 1, 3, 2), p["s5_b_im"].transpose(0, 1, 3, 2)
    wre, wim, vre, vim, pwr, pwi = _s5_tables(a_re, a_im, ldt, b_re, b_im, p["s5_c_re"], p["s5_c_im"])
    waug = _s5_expand(wre, wim, "s5_expand_b")
    vaug = _s5_expand(vre, vim, "s5_expand_c")
    lanes = lambda t: t.reshape(2, SUB + 1, nch, 8 * N).transpose(0, 2, 1, 3)
    pw_re, pw_im = lanes(pwr), lanes(pwi)
    near = lambda t: t[:, :, 1:]
    far = lambda t: t[:, :, :0:-1]
    pw_c = jnp.concatenate([near(pw_re), near(pw_im)], axis=-1)
    pw_a = jnp.concatenate([far(pw_re), far(pw_im)], axis=-1)
    pwc_c = jnp.concatenate([near(pw_re), -near(pw_im)], axis=-1)
    pwc_a = jnp.concatenate([far(pw_re), -far(pw_im)], axis=-1)
    y0, hs0 = _s5_fwd(h_all, waug[0], vaug[0], pw_c[0], S5W, T, True, "s5_scan_fwd0")
    y1, hs1 = _s5_fwd(h_all, waug[1], vaug[1], pw_a[1], S5W, T, False, "s5_scan_fwd1")

    def s5_combine(u, yf, yr, dskip):
        y5 = dskip * u + yf + yr
        return y5, jax.nn.gelu(y5)

    y5, z = _rowmap(s5_combine, "s5_combine", T, [(h_all, S5W, 0), y0, y1], [p["s5_d"]], [(S5W, F32), (S5W, BF16)])

    (qn,) = _rowmap(_rms, "q_norm", T, [(h_all, QR, o_cq // QR)], [p["q_norm"]], [(QR, BF16)])
    (kvn,) = _rowmap(_rms, "kv_norm", S, [(h_all, KVR, o_ckv // KVR)], [p["kv_norm"]], [(KVR, BF16)])
    qraw = _mm(qn, W["w_uq"], "nn", F32, "mm_uq")
    kvraw = _mm(kvn, W["w_ukv"], "nn", BF16, "mm_ukv")
    cos_q, sin_q = _rope_tables(T, H)
    padl = lambda t: jnp.pad(t[:, :QK_ROPE], ((0, Tc), (0, LANE - QK_ROPE)))
    cos_k = padl(cos_q) + jnp.pad(jnp.ones((Tc, LANE), F32), ((T, 0), (0, 0)))
    sin_k = padl(sin_q)
    hn = H * QK_NOPE

    def q_post(q, cos, sin):
        return q[:, :hn], _rope(q[:, hn:], cos, sin)

    q_nope, q_rope = _rowmap(q_post, "q_rope", T, [qraw, cos_q, sin_q], [], [(hn, BF16), (H * QK_ROPE, BF16)])
    (kr,) = _rowmap(_rope, "k_rope", S, [(h_all, LANE, o_kr // LANE), cos_k, sin_k], [], [(LANE, BF16)])
    q3 = jnp.concatenate([q_nope.reshape(T, H, QK_NOPE), q_rope.reshape(T, H, QK_ROPE)], axis=-1).transpose(1, 0, 2)
    k3 = jnp.concatenate([kvraw[:, :hn].reshape(S, H, QK_NOPE),
                          jnp.broadcast_to(kr[:, None, :QK_ROPE], (S, H, QK_ROPE))], axis=-1).transpose(1, 0, 2)
    v3 = kvraw[:, hn:].reshape(S, H, V_DIM).transpose(1, 0, 2)
    scale = (QK_NOPE + QK_ROPE) ** -0.5
    o, lse = _attn_fwd(q3, k3, v3, scale)

    zz = _mm(z, W["w_glu"], "nn", F32, "mm_glu")
    br_mla = _mm(o, W["w_mla_o"], "nn", F32, "mm_mla_o")

    def merge(zz, brm, gs, gm):
        a, b = zz[:, :D], zz[:, D:]
        return jax.nn.sigmoid(gs) * (a * jax.nn.sigmoid(b)) + jax.nn.sigmoid(gm) * brm

    gb = goff // D
    merge_ins = [zz, br_mla, (h_all, D, gb), (h_all, D, gb + 1)]
    (mix,) = _rowmap(merge, "merge", T, merge_ins, [], [(D, BF16)])
    out1 = _mm(mix, W["w_out"], "nn", F32, "mm_out")

    def resid_norm2(x, out1, g1, n2, sc2, sh2):
        x1 = x + g1 * out1
        return x1, _normmod(x1, n2, sc2, sh2)

    x1, hm = _rowmap(resid_norm2, "resid_norm2", T, [x, out1], [g1, n2, sc2, sh2], [(D, F32), (D, BF16)])

    ab = _mm(hm, W["w_ffn_in"], "nn", F32, "mm_ffn_in")

    def swiglu_act(a, b):
        return jax.nn.silu(a) * b

    (f,) = _rowmap(swiglu_act, "ffn_act", T, [(ab, FF, 0), (ab, FF, 1)], [], [(FF, BF16)])
    out2 = _mm(f, W["w_ffn_out"], "nn", F32, "mm_ffn_out")

    def loss_rows(x1, out2, g2, nf, tgt):
        y = _rms(x1 + g2 * out2, nf)
        return 0.5 * jnp.sum(jnp.mean(jnp.square(y - tgt), axis=-1))

    def final(x1, out2, tgt, g2, nf):
        val, (dx1, dout2, dg2, dnf) = jax.value_and_grad(loss_rows, argnums=(0, 1, 2, 3))(x1, out2, g2, nf, tgt)
        return dx1, dout2, jnp.full((1, LANE), val, F32), dg2, dnf

    dx2, dout2, loss_acc, dg2, dnf = _rowmap(final, "final_loss", T, [x1, out2, tgt], [g2, nf],
                                             [(D, F32), (D, BF16)], [LANE, D, D])

    gW = {}
    df = _mm(dout2, W["w_ffn_out"], "nt", F32, "mm_ffn_out_dx")
    gW["w_ffn_out"] = _dw(f, dout2, W["w_ffn_out"], "mm_ffn_out_dw")

    def swiglu_bwd(a, b, df):
        _, vjp = jax.vjp(swiglu_act, a, b)
        da, db = vjp(df)
        return jnp.concatenate([da, db], axis=1)

    (dab,) = _rowmap(swiglu_bwd, "ffn_act_bwd", T, [(ab, FF, 0), (ab, FF, 1), df], [], [(2 * FF, BF16)])
    dhm = _mm(dab, W["w_ffn_in"], "nt", F32, "mm_ffn_in_dx")
    gW["w_ffn_in"] = _dw(hm, dab, W["w_ffn_in"], "mm_ffn_in_dw")

    def resid_norm2_bwd(x, out1, dx2, dhm, g1, n2, sc2, sh2):
        _, vjp = jax.vjp(resid_norm2, x, out1, g1, n2, sc2, sh2)
        dx, dout1, dg1, dn2, dsc2, dsh2 = vjp((dx2, dhm))
        return dx, dout1, dg1, dn2, dsc2, dsh2

    dx1, dout1, dg1, dn2, dsc2, dsh2 = _rowmap(resid_norm2_bwd, "resid_norm2_bwd", T, [x, out1, dx2, dhm],
                                               [g1, n2, sc2, sh2], [(D, F32), (D, BF16)], [D, D, D, D])

    dmix = _mm(dout1, W["w_out"], "nt", F32, "mm_out_dx")
    gW["w_out"] = _dw(mix, dout1, W["w_out"], "mm_out_dw")

    def merge_bwd(zz, brm, gs, gm, dmix):
        _, vjp = jax.vjp(merge, zz, brm, gs, gm)
        dzz, dbrm, dgs, dgm = vjp(dmix)
        return dzz, dbrm, jnp.concatenate([dgs, dgm], axis=1)

    dzz, dbrm, dgates = _rowmap(merge_bwd, "merge_bwd", T, merge_ins + [dmix], [],
                                [(2 * D, BF16), (D, BF16), (2 * D, BF16)])
    do = _mm(dbrm, W["w_mla_o"], "nt", BF16, "mm_mla_o_dx")
    gW["w_mla_o"] = _dw(o, dbrm, W["w_mla_o"], "mm_mla_o_dw")
    dz = _mm(dzz, W["w_glu"], "nt", F32, "mm_glu_dx")
    gW["w_glu"] = _dw(z, dzz, W["w_glu"], "mm_glu_dw")

    def s5_combine_bwd(u, y5, dz, dskip):
        _, vjp = jax.vjp(lambda y: jax.nn.gelu(y), y5)
        (dy5,) = vjp(dz)
        return dy5, jnp.sum(dy5 * u, axis=0, keepdims=True)

    dy5, d_skip = _rowmap(s5_combine_bwd, "s5_combine_bwd", T, [(h_all, S5W, 0), y5, dz], [p["s5_d"]], [(S5W, F32)], [S5W])

    dq3, dk3, dv3 = _attn_bwd(q3, k3, v3, do, lse, scale)
    dq_t = dq3.transpose(1, 0, 2)
    dq_cat = jnp.concatenate([dq_t[:, :, :QK_NOPE].reshape(T, hn), dq_t[:, :, QK_NOPE:].reshape(T, H * QK_ROPE)], axis=1)

    def q_post_bwd(dq, cos, sin):
        return jnp.concatenate([dq[:, :hn], _rope_bwd(dq[:, hn:], cos, sin)], axis=1)

    (dqraw,) = _rowmap(q_post_bwd, "q_rope_bwd", T, [dq_cat, cos_q, sin_q], [], [(H * (QK_NOPE + QK_ROPE), BF16)])
    dk_t = dk3.transpose(1, 0, 2)
    dkvraw = jnp.concatenate([dk_t[:, :, :QK_NOPE].reshape(S, hn), dv3.transpose(1, 0, 2).reshape(S, H * V_DIM)], axis=1)
    dkr_heads = jnp.pad(dk_t[:, :, QK_NOPE:], ((0, 0), (0, 0), (0, LANE - QK_ROPE))).reshape(S, H * LANE)

    def k_rope_bwd(dkh, cos, sin):
        d = dkh[:, :LANE]
        for h in range(1, H):
            d = d + dkh[:, h * LANE:(h + 1) * LANE]
        return _rope_bwd(d, cos, sin)

    (dkr,) = _rowmap(k_rope_bwd, "k_rope_bwd", S, [dkr_heads, cos_k, sin_k], [], [(LANE, BF16)])
    dqn = _mm(dqraw, W["w_uq"], "nt", F32, "mm_uq_dx")
    gW["w_uq"] = _dw(qn, dqraw, W["w_uq"], "mm_uq_dw")
    dkvn = _mm(dkvraw, W["w_ukv"], "nt", F32, "mm_ukv_dx")
    gW["w_ukv"] = _dw(kvn, dkvraw, W["w_ukv"], "mm_ukv_dw")

    def rms_bwd(cx, dn, g):
        _, vjp = jax.vjp(_rms, cx, g)
        return vjp(dn)

    dcq, dq_norm = _rowmap(rms_bwd, "q_norm_bwd", T, [(h_all, QR, o_cq // QR), dqn], [p["q_norm"]], [(QR, BF16)], [QR])
    dckv, dkv_norm = _rowmap(rms_bwd, "kv_norm_bwd", S, [(h_all, KVR, o_ckv // KVR), dkvn], [p["kv_norm"]],
                             [(KVR, BF16)], [KVR])

    dy_all = jnp.concatenate([dy5, jnp.zeros((Tc, S5W), F32)], axis=0)
    du0, dbb0, dc0, da0 = _s5_bwd(dy_all, h_all, hs0, waug[0], vaug[0], pwc_a[0], S5W, T, True, "s5_scan_bwd0")
    du1, dbb1, dc1, da1 = _s5_bwd(dy_all, h_all, hs1, waug[1], vaug[1], pwc_c[1], S5W, T, False, "s5_scan_bwd1")

    def du_combine(a, b, dy, dskip):
        return a + b + dskip * dy

    (du_all,) = _rowmap(du_combine, "s5_du", S, [du0, du1, dy_all], [p["s5_d"]], [(S5W, BF16)])
    dbb = jnp.einsum("dsgpcgn->dcsgpn", jnp.stack([dbb0, dbb1]).reshape(2, nch, 8, P, 2, 8, N)).reshape(2, 2, G, P, N)
    dcm = jnp.einsum("dscgngp->dcsgpn", jnp.stack([dc0, dc1]).reshape(2, nch, 2, 8, N, 8, P)).reshape(2, 2, G, P, N)
    da = jnp.stack([da0, da1]).reshape(2, nch, 2, 8, N).transpose(0, 2, 1, 3, 4).reshape(2, 2, G, 1, N)
    d_lr, d_li, d_ldt, d_br, d_bi = _s5_param_bwd(a_re, a_im, ldt, b_re, b_im, da[:, 0], da[:, 1], dbb[:, 0], dbb[:, 1])

    lat_only = lambda t: jnp.pad(t, ((0, Tc), (0, 0)))
    dh_all = jnp.concatenate([du_all, lat_only(dcq), dckv, dkr, jnp.zeros((S, goff - o_kr - LANE), BF16), lat_only(dgates)],
                             axis=1)
    dxm = _mm(dh_all, W["w_in"], "nt", F32, "mm_in_dx")
    gW["w_in"] = _dw(xm_all, dh_all, W["w_in"], "mm_in_dw")

    def norm1_bwd(x, dxm, dx1, n1, sc, sh):
        _, vjp = jax.vjp(_normmod, x, n1, sc, sh)
        dx, dn, dsc, dsh = vjp(dxm)
        return dx + dx1, dn, dsc, dsh

    grad_x, dn1_l, dsc1, dsh1 = _rowmap(norm1_bwd, "norm1_lat_bwd", T, [x, dxm, dx1], [n1, sc1, sh1], [(D, F32)], [D, D, D])

    def norm1_ctx_bwd(x, dxm, n1, sc, sh):
        _, vjp = jax.vjp(_normmod, x, n1, sc, sh)
        return vjp(dxm)[1:]

    dn1_c, dcsc1, dcsh1 = _rowmap(norm1_ctx_bwd, "norm1_ctx_bwd", Tc, [ctx, dxm[T:]], [n1, csc1, csh1], [], [D, D, D])

    zero = jnp.zeros((1, D), F32)
    dm_lat = jnp.concatenate([dsh1, dsc1, dg1, dsh2, dsc2, dg2], axis=0)
    dm_ctx = jnp.concatenate([dcsh1, dcsc1, zero, zero, zero, zero], axis=0)
    small = {
        "norm1": dn1_l + dn1_c, "norm2": dn2, "norm_f": dnf, "q_norm": dq_norm, "kv_norm": dkv_norm, "s5_d": d_skip,
        "s5_a_re": d_lr, "s5_a_im": d_li, "s5_log_dt": d_ldt, "s5_b_re": d_br.transpose(0, 1, 3, 2),
        "s5_b_im": d_bi.transpose(0, 1, 3, 2), "s5_c_re": dcm[:, 0], "s5_c_im": -dcm[:, 1],
    }
    return loss_acc[:, :1], grad_x, small, dm_lat, dm_ctx, gW


BIG = ("w_in", "w_uq", "w_ukv", "w_glu", "w_mla_o", "w_out", "w_ffn_in", "w_ffn_out")
ROW_SHARDED = ("w_out", "w_ffn_out")
RELAID = ("w_in", "w_uq", "w_ukv")
SMALL = ("c_ctx", "b_mod", "norm1", "norm2", "s5_a_re", "s5_a_im", "s5_log_dt", "s5_b_re", "s5_b_im", "s5_c_re",
         "s5_c_im", "s5_d", "q_norm", "kv_norm", "norm_f")
WEIGHTS = ("c_ctx", "w_mod", "b_mod", "norm1", "norm2", "w_in", "s5_a_re", "s5_a_im", "s5_log_dt", "s5_b_re", "s5_b_im",
           "s5_c_re", "s5_c_im", "s5_d", "w_glu", "q_norm", "kv_norm", "w_uq", "w_ukv", "w_mla_o", "w_out", "w_ffn_in",
           "w_ffn_out", "norm_f")


def _heads_split(w, heads, first):
    k = w.shape[0]
    w3 = w.reshape(k, heads, -1)
    return jnp.concatenate([w3[:, :, :first].reshape(k, -1), w3[:, :, first:].reshape(k, -1)], axis=1)


def _heads_merge(w, heads, first):
    k = w.shape[0]
    a, b = w[:, :heads * first].reshape(k, heads, first), w[:, heads * first:].reshape(k, heads, -1)
    return jnp.concatenate([a, b], axis=2).reshape(k, -1)


def _cols_full(w8):
    return w8.transpose(1, 0, 2).reshape(w8.shape[1], -1)


def _cols_slots(w):
    return w.reshape(w.shape[0], N_DEV, -1).transpose(1, 0, 2)


def _model_weights(g8, D):
    W = {}
    for n, w8 in g8.items():
        if n in ROW_SHARDED:
            W[n] = w8.reshape(-1, w8.shape[-1])
        elif n in RELAID or w8.shape[-1] % LANE:
            W[n] = _cols_full(w8)
        else:
            W[n] = w8
    w_in = W["w_in"]
    n_front = w_in.shape[1] - 2 * D
    goff = -(-n_front // D) * D
    W["w_in"] = jnp.concatenate([w_in[:, :n_front], jnp.zeros((D, goff - n_front), w_in.dtype), w_in[:, n_front:]], axis=1)
    heads = W["w_uq"].shape[1] // (QK_NOPE + QK_ROPE)
    W["w_uq"] = _heads_split(W["w_uq"], heads, QK_NOPE)
    W["w_ukv"] = _heads_split(W["w_ukv"], heads, QK_NOPE)
    return W, goff


def kernel(x, c, ctx, c_ctx, w_mod, b_mod, norm1, norm2, w_in, s5_a_re, s5_a_im, s5_log_dt, s5_b_re, s5_b_im, s5_c_re, s5_c_im, s5_d, w_glu, q_norm, kv_norm, w_uq, w_ukv, w_mla_o, w_out, w_ffn_in, w_ffn_out, norm_f, loss_target, m_c_ctx, m_w_mod, m_b_mod, m_norm1, m_norm2, m_w_in, m_s5_a_re, m_s5_a_im, m_s5_log_dt, m_s5_b_re, m_s5_b_im, m_s5_c_re, m_s5_c_im, m_s5_d, m_w_glu, m_q_norm, m_kv_norm, m_w_uq, m_w_ukv, m_w_mla_o, m_w_out, m_w_ffn_in, m_w_ffn_out, m_norm_f, v_c_ctx, v_w_mod, v_b_mod, v_norm1, v_norm2, v_w_in, v_s5_a_re, v_s5_a_im, v_s5_log_dt, v_s5_b_re, v_s5_b_im, v_s5_c_re, v_s5_c_im, v_s5_d, v_w_glu, v_q_norm, v_kv_norm, v_w_uq, v_w_ukv, v_w_mla_o, v_w_out, v_w_ffn_in, v_w_ffn_out, v_norm_f):
    a = dict(locals())
    D = x.shape[-1]
    me = 4 * lax.axis_index("x") + 2 * lax.axis_index("y") + lax.axis_index("c")

    shard = {n: a[n][0] for n in BIG}
    gathered = _all_gather([shard[n].astype(BF16) for n in BIG] + [jnp.broadcast_to(c, (8, D))], "ag_weights")
    W, goff = _model_weights(dict(zip(BIG, gathered[:-1])), D)
    cg = gathered[-1]

    wm = w_mod[0]
    ncol = wm.shape[1]
    c16 = jnp.concatenate([cg[:, 0, :], c_ctx[None], jnp.zeros((7, D), F32)], axis=0)
    (s16,) = _rowmap(jax.nn.silu, "mod_silu", 16, [c16], [], [(D, BF16)])
    m_cols = _mm(s16, wm, "nn", F32, "mm_mod")
    (mg,) = _all_gather([m_cols], "ag_mod")
    (m16,) = _rowmap(lambda m, b: m + b, "mod_bias", 16, [_cols_full(mg)], [b_mod], [(N_DEV * ncol, F32)])
    m_lat = lax.dynamic_slice(m16, (me, 0), (1, 6 * D)).reshape(6, D)
    m_ctx = m16[8].reshape(6, D)

    p = {n: a[n][0] for n in ("norm1", "norm2", "s5_a_re", "s5_a_im", "s5_log_dt", "s5_b_re", "s5_b_im", "s5_c_re",
                              "s5_c_im", "q_norm", "kv_norm")}
    p = {k: (v[None] if v.ndim == 1 else v) for k, v in p.items()}
    p["s5_d"] = s5_d.reshape(1, -1)
    p["norm_f"] = norm_f[None]
    loss_part, grad_x, small, dm_lat, dm_ctx, gW = _local_step(x[0], ctx[0], loss_target[0], m_lat, m_ctx, p, W, goff)
    loss = lax.psum(loss_part[0, 0], ("x", "y", "c"))

    dm16 = jnp.concatenate([dm_lat.reshape(1, -1), dm_ctx.reshape(1, -1), jnp.zeros((14, 6 * D), F32)], axis=0)
    (dmg,) = _all_gather([dm16], "ag_dmod")
    dm_sum = _sum_slots(dmg, "sum_dmod")
    dM16 = jnp.concatenate([dmg[:, 0, :], dm_sum[1:2], jnp.zeros((7, 6 * D), F32)], axis=0)
    (g_b_mod,) = _rowmap(lambda d: jnp.sum(d, axis=0, keepdims=True), "b_mod_grad", 16, [dM16], [], [], [6 * D])
    dM_loc = lax.dynamic_slice(dM16, (0, me * ncol), (16, ncol))
    g_w_mod = _mm(s16, dM_loc, "tn", F32, "mm_mod_dw")
    ds16_part = _mm(dM_loc, wm, "nt", F32, "mm_mod_dx")

    small_names = [n for n in SMALL if n not in ("c_ctx", "b_mod")]
    small_shapes = [small[n].shape for n in small_names] + [(1, D)]
    (sg,) = _all_gather([_pack_rows([small[n] for n in small_names] + [ds16_part[8:9]], F32)], "ag_small")
    parts = _unpack_rows(_sum_slots(sg, "sum_small"), small_shapes)
    grads = dict(zip(small_names, parts[:-1]))

    def silu_bwd(cc, ds):
        _, vjp = jax.vjp(jax.nn.silu, cc)
        return vjp(ds)[0]

    (g_c_ctx,) = _rowmap(silu_bwd, "c_ctx_grad", 1, [c_ctx[None], parts[-1]], [], [(D, F32)])
    grads["c_ctx"], grads["b_mod"] = g_c_ctx, g_b_mod

    gW = dict(gW)
    n_front = w_in.shape[-1] * N_DEV - 2 * D
    gW["w_in"] = jnp.concatenate([gW["w_in"][:, :n_front], gW["w_in"][:, goff:]], axis=1)
    heads = gW["w_uq"].shape[1] // (QK_NOPE + QK_ROPE)
    gW["w_uq"] = _heads_merge(gW["w_uq"], heads, QK_NOPE)
    gW["w_ukv"] = _heads_merge(gW["w_ukv"], heads, QK_NOPE)
    slots = []
    for n in BIG:
        g = gW[n]
        if g.ndim == 2:
            g = g.reshape(N_DEV, g.shape[0] // N_DEV, g.shape[1]) if n in ROW_SHARDED else _cols_slots(g)
        slots.append(g)
    from_sibling = _rs_pair(slots, "rs_pair")
    chip_sums = [_add_pair(pp, rr, "rs_add_" + n) for n, pp, rr in zip(BIG, slots, from_sibling)]
    from_chips = _rs_chips(chip_sums, "rs_chips")
    for n, g4 in zip(BIG, from_chips):
        grads[n] = _sum_slots(g4, "rs_sum_" + n)
    grads["w_mod"] = g_w_mod

    out = {}
    for n in BIG + ("w_mod",):
        d, nm, nv = _adamw(a[n][0], grads[n], a["m_" + n][0], a["v_" + n][0], "adamw_" + n)
        for k, val in (("grad_", grads[n]), ("delta_", d), ("new_m_", nm), ("new_v_", nv)):
            out[k + n] = val.reshape(a[n].shape)
    packs = [_pack_rows([t[n] for n in SMALL], F32) for t in (
        {n: a[n] for n in SMALL}, {n: grads[n] for n in SMALL}, {n: a["m_" + n] for n in SMALL}, {n: a["v_" + n] for n in SMALL})]
    res = _adamw(*packs, "adamw_small")
    shapes = [a[n].shape for n in SMALL]
    for k, packed in (("grad_", packs[1]), ("delta_", res[0]), ("new_m_", res[1]), ("new_v_", res[2])):
        for n, val in zip(SMALL, _unpack_rows(packed, shapes)):
            out[k + n] = val
    return (loss, grad_x[None]) + tuple(out[k + n] for k in ("grad_", "delta_", "new_m_", "new_v_") for n in WEIGHTS)
```

```python
import functools
import math

import jax
import jax.numpy as jnp
from jax import lax
from jax.experimental import pallas as pl
from jax.experimental.pallas import tpu as pltpu

F32 = jnp.float32
BF16 = jnp.bfloat16

N_DEV = 8
N_CHIP = 4
EPS = 1e-6
GRID_W = 64
S5_GROUP = 16
QK_NOPE, QK_ROPE, V_DIM = 128, 64, 128
ROPE_BASE = 10000.0
ADAM_LR, ADAM_B1, ADAM_B2, ADAM_EPS, ADAM_WD, ADAM_STEP = 0.001, 0.9, 0.999, 1e-08, 0.01, 10

LANE = 128
SUB = 8
PACK_W = 1024
PACK_ROWS = 32
VMEM_LIMIT = 48 << 20
ROWMAP_TILE_BYTES = 10 << 20
MESH = pl.DeviceIdType.MESH
_NT = (((1,), (1,)), ((), ()))
_TN = (((0,), (0,)), ((), ()))


def _pick(dim, cands):
    for c in cands:
        if dim % c == 0:
            return c
    return dim


def _cparams(sem):
    return pltpu.CompilerParams(dimension_semantics=sem, vmem_limit_bytes=VMEM_LIMIT)


def _mm(a, b, dims, out_dtype, name, out_slots=None):
    a = a.astype(BF16)
    b = b.astype(BF16)
    b3 = b.ndim == 3
    if dims == "nn":
        (M, K), N = a.shape, (b.shape[0] * b.shape[2] if b3 else b.shape[1])
    elif dims == "nt":
        M, N = a.shape[0], b.shape[-2]
        K = b.shape[0] * b.shape[2] if b3 else b.shape[1]
    else:
        (K, M), N = a.shape, b.shape[1]
    unit_n = b.shape[2] if (b3 and dims == "nn") else (N // out_slots if out_slots else N)
    unit_k = b.shape[2] if (b3 and dims == "nt") else K
    tn = _pick(unit_n, (512, 256) + ((unit_n,) if unit_n <= 1536 else ()) + (128,))
    tk = _pick(unit_k, ((2048, 1536) if dims != "tn" else ()) + (1024, 768, 512, 256) + ((unit_k,) if unit_k <= 1536 else ())
               + (128, 64, 32, 16))
    tm = _pick(M, ((1024, 768) if tn <= 512 else ()) + (512, 256, 128, 64, 32, 16))
    nk, npt, kpt = K // tk, unit_n // tn, unit_k // tk
    if dims == "nn":
        a_spec = pl.BlockSpec((tm, tk), lambda i, j, k: (i, k))
        b_spec = (pl.BlockSpec((None, tk, tn), lambda i, j, k: (j // npt, k, j % npt)) if b3
                  else pl.BlockSpec((tk, tn), lambda i, j, k: (k, j)))
        dn = (((1,), (0,)), ((), ()))
    elif dims == "nt":
        a_spec = pl.BlockSpec((tm, tk), lambda i, j, k: (i, k))
        b_spec = (pl.BlockSpec((None, tn, tk), lambda i, j, k: (k // kpt, j, k % kpt)) if b3
                  else pl.BlockSpec((tn, tk), lambda i, j, k: (j, k)))
        dn = _NT
    else:
        a_spec = pl.BlockSpec((tk, tm), lambda i, j, k: (k, i))
        b_spec = pl.BlockSpec((tk, tn), lambda i, j, k: (k, j))
        dn = _TN
    if out_slots:
        out_spec = pl.BlockSpec((None, tm, tn), lambda i, j, k: (j // npt, i, j % npt))
        out_shape = jax.ShapeDtypeStruct((out_slots, M, unit_n), out_dtype)
    else:
        out_spec = pl.BlockSpec((tm, tn), lambda i, j, k: (i, j))
        out_shape = jax.ShapeDtypeStruct((M, N), out_dtype)

    def body(a_ref, b_ref, o_ref, acc_ref):
        k = pl.program_id(2)

        @pl.when(k == 0)
        def _():
            acc_ref[...] = jnp.zeros_like(acc_ref)

        acc_ref[...] += lax.dot_general(a_ref[...], b_ref[...], dn, preferred_element_type=F32)

        @pl.when(k == nk - 1)
        def _():
            o_ref[...] = acc_ref[...].astype(o_ref.dtype)

    return pl.pallas_call(
        body, name=name, grid=(M // tm, N // tn, nk),
        in_specs=[a_spec, b_spec], out_specs=out_spec, out_shape=out_shape,
        scratch_shapes=[pltpu.VMEM((tm, tn), F32)],
        compiler_params=_cparams(("parallel", "parallel", "arbitrary")),
    )(a, b)


def _rowmap(fn, name, M, row_ins, bc_ins, row_outs, acc_outs=()):
    row_ins = [r if isinstance(r, tuple) else (r, r.shape[1], 0) for r in row_ins]
    row_bytes = sum(w * a.dtype.itemsize for a, w, _ in row_ins) + sum(w * jnp.dtype(d).itemsize for w, d in row_outs)
    widest = max([w for _, w, _ in row_ins] + [w for w, _ in row_outs])
    row_bytes = 2 * row_bytes + 6 * 4 * widest
    tm = _pick(M, [t for t in (512, 256, 128, 64, 32, 16) if t * row_bytes <= ROWMAP_TILE_BYTES] + [16])
    n_in, n_row, n_acc = len(row_ins) + len(bc_ins), len(row_outs), len(acc_outs)

    def body(*refs):
        res = fn(*[r[...] for r in refs[:n_in]])
        res = res if isinstance(res, (tuple, list)) else (res,)
        outs = refs[n_in:]
        for k in range(n_row):
            outs[k][...] = res[k].astype(outs[k].dtype)
        if n_acc:
            @pl.when(pl.program_id(0) == 0)
            def _():
                for k in range(n_acc):
                    outs[n_row + k][...] = jnp.zeros_like(outs[n_row + k])

            for k in range(n_acc):
                outs[n_row + k][...] += res[n_row + k].astype(F32)

    in_specs = [pl.BlockSpec((tm, w), functools.partial(lambda i, blk: (i, blk), blk=blk)) for _, w, blk in row_ins]
    in_specs += [pl.BlockSpec(b.shape, lambda i: (0, 0)) for b in bc_ins]
    out_specs = [pl.BlockSpec((tm, w), lambda i: (i, 0)) for w, _ in row_outs]
    out_specs += [pl.BlockSpec((1, w), lambda i: (0, 0)) for w in acc_outs]
    out_shape = [jax.ShapeDtypeStruct((M, w), d) for w, d in row_outs]
    out_shape += [jax.ShapeDtypeStruct((1, w), F32) for w in acc_outs]
    return pl.pallas_call(
        body, name=name, grid=(M // tm,), in_specs=in_specs, out_specs=out_specs, out_shape=out_shape,
        compiler_params=_cparams(("arbitrary",) if n_acc else ("parallel",)),
    )(*[a for a, _, _ in row_ins], *bc_ins)


def _rms(x, g):
    return x * lax.rsqrt(jnp.mean(x * x, axis=-1, keepdims=True) + EPS) * g


def _normmod(x, g, sc, sh):
    return _rms(x, g) * (1.0 + sc) + sh


def _swap16(v):
    w = v.shape[1]
    lane = lax.broadcasted_iota(jnp.int32, v.shape, 1)
    return jnp.where((lane // 16) % 2 == 0, pltpu.roll(v, w - 16, 1), pltpu.roll(v, 16, 1))


def _rope(v, cos, sin_signed):
    return v * cos + _swap16(v) * sin_signed


def _rope_bwd(d, cos, sin_signed):
    return d * cos + _swap16(d * sin_signed)


def _mesh_pos():
    return lax.axis_index("x"), lax.axis_index("y"), lax.axis_index("c")


def _hbm_call(body, name, ins, out_shapes, n_sems):
    any_spec = pl.BlockSpec(memory_space=pl.ANY)
    return pl.pallas_call(
        body, name=name, out_shape=out_shapes, in_specs=[any_spec] * len(ins), out_specs=[any_spec] * len(out_shapes),
        scratch_shapes=[pltpu.SemaphoreType.DMA((n_sems,)), pltpu.SemaphoreType.DMA((n_sems,)),
                        pltpu.SemaphoreType.DMA((len(ins),))],
    )(*ins)


def _all_gather(xs, name):
    n = len(xs)

    def body(*refs):
        x_refs, out_refs, (send_sems, recv_sems, local_sems) = refs[:n], refs[n:2 * n], refs[2 * n:]
        x, y, c = _mesh_pos()
        me, sibling = (x, y, c), (x, y, 1 - c)
        chips = [(1 - x, y), (x, 1 - y), (1 - x, 1 - y)]
        locals_, first, passed, arrivals = [], [], [], []
        for a in range(n):
            def slot(px, py, pc, a=a):
                return out_refs[a].at[4 * px + 2 * py + pc]

            def copy(k, block, to, src=None, a=a, slot=slot):
                return pltpu.make_async_remote_copy(
                    src_ref=slot(*block) if src is None else src, dst_ref=slot(*block),
                    send_sem=send_sems.at[7 * a + k], recv_sem=recv_sems.at[7 * a + k], device_id=to, device_id_type=MESH)

            locals_.append(pltpu.make_async_copy(x_refs[a], slot(*me), local_sems.at[a]))
            first.append(copy(0, me, sibling, src=x_refs[a]))
            first += [copy(1 + j, me, (*chip, c), src=x_refs[a]) for j, chip in enumerate(chips)]
            passed.append([copy(4 + j, (*chip, c), sibling) for j, chip in enumerate(chips)])
            arrivals.append([copy(1 + j, (*chip, c), me) for j, chip in enumerate(chips)]
                            + [copy(0, sibling, me)] + [copy(4 + j, (*chip, 1 - c), me) for j, chip in enumerate(chips)])
        for cp in locals_ + first:
            cp.start()
        for j in range(3):
            for a in range(n):
                arrivals[a][j].wait_recv()
                passed[a][j].start()
        for a in range(n):
            for cp in arrivals[a][3:]:
                cp.wait_recv()
        for cp in first + [p for ps in passed for p in ps]:
            cp.wait_send()
        for cp in locals_:
            cp.wait()

    return _hbm_call(body, name, xs, [jax.ShapeDtypeStruct((N_DEV,) + x.shape, x.dtype) for x in xs], 7 * n)


def _rs_pair(ps, name):
    n = len(ps)

    def body(*refs):
        p_refs, out_refs, (send_sems, recv_sems, _) = refs[:n], refs[n:2 * n], refs[2 * n:]
        x, y, c = _mesh_pos()
        sends, recvs = [], []
        for a in range(n):
            for q in range(N_CHIP):
                sem = dict(send_sem=send_sems.at[4 * a + q], recv_sem=recv_sems.at[4 * a + q],
                           device_id=(x, y, 1 - c), device_id_type=MESH)
                sends.append(pltpu.make_async_remote_copy(src_ref=p_refs[a].at[2 * q + 1 - c], dst_ref=out_refs[a].at[q], **sem))
                recvs.append(pltpu.make_async_remote_copy(src_ref=p_refs[a].at[2 * q + c], dst_ref=out_refs[a].at[q], **sem))
        for cp in sends:
            cp.start()
        for cp in recvs:
            cp.wait_recv()
        for cp in sends:
            cp.wait_send()

    return _hbm_call(body, name, ps, [jax.ShapeDtypeStruct((N_CHIP,) + p.shape[1:], p.dtype) for p in ps], 4 * n)


def _rs_chips(qs, name):
    n = len(qs)

    def body(*refs):
        q_refs, out_refs, (send_sems, recv_sems, local_sems) = refs[:n], refs[n:2 * n], refs[2 * n:]
        x, y, c = _mesh_pos()
        mine = 2 * x + y
        locals_, sends, recvs = [], [], []
        for a in range(n):
            locals_.append(pltpu.make_async_copy(q_refs[a].at[mine], out_refs[a].at[mine], local_sems.at[a]))
            for r in range(1, N_CHIP):
                px = 1 - x if r & 2 else x
                py = 1 - y if r & 1 else y
                peer = 2 * px + py
                sem = dict(send_sem=send_sems.at[3 * a + r - 1], recv_sem=recv_sems.at[3 * a + r - 1],
                           device_id=(px, py, c), device_id_type=MESH)
                sends.append(pltpu.make_async_remote_copy(src_ref=q_refs[a].at[peer], dst_ref=out_refs[a].at[mine], **sem))
                recvs.append(pltpu.make_async_remote_copy(src_ref=q_refs[a].at[peer], dst_ref=out_refs[a].at[peer], **sem))
        for cp in locals_ + sends:
            cp.start()
        for cp in recvs:
            cp.wait_recv()
        for cp in sends:
            cp.wait_send()
        for cp in locals_:
            cp.wait()

    return _hbm_call(body, name, qs, [jax.ShapeDtypeStruct(q.shape, q.dtype) for q in qs], 3 * n)


def _xchg_copies(src_refs, land_refs, send_sems, recv_sems, slot_src):
    x, y, c = _mesh_pos()
    me = 4 * x + 2 * y + c
    sends, recvs = [], []
    for a, (src, land) in enumerate(zip(src_refs, land_refs)):
        for r in range(1, N_DEV):
            px = 1 - x if r & 4 else x
            py = 1 - y if r & 2 else y
            pc = 1 - c if r & 1 else c
            peer = 4 * px + 2 * py + pc
            sem = dict(send_sem=send_sems.at[7 * a + r - 1], recv_sem=recv_sems.at[7 * a + r - 1],
                       device_id=(px, py, pc), device_id_type=MESH)
            s = src.at[peer] if slot_src else src
            sends.append(pltpu.make_async_remote_copy(src_ref=s, dst_ref=land.at[me], **sem))
            recvs.append(pltpu.make_async_remote_copy(src_ref=s, dst_ref=land.at[peer], **sem))
    return sends, recvs


_HBM = pl.BlockSpec(memory_space=pltpu.HBM)
_SEM = pl.BlockSpec(memory_space=pltpu.SEMAPHORE)
_EFFECT = pltpu.SideEffectType.DATAFLOW_SIDE_EFFECTING


def _xchg_start(srcs, lands, slot_src, name):
    n = len(srcs)

    def body(*refs):
        sends, _ = _xchg_copies(refs[:n], refs[n:2 * n], refs[2 * n], refs[2 * n + 1], slot_src)
        for cp in sends:
            cp.start()
        refs[-1][...] = jnp.zeros_like(refs[-1])

    bufs = list(srcs) + list(lands)
    res = pl.pallas_call(
        body, name=name,
        out_shape=(pltpu.SemaphoreType.DMA((7 * n,)), pltpu.SemaphoreType.DMA((7 * n,)))
        + tuple(pltpu.HBM(b.shape, b.dtype) for b in bufs) + (jax.ShapeDtypeStruct((SUB, LANE), F32),),
        in_specs=(_HBM,) * (2 * n), out_specs=(_SEM, _SEM) + (_HBM,) * (2 * n) + (pl.BlockSpec(memory_space=pltpu.VMEM),),
        input_output_aliases={i: 2 + i for i in range(2 * n)},
        compiler_params=pltpu.CompilerParams(has_side_effects=_EFFECT),
    )(*[pltpu.with_memory_space_constraint(b, pltpu.HBM) for b in bufs])
    return res[0], res[1], res[2:-1], res[-1]


def _xchg_wait(send_sems, recv_sems, thru, after, slot_src, name):
    n = len(thru) // 2

    def body(*refs):
        sends, recvs = _xchg_copies(refs[:n], refs[n:2 * n], refs[2 * n], refs[2 * n + 1], slot_src)
        for cp in sends:
            cp.wait_send()
        for cp in recvs:
            cp.wait_recv()

    res = pl.pallas_call(
        body, name=name, out_shape=tuple(pltpu.HBM(b.shape, b.dtype) for b in thru),
        in_specs=(_HBM,) * (2 * n) + (_SEM, _SEM, pl.BlockSpec(memory_space=pl.ANY)), out_specs=(_HBM,) * (2 * n),
        input_output_aliases={i: i for i in range(2 * n)},
        compiler_params=pltpu.CompilerParams(has_side_effects=_EFFECT),
    )(*thru, send_sems, recv_sems, after)
    return res[n:]


def _own_slot(block, me):
    return lax.dynamic_update_slice(lax.empty((N_DEV,) + block.shape, block.dtype), block[None], (me, 0, 0))


def _add_pair(p, r, name):
    _, R, C = p.shape
    tr = _pick(R, (512, 256, 128, 64, 32, 16))

    def body(c_ref, p_ref, r_ref, o_ref):
        o_ref[...] = (p_ref[...].astype(F32) + r_ref[...].astype(F32)).astype(o_ref.dtype)

    return pl.pallas_call(
        body, name=name, out_shape=jax.ShapeDtypeStruct((N_CHIP, R, C), p.dtype),
        grid_spec=pltpu.PrefetchScalarGridSpec(
            num_scalar_prefetch=1, grid=(N_CHIP, R // tr),
            in_specs=[pl.BlockSpec((None, None, tr, C), lambda q, i, c_ref: (q, c_ref[0], i, 0)),
                      pl.BlockSpec((None, tr, C), lambda q, i, c_ref: (q, i, 0))],
            out_specs=pl.BlockSpec((None, tr, C), lambda q, i, c_ref: (q, i, 0))),
        compiler_params=_cparams(("parallel", "parallel")),
    )(lax.axis_index("c").reshape(1).astype(jnp.int32), p.reshape(N_CHIP, 2, R, C), r)


def _sum_slots(g, name):
    ns, R, C = g.shape
    tr = _pick(R, (256, 128, 64, 32, 16))

    def body(g_ref, o_ref):
        acc = g_ref[0].astype(F32)
        for j in range(1, ns):
            acc = acc + g_ref[j].astype(F32)
        o_ref[...] = acc

    return pl.pallas_call(
        body, name=name, grid=(R // tr,),
        in_specs=[pl.BlockSpec((ns, tr, C), lambda i: (0, i, 0))], out_specs=pl.BlockSpec((tr, C), lambda i: (i, 0)),
        out_shape=jax.ShapeDtypeStruct((R, C), F32), compiler_params=_cparams(("parallel",)),
    )(g)


def _pack_rows(arrs, dtype):
    parts = []
    for a in arrs:
        flat = a.reshape(-1).astype(dtype)
        pad = (-flat.shape[0]) % (PACK_W * 16)
        parts.append(jnp.pad(flat, (0, pad)).reshape(-1, PACK_W))
    out = jnp.concatenate(parts, axis=0)
    return jnp.pad(out, ((0, (-out.shape[0]) % PACK_ROWS), (0, 0)))


def _packed_rows(shape):
    n = math.prod(shape)
    return (n + PACK_W * 16 - 1) // (PACK_W * 16) * 16


def _unpack_rows(packed, shapes):
    out, r0 = [], 0
    for s in shapes:
        rows, n = _packed_rows(s), math.prod(s)
        out.append(packed[r0:r0 + rows].reshape(rows * PACK_W)[:n].reshape(s))
        r0 += rows
    return out


def _adamw_math(w, g, m, v):
    m = ADAM_B1 * m + (1.0 - ADAM_B1) * g
    v = ADAM_B2 * v + (1.0 - ADAM_B2) * (g * g)
    m_hat = m / (1.0 - ADAM_B1 ** ADAM_STEP)
    v_hat = v / (1.0 - ADAM_B2 ** ADAM_STEP)
    delta = -ADAM_LR * (m_hat / (jnp.sqrt(v_hat) + ADAM_EPS) + ADAM_WD * w)
    return delta, m, v


def _adamw(w, g, m, v, name):
    R, C = w.shape
    return _rowmap(_adamw_math, name, R, [w, g, m, v], [], [(C, F32)] * 3)


def _s5_disc_math(lr, li, ldt, br, bi):
    dt = jnp.exp(ldt)
    mag = jnp.exp(lr * dt)
    ab_re, ab_im = mag * jnp.cos(li * dt), mag * jnp.sin(li * dt)
    den = lr * lr + li * li
    nr, ni = ab_re - 1.0, ab_im
    co_re = (nr * lr + ni * li) / den
    co_im = (ni * lr - nr * li) / den
    bb_re = co_re * br - co_im * bi
    bb_im = co_re * bi + co_im * br
    return ab_re, ab_im, bb_re, bb_im


def _s5_tables(a_re, a_im, ldt, b_re, b_im, c_re, c_im):
    _, G, P, N = b_re.shape
    nch = G // 8

    def body(lr_ref, li_ref, ldt_ref, br_ref, bi_ref, cr_ref, ci_ref, wre, wim, vre, vim, pwr, pwi):
        ar, ai, bb_re, bb_im = _s5_disc_math(lr_ref[0], li_ref[0], ldt_ref[0], br_ref[0], bi_ref[0])
        cr, ci = cr_ref[0], ci_ref[0]
        pr, pi = jnp.ones_like(ar), jnp.zeros_like(ar)
        for j in range(SUB + 1):
            pwr[0, j], pwi[0, j] = pr, pi
            if j < SUB:
                tabs = ((wre, bb_re * pr - bb_im * pi), (wim, bb_re * pi + bb_im * pr),
                        (vre, cr * pr - ci * pi), (vim, -(cr * pi + ci * pr)))
                for ref, val in tabs:
                    for s in range(nch):
                        ref[0, s, pl.ds(j * LANE, LANE), :] = val[s * 8:(s + 1) * 8].reshape(LANE, N).astype(BF16)
            pr, pi = pr * ar - pi * ai, pr * ai + pi * ar

    g1n = pl.BlockSpec((1, G, 1, N), lambda d: (d, 0, 0, 0))
    gpn = pl.BlockSpec((1, G, P, N), lambda d: (d, 0, 0, 0))
    tab = pl.BlockSpec((1, nch, SUB * LANE, N), lambda d: (d, 0, 0, 0))
    pw = pl.BlockSpec((1, SUB + 1, G, 1, N), lambda d: (d, 0, 0, 0, 0))
    s_tab = jax.ShapeDtypeStruct((2, nch, SUB * LANE, N), BF16)
    s_pw = jax.ShapeDtypeStruct((2, SUB + 1, G, 1, N), F32)
    return pl.pallas_call(
        body, name="s5_tables", grid=(2,),
        in_specs=[g1n, g1n, pl.BlockSpec((1, G, 1, 1), lambda d: (d, 0, 0, 0)), gpn, gpn, gpn, gpn],
        out_specs=[tab] * 4 + [pw] * 2, out_shape=[s_tab] * 4 + [s_pw] * 2,
        compiler_params=_cparams(("parallel",)),
    )(a_re, a_im, ldt, b_re, b_im, c_re, c_im)


def _s5_expand(t_re, t_im, name):
    _, nch, R, N = t_re.shape
    sw = 8 * N

    def body(re_ref, im_ref, o_ref):
        spread = (lax.broadcasted_iota(jnp.int32, (N, sw), 1) % N == lax.broadcasted_iota(jnp.int32, (N, sw), 0)).astype(BF16)
        row_g = (lax.broadcasted_iota(jnp.int32, (R, sw), 0) % LANE) // S5_GROUP
        keep = row_g == lax.broadcasted_iota(jnp.int32, (R, sw), 1) // N
        for half, ref in enumerate((re_ref, im_ref)):
            t = jnp.dot(ref[0, 0], spread, preferred_element_type=F32)
            o_ref[0, 0, :, pl.ds(half * sw, sw)] = jnp.where(keep, t, 0.0).astype(BF16)

    spec = pl.BlockSpec((1, 1, R, N), lambda d, s: (d, s, 0, 0))
    return pl.pallas_call(
        body, name=name, grid=(2, nch), in_specs=[spec, spec],
        out_specs=pl.BlockSpec((1, 1, R, 2 * sw), lambda d, s: (d, s, 0, 0)),
        out_shape=jax.ShapeDtypeStruct((2, nch, R, 2 * sw), BF16), compiler_params=_cparams(("parallel", "parallel")),
    )(t_re, t_im)


def _s5_param_bwd(a_re, a_im, ldt, b_re, b_im, da_re, da_im, dbb_re, dbb_im):
    _, G, P, N = b_re.shape

    def body(lr_ref, li_ref, ldt_ref, br_ref, bi_ref, dar, dai, dbr, dbi, o_lr, o_li, o_ldt, o_br, o_bi):
        _, vjp = jax.vjp(_s5_disc_math, lr_ref[0], li_ref[0], ldt_ref[0], br_ref[0], bi_ref[0])
        o_lr[0], o_li[0], o_ldt[0], o_br[0], o_bi[0] = vjp((dar[0], dai[0], dbr[0], dbi[0]))

    g1n = pl.BlockSpec((1, G, 1, N), lambda d: (d, 0, 0, 0))
    g11 = pl.BlockSpec((1, G, 1, 1), lambda d: (d, 0, 0, 0))
    gpn = pl.BlockSpec((1, G, P, N), lambda d: (d, 0, 0, 0))
    s_g1n, s_g11, s_gpn = (jax.ShapeDtypeStruct(s, F32) for s in ((2, G, 1, N), (2, G, 1, 1), (2, G, P, N)))
    return pl.pallas_call(
        body, name="s5_param_bwd", grid=(2,),
        in_specs=[g1n, g1n, g11, gpn, gpn, g1n, g1n, gpn, gpn], out_specs=[g1n, g1n, g11, gpn, gpn],
        out_shape=[s_g1n, s_g1n, s_g11, s_gpn, s_gpn], compiler_params=_cparams(("parallel",)),
    )(a_re, a_im, ldt, b_re, b_im, da_re, da_im, dbb_re, dbb_im)


def _shift_stack(u, back):
    tb = u.shape[0]
    tau = lax.broadcasted_iota(jnp.int32, u.shape, 0) % SUB
    parts = [u]
    for j in range(1, SUB):
        if back:
            parts.append(jnp.where(tau >= j, pltpu.roll(u, j, 0), 0.0))
        else:
            parts.append(jnp.where(tau <= SUB - 1 - j, pltpu.roll(u, tb - j, 0), 0.0))
    return jnp.concatenate(parts, axis=1).astype(BF16)


def _cmul_add(tile, pw, carry, sw):
    pr, pi, cr, ci = pw[:, :sw], pw[:, sw:], carry[:, :sw], carry[:, sw:]
    return tile + jnp.concatenate([pr * cr - pi * ci, pr * ci + pi * cr], axis=1)


def _tile_scan(buf, base, ntile, pw, carry, sw, causal):
    def step(k, c):
        i = k if causal else ntile - 1 - k
        r = pl.multiple_of(base + i * SUB, SUB)
        tile = _cmul_add(buf[pl.ds(r, SUB), :], pw, c, sw)
        buf[pl.ds(r, SUB), :] = tile
        return tile[SUB - 1:SUB, :] if causal else tile[0:1, :]

    return lax.fori_loop(0, ntile, step, carry)


def _s5_fwd(h_all, waug, vaug, pw, S5W, T, causal, name):
    S = h_all.shape[0]
    nch, _, sw2 = waug.shape
    sw = sw2 // 2
    tb = _pick(math.gcd(T, S - T), (256, 128, 64, 32, 16))
    ntile, nt, off = tb // SUB, S // tb, T // tb
    rb = (lambda s, t: ((t + off) % nt, s)) if causal else (lambda s, t: (nt - 1 - t, s))

    def body(u_ref, w_ref, v_ref, p_ref, y_ref, h_ref, hblk, carry):
        @pl.when(pl.program_id(1) == 0)
        def _():
            carry[...] = jnp.zeros_like(carry)

        hblk[...] = jnp.dot(_shift_stack(u_ref[...], causal), w_ref[...], preferred_element_type=F32)
        carry[...] = _tile_scan(hblk, 0, ntile, p_ref[...], carry[...], sw, causal)
        hb = hblk[...].astype(BF16)
        h_ref[...] = hb
        y_ref[...] = lax.dot_general(hb, v_ref[...], _NT, preferred_element_type=F32)

    return pl.pallas_call(
        body, name=name, grid=(nch, nt),
        in_specs=[pl.BlockSpec((tb, LANE), rb),
                  pl.BlockSpec((None, SUB * LANE, sw2), lambda s, t: (s, 0, 0)),
                  pl.BlockSpec((None, LANE, sw2), lambda s, t: (s, 0, 0)),
                  pl.BlockSpec((None, SUB, sw2), lambda s, t: (s, 0, 0))],
        out_specs=[pl.BlockSpec((tb, LANE), rb), pl.BlockSpec((tb, sw2), rb)],
        out_shape=[jax.ShapeDtypeStruct((S, S5W), F32), jax.ShapeDtypeStruct((S, nch * sw2), BF16)],
        scratch_shapes=[pltpu.VMEM((tb, sw2), F32), pltpu.VMEM((1, sw2), F32)],
        compiler_params=_cparams(("parallel", "arbitrary")),
    )(h_all, waug, vaug, pw)


def _s5_bwd(dy_all, h_all, hs, waug, vaug, pwc, S5W, T, causal, name):
    S = h_all.shape[0]
    nch, _, sw2 = waug.shape
    sw = sw2 // 2
    tb = _pick(math.gcd(T, S - T), (256, 128, 64, 32, 16))
    ntile, nt, off = tb // SUB, S // tb, T // tb
    rb = (lambda s, t: ((nt - 1 - t + off) % nt, s)) if causal else (lambda s, t: (t, s))
    adj_causal = not causal
    edge = SUB - 1 if adj_causal else SUB + tb
    keep_src, keep_dst = (tb, 0) if adj_causal else (SUB, SUB + tb)

    def body(dy_ref, u_ref, h_ref, w_ref, v_ref, p_ref, du_ref, dbb_ref, dc_ref, da_ref, lam):
        @pl.when(pl.program_id(1) == 0)
        def _():
            lam[pl.ds(0, SUB), :] = jnp.zeros((SUB, sw2), F32)
            lam[pl.ds(SUB + tb, SUB), :] = jnp.zeros((SUB, sw2), F32)
            dbb_ref[...] = jnp.zeros_like(dbb_ref)
            dc_ref[...] = jnp.zeros_like(dc_ref)
            da_ref[...] = jnp.zeros_like(da_ref)

        dy = dy_ref[...]
        lam[pl.ds(SUB, tb), :] = jnp.dot(_shift_stack(dy, adj_causal), v_ref[...], preferred_element_type=F32)
        _tile_scan(lam, SUB, ntile, p_ref[...], lam[pl.ds(edge, 1), :], sw, adj_causal)
        lb = lam[pl.ds(SUB, tb), :].astype(BF16)
        du_ref[...] = lax.dot_general(lb, w_ref[...], _NT, preferred_element_type=F32)
        dbb_ref[...] += lax.dot_general(u_ref[...].astype(BF16), lb, _TN, preferred_element_type=F32)
        dc_ref[...] += lax.dot_general(h_ref[...], dy.astype(BF16), _TN, preferred_element_type=F32)
        h = h_ref[...].astype(F32)
        ln = lam[pl.ds(SUB + 1 if causal else SUB - 1, tb), :]
        hr, hi, lr, li = h[:, :sw], h[:, sw:], ln[:, :sw], ln[:, sw:]
        da_ref[...] += jnp.concatenate([jnp.sum(hr * lr + hi * li, axis=0, keepdims=True),
                                        jnp.sum(hr * li - hi * lr, axis=0, keepdims=True)], axis=1)
        lam[pl.ds(keep_dst, SUB), :] = lam[pl.ds(keep_src, SUB), :]

    fixed = lambda s, t: (s, 0, 0)
    return pl.pallas_call(
        body, name=name, grid=(nch, nt),
        in_specs=[pl.BlockSpec((tb, LANE), rb), pl.BlockSpec((tb, LANE), rb), pl.BlockSpec((tb, sw2), rb),
                  pl.BlockSpec((None, LANE, sw2), fixed), pl.BlockSpec((None, SUB * LANE, sw2), fixed),
                  pl.BlockSpec((None, SUB, sw2), fixed)],
        out_specs=[pl.BlockSpec((tb, LANE), rb), pl.BlockSpec((None, LANE, sw2), fixed),
                   pl.BlockSpec((None, sw2, LANE), fixed), pl.BlockSpec((None, 1, sw2), fixed)],
        out_shape=[jax.ShapeDtypeStruct((S, S5W), F32), jax.ShapeDtypeStruct((nch, LANE, sw2), F32),
                   jax.ShapeDtypeStruct((nch, sw2, LANE), F32), jax.ShapeDtypeStruct((nch, 1, sw2), F32)],
        scratch_shapes=[pltpu.VMEM((tb + 2 * SUB, sw2), F32)],
        compiler_params=_cparams(("parallel", "arbitrary")),
    )(dy_all, h_all, hs, waug, vaug, pwc)


def _attn_fwd(q3, k3, v3, scale):
    H, T, dk = q3.shape
    S, dv = k3.shape[1], v3.shape[2]
    tq = _pick(T, (256, 128, 64, 32, 16))

    def body(q_ref, k_ref, v_ref, o_ref, lse_ref):
        s = lax.dot_general(q_ref[0], k_ref[0], _NT, preferred_element_type=F32) * scale
        m = jnp.max(s, axis=1, keepdims=True)
        p = jnp.exp(s - m)
        l = jnp.sum(p, axis=1, keepdims=True)
        o_ref[...] = jnp.dot((p / l).astype(BF16), v_ref[0], preferred_element_type=F32).astype(o_ref.dtype)
        lse_ref[0] = m + jnp.log(l)

    return pl.pallas_call(
        body, name="attn_fwd", grid=(H, T // tq),
        in_specs=[pl.BlockSpec((1, tq, dk), lambda h, i: (h, i, 0)), pl.BlockSpec((1, S, dk), lambda h, i: (h, 0, 0)),
                  pl.BlockSpec((1, S, dv), lambda h, i: (h, 0, 0))],
        out_specs=[pl.BlockSpec((tq, dv), lambda h, i: (i, h)), pl.BlockSpec((1, tq, 1), lambda h, i: (h, i, 0))],
        out_shape=[jax.ShapeDtypeStruct((T, H * dv), BF16), jax.ShapeDtypeStruct((H, T, 1), F32)],
        compiler_params=_cparams(("parallel", "parallel")),
    )(q3, k3, v3)


def _attn_bwd(q3, k3, v3, do, lse, scale):
    H, T, dk = q3.shape
    S, dv = k3.shape[1], v3.shape[2]
    tq = _pick(T, (256, 128, 64, 32, 16))

    def body(q_ref, k_ref, v_ref, do_ref, lse_ref, dq_ref, dk_ref, dv_ref):
        @pl.when(pl.program_id(1) == 0)
        def _():
            dk_ref[...] = jnp.zeros_like(dk_ref)
            dv_ref[...] = jnp.zeros_like(dv_ref)

        q, k, v, d_o = q_ref[0], k_ref[0], v_ref[0], do_ref[...]
        s = lax.dot_general(q, k, _NT, preferred_element_type=F32) * scale
        p = jnp.exp(s - lse_ref[0])
        dv_ref[0] += lax.dot_general(p.astype(BF16), d_o, _TN, preferred_element_type=F32)
        dp = lax.dot_general(d_o, v, _NT, preferred_element_type=F32)
        ds = (p * (dp - jnp.sum(p * dp, axis=1, keepdims=True)) * scale).astype(BF16)
        dq_ref[0] = jnp.dot(ds, k, preferred_element_type=F32)
        dk_ref[0] += lax.dot_general(ds, q, _TN, preferred_element_type=F32)

    return pl.pallas_call(
        body, name="attn_bwd", grid=(H, T // tq),
        in_specs=[pl.BlockSpec((1, tq, dk), lambda h, i: (h, i, 0)), pl.BlockSpec((1, S, dk), lambda h, i: (h, 0, 0)),
                  pl.BlockSpec((1, S, dv), lambda h, i: (h, 0, 0)), pl.BlockSpec((tq, dv), lambda h, i: (i, h)),
                  pl.BlockSpec((1, tq, 1), lambda h, i: (h, i, 0))],
        out_specs=[pl.BlockSpec((1, tq, dk), lambda h, i: (h, i, 0)), pl.BlockSpec((1, S, dk), lambda h, i: (h, 0, 0)),
                   pl.BlockSpec((1, S, dv), lambda h, i: (h, 0, 0))],
        out_shape=[jax.ShapeDtypeStruct((H, T, dk), F32), jax.ShapeDtypeStruct((H, S, dk), F32),
                   jax.ShapeDtypeStruct((H, S, dv), F32)],
        compiler_params=_cparams(("parallel", "arbitrary")),
    )(q3, k3, v3, do, lse)


def _rope_tables(T, heads):
    rows = T // GRID_W
    row = jnp.repeat(jnp.arange(rows, dtype=F32), GRID_W)
    col = jnp.tile(jnp.arange(GRID_W, dtype=F32), rows)
    n_freq = QK_ROPE // 4
    inv = ROPE_BASE ** (-jnp.arange(n_freq, dtype=F32) / n_freq)
    ar, ac = row[:, None] * inv, col[:, None] * inv
    cos = jnp.concatenate([jnp.cos(ar), jnp.cos(ar), jnp.cos(ac), jnp.cos(ac)], axis=1)
    sin = jnp.concatenate([-jnp.sin(ar), jnp.sin(ar), -jnp.sin(ac), jnp.sin(ac)], axis=1)
    return jnp.tile(cos, (1, heads)), jnp.tile(sin, (1, heads))


def _dw(a, dy, w, name):
    return _mm(a, dy, "tn", BF16, name, out_slots=w.shape[0] if w.ndim == 3 else None)


def _local_step(x, ctx, tgt, m_lat, m_ctx, p, W, goff, ffn_weights=None, ffn_grads=None):
    T, D = x.shape
    Tc = ctx.shape[0]
    S = T + Tc
    S5W = p["s5_d"].shape[1]
    QR, KVR = p["q_norm"].shape[1], p["kv_norm"].shape[1]
    H = W["w_uq"].shape[1] // (QK_NOPE + QK_ROPE)
    G, N = p["s5_a_re"].shape[1:]
    P = S5_GROUP
    nch = G // 8
    o_cq, o_ckv, o_kr = S5W, S5W + QR, S5W + QR + KVR
    assert o_cq % QR == 0 and o_ckv % KVR == 0 and o_kr % LANE == 0 and goff % D == 0 and S5W % LANE == 0 and G % 8 == 0
    assert 8 * P == LANE
    row = lambda k, m: m[k:k + 1]
    sh1, sc1, g1, sh2, sc2, g2 = (row(k, m_lat) for k in range(6))
    csh1, csc1 = row(0, m_ctx), row(1, m_ctx)
    n1, n2, nf = p["norm1"], p["norm2"], p["norm_f"]

    (xm_lat,) = _rowmap(_normmod, "norm1_lat", T, [x], [n1, sc1, sh1], [(D, BF16)])
    (xm_ctx,) = _rowmap(_normmod, "norm1_ctx", Tc, [ctx], [n1, csc1, csh1], [(D, BF16)])
    xm_all = jnp.concatenate([xm_lat, xm_ctx], axis=0)
    h_all = _mm(xm_all, W["w_in"], "nn", F32, "mm_in")

    a_re, a_im = p["s5_a_re"][:, :, None, :], p["s5_a_im"][:, :, None, :]
    ldt = p["s5_log_dt"][:, :, None, None]
    b_re, b_im = p["s5_b_re"].transpose(0, 1, 3, 2), p["s5_b_im"].transpose(0, 1, 3, 2)
    wre, wim, vre, vim, pwr, pwi = _s5_tables(a_re, a_im, ldt, b_re, b_im, p["s5_c_re"], p["s5_c_im"])
    waug = _s5_expand(wre, wim, "s5_expand_b")
    vaug = _s5_expand(vre, vim, "s5_expand_c")
    lanes = lambda t: t.reshape(2, SUB + 1, nch, 8 * N).transpose(0, 2, 1, 3)
    pw_re, pw_im = lanes(pwr), lanes(pwi)
    near = lambda t: t[:, :, 1:]
    far = lambda t: t[:, :, :0:-1]
    pw_c = jnp.concatenate([near(pw_re), near(pw_im)], axis=-1)
    pw_a = jnp.concatenate([far(pw_re), far(pw_im)], axis=-1)
    pwc_c = jnp.concatenate([near(pw_re), -near(pw_im)], axis=-1)
    pwc_a = jnp.concatenate([far(pw_re), -far(pw_im)], axis=-1)
    y0, hs0 = _s5_fwd(h_all, waug[0], vaug[0], pw_c[0], S5W, T, True, "s5_scan_fwd0")
    y1, hs1 = _s5_fwd(h_all, waug[1], vaug[1], pw_a[1], S5W, T, False, "s5_scan_fwd1")

    def s5_combine(u, yf, yr, dskip):
        y5 = dskip * u + yf + yr
        return y5, jax.nn.gelu(y5)

    y5, z = _rowmap(s5_combine, "s5_combine", T, [(h_all, S5W, 0), y0, y1], [p["s5_d"]], [(S5W, F32), (S5W, BF16)])

    (qn,) = _rowmap(_rms, "q_norm", T, [(h_all, QR, o_cq // QR)], [p["q_norm"]], [(QR, BF16)])
    (kvn,) = _rowmap(_rms, "kv_norm", S, [(h_all, KVR, o_ckv // KVR)], [p["kv_norm"]], [(KVR, BF16)])
    qraw = _mm(qn, W["w_uq"], "nn", F32, "mm_uq")
    kvraw = _mm(kvn, W["w_ukv"], "nn", BF16, "mm_ukv")
    cos_q, sin_q = _rope_tables(T, H)
    padl = lambda t: jnp.pad(t[:, :QK_ROPE], ((0, Tc), (0, LANE - QK_ROPE)))
    cos_k = padl(cos_q) + jnp.pad(jnp.ones((Tc, LANE), F32), ((T, 0), (0, 0)))
    sin_k = padl(sin_q)
    hn = H * QK_NOPE

    def q_post(q, cos, sin):
        return q[:, :hn], _rope(q[:, hn:], cos, sin)

    q_nope, q_rope = _rowmap(q_post, "q_rope", T, [qraw, cos_q, sin_q], [], [(hn, BF16), (H * QK_ROPE, BF16)])
    (kr,) = _rowmap(_rope, "k_rope", S, [(h_all, LANE, o_kr // LANE), cos_k, sin_k], [], [(LANE, BF16)])
    q3 = jnp.concatenate([q_nope.reshape(T, H, QK_NOPE), q_rope.reshape(T, H, QK_ROPE)], axis=-1).transpose(1, 0, 2)
    k3 = jnp.concatenate([kvraw[:, :hn].reshape(S, H, QK_NOPE),
                          jnp.broadcast_to(kr[:, None, :QK_ROPE], (S, H, QK_ROPE))], axis=-1).transpose(1, 0, 2)
    v3 = kvraw[:, hn:].reshape(S, H, V_DIM).transpose(1, 0, 2)
    scale = (QK_NOPE + QK_ROPE) ** -0.5
    o, lse = _attn_fwd(q3, k3, v3, scale)

    zz = _mm(z, W["w_glu"], "nn", F32, "mm_glu")
    br_mla = _mm(o, W["w_mla_o"], "nn", F32, "mm_mla_o")

    def merge(zz, brm, gs, gm):
        a, b = zz[:, :D], zz[:, D:]
        return jax.nn.sigmoid(gs) * (a * jax.nn.sigmoid(b)) + jax.nn.sigmoid(gm) * brm

    gb = goff // D
    merge_ins = [zz, br_mla, (h_all, D, gb), (h_all, D, gb + 1)]
    (mix,) = _rowmap(merge, "merge", T, merge_ins, [], [(D, BF16)])
    out1 = _mm(mix, W["w_out"], "nn", F32, "mm_out")

    def resid_norm2(x, out1, g1, n2, sc2, sh2):
        x1 = x + g1 * out1
        return x1, _normmod(x1, n2, sc2, sh2)

    x1, hm = _rowmap(resid_norm2, "resid_norm2", T, [x, out1], [g1, n2, sc2, sh2], [(D, F32), (D, BF16)])

    if ffn_weights is not None:
        W = {**W, **ffn_weights(hm)}
    FF = W["w_ffn_out"].shape[0]
    assert FF % LANE == 0
    ab = _mm(hm, W["w_ffn_in"], "nn", F32, "mm_ffn_in")

    def swiglu_act(a, b):
        return jax.nn.silu(a) * b

    (f,) = _rowmap(swiglu_act, "ffn_act", T, [(ab, FF, 0), (ab, FF, 1)], [], [(FF, BF16)])
    out2 = _mm(f, W["w_ffn_out"], "nn", F32, "mm_ffn_out")

    def loss_rows(x1, out2, g2, nf, tgt):
        y = _rms(x1 + g2 * out2, nf)
        return 0.5 * jnp.sum(jnp.mean(jnp.square(y - tgt), axis=-1))

    def final(x1, out2, tgt, g2, nf):
        val, (dx1, dout2, dg2, dnf) = jax.value_and_grad(loss_rows, argnums=(0, 1, 2, 3))(x1, out2, g2, nf, tgt)
        return dx1, dout2, jnp.full((1, LANE), val, F32), dg2, dnf

    dx2, dout2, loss_acc, dg2, dnf = _rowmap(final, "final_loss", T, [x1, out2, tgt], [g2, nf],
                                             [(D, F32), (D, BF16)], [LANE, D, D])

    gW = {}
    df = _mm(dout2, W["w_ffn_out"], "nt", F32, "mm_ffn_out_dx")
    gW["w_ffn_out"] = _dw(f, dout2, W["w_ffn_out"], "mm_ffn_out_dw")

    def swiglu_bwd(a, b, df):
        _, vjp = jax.vjp(swiglu_act, a, b)
        da, db = vjp(df)
        return jnp.concatenate([da, db], axis=1)

    (dab,) = _rowmap(swiglu_bwd, "ffn_act_bwd", T, [(ab, FF, 0), (ab, FF, 1), df], [], [(2 * FF, BF16)])
    dhm = _mm(dab, W["w_ffn_in"], "nt", F32, "mm_ffn_in_dx")
    gW["w_ffn_in"] = _dw(hm, dab, W["w_ffn_in"], "mm_ffn_in_dw")
    if ffn_grads is not None:
        token = ffn_grads(gW.pop("w_ffn_in"), gW.pop("w_ffn_out"))
        g1 = g1 if token is None else g1 + token[:1, :1]

    def resid_norm2_bwd(x, out1, dx2, dhm, g1, n2, sc2, sh2):
        _, vjp = jax.vjp(resid_norm2, x, out1, g1, n2, sc2, sh2)
        dx, dout1, dg1, dn2, dsc2, dsh2 = vjp((dx2, dhm))
        return dx, dout1, dg1, dn2, dsc2, dsh2

    dx1, dout1, dg1, dn2, dsc2, dsh2 = _rowmap(resid_norm2_bwd, "resid_norm2_bwd", T, [x, out1, dx2, dhm],
                                               [g1, n2, sc2, sh2], [(D, F32), (D, BF16)], [D, D, D, D])

    dmix = _mm(dout1, W["w_out"], "nt", F32, "mm_out_dx")
    gW["w_out"] = _dw(mix, dout1, W["w_out"], "mm_out_dw")

    def merge_bwd(zz, brm, gs, gm, dmix):
        _, vjp = jax.vjp(merge, zz, brm, gs, gm)
        dzz, dbrm, dgs, dgm = vjp(dmix)
        return dzz, dbrm, jnp.concatenate([dgs, dgm], axis=1)

    dzz, dbrm, dgates = _rowmap(merge_bwd, "merge_bwd", T, merge_ins + [dmix], [],
                                [(2 * D, BF16), (D, BF16), (2 * D, BF16)])
    do = _mm(dbrm, W["w_mla_o"], "nt", BF16, "mm_mla_o_dx")
    gW["w_mla_o"] = _dw(o, dbrm, W["w_mla_o"], "mm_mla_o_dw")
    dz = _mm(dzz, W["w_glu"], "nt", F32, "mm_glu_dx")
    gW["w_glu"] = _dw(z, dzz, W["w_glu"], "mm_glu_dw")

    def s5_combine_bwd(u, y5, dz, dskip):
        _, vjp = jax.vjp(lambda y: jax.nn.gelu(y), y5)
        (dy5,) = vjp(dz)
        return dy5, jnp.sum(dy5 * u, axis=0, keepdims=True)

    dy5, d_skip = _rowmap(s5_combine_bwd, "s5_combine_bwd", T, [(h_all, S5W, 0), y5, dz], [p["s5_d"]], [(S5W, F32)], [S5W])

    dq3, dk3, dv3 = _attn_bwd(q3, k3, v3, do, lse, scale)
    dq_t = dq3.transpose(1, 0, 2)
    dq_cat = jnp.concatenate([dq_t[:, :, :QK_NOPE].reshape(T, hn), dq_t[:, :, QK_NOPE:].reshape(T, H * QK_ROPE)], axis=1)

    def q_post_bwd(dq, cos, sin):
        return jnp.concatenate([dq[:, :hn], _rope_bwd(dq[:, hn:], cos, sin)], axis=1)

    (dqraw,) = _rowmap(q_post_bwd, "q_rope_bwd", T, [dq_cat, cos_q, sin_q], [], [(H * (QK_NOPE + QK_ROPE), BF16)])
    dk_t = dk3.transpose(1, 0, 2)
    dkvraw = jnp.concatenate([dk_t[:, :, :QK_NOPE].reshape(S, hn), dv3.transpose(1, 0, 2).reshape(S, H * V_DIM)], axis=1)
    dkr_heads = jnp.pad(dk_t[:, :, QK_NOPE:], ((0, 0), (0, 0), (0, LANE - QK_ROPE))).reshape(S, H * LANE)

    def k_rope_bwd(dkh, cos, sin):
        d = dkh[:, :LANE]
        for h in range(1, H):
            d = d + dkh[:, h * LANE:(h + 1) * LANE]
        return _rope_bwd(d, cos, sin)

    (dkr,) = _rowmap(k_rope_bwd, "k_rope_bwd", S, [dkr_heads, cos_k, sin_k], [], [(LANE, BF16)])
    dqn = _mm(dqraw, W["w_uq"], "nt", F32, "mm_uq_dx")
    gW["w_uq"] = _dw(qn, dqraw, W["w_uq"], "mm_uq_dw")
    dkvn = _mm(dkvraw, W["w_ukv"], "nt", F32, "mm_ukv_dx")
    gW["w_ukv"] = _dw(kvn, dkvraw, W["w_ukv"], "mm_ukv_dw")

    def rms_bwd(cx, dn, g):
        _, vjp = jax.vjp(_rms, cx, g)
        return vjp(dn)

    dcq, dq_norm = _rowmap(rms_bwd, "q_norm_bwd", T, [(h_all, QR, o_cq // QR), dqn], [p["q_norm"]], [(QR, BF16)], [QR])
    dckv, dkv_norm = _rowmap(rms_bwd, "kv_norm_bwd", S, [(h_all, KVR, o_ckv // KVR), dkvn], [p["kv_norm"]],
                             [(KVR, BF16)], [KVR])

    dy_all = jnp.concatenate([dy5, jnp.zeros((Tc, S5W), F32)], axis=0)
    du0, dbb0, dc0, da0 = _s5_bwd(dy_all, h_all, hs0, waug[0], vaug[0], pwc_a[0], S5W, T, True, "s5_scan_bwd0")
    du1, dbb1, dc1, da1 = _s5_bwd(dy_all, h_all, hs1, waug[1], vaug[1], pwc_c[1], S5W, T, False, "s5_scan_bwd1")

    def du_combine(a, b, dy, dskip):
        return a + b + dskip * dy

    (du_all,) = _rowmap(du_combine, "s5_du", S, [du0, du1, dy_all], [p["s5_d"]], [(S5W, BF16)])
    dbb = jnp.einsum("dsgpcgn->dcsgpn", jnp.stack([dbb0, dbb1]).reshape(2, nch, 8, P, 2, 8, N)).reshape(2, 2, G, P, N)
    dcm = jnp.einsum("dscgngp->dcsgpn", jnp.stack([dc0, dc1]).reshape(2, nch, 2, 8, N, 8, P)).reshape(2, 2, G, P, N)
    da = jnp.stack([da0, da1]).reshape(2, nch, 2, 8, N).transpose(0, 2, 1, 3, 4).reshape(2, 2, G, 1, N)
    d_lr, d_li, d_ldt, d_br, d_bi = _s5_param_bwd(a_re, a_im, ldt, b_re, b_im, da[:, 0], da[:, 1], dbb[:, 0], dbb[:, 1])

    lat_only = lambda t: jnp.pad(t, ((0, Tc), (0, 0)))
    dh_all = jnp.concatenate([du_all, lat_only(dcq), dckv, dkr, jnp.zeros((S, goff - o_kr - LANE), BF16), lat_only(dgates)],
                             axis=1)
    dxm = _mm(dh_all, W["w_in"], "nt", F32, "mm_in_dx")
    gW["w_in"] = _dw(xm_all, dh_all, W["w_in"], "mm_in_dw")

    def norm1_bwd(x, dxm, dx1, n1, sc, sh):
        _, vjp = jax.vjp(_normmod, x, n1, sc, sh)
        dx, dn, dsc, dsh = vjp(dxm)
        return dx + dx1, dn, dsc, dsh

    grad_x, dn1_l, dsc1, dsh1 = _rowmap(norm1_bwd, "norm1_lat_bwd", T, [x, dxm, dx1], [n1, sc1, sh1], [(D, F32)], [D, D, D])

    def norm1_ctx_bwd(x, dxm, n1, sc, sh):
        _, vjp = jax.vjp(_normmod, x, n1, sc, sh)
        return vjp(dxm)[1:]

    dn1_c, dcsc1, dcsh1 = _rowmap(norm1_ctx_bwd, "norm1_ctx_bwd", Tc, [ctx, dxm[T:]], [n1, csc1, csh1], [], [D, D, D])

    zero = jnp.zeros((1, D), F32)
    dm_lat = jnp.concatenate([dsh1, dsc1, dg1, dsh2, dsc2, dg2], axis=0)
    dm_ctx = jnp.concatenate([dcsh1, dcsc1, zero, zero, zero, zero], axis=0)
    small = {
        "norm1": dn1_l + dn1_c, "norm2": dn2, "norm_f": dnf, "q_norm": dq_norm, "kv_norm": dkv_norm, "s5_d": d_skip,
        "s5_a_re": d_lr, "s5_a_im": d_li, "s5_log_dt": d_ldt, "s5_b_re": d_br.transpose(0, 1, 3, 2),
        "s5_b_im": d_bi.transpose(0, 1, 3, 2), "s5_c_re": dcm[:, 0], "s5_c_im": -dcm[:, 1],
    }
    return loss_acc[:, :1], grad_x, small, dm_lat, dm_ctx, gW


BIG = ("w_in", "w_uq", "w_ukv", "w_glu", "w_mla_o", "w_out", "w_ffn_in", "w_ffn_out")
FFN = ("w_ffn_in", "w_ffn_out")
ROW_SHARDED = ("w_out", "w_ffn_out")
RELAID = ("w_in", "w_uq", "w_ukv")
SMALL = ("c_ctx", "b_mod", "norm1", "norm2", "s5_a_re", "s5_a_im", "s5_log_dt", "s5_b_re", "s5_b_im", "s5_c_re",
         "s5_c_im", "s5_d", "q_norm", "kv_norm", "norm_f")
WEIGHTS = ("c_ctx", "w_mod", "b_mod", "norm1", "norm2", "w_in", "s5_a_re", "s5_a_im", "s5_log_dt", "s5_b_re", "s5_b_im",
           "s5_c_re", "s5_c_im", "s5_d", "w_glu", "q_norm", "kv_norm", "w_uq", "w_ukv", "w_mla_o", "w_out", "w_ffn_in",
           "w_ffn_out", "norm_f")


def _heads_split(w, heads, first):
    k = w.shape[0]
    w3 = w.reshape(k, heads, -1)
    return jnp.concatenate([w3[:, :, :first].reshape(k, -1), w3[:, :, first:].reshape(k, -1)], axis=1)


def _heads_merge(w, heads, first):
    k = w.shape[0]
    a, b = w[:, :heads * first].reshape(k, heads, first), w[:, heads * first:].reshape(k, heads, -1)
    return jnp.concatenate([a, b], axis=2).reshape(k, -1)


def _cols_full(w8):
    return w8.transpose(1, 0, 2).reshape(w8.shape[1], -1)


def _cols_slots(w):
    return w.reshape(w.shape[0], N_DEV, -1).transpose(1, 0, 2)


def _weight_layout(n, w8):
    if n in ROW_SHARDED:
        return w8.reshape(-1, w8.shape[-1])
    return _cols_full(w8) if (n in RELAID or w8.shape[-1] % LANE) else w8


def _grad_slots(n, g):
    if g.ndim == 3:
        return g
    return g.reshape(N_DEV, g.shape[0] // N_DEV, g.shape[1]) if n in ROW_SHARDED else _cols_slots(g)


def _model_weights(g8, D):
    W = {n: _weight_layout(n, w8) for n, w8 in g8.items()}
    w_in = W["w_in"]
    n_front = w_in.shape[1] - 2 * D
    goff = -(-n_front // D) * D
    W["w_in"] = jnp.concatenate([w_in[:, :n_front], jnp.zeros((D, goff - n_front), w_in.dtype), w_in[:, n_front:]], axis=1)
    heads = W["w_uq"].shape[1] // (QK_NOPE + QK_ROPE)
    W["w_uq"] = _heads_split(W["w_uq"], heads, QK_NOPE)
    W["w_ukv"] = _heads_split(W["w_ukv"], heads, QK_NOPE)
    return W, goff


def kernel(x, c, ctx, c_ctx, w_mod, b_mod, norm1, norm2, w_in, s5_a_re, s5_a_im, s5_log_dt, s5_b_re, s5_b_im, s5_c_re, s5_c_im, s5_d, w_glu, q_norm, kv_norm, w_uq, w_ukv, w_mla_o, w_out, w_ffn_in, w_ffn_out, norm_f, loss_target, m_c_ctx, m_w_mod, m_b_mod, m_norm1, m_norm2, m_w_in, m_s5_a_re, m_s5_a_im, m_s5_log_dt, m_s5_b_re, m_s5_b_im, m_s5_c_re, m_s5_c_im, m_s5_d, m_w_glu, m_q_norm, m_kv_norm, m_w_uq, m_w_ukv, m_w_mla_o, m_w_out, m_w_ffn_in, m_w_ffn_out, m_norm_f, v_c_ctx, v_w_mod, v_b_mod, v_norm1, v_norm2, v_w_in, v_s5_a_re, v_s5_a_im, v_s5_log_dt, v_s5_b_re, v_s5_b_im, v_s5_c_re, v_s5_c_im, v_s5_d, v_w_glu, v_q_norm, v_kv_norm, v_w_uq, v_w_ukv, v_w_mla_o, v_w_out, v_w_ffn_in, v_w_ffn_out, v_norm_f):
    a = dict(locals())
    D = x.shape[-1]
    me = 4 * lax.axis_index("x") + 2 * lax.axis_index("y") + lax.axis_index("c")

    shard = {n: a[n][0] for n in BIG}
    first = [n for n in BIG if n not in FFN]
    gathered = _all_gather([shard[n].astype(BF16) for n in first] + [jnp.broadcast_to(c, (8, D))], "ag_weights")
    W, goff = _model_weights(dict(zip(first, gathered[:-1])), D)
    cg = gathered[-1]
    ffn_blocks = [shard[n].astype(BF16) for n in FFN]
    ag_send, ag_recv, ag_thru, ag_token = _xchg_start(ffn_blocks, [_own_slot(b, me) for b in ffn_blocks], False, "ag_ffn_start")

    def ffn_weights(after):
        lands = _xchg_wait(ag_send, ag_recv, ag_thru, after, False, "ag_ffn_wait")
        return {n: _weight_layout(n, w8) for n, w8 in zip(FFN, lands)}

    rs_ffn = []

    def ffn_grads(g_in, g_out):
        slots = [_grad_slots(n, g) for n, g in zip(FFN, (g_in, g_out))]
        lands = [_own_slot(lax.dynamic_index_in_dim(s, me, 0, keepdims=False), me) for s in slots]
        rs_ffn.extend(_xchg_start(slots, lands, True, "rs_ffn_start"))
        return rs_ffn[3]

    wm = w_mod[0]
    ncol = wm.shape[1]
    c16 = jnp.concatenate([cg[:, 0, :], c_ctx[None], jnp.zeros((7, D), F32)], axis=0) + ag_token[:1, :1]
    (s16,) = _rowmap(jax.nn.silu, "mod_silu", 16, [c16], [], [(D, BF16)])
    m_cols = _mm(s16, wm, "nn", F32, "mm_mod")
    (mg,) = _all_gather([m_cols], "ag_mod")
    (m16,) = _rowmap(lambda m, b: m + b, "mod_bias", 16, [_cols_full(mg)], [b_mod], [(N_DEV * ncol, F32)])
    m_lat = lax.dynamic_slice(m16, (me, 0), (1, 6 * D)).reshape(6, D)
    m_ctx = m16[8].reshape(6, D)

    p = {n: a[n][0] for n in ("norm1", "norm2", "s5_a_re", "s5_a_im", "s5_log_dt", "s5_b_re", "s5_b_im", "s5_c_re",
                              "s5_c_im", "q_norm", "kv_norm")}
    p = {k: (v[None] if v.ndim == 1 else v) for k, v in p.items()}
    p["s5_d"] = s5_d.reshape(1, -1)
    p["norm_f"] = norm_f[None]
    loss_part, grad_x, small, dm_lat, dm_ctx, gW = _local_step(x[0], ctx[0], loss_target[0], m_lat, m_ctx, p, W, goff,
                                                               ffn_weights, ffn_grads)
    loss = lax.psum(loss_part[0, 0], ("x", "y", "c"))

    dm16 = jnp.concatenate([dm_lat.reshape(1, -1), dm_ctx.reshape(1, -1), jnp.zeros((14, 6 * D), F32)], axis=0)
    (dmg,) = _all_gather([dm16], "ag_dmod")
    dm_sum = _sum_slots(dmg, "sum_dmod")
    dM16 = jnp.concatenate([dmg[:, 0, :], dm_sum[1:2], jnp.zeros((7, 6 * D), F32)], axis=0)
    (g_b_mod,) = _rowmap(lambda d: jnp.sum(d, axis=0, keepdims=True), "b_mod_grad", 16, [dM16], [], [], [6 * D])
    dM_loc = lax.dynamic_slice(dM16, (0, me * ncol), (16, ncol))
    g_w_mod = _mm(s16, dM_loc, "tn", F32, "mm_mod_dw")
    ds16_part = _mm(dM_loc, wm, "nt", F32, "mm_mod_dx")

    small_names = [n for n in SMALL if n not in ("c_ctx", "b_mod")]
    small_shapes = [small[n].shape for n in small_names] + [(1, D)]
    (sg,) = _all_gather([_pack_rows([small[n] for n in small_names] + [ds16_part[8:9]], F32)], "ag_small")
    parts = _unpack_rows(_sum_slots(sg, "sum_small"), small_shapes)
    grads = dict(zip(small_names, parts[:-1]))

    def silu_bwd(cc, ds):
        _, vjp = jax.vjp(jax.nn.silu, cc)
        return vjp(ds)[0]

    (g_c_ctx,) = _rowmap(silu_bwd, "c_ctx_grad", 1, [c_ctx[None], parts[-1]], [], [(D, F32)])
    grads["c_ctx"], grads["b_mod"] = g_c_ctx, g_b_mod

    gW = dict(gW)
    n_front = w_in.shape[-1] * N_DEV - 2 * D
    gW["w_in"] = jnp.concatenate([gW["w_in"][:, :n_front], gW["w_in"][:, goff:]], axis=1)
    heads = gW["w_uq"].shape[1] // (QK_NOPE + QK_ROPE)
    gW["w_uq"] = _heads_merge(gW["w_uq"], heads, QK_NOPE)
    gW["w_ukv"] = _heads_merge(gW["w_ukv"], heads, QK_NOPE)
    slots = [_grad_slots(n, gW[n]) for n in first]
    from_sibling = _rs_pair(slots, "rs_pair")
    chip_sums = [_add_pair(pp, rr, "rs_add_" + n) for n, pp, rr in zip(first, slots, from_sibling)]
    from_chips = _rs_chips(chip_sums, "rs_chips")
    for n, g4 in zip(first, from_chips):
        grads[n] = _sum_slots(g4, "rs_sum_" + n)
    for n, g8 in zip(FFN, _xchg_wait(rs_ffn[0], rs_ffn[1], rs_ffn[2], from_chips[0], True, "rs_ffn_wait")):
        grads[n] = _sum_slots(g8, "rs_sum_" + n)
    grads["w_mod"] = g_w_mod

    out = {}
    for n in BIG + ("w_mod",):
        d, nm, nv = _adamw(a[n][0], grads[n], a["m_" + n][0], a["v_" + n][0], "adamw_" + n)
        for k, val in (("grad_", grads[n]), ("delta_", d), ("new_m_", nm), ("new_v_", nv)):
            out[k + n] = val.reshape(a[n].shape)
    packs = [_pack_rows([t[n] for n in SMALL], F32) for t in (
        {n: a[n] for n in SMALL}, {n: grads[n] for n in SMALL}, {n: a["m_" + n] for n in SMALL}, {n: a["v_" + n] for n in SMALL})]
    res = _adamw(*packs, "adamw_small")
    shapes = [a[n].shape for n in SMALL]
    for k, packed in (("grad_", packs[1]), ("delta_", res[0]), ("new_m_", res[1]), ("new_v_", res[2])):
        for n, val in zip(SMALL, _unpack_rows(packed, shapes)):
            out[k + n] = val
    return (loss, grad_x[None]) + tuple(out[k + n] for k in ("grad_", "delta_", "new_m_", "new_v_") for n in WEIGHTS)
```

```python
import functools
import math

import jax
import jax.numpy as jnp
from jax import lax
from jax.experimental import pallas as pl
from jax.experimental.pallas import tpu as pltpu

F32 = jnp.float32
BF16 = jnp.bfloat16

N_DEV = 8
N_CHIP = 4
EPS = 1e-6
GRID_W = 64
S5_GROUP = 16
QK_NOPE, QK_ROPE, V_DIM = 128, 64, 128
ROPE_BASE = 10000.0
ADAM_LR, ADAM_B1, ADAM_B2, ADAM_EPS, ADAM_WD, ADAM_STEP = 0.001, 0.9, 0.999, 1e-08, 0.01, 10

LANE = 128
SUB = 8
PACK_W = 1024
PACK_ROWS = 32
VMEM_LIMIT = 48 << 20
ROWMAP_TILE_BYTES = 10 << 20
MESH = pl.DeviceIdType.MESH
_NT = (((1,), (1,)), ((), ()))
_TN = (((0,), (0,)), ((), ()))


def _pick(dim, cands):
    for c in cands:
        if dim % c == 0:
            return c
    return dim


def _cparams(sem):
    return pltpu.CompilerParams(dimension_semantics=sem, vmem_limit_bytes=VMEM_LIMIT)


def _mm(a, b, dims, out_dtype, name, out_slots=None):
    a = a.astype(BF16)
    b = b.astype(BF16)
    b3 = b.ndim == 3
    if dims == "nn":
        (M, K), N = a.shape, (b.shape[0] * b.shape[2] if b3 else b.shape[1])
    elif dims == "nt":
        M, N = a.shape[0], b.shape[-2]
        K = b.shape[0] * b.shape[2] if b3 else b.shape[1]
    else:
        (K, M), N = a.shape, b.shape[1]
    unit_n = b.shape[2] if (b3 and dims == "nn") else (N // out_slots if out_slots else N)
    unit_k = b.shape[2] if (b3 and dims == "nt") else K
    tn = _pick(unit_n, (512, 256) + ((unit_n,) if unit_n <= 1536 else ()) + (128,))
    tk = _pick(unit_k, ((2048, 1536) if dims != "tn" else ()) + (1024, 768, 512, 256) + ((unit_k,) if unit_k <= 1536 else ())
               + (128, 64, 32, 16))
    tm = _pick(M, ((1024, 768) if tn <= 512 else ()) + (512, 256, 128, 64, 32, 16))
    nk, npt, kpt = K // tk, unit_n // tn, unit_k // tk
    if dims == "nn":
        a_spec = pl.BlockSpec((tm, tk), lambda i, j, k: (i, k))
        b_spec = (pl.BlockSpec((None, tk, tn), lambda i, j, k: (j // npt, k, j % npt)) if b3
                  else pl.BlockSpec((tk, tn), lambda i, j, k: (k, j)))
        dn = (((1,), (0,)), ((), ()))
    elif dims == "nt":
        a_spec = pl.BlockSpec((tm, tk), lambda i, j, k: (i, k))
        b_spec = (pl.BlockSpec((None, tn, tk), lambda i, j, k: (k // kpt, j, k % kpt)) if b3
                  else pl.BlockSpec((tn, tk), lambda i, j, k: (j, k)))
        dn = _NT
    else:
        a_spec = pl.BlockSpec((tk, tm), lambda i, j, k: (k, i))
        b_spec = pl.BlockSpec((tk, tn), lambda i, j, k: (k, j))
        dn = _TN
    if out_slots:
        out_spec = pl.BlockSpec((None, tm, tn), lambda i, j, k: (j // npt, i, j % npt))
        out_shape = jax.ShapeDtypeStruct((out_slots, M, unit_n), out_dtype)
    else:
        out_spec = pl.BlockSpec((tm, tn), lambda i, j, k: (i, j))
        out_shape = jax.ShapeDtypeStruct((M, N), out_dtype)

    def body(a_ref, b_ref, o_ref, acc_ref):
        k = pl.program_id(2)

        @pl.when(k == 0)
        def _():
            acc_ref[...] = jnp.zeros_like(acc_ref)

        acc_ref[...] += lax.dot_general(a_ref[...], b_ref[...], dn, preferred_element_type=F32)

        @pl.when(k == nk - 1)
        def _():
            o_ref[...] = acc_ref[...].astype(o_ref.dtype)

    return pl.pallas_call(
        body, name=name, grid=(M // tm, N // tn, nk),
        in_specs=[a_spec, b_spec], out_specs=out_spec, out_shape=out_shape,
        scratch_shapes=[pltpu.VMEM((tm, tn), F32)],
        compiler_params=_cparams(("parallel", "parallel", "arbitrary")),
    )(a, b)


def _rowmap(fn, name, M, row_ins, bc_ins, row_outs, acc_outs=()):
    row_ins = [r if isinstance(r, tuple) else (r, r.shape[1], 0) for r in row_ins]
    row_bytes = sum(w * a.dtype.itemsize for a, w, _ in row_ins) + sum(w * jnp.dtype(d).itemsize for w, d in row_outs)
    widest = max([w for _, w, _ in row_ins] + [w for w, _ in row_outs])
    row_bytes = 2 * row_bytes + 6 * 4 * widest
    tm = _pick(M, [t for t in (512, 256, 128, 64, 32, 16) if t * row_bytes <= ROWMAP_TILE_BYTES] + [16])
    n_in, n_row, n_acc = len(row_ins) + len(bc_ins), len(row_outs), len(acc_outs)

    def body(*refs):
        res = fn(*[r[...] for r in refs[:n_in]])
        res = res if isinstance(res, (tuple, list)) else (res,)
        outs = refs[n_in:]
        for k in range(n_row):
            outs[k][...] = res[k].astype(outs[k].dtype)
        if n_acc:
            @pl.when(pl.program_id(0) == 0)
            def _():
                for k in range(n_acc):
                    outs[n_row + k][...] = jnp.zeros_like(outs[n_row + k])

            for k in range(n_acc):
                outs[n_row + k][...] += res[n_row + k].astype(F32)

    in_specs = [pl.BlockSpec((tm, w), functools.partial(lambda i, blk: (i, blk), blk=blk)) for _, w, blk in row_ins]
    in_specs += [pl.BlockSpec(b.shape, lambda i: (0, 0)) for b in bc_ins]
    out_specs = [pl.BlockSpec((tm, w), lambda i: (i, 0)) for w, _ in row_outs]
    out_specs += [pl.BlockSpec((1, w), lambda i: (0, 0)) for w in acc_outs]
    out_shape = [jax.ShapeDtypeStruct((M, w), d) for w, d in row_outs]
    out_shape += [jax.ShapeDtypeStruct((1, w), F32) for w in acc_outs]
    return pl.pallas_call(
        body, name=name, grid=(M // tm,), in_specs=in_specs, out_specs=out_specs, out_shape=out_shape,
        compiler_params=_cparams(("arbitrary",) if n_acc else ("parallel",)),
    )(*[a for a, _, _ in row_ins], *bc_ins)


def _rms(x, g):
    return x * lax.rsqrt(jnp.mean(x * x, axis=-1, keepdims=True) + EPS) * g


def _normmod(x, g, sc, sh):
    return _rms(x, g) * (1.0 + sc) + sh


def _swap16(v):
    w = v.shape[1]
    lane = lax.broadcasted_iota(jnp.int32, v.shape, 1)
    return jnp.where((lane // 16) % 2 == 0, pltpu.roll(v, w - 16, 1), pltpu.roll(v, 16, 1))


def _rope(v, cos, sin_signed):
    return v * cos + _swap16(v) * sin_signed


def _rope_bwd(d, cos, sin_signed):
    return d * cos + _swap16(d * sin_signed)


def _mesh_pos():
    return lax.axis_index("x"), lax.axis_index("y"), lax.axis_index("c")


def _hbm_call(body, name, ins, out_shapes, n_sems):
    any_spec = pl.BlockSpec(memory_space=pl.ANY)
    return pl.pallas_call(
        body, name=name, out_shape=out_shapes, in_specs=[any_spec] * len(ins), out_specs=[any_spec] * len(out_shapes),
        scratch_shapes=[pltpu.SemaphoreType.DMA((n_sems,)), pltpu.SemaphoreType.DMA((n_sems,)),
                        pltpu.SemaphoreType.DMA((len(ins),))],
    )(*ins)


def _all_gather(xs, name):
    n = len(xs)

    def body(*refs):
        x_refs, out_refs, (send_sems, recv_sems, local_sems) = refs[:n], refs[n:2 * n], refs[2 * n:]
        x, y, c = _mesh_pos()
        me, sibling = (x, y, c), (x, y, 1 - c)
        chips = [(1 - x, y), (x, 1 - y), (1 - x, 1 - y)]
        locals_, first, passed, arrivals = [], [], [], []
        for a in range(n):
            def slot(px, py, pc, a=a):
                return out_refs[a].at[4 * px + 2 * py + pc]

            def copy(k, block, to, src=None, a=a, slot=slot):
                return pltpu.make_async_remote_copy(
                    src_ref=slot(*block) if src is None else src, dst_ref=slot(*block),
                    send_sem=send_sems.at[7 * a + k], recv_sem=recv_sems.at[7 * a + k], device_id=to, device_id_type=MESH)

            locals_.append(pltpu.make_async_copy(x_refs[a], slot(*me), local_sems.at[a]))
            first.append(copy(0, me, sibling, src=x_refs[a]))
            first += [copy(1 + j, me, (*chip, c), src=x_refs[a]) for j, chip in enumerate(chips)]
            passed.append([copy(4 + j, (*chip, c), sibling) for j, chip in enumerate(chips)])
            arrivals.append([copy(1 + j, (*chip, c), me) for j, chip in enumerate(chips)]
                            + [copy(0, sibling, me)] + [copy(4 + j, (*chip, 1 - c), me) for j, chip in enumerate(chips)])
        for cp in locals_ + first:
            cp.start()
        for j in range(3):
            for a in range(n):
                arrivals[a][j].wait_recv()
                passed[a][j].start()
        for a in range(n):
            for cp in arrivals[a][3:]:
                cp.wait_recv()
        for cp in first + [p for ps in passed for p in ps]:
            cp.wait_send()
        for cp in locals_:
            cp.wait()

    return _hbm_call(body, name, xs, [jax.ShapeDtypeStruct((N_DEV,) + x.shape, x.dtype) for x in xs], 7 * n)


def _rs_pair(ps, name):
    n = len(ps)

    def body(*refs):
        p_refs, out_refs, (send_sems, recv_sems, _) = refs[:n], refs[n:2 * n], refs[2 * n:]
        x, y, c = _mesh_pos()
        sends, recvs = [], []
        for a in range(n):
            for q in range(N_CHIP):
                sem = dict(send_sem=send_sems.at[4 * a + q], recv_sem=recv_sems.at[4 * a + q],
                           device_id=(x, y, 1 - c), device_id_type=MESH)
                sends.append(pltpu.make_async_remote_copy(src_ref=p_refs[a].at[2 * q + 1 - c], dst_ref=out_refs[a].at[q], **sem))
                recvs.append(pltpu.make_async_remote_copy(src_ref=p_refs[a].at[2 * q + c], dst_ref=out_refs[a].at[q], **sem))
        for cp in sends:
            cp.start()
        for cp in recvs:
            cp.wait_recv()
        for cp in sends:
            cp.wait_send()

    return _hbm_call(body, name, ps, [jax.ShapeDtypeStruct((N_CHIP,) + p.shape[1:], p.dtype) for p in ps], 4 * n)


def _rs_chips(qs, name):
    n = len(qs)

    def body(*refs):
        q_refs, out_refs, (send_sems, recv_sems, local_sems) = refs[:n], refs[n:2 * n], refs[2 * n:]
        x, y, c = _mesh_pos()
        mine = 2 * x + y
        locals_, sends, recvs = [], [], []
        for a in range(n):
            locals_.append(pltpu.make_async_copy(q_refs[a].at[mine], out_refs[a].at[mine], local_sems.at[a]))
            for r in range(1, N_CHIP):
                px = 1 - x if r & 2 else x
                py = 1 - y if r & 1 else y
                peer = 2 * px + py
                sem = dict(send_sem=send_sems.at[3 * a + r - 1], recv_sem=recv_sems.at[3 * a + r - 1],
                           device_id=(px, py, c), device_id_type=MESH)
                sends.append(pltpu.make_async_remote_copy(src_ref=q_refs[a].at[peer], dst_ref=out_refs[a].at[mine], **sem))
                recvs.append(pltpu.make_async_remote_copy(src_ref=q_refs[a].at[peer], dst_ref=out_refs[a].at[peer], **sem))
        for cp in locals_ + sends:
            cp.start()
        for cp in recvs:
            cp.wait_recv()
        for cp in sends:
            cp.wait_send()
        for cp in locals_:
            cp.wait()

    return _hbm_call(body, name, qs, [jax.ShapeDtypeStruct(q.shape, q.dtype) for q in qs], 3 * n)


def _xchg_copies(src_refs, land_refs, send_sems, recv_sems, slot_src):
    x, y, c = _mesh_pos()
    me = 4 * x + 2 * y + c
    sends, recvs = [], []
    for a, (src, land) in enumerate(zip(src_refs, land_refs)):
        for r in range(1, N_DEV):
            px = 1 - x if r & 4 else x
            py = 1 - y if r & 2 else y
            pc = 1 - c if r & 1 else c
            peer = 4 * px + 2 * py + pc
            sem = dict(send_sem=send_sems.at[7 * a + r - 1], recv_sem=recv_sems.at[7 * a + r - 1],
                       device_id=(px, py, pc), device_id_type=MESH)
            s = src.at[peer] if slot_src else src
            sends.append(pltpu.make_async_remote_copy(src_ref=s, dst_ref=land.at[me], **sem))
            recvs.append(pltpu.make_async_remote_copy(src_ref=s, dst_ref=land.at[peer], **sem))
    return sends, recvs


_HBM = pl.BlockSpec(memory_space=pltpu.HBM)
_SEM = pl.BlockSpec(memory_space=pltpu.SEMAPHORE)
_EFFECT = pltpu.SideEffectType.DATAFLOW_SIDE_EFFECTING


def _xchg_start(srcs, lands, slot_src, name):
    n = len(srcs)

    def body(*refs):
        sends, _ = _xchg_copies(refs[:n], refs[n:2 * n], refs[2 * n], refs[2 * n + 1], slot_src)
        for cp in sends:
            cp.start()
        refs[-1][...] = jnp.zeros_like(refs[-1])

    bufs = list(srcs) + list(lands)
    res = pl.pallas_call(
        body, name=name,
        out_shape=(pltpu.SemaphoreType.DMA((7 * n,)), pltpu.SemaphoreType.DMA((7 * n,)))
        + tuple(pltpu.HBM(b.shape, b.dtype) for b in bufs) + (jax.ShapeDtypeStruct((SUB, LANE), F32),),
        in_specs=(_HBM,) * (2 * n), out_specs=(_SEM, _SEM) + (_HBM,) * (2 * n) + (pl.BlockSpec(memory_space=pltpu.VMEM),),
        input_output_aliases={i: 2 + i for i in range(2 * n)},
        compiler_params=pltpu.CompilerParams(has_side_effects=_EFFECT),
    )(*[pltpu.with_memory_space_constraint(b, pltpu.HBM) for b in bufs])
    return res[0], res[1], res[2:-1], res[-1]


def _xchg_wait(send_sems, recv_sems, thru, after, slot_src, name):
    n = len(thru) // 2

    def body(*refs):
        sends, recvs = _xchg_copies(refs[:n], refs[n:2 * n], refs[2 * n], refs[2 * n + 1], slot_src)
        for cp in sends:
            cp.wait_send()
        for cp in recvs:
            cp.wait_recv()

    res = pl.pallas_call(
        body, name=name, out_shape=tuple(pltpu.HBM(b.shape, b.dtype) for b in thru),
        in_specs=(_HBM,) * (2 * n) + (_SEM, _SEM, pl.BlockSpec(memory_space=pl.ANY)), out_specs=(_HBM,) * (2 * n),
        input_output_aliases={i: i for i in range(2 * n)},
        compiler_params=pltpu.CompilerParams(has_side_effects=_EFFECT),
    )(*thru, send_sems, recv_sems, after)
    return res[n:]


def _own_slot(block, me):
    return lax.dynamic_update_slice(lax.empty((N_DEV,) + block.shape, block.dtype), block[None], (me, 0, 0))


def _add_pair(p, r, name):
    _, R, C = p.shape
    tr = _pick(R, (512, 256, 128, 64, 32, 16))

    def body(c_ref, p_ref, r_ref, o_ref):
        o_ref[...] = (p_ref[...].astype(F32) + r_ref[...].astype(F32)).astype(o_ref.dtype)

    return pl.pallas_call(
        body, name=name, out_shape=jax.ShapeDtypeStruct((N_CHIP, R, C), p.dtype),
        grid_spec=pltpu.PrefetchScalarGridSpec(
            num_scalar_prefetch=1, grid=(N_CHIP, R // tr),
            in_specs=[pl.BlockSpec((None, None, tr, C), lambda q, i, c_ref: (q, c_ref[0], i, 0)),
                      pl.BlockSpec((None, tr, C), lambda q, i, c_ref: (q, i, 0))],
            out_specs=pl.BlockSpec((None, tr, C), lambda q, i, c_ref: (q, i, 0))),
        compiler_params=_cparams(("parallel", "parallel")),
    )(lax.axis_index("c").reshape(1).astype(jnp.int32), p.reshape(N_CHIP, 2, R, C), r)


def _sum_slots(g, name):
    ns, R, C = g.shape
    tr = _pick(R, (256, 128, 64, 32, 16))

    def body(g_ref, o_ref):
        acc = g_ref[0].astype(F32)
        for j in range(1, ns):
            acc = acc + g_ref[j].astype(F32)
        o_ref[...] = acc

    return pl.pallas_call(
        body, name=name, grid=(R // tr,),
        in_specs=[pl.BlockSpec((ns, tr, C), lambda i: (0, i, 0))], out_specs=pl.BlockSpec((tr, C), lambda i: (i, 0)),
        out_shape=jax.ShapeDtypeStruct((R, C), F32), compiler_params=_cparams(("parallel",)),
    )(g)


def _pack_rows(arrs, dtype):
    parts = []
    for a in arrs:
        flat = a.reshape(-1).astype(dtype)
        pad = (-flat.shape[0]) % (PACK_W * 16)
        parts.append(jnp.pad(flat, (0, pad)).reshape(-1, PACK_W))
    out = jnp.concatenate(parts, axis=0)
    return jnp.pad(out, ((0, (-out.shape[0]) % PACK_ROWS), (0, 0)))


def _packed_rows(shape):
    n = math.prod(shape)
    return (n + PACK_W * 16 - 1) // (PACK_W * 16) * 16


def _unpack_rows(packed, shapes):
    out, r0 = [], 0
    for s in shapes:
        rows, n = _packed_rows(s), math.prod(s)
        out.append(packed[r0:r0 + rows].reshape(rows * PACK_W)[:n].reshape(s))
        r0 += rows
    return out


def _adamw_math(w, g, m, v):
    m = ADAM_B1 * m + (1.0 - ADAM_B1) * g
    v = ADAM_B2 * v + (1.0 - ADAM_B2) * (g * g)
    m_hat = m / (1.0 - ADAM_B1 ** ADAM_STEP)
    v_hat = v / (1.0 - ADAM_B2 ** ADAM_STEP)
    delta = -ADAM_LR * (m_hat / (jnp.sqrt(v_hat) + ADAM_EPS) + ADAM_WD * w)
    return delta, m, v


def _adamw(w, g, m, v, name):
    R, C = w.shape
    return _rowmap(_adamw_math, name, R, [w, g, m, v], [], [(C, F32)] * 3)


def _s5_disc_math(lr, li, ldt, br, bi):
    dt = jnp.exp(ldt)
    mag = jnp.exp(lr * dt)
    ab_re, ab_im = mag * jnp.cos(li * dt), mag * jnp.sin(li * dt)
    den = lr * lr + li * li
    nr, ni = ab_re - 1.0, ab_im
    co_re = (nr * lr + ni * li) / den
    co_im = (ni * lr - nr * li) / den
    bb_re = co_re * br - co_im * bi
    bb_im = co_re * bi + co_im * br
    return ab_re, ab_im, bb_re, bb_im


def _s5_tables(a_re, a_im, ldt, b_re, b_im, c_re, c_im):
    _, G, P, N = b_re.shape
    nch = G // 8

    def body(lr_ref, li_ref, ldt_ref, br_ref, bi_ref, cr_ref, ci_ref, wre, wim, vre, vim, pwr, pwi):
        ar, ai, bb_re, bb_im = _s5_disc_math(lr_ref[0], li_ref[0], ldt_ref[0], br_ref[0], bi_ref[0])
        cr, ci = cr_ref[0], ci_ref[0]
        pr, pi = jnp.ones_like(ar), jnp.zeros_like(ar)
        for j in range(SUB + 1):
            pwr[0, j], pwi[0, j] = pr, pi
            if j < SUB:
                tabs = ((wre, bb_re * pr - bb_im * pi), (wim, bb_re * pi + bb_im * pr),
                        (vre, cr * pr - ci * pi), (vim, -(cr * pi + ci * pr)))
                for ref, val in tabs:
                    for s in range(nch):
                        ref[0, s, pl.ds(j * LANE, LANE), :] = val[s * 8:(s + 1) * 8].reshape(LANE, N).astype(BF16)
            pr, pi = pr * ar - pi * ai, pr * ai + pi * ar

    g1n = pl.BlockSpec((1, G, 1, N), lambda d: (d, 0, 0, 0))
    gpn = pl.BlockSpec((1, G, P, N), lambda d: (d, 0, 0, 0))
    tab = pl.BlockSpec((1, nch, SUB * LANE, N), lambda d: (d, 0, 0, 0))
    pw = pl.BlockSpec((1, SUB + 1, G, 1, N), lambda d: (d, 0, 0, 0, 0))
    s_tab = jax.ShapeDtypeStruct((2, nch, SUB * LANE, N), BF16)
    s_pw = jax.ShapeDtypeStruct((2, SUB + 1, G, 1, N), F32)
    return pl.pallas_call(
        body, name="s5_tables", grid=(2,),
        in_specs=[g1n, g1n, pl.BlockSpec((1, G, 1, 1), lambda d: (d, 0, 0, 0)), gpn, gpn, gpn, gpn],
        out_specs=[tab] * 4 + [pw] * 2, out_shape=[s_tab] * 4 + [s_pw] * 2,
        compiler_params=_cparams(("parallel",)),
    )(a_re, a_im, ldt, b_re, b_im, c_re, c_im)


def _s5_expand(t_re, t_im, name):
    _, nch, R, N = t_re.shape
    sw = 8 * N

    def body(re_ref, im_ref, o_ref):
        spread = (lax.broadcasted_iota(jnp.int32, (N, sw), 1) % N == lax.broadcasted_iota(jnp.int32, (N, sw), 0)).astype(BF16)
        row_g = (lax.broadcasted_iota(jnp.int32, (R, sw), 0) % LANE) // S5_GROUP
        keep = row_g == lax.broadcasted_iota(jnp.int32, (R, sw), 1) // N
        for half, ref in enumerate((re_ref, im_ref)):
            t = jnp.dot(ref[0, 0], spread, preferred_element_type=F32)
            o_ref[0, 0, :, pl.ds(half * sw, sw)] = jnp.where(keep, t, 0.0).astype(BF16)

    spec = pl.BlockSpec((1, 1, R, N), lambda d, s: (d, s, 0, 0))
    return pl.pallas_call(
        body, name=name, grid=(2, nch), in_specs=[spec, spec],
        out_specs=pl.BlockSpec((1, 1, R, 2 * sw), lambda d, s: (d, s, 0, 0)),
        out_shape=jax.ShapeDtypeStruct((2, nch, R, 2 * sw), BF16), compiler_params=_cparams(("parallel", "parallel")),
    )(t_re, t_im)


def _s5_param_bwd(a_re, a_im, ldt, b_re, b_im, da_re, da_im, dbb_re, dbb_im):
    _, G, P, N = b_re.shape

    def body(lr_ref, li_ref, ldt_ref, br_ref, bi_ref, dar, dai, dbr, dbi, o_lr, o_li, o_ldt, o_br, o_bi):
        _, vjp = jax.vjp(_s5_disc_math, lr_ref[0], li_ref[0], ldt_ref[0], br_ref[0], bi_ref[0])
        o_lr[0], o_li[0], o_ldt[0], o_br[0], o_bi[0] = vjp((dar[0], dai[0], dbr[0], dbi[0]))

    g1n = pl.BlockSpec((1, G, 1, N), lambda d: (d, 0, 0, 0))
    g11 = pl.BlockSpec((1, G, 1, 1), lambda d: (d, 0, 0, 0))
    gpn = pl.BlockSpec((1, G, P, N), lambda d: (d, 0, 0, 0))
    s_g1n, s_g11, s_gpn = (jax.ShapeDtypeStruct(s, F32) for s in ((2, G, 1, N), (2, G, 1, 1), (2, G, P, N)))
    return pl.pallas_call(
        body, name="s5_param_bwd", grid=(2,),
        in_specs=[g1n, g1n, g11, gpn, gpn, g1n, g1n, gpn, gpn], out_specs=[g1n, g1n, g11, gpn, gpn],
        out_shape=[s_g1n, s_g1n, s_g11, s_gpn, s_gpn], compiler_params=_cparams(("parallel",)),
    )(a_re, a_im, ldt, b_re, b_im, da_re, da_im, dbb_re, dbb_im)


def _shift_stack(u, back):
    tb = u.shape[0]
    tau = lax.broadcasted_iota(jnp.int32, u.shape, 0) % SUB
    parts = [u]
    for j in range(1, SUB):
        if back:
            parts.append(jnp.where(tau >= j, pltpu.roll(u, j, 0), 0.0))
        else:
            parts.append(jnp.where(tau <= SUB - 1 - j, pltpu.roll(u, tb - j, 0), 0.0))
    return jnp.concatenate(parts, axis=1).astype(BF16)


def _cmul_add(tile, pw, carry, sw):
    pr, pi, cr, ci = pw[:, :sw], pw[:, sw:], carry[:, :sw], carry[:, sw:]
    return tile + jnp.concatenate([pr * cr - pi * ci, pr * ci + pi * cr], axis=1)


def _tile_scan(buf, base, ntile, pw, carry, sw, causal):
    def step(k, c):
        i = k if causal else ntile - 1 - k
        r = pl.multiple_of(base + i * SUB, SUB)
        tile = _cmul_add(buf[pl.ds(r, SUB), :], pw, c, sw)
        buf[pl.ds(r, SUB), :] = tile
        return tile[SUB - 1:SUB, :] if causal else tile[0:1, :]

    return lax.fori_loop(0, ntile, step, carry)


def _s5_fwd(h_all, waug, vaug, pw, S5W, T, causal, name):
    S = h_all.shape[0]
    nch, _, sw2 = waug.shape
    sw = sw2 // 2
    tb = _pick(math.gcd(T, S - T), (256, 128, 64, 32, 16))
    ntile, nt, off = tb // SUB, S // tb, T // tb
    rb = (lambda s, t: ((t + off) % nt, s)) if causal else (lambda s, t: (nt - 1 - t, s))

    def body(u_ref, w_ref, v_ref, p_ref, y_ref, h_ref, hblk, carry):
        @pl.when(pl.program_id(1) == 0)
        def _():
            carry[...] = jnp.zeros_like(carry)

        hblk[...] = jnp.dot(_shift_stack(u_ref[...], causal), w_ref[...], preferred_element_type=F32)
        carry[...] = _tile_scan(hblk, 0, ntile, p_ref[...], carry[...], sw, causal)
        hb = hblk[...].astype(BF16)
        h_ref[...] = hb
        y_ref[...] = lax.dot_general(hb, v_ref[...], _NT, preferred_element_type=F32)

    return pl.pallas_call(
        body, name=name, grid=(nch, nt),
        in_specs=[pl.BlockSpec((tb, LANE), rb),
                  pl.BlockSpec((None, SUB * LANE, sw2), lambda s, t: (s, 0, 0)),
                  pl.BlockSpec((None, LANE, sw2), lambda s, t: (s, 0, 0)),
                  pl.BlockSpec((None, SUB, sw2), lambda s, t: (s, 0, 0))],
        out_specs=[pl.BlockSpec((tb, LANE), rb), pl.BlockSpec((tb, sw2), rb)],
        out_shape=[jax.ShapeDtypeStruct((S, S5W), F32), jax.ShapeDtypeStruct((S, nch * sw2), BF16)],
        scratch_shapes=[pltpu.VMEM((tb, sw2), F32), pltpu.VMEM((1, sw2), F32)],
        compiler_params=_cparams(("parallel", "arbitrary")),
    )(h_all, waug, vaug, pw)


def _s5_bwd(dy_all, h_all, hs, waug, vaug, pwc, S5W, T, causal, name):
    S = h_all.shape[0]
    nch, _, sw2 = waug.shape
    sw = sw2 // 2
    tb = _pick(math.gcd(T, S - T), (256, 128, 64, 32, 16))
    ntile, nt, off = tb // SUB, S // tb, T // tb
    rb = (lambda s, t: ((nt - 1 - t + off) % nt, s)) if causal else (lambda s, t: (t, s))
    adj_causal = not causal
    edge = SUB - 1 if adj_causal else SUB + tb
    keep_src, keep_dst = (tb, 0) if adj_causal else (SUB, SUB + tb)

    def body(dy_ref, u_ref, h_ref, w_ref, v_ref, p_ref, du_ref, dbb_ref, dc_ref, da_ref, lam):
        @pl.when(pl.program_id(1) == 0)
        def _():
            lam[pl.ds(0, SUB), :] = jnp.zeros((SUB, sw2), F32)
            lam[pl.ds(SUB + tb, SUB), :] = jnp.zeros((SUB, sw2), F32)
            dbb_ref[...] = jnp.zeros_like(dbb_ref)
            dc_ref[...] = jnp.zeros_like(dc_ref)
            da_ref[...] = jnp.zeros_like(da_ref)

        dy = dy_ref[...]
        lam[pl.ds(SUB, tb), :] = jnp.dot(_shift_stack(dy, adj_causal), v_ref[...], preferred_element_type=F32)
        _tile_scan(lam, SUB, ntile, p_ref[...], lam[pl.ds(edge, 1), :], sw, adj_causal)
        lb = lam[pl.ds(SUB, tb), :].astype(BF16)
        du_ref[...] = lax.dot_general(lb, w_ref[...], _NT, preferred_element_type=F32)
        dbb_ref[...] += lax.dot_general(u_ref[...].astype(BF16), lb, _TN, preferred_element_type=F32)
        dc_ref[...] += lax.dot_general(h_ref[...], dy.astype(BF16), _TN, preferred_element_type=F32)
        h = h_ref[...].astype(F32)
        ln = lam[pl.ds(SUB + 1 if causal else SUB - 1, tb), :]
        hr, hi, lr, li = h[:, :sw], h[:, sw:], ln[:, :sw], ln[:, sw:]
        da_ref[...] += jnp.concatenate([jnp.sum(hr * lr + hi * li, axis=0, keepdims=True),
                                        jnp.sum(hr * li - hi * lr, axis=0, keepdims=True)], axis=1)
        lam[pl.ds(keep_dst, SUB), :] = lam[pl.ds(keep_src, SUB), :]

    fixed = lambda s, t: (s, 0, 0)
    return pl.pallas_call(
        body, name=name, grid=(nch, nt),
        in_specs=[pl.BlockSpec((tb, LANE), rb), pl.BlockSpec((tb, LANE), rb), pl.BlockSpec((tb, sw2), rb),
                  pl.BlockSpec((None, LANE, sw2), fixed), pl.BlockSpec((None, SUB * LANE, sw2), fixed),
                  pl.BlockSpec((None, SUB, sw2), fixed)],
        out_specs=[pl.BlockSpec((tb, LANE), rb), pl.BlockSpec((None, LANE, sw2), fixed),
                   pl.BlockSpec((None, sw2, LANE), fixed), pl.BlockSpec((None, 1, sw2), fixed)],
        out_shape=[jax.ShapeDtypeStruct((S, S5W), F32), jax.ShapeDtypeStruct((nch, LANE, sw2), F32),
                   jax.ShapeDtypeStruct((nch, sw2, LANE), F32), jax.ShapeDtypeStruct((nch, 1, sw2), F32)],
        scratch_shapes=[pltpu.VMEM((tb + 2 * SUB, sw2), F32)],
        compiler_params=_cparams(("parallel", "arbitrary")),
    )(dy_all, h_all, hs, waug, vaug, pwc)


def _attn_fwd(q3, k3, v3, scale):
    H, T, dk = q3.shape
    S, dv = k3.shape[1], v3.shape[2]
    tq = _pick(T, (256, 128, 64, 32, 16))

    def body(q_ref, k_ref, v_ref, o_ref, lse_ref):
        s = lax.dot_general(q_ref[0], k_ref[0], _NT, preferred_element_type=F32) * scale
        m = jnp.max(s, axis=1, keepdims=True)
        p = jnp.exp(s - m)
        l = jnp.sum(p, axis=1, keepdims=True)
        o_ref[...] = jnp.dot((p / l).astype(BF16), v_ref[0], preferred_element_type=F32).astype(o_ref.dtype)
        lse_ref[0] = m + jnp.log(l)

    return pl.pallas_call(
        body, name="attn_fwd", grid=(H, T // tq),
        in_specs=[pl.BlockSpec((1, tq, dk), lambda h, i: (h, i, 0)), pl.BlockSpec((1, S, dk), lambda h, i: (h, 0, 0)),
                  pl.BlockSpec((1, S, dv), lambda h, i: (h, 0, 0))],
        out_specs=[pl.BlockSpec((tq, dv), lambda h, i: (i, h)), pl.BlockSpec((1, tq, 1), lambda h, i: (h, i, 0))],
        out_shape=[jax.ShapeDtypeStruct((T, H * dv), BF16), jax.ShapeDtypeStruct((H, T, 1), F32)],
        compiler_params=_cparams(("parallel", "parallel")),
    )(q3, k3, v3)


def _attn_bwd(q3, k3, v3, do, lse, scale):
    H, T, dk = q3.shape
    S, dv = k3.shape[1], v3.shape[2]
    tq = _pick(T, (256, 128, 64, 32, 16))

    def body(q_ref, k_ref, v_ref, do_ref, lse_ref, dq_ref, dk_ref, dv_ref):
        @pl.when(pl.program_id(1) == 0)
        def _():
            dk_ref[...] = jnp.zeros_like(dk_ref)
            dv_ref[...] = jnp.zeros_like(dv_ref)

        q, k, v, d_o = q_ref[0], k_ref[0], v_ref[0], do_ref[...]
        s = lax.dot_general(q, k, _NT, preferred_element_type=F32) * scale
        p = jnp.exp(s - lse_ref[0])
        dv_ref[0] += lax.dot_general(p.astype(BF16), d_o, _TN, preferred_element_type=F32)
        dp = lax.dot_general(d_o, v, _NT, preferred_element_type=F32)
        ds = (p * (dp - jnp.sum(p * dp, axis=1, keepdims=True)) * scale).astype(BF16)
        dq_ref[0] = jnp.dot(ds, k, preferred_element_type=F32)
        dk_ref[0] += lax.dot_general(ds, q, _TN, preferred_element_type=F32)

    return pl.pallas_call(
        body, name="attn_bwd", grid=(H, T // tq),
        in_specs=[pl.BlockSpec((1, tq, dk), lambda h, i: (h, i, 0)), pl.BlockSpec((1, S, dk), lambda h, i: (h, 0, 0)),
                  pl.BlockSpec((1, S, dv), lambda h, i: (h, 0, 0)), pl.BlockSpec((tq, dv), lambda h, i: (i, h)),
                  pl.BlockSpec((1, tq, 1), lambda h, i: (h, i, 0))],
        out_specs=[pl.BlockSpec((1, tq, dk), lambda h, i: (h, i, 0)), pl.BlockSpec((1, S, dk), lambda h, i: (h, 0, 0)),
                   pl.BlockSpec((1, S, dv), lambda h, i: (h, 0, 0))],
        out_shape=[jax.ShapeDtypeStruct((H, T, dk), F32), jax.ShapeDtypeStruct((H, S, dk), F32),
                   jax.ShapeDtypeStruct((H, S, dv), F32)],
        compiler_params=_cparams(("parallel", "arbitrary")),
    )(q3, k3, v3, do, lse)


def _rope_tables(T, heads):
    rows = T // GRID_W
    row = jnp.repeat(jnp.arange(rows, dtype=F32), GRID_W)
    col = jnp.tile(jnp.arange(GRID_W, dtype=F32), rows)
    n_freq = QK_ROPE // 4
    inv = ROPE_BASE ** (-jnp.arange(n_freq, dtype=F32) / n_freq)
    ar, ac = row[:, None] * inv, col[:, None] * inv
    cos = jnp.concatenate([jnp.cos(ar), jnp.cos(ar), jnp.cos(ac), jnp.cos(ac)], axis=1)
    sin = jnp.concatenate([-jnp.sin(ar), jnp.sin(ar), -jnp.sin(ac), jnp.sin(ac)], axis=1)
    return jnp.tile(cos, (1, heads)), jnp.tile(sin, (1, heads))


def _dw(a, dy, w, name):
    return _mm(a, dy, "tn", BF16, name, out_slots=w.shape[0] if w.ndim == 3 else None)


def _local_step(x, ctx, tgt, m_lat, m_ctx, p, W, goff, ffn_weights=None, send_grads=None):
    T, D = x.shape
    Tc = ctx.shape[0]
    S = T + Tc
    S5W = p["s5_d"].shape[1]
    QR, KVR = p["q_norm"].shape[1], p["kv_norm"].shape[1]
    H = W["w_uq"].shape[1] // (QK_NOPE + QK_ROPE)
    G, N = p["s5_a_re"].shape[1:]
    P = S5_GROUP
    nch = G // 8
    o_cq, o_ckv, o_kr = S5W, S5W + QR, S5W + QR + KVR
    assert o_cq % QR == 0 and o_ckv % KVR == 0 and o_kr % LANE == 0 and goff % D == 0 and S5W % LANE == 0 and G % 8 == 0
    assert 8 * P == LANE
    row = lambda k, m: m[k:k + 1]
    sh1, sc1, g1, sh2, sc2, g2 = (row(k, m_lat) for k in range(6))
    csh1, csc1 = row(0, m_ctx), row(1, m_ctx)
    n1, n2, nf = p["norm1"], p["norm2"], p["norm_f"]

    (xm_lat,) = _rowmap(_normmod, "norm1_lat", T, [x], [n1, sc1, sh1], [(D, BF16)])
    (xm_ctx,) = _rowmap(_normmod, "norm1_ctx", Tc, [ctx], [n1, csc1, csh1], [(D, BF16)])
    xm_all = jnp.concatenate([xm_lat, xm_ctx], axis=0)
    h_all = _mm(xm_all, W["w_in"], "nn", F32, "mm_in")

    a_re, a_im = p["s5_a_re"][:, :, None, :], p["s5_a_im"][:, :, None, :]
    ldt = p["s5_log_dt"][:, :, None, None]
    b_re, b_im = p["s5_b_re"].transpose(0, 1, 3, 2), p["s5_b_im"].transpose(0, 1, 3, 2)
    wre, wim, vre, vim, pwr, pwi = _s5_tables(a_re, a_im, ldt, b_re, b_im, p["s5_c_re"], p["s5_c_im"])
    waug = _s5_expand(wre, wim, "s5_expand_b")
    vaug = _s5_expand(vre, vim, "s5_expand_c")
    lanes = lambda t: t.reshape(2, SUB + 1, nch, 8 * N).transpose(0, 2, 1, 3)
    pw_re, pw_im = lanes(pwr), lanes(pwi)
    near = lambda t: t[:, :, 1:]
    far = lambda t: t[:, :, :0:-1]
    pw_c = jnp.concatenate([near(pw_re), near(pw_im)], axis=-1)
    pw_a = jnp.concatenate([far(pw_re), far(pw_im)], axis=-1)
    pwc_c = jnp.concatenate([near(pw_re), -near(pw_im)], axis=-1)
    pwc_a = jnp.concatenate([far(pw_re), -far(pw_im)], axis=-1)
    y0, hs0 = _s5_fwd(h_all, waug[0], vaug[0], pw_c[0], S5W, T, True, "s5_scan_fwd0")
    y1, hs1 = _s5_fwd(h_all, waug[1], vaug[1], pw_a[1], S5W, T, False, "s5_scan_fwd1")

    def s5_combine(u, yf, yr, dskip):
        y5 = dskip * u + yf + yr
        return y5, jax.nn.gelu(y5)

    y5, z = _rowmap(s5_combine, "s5_combine", T, [(h_all, S5W, 0), y0, y1], [p["s5_d"]], [(S5W, F32), (S5W, BF16)])

    (qn,) = _rowmap(_rms, "q_norm", T, [(h_all, QR, o_cq // QR)], [p["q_norm"]], [(QR, BF16)])
    (kvn,) = _rowmap(_rms, "kv_norm", S, [(h_all, KVR, o_ckv // KVR)], [p["kv_norm"]], [(KVR, BF16)])
    qraw = _mm(qn, W["w_uq"], "nn", F32, "mm_uq")
    kvraw = _mm(kvn, W["w_ukv"], "nn", BF16, "mm_ukv")
    cos_q, sin_q = _rope_tables(T, H)
    padl = lambda t: jnp.pad(t[:, :QK_ROPE], ((0, Tc), (0, LANE - QK_ROPE)))
    cos_k = padl(cos_q) + jnp.pad(jnp.ones((Tc, LANE), F32), ((T, 0), (0, 0)))
    sin_k = padl(sin_q)
    hn = H * QK_NOPE

    def q_post(q, cos, sin):
        return q[:, :hn], _rope(q[:, hn:], cos, sin)

    q_nope, q_rope = _rowmap(q_post, "q_rope", T, [qraw, cos_q, sin_q], [], [(hn, BF16), (H * QK_ROPE, BF16)])
    (kr,) = _rowmap(_rope, "k_rope", S, [(h_all, LANE, o_kr // LANE), cos_k, sin_k], [], [(LANE, BF16)])
    q3 = jnp.concatenate([q_nope.reshape(T, H, QK_NOPE), q_rope.reshape(T, H, QK_ROPE)], axis=-1).transpose(1, 0, 2)
    k3 = jnp.concatenate([kvraw[:, :hn].reshape(S, H, QK_NOPE),
                          jnp.broadcast_to(kr[:, None, :QK_ROPE], (S, H, QK_ROPE))], axis=-1).transpose(1, 0, 2)
    v3 = kvraw[:, hn:].reshape(S, H, V_DIM).transpose(1, 0, 2)
    scale = (QK_NOPE + QK_ROPE) ** -0.5
    o, lse = _attn_fwd(q3, k3, v3, scale)

    zz = _mm(z, W["w_glu"], "nn", F32, "mm_glu")
    br_mla = _mm(o, W["w_mla_o"], "nn", F32, "mm_mla_o")

    def merge(zz, brm, gs, gm):
        a, b = zz[:, :D], zz[:, D:]
        return jax.nn.sigmoid(gs) * (a * jax.nn.sigmoid(b)) + jax.nn.sigmoid(gm) * brm

    gb = goff // D
    merge_ins = [zz, br_mla, (h_all, D, gb), (h_all, D, gb + 1)]
    (mix,) = _rowmap(merge, "merge", T, merge_ins, [], [(D, BF16)])
    out1 = _mm(mix, W["w_out"], "nn", F32, "mm_out")

    def resid_norm2(x, out1, g1, n2, sc2, sh2):
        x1 = x + g1 * out1
        return x1, _normmod(x1, n2, sc2, sh2)

    x1, hm = _rowmap(resid_norm2, "resid_norm2", T, [x, out1], [g1, n2, sc2, sh2], [(D, F32), (D, BF16)])

    if ffn_weights is not None:
        W = {**W, **ffn_weights(hm)}
    FF = W["w_ffn_out"].shape[0]
    assert FF % LANE == 0
    ab = _mm(hm, W["w_ffn_in"], "nn", F32, "mm_ffn_in")

    def swiglu_act(a, b):
        return jax.nn.silu(a) * b

    (f,) = _rowmap(swiglu_act, "ffn_act", T, [(ab, FF, 0), (ab, FF, 1)], [], [(FF, BF16)])
    out2 = _mm(f, W["w_ffn_out"], "nn", F32, "mm_ffn_out")

    def loss_rows(x1, out2, g2, nf, tgt):
        y = _rms(x1 + g2 * out2, nf)
        return 0.5 * jnp.sum(jnp.mean(jnp.square(y - tgt), axis=-1))

    def final(x1, out2, tgt, g2, nf):
        val, (dx1, dout2, dg2, dnf) = jax.value_and_grad(loss_rows, argnums=(0, 1, 2, 3))(x1, out2, g2, nf, tgt)
        return dx1, dout2, jnp.full((1, LANE), val, F32), dg2, dnf

    dx2, dout2, loss_acc, dg2, dnf = _rowmap(final, "final_loss", T, [x1, out2, tgt], [g2, nf],
                                             [(D, F32), (D, BF16)], [LANE, D, D])

    gW = {}
    df = _mm(dout2, W["w_ffn_out"], "nt", F32, "mm_ffn_out_dx")
    gW["w_ffn_out"] = _dw(f, dout2, W["w_ffn_out"], "mm_ffn_out_dw")

    def swiglu_bwd(a, b, df):
        _, vjp = jax.vjp(swiglu_act, a, b)
        da, db = vjp(df)
        return jnp.concatenate([da, db], axis=1)

    (dab,) = _rowmap(swiglu_bwd, "ffn_act_bwd", T, [(ab, FF, 0), (ab, FF, 1), df], [], [(2 * FF, BF16)])
    dhm = _mm(dab, W["w_ffn_in"], "nt", F32, "mm_ffn_in_dx")
    gW["w_ffn_in"] = _dw(hm, dab, W["w_ffn_in"], "mm_ffn_in_dw")
    if send_grads is not None:
        token = send_grads(FFN, [gW.pop(n) for n in FFN])
        g1 = g1 if token is None else g1 + token[:1, :1]

    def resid_norm2_bwd(x, out1, dx2, dhm, g1, n2, sc2, sh2):
        _, vjp = jax.vjp(resid_norm2, x, out1, g1, n2, sc2, sh2)
        dx, dout1, dg1, dn2, dsc2, dsh2 = vjp((dx2, dhm))
        return dx, dout1, dg1, dn2, dsc2, dsh2

    dx1, dout1, dg1, dn2, dsc2, dsh2 = _rowmap(resid_norm2_bwd, "resid_norm2_bwd", T, [x, out1, dx2, dhm],
                                               [g1, n2, sc2, sh2], [(D, F32), (D, BF16)], [D, D, D, D])

    dmix = _mm(dout1, W["w_out"], "nt", F32, "mm_out_dx")
    gW["w_out"] = _dw(mix, dout1, W["w_out"], "mm_out_dw")

    def merge_bwd(zz, brm, gs, gm, dmix):
        _, vjp = jax.vjp(merge, zz, brm, gs, gm)
        dzz, dbrm, dgs, dgm = vjp(dmix)
        return dzz, dbrm, jnp.concatenate([dgs, dgm], axis=1)

    dzz, dbrm, dgates = _rowmap(merge_bwd, "merge_bwd", T, merge_ins + [dmix], [],
                                [(2 * D, BF16), (D, BF16), (2 * D, BF16)])
    do = _mm(dbrm, W["w_mla_o"], "nt", BF16, "mm_mla_o_dx")
    gW["w_mla_o"] = _dw(o, dbrm, W["w_mla_o"], "mm_mla_o_dw")
    dz = _mm(dzz, W["w_glu"], "nt", F32, "mm_glu_dx")
    gW["w_glu"] = _dw(z, dzz, W["w_glu"], "mm_glu_dw")
    d_skip_w = p["s5_d"]
    if send_grads is not None:
        token = send_grads(MIX, [gW.pop(n) for n in MIX])
        d_skip_w = d_skip_w if token is None else d_skip_w + token[:1, :1]

    def s5_combine_bwd(u, y5, dz, dskip):
        _, vjp = jax.vjp(lambda y: jax.nn.gelu(y), y5)
        (dy5,) = vjp(dz)
        return dy5, jnp.sum(dy5 * u, axis=0, keepdims=True)

    dy5, d_skip = _rowmap(s5_combine_bwd, "s5_combine_bwd", T, [(h_all, S5W, 0), y5, dz], [d_skip_w], [(S5W, F32)], [S5W])

    dq3, dk3, dv3 = _attn_bwd(q3, k3, v3, do, lse, scale)
    dq_t = dq3.transpose(1, 0, 2)
    dq_cat = jnp.concatenate([dq_t[:, :, :QK_NOPE].reshape(T, hn), dq_t[:, :, QK_NOPE:].reshape(T, H * QK_ROPE)], axis=1)

    def q_post_bwd(dq, cos, sin):
        return jnp.concatenate([dq[:, :hn], _rope_bwd(dq[:, hn:], cos, sin)], axis=1)

    (dqraw,) = _rowmap(q_post_bwd, "q_rope_bwd", T, [dq_cat, cos_q, sin_q], [], [(H * (QK_NOPE + QK_ROPE), BF16)])
    dk_t = dk3.transpose(1, 0, 2)
    dkvraw = jnp.concatenate([dk_t[:, :, :QK_NOPE].reshape(S, hn), dv3.transpose(1, 0, 2).reshape(S, H * V_DIM)], axis=1)
    dkr_heads = jnp.pad(dk_t[:, :, QK_NOPE:], ((0, 0), (0, 0), (0, LANE - QK_ROPE))).reshape(S, H * LANE)

    def k_rope_bwd(dkh, cos, sin):
        d = dkh[:, :LANE]
        for h in range(1, H):
            d = d + dkh[:, h * LANE:(h + 1) * LANE]
        return _rope_bwd(d, cos, sin)

    (dkr,) = _rowmap(k_rope_bwd, "k_rope_bwd", S, [dkr_heads, cos_k, sin_k], [], [(LANE, BF16)])
    dqn = _mm(dqraw, W["w_uq"], "nt", F32, "mm_uq_dx")
    gW["w_uq"] = _dw(qn, dqraw, W["w_uq"], "mm_uq_dw")
    dkvn = _mm(dkvraw, W["w_ukv"], "nt", F32, "mm_ukv_dx")
    gW["w_ukv"] = _dw(kvn, dkvraw, W["w_ukv"], "mm_ukv_dw")

    def rms_bwd(cx, dn, g):
        _, vjp = jax.vjp(_rms, cx, g)
        return vjp(dn)

    dcq, dq_norm = _rowmap(rms_bwd, "q_norm_bwd", T, [(h_all, QR, o_cq // QR), dqn], [p["q_norm"]], [(QR, BF16)], [QR])
    dckv, dkv_norm = _rowmap(rms_bwd, "kv_norm_bwd", S, [(h_all, KVR, o_ckv // KVR), dkvn], [p["kv_norm"]],
                             [(KVR, BF16)], [KVR])

    dy_all = jnp.concatenate([dy5, jnp.zeros((Tc, S5W), F32)], axis=0)
    du0, dbb0, dc0, da0 = _s5_bwd(dy_all, h_all, hs0, waug[0], vaug[0], pwc_a[0], S5W, T, True, "s5_scan_bwd0")
    du1, dbb1, dc1, da1 = _s5_bwd(dy_all, h_all, hs1, waug[1], vaug[1], pwc_c[1], S5W, T, False, "s5_scan_bwd1")

    def du_combine(a, b, dy, dskip):
        return a + b + dskip * dy

    (du_all,) = _rowmap(du_combine, "s5_du", S, [du0, du1, dy_all], [p["s5_d"]], [(S5W, BF16)])
    dbb = jnp.einsum("dsgpcgn->dcsgpn", jnp.stack([dbb0, dbb1]).reshape(2, nch, 8, P, 2, 8, N)).reshape(2, 2, G, P, N)
    dcm = jnp.einsum("dscgngp->dcsgpn", jnp.stack([dc0, dc1]).reshape(2, nch, 2, 8, N, 8, P)).reshape(2, 2, G, P, N)
    da = jnp.stack([da0, da1]).reshape(2, nch, 2, 8, N).transpose(0, 2, 1, 3, 4).reshape(2, 2, G, 1, N)
    d_lr, d_li, d_ldt, d_br, d_bi = _s5_param_bwd(a_re, a_im, ldt, b_re, b_im, da[:, 0], da[:, 1], dbb[:, 0], dbb[:, 1])

    lat_only = lambda t: jnp.pad(t, ((0, Tc), (0, 0)))
    dh_all = jnp.concatenate([du_all, lat_only(dcq), dckv, dkr, jnp.zeros((S, goff - o_kr - LANE), BF16), lat_only(dgates)],
                             axis=1)
    dxm = _mm(dh_all, W["w_in"], "nt", F32, "mm_in_dx")
    gW["w_in"] = _dw(xm_all, dh_all, W["w_in"], "mm_in_dw")

    def norm1_bwd(x, dxm, dx1, n1, sc, sh):
        _, vjp = jax.vjp(_normmod, x, n1, sc, sh)
        dx, dn, dsc, dsh = vjp(dxm)
        return dx + dx1, dn, dsc, dsh

    grad_x, dn1_l, dsc1, dsh1 = _rowmap(norm1_bwd, "norm1_lat_bwd", T, [x, dxm, dx1], [n1, sc1, sh1], [(D, F32)], [D, D, D])

    def norm1_ctx_bwd(x, dxm, n1, sc, sh):
        _, vjp = jax.vjp(_normmod, x, n1, sc, sh)
        return vjp(dxm)[1:]

    dn1_c, dcsc1, dcsh1 = _rowmap(norm1_ctx_bwd, "norm1_ctx_bwd", Tc, [ctx, dxm[T:]], [n1, csc1, csh1], [], [D, D, D])

    zero = jnp.zeros((1, D), F32)
    dm_lat = jnp.concatenate([dsh1, dsc1, dg1, dsh2, dsc2, dg2], axis=0)
    dm_ctx = jnp.concatenate([dcsh1, dcsc1, zero, zero, zero, zero], axis=0)
    small = {
        "norm1": dn1_l + dn1_c, "norm2": dn2, "norm_f": dnf, "q_norm": dq_norm, "kv_norm": dkv_norm, "s5_d": d_skip,
        "s5_a_re": d_lr, "s5_a_im": d_li, "s5_log_dt": d_ldt, "s5_b_re": d_br.transpose(0, 1, 3, 2),
        "s5_b_im": d_bi.transpose(0, 1, 3, 2), "s5_c_re": dcm[:, 0], "s5_c_im": -dcm[:, 1],
    }
    return loss_acc[:, :1], grad_x, small, dm_lat, dm_ctx, gW


BIG = ("w_in", "w_uq", "w_ukv", "w_glu", "w_mla_o", "w_out", "w_ffn_in", "w_ffn_out")
FFN = ("w_ffn_in", "w_ffn_out")
MIX = ("w_out", "w_mla_o", "w_glu")
ROW_SHARDED = ("w_out", "w_ffn_out")
RELAID = ("w_in", "w_uq", "w_ukv")
SMALL = ("c_ctx", "b_mod", "norm1", "norm2", "s5_a_re", "s5_a_im", "s5_log_dt", "s5_b_re", "s5_b_im", "s5_c_re",
         "s5_c_im", "s5_d", "q_norm", "kv_norm", "norm_f")
WEIGHTS = ("c_ctx", "w_mod", "b_mod", "norm1", "norm2", "w_in", "s5_a_re", "s5_a_im", "s5_log_dt", "s5_b_re", "s5_b_im",
           "s5_c_re", "s5_c_im", "s5_d", "w_glu", "q_norm", "kv_norm", "w_uq", "w_ukv", "w_mla_o", "w_out", "w_ffn_in",
           "w_ffn_out", "norm_f")


def _heads_split(w, heads, first):
    k = w.shape[0]
    w3 = w.reshape(k, heads, -1)
    return jnp.concatenate([w3[:, :, :first].reshape(k, -1), w3[:, :, first:].reshape(k, -1)], axis=1)


def _heads_merge(w, heads, first):
    k = w.shape[0]
    a, b = w[:, :heads * first].reshape(k, heads, first), w[:, heads * first:].reshape(k, heads, -1)
    return jnp.concatenate([a, b], axis=2).reshape(k, -1)


def _cols_full(w8):
    return w8.transpose(1, 0, 2).reshape(w8.shape[1], -1)


def _cols_slots(w):
    return w.reshape(w.shape[0], N_DEV, -1).transpose(1, 0, 2)


def _weight_layout(n, w8):
    if n in ROW_SHARDED:
        return w8.reshape(-1, w8.shape[-1])
    return _cols_full(w8) if (n in RELAID or w8.shape[-1] % LANE) else w8


def _grad_slots(n, g):
    if g.ndim == 3:
        return g
    return g.reshape(N_DEV, g.shape[0] // N_DEV, g.shape[1]) if n in ROW_SHARDED else _cols_slots(g)


def _model_weights(g8, D):
    W = {n: _weight_layout(n, w8) for n, w8 in g8.items()}
    w_in = W["w_in"]
    n_front = w_in.shape[1] - 2 * D
    goff = -(-n_front // D) * D
    W["w_in"] = jnp.concatenate([w_in[:, :n_front], jnp.zeros((D, goff - n_front), w_in.dtype), w_in[:, n_front:]], axis=1)
    heads = W["w_uq"].shape[1] // (QK_NOPE + QK_ROPE)
    W["w_uq"] = _heads_split(W["w_uq"], heads, QK_NOPE)
    W["w_ukv"] = _heads_split(W["w_ukv"], heads, QK_NOPE)
    return W, goff


def kernel(x, c, ctx, c_ctx, w_mod, b_mod, norm1, norm2, w_in, s5_a_re, s5_a_im, s5_log_dt, s5_b_re, s5_b_im, s5_c_re, s5_c_im, s5_d, w_glu, q_norm, kv_norm, w_uq, w_ukv, w_mla_o, w_out, w_ffn_in, w_ffn_out, norm_f, loss_target, m_c_ctx, m_w_mod, m_b_mod, m_norm1, m_norm2, m_w_in, m_s5_a_re, m_s5_a_im, m_s5_log_dt, m_s5_b_re, m_s5_b_im, m_s5_c_re, m_s5_c_im, m_s5_d, m_w_glu, m_q_norm, m_kv_norm, m_w_uq, m_w_ukv, m_w_mla_o, m_w_out, m_w_ffn_in, m_w_ffn_out, m_norm_f, v_c_ctx, v_w_mod, v_b_mod, v_norm1, v_norm2, v_w_in, v_s5_a_re, v_s5_a_im, v_s5_log_dt, v_s5_b_re, v_s5_b_im, v_s5_c_re, v_s5_c_im, v_s5_d, v_w_glu, v_q_norm, v_kv_norm, v_w_uq, v_w_ukv, v_w_mla_o, v_w_out, v_w_ffn_in, v_w_ffn_out, v_norm_f):
    a = dict(locals())
    D = x.shape[-1]
    me = 4 * lax.axis_index("x") + 2 * lax.axis_index("y") + lax.axis_index("c")

    shard = {n: a[n][0] for n in BIG}
    first = [n for n in BIG if n not in FFN]
    gathered = _all_gather([shard[n].astype(BF16) for n in first] + [jnp.broadcast_to(c, (8, D))], "ag_weights")
    W, goff = _model_weights(dict(zip(first, gathered[:-1])), D)
    cg = gathered[-1]

    wm = w_mod[0]
    ncol = wm.shape[1]
    c16 = jnp.concatenate([cg[:, 0, :], c_ctx[None], jnp.zeros((7, D), F32)], axis=0)
    (s16,) = _rowmap(jax.nn.silu, "mod_silu", 16, [c16], [], [(D, BF16)])
    m_cols = _mm(s16, wm, "nn", F32, "mm_mod")
    (mg,) = _all_gather([m_cols], "ag_mod")
    (m16,) = _rowmap(lambda m, b: m + b, "mod_bias", 16, [_cols_full(mg)], [b_mod], [(N_DEV * ncol, F32)])

    ffn_blocks = [shard[n].astype(BF16) for n in FFN]
    ag_send, ag_recv, ag_thru, ag_token = _xchg_start(ffn_blocks, [_own_slot(b, me) for b in ffn_blocks], False, "ag_ffn_start")
    m16 = m16 + ag_token[:1, :1]

    def ffn_weights(after):
        lands = _xchg_wait(ag_send, ag_recv, ag_thru, after, False, "ag_ffn_wait")
        return {n: _weight_layout(n, w8) for n, w8 in zip(FFN, lands)}

    rs_async = {}

    def send_grads(names, gs):
        slots = [_grad_slots(n, g) for n, g in zip(names, gs)]
        lands = [_own_slot(lax.dynamic_index_in_dim(s, me, 0, keepdims=False), me) for s in slots]
        rs_async[names] = _xchg_start(slots, lands, True, "rs_start_" + names[0])
        return rs_async[names][3]

    m_lat = lax.dynamic_slice(m16, (me, 0), (1, 6 * D)).reshape(6, D)
    m_ctx = m16[8].reshape(6, D)

    p = {n: a[n][0] for n in ("norm1", "norm2", "s5_a_re", "s5_a_im", "s5_log_dt", "s5_b_re", "s5_b_im", "s5_c_re",
                              "s5_c_im", "q_norm", "kv_norm")}
    p = {k: (v[None] if v.ndim == 1 else v) for k, v in p.items()}
    p["s5_d"] = s5_d.reshape(1, -1)
    p["norm_f"] = norm_f[None]
    loss_part, grad_x, small, dm_lat, dm_ctx, gW = _local_step(x[0], ctx[0], loss_target[0], m_lat, m_ctx, p, W, goff,
                                                               ffn_weights, send_grads)
    loss = lax.psum(loss_part[0, 0], ("x", "y", "c"))

    dm16 = jnp.concatenate([dm_lat.reshape(1, -1), dm_ctx.reshape(1, -1), jnp.zeros((14, 6 * D), F32)], axis=0)
    (dmg,) = _all_gather([dm16], "ag_dmod")
    dm_sum = _sum_slots(dmg, "sum_dmod")
    dM16 = jnp.concatenate([dmg[:, 0, :], dm_sum[1:2], jnp.zeros((7, 6 * D), F32)], axis=0)
    (g_b_mod,) = _rowmap(lambda d: jnp.sum(d, axis=0, keepdims=True), "b_mod_grad", 16, [dM16], [], [], [6 * D])
    dM_loc = lax.dynamic_slice(dM16, (0, me * ncol), (16, ncol))
    g_w_mod = _mm(s16, dM_loc, "tn", F32, "mm_mod_dw")
    ds16_part = _mm(dM_loc, wm, "nt", F32, "mm_mod_dx")

    small_names = [n for n in SMALL if n not in ("c_ctx", "b_mod")]
    small_shapes = [small[n].shape for n in small_names] + [(1, D)]
    (sg,) = _all_gather([_pack_rows([small[n] for n in small_names] + [ds16_part[8:9]], F32)], "ag_small")
    parts = _unpack_rows(_sum_slots(sg, "sum_small"), small_shapes)
    grads = dict(zip(small_names, parts[:-1]))

    def silu_bwd(cc, ds):
        _, vjp = jax.vjp(jax.nn.silu, cc)
        return vjp(ds)[0]

    (g_c_ctx,) = _rowmap(silu_bwd, "c_ctx_grad", 1, [c_ctx[None], parts[-1]], [], [(D, F32)])
    grads["c_ctx"], grads["b_mod"] = g_c_ctx, g_b_mod

    gW = dict(gW)
    n_front = w_in.shape[-1] * N_DEV - 2 * D
    gW["w_in"] = jnp.concatenate([gW["w_in"][:, :n_front], gW["w_in"][:, goff:]], axis=1)
    heads = gW["w_uq"].shape[1] // (QK_NOPE + QK_ROPE)
    gW["w_uq"] = _heads_merge(gW["w_uq"], heads, QK_NOPE)
    gW["w_ukv"] = _heads_merge(gW["w_ukv"], heads, QK_NOPE)
    last = [n for n in BIG if n in gW]
    slots = [_grad_slots(n, gW[n]) for n in last]
    from_sibling = _rs_pair(slots, "rs_pair")
    chip_sums = [_add_pair(pp, rr, "rs_add_" + n) for n, pp, rr in zip(last, slots, from_sibling)]
    from_chips = _rs_chips(chip_sums, "rs_chips")
    for n, g4 in zip(last, from_chips):
        grads[n] = _sum_slots(g4, "rs_sum_" + n)
    for names, (send, recv, thru, _) in rs_async.items():
        for n, g8 in zip(names, _xchg_wait(send, recv, thru, from_chips[0], True, "rs_wait_" + names[0])):
            grads[n] = _sum_slots(g8, "rs_sum_" + n)
    grads["w_mod"] = g_w_mod

    out = {}
    for n in BIG + ("w_mod",):
        d, nm, nv = _adamw(a[n][0], grads[n], a["m_" + n][0], a["v_" + n][0], "adamw_" + n)
        for k, val in (("grad_", grads[n]), ("delta_", d), ("new_m_", nm), ("new_v_", nv)):
            out[k + n] = val.reshape(a[n].shape)
    packs = [_pack_rows([t[n] for n in SMALL], F32) for t in (
        {n: a[n] for n in SMALL}, {n: grads[n] for n in SMALL}, {n: a["m_" + n] for n in SMALL}, {n: a["v_" + n] for n in SMALL})]
    res = _adamw(*packs, "adamw_small")
    shapes = [a[n].shape for n in SMALL]
    for k, packed in (("grad_", packs[1]), ("delta_", res[0]), ("new_m_", res[1]), ("new_v_", res[2])):
        for n, val in zip(SMALL, _unpack_rows(packed, shapes)):
            out[k + n] = val
    return (loss, grad_x[None]) + tuple(out[k + n] for k in ("grad_", "delta_", "new_m_", "new_v_") for n in WEIGHTS)
```

```python
import functools
import math

import jax
import jax.numpy as jnp
from jax import lax
from jax.experimental import pallas as pl
from jax.experimental.pallas import tpu as pltpu

F32 = jnp.float32
BF16 = jnp.bfloat16

N_DEV = 8
N_CHIP = 4
EPS = 1e-6
GRID_W = 64
S5_GROUP = 16
QK_NOPE, QK_ROPE, V_DIM = 128, 64, 128
ROPE_BASE = 10000.0
ADAM_LR, ADAM_B1, ADAM_B2, ADAM_EPS, ADAM_WD, ADAM_STEP = 0.001, 0.9, 0.999, 1e-08, 0.01, 10

LANE = 128
SUB = 8
PACK_W = 1024
PACK_ROWS = 32
VMEM_LIMIT = 48 << 20
ROWMAP_TILE_BYTES = 10 << 20
MM_VMEM_BUDGET = 36 << 20
MESH = pl.DeviceIdType.MESH
_NT = (((1,), (1,)), ((), ()))
_TN = (((0,), (0,)), ((), ()))


def _pick(dim, cands):
    for c in cands:
        if dim % c == 0:
            return c
    return dim


def _cparams(sem):
    return pltpu.CompilerParams(dimension_semantics=sem, vmem_limit_bytes=VMEM_LIMIT)


def _mm(a, b, dims, out_dtype, name, out_slots=None):
    a = a.astype(BF16)
    b = b.astype(BF16)
    b3 = b.ndim == 3
    if dims == "nn":
        (M, K), N = a.shape, (b.shape[0] * b.shape[2] if b3 else b.shape[1])
    elif dims == "nt":
        M, N = a.shape[0], b.shape[-2]
        K = b.shape[0] * b.shape[2] if b3 else b.shape[1]
    else:
        (K, M), N = a.shape, b.shape[1]
    unit_n = b.shape[2] if (b3 and dims == "nn") else (N // out_slots if out_slots else N)
    unit_k = b.shape[2] if (b3 and dims == "nt") else K
    osz = jnp.dtype(out_dtype).itemsize
    tm, tn, tk = _mm_tiles(M, unit_n, unit_k, osz, LANE if dims == "tn" else 16)
    nk, npt, kpt = K // tk, unit_n // tn, unit_k // tk
    use_acc = nk > 1 and out_dtype != F32
    if dims == "nn":
        a_spec = pl.BlockSpec((tm, tk), lambda i, j, k: (i, k))
        b_spec = (pl.BlockSpec((None, tk, tn), lambda i, j, k: (j // npt, k, j % npt)) if b3
                  else pl.BlockSpec((tk, tn), lambda i, j, k: (k, j)))
        dn = (((1,), (0,)), ((), ()))
    elif dims == "nt":
        a_spec = pl.BlockSpec((tm, tk), lambda i, j, k: (i, k))
        b_spec = (pl.BlockSpec((None, tn, tk), lambda i, j, k: (k // kpt, j, k % kpt)) if b3
                  else pl.BlockSpec((tn, tk), lambda i, j, k: (j, k)))
        dn = _NT
    else:
        a_spec = pl.BlockSpec((tk, tm), lambda i, j, k: (k, i))
        b_spec = pl.BlockSpec((tk, tn), lambda i, j, k: (k, j))
        dn = _TN
    if out_slots:
        out_spec = pl.BlockSpec((None, tm, tn), lambda i, j, k: (j // npt, i, j % npt))
        out_shape = jax.ShapeDtypeStruct((out_slots, M, unit_n), out_dtype)
    else:
        out_spec = pl.BlockSpec((tm, tn), lambda i, j, k: (i, j))
        out_shape = jax.ShapeDtypeStruct((M, N), out_dtype)

    def body(a_ref, b_ref, o_ref, *scratch):
        part = lax.dot_general(a_ref[...], b_ref[...], dn, preferred_element_type=F32)
        if nk == 1:
            o_ref[...] = part.astype(o_ref.dtype)
            return
        acc_ref = scratch[0] if use_acc else o_ref
        k = pl.program_id(2)

        @pl.when(k == 0)
        def _():
            acc_ref[...] = part

        @pl.when(k > 0)
        def _():
            acc_ref[...] += part

        if use_acc:
            @pl.when(k == nk - 1)
            def _():
                o_ref[...] = acc_ref[...].astype(o_ref.dtype)

    return pl.pallas_call(
        body, name=name, grid=(M // tm, N // tn, nk),
        in_specs=[a_spec, b_spec], out_specs=out_spec, out_shape=out_shape,
        scratch_shapes=[pltpu.VMEM((tm, tn), F32)] if use_acc else [],
        compiler_params=_cparams(("parallel", "parallel", "arbitrary")),
    )(a, b)


def _divisors(n, mult, cap):
    d = [t for t in range(mult, min(n, cap) + 1, mult) if n % t == 0]
    return d[::-1] or [n]


def _mm_tiles(M, unit_n, unit_k, out_itemsize, tm_mult):
    best = None
    for tk in _divisors(unit_k, LANE, 2816):
        for tn in _divisors(unit_n, LANE, 1536):
            for tm in _divisors(M, tm_mult, 1024):
                vmem = 2 * 2 * (tm * tk + tk * tn) + 2 * tm * tn * out_itemsize + 4 * tm * tn * (2 if unit_k > tk else 1)
                if vmem > MM_VMEM_BUDGET:
                    continue
                steps = (M // tm) * (unit_n // tn) * (unit_k // tk)
                key = (steps, -tk, -tn)
                if best is None or key < best[0]:
                    best = (key, (tm, tn, tk))
                break
    return best[1]


def _rowmap(fn, name, M, row_ins, bc_ins, row_outs, acc_outs=()):
    row_ins = [r if isinstance(r, tuple) else (r, r.shape[1], 0) for r in row_ins]
    row_bytes = sum(w * a.dtype.itemsize for a, w, _ in row_ins) + sum(w * jnp.dtype(d).itemsize for w, d in row_outs)
    widest = max([w for _, w, _ in row_ins] + [w for w, _ in row_outs])
    row_bytes = 2 * row_bytes + 6 * 4 * widest
    tm = _pick(M, [t for t in (512, 256, 128, 64, 32, 16) if t * row_bytes <= ROWMAP_TILE_BYTES] + [16])
    n_in, n_row, n_acc = len(row_ins) + len(bc_ins), len(row_outs), len(acc_outs)

    def body(*refs):
        res = fn(*[r[...].astype(F32) for r in refs[:n_in]])
        res = res if isinstance(res, (tuple, list)) else (res,)
        outs = refs[n_in:]
        for k in range(n_row):
            outs[k][...] = res[k].astype(outs[k].dtype)
        if n_acc:
            @pl.when(pl.program_id(0) == 0)
            def _():
                for k in range(n_acc):
                    outs[n_row + k][...] = jnp.zeros_like(outs[n_row + k])

            for k in range(n_acc):
                outs[n_row + k][...] += res[n_row + k].astype(F32)

    in_specs = [pl.BlockSpec((tm, w), functools.partial(lambda i, blk: (i, blk), blk=blk)) for _, w, blk in row_ins]
    in_specs += [pl.BlockSpec(b.shape, lambda i: (0, 0)) for b in bc_ins]
    out_specs = [pl.BlockSpec((tm, w), lambda i: (i, 0)) for w, _ in row_outs]
    out_specs += [pl.BlockSpec((1, w), lambda i: (0, 0)) for w in acc_outs]
    out_shape = [jax.ShapeDtypeStruct((M, w), d) for w, d in row_outs]
    out_shape += [jax.ShapeDtypeStruct((1, w), F32) for w in acc_outs]
    return pl.pallas_call(
        body, name=name, grid=(M // tm,), in_specs=in_specs, out_specs=out_specs, out_shape=out_shape,
        compiler_params=_cparams(("arbitrary",) if n_acc else ("parallel",)),
    )(*[a for a, _, _ in row_ins], *bc_ins)


def _rms(x, g):
    return x * lax.rsqrt(jnp.mean(x * x, axis=-1, keepdims=True) + EPS) * g


def _normmod(x, g, sc, sh):
    return _rms(x, g) * (1.0 + sc) + sh


def _swap16(v):
    w = v.shape[1]
    lane = lax.broadcasted_iota(jnp.int32, v.shape, 1)
    return jnp.where((lane // 16) % 2 == 0, pltpu.roll(v, w - 16, 1), pltpu.roll(v, 16, 1))


def _rope(v, cos, sin_signed):
    return v * cos + _swap16(v) * sin_signed


def _rope_bwd(d, cos, sin_signed):
    return d * cos + _swap16(d * sin_signed)


def _mesh_pos():
    return lax.axis_index("x"), lax.axis_index("y"), lax.axis_index("c")


def _hbm_call(body, name, ins, out_shapes, n_sems):
    any_spec = pl.BlockSpec(memory_space=pl.ANY)
    return pl.pallas_call(
        body, name=name, out_shape=out_shapes, in_specs=[any_spec] * len(ins), out_specs=[any_spec] * len(out_shapes),
        scratch_shapes=[pltpu.SemaphoreType.DMA((n_sems,)), pltpu.SemaphoreType.DMA((n_sems,)),
                        pltpu.SemaphoreType.DMA((len(ins),))],
    )(*ins)


def _all_gather(xs, name):
    n = len(xs)

    def body(*refs):
        x_refs, out_refs, (send_sems, recv_sems, local_sems) = refs[:n], refs[n:2 * n], refs[2 * n:]
        x, y, c = _mesh_pos()
        me, sibling = (x, y, c), (x, y, 1 - c)
        chips = [(1 - x, y), (x, 1 - y), (1 - x, 1 - y)]
        locals_, first, passed, arrivals = [], [], [], []
        for a in range(n):
            def slot(px, py, pc, a=a):
                return out_refs[a].at[4 * px + 2 * py + pc]

            def copy(k, block, to, src=None, a=a, slot=slot):
                return pltpu.make_async_remote_copy(
                    src_ref=slot(*block) if src is None else src, dst_ref=slot(*block),
                    send_sem=send_sems.at[7 * a + k], recv_sem=recv_sems.at[7 * a + k], device_id=to, device_id_type=MESH)

            locals_.append(pltpu.make_async_copy(x_refs[a], slot(*me), local_sems.at[a]))
            first.append(copy(0, me, sibling, src=x_refs[a]))
            first += [copy(1 + j, me, (*chip, c), src=x_refs[a]) for j, chip in enumerate(chips)]
            passed.append([copy(4 + j, (*chip, c), sibling) for j, chip in enumerate(chips)])
            arrivals.append([copy(1 + j, (*chip, c), me) for j, chip in enumerate(chips)]
                            + [copy(0, sibling, me)] + [copy(4 + j, (*chip, 1 - c), me) for j, chip in enumerate(chips)])
        for cp in locals_ + first:
            cp.start()
        for j in range(3):
            for a in range(n):
                arrivals[a][j].wait_recv()
                passed[a][j].start()
        for a in range(n):
            for cp in arrivals[a][3:]:
                cp.wait_recv()
        for cp in first + [p for ps in passed for p in ps]:
            cp.wait_send()
        for cp in locals_:
            cp.wait()

    return _hbm_call(body, name, xs, [jax.ShapeDtypeStruct((N_DEV,) + x.shape, x.dtype) for x in xs], 7 * n)


def _rs_pair(ps, name):
    n = len(ps)

    def body(*refs):
        p_refs, out_refs, (send_sems, recv_sems, _) = refs[:n], refs[n:2 * n], refs[2 * n:]
        x, y, c = _mesh_pos()
        sends, recvs = [], []
        for a in range(n):
            for q in range(N_CHIP):
                sem = dict(send_sem=send_sems.at[4 * a + q], recv_sem=recv_sems.at[4 * a + q],
                           device_id=(x, y, 1 - c), device_id_type=MESH)
                sends.append(pltpu.make_async_remote_copy(src_ref=p_refs[a].at[2 * q + 1 - c], dst_ref=out_refs[a].at[q], **sem))
                recvs.append(pltpu.make_async_remote_copy(src_ref=p_refs[a].at[2 * q + c], dst_ref=out_refs[a].at[q], **sem))
        for cp in sends:
            cp.start()
        for cp in recvs:
            cp.wait_recv()
        for cp in sends:
            cp.wait_send()

    return _hbm_call(body, name, ps, [jax.ShapeDtypeStruct((N_CHIP,) + p.shape[1:], p.dtype) for p in ps], 4 * n)


def _rs_chips(qs, name):
    n = len(qs)

    def body(*refs):
        q_refs, out_refs, (send_sems, recv_sems, local_sems) = refs[:n], refs[n:2 * n], refs[2 * n:]
        x, y, c = _mesh_pos()
        mine = 2 * x + y
        locals_, sends, recvs = [], [], []
        for a in range(n):
            locals_.append(pltpu.make_async_copy(q_refs[a].at[mine], out_refs[a].at[mine], local_sems.at[a]))
            for r in range(1, N_CHIP):
                px = 1 - x if r & 2 else x
                py = 1 - y if r & 1 else y
                peer = 2 * px + py
                sem = dict(send_sem=send_sems.at[3 * a + r - 1], recv_sem=recv_sems.at[3 * a + r - 1],
                           device_id=(px, py, c), device_id_type=MESH)
                sends.append(pltpu.make_async_remote_copy(src_ref=q_refs[a].at[peer], dst_ref=out_refs[a].at[mine], **sem))
                recvs.append(pltpu.make_async_remote_copy(src_ref=q_refs[a].at[peer], dst_ref=out_refs[a].at[peer], **sem))
        for cp in locals_ + sends:
            cp.start()
        for cp in recvs:
            cp.wait_recv()
        for cp in sends:
            cp.wait_send()
        for cp in locals_:
            cp.wait()

    return _hbm_call(body, name, qs, [jax.ShapeDtypeStruct(q.shape, q.dtype) for q in qs], 3 * n)


def _xchg_copies(src_refs, land_refs, send_sems, recv_sems, slot_src):
    x, y, c = _mesh_pos()
    me = 4 * x + 2 * y + c
    sends, recvs = [], []
    for a, (src, land) in enumerate(zip(src_refs, land_refs)):
        for r in range(1, N_DEV):
            px = 1 - x if r & 4 else x
            py = 1 - y if r & 2 else y
            pc = 1 - c if r & 1 else c
            peer = 4 * px + 2 * py + pc
            sem = dict(send_sem=send_sems.at[7 * a + r - 1], recv_sem=recv_sems.at[7 * a + r - 1],
                       device_id=(px, py, pc), device_id_type=MESH)
            s = src.at[peer] if slot_src else src
            sends.append(pltpu.make_async_remote_copy(src_ref=s, dst_ref=land.at[me], **sem))
            recvs.append(pltpu.make_async_remote_copy(src_ref=s, dst_ref=land.at[peer], **sem))
    return sends, recvs


_HBM = pl.BlockSpec(memory_space=pltpu.HBM)
_SEM = pl.BlockSpec(memory_space=pltpu.SEMAPHORE)
_EFFECT = pltpu.SideEffectType.DATAFLOW_SIDE_EFFECTING


def _xchg_start(srcs, lands, slot_src, name):
    n = len(srcs)

    def body(*refs):
        sends, _ = _xchg_copies(refs[:n], refs[n:2 * n], refs[2 * n], refs[2 * n + 1], slot_src)
        for cp in sends:
            cp.start()
        refs[-1][...] = jnp.zeros_like(refs[-1])

    bufs = list(srcs) + list(lands)
    res = pl.pallas_call(
        body, name=name,
        out_shape=(pltpu.SemaphoreType.DMA((7 * n,)), pltpu.SemaphoreType.DMA((7 * n,)))
        + tuple(pltpu.HBM(b.shape, b.dtype) for b in bufs) + (jax.ShapeDtypeStruct((SUB, LANE), F32),),
        in_specs=(_HBM,) * (2 * n), out_specs=(_SEM, _SEM) + (_HBM,) * (2 * n) + (pl.BlockSpec(memory_space=pltpu.VMEM),),
        input_output_aliases={i: 2 + i for i in range(2 * n)},
        compiler_params=pltpu.CompilerParams(has_side_effects=_EFFECT),
    )(*[pltpu.with_memory_space_constraint(b, pltpu.HBM) for b in bufs])
    return res[0], res[1], res[2:-1], res[-1]


def _xchg_wait(send_sems, recv_sems, thru, after, slot_src, name):
    n = len(thru) // 2

    def body(*refs):
        sends, recvs = _xchg_copies(refs[:n], refs[n:2 * n], refs[2 * n], refs[2 * n + 1], slot_src)
        for cp in sends:
            cp.wait_send()
        for cp in recvs:
            cp.wait_recv()

    res = pl.pallas_call(
        body, name=name, out_shape=tuple(pltpu.HBM(b.shape, b.dtype) for b in thru),
        in_specs=(_HBM,) * (2 * n) + (_SEM, _SEM, pl.BlockSpec(memory_space=pl.ANY)), out_specs=(_HBM,) * (2 * n),
        input_output_aliases={i: i for i in range(2 * n)},
        compiler_params=pltpu.CompilerParams(has_side_effects=_EFFECT),
    )(*thru, send_sems, recv_sems, after)
    return res[n:]


def _own_slot(block, me):
    return lax.dynamic_update_slice(lax.empty((N_DEV,) + block.shape, block.dtype), block[None], (me, 0, 0))


def _add_pair(p, r, name):
    _, R, C = p.shape
    tr = _pick(R, (512, 256, 128, 64, 32, 16))

    def body(c_ref, p_ref, r_ref, o_ref):
        o_ref[...] = (p_ref[...].astype(F32) + r_ref[...].astype(F32)).astype(o_ref.dtype)

    return pl.pallas_call(
        body, name=name, out_shape=jax.ShapeDtypeStruct((N_CHIP, R, C), p.dtype),
        grid_spec=pltpu.PrefetchScalarGridSpec(
            num_scalar_prefetch=1, grid=(N_CHIP, R // tr),
            in_specs=[pl.BlockSpec((None, None, tr, C), lambda q, i, c_ref: (q, c_ref[0], i, 0)),
                      pl.BlockSpec((None, tr, C), lambda q, i, c_ref: (q, i, 0))],
            out_specs=pl.BlockSpec((None, tr, C), lambda q, i, c_ref: (q, i, 0))),
        compiler_params=_cparams(("parallel", "parallel")),
    )(lax.axis_index("c").reshape(1).astype(jnp.int32), p.reshape(N_CHIP, 2, R, C), r)


def _sum_slots(g, name):
    ns, R, C = g.shape
    tr = _pick(R, (256, 128, 64, 32, 16))

    def body(g_ref, o_ref):
        acc = g_ref[0].astype(F32)
        for j in range(1, ns):
            acc = acc + g_ref[j].astype(F32)
        o_ref[...] = acc

    return pl.pallas_call(
        body, name=name, grid=(R // tr,),
        in_specs=[pl.BlockSpec((ns, tr, C), lambda i: (0, i, 0))], out_specs=pl.BlockSpec((tr, C), lambda i: (i, 0)),
        out_shape=jax.ShapeDtypeStruct((R, C), F32), compiler_params=_cparams(("parallel",)),
    )(g)


def _pack_rows(arrs, dtype):
    parts = []
    for a in arrs:
        flat = a.reshape(-1).astype(dtype)
        pad = (-flat.shape[0]) % (PACK_W * 16)
        parts.append(jnp.pad(flat, (0, pad)).reshape(-1, PACK_W))
    out = jnp.concatenate(parts, axis=0)
    return jnp.pad(out, ((0, (-out.shape[0]) % PACK_ROWS), (0, 0)))


def _packed_rows(shape):
    n = math.prod(shape)
    return (n + PACK_W * 16 - 1) // (PACK_W * 16) * 16


def _unpack_rows(packed, shapes):
    out, r0 = [], 0
    for s in shapes:
        rows, n = _packed_rows(s), math.prod(s)
        out.append(packed[r0:r0 + rows].reshape(rows * PACK_W)[:n].reshape(s))
        r0 += rows
    return out


def _adamw_math(w, g, m, v):
    m = ADAM_B1 * m + (1.0 - ADAM_B1) * g
    v = ADAM_B2 * v + (1.0 - ADAM_B2) * (g * g)
    m_hat = m / (1.0 - ADAM_B1 ** ADAM_STEP)
    v_hat = v / (1.0 - ADAM_B2 ** ADAM_STEP)
    delta = -ADAM_LR * (m_hat / (jnp.sqrt(v_hat) + ADAM_EPS) + ADAM_WD * w)
    return delta, m, v


def _adamw(w, g, m, v, name):
    R, C = w.shape
    return _rowmap(_adamw_math, name, R, [w, g, m, v], [], [(C, F32)] * 3)


def _s5_disc_math(lr, li, ldt, br, bi):
    dt = jnp.exp(ldt)
    mag = jnp.exp(lr * dt)
    ab_re, ab_im = mag * jnp.cos(li * dt), mag * jnp.sin(li * dt)
    den = lr * lr + li * li
    nr, ni = ab_re - 1.0, ab_im
    co_re = (nr * lr + ni * li) / den
    co_im = (ni * lr - nr * li) / den
    bb_re = co_re * br - co_im * bi
    bb_im = co_re * bi + co_im * br
    return ab_re, ab_im, bb_re, bb_im


def _s5_tables(a_re, a_im, ldt, b_re, b_im, c_re, c_im):
    _, G, P, N = b_re.shape
    nch = G // 8

    def body(lr_ref, li_ref, ldt_ref, br_ref, bi_ref, cr_ref, ci_ref, wre, wim, vre, vim, pwr, pwi):
        ar, ai, bb_re, bb_im = _s5_disc_math(lr_ref[0], li_ref[0], ldt_ref[0], br_ref[0], bi_ref[0])
        cr, ci = cr_ref[0], ci_ref[0]
        pr, pi = jnp.ones_like(ar), jnp.zeros_like(ar)
        for j in range(SUB + 1):
            pwr[0, j], pwi[0, j] = pr, pi
            if j < SUB:
                tabs = ((wre, bb_re * pr - bb_im * pi), (wim, bb_re * pi + bb_im * pr),
                        (vre, cr * pr - ci * pi), (vim, -(cr * pi + ci * pr)))
                for ref, val in tabs:
                    for s in range(nch):
                        ref[0, s, pl.ds(j * LANE, LANE), :] = val[s * 8:(s + 1) * 8].reshape(LANE, N).astype(BF16)
            pr, pi = pr * ar - pi * ai, pr * ai + pi * ar

    g1n = pl.BlockSpec((1, G, 1, N), lambda d: (d, 0, 0, 0))
    gpn = pl.BlockSpec((1, G, P, N), lambda d: (d, 0, 0, 0))
    tab = pl.BlockSpec((1, nch, SUB * LANE, N), lambda d: (d, 0, 0, 0))
    pw = pl.BlockSpec((1, SUB + 1, G, 1, N), lambda d: (d, 0, 0, 0, 0))
    s_tab = jax.ShapeDtypeStruct((2, nch, SUB * LANE, N), BF16)
    s_pw = jax.ShapeDtypeStruct((2, SUB + 1, G, 1, N), F32)
    return pl.pallas_call(
        body, name="s5_tables", grid=(2,),
        in_specs=[g1n, g1n, pl.BlockSpec((1, G, 1, 1), lambda d: (d, 0, 0, 0)), gpn, gpn, gpn, gpn],
        out_specs=[tab] * 4 + [pw] * 2, out_shape=[s_tab] * 4 + [s_pw] * 2,
        compiler_params=_cparams(("parallel",)),
    )(a_re, a_im, ldt, b_re, b_im, c_re, c_im)


def _s5_expand(t_re, t_im, name):
    _, nch, R, N = t_re.shape
    sw = 8 * N

    def body(re_ref, im_ref, o_ref):
        spread = (lax.broadcasted_iota(jnp.int32, (N, sw), 1) % N == lax.broadcasted_iota(jnp.int32, (N, sw), 0)).astype(BF16)
        row_g = (lax.broadcasted_iota(jnp.int32, (R, sw), 0) % LANE) // S5_GROUP
        keep = row_g == lax.broadcasted_iota(jnp.int32, (R, sw), 1) // N
        for half, ref in enumerate((re_ref, im_ref)):
            t = jnp.dot(ref[0, 0], spread, preferred_element_type=F32)
            o_ref[0, 0, :, pl.ds(half * sw, sw)] = jnp.where(keep, t, 0.0).astype(BF16)

    spec = pl.BlockSpec((1, 1, R, N), lambda d, s: (d, s, 0, 0))
    return pl.pallas_call(
        body, name=name, grid=(2, nch), in_specs=[spec, spec],
        out_specs=pl.BlockSpec((1, 1, R, 2 * sw), lambda d, s: (d, s, 0, 0)),
        out_shape=jax.ShapeDtypeStruct((2, nch, R, 2 * sw), BF16), compiler_params=_cparams(("parallel", "parallel")),
    )(t_re, t_im)


def _s5_param_bwd(a_re, a_im, ldt, b_re, b_im, da_re, da_im, dbb_re, dbb_im):
    _, G, P, N = b_re.shape

    def body(lr_ref, li_ref, ldt_ref, br_ref, bi_ref, dar, dai, dbr, dbi, o_lr, o_li, o_ldt, o_br, o_bi):
        _, vjp = jax.vjp(_s5_disc_math, lr_ref[0], li_ref[0], ldt_ref[0], br_ref[0], bi_ref[0])
        o_lr[0], o_li[0], o_ldt[0], o_br[0], o_bi[0] = vjp((dar[0], dai[0], dbr[0], dbi[0]))

    g1n = pl.BlockSpec((1, G, 1, N), lambda d: (d, 0, 0, 0))
    g11 = pl.BlockSpec((1, G, 1, 1), lambda d: (d, 0, 0, 0))
    gpn = pl.BlockSpec((1, G, P, N), lambda d: (d, 0, 0, 0))
    s_g1n, s_g11, s_gpn = (jax.ShapeDtypeStruct(s, F32) for s in ((2, G, 1, N), (2, G, 1, 1), (2, G, P, N)))
    return pl.pallas_call(
        body, name="s5_param_bwd", grid=(2,),
        in_specs=[g1n, g1n, g11, gpn, gpn, g1n, g1n, gpn, gpn], out_specs=[g1n, g1n, g11, gpn, gpn],
        out_shape=[s_g1n, s_g1n, s_g11, s_gpn, s_gpn], compiler_params=_cparams(("parallel",)),
    )(a_re, a_im, ldt, b_re, b_im, da_re, da_im, dbb_re, dbb_im)


def _shift_stack(u, back):
    tb = u.shape[0]
    tau = lax.broadcasted_iota(jnp.int32, u.shape, 0) % SUB
    parts = [u]
    for j in range(1, SUB):
        if back:
            parts.append(jnp.where(tau >= j, pltpu.roll(u, j, 0), 0.0))
        else:
            parts.append(jnp.where(tau <= SUB - 1 - j, pltpu.roll(u, tb - j, 0), 0.0))
    return jnp.concatenate(parts, axis=1).astype(BF16)


def _cmul_add(tile, pw, carry, sw):
    pr, pi, cr, ci = pw[:, :sw], pw[:, sw:], carry[:, :sw], carry[:, sw:]
    return tile + jnp.concatenate([pr * cr - pi * ci, pr * ci + pi * cr], axis=1)


def _tile_scan(buf, base, ntile, pw, carry, sw, causal):
    def step(k, c):
        i = k if causal else ntile - 1 - k
        r = pl.multiple_of(base + i * SUB, SUB)
        tile = _cmul_add(buf[pl.ds(r, SUB), :], pw, c, sw)
        buf[pl.ds(r, SUB), :] = tile
        return tile[SUB - 1:SUB, :] if causal else tile[0:1, :]

    return lax.fori_loop(0, ntile, step, carry)


def _s5_fwd(h_all, waug, vaug, pw, S5W, T, causal, name):
    S = h_all.shape[0]
    nch, _, sw2 = waug.shape
    sw = sw2 // 2
    tb = _pick(math.gcd(T, S - T), (256, 128, 64, 32, 16))
    ntile, nt, off = tb // SUB, S // tb, T // tb
    rb = (lambda s, t: ((t + off) % nt, s)) if causal else (lambda s, t: (nt - 1 - t, s))

    def body(u_ref, w_ref, v_ref, p_ref, y_ref, h_ref, hblk, carry):
        @pl.when(pl.program_id(1) == 0)
        def _():
            carry[...] = jnp.zeros_like(carry)

        hblk[...] = jnp.dot(_shift_stack(u_ref[...], causal), w_ref[...], preferred_element_type=F32)
        carry[...] = _tile_scan(hblk, 0, ntile, p_ref[...], carry[...], sw, causal)
        hb = hblk[...].astype(BF16)
        h_ref[...] = hb
        y_ref[...] = lax.dot_general(hb, v_ref[...], _NT, preferred_element_type=F32)

    return pl.pallas_call(
        body, name=name, grid=(nch, nt),
        in_specs=[pl.BlockSpec((tb, LANE), rb),
                  pl.BlockSpec((None, SUB * LANE, sw2), lambda s, t: (s, 0, 0)),
                  pl.BlockSpec((None, LANE, sw2), lambda s, t: (s, 0, 0)),
                  pl.BlockSpec((None, SUB, sw2), lambda s, t: (s, 0, 0))],
        out_specs=[pl.BlockSpec((tb, LANE), rb), pl.BlockSpec((tb, sw2), rb)],
        out_shape=[jax.ShapeDtypeStruct((S, S5W), F32), jax.ShapeDtypeStruct((S, nch * sw2), BF16)],
        scratch_shapes=[pltpu.VMEM((tb, sw2), F32), pltpu.VMEM((1, sw2), F32)],
        compiler_params=_cparams(("parallel", "arbitrary")),
    )(h_all, waug, vaug, pw)


def _s5_bwd(dy_all, h_all, hs, waug, vaug, pwc, S5W, T, causal, name):
    S = h_all.shape[0]
    nch, _, sw2 = waug.shape
    sw = sw2 // 2
    tb = _pick(math.gcd(T, S - T), (256, 128, 64, 32, 16))
    ntile, nt, off = tb // SUB, S // tb, T // tb
    rb = (lambda s, t: ((nt - 1 - t + off) % nt, s)) if causal else (lambda s, t: (t, s))
    adj_causal = not causal
    edge = SUB - 1 if adj_causal else SUB + tb
    keep_src, keep_dst = (tb, 0) if adj_causal else (SUB, SUB + tb)

    def body(dy_ref, u_ref, h_ref, w_ref, v_ref, p_ref, du_ref, dbb_ref, dc_ref, da_ref, lam):
        @pl.when(pl.program_id(1) == 0)
        def _():
            lam[pl.ds(0, SUB), :] = jnp.zeros((SUB, sw2), F32)
            lam[pl.ds(SUB + tb, SUB), :] = jnp.zeros((SUB, sw2), F32)
            dbb_ref[...] = jnp.zeros_like(dbb_ref)
            dc_ref[...] = jnp.zeros_like(dc_ref)
            da_ref[...] = jnp.zeros_like(da_ref)

        dy = dy_ref[...]
        lam[pl.ds(SUB, tb), :] = jnp.dot(_shift_stack(dy, adj_causal), v_ref[...], preferred_element_type=F32)
        _tile_scan(lam, SUB, ntile, p_ref[...], lam[pl.ds(edge, 1), :], sw, adj_causal)
        lb = lam[pl.ds(SUB, tb), :].astype(BF16)
        du_ref[...] = lax.dot_general(lb, w_ref[...], _NT, preferred_element_type=F32)
        dbb_ref[...] += lax.dot_general(u_ref[...].astype(BF16), lb, _TN, preferred_element_type=F32)
        dc_ref[...] += lax.dot_general(h_ref[...], dy.astype(BF16), _TN, preferred_element_type=F32)
        h = h_ref[...].astype(F32)
        ln = lam[pl.ds(SUB + 1 if causal else SUB - 1, tb), :]
        hr, hi, lr, li = h[:, :sw], h[:, sw:], ln[:, :sw], ln[:, sw:]
        da_ref[...] += jnp.concatenate([jnp.sum(hr * lr + hi * li, axis=0, keepdims=True),
                                        jnp.sum(hr * li - hi * lr, axis=0, keepdims=True)], axis=1)
        lam[pl.ds(keep_dst, SUB), :] = lam[pl.ds(keep_src, SUB), :]

    fixed = lambda s, t: (s, 0, 0)
    return pl.pallas_call(
        body, name=name, grid=(nch, nt),
        in_specs=[pl.BlockSpec((tb, LANE), rb), pl.BlockSpec((tb, LANE), rb), pl.BlockSpec((tb, sw2), rb),
                  pl.BlockSpec((None, LANE, sw2), fixed), pl.BlockSpec((None, SUB * LANE, sw2), fixed),
                  pl.BlockSpec((None, SUB, sw2), fixed)],
        out_specs=[pl.BlockSpec((tb, LANE), rb), pl.BlockSpec((None, LANE, sw2), fixed),
                   pl.BlockSpec((None, sw2, LANE), fixed), pl.BlockSpec((None, 1, sw2), fixed)],
        out_shape=[jax.ShapeDtypeStruct((S, S5W), F32), jax.ShapeDtypeStruct((nch, LANE, sw2), F32),
                   jax.ShapeDtypeStruct((nch, sw2, LANE), F32), jax.ShapeDtypeStruct((nch, 1, sw2), F32)],
        scratch_shapes=[pltpu.VMEM((tb + 2 * SUB, sw2), F32)],
        compiler_params=_cparams(("parallel", "arbitrary")),
    )(dy_all, h_all, hs, waug, vaug, pwc)


def _attn_fwd(q3, k3, v3, scale):
    H, T, dk = q3.shape
    S, dv = k3.shape[1], v3.shape[2]
    tq = _pick(T, (256, 128, 64, 32, 16))

    def body(q_ref, k_ref, v_ref, o_ref, lse_ref):
        s = lax.dot_general(q_ref[0], k_ref[0], _NT, preferred_element_type=F32) * scale
        m = jnp.max(s, axis=1, keepdims=True)
        p = jnp.exp(s - m)
        l = jnp.sum(p, axis=1, keepdims=True)
        o_ref[...] = jnp.dot((p / l).astype(BF16), v_ref[0], preferred_element_type=F32).astype(o_ref.dtype)
        lse_ref[0] = m + jnp.log(l)

    return pl.pallas_call(
        body, name="attn_fwd", grid=(H, T // tq),
        in_specs=[pl.BlockSpec((1, tq, dk), lambda h, i: (h, i, 0)), pl.BlockSpec((1, S, dk), lambda h, i: (h, 0, 0)),
                  pl.BlockSpec((1, S, dv), lambda h, i: (h, 0, 0))],
        out_specs=[pl.BlockSpec((tq, dv), lambda h, i: (i, h)), pl.BlockSpec((1, tq, 1), lambda h, i: (h, i, 0))],
        out_shape=[jax.ShapeDtypeStruct((T, H * dv), BF16), jax.ShapeDtypeStruct((H, T, 1), F32)],
        compiler_params=_cparams(("parallel", "parallel")),
    )(q3, k3, v3)


def _attn_bwd(q3, k3, v3, do, lse, scale):
    H, T, dk = q3.shape
    S, dv = k3.shape[1], v3.shape[2]
    tq = _pick(T, (256, 128, 64, 32, 16))

    def body(q_ref, k_ref, v_ref, do_ref, lse_ref, dq_ref, dk_ref, dv_ref):
        @pl.when(pl.program_id(1) == 0)
        def _():
            dk_ref[...] = jnp.zeros_like(dk_ref)
            dv_ref[...] = jnp.zeros_like(dv_ref)

        q, k, v, d_o = q_ref[0], k_ref[0], v_ref[0], do_ref[...]
        s = lax.dot_general(q, k, _NT, preferred_element_type=F32) * scale
        p = jnp.exp(s - lse_ref[0])
        dv_ref[0] += lax.dot_general(p.astype(BF16), d_o, _TN, preferred_element_type=F32)
        dp = lax.dot_general(d_o, v, _NT, preferred_element_type=F32)
        ds = (p * (dp - jnp.sum(p * dp, axis=1, keepdims=True)) * scale).astype(BF16)
        dq_ref[0] = jnp.dot(ds, k, preferred_element_type=F32)
        dk_ref[0] += lax.dot_general(ds, q, _TN, preferred_element_type=F32)

    return pl.pallas_call(
        body, name="attn_bwd", grid=(H, T // tq),
        in_specs=[pl.BlockSpec((1, tq, dk), lambda h, i: (h, i, 0)), pl.BlockSpec((1, S, dk), lambda h, i: (h, 0, 0)),
                  pl.BlockSpec((1, S, dv), lambda h, i: (h, 0, 0)), pl.BlockSpec((tq, dv), lambda h, i: (i, h)),
                  pl.BlockSpec((1, tq, 1), lambda h, i: (h, i, 0))],
        out_specs=[pl.BlockSpec((1, tq, dk), lambda h, i: (h, i, 0)), pl.BlockSpec((1, S, dk), lambda h, i: (h, 0, 0)),
                   pl.BlockSpec((1, S, dv), lambda h, i: (h, 0, 0))],
        out_shape=[jax.ShapeDtypeStruct((H, T, dk), F32), jax.ShapeDtypeStruct((H, S, dk), F32),
                   jax.ShapeDtypeStruct((H, S, dv), F32)],
        compiler_params=_cparams(("parallel", "arbitrary")),
    )(q3, k3, v3, do, lse)


def _rope_tables(T, heads):
    rows = T // GRID_W
    row = jnp.repeat(jnp.arange(rows, dtype=F32), GRID_W)
    col = jnp.tile(jnp.arange(GRID_W, dtype=F32), rows)
    n_freq = QK_ROPE // 4
    inv = ROPE_BASE ** (-jnp.arange(n_freq, dtype=F32) / n_freq)
    ar, ac = row[:, None] * inv, col[:, None] * inv
    cos = jnp.concatenate([jnp.cos(ar), jnp.cos(ar), jnp.cos(ac), jnp.cos(ac)], axis=1)
    sin = jnp.concatenate([-jnp.sin(ar), jnp.sin(ar), -jnp.sin(ac), jnp.sin(ac)], axis=1)
    return jnp.tile(cos, (1, heads)), jnp.tile(sin, (1, heads))


def _dw(a, dy, w, name):
    return _mm(a, dy, "tn", BF16, name, out_slots=w.shape[0] if w.ndim == 3 else None)


def _local_step(x, ctx, tgt, m_lat, m_ctx, p, W, goff, ffn_weights=None, send_grads=None):
    T, D = x.shape
    Tc = ctx.shape[0]
    S = T + Tc
    S5W = p["s5_d"].shape[1]
    QR, KVR = p["q_norm"].shape[1], p["kv_norm"].shape[1]
    H = W["w_uq"].shape[1] // (QK_NOPE + QK_ROPE)
    G, N = p["s5_a_re"].shape[1:]
    P = S5_GROUP
    nch = G // 8
    o_cq, o_ckv, o_kr = S5W, S5W + QR, S5W + QR + KVR
    assert o_cq % QR == 0 and o_ckv % KVR == 0 and o_kr % LANE == 0 and goff % D == 0 and S5W % LANE == 0 and G % 8 == 0
    assert 8 * P == LANE
    row = lambda k, m: m[k:k + 1]
    sh1, sc1, g1, sh2, sc2, g2 = (row(k, m_lat) for k in range(6))
    csh1, csc1 = row(0, m_ctx), row(1, m_ctx)
    n1, n2, nf = p["norm1"], p["norm2"], p["norm_f"]

    (xm_lat,) = _rowmap(_normmod, "norm1_lat", T, [x], [n1, sc1, sh1], [(D, BF16)])
    (xm_ctx,) = _rowmap(_normmod, "norm1_ctx", Tc, [ctx], [n1, csc1, csh1], [(D, BF16)])
    xm_all = jnp.concatenate([xm_lat, xm_ctx], axis=0)
    h_all = _mm(xm_all, W["w_in"], "nn", F32, "mm_in")

    a_re, a_im = p["s5_a_re"][:, :, None, :], p["s5_a_im"][:, :, None, :]
    ldt = p["s5_log_dt"][:, :, None, None]
    b_re, b_im = p["s5_b_re"].transpose(0, 1, 3, 2), p["s5_b_im"].transpose(0, 1, 3, 2)
    wre, wim, vre, vim, pwr, pwi = _s5_tables(a_re, a_im, ldt, b_re, b_im, p["s5_c_re"], p["s5_c_im"])
    waug = _s5_expand(wre, wim, "s5_expand_b")
    vaug = _s5_expand(vre, vim, "s5_expand_c")
    lanes = lambda t: t.reshape(2, SUB + 1, nch, 8 * N).transpose(0, 2, 1, 3)
    pw_re, pw_im = lanes(pwr), lanes(pwi)
    near = lambda t: t[:, :, 1:]
    far = lambda t: t[:, :, :0:-1]
    pw_c = jnp.concatenate([near(pw_re), near(pw_im)], axis=-1)
    pw_a = jnp.concatenate([far(pw_re), far(pw_im)], axis=-1)
    pwc_c = jnp.concatenate([near(pw_re), -near(pw_im)], axis=-1)
    pwc_a = jnp.concatenate([far(pw_re), -far(pw_im)], axis=-1)
    y0, hs0 = _s5_fwd(h_all, waug[0], vaug[0], pw_c[0], S5W, T, True, "s5_scan_fwd0")
    y1, hs1 = _s5_fwd(h_all, waug[1], vaug[1], pw_a[1], S5W, T, False, "s5_scan_fwd1")

    def s5_combine(u, yf, yr, dskip):
        y5 = dskip * u + yf + yr
        return y5, jax.nn.gelu(y5)

    y5, z = _rowmap(s5_combine, "s5_combine", T, [(h_all, S5W, 0), y0, y1], [p["s5_d"]], [(S5W, F32), (S5W, BF16)])

    (qn,) = _rowmap(_rms, "q_norm", T, [(h_all, QR, o_cq // QR)], [p["q_norm"]], [(QR, BF16)])
    (kvn,) = _rowmap(_rms, "kv_norm", S, [(h_all, KVR, o_ckv // KVR)], [p["kv_norm"]], [(KVR, BF16)])
    qraw = _mm(qn, W["w_uq"], "nn", F32, "mm_uq")
    kvraw = _mm(kvn, W["w_ukv"], "nn", BF16, "mm_ukv")
    cos_q, sin_q = _rope_tables(T, H)
    padl = lambda t: jnp.pad(t[:, :QK_ROPE], ((0, Tc), (0, LANE - QK_ROPE)))
    cos_k = padl(cos_q) + jnp.pad(jnp.ones((Tc, LANE), F32), ((T, 0), (0, 0)))
    sin_k = padl(sin_q)
    hn = H * QK_NOPE

    def q_post(q, cos, sin):
        return q[:, :hn], _rope(q[:, hn:], cos, sin)

    q_nope, q_rope = _rowmap(q_post, "q_rope", T, [qraw, cos_q, sin_q], [], [(hn, BF16), (H * QK_ROPE, BF16)])
    (kr,) = _rowmap(_rope, "k_rope", S, [(h_all, LANE, o_kr // LANE), cos_k, sin_k], [], [(LANE, BF16)])
    q3 = jnp.concatenate([q_nope.reshape(T, H, QK_NOPE), q_rope.reshape(T, H, QK_ROPE)], axis=-1).transpose(1, 0, 2)
    k3 = jnp.concatenate([kvraw[:, :hn].reshape(S, H, QK_NOPE),
                          jnp.broadcast_to(kr[:, None, :QK_ROPE], (S, H, QK_ROPE))], axis=-1).transpose(1, 0, 2)
    v3 = kvraw[:, hn:].reshape(S, H, V_DIM).transpose(1, 0, 2)
    scale = (QK_NOPE + QK_ROPE) ** -0.5
    o, lse = _attn_fwd(q3, k3, v3, scale)

    zz = _mm(z, W["w_glu"], "nn", BF16, "mm_glu")
    br_mla = _mm(o, W["w_mla_o"], "nn", BF16, "mm_mla_o")

    def merge(zz, brm, gs, gm):
        a, b = zz[:, :D], zz[:, D:]
        return jax.nn.sigmoid(gs) * (a * jax.nn.sigmoid(b)) + jax.nn.sigmoid(gm) * brm

    gb = goff // D
    merge_ins = [zz, br_mla, (h_all, D, gb), (h_all, D, gb + 1)]
    (mix,) = _rowmap(merge, "merge", T, merge_ins, [], [(D, BF16)])
    out1 = _mm(mix, W["w_out"], "nn", F32, "mm_out")

    def resid_norm2(x, out1, g1, n2, sc2, sh2):
        x1 = x + g1 * out1
        return x1, _normmod(x1, n2, sc2, sh2)

    x1, hm = _rowmap(resid_norm2, "resid_norm2", T, [x, out1], [g1, n2, sc2, sh2], [(D, F32), (D, BF16)])

    if ffn_weights is not None:
        W = {**W, **ffn_weights(hm)}
    FF = W["w_ffn_out"].shape[0]
    assert FF % LANE == 0
    ab = _mm(hm, W["w_ffn_in"], "nn", BF16, "mm_ffn_in")

    def swiglu_act(a, b):
        return jax.nn.silu(a) * b

    (f,) = _rowmap(swiglu_act, "ffn_act", T, [(ab, FF, 0), (ab, FF, 1)], [], [(FF, BF16)])
    out2 = _mm(f, W["w_ffn_out"], "nn", F32, "mm_ffn_out")

    def loss_rows(x1, out2, g2, nf, tgt):
        y = _rms(x1 + g2 * out2, nf)
        return 0.5 * jnp.sum(jnp.mean(jnp.square(y - tgt), axis=-1))

    def final(x1, out2, tgt, g2, nf):
        val, (dx1, dout2, dg2, dnf) = jax.value_and_grad(loss_rows, argnums=(0, 1, 2, 3))(x1, out2, g2, nf, tgt)
        return dx1, dout2, jnp.full((1, LANE), val, F32), dg2, dnf

    dx2, dout2, loss_acc, dg2, dnf = _rowmap(final, "final_loss", T, [x1, out2, tgt], [g2, nf],
                                             [(D, F32), (D, BF16)], [LANE, D, D])

    gW = {}
    df = _mm(dout2, W["w_ffn_out"], "nt", BF16, "mm_ffn_out_dx")
    gW["w_ffn_out"] = _dw(f, dout2, W["w_ffn_out"], "mm_ffn_out_dw")

    def swiglu_bwd(a, b, df):
        _, vjp = jax.vjp(swiglu_act, a, b)
        da, db = vjp(df)
        return jnp.concatenate([da, db], axis=1)

    (dab,) = _rowmap(swiglu_bwd, "ffn_act_bwd", T, [(ab, FF, 0), (ab, FF, 1), df], [], [(2 * FF, BF16)])
    dhm = _mm(dab, W["w_ffn_in"], "nt", F32, "mm_ffn_in_dx")
    gW["w_ffn_in"] = _dw(hm, dab, W["w_ffn_in"], "mm_ffn_in_dw")
    if send_grads is not None:
        token = send_grads(FFN, [gW.pop(n) for n in FFN])
        g1 = g1 if token is None else g1 + token[:1, :1]

    def resid_norm2_bwd(x, out1, dx2, dhm, g1, n2, sc2, sh2):
        _, vjp = jax.vjp(resid_norm2, x, out1, g1, n2, sc2, sh2)
        dx, dout1, dg1, dn2, dsc2, dsh2 = vjp((dx2, dhm))
        return dx, dout1, dg1, dn2, dsc2, dsh2

    dx1, dout1, dg1, dn2, dsc2, dsh2 = _rowmap(resid_norm2_bwd, "resid_norm2_bwd", T, [x, out1, dx2, dhm],
                                               [g1, n2, sc2, sh2], [(D, F32), (D, BF16)], [D, D, D, D])

    dmix = _mm(dout1, W["w_out"], "nt", BF16, "mm_out_dx")
    gW["w_out"] = _dw(mix, dout1, W["w_out"], "mm_out_dw")

    def merge_bwd(zz, brm, gs, gm, dmix):
        _, vjp = jax.vjp(merge, zz, brm, gs, gm)
        dzz, dbrm, dgs, dgm = vjp(dmix)
        return dzz, dbrm, jnp.concatenate([dgs, dgm], axis=1)

    dzz, dbrm, dgates = _rowmap(merge_bwd, "merge_bwd", T, merge_ins + [dmix], [],
                                [(2 * D, BF16), (D, BF16), (2 * D, BF16)])
    do = _mm(dbrm, W["w_mla_o"], "nt", BF16, "mm_mla_o_dx")
    gW["w_mla_o"] = _dw(o, dbrm, W["w_mla_o"], "mm_mla_o_dw")
    dz = _mm(dzz, W["w_glu"], "nt", BF16, "mm_glu_dx")
    gW["w_glu"] = _dw(z, dzz, W["w_glu"], "mm_glu_dw")
    d_skip_w = p["s5_d"]
    if send_grads is not None:
        token = send_grads(MIX, [gW.pop(n) for n in MIX])
        d_skip_w = d_skip_w if token is None else d_skip_w + token[:1, :1]

    def s5_combine_bwd(u, y5, dz, dskip):
        _, vjp = jax.vjp(lambda y: jax.nn.gelu(y), y5)
        (dy5,) = vjp(dz)
        return dy5, jnp.sum(dy5 * u, axis=0, keepdims=True)

    dy5, d_skip = _rowmap(s5_combine_bwd, "s5_combine_bwd", T, [(h_all, S5W, 0), y5, dz], [d_skip_w], [(S5W, F32)], [S5W])

    dq3, dk3, dv3 = _attn_bwd(q3, k3, v3, do, lse, scale)
    dq_t = dq3.transpose(1, 0, 2)
    dq_cat = jnp.concatenate([dq_t[:, :, :QK_NOPE].reshape(T, hn), dq_t[:, :, QK_NOPE:].reshape(T, H * QK_ROPE)], axis=1)

    def q_post_bwd(dq, cos, sin):
        return jnp.concatenate([dq[:, :hn], _rope_bwd(dq[:, hn:], cos, sin)], axis=1)

    (dqraw,) = _rowmap(q_post_bwd, "q_rope_bwd", T, [dq_cat, cos_q, sin_q], [], [(H * (QK_NOPE + QK_ROPE), BF16)])
    dk_t = dk3.transpose(1, 0, 2)
    dkvraw = jnp.concatenate([dk_t[:, :, :QK_NOPE].reshape(S, hn), dv3.transpose(1, 0, 2).reshape(S, H * V_DIM)], axis=1)
    dkr_heads = jnp.pad(dk_t[:, :, QK_NOPE:], ((0, 0), (0, 0), (0, LANE - QK_ROPE))).reshape(S, H * LANE)

    def k_rope_bwd(dkh, cos, sin):
        d = dkh[:, :LANE]
        for h in range(1, H):
            d = d + dkh[:, h * LANE:(h + 1) * LANE]
        return _rope_bwd(d, cos, sin)

    (dkr,) = _rowmap(k_rope_bwd, "k_rope_bwd", S, [dkr_heads, cos_k, sin_k], [], [(LANE, BF16)])
    dqn = _mm(dqraw, W["w_uq"], "nt", F32, "mm_uq_dx")
    gW["w_uq"] = _dw(qn, dqraw, W["w_uq"], "mm_uq_dw")
    dkvn = _mm(dkvraw, W["w_ukv"], "nt", F32, "mm_ukv_dx")
    gW["w_ukv"] = _dw(kvn, dkvraw, W["w_ukv"], "mm_ukv_dw")

    def rms_bwd(cx, dn, g):
        _, vjp = jax.vjp(_rms, cx, g)
        return vjp(dn)

    dcq, dq_norm = _rowmap(rms_bwd, "q_norm_bwd", T, [(h_all, QR, o_cq // QR), dqn], [p["q_norm"]], [(QR, BF16)], [QR])
    dckv, dkv_norm = _rowmap(rms_bwd, "kv_norm_bwd", S, [(h_all, KVR, o_ckv // KVR), dkvn], [p["kv_norm"]],
                             [(KVR, BF16)], [KVR])

    dy_all = jnp.concatenate([dy5, jnp.zeros((Tc, S5W), F32)], axis=0)
    du0, dbb0, dc0, da0 = _s5_bwd(dy_all, h_all, hs0, waug[0], vaug[0], pwc_a[0], S5W, T, True, "s5_scan_bwd0")
    du1, dbb1, dc1, da1 = _s5_bwd(dy_all, h_all, hs1, waug[1], vaug[1], pwc_c[1], S5W, T, False, "s5_scan_bwd1")

    def du_combine(a, b, dy, dskip):
        return a + b + dskip * dy

    (du_all,) = _rowmap(du_combine, "s5_du", S, [du0, du1, dy_all], [p["s5_d"]], [(S5W, BF16)])
    dbb = jnp.einsum("dsgpcgn->dcsgpn", jnp.stack([dbb0, dbb1]).reshape(2, nch, 8, P, 2, 8, N)).reshape(2, 2, G, P, N)
    dcm = jnp.einsum("dscgngp->dcsgpn", jnp.stack([dc0, dc1]).reshape(2, nch, 2, 8, N, 8, P)).reshape(2, 2, G, P, N)
    da = jnp.stack([da0, da1]).reshape(2, nch, 2, 8, N).transpose(0, 2, 1, 3, 4).reshape(2, 2, G, 1, N)
    d_lr, d_li, d_ldt, d_br, d_bi = _s5_param_bwd(a_re, a_im, ldt, b_re, b_im, da[:, 0], da[:, 1], dbb[:, 0], dbb[:, 1])

    lat_only = lambda t: jnp.pad(t, ((0, Tc), (0, 0)))
    dh_all = jnp.concatenate([du_all, lat_only(dcq), dckv, dkr, jnp.zeros((S, goff - o_kr - LANE), BF16), lat_only(dgates)],
                             axis=1)
    dxm = _mm(dh_all, W["w_in"], "nt", F32, "mm_in_dx")
    gW["w_in"] = _dw(xm_all, dh_all, W["w_in"], "mm_in_dw")

    def norm1_bwd(x, dxm, dx1, n1, sc, sh):
        _, vjp = jax.vjp(_normmod, x, n1, sc, sh)
        dx, dn, dsc, dsh = vjp(dxm)
        return dx + dx1, dn, dsc, dsh

    grad_x, dn1_l, dsc1, dsh1 = _rowmap(norm1_bwd, "norm1_lat_bwd", T, [x, dxm, dx1], [n1, sc1, sh1], [(D, F32)], [D, D, D])

    def norm1_ctx_bwd(x, dxm, n1, sc, sh):
        _, vjp = jax.vjp(_normmod, x, n1, sc, sh)
        return vjp(dxm)[1:]

    dn1_c, dcsc1, dcsh1 = _rowmap(norm1_ctx_bwd, "norm1_ctx_bwd", Tc, [ctx, dxm[T:]], [n1, csc1, csh1], [], [D, D, D])

    zero = jnp.zeros((1, D), F32)
    dm_lat = jnp.concatenate([dsh1, dsc1, dg1, dsh2, dsc2, dg2], axis=0)
    dm_ctx = jnp.concatenate([dcsh1, dcsc1, zero, zero, zero, zero], axis=0)
    small = {
        "norm1": dn1_l + dn1_c, "norm2": dn2, "norm_f": dnf, "q_norm": dq_norm, "kv_norm": dkv_norm, "s5_d": d_skip,
        "s5_a_re": d_lr, "s5_a_im": d_li, "s5_log_dt": d_ldt, "s5_b_re": d_br.transpose(0, 1, 3, 2),
        "s5_b_im": d_bi.transpose(0, 1, 3, 2), "s5_c_re": dcm[:, 0], "s5_c_im": -dcm[:, 1],
    }
    return loss_acc[:, :1], grad_x, small, dm_lat, dm_ctx, gW


BIG = ("w_in", "w_uq", "w_ukv", "w_glu", "w_mla_o", "w_out", "w_ffn_in", "w_ffn_out")
FFN = ("w_ffn_in", "w_ffn_out")
MIX = ("w_out", "w_mla_o", "w_glu")
ROW_SHARDED = ("w_out", "w_ffn_out")
RELAID = ("w_in", "w_uq", "w_ukv")
SMALL = ("c_ctx", "b_mod", "norm1", "norm2", "s5_a_re", "s5_a_im", "s5_log_dt", "s5_b_re", "s5_b_im", "s5_c_re",
         "s5_c_im", "s5_d", "q_norm", "kv_norm", "norm_f")
WEIGHTS = ("c_ctx", "w_mod", "b_mod", "norm1", "norm2", "w_in", "s5_a_re", "s5_a_im", "s5_log_dt", "s5_b_re", "s5_b_im",
           "s5_c_re", "s5_c_im", "s5_d", "w_glu", "q_norm", "kv_norm", "w_uq", "w_ukv", "w_mla_o", "w_out", "w_ffn_in",
           "w_ffn_out", "norm_f")


def _heads_split(w, heads, first):
    k = w.shape[0]
    w3 = w.reshape(k, heads, -1)
    return jnp.concatenate([w3[:, :, :first].reshape(k, -1), w3[:, :, first:].reshape(k, -1)], axis=1)


def _heads_merge(w, heads, first):
    k = w.shape[0]
    a, b = w[:, :heads * first].reshape(k, heads, first), w[:, heads * first:].reshape(k, heads, -1)
    return jnp.concatenate([a, b], axis=2).reshape(k, -1)


def _cols_full(w8):
    return w8.transpose(1, 0, 2).reshape(w8.shape[1], -1)


def _cols_slots(w):
    return w.reshape(w.shape[0], N_DEV, -1).transpose(1, 0, 2)


def _weight_layout(n, w8):
    if n in ROW_SHARDED:
        return w8.reshape(-1, w8.shape[-1])
    return _cols_full(w8) if (n in RELAID or w8.shape[-1] % LANE) else w8


def _grad_slots(n, g):
    if g.ndim == 3:
        return g
    return g.reshape(N_DEV, g.shape[0] // N_DEV, g.shape[1]) if n in ROW_SHARDED else _cols_slots(g)


def _model_weights(g8, D):
    W = {n: _weight_layout(n, w8) for n, w8 in g8.items()}
    w_in = W["w_in"]
    n_front = w_in.shape[1] - 2 * D
    goff = -(-n_front // D) * D
    W["w_in"] = jnp.concatenate([w_in[:, :n_front], jnp.zeros((D, goff - n_front), w_in.dtype), w_in[:, n_front:]], axis=1)
    heads = W["w_uq"].shape[1] // (QK_NOPE + QK_ROPE)
    W["w_uq"] = _heads_split(W["w_uq"], heads, QK_NOPE)
    W["w_ukv"] = _heads_split(W["w_ukv"], heads, QK_NOPE)
    return W, goff


def kernel(x, c, ctx, c_ctx, w_mod, b_mod, norm1, norm2, w_in, s5_a_re, s5_a_im, s5_log_dt, s5_b_re, s5_b_im, s5_c_re, s5_c_im, s5_d, w_glu, q_norm, kv_norm, w_uq, w_ukv, w_mla_o, w_out, w_ffn_in, w_ffn_out, norm_f, loss_target, m_c_ctx, m_w_mod, m_b_mod, m_norm1, m_norm2, m_w_in, m_s5_a_re, m_s5_a_im, m_s5_log_dt, m_s5_b_re, m_s5_b_im, m_s5_c_re, m_s5_c_im, m_s5_d, m_w_glu, m_q_norm, m_kv_norm, m_w_uq, m_w_ukv, m_w_mla_o, m_w_out, m_w_ffn_in, m_w_ffn_out, m_norm_f, v_c_ctx, v_w_mod, v_b_mod, v_norm1, v_norm2, v_w_in, v_s5_a_re, v_s5_a_im, v_s5_log_dt, v_s5_b_re, v_s5_b_im, v_s5_c_re, v_s5_c_im, v_s5_d, v_w_glu, v_q_norm, v_kv_norm, v_w_uq, v_w_ukv, v_w_mla_o, v_w_out, v_w_ffn_in, v_w_ffn_out, v_norm_f):
    a = dict(locals())
    D = x.shape[-1]
    me = 4 * lax.axis_index("x") + 2 * lax.axis_index("y") + lax.axis_index("c")

    shard = {n: a[n][0] for n in BIG}
    first = [n for n in BIG if n not in FFN]
    gathered = _all_gather([shard[n].astype(BF16) for n in first] + [jnp.broadcast_to(c, (8, D))], "ag_weights")
    W, goff = _model_weights(dict(zip(first, gathered[:-1])), D)
    cg = gathered[-1]

    wm = w_mod[0]
    ncol = wm.shape[1]
    c16 = jnp.concatenate([cg[:, 0, :], c_ctx[None], jnp.zeros((7, D), F32)], axis=0)
    (s16,) = _rowmap(jax.nn.silu, "mod_silu", 16, [c16], [], [(D, BF16)])
    m_cols = _mm(s16, wm, "nn", F32, "mm_mod")
    (mg,) = _all_gather([m_cols], "ag_mod")
    (m16,) = _rowmap(lambda m, b: m + b, "mod_bias", 16, [_cols_full(mg)], [b_mod], [(N_DEV * ncol, F32)])

    ffn_blocks = [shard[n].astype(BF16) for n in FFN]
    ag_send, ag_recv, ag_thru, ag_token = _xchg_start(ffn_blocks, [_own_slot(b, me) for b in ffn_blocks], False, "ag_ffn_start")
    m16 = m16 + ag_token[:1, :1]

    def ffn_weights(after):
        lands = _xchg_wait(ag_send, ag_recv, ag_thru, after, False, "ag_ffn_wait")
        return {n: _weight_layout(n, w8) for n, w8 in zip(FFN, lands)}

    rs_async = {}

    def send_grads(names, gs):
        slots = [_grad_slots(n, g) for n, g in zip(names, gs)]
        lands = [_own_slot(lax.dynamic_index_in_dim(s, me, 0, keepdims=False), me) for s in slots]
        rs_async[names] = _xchg_start(slots, lands, True, "rs_start_" + names[0])
        return rs_async[names][3]

    m_lat = lax.dynamic_slice(m16, (me, 0), (1, 6 * D)).reshape(6, D)
    m_ctx = m16[8].reshape(6, D)

    p = {n: a[n][0] for n in ("norm1", "norm2", "s5_a_re", "s5_a_im", "s5_log_dt", "s5_b_re", "s5_b_im", "s5_c_re",
                              "s5_c_im", "q_norm", "kv_norm")}
    p = {k: (v[None] if v.ndim == 1 else v) for k, v in p.items()}
    p["s5_d"] = s5_d.reshape(1, -1)
    p["norm_f"] = norm_f[None]
    loss_part, grad_x, small, dm_lat, dm_ctx, gW = _local_step(x[0], ctx[0], loss_target[0], m_lat, m_ctx, p, W, goff,
                                                               ffn_weights, send_grads)
    loss = lax.psum(loss_part[0, 0], ("x", "y", "c"))

    dm16 = jnp.concatenate([dm_lat.reshape(1, -1), dm_ctx.reshape(1, -1), jnp.zeros((14, 6 * D), F32)], axis=0)
    (dmg,) = _all_gather([dm16], "ag_dmod")
    dm_sum = _sum_slots(dmg, "sum_dmod")
    dM16 = jnp.concatenate([dmg[:, 0, :], dm_sum[1:2], jnp.zeros((7, 6 * D), F32)], axis=0)
    (g_b_mod,) = _rowmap(lambda d: jnp.sum(d, axis=0, keepdims=True), "b_mod_grad", 16, [dM16], [], [], [6 * D])
    dM_loc = lax.dynamic_slice(dM16, (0, me * ncol), (16, ncol))
    g_w_mod = _mm(s16, dM_loc, "tn", F32, "mm_mod_dw")
    ds16_part = _mm(dM_loc, wm, "nt", F32, "mm_mod_dx")

    small_names = [n for n in SMALL if n not in ("c_ctx", "b_mod")]
    small_shapes = [small[n].shape for n in small_names] + [(1, D)]
    (sg,) = _all_gather([_pack_rows([small[n] for n in small_names] + [ds16_part[8:9]], F32)], "ag_small")
    parts = _unpack_rows(_sum_slots(sg, "sum_small"), small_shapes)
    grads = dict(zip(small_names, parts[:-1]))

    def silu_bwd(cc, ds):
        _, vjp = jax.vjp(jax.nn.silu, cc)
        return vjp(ds)[0]

    (g_c_ctx,) = _rowmap(silu_bwd, "c_ctx_grad", 1, [c_ctx[None], parts[-1]], [], [(D, F32)])
    grads["c_ctx"], grads["b_mod"] = g_c_ctx, g_b_mod

    gW = dict(gW)
    n_front = w_in.shape[-1] * N_DEV - 2 * D
    gW["w_in"] = jnp.concatenate([gW["w_in"][:, :n_front], gW["w_in"][:, goff:]], axis=1)
    heads = gW["w_uq"].shape[1] // (QK_NOPE + QK_ROPE)
    gW["w_uq"] = _heads_merge(gW["w_uq"], heads, QK_NOPE)
    gW["w_ukv"] = _heads_merge(gW["w_ukv"], heads, QK_NOPE)
    last = [n for n in BIG if n in gW]
    slots = [_grad_slots(n, gW[n]) for n in last]
    from_sibling = _rs_pair(slots, "rs_pair")
    chip_sums = [_add_pair(pp, rr, "rs_add_" + n) for n, pp, rr in zip(last, slots, from_sibling)]
    from_chips = _rs_chips(chip_sums, "rs_chips")
    for n, g4 in zip(last, from_chips):
        grads[n] = _sum_slots(g4, "rs_sum_" + n)
    for names, (send, recv, thru, _) in rs_async.items():
        for n, g8 in zip(names, _xchg_wait(send, recv, thru, from_chips[0], True, "rs_wait_" + names[0])):
            grads[n] = _sum_slots(g8, "rs_sum_" + n)
    grads["w_mod"] = g_w_mod

    out = {}
    for n in BIG + ("w_mod",):
        d, nm, nv = _adamw(a[n][0], grads[n], a["m_" + n][0], a["v_" + n][0], "adamw_" + n)
        for k, val in (("grad_", grads[n]), ("delta_", d), ("new_m_", nm), ("new_v_", nv)):
            out[k + n] = val.reshape(a[n].shape)
    packs = [_pack_rows([t[n] for n in SMALL], F32) for t in (
        {n: a[n] for n in SMALL}, {n: grads[n] for n in SMALL}, {n: a["m_" + n] for n in SMALL}, {n: a["v_" + n] for n in SMALL})]
    res = _adamw(*packs, "adamw_small")
    shapes = [a[n].shape for n in SMALL]
    for k, packed in (("grad_", packs[1]), ("delta_", res[0]), ("new_m_", res[1]), ("new_v_", res[2])):
        for n, val in zip(SMALL, _unpack_rows(packed, shapes)):
            out[k + n] = val
    return (loss, grad_x[None]) + tuple(out[k + n] for k in ("grad_", "delta_", "new_m_", "new_v_") for n in WEIGHTS)
```

```python
import functools
import math

import jax
import jax.numpy as jnp
from jax import lax
from jax.experimental import pallas as pl
from jax.experimental.pallas import tpu as pltpu

F32 = jnp.float32
BF16 = jnp.bfloat16

N_DEV = 8
N_CHIP = 4
EPS = 1e-6
GRID_W = 64
S5_GROUP = 16
QK_NOPE, QK_ROPE, V_DIM = 128, 64, 128
ROPE_BASE = 10000.0
ADAM_LR, ADAM_B1, ADAM_B2, ADAM_EPS, ADAM_WD, ADAM_STEP = 0.001, 0.9, 0.999, 1e-08, 0.01, 10

LANE = 128
SUB = 8
PACK_W = 1024
PACK_ROWS = 32
VMEM_LIMIT = 48 << 20
ROWMAP_TILE_BYTES = 20 << 20
MM_VMEM_BUDGET = 36 << 20
MESH = pl.DeviceIdType.MESH
_NT = (((1,), (1,)), ((), ()))
_TN = (((0,), (0,)), ((), ()))


def _pick(dim, cands):
    for c in cands:
        if dim % c == 0:
            return c
    return dim


def _cparams(sem):
    return pltpu.CompilerParams(dimension_semantics=sem, vmem_limit_bytes=VMEM_LIMIT)


def _mm(a, b, dims, out_dtype, name, out_slots=None):
    a = a.astype(BF16)
    b = b.astype(BF16)
    b3 = b.ndim == 3
    if dims == "nn":
        (M, K), N = a.shape, (b.shape[0] * b.shape[2] if b3 else b.shape[1])
    elif dims == "nt":
        M, N = a.shape[0], b.shape[-2]
        K = b.shape[0] * b.shape[2] if b3 else b.shape[1]
    else:
        (K, M), N = a.shape, b.shape[1]
    unit_n = b.shape[2] if (b3 and dims == "nn") else (N // out_slots if out_slots else N)
    unit_k = b.shape[2] if (b3 and dims == "nt") else K
    osz = jnp.dtype(out_dtype).itemsize
    tm, tn, tk = _mm_tiles(M, unit_n, unit_k, osz, LANE if dims == "tn" else 16)
    nk, npt, kpt = K // tk, unit_n // tn, unit_k // tk
    use_acc = nk > 1 and out_dtype != F32
    if dims == "nn":
        a_spec = pl.BlockSpec((tm, tk), lambda i, j, k: (i, k))
        b_spec = (pl.BlockSpec((None, tk, tn), lambda i, j, k: (j // npt, k, j % npt)) if b3
                  else pl.BlockSpec((tk, tn), lambda i, j, k: (k, j)))
        dn = (((1,), (0,)), ((), ()))
    elif dims == "nt":
        a_spec = pl.BlockSpec((tm, tk), lambda i, j, k: (i, k))
        b_spec = (pl.BlockSpec((None, tn, tk), lambda i, j, k: (k // kpt, j, k % kpt)) if b3
                  else pl.BlockSpec((tn, tk), lambda i, j, k: (j, k)))
        dn = _NT
    else:
        a_spec = pl.BlockSpec((tk, tm), lambda i, j, k: (k, i))
        b_spec = pl.BlockSpec((tk, tn), lambda i, j, k: (k, j))
        dn = _TN
    if out_slots:
        out_spec = pl.BlockSpec((None, tm, tn), lambda i, j, k: (j // npt, i, j % npt))
        out_shape = jax.ShapeDtypeStruct((out_slots, M, unit_n), out_dtype)
    else:
        out_spec = pl.BlockSpec((tm, tn), lambda i, j, k: (i, j))
        out_shape = jax.ShapeDtypeStruct((M, N), out_dtype)

    def body(a_ref, b_ref, o_ref, *scratch):
        part = lax.dot_general(a_ref[...], b_ref[...], dn, preferred_element_type=F32)
        if nk == 1:
            o_ref[...] = part.astype(o_ref.dtype)
            return
        acc_ref = scratch[0] if use_acc else o_ref
        k = pl.program_id(2)

        @pl.when(k == 0)
        def _():
            acc_ref[...] = part

        @pl.when(k > 0)
        def _():
            acc_ref[...] += part

        if use_acc:
            @pl.when(k == nk - 1)
            def _():
                o_ref[...] = acc_ref[...].astype(o_ref.dtype)

    return pl.pallas_call(
        body, name=name, grid=(M // tm, N // tn, nk),
        in_specs=[a_spec, b_spec], out_specs=out_spec, out_shape=out_shape,
        scratch_shapes=[pltpu.VMEM((tm, tn), F32)] if use_acc else [],
        compiler_params=_cparams(("parallel", "parallel", "arbitrary")),
    )(a, b)


def _divisors(n, mult, cap):
    d = [t for t in range(mult, min(n, cap) + 1, mult) if n % t == 0]
    return d[::-1] or [n]


def _mm_tiles(M, unit_n, unit_k, out_itemsize, tm_mult):
    best = None
    for tk in _divisors(unit_k, LANE, 2816):
        for tn in _divisors(unit_n, LANE, 1536):
            for tm in _divisors(M, tm_mult, 1024):
                vmem = 2 * 2 * (tm * tk + tk * tn) + 2 * tm * tn * out_itemsize + 4 * tm * tn * (2 if unit_k > tk else 1)
                if vmem > MM_VMEM_BUDGET:
                    continue
                steps = (M // tm) * (unit_n // tn) * (unit_k // tk)
                key = (steps, -tk, -tn)
                if best is None or key < best[0]:
                    best = (key, (tm, tn, tk))
                break
    return best[1]


def _rowmap(fn, name, M, row_ins, bc_ins, row_outs, acc_outs=(), after=None):
    row_ins = [r if isinstance(r, tuple) else (r, r.shape[1], 0) for r in row_ins]
    row_bytes = sum(w * a.dtype.itemsize for a, w, _ in row_ins) + sum(w * jnp.dtype(d).itemsize for w, d in row_outs)
    widest = max([w for _, w, _ in row_ins] + [w for w, _ in row_outs])
    row_bytes = 2 * row_bytes + 6 * 4 * widest
    tm = _pick(M, [t for t in (512, 256, 128, 64, 32, 16) if t * row_bytes <= ROWMAP_TILE_BYTES] + [16])
    n_in, n_row, n_acc = len(row_ins) + len(bc_ins), len(row_outs), len(acc_outs)

    def body(*refs):
        res = fn(*[r[...].astype(F32) for r in refs[:n_in]])
        res = res if isinstance(res, (tuple, list)) else (res,)
        outs = refs[n_in + (after is not None):]
        for k in range(n_row):
            outs[k][...] = res[k].astype(outs[k].dtype)
        if n_acc:
            @pl.when(pl.program_id(0) == 0)
            def _():
                for k in range(n_acc):
                    outs[n_row + k][...] = jnp.zeros_like(outs[n_row + k])

            for k in range(n_acc):
                outs[n_row + k][...] += res[n_row + k].astype(F32)

    in_specs = [pl.BlockSpec((tm, w), functools.partial(lambda i, blk: (i, blk), blk=blk)) for _, w, blk in row_ins]
    in_specs += [pl.BlockSpec(b.shape, lambda i: (0, 0)) for b in bc_ins]
    in_specs += [pl.BlockSpec(memory_space=pl.ANY)] * (after is not None)
    out_specs = [pl.BlockSpec((tm, w), lambda i: (i, 0)) for w, _ in row_outs]
    out_specs += [pl.BlockSpec((1, w), lambda i: (0, 0)) for w in acc_outs]
    out_shape = [jax.ShapeDtypeStruct((M, w), d) for w, d in row_outs]
    out_shape += [jax.ShapeDtypeStruct((1, w), F32) for w in acc_outs]
    return pl.pallas_call(
        body, name=name, grid=(M // tm,), in_specs=in_specs, out_specs=out_specs, out_shape=out_shape,
        compiler_params=_cparams(("arbitrary",) if n_acc else ("parallel",)),
    )(*[a for a, _, _ in row_ins], *bc_ins, *([] if after is None else [after]))


def _rms(x, g):
    return x * lax.rsqrt(jnp.mean(x * x, axis=-1, keepdims=True) + EPS) * g


def _normmod(x, g, sc, sh):
    return _rms(x, g) * (1.0 + sc) + sh


def _swap16(v):
    w = v.shape[1]
    lane = lax.broadcasted_iota(jnp.int32, v.shape, 1)
    return jnp.where((lane // 16) % 2 == 0, pltpu.roll(v, w - 16, 1), pltpu.roll(v, 16, 1))


def _rope(v, cos, sin_signed):
    return v * cos + _swap16(v) * sin_signed


def _rope_bwd(d, cos, sin_signed):
    return d * cos + _swap16(d * sin_signed)


def _mesh_pos():
    return lax.axis_index("x"), lax.axis_index("y"), lax.axis_index("c")


def _hbm_call(body, name, ins, out_shapes, n_sems):
    any_spec = pl.BlockSpec(memory_space=pl.ANY)
    return pl.pallas_call(
        body, name=name, out_shape=out_shapes, in_specs=[any_spec] * len(ins), out_specs=[any_spec] * len(out_shapes),
        scratch_shapes=[pltpu.SemaphoreType.DMA((n_sems,)), pltpu.SemaphoreType.DMA((n_sems,)),
                        pltpu.SemaphoreType.DMA((len(ins),))],
    )(*ins)


def _all_gather(xs, name):
    n = len(xs)

    def body(*refs):
        x_refs, out_refs, (send_sems, recv_sems, local_sems) = refs[:n], refs[n:2 * n], refs[2 * n:]
        x, y, c = _mesh_pos()
        me, sibling = (x, y, c), (x, y, 1 - c)
        chips = [(1 - x, y), (x, 1 - y), (1 - x, 1 - y)]
        locals_, first, passed, arrivals = [], [], [], []
        for a in range(n):
            def slot(px, py, pc, a=a):
                return out_refs[a].at[4 * px + 2 * py + pc]

            def copy(k, block, to, src=None, a=a, slot=slot):
                return pltpu.make_async_remote_copy(
                    src_ref=slot(*block) if src is None else src, dst_ref=slot(*block),
                    send_sem=send_sems.at[7 * a + k], recv_sem=recv_sems.at[7 * a + k], device_id=to, device_id_type=MESH)

            locals_.append(pltpu.make_async_copy(x_refs[a], slot(*me), local_sems.at[a]))
            first.append(copy(0, me, sibling, src=x_refs[a]))
            first += [copy(1 + j, me, (*chip, c), src=x_refs[a]) for j, chip in enumerate(chips)]
            passed.append([copy(4 + j, (*chip, c), sibling) for j, chip in enumerate(chips)])
            arrivals.append([copy(1 + j, (*chip, c), me) for j, chip in enumerate(chips)]
                            + [copy(0, sibling, me)] + [copy(4 + j, (*chip, 1 - c), me) for j, chip in enumerate(chips)])
        for cp in locals_ + first:
            cp.start()
        for j in range(3):
            for a in range(n):
                arrivals[a][j].wait_recv()
                passed[a][j].start()
        for a in range(n):
            for cp in arrivals[a][3:]:
                cp.wait_recv()
        for cp in first + [p for ps in passed for p in ps]:
            cp.wait_send()
        for cp in locals_:
            cp.wait()

    return _hbm_call(body, name, xs, [jax.ShapeDtypeStruct((N_DEV,) + x.shape, x.dtype) for x in xs], 7 * n)


def _rs_pair(ps, name):
    n = len(ps)

    def body(*refs):
        p_refs, out_refs, (send_sems, recv_sems, _) = refs[:n], refs[n:2 * n], refs[2 * n:]
        x, y, c = _mesh_pos()
        sends, recvs = [], []
        for a in range(n):
            for q in range(N_CHIP):
                sem = dict(send_sem=send_sems.at[4 * a + q], recv_sem=recv_sems.at[4 * a + q],
                           device_id=(x, y, 1 - c), device_id_type=MESH)
                sends.append(pltpu.make_async_remote_copy(src_ref=p_refs[a].at[2 * q + 1 - c], dst_ref=out_refs[a].at[q], **sem))
                recvs.append(pltpu.make_async_remote_copy(src_ref=p_refs[a].at[2 * q + c], dst_ref=out_refs[a].at[q], **sem))
        for cp in sends:
            cp.start()
        for cp in recvs:
            cp.wait_recv()
        for cp in sends:
            cp.wait_send()

    return _hbm_call(body, name, ps, [jax.ShapeDtypeStruct((N_CHIP,) + p.shape[1:], p.dtype) for p in ps], 4 * n)


def _xchg_copies(src_refs, land_refs, send_sems, recv_sems, slot_src):
    x, y, c = _mesh_pos()
    sends, recvs = [], []
    for a, (src, land) in enumerate(zip(src_refs, land_refs)):
        chips = land.shape[0] == N_CHIP
        npeer = land.shape[0] - 1
        me = 2 * x + y if chips else 4 * x + 2 * y + c
        for r in range(1, npeer + 1):
            px = 1 - x if r & (2 if chips else 4) else x
            py = 1 - y if r & (1 if chips else 2) else y
            pc = c if chips else (1 - c if r & 1 else c)
            peer = 2 * px + py if chips else 4 * px + 2 * py + pc
            sem = dict(send_sem=send_sems.at[npeer * a + r - 1], recv_sem=recv_sems.at[npeer * a + r - 1],
                       device_id=(px, py, pc), device_id_type=MESH)
            s = src.at[peer] if slot_src else src
            sends.append(pltpu.make_async_remote_copy(src_ref=s, dst_ref=land.at[me], **sem))
            recvs.append(pltpu.make_async_remote_copy(src_ref=s, dst_ref=land.at[peer], **sem))
    return sends, recvs


_HBM = pl.BlockSpec(memory_space=pltpu.HBM)
_SEM = pl.BlockSpec(memory_space=pltpu.SEMAPHORE)
_EFFECT = pltpu.SideEffectType.DATAFLOW_SIDE_EFFECTING


def _xchg_start(srcs, lands, slot_src, name):
    n = len(srcs)

    def body(*refs):
        sends, _ = _xchg_copies(refs[:n], refs[n:2 * n], refs[2 * n], refs[2 * n + 1], slot_src)
        for cp in sends:
            cp.start()
        refs[-1][...] = jnp.zeros_like(refs[-1])

    bufs = list(srcs) + list(lands)
    n_sems = n * (lands[0].shape[0] - 1)
    res = pl.pallas_call(
        body, name=name,
        out_shape=(pltpu.SemaphoreType.DMA((n_sems,)), pltpu.SemaphoreType.DMA((n_sems,)))
        + tuple(pltpu.HBM(b.shape, b.dtype) for b in bufs) + (jax.ShapeDtypeStruct((SUB, LANE), F32),),
        in_specs=(_HBM,) * (2 * n), out_specs=(_SEM, _SEM) + (_HBM,) * (2 * n) + (pl.BlockSpec(memory_space=pltpu.VMEM),),
        input_output_aliases={i: 2 + i for i in range(2 * n)},
        compiler_params=pltpu.CompilerParams(has_side_effects=_EFFECT),
    )(*[pltpu.with_memory_space_constraint(b, pltpu.HBM) for b in bufs])
    return res[0], res[1], res[2:-1], res[-1]


def _xchg_wait(send_sems, recv_sems, thru, after, slot_src, name):
    n = len(thru) // 2

    def body(*refs):
        sends, recvs = _xchg_copies(refs[:n], refs[n:2 * n], refs[2 * n], refs[2 * n + 1], slot_src)
        for cp in sends:
            cp.wait_send()
        for cp in recvs:
            cp.wait_recv()

    res = pl.pallas_call(
        body, name=name, out_shape=tuple(pltpu.HBM(b.shape, b.dtype) for b in thru),
        in_specs=(_HBM,) * (2 * n) + (_SEM, _SEM, pl.BlockSpec(memory_space=pl.ANY)), out_specs=(_HBM,) * (2 * n),
        input_output_aliases={i: i for i in range(2 * n)},
        compiler_params=pltpu.CompilerParams(has_side_effects=_EFFECT),
    )(*thru, send_sems, recv_sems, after)
    return res[n:]


def _own_slot(block, me, slots=N_DEV):
    return lax.dynamic_update_slice(lax.empty((slots,) + block.shape, block.dtype), block[None], (me, 0, 0))


def _add_pair(p, r, name):
    _, R, C = p.shape
    tr = _pick(R, (512, 256, 128, 64, 32, 16))

    def body(c_ref, p_ref, r_ref, o_ref):
        o_ref[...] = (p_ref[...].astype(F32) + r_ref[...].astype(F32)).astype(o_ref.dtype)

    return pl.pallas_call(
        body, name=name, out_shape=jax.ShapeDtypeStruct((N_CHIP, R, C), p.dtype),
        grid_spec=pltpu.PrefetchScalarGridSpec(
            num_scalar_prefetch=1, grid=(N_CHIP, R // tr),
            in_specs=[pl.BlockSpec((None, None, tr, C), lambda q, i, c_ref: (q, c_ref[0], i, 0)),
                      pl.BlockSpec((None, tr, C), lambda q, i, c_ref: (q, i, 0))],
            out_specs=pl.BlockSpec((None, tr, C), lambda q, i, c_ref: (q, i, 0))),
        compiler_params=_cparams(("parallel", "parallel")),
    )(lax.axis_index("c").reshape(1).astype(jnp.int32), p.reshape(N_CHIP, 2, R, C), r)


def _sum_slots(g, name):
    ns, R, C = g.shape
    tr = _pick(R, (256, 128, 64, 32, 16))

    def body(g_ref, o_ref):
        acc = g_ref[0].astype(F32)
        for j in range(1, ns):
            acc = acc + g_ref[j].astype(F32)
        o_ref[...] = acc

    return pl.pallas_call(
        body, name=name, grid=(R // tr,),
        in_specs=[pl.BlockSpec((ns, tr, C), lambda i: (0, i, 0))], out_specs=pl.BlockSpec((tr, C), lambda i: (i, 0)),
        out_shape=jax.ShapeDtypeStruct((R, C), F32), compiler_params=_cparams(("parallel",)),
    )(g)


def _pack_rows(arrs, dtype):
    parts = []
    for a in arrs:
        flat = a.reshape(-1).astype(dtype)
        pad = (-flat.shape[0]) % (PACK_W * 16)
        parts.append(jnp.pad(flat, (0, pad)).reshape(-1, PACK_W))
    out = jnp.concatenate(parts, axis=0)
    return jnp.pad(out, ((0, (-out.shape[0]) % PACK_ROWS), (0, 0)))


def _packed_rows(shape):
    n = math.prod(shape)
    return (n + PACK_W * 16 - 1) // (PACK_W * 16) * 16


def _unpack_rows(packed, shapes):
    out, r0 = [], 0
    for s in shapes:
        rows, n = _packed_rows(s), math.prod(s)
        out.append(packed[r0:r0 + rows].reshape(rows * PACK_W)[:n].reshape(s))
        r0 += rows
    return out


def _adamw_math(w, g, m, v):
    m = ADAM_B1 * m + (1.0 - ADAM_B1) * g
    v = ADAM_B2 * v + (1.0 - ADAM_B2) * (g * g)
    m_hat = m / (1.0 - ADAM_B1 ** ADAM_STEP)
    v_hat = v / (1.0 - ADAM_B2 ** ADAM_STEP)
    delta = -ADAM_LR * (m_hat / (jnp.sqrt(v_hat) + ADAM_EPS) + ADAM_WD * w)
    return delta, m, v


def _adamw(w, g, m, v, name, after=None):
    R, C = w.shape
    return _rowmap(_adamw_math, name, R, [w, g, m, v], [], [(C, F32)] * 3, after=after)


def _s5_disc_math(lr, li, ldt, br, bi):
    dt = jnp.exp(ldt)
    mag = jnp.exp(lr * dt)
    ab_re, ab_im = mag * jnp.cos(li * dt), mag * jnp.sin(li * dt)
    den = lr * lr + li * li
    nr, ni = ab_re - 1.0, ab_im
    co_re = (nr * lr + ni * li) / den
    co_im = (ni * lr - nr * li) / den
    bb_re = co_re * br - co_im * bi
    bb_im = co_re * bi + co_im * br
    return ab_re, ab_im, bb_re, bb_im


def _s5_tables(a_re, a_im, ldt, b_re, b_im, c_re, c_im):
    _, G, P, N = b_re.shape
    nch = G // 8

    def body(lr_ref, li_ref, ldt_ref, br_ref, bi_ref, cr_ref, ci_ref, wre, wim, vre, vim, pwr, pwi):
        ar, ai, bb_re, bb_im = _s5_disc_math(lr_ref[0], li_ref[0], ldt_ref[0], br_ref[0], bi_ref[0])
        cr, ci = cr_ref[0], ci_ref[0]
        pr, pi = jnp.ones_like(ar), jnp.zeros_like(ar)
        for j in range(SUB + 1):
            pwr[0, j], pwi[0, j] = pr, pi
            if j < SUB:
                tabs = ((wre, bb_re * pr - bb_im * pi), (wim, bb_re * pi + bb_im * pr),
                        (vre, cr * pr - ci * pi), (vim, -(cr * pi + ci * pr)))
                for ref, val in tabs:
                    for s in range(nch):
                        ref[0, s, pl.ds(j * LANE, LANE), :] = val[s * 8:(s + 1) * 8].reshape(LANE, N).astype(BF16)
            pr, pi = pr * ar - pi * ai, pr * ai + pi * ar

    g1n = pl.BlockSpec((1, G, 1, N), lambda d: (d, 0, 0, 0))
    gpn = pl.BlockSpec((1, G, P, N), lambda d: (d, 0, 0, 0))
    tab = pl.BlockSpec((1, nch, SUB * LANE, N), lambda d: (d, 0, 0, 0))
    pw = pl.BlockSpec((1, SUB + 1, G, 1, N), lambda d: (d, 0, 0, 0, 0))
    s_tab = jax.ShapeDtypeStruct((2, nch, SUB * LANE, N), BF16)
    s_pw = jax.ShapeDtypeStruct((2, SUB + 1, G, 1, N), F32)
    return pl.pallas_call(
        body, name="s5_tables", grid=(2,),
        in_specs=[g1n, g1n, pl.BlockSpec((1, G, 1, 1), lambda d: (d, 0, 0, 0)), gpn, gpn, gpn, gpn],
        out_specs=[tab] * 4 + [pw] * 2, out_shape=[s_tab] * 4 + [s_pw] * 2,
        compiler_params=_cparams(("parallel",)),
    )(a_re, a_im, ldt, b_re, b_im, c_re, c_im)


def _s5_expand(t_re, t_im, name):
    _, nch, R, N = t_re.shape
    sw = 8 * N

    def body(re_ref, im_ref, o_ref):
        spread = (lax.broadcasted_iota(jnp.int32, (N, sw), 1) % N == lax.broadcasted_iota(jnp.int32, (N, sw), 0)).astype(BF16)
        row_g = (lax.broadcasted_iota(jnp.int32, (R, sw), 0) % LANE) // S5_GROUP
        keep = row_g == lax.broadcasted_iota(jnp.int32, (R, sw), 1) // N
        for half, ref in enumerate((re_ref, im_ref)):
            t = jnp.dot(ref[0, 0], spread, preferred_element_type=F32)
            o_ref[0, 0, :, pl.ds(half * sw, sw)] = jnp.where(keep, t, 0.0).astype(BF16)

    spec = pl.BlockSpec((1, 1, R, N), lambda d, s: (d, s, 0, 0))
    return pl.pallas_call(
        body, name=name, grid=(2, nch), in_specs=[spec, spec],
        out_specs=pl.BlockSpec((1, 1, R, 2 * sw), lambda d, s: (d, s, 0, 0)),
        out_shape=jax.ShapeDtypeStruct((2, nch, R, 2 * sw), BF16), compiler_params=_cparams(("parallel", "parallel")),
    )(t_re, t_im)


def _s5_param_bwd(a_re, a_im, ldt, b_re, b_im, da_re, da_im, dbb_re, dbb_im):
    _, G, P, N = b_re.shape

    def body(lr_ref, li_ref, ldt_ref, br_ref, bi_ref, dar, dai, dbr, dbi, o_lr, o_li, o_ldt, o_br, o_bi):
        _, vjp = jax.vjp(_s5_disc_math, lr_ref[0], li_ref[0], ldt_ref[0], br_ref[0], bi_ref[0])
        o_lr[0], o_li[0], o_ldt[0], o_br[0], o_bi[0] = vjp((dar[0], dai[0], dbr[0], dbi[0]))

    g1n = pl.BlockSpec((1, G, 1, N), lambda d: (d, 0, 0, 0))
    g11 = pl.BlockSpec((1, G, 1, 1), lambda d: (d, 0, 0, 0))
    gpn = pl.BlockSpec((1, G, P, N), lambda d: (d, 0, 0, 0))
    s_g1n, s_g11, s_gpn = (jax.ShapeDtypeStruct(s, F32) for s in ((2, G, 1, N), (2, G, 1, 1), (2, G, P, N)))
    return pl.pallas_call(
        body, name="s5_param_bwd", grid=(2,),
        in_specs=[g1n, g1n, g11, gpn, gpn, g1n, g1n, gpn, gpn], out_specs=[g1n, g1n, g11, gpn, gpn],
        out_shape=[s_g1n, s_g1n, s_g11, s_gpn, s_gpn], compiler_params=_cparams(("parallel",)),
    )(a_re, a_im, ldt, b_re, b_im, da_re, da_im, dbb_re, dbb_im)


def _shift_stack(u, back):
    tb = u.shape[0]
    tau = lax.broadcasted_iota(jnp.int32, u.shape, 0) % SUB
    parts = [u]
    for j in range(1, SUB):
        if back:
            parts.append(jnp.where(tau >= j, pltpu.roll(u, j, 0), 0.0))
        else:
            parts.append(jnp.where(tau <= SUB - 1 - j, pltpu.roll(u, tb - j, 0), 0.0))
    return jnp.concatenate(parts, axis=1).astype(BF16)


def _cmul_add(tile, pw, carry, sw):
    pr, pi, cr, ci = pw[:, :sw], pw[:, sw:], carry[:, :sw], carry[:, sw:]
    return tile + jnp.concatenate([pr * cr - pi * ci, pr * ci + pi * cr], axis=1)


def _tile_scan(buf, base, ntile, pw, carry, sw, causal):
    def step(k, c):
        i = k if causal else ntile - 1 - k
        r = pl.multiple_of(base + i * SUB, SUB)
        tile = _cmul_add(buf[pl.ds(r, SUB), :], pw, c, sw)
        buf[pl.ds(r, SUB), :] = tile
        return tile[SUB - 1:SUB, :] if causal else tile[0:1, :]

    return lax.fori_loop(0, ntile, step, carry)


def _s5_fwd(h_all, waug, vaug, pw, S5W, T, d, causal, name):
    S = h_all.shape[0]
    _, nch, _, sw2 = waug.shape
    sw = sw2 // 2
    tb = _pick(math.gcd(T, S - T), (256, 128, 64, 32, 16))
    ntile, nt, off = tb // SUB, S // tb, T // tb
    rb = (lambda s, t: ((t + off) % nt, s)) if causal else (lambda s, t: (nt - 1 - t, s))

    def body(u_ref, w_ref, v_ref, p_ref, y_ref, h_ref, hblk, carry):
        @pl.when(pl.program_id(1) == 0)
        def _():
            carry[...] = jnp.zeros_like(carry)

        hblk[...] = jnp.dot(_shift_stack(u_ref[...], causal), w_ref[...], preferred_element_type=F32)
        carry[...] = _tile_scan(hblk, 0, ntile, p_ref[...], carry[...], sw, causal)
        hb = hblk[...].astype(BF16)
        h_ref[...] = hb
        y_ref[...] = lax.dot_general(hb, v_ref[...], _NT, preferred_element_type=F32)

    return pl.pallas_call(
        body, name=name, grid=(nch, nt),
        in_specs=[pl.BlockSpec((tb, LANE), rb),
                  pl.BlockSpec((None, None, SUB * LANE, sw2), lambda s, t: (d, s, 0, 0)),
                  pl.BlockSpec((None, None, LANE, sw2), lambda s, t: (d, s, 0, 0)),
                  pl.BlockSpec((None, SUB, sw2), lambda s, t: (s, 0, 0))],
        out_specs=[pl.BlockSpec((tb, LANE), rb), pl.BlockSpec((tb, sw2), rb)],
        out_shape=[jax.ShapeDtypeStruct((S, S5W), F32), jax.ShapeDtypeStruct((S, nch * sw2), BF16)],
        scratch_shapes=[pltpu.VMEM((tb, sw2), F32), pltpu.VMEM((1, sw2), F32)],
        compiler_params=_cparams(("parallel", "arbitrary")),
    )(h_all, waug, vaug, pw)


def _s5_bwd(dy_all, h_all, hs, waug, vaug, pwc, S5W, T, d, causal, name):
    S = h_all.shape[0]
    _, nch, _, sw2 = waug.shape
    sw = sw2 // 2
    tb = _pick(math.gcd(T, S - T), (256, 128, 64, 32, 16))
    ntile, nt, off = tb // SUB, S // tb, T // tb
    rb = (lambda s, t: ((nt - 1 - t + off) % nt, s)) if causal else (lambda s, t: (t, s))
    adj_causal = not causal
    edge = SUB - 1 if adj_causal else SUB + tb
    keep_src, keep_dst = (tb, 0) if adj_causal else (SUB, SUB + tb)

    def body(dy_ref, u_ref, h_ref, w_ref, v_ref, p_ref, du_ref, dbb_ref, dc_ref, da_ref, lam):
        @pl.when(pl.program_id(1) == 0)
        def _():
            lam[pl.ds(0, SUB), :] = jnp.zeros((SUB, sw2), F32)
            lam[pl.ds(SUB + tb, SUB), :] = jnp.zeros((SUB, sw2), F32)
            dbb_ref[...] = jnp.zeros_like(dbb_ref)
            dc_ref[...] = jnp.zeros_like(dc_ref)
            da_ref[...] = jnp.zeros_like(da_ref)

        dy = dy_ref[...]
        lam[pl.ds(SUB, tb), :] = jnp.dot(_shift_stack(dy, adj_causal), v_ref[...], preferred_element_type=F32)
        _tile_scan(lam, SUB, ntile, p_ref[...], lam[pl.ds(edge, 1), :], sw, adj_causal)
        lb = lam[pl.ds(SUB, tb), :].astype(BF16)
        du_ref[...] = lax.dot_general(lb, w_ref[...], _NT, preferred_element_type=F32)
        dbb_ref[...] += lax.dot_general(u_ref[...].astype(BF16), lb, _TN, preferred_element_type=F32)
        dc_ref[...] += lax.dot_general(h_ref[...], dy.astype(BF16), _TN, preferred_element_type=F32)
        h = h_ref[...].astype(F32)
        ln = lam[pl.ds(SUB + 1 if causal else SUB - 1, tb), :]
        hr, hi, lr, li = h[:, :sw], h[:, sw:], ln[:, :sw], ln[:, sw:]
        da_ref[...] += jnp.concatenate([jnp.sum(hr * lr + hi * li, axis=0, keepdims=True),
                                        jnp.sum(hr * li - hi * lr, axis=0, keepdims=True)], axis=1)
        lam[pl.ds(keep_dst, SUB), :] = lam[pl.ds(keep_src, SUB), :]

    fixed = lambda s, t: (s, 0, 0)
    return pl.pallas_call(
        body, name=name, grid=(nch, nt),
        in_specs=[pl.BlockSpec((tb, LANE), rb), pl.BlockSpec((tb, LANE), rb), pl.BlockSpec((tb, sw2), rb),
                  pl.BlockSpec((None, None, LANE, sw2), lambda s, t: (d, s, 0, 0)),
                  pl.BlockSpec((None, None, SUB * LANE, sw2), lambda s, t: (d, s, 0, 0)),
                  pl.BlockSpec((None, SUB, sw2), fixed)],
        out_specs=[pl.BlockSpec((tb, LANE), rb), pl.BlockSpec((None, LANE, sw2), fixed),
                   pl.BlockSpec((None, sw2, LANE), fixed), pl.BlockSpec((None, 1, sw2), fixed)],
        out_shape=[jax.ShapeDtypeStruct((S, S5W), F32), jax.ShapeDtypeStruct((nch, LANE, sw2), F32),
                   jax.ShapeDtypeStruct((nch, sw2, LANE), F32), jax.ShapeDtypeStruct((nch, 1, sw2), F32)],
        scratch_shapes=[pltpu.VMEM((tb + 2 * SUB, sw2), F32)],
        compiler_params=_cparams(("parallel", "arbitrary")),
    )(dy_all, h_all, hs, waug, vaug, pwc)


def _attn_fwd(q3, k3, v3, scale):
    H, T, dk = q3.shape
    S, dv = k3.shape[1], v3.shape[2]
    tq = _pick(T, (256, 128, 64, 32, 16))

    def body(q_ref, k_ref, v_ref, o_ref, lse_ref):
        s = lax.dot_general(q_ref[0], k_ref[0], _NT, preferred_element_type=F32) * scale
        m = jnp.max(s, axis=1, keepdims=True)
        p = jnp.exp(s - m)
        l = jnp.sum(p, axis=1, keepdims=True)
        o_ref[...] = jnp.dot((p / l).astype(BF16), v_ref[0], preferred_element_type=F32).astype(o_ref.dtype)
        lse_ref[0] = m + jnp.log(l)

    return pl.pallas_call(
        body, name="attn_fwd", grid=(H, T // tq),
        in_specs=[pl.BlockSpec((1, tq, dk), lambda h, i: (h, i, 0)), pl.BlockSpec((1, S, dk), lambda h, i: (h, 0, 0)),
                  pl.BlockSpec((1, S, dv), lambda h, i: (h, 0, 0))],
        out_specs=[pl.BlockSpec((tq, dv), lambda h, i: (i, h)), pl.BlockSpec((1, tq, 1), lambda h, i: (h, i, 0))],
        out_shape=[jax.ShapeDtypeStruct((T, H * dv), BF16), jax.ShapeDtypeStruct((H, T, 1), F32)],
        compiler_params=_cparams(("parallel", "parallel")),
    )(q3, k3, v3)


def _attn_bwd(q3, k3, v3, do, lse, scale):
    H, T, dk = q3.shape
    S, dv = k3.shape[1], v3.shape[2]
    tq = _pick(T, (256, 128, 64, 32, 16))

    def body(q_ref, k_ref, v_ref, do_ref, lse_ref, dq_ref, dk_ref, dv_ref):
        @pl.when(pl.program_id(1) == 0)
        def _():
            dk_ref[...] = jnp.zeros_like(dk_ref)
            dv_ref[...] = jnp.zeros_like(dv_ref)

        q, k, v, d_o = q_ref[0], k_ref[0], v_ref[0], do_ref[...]
        s = lax.dot_general(q, k, _NT, preferred_element_type=F32) * scale
        p = jnp.exp(s - lse_ref[0])
        dv_ref[0] += lax.dot_general(p.astype(BF16), d_o, _TN, preferred_element_type=F32)
        dp = lax.dot_general(d_o, v, _NT, preferred_element_type=F32)
        ds = (p * (dp - jnp.sum(p * dp, axis=1, keepdims=True)) * scale).astype(BF16)
        dq_ref[0] = jnp.dot(ds, k, preferred_element_type=F32)
        dk_ref[0] += lax.dot_general(ds, q, _TN, preferred_element_type=F32)

    return pl.pallas_call(
        body, name="attn_bwd", grid=(H, T // tq),
        in_specs=[pl.BlockSpec((1, tq, dk), lambda h, i: (h, i, 0)), pl.BlockSpec((1, S, dk), lambda h, i: (h, 0, 0)),
                  pl.BlockSpec((1, S, dv), lambda h, i: (h, 0, 0)), pl.BlockSpec((tq, dv), lambda h, i: (i, h)),
                  pl.BlockSpec((1, tq, 1), lambda h, i: (h, i, 0))],
        out_specs=[pl.BlockSpec((1, tq, dk), lambda h, i: (h, i, 0)), pl.BlockSpec((1, S, dk), lambda h, i: (h, 0, 0)),
                   pl.BlockSpec((1, S, dv), lambda h, i: (h, 0, 0))],
        out_shape=[jax.ShapeDtypeStruct((H, T, dk), F32), jax.ShapeDtypeStruct((H, S, dk), F32),
                   jax.ShapeDtypeStruct((H, S, dv), F32)],
        compiler_params=_cparams(("parallel", "arbitrary")),
    )(q3, k3, v3, do, lse)


def _rope_tables(T, heads):
    rows = T // GRID_W
    row = jnp.repeat(jnp.arange(rows, dtype=F32), GRID_W)
    col = jnp.tile(jnp.arange(GRID_W, dtype=F32), rows)
    n_freq = QK_ROPE // 4
    inv = ROPE_BASE ** (-jnp.arange(n_freq, dtype=F32) / n_freq)
    ar, ac = row[:, None] * inv, col[:, None] * inv
    cos = jnp.concatenate([jnp.cos(ar), jnp.cos(ar), jnp.cos(ac), jnp.cos(ac)], axis=1)
    sin = jnp.concatenate([-jnp.sin(ar), jnp.sin(ar), -jnp.sin(ac), jnp.sin(ac)], axis=1)
    return jnp.tile(cos, (1, heads)), jnp.tile(sin, (1, heads))


def _dw(a, dy, w, name):
    return _mm(a, dy, "tn", BF16, name, out_slots=w.shape[0] if w.ndim == 3 else None)


def _local_step(x, ctx, tgt, m_lat, m_ctx, p, W, goff, ffn_weights=None, send_grads=None):
    T, D = x.shape
    Tc = ctx.shape[0]
    S = T + Tc
    S5W = p["s5_d"].shape[1]
    QR, KVR = p["q_norm"].shape[1], p["kv_norm"].shape[1]
    H = W["w_uq"].shape[1] // (QK_NOPE + QK_ROPE)
    G, N = p["s5_a_re"].shape[1:]
    P = S5_GROUP
    nch = G // 8
    o_cq, o_ckv, o_kr = S5W, S5W + QR, S5W + QR + KVR
    assert o_cq % QR == 0 and o_ckv % KVR == 0 and o_kr % LANE == 0 and goff % D == 0 and S5W % LANE == 0 and G % 8 == 0
    assert 8 * P == LANE
    row = lambda k, m: m[k:k + 1]
    sh1, sc1, g1, sh2, sc2, g2 = (row(k, m_lat) for k in range(6))
    csh1, csc1 = row(0, m_ctx), row(1, m_ctx)
    n1, n2, nf = p["norm1"], p["norm2"], p["norm_f"]

    (xm_lat,) = _rowmap(_normmod, "norm1_lat", T, [x], [n1, sc1, sh1], [(D, BF16)])
    (xm_ctx,) = _rowmap(_normmod, "norm1_ctx", Tc, [ctx], [n1, csc1, csh1], [(D, BF16)])
    xm_all = jnp.concatenate([xm_lat, xm_ctx], axis=0)
    h_all = _mm(xm_all, W["w_in"], "nn", F32, "mm_in")

    a_re, a_im = p["s5_a_re"][:, :, None, :], p["s5_a_im"][:, :, None, :]
    ldt = p["s5_log_dt"][:, :, None, None]
    b_re, b_im = p["s5_b_re"].transpose(0, 1, 3, 2), p["s5_b_im"].transpose(0, 1, 3, 2)
    wre, wim, vre, vim, pwr, pwi = _s5_tables(a_re, a_im, ldt, b_re, b_im, p["s5_c_re"], p["s5_c_im"])
    waug = _s5_expand(wre, wim, "s5_expand_b")
    vaug = _s5_expand(vre, vim, "s5_expand_c")
    lanes = lambda t: t.reshape(2, SUB + 1, nch, 8 * N).transpose(0, 2, 1, 3)
    pw_re, pw_im = lanes(pwr), lanes(pwi)
    near = lambda t: t[:, :, 1:]
    far = lambda t: t[:, :, :0:-1]
    pw_c = jnp.concatenate([near(pw_re), near(pw_im)], axis=-1)
    pw_a = jnp.concatenate([far(pw_re), far(pw_im)], axis=-1)
    pwc_c = jnp.concatenate([near(pw_re), -near(pw_im)], axis=-1)
    pwc_a = jnp.concatenate([far(pw_re), -far(pw_im)], axis=-1)
    y0, hs0 = _s5_fwd(h_all, waug, vaug, pw_c[0], S5W, T, 0, True, "s5_scan_fwd0")
    y1, hs1 = _s5_fwd(h_all, waug, vaug, pw_a[1], S5W, T, 1, False, "s5_scan_fwd1")

    def s5_combine(u, yf, yr, dskip):
        y5 = dskip * u + yf + yr
        return y5, jax.nn.gelu(y5)

    y5, z = _rowmap(s5_combine, "s5_combine", T, [(h_all, S5W, 0), y0, y1], [p["s5_d"]], [(S5W, F32), (S5W, BF16)])

    (qn,) = _rowmap(_rms, "q_norm", T, [(h_all, QR, o_cq // QR)], [p["q_norm"]], [(QR, BF16)])
    (kvn,) = _rowmap(_rms, "kv_norm", S, [(h_all, KVR, o_ckv // KVR)], [p["kv_norm"]], [(KVR, BF16)])
    qraw = _mm(qn, W["w_uq"], "nn", F32, "mm_uq")
    kvraw = _mm(kvn, W["w_ukv"], "nn", BF16, "mm_ukv")
    cos_q, sin_q = _rope_tables(T, H)
    padl = lambda t: jnp.pad(t[:, :QK_ROPE], ((0, Tc), (0, LANE - QK_ROPE)))
    cos_k = padl(cos_q) + jnp.pad(jnp.ones((Tc, LANE), F32), ((T, 0), (0, 0)))
    sin_k = padl(sin_q)
    hn = H * QK_NOPE

    def q_post(q, cos, sin):
        return q[:, :hn], _rope(q[:, hn:], cos, sin)

    q_nope, q_rope = _rowmap(q_post, "q_rope", T, [qraw, cos_q, sin_q], [], [(hn, BF16), (H * QK_ROPE, BF16)])
    (kr,) = _rowmap(_rope, "k_rope", S, [(h_all, LANE, o_kr // LANE), cos_k, sin_k], [], [(LANE, BF16)])
    q3 = jnp.concatenate([q_nope.reshape(T, H, QK_NOPE), q_rope.reshape(T, H, QK_ROPE)], axis=-1).transpose(1, 0, 2)
    k3 = jnp.concatenate([kvraw[:, :hn].reshape(S, H, QK_NOPE),
                          jnp.broadcast_to(kr[:, None, :QK_ROPE], (S, H, QK_ROPE))], axis=-1).transpose(1, 0, 2)
    v3 = kvraw[:, hn:].reshape(S, H, V_DIM).transpose(1, 0, 2)
    scale = (QK_NOPE + QK_ROPE) ** -0.5
    o, lse = _attn_fwd(q3, k3, v3, scale)

    zz = _mm(z, W["w_glu"], "nn", BF16, "mm_glu")
    br_mla = _mm(o, W["w_mla_o"], "nn", BF16, "mm_mla_o")

    def merge(zz, brm, gs, gm):
        a, b = zz[:, :D], zz[:, D:]
        return jax.nn.sigmoid(gs) * (a * jax.nn.sigmoid(b)) + jax.nn.sigmoid(gm) * brm

    gb = goff // D
    merge_ins = [zz, br_mla, (h_all, D, gb), (h_all, D, gb + 1)]
    (mix,) = _rowmap(merge, "merge", T, merge_ins, [], [(D, BF16)])
    out1 = _mm(mix, W["w_out"], "nn", F32, "mm_out")

    def resid_norm2(x, out1, g1, n2, sc2, sh2):
        x1 = x + g1 * out1
        return x1, _normmod(x1, n2, sc2, sh2)

    x1, hm = _rowmap(resid_norm2, "resid_norm2", T, [x, out1], [g1, n2, sc2, sh2], [(D, F32), (D, BF16)])

    if ffn_weights is not None:
        W = {**W, **ffn_weights(hm)}
    FF = W["w_ffn_out"].shape[0]
    assert FF % LANE == 0
    ab = _mm(hm, W["w_ffn_in"], "nn", BF16, "mm_ffn_in")

    def swiglu_act(a, b):
        return jax.nn.silu(a) * b

    (f,) = _rowmap(swiglu_act, "ffn_act", T, [(ab, FF, 0), (ab, FF, 1)], [], [(FF, BF16)])
    out2 = _mm(f, W["w_ffn_out"], "nn", F32, "mm_ffn_out")

    def loss_rows(x1, out2, g2, nf, tgt):
        y = _rms(x1 + g2 * out2, nf)
        return 0.5 * jnp.sum(jnp.mean(jnp.square(y - tgt), axis=-1))

    def final(x1, out2, tgt, g2, nf):
        val, (dx1, dout2, dg2, dnf) = jax.value_and_grad(loss_rows, argnums=(0, 1, 2, 3))(x1, out2, g2, nf, tgt)
        return dx1, dout2, jnp.full((1, LANE), val, F32), dg2, dnf

    dx2, dout2, loss_acc, dg2, dnf = _rowmap(final, "final_loss", T, [x1, out2, tgt], [g2, nf],
                                             [(D, F32), (D, BF16)], [LANE, D, D])

    gW = {}
    df = _mm(dout2, W["w_ffn_out"], "nt", BF16, "mm_ffn_out_dx")
    gW["w_ffn_out"] = _dw(f, dout2, W["w_ffn_out"], "mm_ffn_out_dw")

    def swiglu_bwd(a, b, df):
        _, vjp = jax.vjp(swiglu_act, a, b)
        da, db = vjp(df)
        return jnp.concatenate([da, db], axis=1)

    (dab,) = _rowmap(swiglu_bwd, "ffn_act_bwd", T, [(ab, FF, 0), (ab, FF, 1), df], [], [(2 * FF, BF16)])
    dhm = _mm(dab, W["w_ffn_in"], "nt", F32, "mm_ffn_in_dx")
    gW["w_ffn_in"] = _dw(hm, dab, W["w_ffn_in"], "mm_ffn_in_dw")
    if send_grads is not None:
        token = send_grads(FFN, [gW.pop(n) for n in FFN])
        g1 = g1 if token is None else g1 + token[:1, :1]

    def resid_norm2_bwd(x, out1, dx2, dhm, g1, n2, sc2, sh2):
        _, vjp = jax.vjp(resid_norm2, x, out1, g1, n2, sc2, sh2)
        dx, dout1, dg1, dn2, dsc2, dsh2 = vjp((dx2, dhm))
        return dx, dout1, dg1, dn2, dsc2, dsh2

    dx1, dout1, dg1, dn2, dsc2, dsh2 = _rowmap(resid_norm2_bwd, "resid_norm2_bwd", T, [x, out1, dx2, dhm],
                                               [g1, n2, sc2, sh2], [(D, F32), (D, BF16)], [D, D, D, D])

    dmix = _mm(dout1, W["w_out"], "nt", BF16, "mm_out_dx")
    gW["w_out"] = _dw(mix, dout1, W["w_out"], "mm_out_dw")

    def merge_bwd(zz, brm, gs, gm, dmix):
        _, vjp = jax.vjp(merge, zz, brm, gs, gm)
        dzz, dbrm, dgs, dgm = vjp(dmix)
        return dzz, dbrm, jnp.concatenate([dgs, dgm], axis=1)

    dzz, dbrm, dgates = _rowmap(merge_bwd, "merge_bwd", T, merge_ins + [dmix], [],
                                [(2 * D, BF16), (D, BF16), (2 * D, BF16)])
    do = _mm(dbrm, W["w_mla_o"], "nt", BF16, "mm_mla_o_dx")
    gW["w_mla_o"] = _dw(o, dbrm, W["w_mla_o"], "mm_mla_o_dw")
    dz = _mm(dzz, W["w_glu"], "nt", BF16, "mm_glu_dx")
    gW["w_glu"] = _dw(z, dzz, W["w_glu"], "mm_glu_dw")
    d_skip_w = p["s5_d"]
    if send_grads is not None:
        token = send_grads(MIX, [gW.pop(n) for n in MIX])
        d_skip_w = d_skip_w if token is None else d_skip_w + token[:1, :1]

    def s5_combine_bwd(u, y5, dz, dskip):
        _, vjp = jax.vjp(lambda y: jax.nn.gelu(y), y5)
        (dy5,) = vjp(dz)
        return dy5, jnp.sum(dy5 * u, axis=0, keepdims=True)

    dy5, d_skip = _rowmap(s5_combine_bwd, "s5_combine_bwd", T, [(h_all, S5W, 0), y5, dz], [d_skip_w], [(S5W, F32)], [S5W])

    dq3, dk3, dv3 = _attn_bwd(q3, k3, v3, do, lse, scale)
    dq_t = dq3.transpose(1, 0, 2)
    dq_cat = jnp.concatenate([dq_t[:, :, :QK_NOPE].reshape(T, hn), dq_t[:, :, QK_NOPE:].reshape(T, H * QK_ROPE)], axis=1)

    def q_post_bwd(dq, cos, sin):
        return jnp.concatenate([dq[:, :hn], _rope_bwd(dq[:, hn:], cos, sin)], axis=1)

    (dqraw,) = _rowmap(q_post_bwd, "q_rope_bwd", T, [dq_cat, cos_q, sin_q], [], [(H * (QK_NOPE + QK_ROPE), BF16)])
    dk_t = dk3.transpose(1, 0, 2)
    dkvraw = jnp.concatenate([dk_t[:, :, :QK_NOPE].reshape(S, hn), dv3.transpose(1, 0, 2).reshape(S, H * V_DIM)], axis=1)
    dkr_heads = jnp.pad(dk_t[:, :, QK_NOPE:], ((0, 0), (0, 0), (0, LANE - QK_ROPE))).reshape(S, H * LANE)

    def k_rope_bwd(dkh, cos, sin):
        d = dkh[:, :LANE]
        for h in range(1, H):
            d = d + dkh[:, h * LANE:(h + 1) * LANE]
        return _rope_bwd(d, cos, sin)

    (dkr,) = _rowmap(k_rope_bwd, "k_rope_bwd", S, [dkr_heads, cos_k, sin_k], [], [(LANE, BF16)])
    dqn = _mm(dqraw, W["w_uq"], "nt", F32, "mm_uq_dx")
    gW["w_uq"] = _dw(qn, dqraw, W["w_uq"], "mm_uq_dw")
    dkvn = _mm(dkvraw, W["w_ukv"], "nt", F32, "mm_ukv_dx")
    gW["w_ukv"] = _dw(kvn, dkvraw, W["w_ukv"], "mm_ukv_dw")

    def rms_bwd(cx, dn, g):
        _, vjp = jax.vjp(_rms, cx, g)
        return vjp(dn)

    dcq, dq_norm = _rowmap(rms_bwd, "q_norm_bwd", T, [(h_all, QR, o_cq // QR), dqn], [p["q_norm"]], [(QR, BF16)], [QR])
    dckv, dkv_norm = _rowmap(rms_bwd, "kv_norm_bwd", S, [(h_all, KVR, o_ckv // KVR), dkvn], [p["kv_norm"]],
                             [(KVR, BF16)], [KVR])

    dy_all = jnp.concatenate([dy5, jnp.zeros((Tc, S5W), F32)], axis=0)
    du0, dbb0, dc0, da0 = _s5_bwd(dy_all, h_all, hs0, waug, vaug, pwc_a[0], S5W, T, 0, True, "s5_scan_bwd0")
    du1, dbb1, dc1, da1 = _s5_bwd(dy_all, h_all, hs1, waug, vaug, pwc_c[1], S5W, T, 1, False, "s5_scan_bwd1")

    def du_combine(a, b, dy, dskip):
        return a + b + dskip * dy

    (du_all,) = _rowmap(du_combine, "s5_du", S, [du0, du1, dy_all], [p["s5_d"]], [(S5W, BF16)])
    dbb = jnp.einsum("dsgpcgn->dcsgpn", jnp.stack([dbb0, dbb1]).reshape(2, nch, 8, P, 2, 8, N)).reshape(2, 2, G, P, N)
    dcm = jnp.einsum("dscgngp->dcsgpn", jnp.stack([dc0, dc1]).reshape(2, nch, 2, 8, N, 8, P)).reshape(2, 2, G, P, N)
    da = jnp.stack([da0, da1]).reshape(2, nch, 2, 8, N).transpose(0, 2, 1, 3, 4).reshape(2, 2, G, 1, N)
    d_lr, d_li, d_ldt, d_br, d_bi = _s5_param_bwd(a_re, a_im, ldt, b_re, b_im, da[:, 0], da[:, 1], dbb[:, 0], dbb[:, 1])

    lat_only = lambda t: jnp.pad(t, ((0, Tc), (0, 0)))
    dh_all = jnp.concatenate([du_all, lat_only(dcq), dckv, dkr, jnp.zeros((S, goff - o_kr - LANE), BF16), lat_only(dgates)],
                             axis=1)
    dxm = _mm(dh_all, W["w_in"], "nt", F32, "mm_in_dx")
    gW["w_in"] = _dw(xm_all, dh_all, W["w_in"], "mm_in_dw")

    def norm1_bwd(x, dxm, dx1, n1, sc, sh):
        _, vjp = jax.vjp(_normmod, x, n1, sc, sh)
        dx, dn, dsc, dsh = vjp(dxm)
        return dx + dx1, dn, dsc, dsh

    grad_x, dn1_l, dsc1, dsh1 = _rowmap(norm1_bwd, "norm1_lat_bwd", T, [x, dxm, dx1], [n1, sc1, sh1], [(D, F32)], [D, D, D])

    def norm1_ctx_bwd(x, dxm, n1, sc, sh):
        _, vjp = jax.vjp(_normmod, x, n1, sc, sh)
        return vjp(dxm)[1:]

    dn1_c, dcsc1, dcsh1 = _rowmap(norm1_ctx_bwd, "norm1_ctx_bwd", Tc, [ctx, dxm[T:]], [n1, csc1, csh1], [], [D, D, D])

    zero = jnp.zeros((1, D), F32)
    dm_lat = jnp.concatenate([dsh1, dsc1, dg1, dsh2, dsc2, dg2], axis=0)
    dm_ctx = jnp.concatenate([dcsh1, dcsc1, zero, zero, zero, zero], axis=0)
    small = {
        "norm1": dn1_l + dn1_c, "norm2": dn2, "norm_f": dnf, "q_norm": dq_norm, "kv_norm": dkv_norm, "s5_d": d_skip,
        "s5_a_re": d_lr, "s5_a_im": d_li, "s5_log_dt": d_ldt, "s5_b_re": d_br.transpose(0, 1, 3, 2),
        "s5_b_im": d_bi.transpose(0, 1, 3, 2), "s5_c_re": dcm[:, 0], "s5_c_im": -dcm[:, 1],
    }
    return loss_acc[:, :1], grad_x, small, dm_lat, dm_ctx, gW


BIG = ("w_in", "w_uq", "w_ukv", "w_glu", "w_mla_o", "w_out", "w_ffn_in", "w_ffn_out")
FFN = ("w_ffn_in", "w_ffn_out")
MIX = ("w_out", "w_mla_o", "w_glu")
ROW_SHARDED = ("w_out", "w_ffn_out")
RELAID = ("w_in", "w_uq", "w_ukv")
SMALL = ("c_ctx", "b_mod", "norm1", "norm2", "s5_a_re", "s5_a_im", "s5_log_dt", "s5_b_re", "s5_b_im", "s5_c_re",
         "s5_c_im", "s5_d", "q_norm", "kv_norm", "norm_f")
WEIGHTS = ("c_ctx", "w_mod", "b_mod", "norm1", "norm2", "w_in", "s5_a_re", "s5_a_im", "s5_log_dt", "s5_b_re", "s5_b_im",
           "s5_c_re", "s5_c_im", "s5_d", "w_glu", "q_norm", "kv_norm", "w_uq", "w_ukv", "w_mla_o", "w_out", "w_ffn_in",
           "w_ffn_out", "norm_f")


def _heads_split(w, heads, first):
    k = w.shape[0]
    w3 = w.reshape(k, heads, -1)
    return jnp.concatenate([w3[:, :, :first].reshape(k, -1), w3[:, :, first:].reshape(k, -1)], axis=1)


def _heads_merge(w, heads, first):
    k = w.shape[0]
    a, b = w[:, :heads * first].reshape(k, heads, first), w[:, heads * first:].reshape(k, heads, -1)
    return jnp.concatenate([a, b], axis=2).reshape(k, -1)


def _cols_full(w8):
    return w8.transpose(1, 0, 2).reshape(w8.shape[1], -1)


def _cols_slots(w):
    return w.reshape(w.shape[0], N_DEV, -1).transpose(1, 0, 2)


def _weight_layout(n, w8):
    if n in ROW_SHARDED:
        return w8.reshape(-1, w8.shape[-1])
    return _cols_full(w8) if (n in RELAID or w8.shape[-1] % LANE) else w8


def _grad_slots(n, g):
    if g.ndim == 3:
        return g
    return g.reshape(N_DEV, g.shape[0] // N_DEV, g.shape[1]) if n in ROW_SHARDED else _cols_slots(g)


def _model_weights(g8, D):
    W = {n: _weight_layout(n, w8) for n, w8 in g8.items()}
    w_in = W["w_in"]
    n_front = w_in.shape[1] - 2 * D
    goff = -(-n_front // D) * D
    W["w_in"] = jnp.concatenate([w_in[:, :n_front], jnp.zeros((D, goff - n_front), w_in.dtype), w_in[:, n_front:]], axis=1)
    heads = W["w_uq"].shape[1] // (QK_NOPE + QK_ROPE)
    W["w_uq"] = _heads_split(W["w_uq"], heads, QK_NOPE)
    W["w_ukv"] = _heads_split(W["w_ukv"], heads, QK_NOPE)
    return W, goff


def kernel(x, c, ctx, c_ctx, w_mod, b_mod, norm1, norm2, w_in, s5_a_re, s5_a_im, s5_log_dt, s5_b_re, s5_b_im, s5_c_re, s5_c_im, s5_d, w_glu, q_norm, kv_norm, w_uq, w_ukv, w_mla_o, w_out, w_ffn_in, w_ffn_out, norm_f, loss_target, m_c_ctx, m_w_mod, m_b_mod, m_norm1, m_norm2, m_w_in, m_s5_a_re, m_s5_a_im, m_s5_log_dt, m_s5_b_re, m_s5_b_im, m_s5_c_re, m_s5_c_im, m_s5_d, m_w_glu, m_q_norm, m_kv_norm, m_w_uq, m_w_ukv, m_w_mla_o, m_w_out, m_w_ffn_in, m_w_ffn_out, m_norm_f, v_c_ctx, v_w_mod, v_b_mod, v_norm1, v_norm2, v_w_in, v_s5_a_re, v_s5_a_im, v_s5_log_dt, v_s5_b_re, v_s5_b_im, v_s5_c_re, v_s5_c_im, v_s5_d, v_w_glu, v_q_norm, v_kv_norm, v_w_uq, v_w_ukv, v_w_mla_o, v_w_out, v_w_ffn_in, v_w_ffn_out, v_norm_f):
    a = dict(locals())
    D = x.shape[-1]
    me = 4 * lax.axis_index("x") + 2 * lax.axis_index("y") + lax.axis_index("c")

    shard = {n: a[n][0] for n in BIG}
    first = [n for n in BIG if n not in FFN]
    gathered = _all_gather([shard[n].astype(BF16) for n in first] + [jnp.broadcast_to(c, (8, D))], "ag_weights")
    W, goff = _model_weights(dict(zip(first, gathered[:-1])), D)
    cg = gathered[-1]

    wm = w_mod[0]
    ncol = wm.shape[1]
    c16 = jnp.concatenate([cg[:, 0, :], c_ctx[None], jnp.zeros((7, D), F32)], axis=0)
    (s16,) = _rowmap(jax.nn.silu, "mod_silu", 16, [c16], [], [(D, BF16)])
    m_cols = _mm(s16, wm, "nn", F32, "mm_mod")
    (mg,) = _all_gather([m_cols], "ag_mod")
    (m16,) = _rowmap(lambda m, b: m + b, "mod_bias", 16, [_cols_full(mg)], [b_mod], [(N_DEV * ncol, F32)])

    ffn_blocks = [shard[n].astype(BF16) for n in FFN]
    ag_send, ag_recv, ag_thru, ag_token = _xchg_start(ffn_blocks, [_own_slot(b, me) for b in ffn_blocks], False, "ag_ffn_start")
    m16 = m16 + ag_token[:1, :1]

    def ffn_weights(after):
        lands = _xchg_wait(ag_send, ag_recv, ag_thru, after, False, "ag_ffn_wait")
        return {n: _weight_layout(n, w8) for n, w8 in zip(FFN, lands)}

    rs_async = {}

    def send_grads(names, gs):
        slots = [_grad_slots(n, g) for n, g in zip(names, gs)]
        lands = [_own_slot(lax.dynamic_index_in_dim(s, me, 0, keepdims=False), me) for s in slots]
        rs_async[names] = _xchg_start(slots, lands, True, "rs_start_" + names[0])
        return rs_async[names][3]

    m_lat = lax.dynamic_slice(m16, (me, 0), (1, 6 * D)).reshape(6, D)
    m_ctx = m16[8].reshape(6, D)

    p = {n: a[n][0] for n in ("norm1", "norm2", "s5_a_re", "s5_a_im", "s5_log_dt", "s5_b_re", "s5_b_im", "s5_c_re",
                              "s5_c_im", "q_norm", "kv_norm")}
    p = {k: (v[None] if v.ndim == 1 else v) for k, v in p.items()}
    p["s5_d"] = s5_d.reshape(1, -1)
    p["norm_f"] = norm_f[None]
    loss_part, grad_x, small, dm_lat, dm_ctx, gW = _local_step(x[0], ctx[0], loss_target[0], m_lat, m_ctx, p, W, goff,
                                                               ffn_weights, send_grads)
    loss = lax.psum(loss_part[0, 0], ("x", "y", "c"))

    dm16 = jnp.concatenate([dm_lat.reshape(1, -1), dm_ctx.reshape(1, -1), jnp.zeros((14, 6 * D), F32)], axis=0)
    (dmg,) = _all_gather([dm16], "ag_dmod")
    dm_sum = _sum_slots(dmg, "sum_dmod")
    dM16 = jnp.concatenate([dmg[:, 0, :], dm_sum[1:2], jnp.zeros((7, 6 * D), F32)], axis=0)
    (g_b_mod,) = _rowmap(lambda d: jnp.sum(d, axis=0, keepdims=True), "b_mod_grad", 16, [dM16], [], [], [6 * D])
    dM_loc = lax.dynamic_slice(dM16, (0, me * ncol), (16, ncol))
    g_w_mod = _mm(s16, dM_loc, "tn", F32, "mm_mod_dw")
    ds16_part = _mm(dM_loc, wm, "nt", F32, "mm_mod_dx")

    small_names = [n for n in SMALL if n not in ("c_ctx", "b_mod")]
    small_shapes = [small[n].shape for n in small_names] + [(1, D)]
    (sg,) = _all_gather([_pack_rows([small[n] for n in small_names] + [ds16_part[8:9]], F32)], "ag_small")
    parts = _unpack_rows(_sum_slots(sg, "sum_small"), small_shapes)
    grads = dict(zip(small_names, parts[:-1]))

    def silu_bwd(cc, ds):
        _, vjp = jax.vjp(jax.nn.silu, cc)
        return vjp(ds)[0]

    (g_c_ctx,) = _rowmap(silu_bwd, "c_ctx_grad", 1, [c_ctx[None], parts[-1]], [], [(D, F32)])
    grads["c_ctx"], grads["b_mod"] = g_c_ctx, g_b_mod

    gW = dict(gW)
    n_front = w_in.shape[-1] * N_DEV - 2 * D
    gW["w_in"] = jnp.concatenate([gW["w_in"][:, :n_front], gW["w_in"][:, goff:]], axis=1)
    heads = gW["w_uq"].shape[1] // (QK_NOPE + QK_ROPE)
    gW["w_uq"] = _heads_merge(gW["w_uq"], heads, QK_NOPE)
    gW["w_ukv"] = _heads_merge(gW["w_ukv"], heads, QK_NOPE)
    last = [n for n in BIG if n in gW]
    slots = [_grad_slots(n, gW[n]) for n in last]
    from_sibling = _rs_pair(slots, "rs_pair")
    chip_sums = [_add_pair(pp, rr, "rs_add_" + n) for n, pp, rr in zip(last, slots, from_sibling)]
    my_chip = 2 * lax.axis_index("x") + lax.axis_index("y")
    lands = [_own_slot(lax.dynamic_index_in_dim(q, my_chip, 0, keepdims=False), my_chip, N_CHIP) for q in chip_sums]
    rs_send, rs_recv, rs_thru, behind = _xchg_start(chip_sums, lands, True, "rs_chips_start")
    for names, (send, recv, thru, _) in rs_async.items():
        for n, g8 in zip(names, _xchg_wait(send, recv, thru, behind, True, "rs_wait_" + names[0])):
            grads[n] = _sum_slots(g8, "rs_sum_" + n)
    grads["w_mod"] = g_w_mod

    out = {}

    def adamw_big(n, after):
        d, nm, nv = _adamw(a[n][0], grads[n], a["m_" + n][0], a["v_" + n][0], "adamw_" + n, after)
        for k, val in (("grad_", grads[n]), ("delta_", d), ("new_m_", nm), ("new_v_", nv)):
            out[k + n] = val.reshape(a[n].shape)
        return nv

    for n in FFN + MIX + ("w_mod",):
        behind = adamw_big(n, behind)
    packs = [_pack_rows([t[n] for n in SMALL], F32) for t in (
        {n: a[n] for n in SMALL}, {n: grads[n] for n in SMALL}, {n: a["m_" + n] for n in SMALL}, {n: a["v_" + n] for n in SMALL})]
    res = _adamw(*packs, "adamw_small", behind)
    for n, g4 in zip(last, _xchg_wait(rs_send, rs_recv, rs_thru, res[2], True, "rs_chips_wait")):
        grads[n] = _sum_slots(g4, "rs_sum_" + n)
        adamw_big(n, None)
    shapes = [a[n].shape for n in SMALL]
    for k, packed in (("grad_", packs[1]), ("delta_", res[0]), ("new_m_", res[1]), ("new_v_", res[2])):
        for n, val in zip(SMALL, _unpack_rows(packed, shapes)):
            out[k + n] = val
    return (loss, grad_x[None]) + tuple(out[k + n] for k in ("grad_", "delta_", "new_m_", "new_v_") for n in WEIGHTS)
```

```python
import functools
import math

import jax
import jax.numpy as jnp
from jax import lax
from jax.experimental import pallas as pl
from jax.experimental.pallas import tpu as pltpu

F32 = jnp.float32
BF16 = jnp.bfloat16

N_DEV = 8
N_CHIP = 4
EPS = 1e-6
GRID_W = 64
S5_GROUP = 16
QK_NOPE, QK_ROPE, V_DIM = 128, 64, 128
ROPE_BASE = 10000.0
ADAM_LR, ADAM_B1, ADAM_B2, ADAM_EPS, ADAM_WD, ADAM_STEP = 0.001, 0.9, 0.999, 1e-08, 0.01, 10

LANE = 128
SUB = 8
PACK_W = 1024
PACK_ROWS = 32
VMEM_LIMIT = 48 << 20
ROWMAP_TILE_BYTES = 20 << 20
MM_VMEM_BUDGET = 36 << 20
MESH = pl.DeviceIdType.MESH
_NT = (((1,), (1,)), ((), ()))
_TN = (((0,), (0,)), ((), ()))


def _pick(dim, cands):
    for c in cands:
        if dim % c == 0:
            return c
    return dim


def _cparams(sem):
    return pltpu.CompilerParams(dimension_semantics=sem, vmem_limit_bytes=VMEM_LIMIT)


def _mm(a, b, dims, out_dtype, name, out_slots=None):
    a = a.astype(BF16)
    b = b.astype(BF16)
    b3 = b.ndim == 3
    if dims == "nn":
        (M, K), N = a.shape, (b.shape[0] * b.shape[2] if b3 else b.shape[1])
    elif dims == "nt":
        M, N = a.shape[0], b.shape[-2]
        K = b.shape[0] * b.shape[2] if b3 else b.shape[1]
    else:
        (K, M), N = a.shape, b.shape[1]
    unit_n = b.shape[2] if (b3 and dims == "nn") else (N // out_slots if out_slots else N)
    unit_k = b.shape[2] if (b3 and dims == "nt") else K
    osz = jnp.dtype(out_dtype).itemsize
    tm, tn, tk = _mm_tiles(M, unit_n, unit_k, osz, LANE if dims == "tn" else 16)
    nk, npt, kpt = K // tk, unit_n // tn, unit_k // tk
    use_acc = nk > 1 and out_dtype != F32
    if dims == "nn":
        a_spec = pl.BlockSpec((tm, tk), lambda i, j, k: (i, k))
        b_spec = (pl.BlockSpec((None, tk, tn), lambda i, j, k: (j // npt, k, j % npt)) if b3
                  else pl.BlockSpec((tk, tn), lambda i, j, k: (k, j)))
        dn = (((1,), (0,)), ((), ()))
    elif dims == "nt":
        a_spec = pl.BlockSpec((tm, tk), lambda i, j, k: (i, k))
        b_spec = (pl.BlockSpec((None, tn, tk), lambda i, j, k: (k // kpt, j, k % kpt)) if b3
                  else pl.BlockSpec((tn, tk), lambda i, j, k: (j, k)))
        dn = _NT
    else:
        a_spec = pl.BlockSpec((tk, tm), lambda i, j, k: (k, i))
        b_spec = pl.BlockSpec((tk, tn), lambda i, j, k: (k, j))
        dn = _TN
    if out_slots:
        out_spec = pl.BlockSpec((None, tm, tn), lambda i, j, k: (j // npt, i, j % npt))
        out_shape = jax.ShapeDtypeStruct((out_slots, M, unit_n), out_dtype)
    else:
        out_spec = pl.BlockSpec((tm, tn), lambda i, j, k: (i, j))
        out_shape = jax.ShapeDtypeStruct((M, N), out_dtype)

    def body(a_ref, b_ref, o_ref, *scratch):
        part = lax.dot_general(a_ref[...], b_ref[...], dn, preferred_element_type=F32)
        if nk == 1:
            o_ref[...] = part.astype(o_ref.dtype)
            return
        acc_ref = scratch[0] if use_acc else o_ref
        k = pl.program_id(2)

        @pl.when(k == 0)
        def _():
            acc_ref[...] = part

        @pl.when(k > 0)
        def _():
            acc_ref[...] += part

        if use_acc:
            @pl.when(k == nk - 1)
            def _():
                o_ref[...] = acc_ref[...].astype(o_ref.dtype)

    return pl.pallas_call(
        body, name=name, grid=(M // tm, N // tn, nk),
        in_specs=[a_spec, b_spec], out_specs=out_spec, out_shape=out_shape,
        scratch_shapes=[pltpu.VMEM((tm, tn), F32)] if use_acc else [],
        compiler_params=_cparams(("parallel", "parallel", "arbitrary")),
    )(a, b)


def _divisors(n, mult, cap):
    d = [t for t in range(mult, min(n, cap) + 1, mult) if n % t == 0]
    return d[::-1] or [n]


def _mm_tiles(M, unit_n, unit_k, out_itemsize, tm_mult):
    best = None
    for tk in _divisors(unit_k, LANE, 2816):
        for tn in _divisors(unit_n, LANE, 1536):
            for tm in _divisors(M, tm_mult, 1024):
                vmem = 2 * 2 * (tm * tk + tk * tn) + 2 * tm * tn * out_itemsize + 4 * tm * tn * (2 if unit_k > tk else 1)
                if vmem > MM_VMEM_BUDGET:
                    continue
                steps = (M // tm) * (unit_n // tn) * (unit_k // tk)
                key = (steps, -tk, -tn)
                if best is None or key < best[0]:
                    best = (key, (tm, tn, tk))
                break
    return best[1]


def _rowmap(fn, name, M, row_ins, bc_ins, row_outs, acc_outs=(), after=None):
    row_ins = [r if isinstance(r, tuple) else (r, r.shape[1], 0) for r in row_ins]
    row_bytes = sum(w * a.dtype.itemsize for a, w, _ in row_ins) + sum(w * jnp.dtype(d).itemsize for w, d in row_outs)
    widest = max([w for _, w, _ in row_ins] + [w for w, _ in row_outs])
    row_bytes = 2 * row_bytes + 6 * 4 * widest
    tm = _pick(M, [t for t in (512, 256, 128, 64, 32, 16) if t * row_bytes <= ROWMAP_TILE_BYTES] + [16])
    n_in, n_row, n_acc = len(row_ins) + len(bc_ins), len(row_outs), len(acc_outs)

    def body(*refs):
        res = fn(*[r[...].astype(F32) for r in refs[:n_in]])
        res = res if isinstance(res, (tuple, list)) else (res,)
        outs = refs[n_in + (after is not None):]
        for k in range(n_row):
            outs[k][...] = res[k].astype(outs[k].dtype)
        if n_acc:
            @pl.when(pl.program_id(0) == 0)
            def _():
                for k in range(n_acc):
                    outs[n_row + k][...] = jnp.zeros_like(outs[n_row + k])

            for k in range(n_acc):
                outs[n_row + k][...] += res[n_row + k].astype(F32)

    in_specs = [pl.BlockSpec((tm, w), functools.partial(lambda i, blk: (i, blk), blk=blk)) for _, w, blk in row_ins]
    in_specs += [pl.BlockSpec(b.shape, lambda i: (0, 0)) for b in bc_ins]
    in_specs += [pl.BlockSpec(memory_space=pl.ANY)] * (after is not None)
    out_specs = [pl.BlockSpec((tm, w), lambda i: (i, 0)) for w, _ in row_outs]
    out_specs += [pl.BlockSpec((1, w), lambda i: (0, 0)) for w in acc_outs]
    out_shape = [jax.ShapeDtypeStruct((M, w), d) for w, d in row_outs]
    out_shape += [jax.ShapeDtypeStruct((1, w), F32) for w in acc_outs]
    return pl.pallas_call(
        body, name=name, grid=(M // tm,), in_specs=in_specs, out_specs=out_specs, out_shape=out_shape,
        compiler_params=_cparams(("arbitrary",) if n_acc else ("parallel",)),
    )(*[a for a, _, _ in row_ins], *bc_ins, *([] if after is None else [after]))


def _rms(x, g):
    return x * lax.rsqrt(jnp.mean(x * x, axis=-1, keepdims=True) + EPS) * g


def _normmod(x, g, sc, sh):
    return _rms(x, g) * (1.0 + sc) + sh


def _swap16(v):
    w = v.shape[1]
    lane = lax.broadcasted_iota(jnp.int32, v.shape, 1)
    return jnp.where((lane // 16) % 2 == 0, pltpu.roll(v, w - 16, 1), pltpu.roll(v, 16, 1))


def _rope(v, cos, sin_signed):
    return v * cos + _swap16(v) * sin_signed


def _rope_bwd(d, cos, sin_signed):
    return d * cos + _swap16(d * sin_signed)


def _mesh_pos():
    return lax.axis_index("x"), lax.axis_index("y"), lax.axis_index("c")


def _hbm_call(body, name, ins, out_shapes, n_sems):
    any_spec = pl.BlockSpec(memory_space=pl.ANY)
    return pl.pallas_call(
        body, name=name, out_shape=out_shapes, in_specs=[any_spec] * len(ins), out_specs=[any_spec] * len(out_shapes),
        scratch_shapes=[pltpu.SemaphoreType.DMA((n_sems,)), pltpu.SemaphoreType.DMA((n_sems,)),
                        pltpu.SemaphoreType.DMA((len(ins),))],
    )(*ins)


def _all_gather(xs, name):
    n = len(xs)

    def body(*refs):
        x_refs, out_refs, (send_sems, recv_sems, local_sems) = refs[:n], refs[n:2 * n], refs[2 * n:]
        x, y, c = _mesh_pos()
        me, sibling = (x, y, c), (x, y, 1 - c)
        chips = [(1 - x, y), (x, 1 - y), (1 - x, 1 - y)]
        locals_, first, passed, arrivals = [], [], [], []
        for a in range(n):
            def slot(px, py, pc, a=a):
                return out_refs[a].at[4 * px + 2 * py + pc]

            def copy(k, block, to, src=None, a=a, slot=slot):
                return pltpu.make_async_remote_copy(
                    src_ref=slot(*block) if src is None else src, dst_ref=slot(*block),
                    send_sem=send_sems.at[7 * a + k], recv_sem=recv_sems.at[7 * a + k], device_id=to, device_id_type=MESH)

            locals_.append(pltpu.make_async_copy(x_refs[a], slot(*me), local_sems.at[a]))
            first.append(copy(0, me, sibling, src=x_refs[a]))
            first += [copy(1 + j, me, (*chip, c), src=x_refs[a]) for j, chip in enumerate(chips)]
            passed.append([copy(4 + j, (*chip, c), sibling) for j, chip in enumerate(chips)])
            arrivals.append([copy(1 + j, (*chip, c), me) for j, chip in enumerate(chips)]
                            + [copy(0, sibling, me)] + [copy(4 + j, (*chip, 1 - c), me) for j, chip in enumerate(chips)])
        for cp in locals_ + first:
            cp.start()
        for j in range(3):
            for a in range(n):
                arrivals[a][j].wait_recv()
                passed[a][j].start()
        for a in range(n):
            for cp in arrivals[a][3:]:
                cp.wait_recv()
        for cp in first + [p for ps in passed for p in ps]:
            cp.wait_send()
        for cp in locals_:
            cp.wait()

    return _hbm_call(body, name, xs, [jax.ShapeDtypeStruct((N_DEV,) + x.shape, x.dtype) for x in xs], 7 * n)


def _rs_pair(ps, name):
    n = len(ps)

    def body(*refs):
        p_refs, out_refs, (send_sems, recv_sems, _) = refs[:n], refs[n:2 * n], refs[2 * n:]
        x, y, c = _mesh_pos()
        sends, recvs = [], []
        for a in range(n):
            for q in range(N_CHIP):
                sem = dict(send_sem=send_sems.at[4 * a + q], recv_sem=recv_sems.at[4 * a + q],
                           device_id=(x, y, 1 - c), device_id_type=MESH)
                sends.append(pltpu.make_async_remote_copy(src_ref=p_refs[a].at[2 * q + 1 - c], dst_ref=out_refs[a].at[q], **sem))
                recvs.append(pltpu.make_async_remote_copy(src_ref=p_refs[a].at[2 * q + c], dst_ref=out_refs[a].at[q], **sem))
        for cp in sends:
            cp.start()
        for cp in recvs:
            cp.wait_recv()
        for cp in sends:
            cp.wait_send()

    return _hbm_call(body, name, ps, [jax.ShapeDtypeStruct((N_CHIP,) + p.shape[1:], p.dtype) for p in ps], 4 * n)


def _xchg_copies(src_refs, land_refs, send_sems, recv_sems, slot_src):
    x, y, c = _mesh_pos()
    sends, recvs = [], []
    for a, (src, land) in enumerate(zip(src_refs, land_refs)):
        chips = land.shape[0] == N_CHIP
        npeer = land.shape[0] - 1
        me = 2 * x + y if chips else 4 * x + 2 * y + c
        for r in range(1, npeer + 1):
            px = 1 - x if r & (2 if chips else 4) else x
            py = 1 - y if r & (1 if chips else 2) else y
            pc = c if chips else (1 - c if r & 1 else c)
            peer = 2 * px + py if chips else 4 * px + 2 * py + pc
            sem = dict(send_sem=send_sems.at[npeer * a + r - 1], recv_sem=recv_sems.at[npeer * a + r - 1],
                       device_id=(px, py, pc), device_id_type=MESH)
            s = src.at[peer] if slot_src else src
            sends.append(pltpu.make_async_remote_copy(src_ref=s, dst_ref=land.at[me], **sem))
            recvs.append(pltpu.make_async_remote_copy(src_ref=s, dst_ref=land.at[peer], **sem))
    return sends, recvs


_HBM = pl.BlockSpec(memory_space=pltpu.HBM)
_SEM = pl.BlockSpec(memory_space=pltpu.SEMAPHORE)
_EFFECT = pltpu.SideEffectType.DATAFLOW_SIDE_EFFECTING


def _xchg_start(srcs, lands, slot_src, name):
    n = len(srcs)

    def body(*refs):
        sends, _ = _xchg_copies(refs[:n], refs[n:2 * n], refs[2 * n], refs[2 * n + 1], slot_src)
        for cp in sends:
            cp.start()
        refs[-1][...] = jnp.zeros_like(refs[-1])

    bufs = list(srcs) + list(lands)
    n_sems = n * (lands[0].shape[0] - 1)
    res = pl.pallas_call(
        body, name=name,
        out_shape=(pltpu.SemaphoreType.DMA((n_sems,)), pltpu.SemaphoreType.DMA((n_sems,)))
        + tuple(pltpu.HBM(b.shape, b.dtype) for b in bufs) + (jax.ShapeDtypeStruct((SUB, LANE), F32),),
        in_specs=(_HBM,) * (2 * n), out_specs=(_SEM, _SEM) + (_HBM,) * (2 * n) + (pl.BlockSpec(memory_space=pltpu.VMEM),),
        input_output_aliases={i: 2 + i for i in range(2 * n)},
        compiler_params=pltpu.CompilerParams(has_side_effects=_EFFECT),
    )(*[pltpu.with_memory_space_constraint(b, pltpu.HBM) for b in bufs])
    return res[0], res[1], res[2:-1], res[-1]


def _xchg_wait(send_sems, recv_sems, thru, after, slot_src, name):
    n = len(thru) // 2

    def body(*refs):
        sends, recvs = _xchg_copies(refs[:n], refs[n:2 * n], refs[2 * n], refs[2 * n + 1], slot_src)
        for cp in sends:
            cp.wait_send()
        for cp in recvs:
            cp.wait_recv()

    res = pl.pallas_call(
        body, name=name, out_shape=tuple(pltpu.HBM(b.shape, b.dtype) for b in thru),
        in_specs=(_HBM,) * (2 * n) + (_SEM, _SEM, pl.BlockSpec(memory_space=pl.ANY)), out_specs=(_HBM,) * (2 * n),
        input_output_aliases={i: i for i in range(2 * n)},
        compiler_params=pltpu.CompilerParams(has_side_effects=_EFFECT),
    )(*thru, send_sems, recv_sems, after)
    return res[n:]


def _ag2_copy(land, sems, k, block, to, src=None):
    slot = land.at[4 * block[0] + 2 * block[1] + block[2]]
    return pltpu.make_async_remote_copy(src_ref=slot if src is None else src, dst_ref=slot, send_sem=sems[0].at[k],
                                        recv_sem=sems[1].at[k], device_id=to, device_id_type=MESH)


def _ag2_start(blocks, lands, name):
    n = len(blocks)

    def body(*refs):
        x, y, c = _mesh_pos()
        for a in range(n):
            sems = (refs[2 * n], refs[2 * n + 1])
            _ag2_copy(refs[n + a], sems, 4 * a, (x, y, c), (x, y, 1 - c), src=refs[a]).start()
            for j, chip in enumerate([(1 - x, y), (x, 1 - y), (1 - x, 1 - y)]):
                _ag2_copy(refs[n + a], sems, 4 * a + 1 + j, (x, y, c), (*chip, c), src=refs[a]).start()
        refs[-1][...] = jnp.zeros_like(refs[-1])

    bufs = list(blocks) + list(lands)
    res = pl.pallas_call(
        body, name=name,
        out_shape=(pltpu.SemaphoreType.DMA((4 * n,)), pltpu.SemaphoreType.DMA((4 * n,)))
        + tuple(pltpu.HBM(b.shape, b.dtype) for b in bufs) + (jax.ShapeDtypeStruct((SUB, LANE), F32),),
        in_specs=(_HBM,) * (2 * n), out_specs=(_SEM, _SEM) + (_HBM,) * (2 * n) + (pl.BlockSpec(memory_space=pltpu.VMEM),),
        input_output_aliases={i: 2 + i for i in range(2 * n)},
        compiler_params=pltpu.CompilerParams(has_side_effects=_EFFECT),
    )(*[pltpu.with_memory_space_constraint(b, pltpu.HBM) for b in bufs])
    return (res[0], res[1]), res[2:-1], res[-1]


def _ag2_mid(sems1, thru, after, name):
    n = len(thru) // 2

    def body(*refs):
        x, y, c = _mesh_pos()
        s1, s2 = (refs[2 * n], refs[2 * n + 1]), (refs[2 * n + 3], refs[2 * n + 4])
        for j, chip in enumerate([(1 - x, y), (x, 1 - y), (1 - x, 1 - y)]):
            for a in range(n):
                _ag2_copy(refs[n + a], s1, 4 * a + 1 + j, (*chip, c), (x, y, c)).wait_recv()
                _ag2_copy(refs[n + a], s2, 3 * a + j, (*chip, c), (x, y, 1 - c)).start()
        refs[-1][...] = jnp.zeros_like(refs[-1])

    res = pl.pallas_call(
        body, name=name,
        out_shape=(pltpu.SemaphoreType.DMA((3 * n,)), pltpu.SemaphoreType.DMA((3 * n,)))
        + tuple(pltpu.HBM(b.shape, b.dtype) for b in thru) + (jax.ShapeDtypeStruct((SUB, LANE), F32),),
        in_specs=(_HBM,) * (2 * n) + (_SEM, _SEM, pl.BlockSpec(memory_space=pl.ANY)),
        out_specs=(_SEM, _SEM) + (_HBM,) * (2 * n) + (pl.BlockSpec(memory_space=pltpu.VMEM),),
        input_output_aliases={i: 2 + i for i in range(2 * n)},
        compiler_params=pltpu.CompilerParams(has_side_effects=_EFFECT),
    )(*thru, *sems1, after)
    return (res[0], res[1]), res[2:-1], res[-1]


def _ag2_end(sems1, sems2, thru, after, name):
    n = len(thru) // 2

    def body(*refs):
        x, y, c = _mesh_pos()
        s1, s2 = (refs[2 * n], refs[2 * n + 1]), (refs[2 * n + 2], refs[2 * n + 3])
        chips = [(1 - x, y), (x, 1 - y), (1 - x, 1 - y)]
        for a in range(n):
            land = refs[n + a]
            _ag2_copy(land, s1, 4 * a, (x, y, c), (x, y, 1 - c), src=refs[a]).wait_send()
            _ag2_copy(land, s1, 4 * a, (x, y, 1 - c), (x, y, c)).wait_recv()
            for j, chip in enumerate(chips):
                _ag2_copy(land, s1, 4 * a + 1 + j, (x, y, c), (*chip, c), src=refs[a]).wait_send()
                _ag2_copy(land, s2, 3 * a + j, (*chip, c), (x, y, 1 - c)).wait_send()
                _ag2_copy(land, s2, 3 * a + j, (*chip, 1 - c), (x, y, c)).wait_recv()

    res = pl.pallas_call(
        body, name=name, out_shape=tuple(pltpu.HBM(b.shape, b.dtype) for b in thru),
        in_specs=(_HBM,) * (2 * n) + (_SEM,) * 4 + (pl.BlockSpec(memory_space=pl.ANY),), out_specs=(_HBM,) * (2 * n),
        input_output_aliases={i: i for i in range(2 * n)},
        compiler_params=pltpu.CompilerParams(has_side_effects=_EFFECT),
    )(*thru, *sems1, *sems2, after)
    return res[n:]


def _own_slot(block, me, slots=N_DEV):
    return lax.dynamic_update_slice(lax.empty((slots,) + block.shape, block.dtype), block[None], (me, 0, 0))


def _add_pair(p, r, name):
    _, R, C = p.shape
    tr = _pick(R, (512, 256, 128, 64, 32, 16))

    def body(c_ref, p_ref, r_ref, o_ref):
        o_ref[...] = (p_ref[...].astype(F32) + r_ref[...].astype(F32)).astype(o_ref.dtype)

    return pl.pallas_call(
        body, name=name, out_shape=jax.ShapeDtypeStruct((N_CHIP, R, C), p.dtype),
        grid_spec=pltpu.PrefetchScalarGridSpec(
            num_scalar_prefetch=1, grid=(N_CHIP, R // tr),
            in_specs=[pl.BlockSpec((None, None, tr, C), lambda q, i, c_ref: (q, c_ref[0], i, 0)),
                      pl.BlockSpec((None, tr, C), lambda q, i, c_ref: (q, i, 0))],
            out_specs=pl.BlockSpec((None, tr, C), lambda q, i, c_ref: (q, i, 0))),
        compiler_params=_cparams(("parallel", "parallel")),
    )(lax.axis_index("c").reshape(1).astype(jnp.int32), p.reshape(N_CHIP, 2, R, C), r)


def _sum_slots(g, name):
    ns, R, C = g.shape
    tr = _pick(R, (256, 128, 64, 32, 16))

    def body(g_ref, o_ref):
        acc = g_ref[0].astype(F32)
        for j in range(1, ns):
            acc = acc + g_ref[j].astype(F32)
        o_ref[...] = acc

    return pl.pallas_call(
        body, name=name, grid=(R // tr,),
        in_specs=[pl.BlockSpec((ns, tr, C), lambda i: (0, i, 0))], out_specs=pl.BlockSpec((tr, C), lambda i: (i, 0)),
        out_shape=jax.ShapeDtypeStruct((R, C), F32), compiler_params=_cparams(("parallel",)),
    )(g)


def _pack_rows(arrs, dtype):
    parts = []
    for a in arrs:
        flat = a.reshape(-1).astype(dtype)
        pad = (-flat.shape[0]) % (PACK_W * 16)
        parts.append(jnp.pad(flat, (0, pad)).reshape(-1, PACK_W))
    out = jnp.concatenate(parts, axis=0)
    return jnp.pad(out, ((0, (-out.shape[0]) % PACK_ROWS), (0, 0)))


def _packed_rows(shape):
    n = math.prod(shape)
    return (n + PACK_W * 16 - 1) // (PACK_W * 16) * 16


def _unpack_rows(packed, shapes):
    out, r0 = [], 0
    for s in shapes:
        rows, n = _packed_rows(s), math.prod(s)
        out.append(packed[r0:r0 + rows].reshape(rows * PACK_W)[:n].reshape(s))
        r0 += rows
    return out


def _adamw_math(w, g, m, v):
    m = ADAM_B1 * m + (1.0 - ADAM_B1) * g
    v = ADAM_B2 * v + (1.0 - ADAM_B2) * (g * g)
    m_hat = m / (1.0 - ADAM_B1 ** ADAM_STEP)
    v_hat = v / (1.0 - ADAM_B2 ** ADAM_STEP)
    delta = -ADAM_LR * (m_hat / (jnp.sqrt(v_hat) + ADAM_EPS) + ADAM_WD * w)
    return delta, m, v


def _adamw(w, g, m, v, name, after=None):
    R, C = w.shape
    return _rowmap(_adamw_math, name, R, [w, g, m, v], [], [(C, F32)] * 3, after=after)


def _s5_disc_math(lr, li, ldt, br, bi):
    dt = jnp.exp(ldt)
    mag = jnp.exp(lr * dt)
    ab_re, ab_im = mag * jnp.cos(li * dt), mag * jnp.sin(li * dt)
    den = lr * lr + li * li
    nr, ni = ab_re - 1.0, ab_im
    co_re = (nr * lr + ni * li) / den
    co_im = (ni * lr - nr * li) / den
    bb_re = co_re * br - co_im * bi
    bb_im = co_re * bi + co_im * br
    return ab_re, ab_im, bb_re, bb_im


def _s5_tables(a_re, a_im, ldt, b_re, b_im, c_re, c_im):
    _, G, P, N = b_re.shape
    nch = G // 8

    def body(lr_ref, li_ref, ldt_ref, br_ref, bi_ref, cr_ref, ci_ref, wre, wim, vre, vim, pwr, pwi):
        ar, ai, bb_re, bb_im = _s5_disc_math(lr_ref[0], li_ref[0], ldt_ref[0], br_ref[0], bi_ref[0])
        cr, ci = cr_ref[0], ci_ref[0]
        pr, pi = jnp.ones_like(ar), jnp.zeros_like(ar)
        for j in range(SUB + 1):
            pwr[0, j], pwi[0, j] = pr, pi
            if j < SUB:
                tabs = ((wre, bb_re * pr - bb_im * pi), (wim, bb_re * pi + bb_im * pr),
                        (vre, cr * pr - ci * pi), (vim, -(cr * pi + ci * pr)))
                for ref, val in tabs:
                    for s in range(nch):
                        ref[0, s, pl.ds(j * LANE, LANE), :] = val[s * 8:(s + 1) * 8].reshape(LANE, N).astype(BF16)
            pr, pi = pr * ar - pi * ai, pr * ai + pi * ar

    g1n = pl.BlockSpec((1, G, 1, N), lambda d: (d, 0, 0, 0))
    gpn = pl.BlockSpec((1, G, P, N), lambda d: (d, 0, 0, 0))
    tab = pl.BlockSpec((1, nch, SUB * LANE, N), lambda d: (d, 0, 0, 0))
    pw = pl.BlockSpec((1, SUB + 1, G, 1, N), lambda d: (d, 0, 0, 0, 0))
    s_tab = jax.ShapeDtypeStruct((2, nch, SUB * LANE, N), BF16)
    s_pw = jax.ShapeDtypeStruct((2, SUB + 1, G, 1, N), F32)
    return pl.pallas_call(
        body, name="s5_tables", grid=(2,),
        in_specs=[g1n, g1n, pl.BlockSpec((1, G, 1, 1), lambda d: (d, 0, 0, 0)), gpn, gpn, gpn, gpn],
        out_specs=[tab] * 4 + [pw] * 2, out_shape=[s_tab] * 4 + [s_pw] * 2,
        compiler_params=_cparams(("parallel",)),
    )(a_re, a_im, ldt, b_re, b_im, c_re, c_im)


def _s5_expand(t_re, t_im, name):
    _, nch, R, N = t_re.shape
    sw = 8 * N

    def body(re_ref, im_ref, o_ref):
        spread = (lax.broadcasted_iota(jnp.int32, (N, sw), 1) % N == lax.broadcasted_iota(jnp.int32, (N, sw), 0)).astype(BF16)
        row_g = (lax.broadcasted_iota(jnp.int32, (R, sw), 0) % LANE) // S5_GROUP
        keep = row_g == lax.broadcasted_iota(jnp.int32, (R, sw), 1) // N
        for half, ref in enumerate((re_ref, im_ref)):
            t = jnp.dot(ref[0, 0], spread, preferred_element_type=F32)
            o_ref[0, 0, :, pl.ds(half * sw, sw)] = jnp.where(keep, t, 0.0).astype(BF16)

    spec = pl.BlockSpec((1, 1, R, N), lambda d, s: (d, s, 0, 0))
    return pl.pallas_call(
        body, name=name, grid=(2, nch), in_specs=[spec, spec],
        out_specs=pl.BlockSpec((1, 1, R, 2 * sw), lambda d, s: (d, s, 0, 0)),
        out_shape=jax.ShapeDtypeStruct((2, nch, R, 2 * sw), BF16), compiler_params=_cparams(("parallel", "parallel")),
    )(t_re, t_im)


def _s5_param_bwd(a_re, a_im, ldt, b_re, b_im, da_re, da_im, dbb_re, dbb_im):
    _, G, P, N = b_re.shape

    def body(lr_ref, li_ref, ldt_ref, br_ref, bi_ref, dar, dai, dbr, dbi, o_lr, o_li, o_ldt, o_br, o_bi):
        _, vjp = jax.vjp(_s5_disc_math, lr_ref[0], li_ref[0], ldt_ref[0], br_ref[0], bi_ref[0])
        o_lr[0], o_li[0], o_ldt[0], o_br[0], o_bi[0] = vjp((dar[0], dai[0], dbr[0], dbi[0]))

    g1n = pl.BlockSpec((1, G, 1, N), lambda d: (d, 0, 0, 0))
    g11 = pl.BlockSpec((1, G, 1, 1), lambda d: (d, 0, 0, 0))
    gpn = pl.BlockSpec((1, G, P, N), lambda d: (d, 0, 0, 0))
    s_g1n, s_g11, s_gpn = (jax.ShapeDtypeStruct(s, F32) for s in ((2, G, 1, N), (2, G, 1, 1), (2, G, P, N)))
    return pl.pallas_call(
        body, name="s5_param_bwd", grid=(2,),
        in_specs=[g1n, g1n, g11, gpn, gpn, g1n, g1n, gpn, gpn], out_specs=[g1n, g1n, g11, gpn, gpn],
        out_shape=[s_g1n, s_g1n, s_g11, s_gpn, s_gpn], compiler_params=_cparams(("parallel",)),
    )(a_re, a_im, ldt, b_re, b_im, da_re, da_im, dbb_re, dbb_im)


def _shift_stack(u, back):
    tb = u.shape[0]
    tau = lax.broadcasted_iota(jnp.int32, u.shape, 0) % SUB
    parts = [u]
    for j in range(1, SUB):
        if back:
            parts.append(jnp.where(tau >= j, pltpu.roll(u, j, 0), 0.0))
        else:
            parts.append(jnp.where(tau <= SUB - 1 - j, pltpu.roll(u, tb - j, 0), 0.0))
    return jnp.concatenate(parts, axis=1).astype(BF16)


def _cmul_add(tile, pw, carry, sw):
    pr, pi, cr, ci = pw[:, :sw], pw[:, sw:], carry[:, :sw], carry[:, sw:]
    return tile + jnp.concatenate([pr * cr - pi * ci, pr * ci + pi * cr], axis=1)


def _tile_scan(buf, base, ntile, pw, carry, sw, causal):
    def step(k, c):
        i = k if causal else ntile - 1 - k
        r = pl.multiple_of(base + i * SUB, SUB)
        tile = _cmul_add(buf[pl.ds(r, SUB), :], pw, c, sw)
        buf[pl.ds(r, SUB), :] = tile
        return tile[SUB - 1:SUB, :] if causal else tile[0:1, :]

    return lax.fori_loop(0, ntile, step, carry)


def _s5_fwd(h_all, waug, vaug, pw, S5W, T, d, causal, name):
    S = h_all.shape[0]
    _, nch, _, sw2 = waug.shape
    sw = sw2 // 2
    tb = _pick(math.gcd(T, S - T), (256, 128, 64, 32, 16))
    ntile, nt, off = tb // SUB, S // tb, T // tb
    rb = (lambda s, t: ((t + off) % nt, s)) if causal else (lambda s, t: (nt - 1 - t, s))

    def body(u_ref, w_ref, v_ref, p_ref, y_ref, h_ref, hblk, carry):
        @pl.when(pl.program_id(1) == 0)
        def _():
            carry[...] = jnp.zeros_like(carry)

        hblk[...] = jnp.dot(_shift_stack(u_ref[...], causal), w_ref[...], preferred_element_type=F32)
        carry[...] = _tile_scan(hblk, 0, ntile, p_ref[...], carry[...], sw, causal)
        hb = hblk[...].astype(BF16)
        h_ref[...] = hb
        y_ref[...] = lax.dot_general(hb, v_ref[...], _NT, preferred_element_type=F32)

    return pl.pallas_call(
        body, name=name, grid=(nch, nt),
        in_specs=[pl.BlockSpec((tb, LANE), rb),
                  pl.BlockSpec((None, None, SUB * LANE, sw2), lambda s, t: (d, s, 0, 0)),
                  pl.BlockSpec((None, None, LANE, sw2), lambda s, t: (d, s, 0, 0)),
                  pl.BlockSpec((None, SUB, sw2), lambda s, t: (s, 0, 0))],
        out_specs=[pl.BlockSpec((tb, LANE), rb), pl.BlockSpec((tb, sw2), rb)],
        out_shape=[jax.ShapeDtypeStruct((S, S5W), F32), jax.ShapeDtypeStruct((S, nch * sw2), BF16)],
        scratch_shapes=[pltpu.VMEM((tb, sw2), F32), pltpu.VMEM((1, sw2), F32)],
        compiler_params=_cparams(("parallel", "arbitrary")),
    )(h_all, waug, vaug, pw)


def _s5_bwd(dy_all, h_all, hs, waug, vaug, pwc, S5W, T, d, causal, name):
    S = h_all.shape[0]
    _, nch, _, sw2 = waug.shape
    sw = sw2 // 2
    tb = _pick(math.gcd(T, S - T), (256, 128, 64, 32, 16))
    ntile, nt, off = tb // SUB, S // tb, T // tb
    rb = (lambda s, t: ((nt - 1 - t + off) % nt, s)) if causal else (lambda s, t: (t, s))
    adj_causal = not causal
    edge = SUB - 1 if adj_causal else SUB + tb
    keep_src, keep_dst = (tb, 0) if adj_causal else (SUB, SUB + tb)

    def body(dy_ref, u_ref, h_ref, w_ref, v_ref, p_ref, du_ref, dbb_ref, dc_ref, da_ref, lam):
        @pl.when(pl.program_id(1) == 0)
        def _():
            lam[pl.ds(0, SUB), :] = jnp.zeros((SUB, sw2), F32)
            lam[pl.ds(SUB + tb, SUB), :] = jnp.zeros((SUB, sw2), F32)
            dbb_ref[...] = jnp.zeros_like(dbb_ref)
            dc_ref[...] = jnp.zeros_like(dc_ref)
            da_ref[...] = jnp.zeros_like(da_ref)

        dy = dy_ref[...]
        lam[pl.ds(SUB, tb), :] = jnp.dot(_shift_stack(dy, adj_causal), v_ref[...], preferred_element_type=F32)
        _tile_scan(lam, SUB, ntile, p_ref[...], lam[pl.ds(edge, 1), :], sw, adj_causal)
        lb = lam[pl.ds(SUB, tb), :].astype(BF16)
        du_ref[...] = lax.dot_general(lb, w_ref[...], _NT, preferred_element_type=F32)
        dbb_ref[...] += lax.dot_general(u_ref[...].astype(BF16), lb, _TN, preferred_element_type=F32)
        dc_ref[...] += lax.dot_general(h_ref[...], dy.astype(BF16), _TN, preferred_element_type=F32)
        h = h_ref[...].astype(F32)
        ln = lam[pl.ds(SUB + 1 if causal else SUB - 1, tb), :]
        hr, hi, lr, li = h[:, :sw], h[:, sw:], ln[:, :sw], ln[:, sw:]
        da_ref[...] += jnp.concatenate([jnp.sum(hr * lr + hi * li, axis=0, keepdims=True),
                                        jnp.sum(hr * li - hi * lr, axis=0, keepdims=True)], axis=1)
        lam[pl.ds(keep_dst, SUB), :] = lam[pl.ds(keep_src, SUB), :]

    fixed = lambda s, t: (s, 0, 0)
    return pl.pallas_call(
        body, name=name, grid=(nch, nt),
        in_specs=[pl.BlockSpec((tb, LANE), rb), pl.BlockSpec((tb, LANE), rb), pl.BlockSpec((tb, sw2), rb),
                  pl.BlockSpec((None, None, LANE, sw2), lambda s, t: (d, s, 0, 0)),
                  pl.BlockSpec((None, None, SUB * LANE, sw2), lambda s, t: (d, s, 0, 0)),
                  pl.BlockSpec((None, SUB, sw2), fixed)],
        out_specs=[pl.BlockSpec((tb, LANE), rb), pl.BlockSpec((None, LANE, sw2), fixed),
                   pl.BlockSpec((None, sw2, LANE), fixed), pl.BlockSpec((None, 1, sw2), fixed)],
        out_shape=[jax.ShapeDtypeStruct((S, S5W), F32), jax.ShapeDtypeStruct((nch, LANE, sw2), F32),
                   jax.ShapeDtypeStruct((nch, sw2, LANE), F32), jax.ShapeDtypeStruct((nch, 1, sw2), F32)],
        scratch_shapes=[pltpu.VMEM((tb + 2 * SUB, sw2), F32)],
        compiler_params=_cparams(("parallel", "arbitrary")),
    )(dy_all, h_all, hs, waug, vaug, pwc)


def _attn_fwd(qn, qr, kv, kr, H, scale):
    T, S = qn.shape[0], kv.shape[0]
    tq = _pick(T, (256, 128, 64, 32, 16))

    def body(qn_ref, qr_ref, kn_ref, v_ref, kr_ref, o_ref, lse_ref):
        q = jnp.concatenate([qn_ref[...], qr_ref[...]], axis=1)
        k = jnp.concatenate([kn_ref[...], kr_ref[...]], axis=1)
        s = lax.dot_general(q, k, _NT, preferred_element_type=F32) * scale
        m = jnp.max(s, axis=1, keepdims=True)
        p = jnp.exp(s - m)
        l = jnp.sum(p, axis=1, keepdims=True)
        o_ref[...] = jnp.dot((p / l).astype(BF16), v_ref[...], preferred_element_type=F32).astype(o_ref.dtype)
        lse_ref[0] = m + jnp.log(l)

    q_spec = pl.BlockSpec((tq, LANE), lambda h, i: (i, h))
    return pl.pallas_call(
        body, name="attn_fwd", grid=(H, T // tq),
        in_specs=[q_spec, q_spec, pl.BlockSpec((S, LANE), lambda h, i: (0, h)), pl.BlockSpec((S, LANE), lambda h, i: (0, H + h)),
                  pl.BlockSpec((S, LANE), lambda h, i: (0, 0))],
        out_specs=[q_spec, pl.BlockSpec((1, tq, 1), lambda h, i: (h, i, 0))],
        out_shape=[jax.ShapeDtypeStruct((T, H * LANE), BF16), jax.ShapeDtypeStruct((H, T, 1), F32)],
        compiler_params=_cparams(("parallel", "parallel")),
    )(qn, qr, kv, kv, kr)


def _attn_bwd(qn, qr, kv, kr, do, lse, H, scale):
    T, S = qn.shape[0], kv.shape[0]
    tq = _pick(T, (256, 128, 64, 32, 16))
    nq = T // tq

    def body(qn_ref, qr_ref, kn_ref, v_ref, kr_ref, do_ref, lse_ref, dqn_ref, dqr_ref, dkn_ref, dkr_ref, dv_ref, dk_acc, dv_acc):
        i = pl.program_id(1)
        q = jnp.concatenate([qn_ref[...], qr_ref[...]], axis=1)
        k = jnp.concatenate([kn_ref[...], kr_ref[...]], axis=1)
        v, d_o = v_ref[...], do_ref[...]
        s = lax.dot_general(q, k, _NT, preferred_element_type=F32) * scale
        p = jnp.exp(s - lse_ref[0])
        dv_part = lax.dot_general(p.astype(BF16), d_o, _TN, preferred_element_type=F32)
        dp = lax.dot_general(d_o, v, _NT, preferred_element_type=F32)
        ds = (p * (dp - jnp.sum(p * dp, axis=1, keepdims=True)) * scale).astype(BF16)
        dq = jnp.dot(ds, k, preferred_element_type=F32)
        dqn_ref[...] = dq[:, :LANE].astype(dqn_ref.dtype)
        dqr_ref[...] = dq[:, LANE:].astype(dqr_ref.dtype)
        dk_part = lax.dot_general(ds, q, _TN, preferred_element_type=F32)

        @pl.when(i == 0)
        def _():
            dk_acc[...] = dk_part
            dv_acc[...] = dv_part

        @pl.when(i > 0)
        def _():
            dk_acc[...] += dk_part
            dv_acc[...] += dv_part

        @pl.when(i == nq - 1)
        def _():
            dkn_ref[...] = dk_acc[:, :LANE].astype(dkn_ref.dtype)
            dkr_ref[...] = dk_acc[:, LANE:].astype(dkr_ref.dtype)
            dv_ref[...] = dv_acc[...].astype(dv_ref.dtype)

    q_spec = pl.BlockSpec((tq, LANE), lambda h, i: (i, h))
    k_spec = pl.BlockSpec((S, LANE), lambda h, i: (0, h))
    t_shape, s_shape = jax.ShapeDtypeStruct((T, H * LANE), BF16), jax.ShapeDtypeStruct((S, H * LANE), BF16)
    return pl.pallas_call(
        body, name="attn_bwd", grid=(H, nq),
        in_specs=[q_spec, q_spec, k_spec, pl.BlockSpec((S, LANE), lambda h, i: (0, H + h)),
                  pl.BlockSpec((S, LANE), lambda h, i: (0, 0)), q_spec, pl.BlockSpec((1, tq, 1), lambda h, i: (h, i, 0))],
        out_specs=[q_spec, q_spec, k_spec, k_spec, k_spec], out_shape=[t_shape, t_shape, s_shape, s_shape, s_shape],
        scratch_shapes=[pltpu.VMEM((S, 2 * LANE), F32), pltpu.VMEM((S, LANE), F32)],
        compiler_params=_cparams(("parallel", "arbitrary")),
    )(qn, qr, kv, kv, kr, do, lse)


def _rope_tables(T, heads):
    rows = T // GRID_W
    row = jnp.repeat(jnp.arange(rows, dtype=F32), GRID_W)
    col = jnp.tile(jnp.arange(GRID_W, dtype=F32), rows)
    n_freq = QK_ROPE // 4
    inv = ROPE_BASE ** (-jnp.arange(n_freq, dtype=F32) / n_freq)
    ar, ac = row[:, None] * inv, col[:, None] * inv
    cos = jnp.concatenate([jnp.cos(ar), jnp.cos(ar), jnp.cos(ac), jnp.cos(ac)], axis=1)
    sin = jnp.concatenate([-jnp.sin(ar), jnp.sin(ar), -jnp.sin(ac), jnp.sin(ac)], axis=1)
    pad = lambda t: jnp.tile(jnp.pad(t, ((0, 0), (0, LANE - QK_ROPE))), (1, heads))
    return pad(cos), pad(sin)


def _dw(a, dy, w, name):
    return _mm(a, dy, "tn", BF16, name, out_slots=w.shape[0] if w.ndim == 3 else None)


def _local_step(x, ctx, tgt, m_lat, m_ctx, p, W, goff, hooks=None):
    T, D = x.shape
    Tc = ctx.shape[0]
    S = T + Tc
    S5W = p["s5_d"].shape[1]
    QR, KVR = p["q_norm"].shape[1], p["kv_norm"].shape[1]
    H = W["w_uq"].shape[1] // (2 * LANE)
    G, N = p["s5_a_re"].shape[1:]
    P = S5_GROUP
    nch = G // 8
    o_cq, o_ckv, o_kr = S5W, S5W + QR, S5W + QR + KVR
    assert o_cq % QR == 0 and o_ckv % KVR == 0 and o_kr % LANE == 0 and goff % D == 0 and S5W % LANE == 0 and G % 8 == 0
    assert 8 * P == LANE
    row = lambda k, m: m[k:k + 1]
    sh1, sc1, g1, sh2, sc2, g2 = (row(k, m_lat) for k in range(6))
    csh1, csc1 = row(0, m_ctx), row(1, m_ctx)
    n1, n2, nf = p["norm1"], p["norm2"], p["norm_f"]

    (xm_lat,) = _rowmap(_normmod, "norm1_lat", T, [x], [n1, sc1, sh1], [(D, BF16)])
    (xm_ctx,) = _rowmap(_normmod, "norm1_ctx", Tc, [ctx], [n1, csc1, csh1], [(D, BF16)])
    xm_all = jnp.concatenate([xm_lat, xm_ctx], axis=0)
    h_all = _mm(xm_all, W["w_in"], "nn", F32, "mm_in")

    a_re, a_im = p["s5_a_re"][:, :, None, :], p["s5_a_im"][:, :, None, :]
    ldt = p["s5_log_dt"][:, :, None, None]
    b_re, b_im = p["s5_b_re"].transpose(0, 1, 3, 2), p["s5_b_im"].transpose(0, 1, 3, 2)
    wre, wim, vre, vim, pwr, pwi = _s5_tables(a_re, a_im, ldt, b_re, b_im, p["s5_c_re"], p["s5_c_im"])
    waug = _s5_expand(wre, wim, "s5_expand_b")
    vaug = _s5_expand(vre, vim, "s5_expand_c")
    lanes = lambda t: t.reshape(2, SUB + 1, nch, 8 * N).transpose(0, 2, 1, 3)
    pw_re, pw_im = lanes(pwr), lanes(pwi)
    near = lambda t: t[:, :, 1:]
    far = lambda t: t[:, :, :0:-1]
    pw_c = jnp.concatenate([near(pw_re), near(pw_im)], axis=-1)
    pw_a = jnp.concatenate([far(pw_re), far(pw_im)], axis=-1)
    pwc_c = jnp.concatenate([near(pw_re), -near(pw_im)], axis=-1)
    pwc_a = jnp.concatenate([far(pw_re), -far(pw_im)], axis=-1)
    y0, hs0 = _s5_fwd(h_all, waug, vaug, pw_c[0], S5W, T, 0, True, "s5_scan_fwd0")
    y1, hs1 = _s5_fwd(h_all, waug, vaug, pw_a[1], S5W, T, 1, False, "s5_scan_fwd1")

    def s5_combine(u, yf, yr, dskip):
        y5 = dskip * u + yf + yr
        return y5, jax.nn.gelu(y5)

    y5, z = _rowmap(s5_combine, "s5_combine", T, [(h_all, S5W, 0), y0, y1], [p["s5_d"]], [(S5W, F32), (S5W, BF16)])

    (qn,) = _rowmap(_rms, "q_norm", T, [(h_all, QR, o_cq // QR)], [p["q_norm"]], [(QR, BF16)])
    (kvn,) = _rowmap(_rms, "kv_norm", S, [(h_all, KVR, o_ckv // KVR)], [p["kv_norm"]], [(KVR, BF16)])
    qraw = _mm(qn, W["w_uq"], "nn", F32, "mm_uq")
    kvraw = _mm(kvn, W["w_ukv"], "nn", BF16, "mm_ukv")
    cos_q, sin_q = _rope_tables(T, H)
    padl = lambda t: jnp.pad(t[:, :LANE], ((0, Tc), (0, 0)))
    cos_k = padl(cos_q) + jnp.pad(jnp.ones((Tc, LANE), F32), ((T, 0), (0, 0)))
    sin_k = padl(sin_q)
    hn = H * LANE

    def q_post(q, cos, sin):
        return q[:, :hn], _rope(q[:, hn:], cos, sin)

    q_nope, q_rope = _rowmap(q_post, "q_rope", T, [qraw, cos_q, sin_q], [], [(hn, BF16), (hn, BF16)])
    (kr,) = _rowmap(_rope, "k_rope", S, [(h_all, LANE, o_kr // LANE), cos_k, sin_k], [], [(LANE, BF16)])
    scale = (QK_NOPE + QK_ROPE) ** -0.5
    o, lse = _attn_fwd(q_nope, q_rope, kvraw, kr, H, scale)
    g1_fwd = g1
    if hooks:
        W = {**W, **hooks["mix_weights"](o)}
        g1_fwd = g1 + hooks["ffn_mid"](o)[:1, :1]

    zz = _mm(z, W["w_glu"], "nn", BF16, "mm_glu")
    br_mla = _mm(o, W["w_mla_o"], "nn", BF16, "mm_mla_o")

    def merge(zz, brm, gs, gm):
        a, b = zz[:, :D], zz[:, D:]
        return jax.nn.sigmoid(gs) * (a * jax.nn.sigmoid(b)) + jax.nn.sigmoid(gm) * brm

    gb = goff // D
    merge_ins = [zz, br_mla, (h_all, D, gb), (h_all, D, gb + 1)]
    (mix,) = _rowmap(merge, "merge", T, merge_ins, [], [(D, BF16)])
    out1 = _mm(mix, W["w_out"], "nn", F32, "mm_out")

    def resid_norm2(x, out1, g1, n2, sc2, sh2):
        x1 = x + g1 * out1
        return x1, _normmod(x1, n2, sc2, sh2)

    x1, hm = _rowmap(resid_norm2, "resid_norm2", T, [x, out1], [g1_fwd, n2, sc2, sh2], [(D, F32), (D, BF16)])

    if hooks:
        W = {**W, **hooks["ffn_weights"](hm)}
    FF = W["w_ffn_out"].shape[0]
    assert FF % LANE == 0
    ab = _mm(hm, W["w_ffn_in"], "nn", BF16, "mm_ffn_in")

    def swiglu_act(a, b):
        return jax.nn.silu(a) * b

    (f,) = _rowmap(swiglu_act, "ffn_act", T, [(ab, FF, 0), (ab, FF, 1)], [], [(FF, BF16)])
    out2 = _mm(f, W["w_ffn_out"], "nn", F32, "mm_ffn_out")

    def loss_rows(x1, out2, g2, nf, tgt):
        y = _rms(x1 + g2 * out2, nf)
        return 0.5 * jnp.sum(jnp.mean(jnp.square(y - tgt), axis=-1))

    def final(x1, out2, tgt, g2, nf):
        val, (dx1, dout2, dg2, dnf) = jax.value_and_grad(loss_rows, argnums=(0, 1, 2, 3))(x1, out2, g2, nf, tgt)
        return dx1, dout2, jnp.full((1, LANE), val, F32), dg2, dnf

    dx2, dout2, loss_acc, dg2, dnf = _rowmap(final, "final_loss", T, [x1, out2, tgt], [g2, nf],
                                             [(D, F32), (D, BF16)], [LANE, D, D])

    gW = {}
    df = _mm(dout2, W["w_ffn_out"], "nt", BF16, "mm_ffn_out_dx")
    gW["w_ffn_out"] = _dw(f, dout2, W["w_ffn_out"], "mm_ffn_out_dw")

    def swiglu_bwd(a, b, df):
        _, vjp = jax.vjp(swiglu_act, a, b)
        da, db = vjp(df)
        return jnp.concatenate([da, db], axis=1)

    (dab,) = _rowmap(swiglu_bwd, "ffn_act_bwd", T, [(ab, FF, 0), (ab, FF, 1), df], [], [(2 * FF, BF16)])
    dhm = _mm(dab, W["w_ffn_in"], "nt", F32, "mm_ffn_in_dx")
    gW["w_ffn_in"] = _dw(hm, dab, W["w_ffn_in"], "mm_ffn_in_dw")
    if hooks:
        token = hooks["send_grads"](FFN, [gW.pop(n) for n in FFN])
        g1 = g1 if token is None else g1 + token[:1, :1]

    def resid_norm2_bwd(x, out1, dx2, dhm, g1, n2, sc2, sh2):
        _, vjp = jax.vjp(resid_norm2, x, out1, g1, n2, sc2, sh2)
        dx, dout1, dg1, dn2, dsc2, dsh2 = vjp((dx2, dhm))
        return dx, dout1, dg1, dn2, dsc2, dsh2

    dx1, dout1, dg1, dn2, dsc2, dsh2 = _rowmap(resid_norm2_bwd, "resid_norm2_bwd", T, [x, out1, dx2, dhm],
                                               [g1, n2, sc2, sh2], [(D, F32), (D, BF16)], [D, D, D, D])

    dmix = _mm(dout1, W["w_out"], "nt", BF16, "mm_out_dx")
    gW["w_out"] = _dw(mix, dout1, W["w_out"], "mm_out_dw")

    def merge_bwd(zz, brm, gs, gm, dmix):
        _, vjp = jax.vjp(merge, zz, brm, gs, gm)
        dzz, dbrm, dgs, dgm = vjp(dmix)
        return dzz, dbrm, jnp.concatenate([dgs, dgm], axis=1)

    dzz, dbrm, dgates = _rowmap(merge_bwd, "merge_bwd", T, merge_ins + [dmix], [],
                                [(2 * D, BF16), (D, BF16), (2 * D, BF16)])
    do = _mm(dbrm, W["w_mla_o"], "nt", BF16, "mm_mla_o_dx")
    gW["w_mla_o"] = _dw(o, dbrm, W["w_mla_o"], "mm_mla_o_dw")
    dz = _mm(dzz, W["w_glu"], "nt", BF16, "mm_glu_dx")
    gW["w_glu"] = _dw(z, dzz, W["w_glu"], "mm_glu_dw")
    d_skip_w = p["s5_d"]
    if hooks:
        token = hooks["send_grads"](MIX, [gW.pop(n) for n in MIX])
        d_skip_w = d_skip_w if token is None else d_skip_w + token[:1, :1]

    def s5_combine_bwd(u, y5, dz, dskip):
        _, vjp = jax.vjp(lambda y: jax.nn.gelu(y), y5)
        (dy5,) = vjp(dz)
        return dy5, jnp.sum(dy5 * u, axis=0, keepdims=True)

    dy5, d_skip = _rowmap(s5_combine_bwd, "s5_combine_bwd", T, [(h_all, S5W, 0), y5, dz], [d_skip_w], [(S5W, F32)], [S5W])

    dq_nope, dq_rope, dk_nope, dkr_heads, dv = _attn_bwd(q_nope, q_rope, kvraw, kr, do, lse, H, scale)

    def q_post_bwd(dqn, dqr, cos, sin):
        return jnp.concatenate([dqn, _rope_bwd(dqr, cos, sin)], axis=1)

    (dqraw,) = _rowmap(q_post_bwd, "q_rope_bwd", T, [dq_nope, dq_rope, cos_q, sin_q], [], [(2 * hn, BF16)])
    dkvraw = jnp.concatenate([dk_nope, dv], axis=1)

    def k_rope_bwd(dkh, cos, sin):
        d = dkh[:, :LANE]
        for h in range(1, H):
            d = d + dkh[:, h * LANE:(h + 1) * LANE]
        return _rope_bwd(d, cos, sin)

    (dkr,) = _rowmap(k_rope_bwd, "k_rope_bwd", S, [dkr_heads, cos_k, sin_k], [], [(LANE, BF16)])
    dqn = _mm(dqraw, W["w_uq"], "nt", F32, "mm_uq_dx")
    gW["w_uq"] = _dw(qn, dqraw, W["w_uq"], "mm_uq_dw")
    dkvn = _mm(dkvraw, W["w_ukv"], "nt", F32, "mm_ukv_dx")
    gW["w_ukv"] = _dw(kvn, dkvraw, W["w_ukv"], "mm_ukv_dw")

    def rms_bwd(cx, dn, g):
        _, vjp = jax.vjp(_rms, cx, g)
        return vjp(dn)

    dcq, dq_norm = _rowmap(rms_bwd, "q_norm_bwd", T, [(h_all, QR, o_cq // QR), dqn], [p["q_norm"]], [(QR, BF16)], [QR])
    dckv, dkv_norm = _rowmap(rms_bwd, "kv_norm_bwd", S, [(h_all, KVR, o_ckv // KVR), dkvn], [p["kv_norm"]],
                             [(KVR, BF16)], [KVR])

    dy_all = jnp.concatenate([dy5, jnp.zeros((Tc, S5W), F32)], axis=0)
    du0, dbb0, dc0, da0 = _s5_bwd(dy_all, h_all, hs0, waug, vaug, pwc_a[0], S5W, T, 0, True, "s5_scan_bwd0")
    du1, dbb1, dc1, da1 = _s5_bwd(dy_all, h_all, hs1, waug, vaug, pwc_c[1], S5W, T, 1, False, "s5_scan_bwd1")

    def du_combine(a, b, dy, dskip):
        return a + b + dskip * dy

    (du_all,) = _rowmap(du_combine, "s5_du", S, [du0, du1, dy_all], [p["s5_d"]], [(S5W, BF16)])
    dbb = jnp.einsum("dsgpcgn->dcsgpn", jnp.stack([dbb0, dbb1]).reshape(2, nch, 8, P, 2, 8, N)).reshape(2, 2, G, P, N)
    dcm = jnp.einsum("dscgngp->dcsgpn", jnp.stack([dc0, dc1]).reshape(2, nch, 2, 8, N, 8, P)).reshape(2, 2, G, P, N)
    da = jnp.stack([da0, da1]).reshape(2, nch, 2, 8, N).transpose(0, 2, 1, 3, 4).reshape(2, 2, G, 1, N)
    d_lr, d_li, d_ldt, d_br, d_bi = _s5_param_bwd(a_re, a_im, ldt, b_re, b_im, da[:, 0], da[:, 1], dbb[:, 0], dbb[:, 1])

    lat_only = lambda t: jnp.pad(t, ((0, Tc), (0, 0)))
    dh_all = jnp.concatenate([du_all, lat_only(dcq), dckv, dkr, jnp.zeros((S, goff - o_kr - LANE), BF16), lat_only(dgates)],
                             axis=1)
    dxm = _mm(dh_all, W["w_in"], "nt", F32, "mm_in_dx")
    gW["w_in"] = _dw(xm_all, dh_all, W["w_in"], "mm_in_dw")

    def norm1_bwd(x, dxm, dx1, n1, sc, sh):
        _, vjp = jax.vjp(_normmod, x, n1, sc, sh)
        dx, dn, dsc, dsh = vjp(dxm)
        return dx + dx1, dn, dsc, dsh

    grad_x, dn1_l, dsc1, dsh1 = _rowmap(norm1_bwd, "norm1_lat_bwd", T, [x, dxm, dx1], [n1, sc1, sh1], [(D, F32)], [D, D, D])

    def norm1_ctx_bwd(x, dxm, n1, sc, sh):
        _, vjp = jax.vjp(_normmod, x, n1, sc, sh)
        return vjp(dxm)[1:]

    dn1_c, dcsc1, dcsh1 = _rowmap(norm1_ctx_bwd, "norm1_ctx_bwd", Tc, [ctx, dxm[T:]], [n1, csc1, csh1], [], [D, D, D])

    zero = jnp.zeros((1, D), F32)
    dm_lat = jnp.concatenate([dsh1, dsc1, dg1, dsh2, dsc2, dg2], axis=0)
    dm_ctx = jnp.concatenate([dcsh1, dcsc1, zero, zero, zero, zero], axis=0)
    small = {
        "norm1": dn1_l + dn1_c, "norm2": dn2, "norm_f": dnf, "q_norm": dq_norm, "kv_norm": dkv_norm, "s5_d": d_skip,
        "s5_a_re": d_lr, "s5_a_im": d_li, "s5_log_dt": d_ldt, "s5_b_re": d_br.transpose(0, 1, 3, 2),
        "s5_b_im": d_bi.transpose(0, 1, 3, 2), "s5_c_re": dcm[:, 0], "s5_c_im": -dcm[:, 1],
    }
    return loss_acc[:, :1], grad_x, small, dm_lat, dm_ctx, gW


BIG = ("w_in", "w_uq", "w_ukv", "w_glu", "w_mla_o", "w_out", "w_ffn_in", "w_ffn_out")
FFN = ("w_ffn_in", "w_ffn_out")
MIX = ("w_out", "w_mla_o", "w_glu")
ROW_SHARDED = ("w_out", "w_ffn_out")
RELAID = ("w_in", "w_uq", "w_ukv")
SMALL = ("c_ctx", "b_mod", "norm1", "norm2", "s5_a_re", "s5_a_im", "s5_log_dt", "s5_b_re", "s5_b_im", "s5_c_re",
         "s5_c_im", "s5_d", "q_norm", "kv_norm", "norm_f")
S5_BULK = ("s5_b_re", "s5_b_im", "s5_c_re", "s5_c_im")
WEIGHTS = ("c_ctx", "w_mod", "b_mod", "norm1", "norm2", "w_in", "s5_a_re", "s5_a_im", "s5_log_dt", "s5_b_re", "s5_b_im",
           "s5_c_re", "s5_c_im", "s5_d", "w_glu", "q_norm", "kv_norm", "w_uq", "w_ukv", "w_mla_o", "w_out", "w_ffn_in",
           "w_ffn_out", "norm_f")


def _heads_split(w, heads, first):
    k = w.shape[0]
    w3 = w.reshape(k, heads, -1)
    return jnp.concatenate([w3[:, :, :first].reshape(k, -1), w3[:, :, first:].reshape(k, -1)], axis=1)


def _uq_layout(w, heads):
    k = w.shape[0]
    w3 = w.reshape(k, heads, QK_NOPE + QK_ROPE)
    rope = jnp.pad(w3[:, :, QK_NOPE:], ((0, 0), (0, 0), (0, LANE - QK_ROPE)))
    return jnp.concatenate([w3[:, :, :QK_NOPE].reshape(k, -1), rope.reshape(k, -1)], axis=1)


def _uq_unlayout(w, heads):
    k = w.shape[0]
    nope = w[:, :heads * QK_NOPE].reshape(k, heads, QK_NOPE)
    rope = w[:, heads * QK_NOPE:].reshape(k, heads, LANE)[:, :, :QK_ROPE]
    return jnp.concatenate([nope, rope], axis=2).reshape(k, -1)


def _heads_merge(w, heads, first):
    k = w.shape[0]
    a, b = w[:, :heads * first].reshape(k, heads, first), w[:, heads * first:].reshape(k, heads, -1)
    return jnp.concatenate([a, b], axis=2).reshape(k, -1)


def _cols_full(w8):
    return w8.transpose(1, 0, 2).reshape(w8.shape[1], -1)


def _cols_slots(w):
    return w.reshape(w.shape[0], N_DEV, -1).transpose(1, 0, 2)


def _weight_layout(n, w8):
    if n in ROW_SHARDED:
        return w8.reshape(-1, w8.shape[-1])
    return _cols_full(w8) if (n in RELAID or w8.shape[-1] % LANE) else w8


def _grad_slots(n, g):
    if g.ndim == 3:
        return g
    return g.reshape(N_DEV, g.shape[0] // N_DEV, g.shape[1]) if n in ROW_SHARDED else _cols_slots(g)


def _model_weights(g8, D):
    W = {n: _weight_layout(n, w8) for n, w8 in g8.items()}
    w_in = W["w_in"]
    n_front = w_in.shape[1] - 2 * D
    goff = -(-n_front // D) * D
    W["w_in"] = jnp.concatenate([w_in[:, :n_front], jnp.zeros((D, goff - n_front), w_in.dtype), w_in[:, n_front:]], axis=1)
    heads = W["w_uq"].shape[1] // (QK_NOPE + QK_ROPE)
    W["w_uq"] = _uq_layout(W["w_uq"], heads)
    W["w_ukv"] = _heads_split(W["w_ukv"], heads, QK_NOPE)
    return W, goff


def kernel(x, c, ctx, c_ctx, w_mod, b_mod, norm1, norm2, w_in, s5_a_re, s5_a_im, s5_log_dt, s5_b_re, s5_b_im, s5_c_re, s5_c_im, s5_d, w_glu, q_norm, kv_norm, w_uq, w_ukv, w_mla_o, w_out, w_ffn_in, w_ffn_out, norm_f, loss_target, m_c_ctx, m_w_mod, m_b_mod, m_norm1, m_norm2, m_w_in, m_s5_a_re, m_s5_a_im, m_s5_log_dt, m_s5_b_re, m_s5_b_im, m_s5_c_re, m_s5_c_im, m_s5_d, m_w_glu, m_q_norm, m_kv_norm, m_w_uq, m_w_ukv, m_w_mla_o, m_w_out, m_w_ffn_in, m_w_ffn_out, m_norm_f, v_c_ctx, v_w_mod, v_b_mod, v_norm1, v_norm2, v_w_in, v_s5_a_re, v_s5_a_im, v_s5_log_dt, v_s5_b_re, v_s5_b_im, v_s5_c_re, v_s5_c_im, v_s5_d, v_w_glu, v_q_norm, v_kv_norm, v_w_uq, v_w_ukv, v_w_mla_o, v_w_out, v_w_ffn_in, v_w_ffn_out, v_norm_f):
    a = dict(locals())
    D = x.shape[-1]
    me = 4 * lax.axis_index("x") + 2 * lax.axis_index("y") + lax.axis_index("c")

    shard = {n: a[n][0] for n in BIG}
    first = [n for n in BIG if n not in FFN + MIX]
    gathered = _all_gather([shard[n].astype(BF16) for n in first] + [jnp.broadcast_to(c, (8, D))], "ag_weights")
    W, goff = _model_weights(dict(zip(first, gathered[:-1])), D)
    cg = gathered[-1]

    wm = w_mod[0]
    ncol = wm.shape[1]
    c16 = jnp.concatenate([cg[:, 0, :], c_ctx[None], jnp.zeros((7, D), F32)], axis=0)
    (s16,) = _rowmap(jax.nn.silu, "mod_silu", 16, [c16], [], [(D, BF16)])
    m_cols = _mm(s16, wm, "nn", F32, "mm_mod")
    (mg,) = _all_gather([m_cols], "ag_mod")
    (m16,) = _rowmap(lambda m, b: m + b, "mod_bias", 16, [_cols_full(mg)], [b_mod], [(N_DEV * ncol, F32)])

    mix_blocks = [shard[n].astype(BF16) for n in MIX]
    mix = _xchg_start(mix_blocks, [_own_slot(b, me) for b in mix_blocks], False, "ag_mix_start")
    ffn_blocks = [shard[n].astype(BF16) for n in FFN]
    ffn = {}
    ffn["sems1"], ffn["thru"], ag_token = _ag2_start(ffn_blocks, [_own_slot(b, me) for b in ffn_blocks], "ag_ffn_start")
    m16 = m16 + (mix[3][:1, :1] + ag_token[:1, :1])

    def mix_weights(after):
        lands = _xchg_wait(mix[0], mix[1], mix[2], after, False, "ag_mix_wait")
        return {n: _weight_layout(n, w8) for n, w8 in zip(MIX, lands)}

    def ffn_mid(after):
        ffn["sems2"], ffn["thru"], token = _ag2_mid(ffn["sems1"], ffn["thru"], after, "ag_ffn_mid")
        return token

    def ffn_weights(after):
        lands = _ag2_end(ffn["sems1"], ffn["sems2"], ffn["thru"], after, "ag_ffn_end")
        return {n: _weight_layout(n, w8) for n, w8 in zip(FFN, lands)}

    rs_async = {}

    def send_grads(names, gs):
        slots = [_grad_slots(n, g) for n, g in zip(names, gs)]
        lands = [_own_slot(lax.dynamic_index_in_dim(s, me, 0, keepdims=False), me) for s in slots]
        rs_async[names] = _xchg_start(slots, lands, True, "rs_start_" + names[0])
        return rs_async[names][3]

    m_lat = lax.dynamic_slice(m16, (me, 0), (1, 6 * D)).reshape(6, D)
    m_ctx = m16[8].reshape(6, D)

    p = {n: a[n][0] for n in ("norm1", "norm2", "s5_a_re", "s5_a_im", "s5_log_dt", "s5_b_re", "s5_b_im", "s5_c_re",
                              "s5_c_im", "q_norm", "kv_norm")}
    p = {k: (v[None] if v.ndim == 1 else v) for k, v in p.items()}
    p["s5_d"] = s5_d.reshape(1, -1)
    p["norm_f"] = norm_f[None]
    loss_part, grad_x, small, dm_lat, dm_ctx, gW = _local_step(x[0], ctx[0], loss_target[0], m_lat, m_ctx, p, W, goff,
                                                               dict(mix_weights=mix_weights, ffn_mid=ffn_mid,
                                                                    ffn_weights=ffn_weights, send_grads=send_grads))
    loss = lax.psum(loss_part[0, 0], ("x", "y", "c"))

    dm16 = jnp.concatenate([dm_lat.reshape(1, -1), dm_ctx.reshape(1, -1), jnp.zeros((14, 6 * D), F32)], axis=0)
    (dmg,) = _all_gather([dm16], "ag_dmod")
    dm_sum = _sum_slots(dmg, "sum_dmod")
    dM16 = jnp.concatenate([dmg[:, 0, :], dm_sum[1:2], jnp.zeros((7, 6 * D), F32)], axis=0)
    (g_b_mod,) = _rowmap(lambda d: jnp.sum(d, axis=0, keepdims=True), "b_mod_grad", 16, [dM16], [], [], [6 * D])
    dM_loc = lax.dynamic_slice(dM16, (0, me * ncol), (16, ncol))
    g_w_mod = _mm(s16, dM_loc, "tn", F32, "mm_mod_dw")
    ds16_part = _mm(dM_loc, wm, "nt", F32, "mm_mod_dx")

    fine = [n for n in SMALL if n not in ("c_ctx", "b_mod") + S5_BULK]
    sg, sgb = _all_gather([_pack_rows([small[n] for n in fine] + [ds16_part[8:9]], F32),
                           _pack_rows([small[n] for n in S5_BULK], BF16)], "ag_small")
    parts = _unpack_rows(_sum_slots(sg, "sum_small"), [small[n].shape for n in fine] + [(1, D)])
    grads = dict(zip(fine, parts[:-1]))
    grads.update(zip(S5_BULK, _unpack_rows(_sum_slots(sgb, "sum_small_bulk"), [small[n].shape for n in S5_BULK])))

    def silu_bwd(cc, ds):
        _, vjp = jax.vjp(jax.nn.silu, cc)
        return vjp(ds)[0]

    (g_c_ctx,) = _rowmap(silu_bwd, "c_ctx_grad", 1, [c_ctx[None], parts[-1]], [], [(D, F32)])
    grads["c_ctx"], grads["b_mod"] = g_c_ctx, g_b_mod

    gW = dict(gW)
    n_front = w_in.shape[-1] * N_DEV - 2 * D
    gW["w_in"] = jnp.concatenate([gW["w_in"][:, :n_front], gW["w_in"][:, goff:]], axis=1)
    heads = gW["w_uq"].shape[1] // (2 * LANE)
    gW["w_uq"] = _uq_unlayout(gW["w_uq"], heads)
    gW["w_ukv"] = _heads_merge(gW["w_ukv"], heads, QK_NOPE)
    last = [n for n in BIG if n in gW]
    slots = [_grad_slots(n, gW[n]) for n in last]
    from_sibling = _rs_pair(slots, "rs_pair")
    chip_sums = [_add_pair(pp, rr, "rs_add_" + n) for n, pp, rr in zip(last, slots, from_sibling)]
    my_chip = 2 * lax.axis_index("x") + lax.axis_index("y")
    lands = [_own_slot(lax.dynamic_index_in_dim(q, my_chip, 0, keepdims=False), my_chip, N_CHIP) for q in chip_sums]
    rs_send, rs_recv, rs_thru, behind = _xchg_start(chip_sums, lands, True, "rs_chips_start")
    for names, (send, recv, thru, _) in rs_async.items():
        for n, g8 in zip(names, _xchg_wait(send, recv, thru, behind, True, "rs_wait_" + names[0])):
            grads[n] = _sum_slots(g8, "rs_sum_" + n)
    grads["w_mod"] = g_w_mod

    out = {}

    def adamw_big(n, after):
        d, nm, nv = _adamw(a[n][0], grads[n], a["m_" + n][0], a["v_" + n][0], "adamw_" + n, after)
        for k, val in (("grad_", grads[n]), ("delta_", d), ("new_m_", nm), ("new_v_", nv)):
            out[k + n] = val.reshape(a[n].shape)
        return nv

    for n in FFN + MIX + ("w_mod",):
        behind = adamw_big(n, behind)
    packs = [_pack_rows([t[n] for n in SMALL], F32) for t in (
        {n: a[n] for n in SMALL}, {n: grads[n] for n in SMALL}, {n: a["m_" + n] for n in SMALL}, {n: a["v_" + n] for n in SMALL})]
    res = _adamw(*packs, "adamw_small", behind)
    for n, g4 in zip(last, _xchg_wait(rs_send, rs_recv, rs_thru, res[2], True, "rs_chips_wait")):
        grads[n] = _sum_slots(g4, "rs_sum_" + n)
        adamw_big(n, None)
    shapes = [a[n].shape for n in SMALL]
    for k, packed in (("grad_", packs[1]), ("delta_", res[0]), ("new_m_", res[1]), ("new_v_", res[2])):
        for n, val in zip(SMALL, _unpack_rows(packed, shapes)):
            out[k + n] = val
    return (loss, grad_x[None]) + tuple(out[k + n] for k in ("grad_", "delta_", "new_m_", "new_v_") for n in WEIGHTS)
```

```python
import functools
import math

import jax
import jax.numpy as jnp
from jax import lax
from jax.experimental import pallas as pl
from jax.experimental.pallas import tpu as pltpu

F32 = jnp.float32
BF16 = jnp.bfloat16

N_DEV = 8
N_CHIP = 4
EPS = 1e-6
GRID_W = 64
S5_GROUP = 16
QK_NOPE, QK_ROPE, V_DIM = 128, 64, 128
ROPE_BASE = 10000.0
ADAM_LR, ADAM_B1, ADAM_B2, ADAM_EPS, ADAM_WD, ADAM_STEP = 0.001, 0.9, 0.999, 1e-08, 0.01, 10

LANE = 128
SUB = 8
PACK_W = 1024
PACK_ROWS = 32
VMEM_LIMIT = 48 << 20
ROWMAP_TILE_BYTES = 20 << 20
MM_VMEM_BUDGET = 36 << 20
MESH = pl.DeviceIdType.MESH
_NT = (((1,), (1,)), ((), ()))
_TN = (((0,), (0,)), ((), ()))


def _pick(dim, cands):
    for c in cands:
        if dim % c == 0:
            return c
    return dim


def _cparams(sem):
    return pltpu.CompilerParams(dimension_semantics=sem, vmem_limit_bytes=VMEM_LIMIT)


def _mm(a, b, dims, out_dtype, name, out_slots=None):
    a = a.astype(BF16)
    b = b.astype(BF16)
    b3 = b.ndim == 3
    if dims == "nn":
        (M, K), N = a.shape, (b.shape[0] * b.shape[2] if b3 else b.shape[1])
    elif dims == "nt":
        M, N = a.shape[0], b.shape[-2]
        K = b.shape[0] * b.shape[2] if b3 else b.shape[1]
    else:
        (K, M), N = a.shape, b.shape[1]
    unit_n = b.shape[2] if (b3 and dims == "nn") else (N // out_slots if out_slots else N)
    unit_k = b.shape[2] if (b3 and dims == "nt") else K
    osz = jnp.dtype(out_dtype).itemsize
    tm, tn, tk = _mm_tiles(M, unit_n, unit_k, osz, LANE if dims == "tn" else 16)
    nk, npt, kpt = K // tk, unit_n // tn, unit_k // tk
    use_acc = nk > 1 and out_dtype != F32
    if dims == "nn":
        a_spec = pl.BlockSpec((tm, tk), lambda i, j, k: (i, k))
        b_spec = (pl.BlockSpec((None, tk, tn), lambda i, j, k: (j // npt, k, j % npt)) if b3
                  else pl.BlockSpec((tk, tn), lambda i, j, k: (k, j)))
        dn = (((1,), (0,)), ((), ()))
    elif dims == "nt":
        a_spec = pl.BlockSpec((tm, tk), lambda i, j, k: (i, k))
        b_spec = (pl.BlockSpec((None, tn, tk), lambda i, j, k: (k // kpt, j, k % kpt)) if b3
                  else pl.BlockSpec((tn, tk), lambda i, j, k: (j, k)))
        dn = _NT
    else:
        a_spec = pl.BlockSpec((tk, tm), lambda i, j, k: (k, i))
        b_spec = pl.BlockSpec((tk, tn), lambda i, j, k: (k, j))
        dn = _TN
    if out_slots:
        out_spec = pl.BlockSpec((None, tm, tn), lambda i, j, k: (j // npt, i, j % npt))
        out_shape = jax.ShapeDtypeStruct((out_slots, M, unit_n), out_dtype)
    else:
        out_spec = pl.BlockSpec((tm, tn), lambda i, j, k: (i, j))
        out_shape = jax.ShapeDtypeStruct((M, N), out_dtype)

    def body(a_ref, b_ref, o_ref, *scratch):
        part = lax.dot_general(a_ref[...], b_ref[...], dn, preferred_element_type=F32)
        if nk == 1:
            o_ref[...] = part.astype(o_ref.dtype)
            return
        acc_ref = scratch[0] if use_acc else o_ref
        k = pl.program_id(2)

        @pl.when(k == 0)
        def _():
            acc_ref[...] = part

        @pl.when(k > 0)
        def _():
            acc_ref[...] += part

        if use_acc:
            @pl.when(k == nk - 1)
            def _():
                o_ref[...] = acc_ref[...].astype(o_ref.dtype)

    return pl.pallas_call(
        body, name=name, grid=(M // tm, N // tn, nk),
        in_specs=[a_spec, b_spec], out_specs=out_spec, out_shape=out_shape,
        scratch_shapes=[pltpu.VMEM((tm, tn), F32)] if use_acc else [],
        compiler_params=_cparams(("parallel", "parallel", "arbitrary")),
    )(a, b)


def _divisors(n, mult, cap):
    d = [t for t in range(mult, min(n, cap) + 1, mult) if n % t == 0]
    return d[::-1] or [n]


def _mm_tiles(M, unit_n, unit_k, out_itemsize, tm_mult):
    best = None
    for tk in _divisors(unit_k, LANE, 2816):
        for tn in _divisors(unit_n, LANE, 1536):
            for tm in _divisors(M, tm_mult, 1024):
                vmem = 2 * 2 * (tm * tk + tk * tn) + 2 * tm * tn * out_itemsize + 4 * tm * tn * (2 if unit_k > tk else 1)
                if vmem > MM_VMEM_BUDGET:
                    continue
                steps = (M // tm) * (unit_n // tn) * (unit_k // tk)
                key = (steps, -tk, -tn)
                if best is None or key < best[0]:
                    best = (key, (tm, tn, tk))
                break
    return best[1]


def _rowmap(fn, name, M, row_ins, bc_ins, row_outs, acc_outs=(), after=None):
    row_ins = [r if isinstance(r, tuple) else (r, r.shape[1], 0) for r in row_ins]
    row_bytes = sum(w * a.dtype.itemsize for a, w, _ in row_ins) + sum(w * jnp.dtype(d).itemsize for w, d in row_outs)
    widest = max([w for _, w, _ in row_ins] + [w for w, _ in row_outs])
    row_bytes = 2 * row_bytes + 6 * 4 * widest
    tm = _pick(M, [t for t in (512, 256, 128, 64, 32, 16) if t * row_bytes <= ROWMAP_TILE_BYTES] + [16])
    n_in, n_row, n_acc = len(row_ins) + len(bc_ins), len(row_outs), len(acc_outs)

    def body(*refs):
        res = fn(*[r[...].astype(F32) for r in refs[:n_in]])
        res = res if isinstance(res, (tuple, list)) else (res,)
        outs = refs[n_in + (after is not None):]
        for k in range(n_row):
            outs[k][...] = res[k].astype(outs[k].dtype)
        if n_acc:
            @pl.when(pl.program_id(0) == 0)
            def _():
                for k in range(n_acc):
                    outs[n_row + k][...] = jnp.zeros_like(outs[n_row + k])

            for k in range(n_acc):
                outs[n_row + k][...] += res[n_row + k].astype(F32)

    in_specs = [pl.BlockSpec((tm, w), functools.partial(lambda i, blk: (i, blk), blk=blk)) for _, w, blk in row_ins]
    in_specs += [pl.BlockSpec(b.shape, lambda i: (0, 0)) for b in bc_ins]
    in_specs += [pl.BlockSpec(memory_space=pl.ANY)] * (after is not None)
    out_specs = [pl.BlockSpec((tm, w), lambda i: (i, 0)) for w, _ in row_outs]
    out_specs += [pl.BlockSpec((1, w), lambda i: (0, 0)) for w in acc_outs]
    out_shape = [jax.ShapeDtypeStruct((M, w), d) for w, d in row_outs]
    out_shape += [jax.ShapeDtypeStruct((1, w), F32) for w in acc_outs]
    return pl.pallas_call(
        body, name=name, grid=(M // tm,), in_specs=in_specs, out_specs=out_specs, out_shape=out_shape,
        compiler_params=_cparams(("arbitrary",) if n_acc else ("parallel",)),
    )(*[a for a, _, _ in row_ins], *bc_ins, *([] if after is None else [after]))


def _rms(x, g):
    return x * lax.rsqrt(jnp.mean(x * x, axis=-1, keepdims=True) + EPS) * g


def _normmod(x, g, sc, sh):
    return _rms(x, g) * (1.0 + sc) + sh


def _swap16(v):
    w = v.shape[1]
    lane = lax.broadcasted_iota(jnp.int32, v.shape, 1)
    return jnp.where((lane // 16) % 2 == 0, pltpu.roll(v, w - 16, 1), pltpu.roll(v, 16, 1))


def _rope(v, cos, sin_signed):
    return v * cos + _swap16(v) * sin_signed


def _rope_bwd(d, cos, sin_signed):
    return d * cos + _swap16(d * sin_signed)


def _mesh_pos():
    return lax.axis_index("x"), lax.axis_index("y"), lax.axis_index("c")


def _hbm_call(body, name, ins, out_shapes, n_sems):
    any_spec = pl.BlockSpec(memory_space=pl.ANY)
    return pl.pallas_call(
        body, name=name, out_shape=out_shapes, in_specs=[any_spec] * len(ins), out_specs=[any_spec] * len(out_shapes),
        scratch_shapes=[pltpu.SemaphoreType.DMA((n_sems,)), pltpu.SemaphoreType.DMA((n_sems,)),
                        pltpu.SemaphoreType.DMA((len(ins),))],
    )(*ins)


def _all_gather(xs, name):
    n = len(xs)

    def body(*refs):
        x_refs, out_refs, (send_sems, recv_sems, local_sems) = refs[:n], refs[n:2 * n], refs[2 * n:]
        x, y, c = _mesh_pos()
        me, sibling = (x, y, c), (x, y, 1 - c)
        chips = [(1 - x, y), (x, 1 - y), (1 - x, 1 - y)]
        locals_, first, passed, arrivals = [], [], [], []
        for a in range(n):
            def slot(px, py, pc, a=a):
                return out_refs[a].at[4 * px + 2 * py + pc]

            def copy(k, block, to, src=None, a=a, slot=slot):
                return pltpu.make_async_remote_copy(
                    src_ref=slot(*block) if src is None else src, dst_ref=slot(*block),
                    send_sem=send_sems.at[7 * a + k], recv_sem=recv_sems.at[7 * a + k], device_id=to, device_id_type=MESH)

            locals_.append(pltpu.make_async_copy(x_refs[a], slot(*me), local_sems.at[a]))
            first.append(copy(0, me, sibling, src=x_refs[a]))
            first += [copy(1 + j, me, (*chip, c), src=x_refs[a]) for j, chip in enumerate(chips)]
            passed.append([copy(4 + j, (*chip, c), sibling) for j, chip in enumerate(chips)])
            arrivals.append([copy(1 + j, (*chip, c), me) for j, chip in enumerate(chips)]
                            + [copy(0, sibling, me)] + [copy(4 + j, (*chip, 1 - c), me) for j, chip in enumerate(chips)])
        for cp in locals_ + first:
            cp.start()
        for j in range(3):
            for a in range(n):
                arrivals[a][j].wait_recv()
                passed[a][j].start()
        for a in range(n):
            for cp in arrivals[a][3:]:
                cp.wait_recv()
        for cp in first + [p for ps in passed for p in ps]:
            cp.wait_send()
        for cp in locals_:
            cp.wait()

    return _hbm_call(body, name, xs, [jax.ShapeDtypeStruct((N_DEV,) + x.shape, x.dtype) for x in xs], 7 * n)


def _rs_pair(ps, name):
    n = len(ps)

    def body(*refs):
        p_refs, out_refs, (send_sems, recv_sems, _) = refs[:n], refs[n:2 * n], refs[2 * n:]
        x, y, c = _mesh_pos()
        sends, recvs = [], []
        for a in range(n):
            for q in range(N_CHIP):
                sem = dict(send_sem=send_sems.at[4 * a + q], recv_sem=recv_sems.at[4 * a + q],
                           device_id=(x, y, 1 - c), device_id_type=MESH)
                sends.append(pltpu.make_async_remote_copy(src_ref=p_refs[a].at[2 * q + 1 - c], dst_ref=out_refs[a].at[q], **sem))
                recvs.append(pltpu.make_async_remote_copy(src_ref=p_refs[a].at[2 * q + c], dst_ref=out_refs[a].at[q], **sem))
        for cp in sends:
            cp.start()
        for cp in recvs:
            cp.wait_recv()
        for cp in sends:
            cp.wait_send()

    return _hbm_call(body, name, ps, [jax.ShapeDtypeStruct((N_CHIP,) + p.shape[1:], p.dtype) for p in ps], 4 * n)


def _xchg_copies(src_refs, land_refs, send_sems, recv_sems, slot_src):
    x, y, c = _mesh_pos()
    sends, recvs = [], []
    for a, (src, land) in enumerate(zip(src_refs, land_refs)):
        chips = land.shape[0] == N_CHIP
        npeer = land.shape[0] - 1
        me = 2 * x + y if chips else 4 * x + 2 * y + c
        for r in range(1, npeer + 1):
            px = 1 - x if r & (2 if chips else 4) else x
            py = 1 - y if r & (1 if chips else 2) else y
            pc = c if chips else (1 - c if r & 1 else c)
            peer = 2 * px + py if chips else 4 * px + 2 * py + pc
            sem = dict(send_sem=send_sems.at[npeer * a + r - 1], recv_sem=recv_sems.at[npeer * a + r - 1],
                       device_id=(px, py, pc), device_id_type=MESH)
            s = src.at[peer] if slot_src else src
            sends.append(pltpu.make_async_remote_copy(src_ref=s, dst_ref=land.at[me], **sem))
            recvs.append(pltpu.make_async_remote_copy(src_ref=s, dst_ref=land.at[peer], **sem))
    return sends, recvs


_HBM = pl.BlockSpec(memory_space=pltpu.HBM)
_SEM = pl.BlockSpec(memory_space=pltpu.SEMAPHORE)
_EFFECT = pltpu.SideEffectType.DATAFLOW_SIDE_EFFECTING


def _xchg_start(srcs, lands, slot_src, name):
    n = len(srcs)

    def body(*refs):
        sends, _ = _xchg_copies(refs[:n], refs[n:2 * n], refs[2 * n], refs[2 * n + 1], slot_src)
        for cp in sends:
            cp.start()
        refs[-1][...] = jnp.zeros_like(refs[-1])

    bufs = list(srcs) + list(lands)
    n_sems = n * (lands[0].shape[0] - 1)
    res = pl.pallas_call(
        body, name=name,
        out_shape=(pltpu.SemaphoreType.DMA((n_sems,)), pltpu.SemaphoreType.DMA((n_sems,)))
        + tuple(pltpu.HBM(b.shape, b.dtype) for b in bufs) + (jax.ShapeDtypeStruct((SUB, LANE), F32),),
        in_specs=(_HBM,) * (2 * n), out_specs=(_SEM, _SEM) + (_HBM,) * (2 * n) + (pl.BlockSpec(memory_space=pltpu.VMEM),),
        input_output_aliases={i: 2 + i for i in range(2 * n)},
        compiler_params=pltpu.CompilerParams(has_side_effects=_EFFECT),
    )(*[pltpu.with_memory_space_constraint(b, pltpu.HBM) for b in bufs])
    return res[0], res[1], res[2:-1], res[-1]


def _xchg_wait(send_sems, recv_sems, thru, after, slot_src, name):
    n = len(thru) // 2

    def body(*refs):
        sends, recvs = _xchg_copies(refs[:n], refs[n:2 * n], refs[2 * n], refs[2 * n + 1], slot_src)
        for cp in sends:
            cp.wait_send()
        for cp in recvs:
            cp.wait_recv()

    res = pl.pallas_call(
        body, name=name, out_shape=tuple(pltpu.HBM(b.shape, b.dtype) for b in thru),
        in_specs=(_HBM,) * (2 * n) + (_SEM, _SEM, pl.BlockSpec(memory_space=pl.ANY)), out_specs=(_HBM,) * (2 * n),
        input_output_aliases={i: i for i in range(2 * n)},
        compiler_params=pltpu.CompilerParams(has_side_effects=_EFFECT),
    )(*thru, send_sems, recv_sems, after)
    return res[n:]


def _ag2_copy(land, sems, k, block, to, src=None):
    slot = land.at[4 * block[0] + 2 * block[1] + block[2]]
    return pltpu.make_async_remote_copy(src_ref=slot if src is None else src, dst_ref=slot, send_sem=sems[0].at[k],
                                        recv_sem=sems[1].at[k], device_id=to, device_id_type=MESH)


def _ag2_start(blocks, lands, name):
    n = len(blocks)

    def body(*refs):
        x, y, c = _mesh_pos()
        for a in range(n):
            sems = (refs[2 * n], refs[2 * n + 1])
            _ag2_copy(refs[n + a], sems, 4 * a, (x, y, c), (x, y, 1 - c), src=refs[a]).start()
            for j, chip in enumerate([(1 - x, y), (x, 1 - y), (1 - x, 1 - y)]):
                _ag2_copy(refs[n + a], sems, 4 * a + 1 + j, (x, y, c), (*chip, c), src=refs[a]).start()
        refs[-1][...] = jnp.zeros_like(refs[-1])

    bufs = list(blocks) + list(lands)
    res = pl.pallas_call(
        body, name=name,
        out_shape=(pltpu.SemaphoreType.DMA((4 * n,)), pltpu.SemaphoreType.DMA((4 * n,)))
        + tuple(pltpu.HBM(b.shape, b.dtype) for b in bufs) + (jax.ShapeDtypeStruct((SUB, LANE), F32),),
        in_specs=(_HBM,) * (2 * n), out_specs=(_SEM, _SEM) + (_HBM,) * (2 * n) + (pl.BlockSpec(memory_space=pltpu.VMEM),),
        input_output_aliases={i: 2 + i for i in range(2 * n)},
        compiler_params=pltpu.CompilerParams(has_side_effects=_EFFECT),
    )(*[pltpu.with_memory_space_constraint(b, pltpu.HBM) for b in bufs])
    return (res[0], res[1]), res[2:-1], res[-1]


def _ag2_mid(sems1, thru, after, name):
    n = len(thru) // 2

    def body(*refs):
        x, y, c = _mesh_pos()
        s1, s2 = (refs[2 * n], refs[2 * n + 1]), (refs[2 * n + 3], refs[2 * n + 4])
        for j, chip in enumerate([(1 - x, y), (x, 1 - y), (1 - x, 1 - y)]):
            for a in range(n):
                _ag2_copy(refs[n + a], s1, 4 * a + 1 + j, (*chip, c), (x, y, c)).wait_recv()
                _ag2_copy(refs[n + a], s2, 3 * a + j, (*chip, c), (x, y, 1 - c)).start()
        refs[-1][...] = jnp.zeros_like(refs[-1])

    res = pl.pallas_call(
        body, name=name,
        out_shape=(pltpu.SemaphoreType.DMA((3 * n,)), pltpu.SemaphoreType.DMA((3 * n,)))
        + tuple(pltpu.HBM(b.shape, b.dtype) for b in thru) + (jax.ShapeDtypeStruct((SUB, LANE), F32),),
        in_specs=(_HBM,) * (2 * n) + (_SEM, _SEM, pl.BlockSpec(memory_space=pl.ANY)),
        out_specs=(_SEM, _SEM) + (_HBM,) * (2 * n) + (pl.BlockSpec(memory_space=pltpu.VMEM),),
        input_output_aliases={i: 2 + i for i in range(2 * n)},
        compiler_params=pltpu.CompilerParams(has_side_effects=_EFFECT),
    )(*thru, *sems1, after)
    return (res[0], res[1]), res[2:-1], res[-1]


def _ag2_end(sems1, sems2, thru, after, name):
    n = len(thru) // 2

    def body(*refs):
        x, y, c = _mesh_pos()
        s1, s2 = (refs[2 * n], refs[2 * n + 1]), (refs[2 * n + 2], refs[2 * n + 3])
        chips = [(1 - x, y), (x, 1 - y), (1 - x, 1 - y)]
        for a in range(n):
            land = refs[n + a]
            _ag2_copy(land, s1, 4 * a, (x, y, c), (x, y, 1 - c), src=refs[a]).wait_send()
            _ag2_copy(land, s1, 4 * a, (x, y, 1 - c), (x, y, c)).wait_recv()
            for j, chip in enumerate(chips):
                _ag2_copy(land, s1, 4 * a + 1 + j, (x, y, c), (*chip, c), src=refs[a]).wait_send()
                _ag2_copy(land, s2, 3 * a + j, (*chip, c), (x, y, 1 - c)).wait_send()
                _ag2_copy(land, s2, 3 * a + j, (*chip, 1 - c), (x, y, c)).wait_recv()

    res = pl.pallas_call(
        body, name=name, out_shape=tuple(pltpu.HBM(b.shape, b.dtype) for b in thru),
        in_specs=(_HBM,) * (2 * n) + (_SEM,) * 4 + (pl.BlockSpec(memory_space=pl.ANY),), out_specs=(_HBM,) * (2 * n),
        input_output_aliases={i: i for i in range(2 * n)},
        compiler_params=pltpu.CompilerParams(has_side_effects=_EFFECT),
    )(*thru, *sems1, *sems2, after)
    return res[n:]


def _own_slot(block, me, slots=N_DEV):
    return lax.dynamic_update_slice(lax.empty((slots,) + block.shape, block.dtype), block[None], (me, 0, 0))


def _add_pair(p, r, name):
    _, R, C = p.shape
    tr = _pick(R, (512, 256, 128, 64, 32, 16))

    def body(c_ref, p_ref, r_ref, o_ref):
        o_ref[...] = (p_ref[...].astype(F32) + r_ref[...].astype(F32)).astype(o_ref.dtype)

    return pl.pallas_call(
        body, name=name, out_shape=jax.ShapeDtypeStruct((N_CHIP, R, C), p.dtype),
        grid_spec=pltpu.PrefetchScalarGridSpec(
            num_scalar_prefetch=1, grid=(N_CHIP, R // tr),
            in_specs=[pl.BlockSpec((None, None, tr, C), lambda q, i, c_ref: (q, c_ref[0], i, 0)),
                      pl.BlockSpec((None, tr, C), lambda q, i, c_ref: (q, i, 0))],
            out_specs=pl.BlockSpec((None, tr, C), lambda q, i, c_ref: (q, i, 0))),
        compiler_params=_cparams(("parallel", "parallel")),
    )(lax.axis_index("c").reshape(1).astype(jnp.int32), p.reshape(N_CHIP, 2, R, C), r)


def _sum_slots(g, name):
    ns, R, C = g.shape
    tr = _pick(R, (256, 128, 64, 32, 16))

    def body(g_ref, o_ref):
        acc = g_ref[0].astype(F32)
        for j in range(1, ns):
            acc = acc + g_ref[j].astype(F32)
        o_ref[...] = acc

    return pl.pallas_call(
        body, name=name, grid=(R // tr,),
        in_specs=[pl.BlockSpec((ns, tr, C), lambda i: (0, i, 0))], out_specs=pl.BlockSpec((tr, C), lambda i: (i, 0)),
        out_shape=jax.ShapeDtypeStruct((R, C), F32), compiler_params=_cparams(("parallel",)),
    )(g)


def _pack_rows(arrs, dtype):
    parts = []
    for a in arrs:
        flat = a.reshape(-1).astype(dtype)
        pad = (-flat.shape[0]) % (PACK_W * 16)
        parts.append(jnp.pad(flat, (0, pad)).reshape(-1, PACK_W))
    out = jnp.concatenate(parts, axis=0)
    return jnp.pad(out, ((0, (-out.shape[0]) % PACK_ROWS), (0, 0)))


def _packed_rows(shape):
    n = math.prod(shape)
    return (n + PACK_W * 16 - 1) // (PACK_W * 16) * 16


def _unpack_rows(packed, shapes):
    out, r0 = [], 0
    for s in shapes:
        rows, n = _packed_rows(s), math.prod(s)
        out.append(packed[r0:r0 + rows].reshape(rows * PACK_W)[:n].reshape(s))
        r0 += rows
    return out


def _adamw_math(w, g, m, v):
    m = ADAM_B1 * m + (1.0 - ADAM_B1) * g
    v = ADAM_B2 * v + (1.0 - ADAM_B2) * (g * g)
    m_hat = m / (1.0 - ADAM_B1 ** ADAM_STEP)
    v_hat = v / (1.0 - ADAM_B2 ** ADAM_STEP)
    delta = -ADAM_LR * (m_hat / (jnp.sqrt(v_hat) + ADAM_EPS) + ADAM_WD * w)
    return delta, m, v


def _adamw(w, g, m, v, name, after=None):
    R, C = w.shape
    return _rowmap(_adamw_math, name, R, [w, g, m, v], [], [(C, F32)] * 3, after=after)


def _s5_disc_math(lr, li, ldt, br, bi):
    dt = jnp.exp(ldt)
    mag = jnp.exp(lr * dt)
    ab_re, ab_im = mag * jnp.cos(li * dt), mag * jnp.sin(li * dt)
    den = lr * lr + li * li
    nr, ni = ab_re - 1.0, ab_im
    co_re = (nr * lr + ni * li) / den
    co_im = (ni * lr - nr * li) / den
    bb_re = co_re * br - co_im * bi
    bb_im = co_re * bi + co_im * br
    return ab_re, ab_im, bb_re, bb_im


def _s5_tables(a_re, a_im, ldt, b_re, b_im, c_re, c_im):
    _, G, P, N = b_re.shape
    nch = G // 8

    def body(lr_ref, li_ref, ldt_ref, br_ref, bi_ref, cr_ref, ci_ref, wre, wim, vre, vim, pwr, pwi):
        ar, ai, bb_re, bb_im = _s5_disc_math(lr_ref[0], li_ref[0], ldt_ref[0], br_ref[0], bi_ref[0])
        cr, ci = cr_ref[0], ci_ref[0]
        pr, pi = jnp.ones_like(ar), jnp.zeros_like(ar)
        for j in range(SUB + 1):
            pwr[0, j], pwi[0, j] = pr, pi
            if j < SUB:
                tabs = ((wre, bb_re * pr - bb_im * pi), (wim, bb_re * pi + bb_im * pr),
                        (vre, cr * pr - ci * pi), (vim, -(cr * pi + ci * pr)))
                for ref, val in tabs:
                    for s in range(nch):
                        ref[0, s, pl.ds(j * LANE, LANE), :] = val[s * 8:(s + 1) * 8].reshape(LANE, N).astype(BF16)
            pr, pi = pr * ar - pi * ai, pr * ai + pi * ar

    g1n = pl.BlockSpec((1, G, 1, N), lambda d: (d, 0, 0, 0))
    gpn = pl.BlockSpec((1, G, P, N), lambda d: (d, 0, 0, 0))
    tab = pl.BlockSpec((1, nch, SUB * LANE, N), lambda d: (d, 0, 0, 0))
    pw = pl.BlockSpec((1, SUB + 1, G, 1, N), lambda d: (d, 0, 0, 0, 0))
    s_tab = jax.ShapeDtypeStruct((2, nch, SUB * LANE, N), BF16)
    s_pw = jax.ShapeDtypeStruct((2, SUB + 1, G, 1, N), F32)
    return pl.pallas_call(
        body, name="s5_tables", grid=(2,),
        in_specs=[g1n, g1n, pl.BlockSpec((1, G, 1, 1), lambda d: (d, 0, 0, 0)), gpn, gpn, gpn, gpn],
        out_specs=[tab] * 4 + [pw] * 2, out_shape=[s_tab] * 4 + [s_pw] * 2,
        compiler_params=_cparams(("parallel",)),
    )(a_re, a_im, ldt, b_re, b_im, c_re, c_im)


def _s5_expand(t_re, t_im, name):
    _, nch, R, N = t_re.shape
    sw = 8 * N

    def body(re_ref, im_ref, o_ref):
        spread = (lax.broadcasted_iota(jnp.int32, (N, sw), 1) % N == lax.broadcasted_iota(jnp.int32, (N, sw), 0)).astype(BF16)
        row_g = (lax.broadcasted_iota(jnp.int32, (R, sw), 0) % LANE) // S5_GROUP
        keep = row_g == lax.broadcasted_iota(jnp.int32, (R, sw), 1) // N
        for half, ref in enumerate((re_ref, im_ref)):
            t = jnp.dot(ref[0, 0], spread, preferred_element_type=F32)
            o_ref[0, 0, :, pl.ds(half * sw, sw)] = jnp.where(keep, t, 0.0).astype(BF16)

    spec = pl.BlockSpec((1, 1, R, N), lambda d, s: (d, s, 0, 0))
    return pl.pallas_call(
        body, name=name, grid=(2, nch), in_specs=[spec, spec],
        out_specs=pl.BlockSpec((1, 1, R, 2 * sw), lambda d, s: (d, s, 0, 0)),
        out_shape=jax.ShapeDtypeStruct((2, nch, R, 2 * sw), BF16), compiler_params=_cparams(("parallel", "parallel")),
    )(t_re, t_im)


def _s5_param_bwd(a_re, a_im, ldt, b_re, b_im, da_re, da_im, dbb_re, dbb_im):
    _, G, P, N = b_re.shape

    def body(lr_ref, li_ref, ldt_ref, br_ref, bi_ref, dar, dai, dbr, dbi, o_lr, o_li, o_ldt, o_br, o_bi):
        _, vjp = jax.vjp(_s5_disc_math, lr_ref[0], li_ref[0], ldt_ref[0], br_ref[0], bi_ref[0])
        o_lr[0], o_li[0], o_ldt[0], o_br[0], o_bi[0] = vjp((dar[0], dai[0], dbr[0], dbi[0]))

    g1n = pl.BlockSpec((1, G, 1, N), lambda d: (d, 0, 0, 0))
    g11 = pl.BlockSpec((1, G, 1, 1), lambda d: (d, 0, 0, 0))
    gpn = pl.BlockSpec((1, G, P, N), lambda d: (d, 0, 0, 0))
    s_g1n, s_g11, s_gpn = (jax.ShapeDtypeStruct(s, F32) for s in ((2, G, 1, N), (2, G, 1, 1), (2, G, P, N)))
    return pl.pallas_call(
        body, name="s5_param_bwd", grid=(2,),
        in_specs=[g1n, g1n, g11, gpn, gpn, g1n, g1n, gpn, gpn], out_specs=[g1n, g1n, g11, gpn, gpn],
        out_shape=[s_g1n, s_g1n, s_g11, s_gpn, s_gpn], compiler_params=_cparams(("parallel",)),
    )(a_re, a_im, ldt, b_re, b_im, da_re, da_im, dbb_re, dbb_im)


def _tile_local_scan(u, w_ref, back):
    tb, sw2 = u.shape[0], w_ref.shape[1]
    sw, half = sw2 // 2, LANE // 2
    tau = lax.broadcasted_iota(jnp.int32, u.shape, 0) % SUB
    low = lax.broadcasted_iota(jnp.int32, u.shape, 1) < half
    parts = [u]
    for j in range(1, SUB):
        if back:
            parts.append(jnp.where(tau >= j, pltpu.roll(u, j, 0), 0.0))
        else:
            parts.append(jnp.where(tau <= SUB - 1 - j, pltpu.roll(u, tb - j, 0), 0.0))
    out = [None] * 4
    for h in range(2):
        pieces = [jnp.where(low, a, pltpu.roll(b, half, 1)) if h == 0 else jnp.where(low, pltpu.roll(a, half, 1), b)
                  for a, b in zip(parts[0::2], parts[1::2])]
        lhs = jnp.concatenate(pieces, axis=1).astype(BF16)
        rows = jnp.concatenate([w_ref[pl.ds(j * LANE + h * half, half), :] for j in range(SUB)], axis=0)
        for part in range(2):
            cols = rows[:, part * sw + h * (sw // 2):part * sw + (h + 1) * (sw // 2)]
            out[2 * part + h] = jnp.dot(lhs, cols, preferred_element_type=F32)
    return jnp.concatenate(out, axis=1)


def _cmul_add(tile, pw, carry, sw):
    pr, pi, cr, ci = pw[:, :sw], pw[:, sw:], carry[:, :sw], carry[:, sw:]
    return tile + jnp.concatenate([pr * cr - pi * ci, pr * ci + pi * cr], axis=1)


def _tile_scan(buf, base, ntile, pw, carry, sw, causal):
    def step(k, c):
        i = k if causal else ntile - 1 - k
        r = pl.multiple_of(base + i * SUB, SUB)
        tile = _cmul_add(buf[pl.ds(r, SUB), :], pw, c, sw)
        buf[pl.ds(r, SUB), :] = tile
        return tile[SUB - 1:SUB, :] if causal else tile[0:1, :]

    return lax.fori_loop(0, ntile, step, carry)


def _s5_fwd(h_all, waug, vaug, pw, S5W, T, d, causal, name):
    S = h_all.shape[0]
    _, nch, _, sw2 = waug.shape
    sw = sw2 // 2
    tb = _pick(math.gcd(T, S - T), (256, 128, 64, 32, 16))
    ntile, nt, off = tb // SUB, S // tb, T // tb
    rb = (lambda s, t: ((t + off) % nt, s)) if causal else (lambda s, t: (nt - 1 - t, s))

    def body(u_ref, w_ref, v_ref, p_ref, y_ref, h_ref, hblk, carry):
        @pl.when(pl.program_id(1) == 0)
        def _():
            carry[...] = jnp.zeros_like(carry)

        hblk[...] = _tile_local_scan(u_ref[...], w_ref, causal)
        carry[...] = _tile_scan(hblk, 0, ntile, p_ref[...], carry[...], sw, causal)
        hb = hblk[...].astype(BF16)
        h_ref[...] = hb
        y_ref[...] = lax.dot_general(hb, v_ref[...], _NT, preferred_element_type=F32)

    return pl.pallas_call(
        body, name=name, grid=(nch, nt),
        in_specs=[pl.BlockSpec((tb, LANE), rb),
                  pl.BlockSpec((None, None, SUB * LANE, sw2), lambda s, t: (d, s, 0, 0)),
                  pl.BlockSpec((None, None, LANE, sw2), lambda s, t: (d, s, 0, 0)),
                  pl.BlockSpec((None, SUB, sw2), lambda s, t: (s, 0, 0))],
        out_specs=[pl.BlockSpec((tb, LANE), rb), pl.BlockSpec((tb, sw2), rb)],
        out_shape=[jax.ShapeDtypeStruct((S, S5W), F32), jax.ShapeDtypeStruct((S, nch * sw2), BF16)],
        scratch_shapes=[pltpu.VMEM((tb, sw2), F32), pltpu.VMEM((1, sw2), F32)],
        compiler_params=_cparams(("parallel", "arbitrary")),
    )(h_all, waug, vaug, pw)


def _s5_bwd(dy_all, h_all, hs, waug, vaug, pwc, S5W, T, d, causal, name):
    S = h_all.shape[0]
    _, nch, _, sw2 = waug.shape
    sw = sw2 // 2
    tb = _pick(math.gcd(T, S - T), (256, 128, 64, 32, 16))
    ntile, nt, off = tb // SUB, S // tb, T // tb
    rb = (lambda s, t: ((nt - 1 - t + off) % nt, s)) if causal else (lambda s, t: (t, s))
    adj_causal = not causal
    edge = SUB - 1 if adj_causal else SUB + tb
    keep_src, keep_dst = (tb, 0) if adj_causal else (SUB, SUB + tb)

    def body(dy_ref, u_ref, h_ref, w_ref, v_ref, p_ref, du_ref, dbb_ref, dc_ref, da_ref, lam):
        @pl.when(pl.program_id(1) == 0)
        def _():
            lam[pl.ds(0, SUB), :] = jnp.zeros((SUB, sw2), F32)
            lam[pl.ds(SUB + tb, SUB), :] = jnp.zeros((SUB, sw2), F32)
            dbb_ref[...] = jnp.zeros_like(dbb_ref)
            dc_ref[...] = jnp.zeros_like(dc_ref)
            da_ref[...] = jnp.zeros_like(da_ref)

        dy = dy_ref[...]
        lam[pl.ds(SUB, tb), :] = _tile_local_scan(dy, v_ref, adj_causal)
        _tile_scan(lam, SUB, ntile, p_ref[...], lam[pl.ds(edge, 1), :], sw, adj_causal)
        lb = lam[pl.ds(SUB, tb), :].astype(BF16)
        du_ref[...] = lax.dot_general(lb, w_ref[...], _NT, preferred_element_type=F32)
        dbb_ref[...] += lax.dot_general(u_ref[...].astype(BF16), lb, _TN, preferred_element_type=F32)
        dc_ref[...] += lax.dot_general(h_ref[...], dy.astype(BF16), _TN, preferred_element_type=F32)
        h = h_ref[...].astype(F32)
        ln = lam[pl.ds(SUB + 1 if causal else SUB - 1, tb), :]
        hr, hi, lr, li = h[:, :sw], h[:, sw:], ln[:, :sw], ln[:, sw:]
        da_ref[...] += jnp.concatenate([jnp.sum(hr * lr + hi * li, axis=0, keepdims=True),
                                        jnp.sum(hr * li - hi * lr, axis=0, keepdims=True)], axis=1)
        lam[pl.ds(keep_dst, SUB), :] = lam[pl.ds(keep_src, SUB), :]

    fixed = lambda s, t: (s, 0, 0)
    return pl.pallas_call(
        body, name=name, grid=(nch, nt),
        in_specs=[pl.BlockSpec((tb, LANE), rb), pl.BlockSpec((tb, LANE), rb), pl.BlockSpec((tb, sw2), rb),
                  pl.BlockSpec((None, None, LANE, sw2), lambda s, t: (d, s, 0, 0)),
                  pl.BlockSpec((None, None, SUB * LANE, sw2), lambda s, t: (d, s, 0, 0)),
                  pl.BlockSpec((None, SUB, sw2), fixed)],
        out_specs=[pl.BlockSpec((tb, LANE), rb), pl.BlockSpec((None, LANE, sw2), fixed),
                   pl.BlockSpec((None, sw2, LANE), fixed), pl.BlockSpec((None, 1, sw2), fixed)],
        out_shape=[jax.ShapeDtypeStruct((S, S5W), F32), jax.ShapeDtypeStruct((nch, LANE, sw2), F32),
                   jax.ShapeDtypeStruct((nch, sw2, LANE), F32), jax.ShapeDtypeStruct((nch, 1, sw2), F32)],
        scratch_shapes=[pltpu.VMEM((tb + 2 * SUB, sw2), F32)],
        compiler_params=_cparams(("parallel", "arbitrary")),
    )(dy_all, h_all, hs, waug, vaug, pwc)


def _attn_fwd(qn, qr, kv, kr, H, scale):
    T, S = qn.shape[0], kv.shape[0]
    tq = _pick(T, (256, 128, 64, 32, 16))

    def body(qn_ref, qr_ref, kn_ref, v_ref, kr_ref, o_ref, lse_ref):
        q = jnp.concatenate([qn_ref[...], qr_ref[...]], axis=1)
        k = jnp.concatenate([kn_ref[...], kr_ref[...]], axis=1)
        s = lax.dot_general(q, k, _NT, preferred_element_type=F32) * scale
        m = jnp.max(s, axis=1, keepdims=True)
        p = jnp.exp(s - m)
        l = jnp.sum(p, axis=1, keepdims=True)
        o_ref[...] = jnp.dot((p * (1.0 / l)).astype(BF16), v_ref[...], preferred_element_type=F32).astype(o_ref.dtype)
        lse_ref[0] = m + jnp.log(l)

    q_spec = pl.BlockSpec((tq, LANE), lambda h, i: (i, h))
    return pl.pallas_call(
        body, name="attn_fwd", grid=(H, T // tq),
        in_specs=[q_spec, q_spec, pl.BlockSpec((S, LANE), lambda h, i: (0, h)), pl.BlockSpec((S, LANE), lambda h, i: (0, H + h)),
                  pl.BlockSpec((S, LANE), lambda h, i: (0, 0))],
        out_specs=[q_spec, pl.BlockSpec((1, tq, 1), lambda h, i: (h, i, 0))],
        out_shape=[jax.ShapeDtypeStruct((T, H * LANE), BF16), jax.ShapeDtypeStruct((H, T, 1), F32)],
        compiler_params=_cparams(("parallel", "parallel")),
    )(qn, qr, kv, kv, kr)


def _attn_bwd(qn, qr, kv, kr, do, lse, H, scale):
    T, S = qn.shape[0], kv.shape[0]
    tq = _pick(T, (256, 128, 64, 32, 16))
    nq = T // tq

    def body(qn_ref, qr_ref, kn_ref, v_ref, kr_ref, do_ref, lse_ref, dqn_ref, dqr_ref, dkn_ref, dkr_ref, dv_ref, dk_acc, dv_acc):
        i = pl.program_id(1)
        q = jnp.concatenate([qn_ref[...], qr_ref[...]], axis=1)
        k = jnp.concatenate([kn_ref[...], kr_ref[...]], axis=1)
        v, d_o = v_ref[...], do_ref[...]
        s = lax.dot_general(q, k, _NT, preferred_element_type=F32) * scale
        p = jnp.exp(s - lse_ref[0])
        dv_part = lax.dot_general(p.astype(BF16), d_o, _TN, preferred_element_type=F32)
        dp = lax.dot_general(d_o, v, _NT, preferred_element_type=F32)
        ds = (p * (dp - jnp.sum(p * dp, axis=1, keepdims=True)) * scale).astype(BF16)
        dq = jnp.dot(ds, k, preferred_element_type=F32)
        dqn_ref[...] = dq[:, :LANE].astype(dqn_ref.dtype)
        dqr_ref[...] = dq[:, LANE:].astype(dqr_ref.dtype)
        dk_part = lax.dot_general(ds, q, _TN, preferred_element_type=F32)

        @pl.when(i == 0)
        def _():
            dk_acc[...] = dk_part
            dv_acc[...] = dv_part

        @pl.when(i > 0)
        def _():
            dk_acc[...] += dk_part
            dv_acc[...] += dv_part

        @pl.when(i == nq - 1)
        def _():
            dkn_ref[...] = dk_acc[:, :LANE].astype(dkn_ref.dtype)
            dkr_ref[...] = dk_acc[:, LANE:].astype(dkr_ref.dtype)
            dv_ref[...] = dv_acc[...].astype(dv_ref.dtype)

    q_spec = pl.BlockSpec((tq, LANE), lambda h, i: (i, h))
    k_spec = pl.BlockSpec((S, LANE), lambda h, i: (0, h))
    t_shape, s_shape = jax.ShapeDtypeStruct((T, H * LANE), BF16), jax.ShapeDtypeStruct((S, H * LANE), BF16)
    return pl.pallas_call(
        body, name="attn_bwd", grid=(H, nq),
        in_specs=[q_spec, q_spec, k_spec, pl.BlockSpec((S, LANE), lambda h, i: (0, H + h)),
                  pl.BlockSpec((S, LANE), lambda h, i: (0, 0)), q_spec, pl.BlockSpec((1, tq, 1), lambda h, i: (h, i, 0))],
        out_specs=[q_spec, q_spec, k_spec, k_spec, k_spec], out_shape=[t_shape, t_shape, s_shape, s_shape, s_shape],
        scratch_shapes=[pltpu.VMEM((S, 2 * LANE), F32), pltpu.VMEM((S, LANE), F32)],
        compiler_params=_cparams(("parallel", "arbitrary")),
    )(qn, qr, kv, kv, kr, do, lse)


def _rope_tables(T, heads):
    rows = T // GRID_W
    row = jnp.repeat(jnp.arange(rows, dtype=F32), GRID_W)
    col = jnp.tile(jnp.arange(GRID_W, dtype=F32), rows)
    n_freq = QK_ROPE // 4
    inv = ROPE_BASE ** (-jnp.arange(n_freq, dtype=F32) / n_freq)
    ar, ac = row[:, None] * inv, col[:, None] * inv
    cos = jnp.concatenate([jnp.cos(ar), jnp.cos(ar), jnp.cos(ac), jnp.cos(ac)], axis=1)
    sin = jnp.concatenate([-jnp.sin(ar), jnp.sin(ar), -jnp.sin(ac), jnp.sin(ac)], axis=1)
    pad = lambda t: jnp.tile(jnp.pad(t, ((0, 0), (0, LANE - QK_ROPE))), (1, heads))
    return pad(cos), pad(sin)


def _dw(a, dy, w, name):
    return _mm(a, dy, "tn", BF16, name, out_slots=w.shape[0] if w.ndim == 3 else None)


def _local_step(x, ctx, tgt, m_lat, m_ctx, p, W, goff, hooks=None):
    T, D = x.shape
    Tc = ctx.shape[0]
    S = T + Tc
    S5W = p["s5_d"].shape[1]
    QR, KVR = p["q_norm"].shape[1], p["kv_norm"].shape[1]
    H = W["w_uq"].shape[1] // (2 * LANE)
    G, N = p["s5_a_re"].shape[1:]
    P = S5_GROUP
    nch = G // 8
    o_cq, o_ckv, o_kr = S5W, S5W + QR, S5W + QR + KVR
    assert o_cq % QR == 0 and o_ckv % KVR == 0 and o_kr % LANE == 0 and goff % D == 0 and S5W % LANE == 0 and G % 8 == 0
    assert 8 * P == LANE
    row = lambda k, m: m[k:k + 1]
    sh1, sc1, g1, sh2, sc2, g2 = (row(k, m_lat) for k in range(6))
    csh1, csc1 = row(0, m_ctx), row(1, m_ctx)
    n1, n2, nf = p["norm1"], p["norm2"], p["norm_f"]

    (xm_lat,) = _rowmap(_normmod, "norm1_lat", T, [x], [n1, sc1, sh1], [(D, BF16)])
    (xm_ctx,) = _rowmap(_normmod, "norm1_ctx", Tc, [ctx], [n1, csc1, csh1], [(D, BF16)])
    xm_all = jnp.concatenate([xm_lat, xm_ctx], axis=0)
    h_all = _mm(xm_all, W["w_in"], "nn", F32, "mm_in")

    a_re, a_im = p["s5_a_re"][:, :, None, :], p["s5_a_im"][:, :, None, :]
    ldt = p["s5_log_dt"][:, :, None, None]
    b_re, b_im = p["s5_b_re"].transpose(0, 1, 3, 2), p["s5_b_im"].transpose(0, 1, 3, 2)
    wre, wim, vre, vim, pwr, pwi = _s5_tables(a_re, a_im, ldt, b_re, b_im, p["s5_c_re"], p["s5_c_im"])
    waug = _s5_expand(wre, wim, "s5_expand_b")
    vaug = _s5_expand(vre, vim, "s5_expand_c")
    lanes = lambda t: t.reshape(2, SUB + 1, nch, 8 * N).transpose(0, 2, 1, 3)
    pw_re, pw_im = lanes(pwr), lanes(pwi)
    near = lambda t: t[:, :, 1:]
    far = lambda t: t[:, :, :0:-1]
    pw_c = jnp.concatenate([near(pw_re), near(pw_im)], axis=-1)
    pw_a = jnp.concatenate([far(pw_re), far(pw_im)], axis=-1)
    pwc_c = jnp.concatenate([near(pw_re), -near(pw_im)], axis=-1)
    pwc_a = jnp.concatenate([far(pw_re), -far(pw_im)], axis=-1)
    y0, hs0 = _s5_fwd(h_all, waug, vaug, pw_c[0], S5W, T, 0, True, "s5_scan_fwd0")
    y1, hs1 = _s5_fwd(h_all, waug, vaug, pw_a[1], S5W, T, 1, False, "s5_scan_fwd1")

    def s5_combine(u, yf, yr, dskip):
        y5 = dskip * u + yf + yr
        return y5, jax.nn.gelu(y5)

    y5, z = _rowmap(s5_combine, "s5_combine", T, [(h_all, S5W, 0), y0, y1], [p["s5_d"]], [(S5W, F32), (S5W, BF16)])

    (qn,) = _rowmap(_rms, "q_norm", T, [(h_all, QR, o_cq // QR)], [p["q_norm"]], [(QR, BF16)])
    (kvn,) = _rowmap(_rms, "kv_norm", S, [(h_all, KVR, o_ckv // KVR)], [p["kv_norm"]], [(KVR, BF16)])
    qraw = _mm(qn, W["w_uq"], "nn", F32, "mm_uq")
    kvraw = _mm(kvn, W["w_ukv"], "nn", BF16, "mm_ukv")
    cos_q, sin_q = _rope_tables(T, H)
    padl = lambda t: jnp.pad(t[:, :LANE], ((0, Tc), (0, 0)))
    cos_k = padl(cos_q) + jnp.pad(jnp.ones((Tc, LANE), F32), ((T, 0), (0, 0)))
    sin_k = padl(sin_q)
    hn = H * LANE

    def q_post(q, cos, sin):
        return q[:, :hn], _rope(q[:, hn:], cos, sin)

    q_nope, q_rope = _rowmap(q_post, "q_rope", T, [qraw, cos_q, sin_q], [], [(hn, BF16), (hn, BF16)])
    (kr,) = _rowmap(_rope, "k_rope", S, [(h_all, LANE, o_kr // LANE), cos_k, sin_k], [], [(LANE, BF16)])
    scale = (QK_NOPE + QK_ROPE) ** -0.5
    o, lse = _attn_fwd(q_nope, q_rope, kvraw, kr, H, scale)
    g1_fwd = g1
    if hooks:
        W = {**W, **hooks["mix_weights"](o)}
        g1_fwd = g1 + hooks["ffn_mid"](o)[:1, :1]

    zz = _mm(z, W["w_glu"], "nn", BF16, "mm_glu")
    br_mla = _mm(o, W["w_mla_o"], "nn", BF16, "mm_mla_o")

    def merge(zz, brm, gs, gm):
        a, b = zz[:, :D], zz[:, D:]
        return jax.nn.sigmoid(gs) * (a * jax.nn.sigmoid(b)) + jax.nn.sigmoid(gm) * brm

    gb = goff // D
    merge_ins = [zz, br_mla, (h_all, D, gb), (h_all, D, gb + 1)]
    (mix,) = _rowmap(merge, "merge", T, merge_ins, [], [(D, BF16)])
    out1 = _mm(mix, W["w_out"], "nn", F32, "mm_out")

    def resid_norm2(x, out1, g1, n2, sc2, sh2):
        x1 = x + g1 * out1
        return x1, _normmod(x1, n2, sc2, sh2)

    x1, hm = _rowmap(resid_norm2, "resid_norm2", T, [x, out1], [g1_fwd, n2, sc2, sh2], [(D, F32), (D, BF16)])

    if hooks:
        W = {**W, **hooks["ffn_weights"](hm)}
    FF = W["w_ffn_out"].shape[0]
    assert FF % LANE == 0
    ab = _mm(hm, W["w_ffn_in"], "nn", BF16, "mm_ffn_in")

    def swiglu_act(a, b):
        return jax.nn.silu(a) * b

    (f,) = _rowmap(swiglu_act, "ffn_act", T, [(ab, FF, 0), (ab, FF, 1)], [], [(FF, BF16)])
    out2 = _mm(f, W["w_ffn_out"], "nn", F32, "mm_ffn_out")

    def loss_rows(x1, out2, g2, nf, tgt):
        y = _rms(x1 + g2 * out2, nf)
        return 0.5 * jnp.sum(jnp.mean(jnp.square(y - tgt), axis=-1))

    def final(x1, out2, tgt, g2, nf):
        val, (dx1, dout2, dg2, dnf) = jax.value_and_grad(loss_rows, argnums=(0, 1, 2, 3))(x1, out2, g2, nf, tgt)
        return dx1, dout2, jnp.full((1, LANE), val, F32), dg2, dnf

    dx2, dout2, loss_acc, dg2, dnf = _rowmap(final, "final_loss", T, [x1, out2, tgt], [g2, nf],
                                             [(D, F32), (D, BF16)], [LANE, D, D])

    gW = {}
    df = _mm(dout2, W["w_ffn_out"], "nt", BF16, "mm_ffn_out_dx")
    gW["w_ffn_out"] = _dw(f, dout2, W["w_ffn_out"], "mm_ffn_out_dw")

    def swiglu_bwd(a, b, df):
        _, vjp = jax.vjp(swiglu_act, a, b)
        da, db = vjp(df)
        return jnp.concatenate([da, db], axis=1)

    (dab,) = _rowmap(swiglu_bwd, "ffn_act_bwd", T, [(ab, FF, 0), (ab, FF, 1), df], [], [(2 * FF, BF16)])
    dhm = _mm(dab, W["w_ffn_in"], "nt", F32, "mm_ffn_in_dx")
    gW["w_ffn_in"] = _dw(hm, dab, W["w_ffn_in"], "mm_ffn_in_dw")
    if hooks:
        token = hooks["send_grads"](FFN, [gW.pop(n) for n in FFN])
        g1 = g1 if token is None else g1 + token[:1, :1]

    def resid_norm2_bwd(x, out1, dx2, dhm, g1, n2, sc2, sh2):
        _, vjp = jax.vjp(resid_norm2, x, out1, g1, n2, sc2, sh2)
        dx, dout1, dg1, dn2, dsc2, dsh2 = vjp((dx2, dhm))
        return dx, dout1, dg1, dn2, dsc2, dsh2

    dx1, dout1, dg1, dn2, dsc2, dsh2 = _rowmap(resid_norm2_bwd, "resid_norm2_bwd", T, [x, out1, dx2, dhm],
                                               [g1, n2, sc2, sh2], [(D, F32), (D, BF16)], [D, D, D, D])

    dmix = _mm(dout1, W["w_out"], "nt", BF16, "mm_out_dx")
    gW["w_out"] = _dw(mix, dout1, W["w_out"], "mm_out_dw")

    def merge_bwd(zz, brm, gs, gm, dmix):
        _, vjp = jax.vjp(merge, zz, brm, gs, gm)
        dzz, dbrm, dgs, dgm = vjp(dmix)
        return dzz, dbrm, jnp.concatenate([dgs, dgm], axis=1)

    dzz, dbrm, dgates = _rowmap(merge_bwd, "merge_bwd", T, merge_ins + [dmix], [],
                                [(2 * D, BF16), (D, BF16), (2 * D, BF16)])
    do = _mm(dbrm, W["w_mla_o"], "nt", BF16, "mm_mla_o_dx")
    gW["w_mla_o"] = _dw(o, dbrm, W["w_mla_o"], "mm_mla_o_dw")
    dz = _mm(dzz, W["w_glu"], "nt", BF16, "mm_glu_dx")
    gW["w_glu"] = _dw(z, dzz, W["w_glu"], "mm_glu_dw")
    d_skip_w = p["s5_d"]
    if hooks:
        token = hooks["send_grads"](MIX, [gW.pop(n) for n in MIX])
        d_skip_w = d_skip_w if token is None else d_skip_w + token[:1, :1]

    def s5_combine_bwd(u, y5, dz, dskip):
        _, vjp = jax.vjp(lambda y: jax.nn.gelu(y), y5)
        (dy5,) = vjp(dz)
        return dy5, jnp.sum(dy5 * u, axis=0, keepdims=True)

    dy5, d_skip = _rowmap(s5_combine_bwd, "s5_combine_bwd", T, [(h_all, S5W, 0), y5, dz], [d_skip_w], [(S5W, F32)], [S5W])

    dq_nope, dq_rope, dk_nope, dkr_heads, dv = _attn_bwd(q_nope, q_rope, kvraw, kr, do, lse, H, scale)

    def q_post_bwd(dqn, dqr, cos, sin):
        return jnp.concatenate([dqn, _rope_bwd(dqr, cos, sin)], axis=1)

    (dqraw,) = _rowmap(q_post_bwd, "q_rope_bwd", T, [dq_nope, dq_rope, cos_q, sin_q], [], [(2 * hn, BF16)])
    dkvraw = jnp.concatenate([dk_nope, dv], axis=1)

    def k_rope_bwd(dkh, cos, sin):
        d = dkh[:, :LANE]
        for h in range(1, H):
            d = d + dkh[:, h * LANE:(h + 1) * LANE]
        return _rope_bwd(d, cos, sin)

    (dkr,) = _rowmap(k_rope_bwd, "k_rope_bwd", S, [dkr_heads, cos_k, sin_k], [], [(LANE, BF16)])
    dqn = _mm(dqraw, W["w_uq"], "nt", F32, "mm_uq_dx")
    gW["w_uq"] = _dw(qn, dqraw, W["w_uq"], "mm_uq_dw")
    dkvn = _mm(dkvraw, W["w_ukv"], "nt", F32, "mm_ukv_dx")
    gW["w_ukv"] = _dw(kvn, dkvraw, W["w_ukv"], "mm_ukv_dw")

    def rms_bwd(cx, dn, g):
        _, vjp = jax.vjp(_rms, cx, g)
        return vjp(dn)

    dcq, dq_norm = _rowmap(rms_bwd, "q_norm_bwd", T, [(h_all, QR, o_cq // QR), dqn], [p["q_norm"]], [(QR, BF16)], [QR])
    dckv, dkv_norm = _rowmap(rms_bwd, "kv_norm_bwd", S, [(h_all, KVR, o_ckv // KVR), dkvn], [p["kv_norm"]],
                             [(KVR, BF16)], [KVR])

    dy_all = jnp.concatenate([dy5, jnp.zeros((Tc, S5W), F32)], axis=0)
    du0, dbb0, dc0, da0 = _s5_bwd(dy_all, h_all, hs0, waug, vaug, pwc_a[0], S5W, T, 0, True, "s5_scan_bwd0")
    du1, dbb1, dc1, da1 = _s5_bwd(dy_all, h_all, hs1, waug, vaug, pwc_c[1], S5W, T, 1, False, "s5_scan_bwd1")

    def du_combine(a, b, dy, dskip):
        return a + b + dskip * dy

    (du_all,) = _rowmap(du_combine, "s5_du", S, [du0, du1, dy_all], [p["s5_d"]], [(S5W, BF16)])
    dbb = jnp.einsum("dsgpcgn->dcsgpn", jnp.stack([dbb0, dbb1]).reshape(2, nch, 8, P, 2, 8, N)).reshape(2, 2, G, P, N)
    dcm = jnp.einsum("dscgngp->dcsgpn", jnp.stack([dc0, dc1]).reshape(2, nch, 2, 8, N, 8, P)).reshape(2, 2, G, P, N)
    da = jnp.stack([da0, da1]).reshape(2, nch, 2, 8, N).transpose(0, 2, 1, 3, 4).reshape(2, 2, G, 1, N)
    d_lr, d_li, d_ldt, d_br, d_bi = _s5_param_bwd(a_re, a_im, ldt, b_re, b_im, da[:, 0], da[:, 1], dbb[:, 0], dbb[:, 1])

    lat_only = lambda t: jnp.pad(t, ((0, Tc), (0, 0)))
    dh_all = jnp.concatenate([du_all, lat_only(dcq), dckv, dkr, jnp.zeros((S, goff - o_kr - LANE), BF16), lat_only(dgates)],
                             axis=1)
    dxm = _mm(dh_all, W["w_in"], "nt", F32, "mm_in_dx")
    gW["w_in"] = _dw(xm_all, dh_all, W["w_in"], "mm_in_dw")

    def norm1_bwd(x, dxm, dx1, n1, sc, sh):
        _, vjp = jax.vjp(_normmod, x, n1, sc, sh)
        dx, dn, dsc, dsh = vjp(dxm)
        return dx + dx1, dn, dsc, dsh

    grad_x, dn1_l, dsc1, dsh1 = _rowmap(norm1_bwd, "norm1_lat_bwd", T, [x, dxm, dx1], [n1, sc1, sh1], [(D, F32)], [D, D, D])

    def norm1_ctx_bwd(x, dxm, n1, sc, sh):
        _, vjp = jax.vjp(_normmod, x, n1, sc, sh)
        return vjp(dxm)[1:]

    dn1_c, dcsc1, dcsh1 = _rowmap(norm1_ctx_bwd, "norm1_ctx_bwd", Tc, [ctx, dxm[T:]], [n1, csc1, csh1], [], [D, D, D])

    zero = jnp.zeros((1, D), F32)
    dm_lat = jnp.concatenate([dsh1, dsc1, dg1, dsh2, dsc2, dg2], axis=0)
    dm_ctx = jnp.concatenate([dcsh1, dcsc1, zero, zero, zero, zero], axis=0)
    small = {
        "norm1": dn1_l + dn1_c, "norm2": dn2, "norm_f": dnf, "q_norm": dq_norm, "kv_norm": dkv_norm, "s5_d": d_skip,
        "s5_a_re": d_lr, "s5_a_im": d_li, "s5_log_dt": d_ldt, "s5_b_re": d_br.transpose(0, 1, 3, 2),
        "s5_b_im": d_bi.transpose(0, 1, 3, 2), "s5_c_re": dcm[:, 0], "s5_c_im": -dcm[:, 1],
    }
    return loss_acc[:, :1], grad_x, small, dm_lat, dm_ctx, gW


BIG = ("w_in", "w_uq", "w_ukv", "w_glu", "w_mla_o", "w_out", "w_ffn_in", "w_ffn_out")
FFN = ("w_ffn_in", "w_ffn_out")
MIX = ("w_out", "w_mla_o", "w_glu")
ROW_SHARDED = ("w_out", "w_ffn_out")
RELAID = ("w_in", "w_uq", "w_ukv")
SMALL = ("c_ctx", "b_mod", "norm1", "norm2", "s5_a_re", "s5_a_im", "s5_log_dt", "s5_b_re", "s5_b_im", "s5_c_re",
         "s5_c_im", "s5_d", "q_norm", "kv_norm", "norm_f")
S5_BULK = ("s5_b_re", "s5_b_im", "s5_c_re", "s5_c_im")
WEIGHTS = ("c_ctx", "w_mod", "b_mod", "norm1", "norm2", "w_in", "s5_a_re", "s5_a_im", "s5_log_dt", "s5_b_re", "s5_b_im",
           "s5_c_re", "s5_c_im", "s5_d", "w_glu", "q_norm", "kv_norm", "w_uq", "w_ukv", "w_mla_o", "w_out", "w_ffn_in",
           "w_ffn_out", "norm_f")


def _heads_split(w, heads, first):
    k = w.shape[0]
    w3 = w.reshape(k, heads, -1)
    return jnp.concatenate([w3[:, :, :first].reshape(k, -1), w3[:, :, first:].reshape(k, -1)], axis=1)


def _uq_layout(w, heads):
    k = w.shape[0]
    w3 = w.reshape(k, heads, QK_NOPE + QK_ROPE)
    rope = jnp.pad(w3[:, :, QK_NOPE:], ((0, 0), (0, 0), (0, LANE - QK_ROPE)))
    return jnp.concatenate([w3[:, :, :QK_NOPE].reshape(k, -1), rope.reshape(k, -1)], axis=1)


def _uq_unlayout(w, heads):
    k = w.shape[0]
    nope = w[:, :heads * QK_NOPE].reshape(k, heads, QK_NOPE)
    rope = w[:, heads * QK_NOPE:].reshape(k, heads, LANE)[:, :, :QK_ROPE]
    return jnp.concatenate([nope, rope], axis=2).reshape(k, -1)


def _heads_merge(w, heads, first):
    k = w.shape[0]
    a, b = w[:, :heads * first].reshape(k, heads, first), w[:, heads * first:].reshape(k, heads, -1)
    return jnp.concatenate([a, b], axis=2).reshape(k, -1)


def _cols_full(w8):
    return w8.transpose(1, 0, 2).reshape(w8.shape[1], -1)


def _cols_slots(w):
    return w.reshape(w.shape[0], N_DEV, -1).transpose(1, 0, 2)


def _weight_layout(n, w8):
    if n in ROW_SHARDED:
        return w8.reshape(-1, w8.shape[-1])
    return _cols_full(w8) if (n in RELAID or w8.shape[-1] % LANE) else w8


def _grad_slots(n, g):
    if g.ndim == 3:
        return g
    return g.reshape(N_DEV, g.shape[0] // N_DEV, g.shape[1]) if n in ROW_SHARDED else _cols_slots(g)


def _model_weights(g8, D):
    W = {n: _weight_layout(n, w8) for n, w8 in g8.items()}
    w_in = W["w_in"]
    n_front = w_in.shape[1] - 2 * D
    goff = -(-n_front // D) * D
    W["w_in"] = jnp.concatenate([w_in[:, :n_front], jnp.zeros((D, goff - n_front), w_in.dtype), w_in[:, n_front:]], axis=1)
    heads = W["w_uq"].shape[1] // (QK_NOPE + QK_ROPE)
    W["w_uq"] = _uq_layout(W["w_uq"], heads)
    W["w_ukv"] = _heads_split(W["w_ukv"], heads, QK_NOPE)
    return W, goff


def kernel(x, c, ctx, c_ctx, w_mod, b_mod, norm1, norm2, w_in, s5_a_re, s5_a_im, s5_log_dt, s5_b_re, s5_b_im, s5_c_re, s5_c_im, s5_d, w_glu, q_norm, kv_norm, w_uq, w_ukv, w_mla_o, w_out, w_ffn_in, w_ffn_out, norm_f, loss_target, m_c_ctx, m_w_mod, m_b_mod, m_norm1, m_norm2, m_w_in, m_s5_a_re, m_s5_a_im, m_s5_log_dt, m_s5_b_re, m_s5_b_im, m_s5_c_re, m_s5_c_im, m_s5_d, m_w_glu, m_q_norm, m_kv_norm, m_w_uq, m_w_ukv, m_w_mla_o, m_w_out, m_w_ffn_in, m_w_ffn_out, m_norm_f, v_c_ctx, v_w_mod, v_b_mod, v_norm1, v_norm2, v_w_in, v_s5_a_re, v_s5_a_im, v_s5_log_dt, v_s5_b_re, v_s5_b_im, v_s5_c_re, v_s5_c_im, v_s5_d, v_w_glu, v_q_norm, v_kv_norm, v_w_uq, v_w_ukv, v_w_mla_o, v_w_out, v_w_ffn_in, v_w_ffn_out, v_norm_f):
    a = dict(locals())
    D = x.shape[-1]
    me = 4 * lax.axis_index("x") + 2 * lax.axis_index("y") + lax.axis_index("c")

    shard = {n: a[n][0] for n in BIG}
    first = [n for n in BIG if n not in FFN + MIX]
    gathered = _all_gather([shard[n].astype(BF16) for n in first] + [jnp.broadcast_to(c, (8, D))], "ag_weights")
    W, goff = _model_weights(dict(zip(first, gathered[:-1])), D)
    cg = gathered[-1]

    wm = w_mod[0]
    ncol = wm.shape[1]
    c16 = jnp.concatenate([cg[:, 0, :], c_ctx[None], jnp.zeros((7, D), F32)], axis=0)
    (s16,) = _rowmap(jax.nn.silu, "mod_silu", 16, [c16], [], [(D, BF16)])
    m_cols = _mm(s16, wm, "nn", F32, "mm_mod")
    (mg,) = _all_gather([m_cols], "ag_mod")
    (m16,) = _rowmap(lambda m, b: m + b, "mod_bias", 16, [_cols_full(mg)], [b_mod], [(N_DEV * ncol, F32)])

    mix_blocks = [shard[n].astype(BF16) for n in MIX]
    mix = _xchg_start(mix_blocks, [_own_slot(b, me) for b in mix_blocks], False, "ag_mix_start")
    ffn_blocks = [shard[n].astype(BF16) for n in FFN]
    ffn = {}
    ffn["sems1"], ffn["thru"], ag_token = _ag2_start(ffn_blocks, [_own_slot(b, me) for b in ffn_blocks], "ag_ffn_start")
    m16 = m16 + (mix[3][:1, :1] + ag_token[:1, :1])

    def mix_weights(after):
        lands = _xchg_wait(mix[0], mix[1], mix[2], after, False, "ag_mix_wait")
        return {n: _weight_layout(n, w8) for n, w8 in zip(MIX, lands)}

    def ffn_mid(after):
        ffn["sems2"], ffn["thru"], token = _ag2_mid(ffn["sems1"], ffn["thru"], after, "ag_ffn_mid")
        return token

    def ffn_weights(after):
        lands = _ag2_end(ffn["sems1"], ffn["sems2"], ffn["thru"], after, "ag_ffn_end")
        return {n: _weight_layout(n, w8) for n, w8 in zip(FFN, lands)}

    rs_async = {}

    def send_grads(names, gs):
        slots = [_grad_slots(n, g) for n, g in zip(names, gs)]
        lands = [_own_slot(lax.dynamic_index_in_dim(s, me, 0, keepdims=False), me) for s in slots]
        rs_async[names] = _xchg_start(slots, lands, True, "rs_start_" + names[0])
        return rs_async[names][3]

    m_lat = lax.dynamic_slice(m16, (me, 0), (1, 6 * D)).reshape(6, D)
    m_ctx = m16[8].reshape(6, D)

    p = {n: a[n][0] for n in ("norm1", "norm2", "s5_a_re", "s5_a_im", "s5_log_dt", "s5_b_re", "s5_b_im", "s5_c_re",
                              "s5_c_im", "q_norm", "kv_norm")}
    p = {k: (v[None] if v.ndim == 1 else v) for k, v in p.items()}
    p["s5_d"] = s5_d.reshape(1, -1)
    p["norm_f"] = norm_f[None]
    loss_part, grad_x, small, dm_lat, dm_ctx, gW = _local_step(x[0], ctx[0], loss_target[0], m_lat, m_ctx, p, W, goff,
                                                               dict(mix_weights=mix_weights, ffn_mid=ffn_mid,
                                                                    ffn_weights=ffn_weights, send_grads=send_grads))
    loss = lax.psum(loss_part[0, 0], ("x", "y", "c"))

    dm8 = jnp.concatenate([dm_lat.reshape(1, -1), dm_ctx.reshape(1, -1), jnp.zeros((SUB - 2, 6 * D), F32)], axis=0)
    (dmg,) = _all_gather([dm8], "ag_dmod")
    dm_sum = _sum_slots(dmg, "sum_dmod")
    dM16 = jnp.concatenate([dmg[:, 0, :], dm_sum[1:2], jnp.zeros((7, 6 * D), F32)], axis=0)
    (g_b_mod,) = _rowmap(lambda d: jnp.sum(d, axis=0, keepdims=True), "b_mod_grad", 16, [dM16], [], [], [6 * D])
    dM_loc = lax.dynamic_slice(dM16, (0, me * ncol), (16, ncol))
    g_w_mod = _mm(s16, dM_loc, "tn", F32, "mm_mod_dw")
    ds16_part = _mm(dM_loc, wm, "nt", F32, "mm_mod_dx")

    fine = [n for n in SMALL if n not in ("c_ctx", "b_mod") + S5_BULK]
    sg, sgb = _all_gather([_pack_rows([small[n] for n in fine] + [ds16_part[8:9]], F32),
                           _pack_rows([small[n] for n in S5_BULK], BF16)], "ag_small")
    parts = _unpack_rows(_sum_slots(sg, "sum_small"), [small[n].shape for n in fine] + [(1, D)])
    grads = dict(zip(fine, parts[:-1]))
    grads.update(zip(S5_BULK, _unpack_rows(_sum_slots(sgb, "sum_small_bulk"), [small[n].shape for n in S5_BULK])))

    def silu_bwd(cc, ds):
        _, vjp = jax.vjp(jax.nn.silu, cc)
        return vjp(ds)[0]

    (g_c_ctx,) = _rowmap(silu_bwd, "c_ctx_grad", 1, [c_ctx[None], parts[-1]], [], [(D, F32)])
    grads["c_ctx"], grads["b_mod"] = g_c_ctx, g_b_mod

    gW = dict(gW)
    n_front = w_in.shape[-1] * N_DEV - 2 * D
    gW["w_in"] = jnp.concatenate([gW["w_in"][:, :n_front], gW["w_in"][:, goff:]], axis=1)
    heads = gW["w_uq"].shape[1] // (2 * LANE)
    gW["w_uq"] = _uq_unlayout(gW["w_uq"], heads)
    gW["w_ukv"] = _heads_merge(gW["w_ukv"], heads, QK_NOPE)
    last = [n for n in BIG if n in gW]
    slots = [_grad_slots(n, gW[n]) for n in last]
    from_sibling = _rs_pair(slots, "rs_pair")
    chip_sums = [_add_pair(pp, rr, "rs_add_" + n) for n, pp, rr in zip(last, slots, from_sibling)]
    my_chip = 2 * lax.axis_index("x") + lax.axis_index("y")
    lands = [_own_slot(lax.dynamic_index_in_dim(q, my_chip, 0, keepdims=False), my_chip, N_CHIP) for q in chip_sums]
    rs_send, rs_recv, rs_thru, behind = _xchg_start(chip_sums, lands, True, "rs_chips_start")
    for names, (send, recv, thru, _) in rs_async.items():
        for n, g8 in zip(names, _xchg_wait(send, recv, thru, behind, True, "rs_wait_" + names[0])):
            grads[n] = _sum_slots(g8, "rs_sum_" + n)
    grads["w_mod"] = g_w_mod

    out = {}

    def adamw_big(n, after):
        d, nm, nv = _adamw(a[n][0], grads[n], a["m_" + n][0], a["v_" + n][0], "adamw_" + n, after)
        for k, val in (("grad_", grads[n]), ("delta_", d), ("new_m_", nm), ("new_v_", nv)):
            out[k + n] = val.reshape(a[n].shape)
        return nv

    for n in FFN + MIX + ("w_mod",):
        behind = adamw_big(n, behind)
    packs = [_pack_rows([t[n] for n in SMALL], F32) for t in (
        {n: a[n] for n in SMALL}, {n: grads[n] for n in SMALL}, {n: a["m_" + n] for n in SMALL}, {n: a["v_" + n] for n in SMALL})]
    res = _adamw(*packs, "adamw_small", behind)
    for n, g4 in zip(last, _xchg_wait(rs_send, rs_recv, rs_thru, res[2], True, "rs_chips_wait")):
        grads[n] = _sum_slots(g4, "rs_sum_" + n)
        adamw_big(n, None)
    shapes = [a[n].shape for n in SMALL]
    for k, packed in (("grad_", packs[1]), ("delta_", res[0]), ("new_m_", res[1]), ("new_v_", res[2])):
        for n, val in zip(SMALL, _unpack_rows(packed, shapes)):
            out[k + n] = val
    return (loss, grad_x[None]) + tuple(out[k + n] for k in ("grad_", "delta_", "new_m_", "new_v_") for n in WEIGHTS)
```

```python
import functools
import math

import jax
import jax.numpy as jnp
from jax import lax
from jax.experimental import pallas as pl
from jax.experimental.pallas import tpu as pltpu

F32 = jnp.float32
BF16 = jnp.bfloat16

N_DEV = 8
N_CHIP = 4
EPS = 1e-6
GRID_W = 64
S5_GROUP = 16
QK_NOPE, QK_ROPE, V_DIM = 128, 64, 128
ROPE_BASE = 10000.0
ADAM_LR, ADAM_B1, ADAM_B2, ADAM_EPS, ADAM_WD, ADAM_STEP = 0.001, 0.9, 0.999, 1e-08, 0.01, 10

LANE = 128
SUB = 8
PACK_W = 1024
PACK_ROWS = 32
VMEM_LIMIT = 48 << 20
ROWMAP_TILE_BYTES = 20 << 20
MM_VMEM_BUDGET = 36 << 20
MESH = pl.DeviceIdType.MESH
_NT = (((1,), (1,)), ((), ()))
_TN = (((0,), (0,)), ((), ()))


def _pick(dim, cands):
    for c in cands:
        if dim % c == 0:
            return c
    return dim


def _cparams(sem):
    return pltpu.CompilerParams(dimension_semantics=sem, vmem_limit_bytes=VMEM_LIMIT)


def _mm(a, b, dims, out_dtype, name, out_slots=None):
    a = a.astype(BF16)
    b = b.astype(BF16)
    b3 = b.ndim == 3
    if dims == "nn":
        (M, K), N = a.shape, (b.shape[0] * b.shape[2] if b3 else b.shape[1])
    elif dims == "nt":
        M, N = a.shape[0], b.shape[-2]
        K = b.shape[0] * b.shape[2] if b3 else b.shape[1]
    else:
        (K, M), N = a.shape, b.shape[1]
    unit_n = b.shape[2] if (b3 and dims == "nn") else (N // out_slots if out_slots else N)
    unit_k = b.shape[2] if (b3 and dims == "nt") else K
    osz = jnp.dtype(out_dtype).itemsize
    tm, tn, tk = _mm_tiles(M, unit_n, unit_k, osz, LANE if dims == "tn" else 16)
    nk, npt, kpt = K // tk, unit_n // tn, unit_k // tk
    use_acc = nk > 1 and out_dtype != F32
    if dims == "nn":
        a_spec = pl.BlockSpec((tm, tk), lambda i, j, k: (i, k))
        b_spec = (pl.BlockSpec((None, tk, tn), lambda i, j, k: (j // npt, k, j % npt)) if b3
                  else pl.BlockSpec((tk, tn), lambda i, j, k: (k, j)))
        dn = (((1,), (0,)), ((), ()))
    elif dims == "nt":
        a_spec = pl.BlockSpec((tm, tk), lambda i, j, k: (i, k))
        b_spec = (pl.BlockSpec((None, tn, tk), lambda i, j, k: (k // kpt, j, k % kpt)) if b3
                  else pl.BlockSpec((tn, tk), lambda i, j, k: (j, k)))
        dn = _NT
    else:
        a_spec = pl.BlockSpec((tk, tm), lambda i, j, k: (k, i))
        b_spec = pl.BlockSpec((tk, tn), lambda i, j, k: (k, j))
        dn = _TN
    if out_slots:
        out_spec = pl.BlockSpec((None, tm, tn), lambda i, j, k: (j // npt, i, j % npt))
        out_shape = jax.ShapeDtypeStruct((out_slots, M, unit_n), out_dtype)
    else:
        out_spec = pl.BlockSpec((tm, tn), lambda i, j, k: (i, j))
        out_shape = jax.ShapeDtypeStruct((M, N), out_dtype)

    def body(a_ref, b_ref, o_ref, *scratch):
        part = lax.dot_general(a_ref[...], b_ref[...], dn, preferred_element_type=F32)
        if nk == 1:
            o_ref[...] = part.astype(o_ref.dtype)
            return
        acc_ref = scratch[0] if use_acc else o_ref
        k = pl.program_id(2)

        @pl.when(k == 0)
        def _():
            acc_ref[...] = part

        @pl.when(k > 0)
        def _():
            acc_ref[...] += part

        if use_acc:
            @pl.when(k == nk - 1)
            def _():
                o_ref[...] = acc_ref[...].astype(o_ref.dtype)

    return pl.pallas_call(
        body, name=name, grid=(M // tm, N // tn, nk),
        in_specs=[a_spec, b_spec], out_specs=out_spec, out_shape=out_shape,
        scratch_shapes=[pltpu.VMEM((tm, tn), F32)] if use_acc else [],
        compiler_params=_cparams(("parallel", "parallel", "arbitrary")),
    )(a, b)


def _divisors(n, mult, cap):
    d = [t for t in range(mult, min(n, cap) + 1, mult) if n % t == 0]
    return d[::-1] or [n]


def _mm_tiles(M, unit_n, unit_k, out_itemsize, tm_mult):
    best = None
    for tk in _divisors(unit_k, LANE, 2816):
        for tn in _divisors(unit_n, LANE, 1536):
            for tm in _divisors(M, tm_mult, 1024):
                vmem = 2 * 2 * (tm * tk + tk * tn) + 2 * tm * tn * out_itemsize + 4 * tm * tn * (2 if unit_k > tk else 1)
                if vmem > MM_VMEM_BUDGET:
                    continue
                steps = (M // tm) * (unit_n // tn) * (unit_k // tk)
                key = (steps, -tk, -tn)
                if best is None or key < best[0]:
                    best = (key, (tm, tn, tk))
                break
    return best[1]


def _rowmap(fn, name, M, row_ins, bc_ins, row_outs, acc_outs=(), after=None):
    row_ins = [r if isinstance(r, tuple) else (r, r.shape[1], 0) for r in row_ins]
    row_bytes = sum(w * a.dtype.itemsize for a, w, _ in row_ins) + sum(w * jnp.dtype(d).itemsize for w, d in row_outs)
    widest = max([w for _, w, _ in row_ins] + [w for w, _ in row_outs])
    row_bytes = 2 * row_bytes + 6 * 4 * widest
    tm = _pick(M, [t for t in (512, 256, 128, 64, 32, 16) if t * row_bytes <= ROWMAP_TILE_BYTES] + [16])
    n_in, n_row, n_acc = len(row_ins) + len(bc_ins), len(row_outs), len(acc_outs)

    def body(*refs):
        res = fn(*[r[...].astype(F32) for r in refs[:n_in]])
        res = res if isinstance(res, (tuple, list)) else (res,)
        outs = refs[n_in + (after is not None):]
        for k in range(n_row):
            outs[k][...] = res[k].astype(outs[k].dtype)
        if n_acc:
            @pl.when(pl.program_id(0) == 0)
            def _():
                for k in range(n_acc):
                    outs[n_row + k][...] = jnp.zeros_like(outs[n_row + k])

            for k in range(n_acc):
                outs[n_row + k][...] += res[n_row + k].astype(F32)

    in_specs = [pl.BlockSpec((tm, w), functools.partial(lambda i, blk: (i, blk), blk=blk)) for _, w, blk in row_ins]
    in_specs += [pl.BlockSpec(b.shape, lambda i: (0, 0)) for b in bc_ins]
    in_specs += [pl.BlockSpec(memory_space=pl.ANY)] * (after is not None)
    out_specs = [pl.BlockSpec((tm, w), lambda i: (i, 0)) for w, _ in row_outs]
    out_specs += [pl.BlockSpec((1, w), lambda i: (0, 0)) for w in acc_outs]
    out_shape = [jax.ShapeDtypeStruct((M, w), d) for w, d in row_outs]
    out_shape += [jax.ShapeDtypeStruct((1, w), F32) for w in acc_outs]
    return pl.pallas_call(
        body, name=name, grid=(M // tm,), in_specs=in_specs, out_specs=out_specs, out_shape=out_shape,
        compiler_params=_cparams(("arbitrary",) if n_acc else ("parallel",)),
    )(*[a for a, _, _ in row_ins], *bc_ins, *([] if after is None else [after]))


def _rms(x, g):
    return x * lax.rsqrt(jnp.mean(x * x, axis=-1, keepdims=True) + EPS) * g


def _normmod(x, g, sc, sh):
    return _rms(x, g) * (1.0 + sc) + sh


def _swap16(v):
    w = v.shape[1]
    lane = lax.broadcasted_iota(jnp.int32, v.shape, 1)
    return jnp.where((lane // 16) % 2 == 0, pltpu.roll(v, w - 16, 1), pltpu.roll(v, 16, 1))


def _rope(v, cos, sin_signed):
    return v * cos + _swap16(v) * sin_signed


def _rope_bwd(d, cos, sin_signed):
    return d * cos + _swap16(d * sin_signed)


def _mesh_pos():
    return lax.axis_index("x"), lax.axis_index("y"), lax.axis_index("c")


def _hbm_call(body, name, ins, out_shapes, n_sems):
    any_spec = pl.BlockSpec(memory_space=pl.ANY)
    return pl.pallas_call(
        body, name=name, out_shape=out_shapes, in_specs=[any_spec] * len(ins), out_specs=[any_spec] * len(out_shapes),
        scratch_shapes=[pltpu.SemaphoreType.DMA((n_sems,)), pltpu.SemaphoreType.DMA((n_sems,)),
                        pltpu.SemaphoreType.DMA((len(ins),))],
    )(*ins)


def _all_gather(xs, name):
    n = len(xs)

    def body(*refs):
        x_refs, out_refs, (send_sems, recv_sems, local_sems) = refs[:n], refs[n:2 * n], refs[2 * n:]
        x, y, c = _mesh_pos()
        me, sibling = (x, y, c), (x, y, 1 - c)
        chips = [(1 - x, y), (x, 1 - y), (1 - x, 1 - y)]
        locals_, first, passed, arrivals = [], [], [], []
        for a in range(n):
            def slot(px, py, pc, a=a):
                return out_refs[a].at[4 * px + 2 * py + pc]

            def copy(k, block, to, src=None, a=a, slot=slot):
                return pltpu.make_async_remote_copy(
                    src_ref=slot(*block) if src is None else src, dst_ref=slot(*block),
                    send_sem=send_sems.at[7 * a + k], recv_sem=recv_sems.at[7 * a + k], device_id=to, device_id_type=MESH)

            locals_.append(pltpu.make_async_copy(x_refs[a], slot(*me), local_sems.at[a]))
            first.append(copy(0, me, sibling, src=x_refs[a]))
            first += [copy(1 + j, me, (*chip, c), src=x_refs[a]) for j, chip in enumerate(chips)]
            passed.append([copy(4 + j, (*chip, c), sibling) for j, chip in enumerate(chips)])
            arrivals.append([copy(1 + j, (*chip, c), me) for j, chip in enumerate(chips)]
                            + [copy(0, sibling, me)] + [copy(4 + j, (*chip, 1 - c), me) for j, chip in enumerate(chips)])
        for cp in locals_ + first:
            cp.start()
        for j in range(3):
            for a in range(n):
                arrivals[a][j].wait_recv()
                passed[a][j].start()
        for a in range(n):
            for cp in arrivals[a][3:]:
                cp.wait_recv()
        for cp in first + [p for ps in passed for p in ps]:
            cp.wait_send()
        for cp in locals_:
            cp.wait()

    return _hbm_call(body, name, xs, [jax.ShapeDtypeStruct((N_DEV,) + x.shape, x.dtype) for x in xs], 7 * n)


def _rs_pair(ps, name):
    n = len(ps)

    def body(*refs):
        p_refs, out_refs, (send_sems, recv_sems, _) = refs[:n], refs[n:2 * n], refs[2 * n:]
        x, y, c = _mesh_pos()
        sends, recvs = [], []
        for a in range(n):
            for q in range(N_CHIP):
                sem = dict(send_sem=send_sems.at[4 * a + q], recv_sem=recv_sems.at[4 * a + q],
                           device_id=(x, y, 1 - c), device_id_type=MESH)
                sends.append(pltpu.make_async_remote_copy(src_ref=p_refs[a].at[2 * q + 1 - c], dst_ref=out_refs[a].at[q], **sem))
                recvs.append(pltpu.make_async_remote_copy(src_ref=p_refs[a].at[2 * q + c], dst_ref=out_refs[a].at[q], **sem))
        for cp in sends:
            cp.start()
        for cp in recvs:
            cp.wait_recv()
        for cp in sends:
            cp.wait_send()

    return _hbm_call(body, name, ps, [jax.ShapeDtypeStruct((N_CHIP,) + p.shape[1:], p.dtype) for p in ps], 4 * n)


def _xchg_copies(src_refs, land_refs, send_sems, recv_sems, slot_src):
    x, y, c = _mesh_pos()
    sends, recvs = [], []
    for a, (src, land) in enumerate(zip(src_refs, land_refs)):
        chips = land.shape[0] == N_CHIP
        npeer = land.shape[0] - 1
        me = 2 * x + y if chips else 4 * x + 2 * y + c
        for r in range(1, npeer + 1):
            px = 1 - x if r & (2 if chips else 4) else x
            py = 1 - y if r & (1 if chips else 2) else y
            pc = c if chips else (1 - c if r & 1 else c)
            peer = 2 * px + py if chips else 4 * px + 2 * py + pc
            sem = dict(send_sem=send_sems.at[npeer * a + r - 1], recv_sem=recv_sems.at[npeer * a + r - 1],
                       device_id=(px, py, pc), device_id_type=MESH)
            s = src.at[peer] if slot_src else src
            sends.append(pltpu.make_async_remote_copy(src_ref=s, dst_ref=land.at[me], **sem))
            recvs.append(pltpu.make_async_remote_copy(src_ref=s, dst_ref=land.at[peer], **sem))
    return sends, recvs


_HBM = pl.BlockSpec(memory_space=pltpu.HBM)
_SEM = pl.BlockSpec(memory_space=pltpu.SEMAPHORE)
_EFFECT = pltpu.SideEffectType.DATAFLOW_SIDE_EFFECTING


def _xchg_start(srcs, lands, slot_src, name):
    n = len(srcs)

    def body(*refs):
        sends, _ = _xchg_copies(refs[:n], refs[n:2 * n], refs[2 * n], refs[2 * n + 1], slot_src)
        for cp in sends:
            cp.start()
        refs[-1][...] = jnp.zeros_like(refs[-1])

    bufs = list(srcs) + list(lands)
    n_sems = n * (lands[0].shape[0] - 1)
    res = pl.pallas_call(
        body, name=name,
        out_shape=(pltpu.SemaphoreType.DMA((n_sems,)), pltpu.SemaphoreType.DMA((n_sems,)))
        + tuple(pltpu.HBM(b.shape, b.dtype) for b in bufs) + (jax.ShapeDtypeStruct((SUB, LANE), F32),),
        in_specs=(_HBM,) * (2 * n), out_specs=(_SEM, _SEM) + (_HBM,) * (2 * n) + (pl.BlockSpec(memory_space=pltpu.VMEM),),
        input_output_aliases={i: 2 + i for i in range(2 * n)},
        compiler_params=pltpu.CompilerParams(has_side_effects=_EFFECT),
    )(*[pltpu.with_memory_space_constraint(b, pltpu.HBM) for b in bufs])
    return res[0], res[1], res[2:-1], res[-1]


def _xchg_wait(send_sems, recv_sems, thru, after, slot_src, name):
    n = len(thru) // 2

    def body(*refs):
        sends, recvs = _xchg_copies(refs[:n], refs[n:2 * n], refs[2 * n], refs[2 * n + 1], slot_src)
        for cp in sends:
            cp.wait_send()
        for cp in recvs:
            cp.wait_recv()

    res = pl.pallas_call(
        body, name=name, out_shape=tuple(pltpu.HBM(b.shape, b.dtype) for b in thru),
        in_specs=(_HBM,) * (2 * n) + (_SEM, _SEM, pl.BlockSpec(memory_space=pl.ANY)), out_specs=(_HBM,) * (2 * n),
        input_output_aliases={i: i for i in range(2 * n)},
        compiler_params=pltpu.CompilerParams(has_side_effects=_EFFECT),
    )(*thru, send_sems, recv_sems, after)
    return res[n:]


def _ag2_copy(land, sems, k, block, to, src=None):
    slot = land.at[4 * block[0] + 2 * block[1] + block[2]]
    return pltpu.make_async_remote_copy(src_ref=slot if src is None else src, dst_ref=slot, send_sem=sems[0].at[k],
                                        recv_sem=sems[1].at[k], device_id=to, device_id_type=MESH)


def _ag2_start(blocks, lands, name):
    n = len(blocks)

    def body(*refs):
        x, y, c = _mesh_pos()
        for a in range(n):
            sems = (refs[2 * n], refs[2 * n + 1])
            _ag2_copy(refs[n + a], sems, 4 * a, (x, y, c), (x, y, 1 - c), src=refs[a]).start()
            for j, chip in enumerate([(1 - x, y), (x, 1 - y), (1 - x, 1 - y)]):
                _ag2_copy(refs[n + a], sems, 4 * a + 1 + j, (x, y, c), (*chip, c), src=refs[a]).start()
        refs[-1][...] = jnp.zeros_like(refs[-1])

    bufs = list(blocks) + list(lands)
    res = pl.pallas_call(
        body, name=name,
        out_shape=(pltpu.SemaphoreType.DMA((4 * n,)), pltpu.SemaphoreType.DMA((4 * n,)))
        + tuple(pltpu.HBM(b.shape, b.dtype) for b in bufs) + (jax.ShapeDtypeStruct((SUB, LANE), F32),),
        in_specs=(_HBM,) * (2 * n), out_specs=(_SEM, _SEM) + (_HBM,) * (2 * n) + (pl.BlockSpec(memory_space=pltpu.VMEM),),
        input_output_aliases={i: 2 + i for i in range(2 * n)},
        compiler_params=pltpu.CompilerParams(has_side_effects=_EFFECT),
    )(*[pltpu.with_memory_space_constraint(b, pltpu.HBM) for b in bufs])
    return (res[0], res[1]), res[2:-1], res[-1]


def _ag2_mid(sems1, thru, after, name):
    n = len(thru) // 2

    def body(*refs):
        x, y, c = _mesh_pos()
        s1, s2 = (refs[2 * n], refs[2 * n + 1]), (refs[2 * n + 3], refs[2 * n + 4])
        for j, chip in enumerate([(1 - x, y), (x, 1 - y), (1 - x, 1 - y)]):
            for a in range(n):
                _ag2_copy(refs[n + a], s1, 4 * a + 1 + j, (*chip, c), (x, y, c)).wait_recv()
                _ag2_copy(refs[n + a], s2, 3 * a + j, (*chip, c), (x, y, 1 - c)).start()
        refs[-1][...] = jnp.zeros_like(refs[-1])

    res = pl.pallas_call(
        body, name=name,
        out_shape=(pltpu.SemaphoreType.DMA((3 * n,)), pltpu.SemaphoreType.DMA((3 * n,)))
        + tuple(pltpu.HBM(b.shape, b.dtype) for b in thru) + (jax.ShapeDtypeStruct((SUB, LANE), F32),),
        in_specs=(_HBM,) * (2 * n) + (_SEM, _SEM, pl.BlockSpec(memory_space=pl.ANY)),
        out_specs=(_SEM, _SEM) + (_HBM,) * (2 * n) + (pl.BlockSpec(memory_space=pltpu.VMEM),),
        input_output_aliases={i: 2 + i for i in range(2 * n)},
        compiler_params=pltpu.CompilerParams(has_side_effects=_EFFECT),
    )(*thru, *sems1, after)
    return (res[0], res[1]), res[2:-1], res[-1]


def _ag2_end(sems1, sems2, thru, after, name):
    n = len(thru) // 2

    def body(*refs):
        x, y, c = _mesh_pos()
        s1, s2 = (refs[2 * n], refs[2 * n + 1]), (refs[2 * n + 2], refs[2 * n + 3])
        chips = [(1 - x, y), (x, 1 - y), (1 - x, 1 - y)]
        for a in range(n):
            land = refs[n + a]
            _ag2_copy(land, s1, 4 * a, (x, y, c), (x, y, 1 - c), src=refs[a]).wait_send()
            _ag2_copy(land, s1, 4 * a, (x, y, 1 - c), (x, y, c)).wait_recv()
            for j, chip in enumerate(chips):
                _ag2_copy(land, s1, 4 * a + 1 + j, (x, y, c), (*chip, c), src=refs[a]).wait_send()
                _ag2_copy(land, s2, 3 * a + j, (*chip, c), (x, y, 1 - c)).wait_send()
                _ag2_copy(land, s2, 3 * a + j, (*chip, 1 - c), (x, y, c)).wait_recv()

    res = pl.pallas_call(
        body, name=name, out_shape=tuple(pltpu.HBM(b.shape, b.dtype) for b in thru),
        in_specs=(_HBM,) * (2 * n) + (_SEM,) * 4 + (pl.BlockSpec(memory_space=pl.ANY),), out_specs=(_HBM,) * (2 * n),
        input_output_aliases={i: i for i in range(2 * n)},
        compiler_params=pltpu.CompilerParams(has_side_effects=_EFFECT),
    )(*thru, *sems1, *sems2, after)
    return res[n:]


def _own_slot(block, me, slots=N_DEV):
    return lax.dynamic_update_slice(lax.empty((slots,) + block.shape, block.dtype), block[None], (me, 0, 0))


def _add_pair(p, r, name):
    _, R, C = p.shape
    tr = _pick(R, (512, 256, 128, 64, 32, 16))

    def body(c_ref, p_ref, r_ref, o_ref):
        o_ref[...] = (p_ref[...].astype(F32) + r_ref[...].astype(F32)).astype(o_ref.dtype)

    return pl.pallas_call(
        body, name=name, out_shape=jax.ShapeDtypeStruct((N_CHIP, R, C), p.dtype),
        grid_spec=pltpu.PrefetchScalarGridSpec(
            num_scalar_prefetch=1, grid=(N_CHIP, R // tr),
            in_specs=[pl.BlockSpec((None, None, tr, C), lambda q, i, c_ref: (q, c_ref[0], i, 0)),
                      pl.BlockSpec((None, tr, C), lambda q, i, c_ref: (q, i, 0))],
            out_specs=pl.BlockSpec((None, tr, C), lambda q, i, c_ref: (q, i, 0))),
        compiler_params=_cparams(("parallel", "parallel")),
    )(lax.axis_index("c").reshape(1).astype(jnp.int32), p.reshape(N_CHIP, 2, R, C), r)


def _sum_slots(g, name):
    ns, R, C = g.shape
    tr = _pick(R, (256, 128, 64, 32, 16))

    def body(g_ref, o_ref):
        acc = g_ref[0].astype(F32)
        for j in range(1, ns):
            acc = acc + g_ref[j].astype(F32)
        o_ref[...] = acc

    return pl.pallas_call(
        body, name=name, grid=(R // tr,),
        in_specs=[pl.BlockSpec((ns, tr, C), lambda i: (0, i, 0))], out_specs=pl.BlockSpec((tr, C), lambda i: (i, 0)),
        out_shape=jax.ShapeDtypeStruct((R, C), F32), compiler_params=_cparams(("parallel",)),
    )(g)


def _pack_rows(arrs, dtype):
    parts = []
    for a in arrs:
        flat = a.reshape(-1).astype(dtype)
        pad = (-flat.shape[0]) % (PACK_W * 16)
        parts.append(jnp.pad(flat, (0, pad)).reshape(-1, PACK_W))
    out = jnp.concatenate(parts, axis=0)
    return jnp.pad(out, ((0, (-out.shape[0]) % PACK_ROWS), (0, 0)))


def _packed_rows(shape):
    n = math.prod(shape)
    return (n + PACK_W * 16 - 1) // (PACK_W * 16) * 16


def _unpack_rows(packed, shapes):
    out, r0 = [], 0
    for s in shapes:
        rows, n = _packed_rows(s), math.prod(s)
        out.append(packed[r0:r0 + rows].reshape(rows * PACK_W)[:n].reshape(s))
        r0 += rows
    return out


def _adamw_math(w, g, m, v):
    m = ADAM_B1 * m + (1.0 - ADAM_B1) * g
    v = ADAM_B2 * v + (1.0 - ADAM_B2) * (g * g)
    m_hat = m / (1.0 - ADAM_B1 ** ADAM_STEP)
    v_hat = v / (1.0 - ADAM_B2 ** ADAM_STEP)
    delta = -ADAM_LR * (m_hat / (jnp.sqrt(v_hat) + ADAM_EPS) + ADAM_WD * w)
    return delta, m, v


def _adamw(w, g, m, v, name, after=None):
    R, C = w.shape
    return _rowmap(_adamw_math, name, R, [w, g, m, v], [], [(C, F32)] * 3, after=after)


def _s5_disc_math(lr, li, ldt, br, bi):
    dt = jnp.exp(ldt)
    mag = jnp.exp(lr * dt)
    ab_re, ab_im = mag * jnp.cos(li * dt), mag * jnp.sin(li * dt)
    den = lr * lr + li * li
    nr, ni = ab_re - 1.0, ab_im
    co_re = (nr * lr + ni * li) / den
    co_im = (ni * lr - nr * li) / den
    bb_re = co_re * br - co_im * bi
    bb_im = co_re * bi + co_im * br
    return ab_re, ab_im, bb_re, bb_im


def _s5_tables(a_re, a_im, ldt, b_re, b_im, c_re, c_im):
    _, G, P, N = b_re.shape
    nch = G // 8

    def body(lr_ref, li_ref, ldt_ref, br_ref, bi_ref, cr_ref, ci_ref, wre, wim, vre, vim, pwr, pwi):
        ar, ai, bb_re, bb_im = _s5_disc_math(lr_ref[0], li_ref[0], ldt_ref[0], br_ref[0], bi_ref[0])
        cr, ci = cr_ref[0], ci_ref[0]
        pr, pi = jnp.ones_like(ar), jnp.zeros_like(ar)
        for j in range(SUB + 1):
            pwr[0, j], pwi[0, j] = pr, pi
            if j < SUB:
                tabs = ((wre, bb_re * pr - bb_im * pi), (wim, bb_re * pi + bb_im * pr),
                        (vre, cr * pr - ci * pi), (vim, -(cr * pi + ci * pr)))
                for ref, val in tabs:
                    for s in range(nch):
                        ref[0, s, pl.ds(j * LANE, LANE), :] = val[s * 8:(s + 1) * 8].reshape(LANE, N).astype(BF16)
            pr, pi = pr * ar - pi * ai, pr * ai + pi * ar

    g1n = pl.BlockSpec((1, G, 1, N), lambda d: (d, 0, 0, 0))
    gpn = pl.BlockSpec((1, G, P, N), lambda d: (d, 0, 0, 0))
    tab = pl.BlockSpec((1, nch, SUB * LANE, N), lambda d: (d, 0, 0, 0))
    pw = pl.BlockSpec((1, SUB + 1, G, 1, N), lambda d: (d, 0, 0, 0, 0))
    s_tab = jax.ShapeDtypeStruct((2, nch, SUB * LANE, N), BF16)
    s_pw = jax.ShapeDtypeStruct((2, SUB + 1, G, 1, N), F32)
    return pl.pallas_call(
        body, name="s5_tables", grid=(2,),
        in_specs=[g1n, g1n, pl.BlockSpec((1, G, 1, 1), lambda d: (d, 0, 0, 0)), gpn, gpn, gpn, gpn],
        out_specs=[tab] * 4 + [pw] * 2, out_shape=[s_tab] * 4 + [s_pw] * 2,
        compiler_params=_cparams(("parallel",)),
    )(a_re, a_im, ldt, b_re, b_im, c_re, c_im)


def _s5_expand(t_re, t_im, name):
    _, nch, R, N = t_re.shape
    sw = 8 * N

    def body(re_ref, im_ref, o_ref):
        spread = (lax.broadcasted_iota(jnp.int32, (N, sw), 1) % N == lax.broadcasted_iota(jnp.int32, (N, sw), 0)).astype(BF16)
        row_g = (lax.broadcasted_iota(jnp.int32, (R, sw), 0) % LANE) // S5_GROUP
        keep = row_g == lax.broadcasted_iota(jnp.int32, (R, sw), 1) // N
        for half, ref in enumerate((re_ref, im_ref)):
            t = jnp.dot(ref[0, 0], spread, preferred_element_type=F32)
            o_ref[0, 0, :, pl.ds(half * sw, sw)] = jnp.where(keep, t, 0.0).astype(BF16)

    spec = pl.BlockSpec((1, 1, R, N), lambda d, s: (d, s, 0, 0))
    return pl.pallas_call(
        body, name=name, grid=(2, nch), in_specs=[spec, spec],
        out_specs=pl.BlockSpec((1, 1, R, 2 * sw), lambda d, s: (d, s, 0, 0)),
        out_shape=jax.ShapeDtypeStruct((2, nch, R, 2 * sw), BF16), compiler_params=_cparams(("parallel", "parallel")),
    )(t_re, t_im)


def _s5_param_bwd(a_re, a_im, ldt, b_re, b_im, da_re, da_im, dbb_re, dbb_im):
    _, G, P, N = b_re.shape

    def body(lr_ref, li_ref, ldt_ref, br_ref, bi_ref, dar, dai, dbr, dbi, o_lr, o_li, o_ldt, o_br, o_bi):
        _, vjp = jax.vjp(_s5_disc_math, lr_ref[0], li_ref[0], ldt_ref[0], br_ref[0], bi_ref[0])
        o_lr[0], o_li[0], o_ldt[0], o_br[0], o_bi[0] = vjp((dar[0], dai[0], dbr[0], dbi[0]))

    g1n = pl.BlockSpec((1, G, 1, N), lambda d: (d, 0, 0, 0))
    g11 = pl.BlockSpec((1, G, 1, 1), lambda d: (d, 0, 0, 0))
    gpn = pl.BlockSpec((1, G, P, N), lambda d: (d, 0, 0, 0))
    s_g1n, s_g11, s_gpn = (jax.ShapeDtypeStruct(s, F32) for s in ((2, G, 1, N), (2, G, 1, 1), (2, G, P, N)))
    return pl.pallas_call(
        body, name="s5_param_bwd", grid=(2,),
        in_specs=[g1n, g1n, g11, gpn, gpn, g1n, g1n, gpn, gpn], out_specs=[g1n, g1n, g11, gpn, gpn],
        out_shape=[s_g1n, s_g1n, s_g11, s_gpn, s_gpn], compiler_params=_cparams(("parallel",)),
    )(a_re, a_im, ldt, b_re, b_im, da_re, da_im, dbb_re, dbb_im)


def _tile_local_scan(u, w_ref, back):
    tb, sw2 = u.shape[0], w_ref.shape[1]
    sw, half = sw2 // 2, LANE // 2
    tau = lax.broadcasted_iota(jnp.int32, u.shape, 0) % SUB
    low = lax.broadcasted_iota(jnp.int32, u.shape, 1) < half
    parts = [u]
    for j in range(1, SUB):
        if back:
            parts.append(jnp.where(tau >= j, pltpu.roll(u, j, 0), 0.0))
        else:
            parts.append(jnp.where(tau <= SUB - 1 - j, pltpu.roll(u, tb - j, 0), 0.0))
    out = [None] * 4
    for h in range(2):
        pieces = [jnp.where(low, a, pltpu.roll(b, half, 1)) if h == 0 else jnp.where(low, pltpu.roll(a, half, 1), b)
                  for a, b in zip(parts[0::2], parts[1::2])]
        lhs = jnp.concatenate(pieces, axis=1).astype(BF16)
        rows = jnp.concatenate([w_ref[pl.ds(j * LANE + h * half, half), :] for j in range(SUB)], axis=0)
        for part in range(2):
            cols = rows[:, part * sw + h * (sw // 2):part * sw + (h + 1) * (sw // 2)]
            out[2 * part + h] = jnp.dot(lhs, cols, preferred_element_type=F32)
    return jnp.concatenate(out, axis=1)


def _cmul_add(tile, pw, carry, sw):
    pr, pi, cr, ci = pw[:, :sw], pw[:, sw:], carry[:, :sw], carry[:, sw:]
    return tile + jnp.concatenate([pr * cr - pi * ci, pr * ci + pi * cr], axis=1)


def _tile_scan(buf, base, ntile, pw, carry, sw, causal):
    def step(k, c):
        i = k if causal else ntile - 1 - k
        r = pl.multiple_of(base + i * SUB, SUB)
        tile = _cmul_add(buf[pl.ds(r, SUB), :], pw, c, sw)
        buf[pl.ds(r, SUB), :] = tile
        return tile[SUB - 1:SUB, :] if causal else tile[0:1, :]

    return lax.fori_loop(0, ntile, step, carry)


def _s5_fwd(h_all, waug, vaug, pw, S5W, T, d, causal, name):
    S = h_all.shape[0]
    _, nch, _, sw2 = waug.shape
    sw = sw2 // 2
    tb = _pick(math.gcd(T, S - T), (256, 128, 64, 32, 16))
    ntile, nt, off = tb // SUB, S // tb, T // tb
    rb = (lambda s, t: ((t + off) % nt, s)) if causal else (lambda s, t: (nt - 1 - t, s))

    def body(u_ref, w_ref, v_ref, p_ref, y_ref, h_ref, hblk, carry):
        @pl.when(pl.program_id(1) == 0)
        def _():
            carry[...] = jnp.zeros_like(carry)

        hblk[...] = _tile_local_scan(u_ref[...], w_ref, causal)
        carry[...] = _tile_scan(hblk, 0, ntile, p_ref[...], carry[...], sw, causal)
        hb = hblk[...].astype(BF16)
        h_ref[...] = hb
        y_ref[...] = lax.dot_general(hb, v_ref[...], _NT, preferred_element_type=F32)

    return pl.pallas_call(
        body, name=name, grid=(nch, nt),
        in_specs=[pl.BlockSpec((tb, LANE), rb),
                  pl.BlockSpec((None, None, SUB * LANE, sw2), lambda s, t: (d, s, 0, 0)),
                  pl.BlockSpec((None, None, LANE, sw2), lambda s, t: (d, s, 0, 0)),
                  pl.BlockSpec((None, SUB, sw2), lambda s, t: (s, 0, 0))],
        out_specs=[pl.BlockSpec((tb, LANE), rb), pl.BlockSpec((tb, sw2), rb)],
        out_shape=[jax.ShapeDtypeStruct((S, S5W), F32), jax.ShapeDtypeStruct((S, nch * sw2), BF16)],
        scratch_shapes=[pltpu.VMEM((tb, sw2), F32), pltpu.VMEM((1, sw2), F32)],
        compiler_params=_cparams(("parallel", "arbitrary")),
    )(h_all, waug, vaug, pw)


def _s5_bwd(dy_all, h_all, hs, waug, vaug, pwc, S5W, T, d, causal, name):
    S = h_all.shape[0]
    _, nch, _, sw2 = waug.shape
    sw = sw2 // 2
    tb = _pick(math.gcd(T, S - T), (256, 128, 64, 32, 16))
    ntile, nt, off = tb // SUB, S // tb, T // tb
    rb = (lambda s, t: ((nt - 1 - t + off) % nt, s)) if causal else (lambda s, t: (t, s))
    adj_causal = not causal
    edge = SUB - 1 if adj_causal else SUB + tb
    keep_src, keep_dst = (tb, 0) if adj_causal else (SUB, SUB + tb)

    def body(dy_ref, u_ref, h_ref, w_ref, v_ref, p_ref, du_ref, dbb_ref, dc_ref, da_ref, lam):
        @pl.when(pl.program_id(1) == 0)
        def _():
            lam[pl.ds(0, SUB), :] = jnp.zeros((SUB, sw2), F32)
            lam[pl.ds(SUB + tb, SUB), :] = jnp.zeros((SUB, sw2), F32)
            dbb_ref[...] = jnp.zeros_like(dbb_ref)
            dc_ref[...] = jnp.zeros_like(dc_ref)
            da_ref[...] = jnp.zeros_like(da_ref)

        dy = dy_ref[...]
        lam[pl.ds(SUB, tb), :] = _tile_local_scan(dy, v_ref, adj_causal)
        _tile_scan(lam, SUB, ntile, p_ref[...], lam[pl.ds(edge, 1), :], sw, adj_causal)
        lb = lam[pl.ds(SUB, tb), :].astype(BF16)
        du_ref[...] = lax.dot_general(lb, w_ref[...], _NT, preferred_element_type=F32)
        dbb_ref[...] += lax.dot_general(u_ref[...].astype(BF16), lb, _TN, preferred_element_type=F32)
        dc_ref[...] += lax.dot_general(h_ref[...], dy.astype(BF16), _TN, preferred_element_type=F32)
        h = h_ref[...].astype(F32)
        ln = lam[pl.ds(SUB + 1 if causal else SUB - 1, tb), :]
        hr, hi, lr, li = h[:, :sw], h[:, sw:], ln[:, :sw], ln[:, sw:]
        da_ref[...] += jnp.concatenate([jnp.sum(hr * lr + hi * li, axis=0, keepdims=True),
                                        jnp.sum(hr * li - hi * lr, axis=0, keepdims=True)], axis=1)
        lam[pl.ds(keep_dst, SUB), :] = lam[pl.ds(keep_src, SUB), :]

    fixed = lambda s, t: (s, 0, 0)
    return pl.pallas_call(
        body, name=name, grid=(nch, nt),
        in_specs=[pl.BlockSpec((tb, LANE), rb), pl.BlockSpec((tb, LANE), rb), pl.BlockSpec((tb, sw2), rb),
                  pl.BlockSpec((None, None, LANE, sw2), lambda s, t: (d, s, 0, 0)),
                  pl.BlockSpec((None, None, SUB * LANE, sw2), lambda s, t: (d, s, 0, 0)),
                  pl.BlockSpec((None, SUB, sw2), fixed)],
        out_specs=[pl.BlockSpec((tb, LANE), rb), pl.BlockSpec((None, LANE, sw2), fixed),
                   pl.BlockSpec((None, sw2, LANE), fixed), pl.BlockSpec((None, 1, sw2), fixed)],
        out_shape=[jax.ShapeDtypeStruct((S, S5W), F32), jax.ShapeDtypeStruct((nch, LANE, sw2), F32),
                   jax.ShapeDtypeStruct((nch, sw2, LANE), F32), jax.ShapeDtypeStruct((nch, 1, sw2), F32)],
        scratch_shapes=[pltpu.VMEM((tb + 2 * SUB, sw2), F32)],
        compiler_params=_cparams(("parallel", "arbitrary")),
    )(dy_all, h_all, hs, waug, vaug, pwc)


def _attn_fwd(qn, qr, kv, kr, H, scale):
    T, S = qn.shape[0], kv.shape[0]
    tq = _pick(T, (256, 128, 64, 32, 16))

    def body(qn_ref, qr_ref, kn_ref, v_ref, kr_ref, o_ref, lse_ref):
        q = jnp.concatenate([qn_ref[...], qr_ref[...]], axis=1)
        k = jnp.concatenate([kn_ref[...], kr_ref[...]], axis=1)
        s = lax.dot_general(q, k, _NT, preferred_element_type=F32) * scale
        m = jnp.max(s, axis=1, keepdims=True)
        p = jnp.exp(s - m)
        l = jnp.sum(p, axis=1, keepdims=True)
        o_ref[...] = jnp.dot((p * (1.0 / l)).astype(BF16), v_ref[...], preferred_element_type=F32).astype(o_ref.dtype)
        lse_ref[0] = m + jnp.log(l)

    q_spec = pl.BlockSpec((tq, LANE), lambda h, i: (i, h))
    return pl.pallas_call(
        body, name="attn_fwd", grid=(H, T // tq),
        in_specs=[q_spec, q_spec, pl.BlockSpec((S, LANE), lambda h, i: (0, h)), pl.BlockSpec((S, LANE), lambda h, i: (0, H + h)),
                  pl.BlockSpec((S, LANE), lambda h, i: (0, 0))],
        out_specs=[q_spec, pl.BlockSpec((1, tq, 1), lambda h, i: (h, i, 0))],
        out_shape=[jax.ShapeDtypeStruct((T, H * LANE), BF16), jax.ShapeDtypeStruct((H, T, 1), F32)],
        compiler_params=_cparams(("parallel", "parallel")),
    )(qn, qr, kv, kv, kr)


def _attn_bwd(qn, qr, kv, kr, do, lse, H, scale):
    T, S = qn.shape[0], kv.shape[0]
    tq = _pick(T, (256, 128, 64, 32, 16))
    nq = T // tq

    def body(qn_ref, qr_ref, kn_ref, v_ref, kr_ref, do_ref, lse_ref, dqn_ref, dqr_ref, dkn_ref, dkr_ref, dv_ref, dk_acc, dv_acc):
        i = pl.program_id(1)
        q = jnp.concatenate([qn_ref[...], qr_ref[...]], axis=1)
        k = jnp.concatenate([kn_ref[...], kr_ref[...]], axis=1)
        v, d_o = v_ref[...], do_ref[...]
        s = lax.dot_general(q, k, _NT, preferred_element_type=F32) * scale
        p = jnp.exp(s - lse_ref[0])
        dv_part = lax.dot_general(p.astype(BF16), d_o, _TN, preferred_element_type=F32)
        dp = lax.dot_general(d_o, v, _NT, preferred_element_type=F32)
        ds = (p * (dp - jnp.sum(p * dp, axis=1, keepdims=True)) * scale).astype(BF16)
        dq = jnp.dot(ds, k, preferred_element_type=F32)
        dqn_ref[...] = dq[:, :LANE].astype(dqn_ref.dtype)
        dqr_ref[...] = dq[:, LANE:].astype(dqr_ref.dtype)
        dk_part = lax.dot_general(ds, q, _TN, preferred_element_type=F32)

        @pl.when(i == 0)
        def _():
            dk_acc[...] = dk_part
            dv_acc[...] = dv_part

        @pl.when(i > 0)
        def _():
            dk_acc[...] += dk_part
            dv_acc[...] += dv_part

        @pl.when(i == nq - 1)
        def _():
            dkn_ref[...] = dk_acc[:, :LANE].astype(dkn_ref.dtype)
            dkr_ref[...] = dk_acc[:, LANE:].astype(dkr_ref.dtype)
            dv_ref[...] = dv_acc[...].astype(dv_ref.dtype)

    q_spec = pl.BlockSpec((tq, LANE), lambda h, i: (i, h))
    k_spec = pl.BlockSpec((S, LANE), lambda h, i: (0, h))
    t_shape, s_shape = jax.ShapeDtypeStruct((T, H * LANE), BF16), jax.ShapeDtypeStruct((S, H * LANE), BF16)
    return pl.pallas_call(
        body, name="attn_bwd", grid=(H, nq),
        in_specs=[q_spec, q_spec, k_spec, pl.BlockSpec((S, LANE), lambda h, i: (0, H + h)),
                  pl.BlockSpec((S, LANE), lambda h, i: (0, 0)), q_spec, pl.BlockSpec((1, tq, 1), lambda h, i: (h, i, 0))],
        out_specs=[q_spec, q_spec, k_spec, k_spec, k_spec], out_shape=[t_shape, t_shape, s_shape, s_shape, s_shape],
        scratch_shapes=[pltpu.VMEM((S, 2 * LANE), F32), pltpu.VMEM((S, LANE), F32)],
        compiler_params=_cparams(("parallel", "arbitrary")),
    )(qn, qr, kv, kv, kr, do, lse)


def _rope_tables(T, heads):
    rows = T // GRID_W
    row = jnp.repeat(jnp.arange(rows, dtype=F32), GRID_W)
    col = jnp.tile(jnp.arange(GRID_W, dtype=F32), rows)
    n_freq = QK_ROPE // 4
    inv = ROPE_BASE ** (-jnp.arange(n_freq, dtype=F32) / n_freq)
    ar, ac = row[:, None] * inv, col[:, None] * inv
    cos = jnp.concatenate([jnp.cos(ar), jnp.cos(ar), jnp.cos(ac), jnp.cos(ac)], axis=1)
    sin = jnp.concatenate([-jnp.sin(ar), jnp.sin(ar), -jnp.sin(ac), jnp.sin(ac)], axis=1)
    pad = lambda t: jnp.tile(jnp.pad(t, ((0, 0), (0, LANE - QK_ROPE))), (1, heads))
    return pad(cos), pad(sin)


def _dw(a, dy, w, name):
    return _mm(a, dy, "tn", BF16, name, out_slots=w.shape[0] if w.ndim == 3 else None)


def _local_step(x, ctx, tgt, m_lat, m_ctx, p, W, goff, hooks=None):
    T, D = x.shape
    Tc = ctx.shape[0]
    S = T + Tc
    S5W = p["s5_d"].shape[1]
    QR, KVR = p["q_norm"].shape[1], p["kv_norm"].shape[1]
    G, N = p["s5_a_re"].shape[1:]
    P = S5_GROUP
    nch = G // 8
    o_cq, o_ckv, o_kr = S5W, S5W + QR, S5W + QR + KVR
    assert o_cq % QR == 0 and o_ckv % KVR == 0 and o_kr % LANE == 0 and goff % D == 0 and S5W % LANE == 0 and G % 8 == 0
    assert 8 * P == LANE
    row = lambda k, m: m[k:k + 1]
    sh1, sc1, g1, sh2, sc2, g2 = (row(k, m_lat) for k in range(6))
    csh1, csc1 = row(0, m_ctx), row(1, m_ctx)
    n1, n2, nf = p["norm1"], p["norm2"], p["norm_f"]

    (xm_lat,) = _rowmap(_normmod, "norm1_lat", T, [x], [n1, sc1, sh1], [(D, BF16)])
    (xm_ctx,) = _rowmap(_normmod, "norm1_ctx", Tc, [ctx], [n1, csc1, csh1], [(D, BF16)])
    xm_all = jnp.concatenate([xm_lat, xm_ctx], axis=0)

    a_re, a_im = p["s5_a_re"][:, :, None, :], p["s5_a_im"][:, :, None, :]
    ldt = p["s5_log_dt"][:, :, None, None]
    b_re, b_im = p["s5_b_re"].transpose(0, 1, 3, 2), p["s5_b_im"].transpose(0, 1, 3, 2)
    wre, wim, vre, vim, pwr, pwi = _s5_tables(a_re, a_im, ldt, b_re, b_im, p["s5_c_re"], p["s5_c_im"])
    waug = _s5_expand(wre, wim, "s5_expand_b")
    vaug = _s5_expand(vre, vim, "s5_expand_c")
    lanes = lambda t: t.reshape(2, SUB + 1, nch, 8 * N).transpose(0, 2, 1, 3)
    pw_re, pw_im = lanes(pwr), lanes(pwi)
    near = lambda t: t[:, :, 1:]
    far = lambda t: t[:, :, :0:-1]
    pw_c = jnp.concatenate([near(pw_re), near(pw_im)], axis=-1)
    pw_a = jnp.concatenate([far(pw_re), far(pw_im)], axis=-1)
    pwc_c = jnp.concatenate([near(pw_re), -near(pw_im)], axis=-1)
    pwc_a = jnp.concatenate([far(pw_re), -far(pw_im)], axis=-1)

    if hooks:
        W = {**W, **hooks["first_weights"](xm_all, vaug)}
    H = W["w_uq"].shape[1] // (2 * LANE)
    h_all = _mm(xm_all, W["w_in"], "nn", F32, "mm_in")
    y0, hs0 = _s5_fwd(h_all, waug, vaug, pw_c[0], S5W, T, 0, True, "s5_scan_fwd0")
    y1, hs1 = _s5_fwd(h_all, waug, vaug, pw_a[1], S5W, T, 1, False, "s5_scan_fwd1")

    def s5_combine(u, yf, yr, dskip):
        y5 = dskip * u + yf + yr
        return y5, jax.nn.gelu(y5)

    y5, z = _rowmap(s5_combine, "s5_combine", T, [(h_all, S5W, 0), y0, y1], [p["s5_d"]], [(S5W, F32), (S5W, BF16)])

    (qn,) = _rowmap(_rms, "q_norm", T, [(h_all, QR, o_cq // QR)], [p["q_norm"]], [(QR, BF16)])
    (kvn,) = _rowmap(_rms, "kv_norm", S, [(h_all, KVR, o_ckv // KVR)], [p["kv_norm"]], [(KVR, BF16)])
    qraw = _mm(qn, W["w_uq"], "nn", F32, "mm_uq")
    kvraw = _mm(kvn, W["w_ukv"], "nn", BF16, "mm_ukv")
    cos_q, sin_q = _rope_tables(T, H)
    padl = lambda t: jnp.pad(t[:, :LANE], ((0, Tc), (0, 0)))
    cos_k = padl(cos_q) + jnp.pad(jnp.ones((Tc, LANE), F32), ((T, 0), (0, 0)))
    sin_k = padl(sin_q)
    hn = H * LANE

    def q_post(q, cos, sin):
        return q[:, :hn], _rope(q[:, hn:], cos, sin)

    q_nope, q_rope = _rowmap(q_post, "q_rope", T, [qraw, cos_q, sin_q], [], [(hn, BF16), (hn, BF16)])
    (kr,) = _rowmap(_rope, "k_rope", S, [(h_all, LANE, o_kr // LANE), cos_k, sin_k], [], [(LANE, BF16)])
    scale = (QK_NOPE + QK_ROPE) ** -0.5
    o, lse = _attn_fwd(q_nope, q_rope, kvraw, kr, H, scale)
    g1_fwd = g1
    if hooks:
        W = {**W, **hooks["mix_weights"](o)}
        g1_fwd = g1 + hooks["ffn_mid"](o)[:1, :1]

    zz = _mm(z, W["w_glu"], "nn", BF16, "mm_glu")
    br_mla = _mm(o, W["w_mla_o"], "nn", BF16, "mm_mla_o")

    def merge(zz, brm, gs, gm):
        a, b = zz[:, :D], zz[:, D:]
        return jax.nn.sigmoid(gs) * (a * jax.nn.sigmoid(b)) + jax.nn.sigmoid(gm) * brm

    gb = goff // D
    merge_ins = [zz, br_mla, (h_all, D, gb), (h_all, D, gb + 1)]
    (mix,) = _rowmap(merge, "merge", T, merge_ins, [], [(D, BF16)])
    out1 = _mm(mix, W["w_out"], "nn", F32, "mm_out")

    def resid_norm2(x, out1, g1, n2, sc2, sh2):
        x1 = x + g1 * out1
        return x1, _normmod(x1, n2, sc2, sh2)

    x1, hm = _rowmap(resid_norm2, "resid_norm2", T, [x, out1], [g1_fwd, n2, sc2, sh2], [(D, F32), (D, BF16)])

    if hooks:
        W = {**W, **hooks["ffn_weights"](hm)}
    FF = W["w_ffn_out"].shape[0]
    assert FF % LANE == 0
    ab = _mm(hm, W["w_ffn_in"], "nn", BF16, "mm_ffn_in")

    def swiglu_act(a, b):
        return jax.nn.silu(a) * b

    (f,) = _rowmap(swiglu_act, "ffn_act", T, [(ab, FF, 0), (ab, FF, 1)], [], [(FF, BF16)])
    out2 = _mm(f, W["w_ffn_out"], "nn", F32, "mm_ffn_out")

    def loss_rows(x1, out2, g2, nf, tgt):
        y = _rms(x1 + g2 * out2, nf)
        return 0.5 * jnp.sum(jnp.mean(jnp.square(y - tgt), axis=-1))

    def final(x1, out2, tgt, g2, nf):
        val, (dx1, dout2, dg2, dnf) = jax.value_and_grad(loss_rows, argnums=(0, 1, 2, 3))(x1, out2, g2, nf, tgt)
        return dx1, dout2, jnp.full((1, LANE), val, F32), dg2, dnf

    dx2, dout2, loss_acc, dg2, dnf = _rowmap(final, "final_loss", T, [x1, out2, tgt], [g2, nf],
                                             [(D, F32), (D, BF16)], [LANE, D, D])

    gW = {}
    df = _mm(dout2, W["w_ffn_out"], "nt", BF16, "mm_ffn_out_dx")
    gW["w_ffn_out"] = _dw(f, dout2, W["w_ffn_out"], "mm_ffn_out_dw")

    def swiglu_bwd(a, b, df):
        _, vjp = jax.vjp(swiglu_act, a, b)
        da, db = vjp(df)
        return jnp.concatenate([da, db], axis=1)

    (dab,) = _rowmap(swiglu_bwd, "ffn_act_bwd", T, [(ab, FF, 0), (ab, FF, 1), df], [], [(2 * FF, BF16)])
    dhm = _mm(dab, W["w_ffn_in"], "nt", F32, "mm_ffn_in_dx")
    gW["w_ffn_in"] = _dw(hm, dab, W["w_ffn_in"], "mm_ffn_in_dw")
    if hooks:
        token = hooks["send_grads"](FFN, [gW.pop(n) for n in FFN])
        g1 = g1 if token is None else g1 + token[:1, :1]

    def resid_norm2_bwd(x, out1, dx2, dhm, g1, n2, sc2, sh2):
        _, vjp = jax.vjp(resid_norm2, x, out1, g1, n2, sc2, sh2)
        dx, dout1, dg1, dn2, dsc2, dsh2 = vjp((dx2, dhm))
        return dx, dout1, dg1, dn2, dsc2, dsh2

    dx1, dout1, dg1, dn2, dsc2, dsh2 = _rowmap(resid_norm2_bwd, "resid_norm2_bwd", T, [x, out1, dx2, dhm],
                                               [g1, n2, sc2, sh2], [(D, F32), (D, BF16)], [D, D, D, D])

    dmix = _mm(dout1, W["w_out"], "nt", BF16, "mm_out_dx")
    gW["w_out"] = _dw(mix, dout1, W["w_out"], "mm_out_dw")

    def merge_bwd(zz, brm, gs, gm, dmix):
        _, vjp = jax.vjp(merge, zz, brm, gs, gm)
        dzz, dbrm, dgs, dgm = vjp(dmix)
        return dzz, dbrm, jnp.concatenate([dgs, dgm], axis=1)

    dzz, dbrm, dgates = _rowmap(merge_bwd, "merge_bwd", T, merge_ins + [dmix], [],
                                [(2 * D, BF16), (D, BF16), (2 * D, BF16)])
    do = _mm(dbrm, W["w_mla_o"], "nt", BF16, "mm_mla_o_dx")
    gW["w_mla_o"] = _dw(o, dbrm, W["w_mla_o"], "mm_mla_o_dw")
    dz = _mm(dzz, W["w_glu"], "nt", BF16, "mm_glu_dx")
    gW["w_glu"] = _dw(z, dzz, W["w_glu"], "mm_glu_dw")
    d_skip_w = p["s5_d"]
    if hooks:
        token = hooks["send_grads"](MIX, [gW.pop(n) for n in MIX])
        d_skip_w = d_skip_w if token is None else d_skip_w + token[:1, :1]

    def s5_combine_bwd(u, y5, dz, dskip):
        _, vjp = jax.vjp(lambda y: jax.nn.gelu(y), y5)
        (dy5,) = vjp(dz)
        return dy5, jnp.sum(dy5 * u, axis=0, keepdims=True)

    dy5, d_skip = _rowmap(s5_combine_bwd, "s5_combine_bwd", T, [(h_all, S5W, 0), y5, dz], [d_skip_w], [(S5W, F32)], [S5W])

    dq_nope, dq_rope, dk_nope, dkr_heads, dv = _attn_bwd(q_nope, q_rope, kvraw, kr, do, lse, H, scale)

    def q_post_bwd(dqn, dqr, cos, sin):
        return jnp.concatenate([dqn, _rope_bwd(dqr, cos, sin)], axis=1)

    (dqraw,) = _rowmap(q_post_bwd, "q_rope_bwd", T, [dq_nope, dq_rope, cos_q, sin_q], [], [(2 * hn, BF16)])
    dkvraw = jnp.concatenate([dk_nope, dv], axis=1)

    def k_rope_bwd(dkh, cos, sin):
        d = dkh[:, :LANE]
        for h in range(1, H):
            d = d + dkh[:, h * LANE:(h + 1) * LANE]
        return _rope_bwd(d, cos, sin)

    (dkr,) = _rowmap(k_rope_bwd, "k_rope_bwd", S, [dkr_heads, cos_k, sin_k], [], [(LANE, BF16)])
    dqn = _mm(dqraw, W["w_uq"], "nt", F32, "mm_uq_dx")
    gW["w_uq"] = _dw(qn, dqraw, W["w_uq"], "mm_uq_dw")
    dkvn = _mm(dkvraw, W["w_ukv"], "nt", F32, "mm_ukv_dx")
    gW["w_ukv"] = _dw(kvn, dkvraw, W["w_ukv"], "mm_ukv_dw")

    def rms_bwd(cx, dn, g):
        _, vjp = jax.vjp(_rms, cx, g)
        return vjp(dn)

    dcq, dq_norm = _rowmap(rms_bwd, "q_norm_bwd", T, [(h_all, QR, o_cq // QR), dqn], [p["q_norm"]], [(QR, BF16)], [QR])
    dckv, dkv_norm = _rowmap(rms_bwd, "kv_norm_bwd", S, [(h_all, KVR, o_ckv // KVR), dkvn], [p["kv_norm"]],
                             [(KVR, BF16)], [KVR])

    dy_all = jnp.concatenate([dy5, jnp.zeros((Tc, S5W), F32)], axis=0)
    du0, dbb0, dc0, da0 = _s5_bwd(dy_all, h_all, hs0, waug, vaug, pwc_a[0], S5W, T, 0, True, "s5_scan_bwd0")
    du1, dbb1, dc1, da1 = _s5_bwd(dy_all, h_all, hs1, waug, vaug, pwc_c[1], S5W, T, 1, False, "s5_scan_bwd1")

    def du_combine(a, b, dy, dskip):
        return a + b + dskip * dy

    (du_all,) = _rowmap(du_combine, "s5_du", S, [du0, du1, dy_all], [p["s5_d"]], [(S5W, BF16)])
    dbb = jnp.einsum("dsgpcgn->dcsgpn", jnp.stack([dbb0, dbb1]).reshape(2, nch, 8, P, 2, 8, N)).reshape(2, 2, G, P, N)
    dcm = jnp.einsum("dscgngp->dcsgpn", jnp.stack([dc0, dc1]).reshape(2, nch, 2, 8, N, 8, P)).reshape(2, 2, G, P, N)
    da = jnp.stack([da0, da1]).reshape(2, nch, 2, 8, N).transpose(0, 2, 1, 3, 4).reshape(2, 2, G, 1, N)
    d_lr, d_li, d_ldt, d_br, d_bi = _s5_param_bwd(a_re, a_im, ldt, b_re, b_im, da[:, 0], da[:, 1], dbb[:, 0], dbb[:, 1])

    lat_only = lambda t: jnp.pad(t, ((0, Tc), (0, 0)))
    dh_all = jnp.concatenate([du_all, lat_only(dcq), dckv, dkr, jnp.zeros((S, goff - o_kr - LANE), BF16), lat_only(dgates)],
                             axis=1)
    dxm = _mm(dh_all, W["w_in"], "nt", F32, "mm_in_dx")
    gW["w_in"] = _dw(xm_all, dh_all, W["w_in"], "mm_in_dw")

    def norm1_bwd(x, dxm, dx1, n1, sc, sh):
        _, vjp = jax.vjp(_normmod, x, n1, sc, sh)
        dx, dn, dsc, dsh = vjp(dxm)
        return dx + dx1, dn, dsc, dsh

    grad_x, dn1_l, dsc1, dsh1 = _rowmap(norm1_bwd, "norm1_lat_bwd", T, [x, dxm, dx1], [n1, sc1, sh1], [(D, F32)], [D, D, D])

    def norm1_ctx_bwd(x, dxm, n1, sc, sh):
        _, vjp = jax.vjp(_normmod, x, n1, sc, sh)
        return vjp(dxm)[1:]

    dn1_c, dcsc1, dcsh1 = _rowmap(norm1_ctx_bwd, "norm1_ctx_bwd", Tc, [ctx, dxm[T:]], [n1, csc1, csh1], [], [D, D, D])

    zero = jnp.zeros((1, D), F32)
    dm_lat = jnp.concatenate([dsh1, dsc1, dg1, dsh2, dsc2, dg2], axis=0)
    dm_ctx = jnp.concatenate([dcsh1, dcsc1, zero, zero, zero, zero], axis=0)
    small = {
        "norm1": dn1_l + dn1_c, "norm2": dn2, "norm_f": dnf, "q_norm": dq_norm, "kv_norm": dkv_norm, "s5_d": d_skip,
        "s5_a_re": d_lr, "s5_a_im": d_li, "s5_log_dt": d_ldt, "s5_b_re": d_br.transpose(0, 1, 3, 2),
        "s5_b_im": d_bi.transpose(0, 1, 3, 2), "s5_c_re": dcm[:, 0], "s5_c_im": -dcm[:, 1],
    }
    return loss_acc[:, :1], grad_x, small, dm_lat, dm_ctx, gW


BIG = ("w_in", "w_uq", "w_ukv", "w_glu", "w_mla_o", "w_out", "w_ffn_in", "w_ffn_out")
FFN = ("w_ffn_in", "w_ffn_out")
MIX = ("w_out", "w_mla_o", "w_glu")
ROW_SHARDED = ("w_out", "w_ffn_out")
RELAID = ("w_in", "w_uq", "w_ukv")
SMALL = ("c_ctx", "b_mod", "norm1", "norm2", "s5_a_re", "s5_a_im", "s5_log_dt", "s5_b_re", "s5_b_im", "s5_c_re",
         "s5_c_im", "s5_d", "q_norm", "kv_norm", "norm_f")
S5_BULK = ("s5_b_re", "s5_b_im", "s5_c_re", "s5_c_im")
WEIGHTS = ("c_ctx", "w_mod", "b_mod", "norm1", "norm2", "w_in", "s5_a_re", "s5_a_im", "s5_log_dt", "s5_b_re", "s5_b_im",
           "s5_c_re", "s5_c_im", "s5_d", "w_glu", "q_norm", "kv_norm", "w_uq", "w_ukv", "w_mla_o", "w_out", "w_ffn_in",
           "w_ffn_out", "norm_f")


def _heads_split(w, heads, first):
    k = w.shape[0]
    w3 = w.reshape(k, heads, -1)
    return jnp.concatenate([w3[:, :, :first].reshape(k, -1), w3[:, :, first:].reshape(k, -1)], axis=1)


def _uq_layout(w, heads):
    k = w.shape[0]
    w3 = w.reshape(k, heads, QK_NOPE + QK_ROPE)
    rope = jnp.pad(w3[:, :, QK_NOPE:], ((0, 0), (0, 0), (0, LANE - QK_ROPE)))
    return jnp.concatenate([w3[:, :, :QK_NOPE].reshape(k, -1), rope.reshape(k, -1)], axis=1)


def _uq_unlayout(w, heads):
    k = w.shape[0]
    nope = w[:, :heads * QK_NOPE].reshape(k, heads, QK_NOPE)
    rope = w[:, heads * QK_NOPE:].reshape(k, heads, LANE)[:, :, :QK_ROPE]
    return jnp.concatenate([nope, rope], axis=2).reshape(k, -1)


def _heads_merge(w, heads, first):
    k = w.shape[0]
    a, b = w[:, :heads * first].reshape(k, heads, first), w[:, heads * first:].reshape(k, heads, -1)
    return jnp.concatenate([a, b], axis=2).reshape(k, -1)


def _cols_full(w8):
    return w8.transpose(1, 0, 2).reshape(w8.shape[1], -1)


def _cols_slots(w):
    return w.reshape(w.shape[0], N_DEV, -1).transpose(1, 0, 2)


def _weight_layout(n, w8):
    if n in ROW_SHARDED:
        return w8.reshape(-1, w8.shape[-1])
    return _cols_full(w8) if (n in RELAID or w8.shape[-1] % LANE) else w8


def _grad_slots(n, g):
    if g.ndim == 3:
        return g
    return g.reshape(N_DEV, g.shape[0] // N_DEV, g.shape[1]) if n in ROW_SHARDED else _cols_slots(g)


def _gate_offset(in_cols, D):
    return -(-(in_cols - 2 * D) // D) * D


def _model_weights(g8, D):
    W = {n: _weight_layout(n, w8) for n, w8 in g8.items()}
    w_in = W["w_in"]
    n_front = w_in.shape[1] - 2 * D
    goff = _gate_offset(w_in.shape[1], D)
    W["w_in"] = jnp.concatenate([w_in[:, :n_front], jnp.zeros((D, goff - n_front), w_in.dtype), w_in[:, n_front:]], axis=1)
    heads = W["w_uq"].shape[1] // (QK_NOPE + QK_ROPE)
    W["w_uq"] = _uq_layout(W["w_uq"], heads)
    W["w_ukv"] = _heads_split(W["w_ukv"], heads, QK_NOPE)
    return W, goff


def kernel(x, c, ctx, c_ctx, w_mod, b_mod, norm1, norm2, w_in, s5_a_re, s5_a_im, s5_log_dt, s5_b_re, s5_b_im, s5_c_re, s5_c_im, s5_d, w_glu, q_norm, kv_norm, w_uq, w_ukv, w_mla_o, w_out, w_ffn_in, w_ffn_out, norm_f, loss_target, m_c_ctx, m_w_mod, m_b_mod, m_norm1, m_norm2, m_w_in, m_s5_a_re, m_s5_a_im, m_s5_log_dt, m_s5_b_re, m_s5_b_im, m_s5_c_re, m_s5_c_im, m_s5_d, m_w_glu, m_q_norm, m_kv_norm, m_w_uq, m_w_ukv, m_w_mla_o, m_w_out, m_w_ffn_in, m_w_ffn_out, m_norm_f, v_c_ctx, v_w_mod, v_b_mod, v_norm1, v_norm2, v_w_in, v_s5_a_re, v_s5_a_im, v_s5_log_dt, v_s5_b_re, v_s5_b_im, v_s5_c_re, v_s5_c_im, v_s5_d, v_w_glu, v_q_norm, v_kv_norm, v_w_uq, v_w_ukv, v_w_mla_o, v_w_out, v_w_ffn_in, v_w_ffn_out, v_norm_f):
    a = dict(locals())
    D = x.shape[-1]
    me = 4 * lax.axis_index("x") + 2 * lax.axis_index("y") + lax.axis_index("c")

    shard = {n: a[n][0] for n in BIG}
    first = [n for n in BIG if n not in FFN + MIX]
    (cg,) = _all_gather([jnp.broadcast_to(c, (8, D))], "ag_c")
    goff = _gate_offset(w_in.shape[-1] * N_DEV, D)

    wm = w_mod[0]
    ncol = wm.shape[1]
    c16 = jnp.concatenate([cg[:, 0, :], c_ctx[None], jnp.zeros((7, D), F32)], axis=0)
    (s16,) = _rowmap(jax.nn.silu, "mod_silu", 16, [c16], [], [(D, BF16)])
    m_cols = _mm(s16, wm, "nn", F32, "mm_mod")
    (mg,) = _all_gather([m_cols], "ag_mod")
    (m16,) = _rowmap(lambda m, b: m + b, "mod_bias", 16, [_cols_full(mg)], [b_mod], [(N_DEV * ncol, F32)])

    first_blocks = [shard[n].astype(BF16) for n in first]
    fst = {}
    fst["sems1"], fst["thru"], first_token = _ag2_start(first_blocks, [_own_slot(b, me) for b in first_blocks], "ag_first_start")

    def first_weights(after_norm, after_tables):
        sems2, thru, _ = _ag2_mid(fst["sems1"], fst["thru"], after_tables, "ag_first_mid")
        lands = _ag2_end(fst["sems1"], sems2, thru, after_norm, "ag_first_end")
        return _model_weights(dict(zip(first, lands)), D)[0]

    mix_blocks = [shard[n].astype(BF16) for n in MIX]
    mix = _xchg_start(mix_blocks, [_own_slot(b, me) for b in mix_blocks], False, "ag_mix_start")
    ffn_blocks = [shard[n].astype(BF16) for n in FFN]
    ffn = {}
    ffn["sems1"], ffn["thru"], ag_token = _ag2_start(ffn_blocks, [_own_slot(b, me) for b in ffn_blocks], "ag_ffn_start")
    m16 = m16 + (first_token[:1, :1] + mix[3][:1, :1] + ag_token[:1, :1])

    def mix_weights(after):
        lands = _xchg_wait(mix[0], mix[1], mix[2], after, False, "ag_mix_wait")
        return {n: _weight_layout(n, w8) for n, w8 in zip(MIX, lands)}

    def ffn_mid(after):
        ffn["sems2"], ffn["thru"], token = _ag2_mid(ffn["sems1"], ffn["thru"], after, "ag_ffn_mid")
        return token

    def ffn_weights(after):
        lands = _ag2_end(ffn["sems1"], ffn["sems2"], ffn["thru"], after, "ag_ffn_end")
        return {n: _weight_layout(n, w8) for n, w8 in zip(FFN, lands)}

    rs_async = {}

    def send_grads(names, gs):
        slots = [_grad_slots(n, g) for n, g in zip(names, gs)]
        lands = [_own_slot(lax.dynamic_index_in_dim(s, me, 0, keepdims=False), me) for s in slots]
        rs_async[names] = _xchg_start(slots, lands, True, "rs_start_" + names[0])
        return rs_async[names][3]

    m_lat = lax.dynamic_slice(m16, (me, 0), (1, 6 * D)).reshape(6, D)
    m_ctx = m16[8].reshape(6, D)

    p = {n: a[n][0] for n in ("norm1", "norm2", "s5_a_re", "s5_a_im", "s5_log_dt", "s5_b_re", "s5_b_im", "s5_c_re",
                              "s5_c_im", "q_norm", "kv_norm")}
    p = {k: (v[None] if v.ndim == 1 else v) for k, v in p.items()}
    p["s5_d"] = s5_d.reshape(1, -1)
    p["norm_f"] = norm_f[None]
    hooks = dict(first_weights=first_weights, mix_weights=mix_weights, ffn_mid=ffn_mid, ffn_weights=ffn_weights,
                 send_grads=send_grads)
    loss_part, grad_x, small, dm_lat, dm_ctx, gW = _local_step(x[0], ctx[0], loss_target[0], m_lat, m_ctx, p, {}, goff, hooks)
    loss = lax.psum(loss_part[0, 0], ("x", "y", "c"))

    dm8 = jnp.concatenate([dm_lat.reshape(1, -1), dm_ctx.reshape(1, -1), jnp.zeros((SUB - 2, 6 * D), F32)], axis=0)
    (dmg,) = _all_gather([dm8], "ag_dmod")
    dm_sum = _sum_slots(dmg, "sum_dmod")
    dM16 = jnp.concatenate([dmg[:, 0, :], dm_sum[1:2], jnp.zeros((7, 6 * D), F32)], axis=0)
    (g_b_mod,) = _rowmap(lambda d: jnp.sum(d, axis=0, keepdims=True), "b_mod_grad", 16, [dM16], [], [], [6 * D])
    dM_loc = lax.dynamic_slice(dM16, (0, me * ncol), (16, ncol))
    g_w_mod = _mm(s16, dM_loc, "tn", F32, "mm_mod_dw")
    ds16_part = _mm(dM_loc, wm, "nt", F32, "mm_mod_dx")

    fine = [n for n in SMALL if n not in ("c_ctx", "b_mod") + S5_BULK]
    sg, sgb = _all_gather([_pack_rows([small[n] for n in fine] + [ds16_part[8:9]], F32),
                           _pack_rows([small[n] for n in S5_BULK], BF16)], "ag_small")
    parts = _unpack_rows(_sum_slots(sg, "sum_small"), [small[n].shape for n in fine] + [(1, D)])
    grads = dict(zip(fine, parts[:-1]))
    grads.update(zip(S5_BULK, _unpack_rows(_sum_slots(sgb, "sum_small_bulk"), [small[n].shape for n in S5_BULK])))

    def silu_bwd(cc, ds):
        _, vjp = jax.vjp(jax.nn.silu, cc)
        return vjp(ds)[0]

    (g_c_ctx,) = _rowmap(silu_bwd, "c_ctx_grad", 1, [c_ctx[None], parts[-1]], [], [(D, F32)])
    grads["c_ctx"], grads["b_mod"] = g_c_ctx, g_b_mod

    gW = dict(gW)
    n_front = w_in.shape[-1] * N_DEV - 2 * D
    gW["w_in"] = jnp.concatenate([gW["w_in"][:, :n_front], gW["w_in"][:, goff:]], axis=1)
    heads = gW["w_uq"].shape[1] // (2 * LANE)
    gW["w_uq"] = _uq_unlayout(gW["w_uq"], heads)
    gW["w_ukv"] = _heads_merge(gW["w_ukv"], heads, QK_NOPE)
    last = [n for n in BIG if n in gW]
    slots = [_grad_slots(n, gW[n]) for n in last]
    from_sibling = _rs_pair(slots, "rs_pair")
    chip_sums = [_add_pair(pp, rr, "rs_add_" + n) for n, pp, rr in zip(last, slots, from_sibling)]
    my_chip = 2 * lax.axis_index("x") + lax.axis_index("y")
    lands = [_own_slot(lax.dynamic_index_in_dim(q, my_chip, 0, keepdims=False), my_chip, N_CHIP) for q in chip_sums]
    rs_send, rs_recv, rs_thru, behind = _xchg_start(chip_sums, lands, True, "rs_chips_start")
    for names, (send, recv, thru, _) in rs_async.items():
        for n, g8 in zip(names, _xchg_wait(send, recv, thru, behind, True, "rs_wait_" + names[0])):
            grads[n] = _sum_slots(g8, "rs_sum_" + n)
    grads["w_mod"] = g_w_mod

    out = {}

    def adamw_big(n, after):
        d, nm, nv = _adamw(a[n][0], grads[n], a["m_" + n][0], a["v_" + n][0], "adamw_" + n, after)
        for k, val in (("grad_", grads[n]), ("delta_", d), ("new_m_", nm), ("new_v_", nv)):
            out[k + n] = val.reshape(a[n].shape)
        return nv

    for n in FFN + MIX + ("w_mod",):
        behind = adamw_big(n, behind)
    packs = [_pack_rows([t[n] for n in SMALL], F32) for t in (
        {n: a[n] for n in SMALL}, {n: grads[n] for n in SMALL}, {n: a["m_" + n] for n in SMALL}, {n: a["v_" + n] for n in SMALL})]
    res = _adamw(*packs, "adamw_small", behind)
    for n, g4 in zip(last, _xchg_wait(rs_send, rs_recv, rs_thru, res[2], True, "rs_chips_wait")):
        grads[n] = _sum_slots(g4, "rs_sum_" + n)
        adamw_big(n, None)
    shapes = [a[n].shape for n in SMALL]
    for k, packed in (("grad_", packs[1]), ("delta_", res[0]), ("new_m_", res[1]), ("new_v_", res[2])):
        for n, val in zip(SMALL, _unpack_rows(packed, shapes)):
            out[k + n] = val
    return (loss, grad_x[None]) + tuple(out[k + n] for k in ("grad_", "delta_", "new_m_", "new_v_") for n in WEIGHTS)
```

```python
import functools
import math

import jax
import jax.numpy as jnp
from jax import lax
from jax.experimental import pallas as pl
from jax.experimental.pallas import tpu as pltpu

F32 = jnp.float32
BF16 = jnp.bfloat16

N_DEV = 8
N_CHIP = 4
EPS = 1e-6
GRID_W = 64
S5_GROUP = 16
QK_NOPE, QK_ROPE, V_DIM = 128, 64, 128
ROPE_BASE = 10000.0
ADAM_LR, ADAM_B1, ADAM_B2, ADAM_EPS, ADAM_WD, ADAM_STEP = 0.001, 0.9, 0.999, 1e-08, 0.01, 10

LANE = 128
SUB = 8
PACK_W = 1024
PACK_ROWS = 32
VMEM_LIMIT = 48 << 20
ROWMAP_TILE_BYTES = 20 << 20
MM_VMEM_BUDGET = 36 << 20
MESH = pl.DeviceIdType.MESH
_NT = (((1,), (1,)), ((), ()))
_TN = (((0,), (0,)), ((), ()))


def _pick(dim, cands):
    for c in cands:
        if dim % c == 0:
            return c
    return dim


def _cparams(sem):
    return pltpu.CompilerParams(dimension_semantics=sem, vmem_limit_bytes=VMEM_LIMIT)


def _mm(a, b, dims, out_dtype, name, out_slots=None):
    a = a.astype(BF16)
    b = b.astype(BF16)
    b3 = b.ndim == 3
    if dims == "nn":
        (M, K), N = a.shape, (b.shape[0] * b.shape[2] if b3 else b.shape[1])
    elif dims == "nt":
        M, N = a.shape[0], b.shape[-2]
        K = b.shape[0] * b.shape[2] if b3 else b.shape[1]
    else:
        (K, M), N = a.shape, b.shape[1]
    unit_n = b.shape[2] if (b3 and dims == "nn") else (N // out_slots if out_slots else N)
    unit_k = b.shape[2] if (b3 and dims == "nt") else K
    osz = jnp.dtype(out_dtype).itemsize
    tm, tn, tk = _mm_tiles(M, unit_n, unit_k, osz, LANE if dims == "tn" else 16)
    nk, npt, kpt = K // tk, unit_n // tn, unit_k // tk
    use_acc = nk > 1 and out_dtype != F32
    if dims == "nn":
        a_spec = pl.BlockSpec((tm, tk), lambda i, j, k: (i, k))
        b_spec = (pl.BlockSpec((None, tk, tn), lambda i, j, k: (j // npt, k, j % npt)) if b3
                  else pl.BlockSpec((tk, tn), lambda i, j, k: (k, j)))
        dn = (((1,), (0,)), ((), ()))
    elif dims == "nt":
        a_spec = pl.BlockSpec((tm, tk), lambda i, j, k: (i, k))
        b_spec = (pl.BlockSpec((None, tn, tk), lambda i, j, k: (k // kpt, j, k % kpt)) if b3
                  else pl.BlockSpec((tn, tk), lambda i, j, k: (j, k)))
        dn = _NT
    else:
        a_spec = pl.BlockSpec((tk, tm), lambda i, j, k: (k, i))
        b_spec = pl.BlockSpec((tk, tn), lambda i, j, k: (k, j))
        dn = _TN
    if out_slots:
        out_spec = pl.BlockSpec((None, tm, tn), lambda i, j, k: (j // npt, i, j % npt))
        out_shape = jax.ShapeDtypeStruct((out_slots, M, unit_n), out_dtype)
    else:
        out_spec = pl.BlockSpec((tm, tn), lambda i, j, k: (i, j))
        out_shape = jax.ShapeDtypeStruct((M, N), out_dtype)

    def body(a_ref, b_ref, o_ref, *scratch):
        part = lax.dot_general(a_ref[...], b_ref[...], dn, preferred_element_type=F32)
        if nk == 1:
            o_ref[...] = part.astype(o_ref.dtype)
            return
        acc_ref = scratch[0] if use_acc else o_ref
        k = pl.program_id(2)

        @pl.when(k == 0)
        def _():
            acc_ref[...] = part

        @pl.when(k > 0)
        def _():
            acc_ref[...] += part

        if use_acc:
            @pl.when(k == nk - 1)
            def _():
                o_ref[...] = acc_ref[...].astype(o_ref.dtype)

    return pl.pallas_call(
        body, name=name, grid=(M // tm, N // tn, nk),
        in_specs=[a_spec, b_spec], out_specs=out_spec, out_shape=out_shape,
        scratch_shapes=[pltpu.VMEM((tm, tn), F32)] if use_acc else [],
        compiler_params=_cparams(("parallel", "parallel", "arbitrary")),
    )(a, b)


def _divisors(n, mult, cap):
    d = [t for t in range(mult, min(n, cap) + 1, mult) if n % t == 0]
    return d[::-1] or [n]


def _mm_tiles(M, unit_n, unit_k, out_itemsize, tm_mult):
    best = None
    for tk in _divisors(unit_k, LANE, 2816):
        for tn in _divisors(unit_n, LANE, 1536):
            for tm in _divisors(M, tm_mult, 1024):
                vmem = 2 * 2 * (tm * tk + tk * tn) + 2 * tm * tn * out_itemsize + 4 * tm * tn * (2 if unit_k > tk else 1)
                if vmem > MM_VMEM_BUDGET:
                    continue
                steps = (M // tm) * (unit_n // tn) * (unit_k // tk)
                key = (steps, -tk, -tn)
                if best is None or key < best[0]:
                    best = (key, (tm, tn, tk))
                break
    return best[1]


def _rowmap(fn, name, M, row_ins, bc_ins, row_outs, acc_outs=(), after=None):
    row_ins = [r if isinstance(r, tuple) else (r, r.shape[1], 0) for r in row_ins]
    row_bytes = sum(w * a.dtype.itemsize for a, w, _ in row_ins) + sum(w * jnp.dtype(d).itemsize for w, d in row_outs)
    widest = max([w for _, w, _ in row_ins] + [w for w, _ in row_outs])
    row_bytes = 2 * row_bytes + 6 * 4 * widest
    tm = _pick(M, [t for t in (512, 256, 128, 64, 32, 16) if t * row_bytes <= ROWMAP_TILE_BYTES] + [16])
    n_in, n_row, n_acc = len(row_ins) + len(bc_ins), len(row_outs), len(acc_outs)

    def body(*refs):
        res = fn(*[r[...].astype(F32) for r in refs[:n_in]])
        res = res if isinstance(res, (tuple, list)) else (res,)
        outs = refs[n_in + (after is not None):]
        for k in range(n_row):
            outs[k][...] = res[k].astype(outs[k].dtype)
        if n_acc:
            @pl.when(pl.program_id(0) == 0)
            def _():
                for k in range(n_acc):
                    outs[n_row + k][...] = jnp.zeros_like(outs[n_row + k])

            for k in range(n_acc):
                outs[n_row + k][...] += res[n_row + k].astype(F32)

    in_specs = [pl.BlockSpec((tm, w), functools.partial(lambda i, blk: (i, blk), blk=blk)) for _, w, blk in row_ins]
    in_specs += [pl.BlockSpec(b.shape, lambda i: (0, 0)) for b in bc_ins]
    in_specs += [pl.BlockSpec(memory_space=pl.ANY)] * (after is not None)
    out_specs = [pl.BlockSpec((tm, w), lambda i: (i, 0)) for w, _ in row_outs]
    out_specs += [pl.BlockSpec((1, w), lambda i: (0, 0)) for w in acc_outs]
    out_shape = [jax.ShapeDtypeStruct((M, w), d) for w, d in row_outs]
    out_shape += [jax.ShapeDtypeStruct((1, w), F32) for w in acc_outs]
    return pl.pallas_call(
        body, name=name, grid=(M // tm,), in_specs=in_specs, out_specs=out_specs, out_shape=out_shape,
        compiler_params=_cparams(("arbitrary",) if n_acc else ("parallel",)),
    )(*[a for a, _, _ in row_ins], *bc_ins, *([] if after is None else [after]))


def _rms(x, g):
    return x * lax.rsqrt(jnp.mean(x * x, axis=-1, keepdims=True) + EPS) * g


def _normmod(x, g, sc, sh):
    return _rms(x, g) * (1.0 + sc) + sh


def _swap16(v):
    w = v.shape[1]
    lane = lax.broadcasted_iota(jnp.int32, v.shape, 1)
    return jnp.where((lane // 16) % 2 == 0, pltpu.roll(v, w - 16, 1), pltpu.roll(v, 16, 1))


def _rope(v, cos, sin_signed):
    return v * cos + _swap16(v) * sin_signed


def _rope_bwd(d, cos, sin_signed):
    return d * cos + _swap16(d * sin_signed)


def _mesh_pos():
    return lax.axis_index("x"), lax.axis_index("y"), lax.axis_index("c")


def _hbm_call(body, name, ins, out_shapes, n_sems):
    any_spec = pl.BlockSpec(memory_space=pl.ANY)
    return pl.pallas_call(
        body, name=name, out_shape=out_shapes, in_specs=[any_spec] * len(ins), out_specs=[any_spec] * len(out_shapes),
        scratch_shapes=[pltpu.SemaphoreType.DMA((n_sems,)), pltpu.SemaphoreType.DMA((n_sems,)),
                        pltpu.SemaphoreType.DMA((len(ins),))],
    )(*ins)


def _all_gather(xs, name):
    n = len(xs)

    def body(*refs):
        x_refs, out_refs, (send_sems, recv_sems, local_sems) = refs[:n], refs[n:2 * n], refs[2 * n:]
        x, y, c = _mesh_pos()
        me, sibling = (x, y, c), (x, y, 1 - c)
        chips = [(1 - x, y), (x, 1 - y), (1 - x, 1 - y)]
        locals_, first, passed, arrivals = [], [], [], []
        for a in range(n):
            def slot(px, py, pc, a=a):
                return out_refs[a].at[4 * px + 2 * py + pc]

            def copy(k, block, to, src=None, a=a, slot=slot):
                return pltpu.make_async_remote_copy(
                    src_ref=slot(*block) if src is None else src, dst_ref=slot(*block),
                    send_sem=send_sems.at[7 * a + k], recv_sem=recv_sems.at[7 * a + k], device_id=to, device_id_type=MESH)

            locals_.append(pltpu.make_async_copy(x_refs[a], slot(*me), local_sems.at[a]))
            first.append(copy(0, me, sibling, src=x_refs[a]))
            first += [copy(1 + j, me, (*chip, c), src=x_refs[a]) for j, chip in enumerate(chips)]
            passed.append([copy(4 + j, (*chip, c), sibling) for j, chip in enumerate(chips)])
            arrivals.append([copy(1 + j, (*chip, c), me) for j, chip in enumerate(chips)]
                            + [copy(0, sibling, me)] + [copy(4 + j, (*chip, 1 - c), me) for j, chip in enumerate(chips)])
        for cp in locals_ + first:
            cp.start()
        for j in range(3):
            for a in range(n):
                arrivals[a][j].wait_recv()
                passed[a][j].start()
        for a in range(n):
            for cp in arrivals[a][3:]:
                cp.wait_recv()
        for cp in first + [p for ps in passed for p in ps]:
            cp.wait_send()
        for cp in locals_:
            cp.wait()

    return _hbm_call(body, name, xs, [jax.ShapeDtypeStruct((N_DEV,) + x.shape, x.dtype) for x in xs], 7 * n)


def _rs_pair(ps, name):
    n = len(ps)

    def body(*refs):
        p_refs, out_refs, (send_sems, recv_sems, _) = refs[:n], refs[n:2 * n], refs[2 * n:]
        x, y, c = _mesh_pos()
        sends, recvs = [], []
        for a in range(n):
            for q in range(N_CHIP):
                sem = dict(send_sem=send_sems.at[4 * a + q], recv_sem=recv_sems.at[4 * a + q],
                           device_id=(x, y, 1 - c), device_id_type=MESH)
                sends.append(pltpu.make_async_remote_copy(src_ref=p_refs[a].at[2 * q + 1 - c], dst_ref=out_refs[a].at[q], **sem))
                recvs.append(pltpu.make_async_remote_copy(src_ref=p_refs[a].at[2 * q + c], dst_ref=out_refs[a].at[q], **sem))
        for cp in sends:
            cp.start()
        for cp in recvs:
            cp.wait_recv()
        for cp in sends:
            cp.wait_send()

    return _hbm_call(body, name, ps, [jax.ShapeDtypeStruct((N_CHIP,) + p.shape[1:], p.dtype) for p in ps], 4 * n)


def _xchg_copies(src_refs, land_refs, send_sems, recv_sems, slot_src):
    x, y, c = _mesh_pos()
    sends, recvs = [], []
    for a, (src, land) in enumerate(zip(src_refs, land_refs)):
        chips = land.shape[0] == N_CHIP
        npeer = land.shape[0] - 1
        me = 2 * x + y if chips else 4 * x + 2 * y + c
        for r in range(1, npeer + 1):
            px = 1 - x if r & (2 if chips else 4) else x
            py = 1 - y if r & (1 if chips else 2) else y
            pc = c if chips else (1 - c if r & 1 else c)
            peer = 2 * px + py if chips else 4 * px + 2 * py + pc
            sem = dict(send_sem=send_sems.at[npeer * a + r - 1], recv_sem=recv_sems.at[npeer * a + r - 1],
                       device_id=(px, py, pc), device_id_type=MESH)
            s = src.at[peer] if slot_src else src
            sends.append(pltpu.make_async_remote_copy(src_ref=s, dst_ref=land.at[me], **sem))
            recvs.append(pltpu.make_async_remote_copy(src_ref=s, dst_ref=land.at[peer], **sem))
    return sends, recvs


_HBM = pl.BlockSpec(memory_space=pltpu.HBM)
_SEM = pl.BlockSpec(memory_space=pltpu.SEMAPHORE)
_EFFECT = pltpu.SideEffectType.DATAFLOW_SIDE_EFFECTING


def _xchg_start(srcs, lands, slot_src, name):
    n = len(srcs)

    def body(*refs):
        sends, _ = _xchg_copies(refs[:n], refs[n:2 * n], refs[2 * n], refs[2 * n + 1], slot_src)
        for cp in sends:
            cp.start()
        refs[-1][...] = jnp.zeros_like(refs[-1])

    bufs = list(srcs) + list(lands)
    n_sems = n * (lands[0].shape[0] - 1)
    res = pl.pallas_call(
        body, name=name,
        out_shape=(pltpu.SemaphoreType.DMA((n_sems,)), pltpu.SemaphoreType.DMA((n_sems,)))
        + tuple(pltpu.HBM(b.shape, b.dtype) for b in bufs) + (jax.ShapeDtypeStruct((SUB, LANE), F32),),
        in_specs=(_HBM,) * (2 * n), out_specs=(_SEM, _SEM) + (_HBM,) * (2 * n) + (pl.BlockSpec(memory_space=pltpu.VMEM),),
        input_output_aliases={i: 2 + i for i in range(2 * n)},
        compiler_params=pltpu.CompilerParams(has_side_effects=_EFFECT),
    )(*[pltpu.with_memory_space_constraint(b, pltpu.HBM) for b in bufs])
    return res[0], res[1], res[2:-1], res[-1]


def _xchg_wait(send_sems, recv_sems, thru, after, slot_src, name):
    n = len(thru) // 2

    def body(*refs):
        sends, recvs = _xchg_copies(refs[:n], refs[n:2 * n], refs[2 * n], refs[2 * n + 1], slot_src)
        for cp in sends:
            cp.wait_send()
        for cp in recvs:
            cp.wait_recv()

    res = pl.pallas_call(
        body, name=name, out_shape=tuple(pltpu.HBM(b.shape, b.dtype) for b in thru),
        in_specs=(_HBM,) * (2 * n) + (_SEM, _SEM, pl.BlockSpec(memory_space=pl.ANY)), out_specs=(_HBM,) * (2 * n),
        input_output_aliases={i: i for i in range(2 * n)},
        compiler_params=pltpu.CompilerParams(has_side_effects=_EFFECT),
    )(*thru, send_sems, recv_sems, after)
    return res[n:]


def _ag2_copy(land, sems, k, block, to, src=None):
    slot = land.at[4 * block[0] + 2 * block[1] + block[2]]
    return pltpu.make_async_remote_copy(src_ref=slot if src is None else src, dst_ref=slot, send_sem=sems[0].at[k],
                                        recv_sem=sems[1].at[k], device_id=to, device_id_type=MESH)


def _ag2_start(blocks, lands, name):
    n = len(blocks)

    def body(*refs):
        x, y, c = _mesh_pos()
        for a in range(n):
            sems = (refs[2 * n], refs[2 * n + 1])
            _ag2_copy(refs[n + a], sems, 4 * a, (x, y, c), (x, y, 1 - c), src=refs[a]).start()
            for j, chip in enumerate([(1 - x, y), (x, 1 - y), (1 - x, 1 - y)]):
                _ag2_copy(refs[n + a], sems, 4 * a + 1 + j, (x, y, c), (*chip, c), src=refs[a]).start()
        refs[-1][...] = jnp.zeros_like(refs[-1])

    bufs = list(blocks) + list(lands)
    res = pl.pallas_call(
        body, name=name,
        out_shape=(pltpu.SemaphoreType.DMA((4 * n,)), pltpu.SemaphoreType.DMA((4 * n,)))
        + tuple(pltpu.HBM(b.shape, b.dtype) for b in bufs) + (jax.ShapeDtypeStruct((SUB, LANE), F32),),
        in_specs=(_HBM,) * (2 * n), out_specs=(_SEM, _SEM) + (_HBM,) * (2 * n) + (pl.BlockSpec(memory_space=pltpu.VMEM),),
        input_output_aliases={i: 2 + i for i in range(2 * n)},
        compiler_params=pltpu.CompilerParams(has_side_effects=_EFFECT),
    )(*[pltpu.with_memory_space_constraint(b, pltpu.HBM) for b in bufs])
    return (res[0], res[1]), res[2:-1], res[-1]


def _ag2_mid(sems1, thru, after, name):
    n = len(thru) // 2

    def body(*refs):
        x, y, c = _mesh_pos()
        s1, s2 = (refs[2 * n], refs[2 * n + 1]), (refs[2 * n + 3], refs[2 * n + 4])
        for j, chip in enumerate([(1 - x, y), (x, 1 - y), (1 - x, 1 - y)]):
            for a in range(n):
                _ag2_copy(refs[n + a], s1, 4 * a + 1 + j, (*chip, c), (x, y, c)).wait_recv()
                _ag2_copy(refs[n + a], s2, 3 * a + j, (*chip, c), (x, y, 1 - c)).start()
        refs[-1][...] = jnp.zeros_like(refs[-1])

    res = pl.pallas_call(
        body, name=name,
        out_shape=(pltpu.SemaphoreType.DMA((3 * n,)), pltpu.SemaphoreType.DMA((3 * n,)))
        + tuple(pltpu.HBM(b.shape, b.dtype) for b in thru) + (jax.ShapeDtypeStruct((SUB, LANE), F32),),
        in_specs=(_HBM,) * (2 * n) + (_SEM, _SEM, pl.BlockSpec(memory_space=pl.ANY)),
        out_specs=(_SEM, _SEM) + (_HBM,) * (2 * n) + (pl.BlockSpec(memory_space=pltpu.VMEM),),
        input_output_aliases={i: 2 + i for i in range(2 * n)},
        compiler_params=pltpu.CompilerParams(has_side_effects=_EFFECT),
    )(*thru, *sems1, after)
    return (res[0], res[1]), res[2:-1], res[-1]


def _ag2_end(sems1, sems2, thru, after, name):
    n = len(thru) // 2

    def body(*refs):
        x, y, c = _mesh_pos()
        s1, s2 = (refs[2 * n], refs[2 * n + 1]), (refs[2 * n + 2], refs[2 * n + 3])
        chips = [(1 - x, y), (x, 1 - y), (1 - x, 1 - y)]
        for a in range(n):
            land = refs[n + a]
            _ag2_copy(land, s1, 4 * a, (x, y, c), (x, y, 1 - c), src=refs[a]).wait_send()
            _ag2_copy(land, s1, 4 * a, (x, y, 1 - c), (x, y, c)).wait_recv()
            for j, chip in enumerate(chips):
                _ag2_copy(land, s1, 4 * a + 1 + j, (x, y, c), (*chip, c), src=refs[a]).wait_send()
                _ag2_copy(land, s2, 3 * a + j, (*chip, c), (x, y, 1 - c)).wait_send()
                _ag2_copy(land, s2, 3 * a + j, (*chip, 1 - c), (x, y, c)).wait_recv()

    res = pl.pallas_call(
        body, name=name, out_shape=tuple(pltpu.HBM(b.shape, b.dtype) for b in thru),
        in_specs=(_HBM,) * (2 * n) + (_SEM,) * 4 + (pl.BlockSpec(memory_space=pl.ANY),), out_specs=(_HBM,) * (2 * n),
        input_output_aliases={i: i for i in range(2 * n)},
        compiler_params=pltpu.CompilerParams(has_side_effects=_EFFECT),
    )(*thru, *sems1, *sems2, after)
    return res[n:]


def _own_slot(block, me, slots=N_DEV):
    return lax.dynamic_update_slice(lax.empty((slots,) + block.shape, block.dtype), block[None], (me, 0, 0))


def _add_pair(p, r, name, after):
    _, R, C = p.shape
    tr = _pick(R, (512, 256, 128, 64, 32, 16))

    def body(c_ref, p_ref, r_ref, after_ref, o_ref):
        o_ref[...] = (p_ref[...].astype(F32) + r_ref[...].astype(F32)).astype(o_ref.dtype)

    return pl.pallas_call(
        body, name=name, out_shape=jax.ShapeDtypeStruct((N_CHIP, R, C), p.dtype),
        grid_spec=pltpu.PrefetchScalarGridSpec(
            num_scalar_prefetch=1, grid=(N_CHIP, R // tr),
            in_specs=[pl.BlockSpec((None, None, tr, C), lambda q, i, c_ref: (q, c_ref[0], i, 0)),
                      pl.BlockSpec((None, tr, C), lambda q, i, c_ref: (q, i, 0)), pl.BlockSpec(memory_space=pl.ANY)],
            out_specs=pl.BlockSpec((None, tr, C), lambda q, i, c_ref: (q, i, 0))),
        compiler_params=_cparams(("parallel", "parallel")),
    )(lax.axis_index("c").reshape(1).astype(jnp.int32), p.reshape(N_CHIP, 2, R, C), r, after)


def _sum_slots(g, name):
    ns, R, C = g.shape
    tr = _pick(R, (256, 128, 64, 32, 16))

    def body(g_ref, o_ref):
        acc = g_ref[0].astype(F32)
        for j in range(1, ns):
            acc = acc + g_ref[j].astype(F32)
        o_ref[...] = acc

    return pl.pallas_call(
        body, name=name, grid=(R // tr,),
        in_specs=[pl.BlockSpec((ns, tr, C), lambda i: (0, i, 0))], out_specs=pl.BlockSpec((tr, C), lambda i: (i, 0)),
        out_shape=jax.ShapeDtypeStruct((R, C), F32), compiler_params=_cparams(("parallel",)),
    )(g)


def _pack_rows(arrs, dtype):
    parts = []
    for a in arrs:
        flat = a.reshape(-1).astype(dtype)
        pad = (-flat.shape[0]) % (PACK_W * 16)
        parts.append(jnp.pad(flat, (0, pad)).reshape(-1, PACK_W))
    out = jnp.concatenate(parts, axis=0)
    return jnp.pad(out, ((0, (-out.shape[0]) % PACK_ROWS), (0, 0)))


def _packed_rows(shape):
    n = math.prod(shape)
    return (n + PACK_W * 16 - 1) // (PACK_W * 16) * 16


def _unpack_rows(packed, shapes):
    out, r0 = [], 0
    for s in shapes:
        rows, n = _packed_rows(s), math.prod(s)
        out.append(packed[r0:r0 + rows].reshape(rows * PACK_W)[:n].reshape(s))
        r0 += rows
    return out


def _adamw_math(w, g, m, v):
    m = ADAM_B1 * m + (1.0 - ADAM_B1) * g
    v = ADAM_B2 * v + (1.0 - ADAM_B2) * (g * g)
    m_hat = m / (1.0 - ADAM_B1 ** ADAM_STEP)
    v_hat = v / (1.0 - ADAM_B2 ** ADAM_STEP)
    delta = -ADAM_LR * (m_hat / (jnp.sqrt(v_hat) + ADAM_EPS) + ADAM_WD * w)
    return delta, m, v


def _adamw(w, g, m, v, name, after=None):
    R, C = w.shape
    return _rowmap(_adamw_math, name, R, [w, g, m, v], [], [(C, F32)] * 3, after=after)


def _s5_disc_math(lr, li, ldt, br, bi):
    dt = jnp.exp(ldt)
    mag = jnp.exp(lr * dt)
    ab_re, ab_im = mag * jnp.cos(li * dt), mag * jnp.sin(li * dt)
    den = lr * lr + li * li
    nr, ni = ab_re - 1.0, ab_im
    co_re = (nr * lr + ni * li) / den
    co_im = (ni * lr - nr * li) / den
    bb_re = co_re * br - co_im * bi
    bb_im = co_re * bi + co_im * br
    return ab_re, ab_im, bb_re, bb_im


def _s5_tables(a_re, a_im, ldt, b_re, b_im, c_re, c_im):
    _, G, P, N = b_re.shape
    nch = G // 8

    def body(lr_ref, li_ref, ldt_ref, br_ref, bi_ref, cr_ref, ci_ref, wre, wim, vre, vim, pwr, pwi):
        ar, ai, bb_re, bb_im = _s5_disc_math(lr_ref[0], li_ref[0], ldt_ref[0], br_ref[0], bi_ref[0])
        cr, ci = cr_ref[0], ci_ref[0]
        pr, pi = jnp.ones_like(ar), jnp.zeros_like(ar)
        for j in range(SUB + 1):
            pwr[0, j], pwi[0, j] = pr, pi
            if j < SUB:
                tabs = ((wre, bb_re * pr - bb_im * pi), (wim, bb_re * pi + bb_im * pr),
                        (vre, cr * pr - ci * pi), (vim, -(cr * pi + ci * pr)))
                for ref, val in tabs:
                    for s in range(nch):
                        ref[0, s, pl.ds(j * LANE, LANE), :] = val[s * 8:(s + 1) * 8].reshape(LANE, N).astype(BF16)
            pr, pi = pr * ar - pi * ai, pr * ai + pi * ar

    g1n = pl.BlockSpec((1, G, 1, N), lambda d: (d, 0, 0, 0))
    gpn = pl.BlockSpec((1, G, P, N), lambda d: (d, 0, 0, 0))
    tab = pl.BlockSpec((1, nch, SUB * LANE, N), lambda d: (d, 0, 0, 0))
    pw = pl.BlockSpec((1, SUB + 1, G, 1, N), lambda d: (d, 0, 0, 0, 0))
    s_tab = jax.ShapeDtypeStruct((2, nch, SUB * LANE, N), BF16)
    s_pw = jax.ShapeDtypeStruct((2, SUB + 1, G, 1, N), F32)
    return pl.pallas_call(
        body, name="s5_tables", grid=(2,),
        in_specs=[g1n, g1n, pl.BlockSpec((1, G, 1, 1), lambda d: (d, 0, 0, 0)), gpn, gpn, gpn, gpn],
        out_specs=[tab] * 4 + [pw] * 2, out_shape=[s_tab] * 4 + [s_pw] * 2,
        compiler_params=_cparams(("parallel",)),
    )(a_re, a_im, ldt, b_re, b_im, c_re, c_im)


def _s5_expand(t_re, t_im, name):
    _, nch, R, N = t_re.shape
    sw = 8 * N

    def body(re_ref, im_ref, o_ref):
        spread = (lax.broadcasted_iota(jnp.int32, (N, sw), 1) % N == lax.broadcasted_iota(jnp.int32, (N, sw), 0)).astype(BF16)
        row_g = (lax.broadcasted_iota(jnp.int32, (R, sw), 0) % LANE) // S5_GROUP
        keep = row_g == lax.broadcasted_iota(jnp.int32, (R, sw), 1) // N
        for half, ref in enumerate((re_ref, im_ref)):
            t = jnp.dot(ref[0, 0], spread, preferred_element_type=F32)
            o_ref[0, 0, :, pl.ds(half * sw, sw)] = jnp.where(keep, t, 0.0).astype(BF16)

    spec = pl.BlockSpec((1, 1, R, N), lambda d, s: (d, s, 0, 0))
    return pl.pallas_call(
        body, name=name, grid=(2, nch), in_specs=[spec, spec],
        out_specs=pl.BlockSpec((1, 1, R, 2 * sw), lambda d, s: (d, s, 0, 0)),
        out_shape=jax.ShapeDtypeStruct((2, nch, R, 2 * sw), BF16), compiler_params=_cparams(("parallel", "parallel")),
    )(t_re, t_im)


def _s5_param_bwd(a_re, a_im, ldt, b_re, b_im, da_re, da_im, dbb_re, dbb_im):
    _, G, P, N = b_re.shape

    def body(lr_ref, li_ref, ldt_ref, br_ref, bi_ref, dar, dai, dbr, dbi, o_lr, o_li, o_ldt, o_br, o_bi):
        _, vjp = jax.vjp(_s5_disc_math, lr_ref[0], li_ref[0], ldt_ref[0], br_ref[0], bi_ref[0])
        o_lr[0], o_li[0], o_ldt[0], o_br[0], o_bi[0] = vjp((dar[0], dai[0], dbr[0], dbi[0]))

    g1n = pl.BlockSpec((1, G, 1, N), lambda d: (d, 0, 0, 0))
    g11 = pl.BlockSpec((1, G, 1, 1), lambda d: (d, 0, 0, 0))
    gpn = pl.BlockSpec((1, G, P, N), lambda d: (d, 0, 0, 0))
    s_g1n, s_g11, s_gpn = (jax.ShapeDtypeStruct(s, F32) for s in ((2, G, 1, N), (2, G, 1, 1), (2, G, P, N)))
    return pl.pallas_call(
        body, name="s5_param_bwd", grid=(2,),
        in_specs=[g1n, g1n, g11, gpn, gpn, g1n, g1n, gpn, gpn], out_specs=[g1n, g1n, g11, gpn, gpn],
        out_shape=[s_g1n, s_g1n, s_g11, s_gpn, s_gpn], compiler_params=_cparams(("parallel",)),
    )(a_re, a_im, ldt, b_re, b_im, da_re, da_im, dbb_re, dbb_im)


def _tile_local_scan(u, w_ref, back):
    tb, sw2 = u.shape[0], w_ref.shape[1]
    sw, half = sw2 // 2, LANE // 2
    tau = lax.broadcasted_iota(jnp.int32, u.shape, 0) % SUB
    low = lax.broadcasted_iota(jnp.int32, u.shape, 1) < half
    parts = [u]
    for j in range(1, SUB):
        if back:
            parts.append(jnp.where(tau >= j, pltpu.roll(u, j, 0), 0.0))
        else:
            parts.append(jnp.where(tau <= SUB - 1 - j, pltpu.roll(u, tb - j, 0), 0.0))
    out = [None] * 4
    for h in range(2):
        pieces = [jnp.where(low, a, pltpu.roll(b, half, 1)) if h == 0 else jnp.where(low, pltpu.roll(a, half, 1), b)
                  for a, b in zip(parts[0::2], parts[1::2])]
        lhs = jnp.concatenate(pieces, axis=1).astype(BF16)
        rows = jnp.concatenate([w_ref[pl.ds(j * LANE + h * half, half), :] for j in range(SUB)], axis=0)
        for part in range(2):
            cols = rows[:, part * sw + h * (sw // 2):part * sw + (h + 1) * (sw // 2)]
            out[2 * part + h] = jnp.dot(lhs, cols, preferred_element_type=F32)
    return jnp.concatenate(out, axis=1)


def _cmul_add(tile, pw, carry, sw):
    pr, pi, cr, ci = pw[:, :sw], pw[:, sw:], carry[:, :sw], carry[:, sw:]
    return tile + jnp.concatenate([pr * cr - pi * ci, pr * ci + pi * cr], axis=1)


def _tile_scan(buf, base, ntile, pw, carry, sw, causal):
    def step(k, c):
        i = k if causal else ntile - 1 - k
        r = pl.multiple_of(base + i * SUB, SUB)
        tile = _cmul_add(buf[pl.ds(r, SUB), :], pw, c, sw)
        buf[pl.ds(r, SUB), :] = tile
        return tile[SUB - 1:SUB, :] if causal else tile[0:1, :]

    return lax.fori_loop(0, ntile, step, carry)


def _s5_fwd(h_all, waug, vaug, pw, S5W, T, d, causal, name):
    S = h_all.shape[0]
    _, nch, _, sw2 = waug.shape
    sw = sw2 // 2
    tb = _pick(math.gcd(T, S - T), (256, 128, 64, 32, 16))
    ntile, nt, off = tb // SUB, S // tb, T // tb
    rb = (lambda s, t: ((t + off) % nt, s)) if causal else (lambda s, t: (nt - 1 - t, s))

    def body(u_ref, w_ref, v_ref, p_ref, y_ref, h_ref, hblk, carry):
        @pl.when(pl.program_id(1) == 0)
        def _():
            carry[...] = jnp.zeros_like(carry)

        hblk[...] = _tile_local_scan(u_ref[...], w_ref, causal)
        carry[...] = _tile_scan(hblk, 0, ntile, p_ref[...], carry[...], sw, causal)
        hb = hblk[...].astype(BF16)
        h_ref[...] = hb
        y_ref[...] = lax.dot_general(hb, v_ref[...], _NT, preferred_element_type=F32)

    return pl.pallas_call(
        body, name=name, grid=(nch, nt),
        in_specs=[pl.BlockSpec((tb, LANE), rb),
                  pl.BlockSpec((None, None, SUB * LANE, sw2), lambda s, t: (d, s, 0, 0)),
                  pl.BlockSpec((None, None, LANE, sw2), lambda s, t: (d, s, 0, 0)),
                  pl.BlockSpec((None, SUB, sw2), lambda s, t: (s, 0, 0))],
        out_specs=[pl.BlockSpec((tb, LANE), rb), pl.BlockSpec((tb, sw2), rb)],
        out_shape=[jax.ShapeDtypeStruct((S, S5W), F32), jax.ShapeDtypeStruct((S, nch * sw2), BF16)],
        scratch_shapes=[pltpu.VMEM((tb, sw2), F32), pltpu.VMEM((1, sw2), F32)],
        compiler_params=_cparams(("parallel", "arbitrary")),
    )(h_all, waug, vaug, pw)


def _s5_bwd(dy_all, h_all, hs, waug, vaug, pwc, S5W, T, d, causal, name):
    S = h_all.shape[0]
    _, nch, _, sw2 = waug.shape
    sw = sw2 // 2
    tb = _pick(math.gcd(T, S - T), (256, 128, 64, 32, 16))
    ntile, nt, off = tb // SUB, S // tb, T // tb
    rb = (lambda s, t: ((nt - 1 - t + off) % nt, s)) if causal else (lambda s, t: (t, s))
    adj_causal = not causal
    edge = SUB - 1 if adj_causal else SUB + tb
    keep_src, keep_dst = (tb, 0) if adj_causal else (SUB, SUB + tb)

    def body(dy_ref, u_ref, h_ref, w_ref, v_ref, p_ref, du_ref, dbb_ref, dc_ref, da_ref, lam):
        @pl.when(pl.program_id(1) == 0)
        def _():
            lam[pl.ds(0, SUB), :] = jnp.zeros((SUB, sw2), F32)
            lam[pl.ds(SUB + tb, SUB), :] = jnp.zeros((SUB, sw2), F32)
            dbb_ref[...] = jnp.zeros_like(dbb_ref)
            dc_ref[...] = jnp.zeros_like(dc_ref)
            da_ref[...] = jnp.zeros_like(da_ref)

        dy = dy_ref[...]
        lam[pl.ds(SUB, tb), :] = _tile_local_scan(dy, v_ref, adj_causal)
        _tile_scan(lam, SUB, ntile, p_ref[...], lam[pl.ds(edge, 1), :], sw, adj_causal)
        lb = lam[pl.ds(SUB, tb), :].astype(BF16)
        du_ref[...] = lax.dot_general(lb, w_ref[...], _NT, preferred_element_type=F32)
        dbb_ref[...] += lax.dot_general(u_ref[...].astype(BF16), lb, _TN, preferred_element_type=F32)
        dc_ref[...] += lax.dot_general(h_ref[...], dy.astype(BF16), _TN, preferred_element_type=F32)
        h = h_ref[...].astype(F32)
        ln = lam[pl.ds(SUB + 1 if causal else SUB - 1, tb), :]
        hr, hi, lr, li = h[:, :sw], h[:, sw:], ln[:, :sw], ln[:, sw:]
        da_ref[...] += jnp.concatenate([jnp.sum(hr * lr + hi * li, axis=0, keepdims=True),
                                        jnp.sum(hr * li - hi * lr, axis=0, keepdims=True)], axis=1)
        lam[pl.ds(keep_dst, SUB), :] = lam[pl.ds(keep_src, SUB), :]

    fixed = lambda s, t: (s, 0, 0)
    return pl.pallas_call(
        body, name=name, grid=(nch, nt),
        in_specs=[pl.BlockSpec((tb, LANE), rb), pl.BlockSpec((tb, LANE), rb), pl.BlockSpec((tb, sw2), rb),
                  pl.BlockSpec((None, None, LANE, sw2), lambda s, t: (d, s, 0, 0)),
                  pl.BlockSpec((None, None, SUB * LANE, sw2), lambda s, t: (d, s, 0, 0)),
                  pl.BlockSpec((None, SUB, sw2), fixed)],
        out_specs=[pl.BlockSpec((tb, LANE), rb), pl.BlockSpec((None, LANE, sw2), fixed),
                   pl.BlockSpec((None, sw2, LANE), fixed), pl.BlockSpec((None, 1, sw2), fixed)],
        out_shape=[jax.ShapeDtypeStruct((S, S5W), F32), jax.ShapeDtypeStruct((nch, LANE, sw2), F32),
                   jax.ShapeDtypeStruct((nch, sw2, LANE), F32), jax.ShapeDtypeStruct((nch, 1, sw2), F32)],
        scratch_shapes=[pltpu.VMEM((tb + 2 * SUB, sw2), F32)],
        compiler_params=_cparams(("parallel", "arbitrary")),
    )(dy_all, h_all, hs, waug, vaug, pwc)


def _attn_fwd(qn, qr, kv, kr, H, scale):
    T, S = qn.shape[0], kv.shape[0]
    tq = _pick(T, (512, 256, 128, 64, 32, 16))

    def body(qn_ref, qr_ref, kn_ref, v_ref, kr_ref, o_ref, lse_ref):
        q = jnp.concatenate([qn_ref[...], qr_ref[...]], axis=1)
        k = jnp.concatenate([kn_ref[...], kr_ref[...]], axis=1)
        s = lax.dot_general(q, k, _NT, preferred_element_type=F32) * scale
        m = jnp.max(s, axis=1, keepdims=True)
        p = jnp.exp(s - m)
        l = jnp.sum(p, axis=1, keepdims=True)
        o_ref[...] = jnp.dot((p * (1.0 / l)).astype(BF16), v_ref[...], preferred_element_type=F32).astype(o_ref.dtype)
        lse_ref[0] = m + jnp.log(l)

    q_spec = pl.BlockSpec((tq, LANE), lambda h, i: (i, h))
    return pl.pallas_call(
        body, name="attn_fwd", grid=(H, T // tq),
        in_specs=[q_spec, q_spec, pl.BlockSpec((S, LANE), lambda h, i: (0, h)), pl.BlockSpec((S, LANE), lambda h, i: (0, H + h)),
                  pl.BlockSpec((S, LANE), lambda h, i: (0, 0))],
        out_specs=[q_spec, pl.BlockSpec((1, tq, 1), lambda h, i: (h, i, 0))],
        out_shape=[jax.ShapeDtypeStruct((T, H * LANE), BF16), jax.ShapeDtypeStruct((H, T, 1), F32)],
        compiler_params=_cparams(("parallel", "parallel")),
    )(qn, qr, kv, kv, kr)


def _attn_bwd(qn, qr, kv, kr, do, lse, H, scale):
    T, S = qn.shape[0], kv.shape[0]
    tq = _pick(T, (512, 256, 128, 64, 32, 16))
    nq = T // tq

    def body(qn_ref, qr_ref, kn_ref, v_ref, kr_ref, do_ref, lse_ref, dqn_ref, dqr_ref, dkn_ref, dkr_ref, dv_ref, dk_acc, dv_acc):
        i = pl.program_id(1)
        q = jnp.concatenate([qn_ref[...], qr_ref[...]], axis=1)
        k = jnp.concatenate([kn_ref[...], kr_ref[...]], axis=1)
        v, d_o = v_ref[...], do_ref[...]
        s = lax.dot_general(q, k, _NT, preferred_element_type=F32) * scale
        p = jnp.exp(s - lse_ref[0])
        dv_part = lax.dot_general(p.astype(BF16), d_o, _TN, preferred_element_type=F32)
        dp = lax.dot_general(d_o, v, _NT, preferred_element_type=F32)
        ds = (p * (dp - jnp.sum(p * dp, axis=1, keepdims=True)) * scale).astype(BF16)
        dq = jnp.dot(ds, k, preferred_element_type=F32)
        dqn_ref[...] = dq[:, :LANE].astype(dqn_ref.dtype)
        dqr_ref[...] = dq[:, LANE:].astype(dqr_ref.dtype)
        dk_part = lax.dot_general(ds, q, _TN, preferred_element_type=F32)

        @pl.when(i == 0)
        def _():
            dk_acc[...] = dk_part
            dv_acc[...] = dv_part

        @pl.when(i > 0)
        def _():
            dk_acc[...] += dk_part
            dv_acc[...] += dv_part

        @pl.when(i == nq - 1)
        def _():
            dkn_ref[...] = dk_acc[:, :LANE].astype(dkn_ref.dtype)
            dkr_ref[...] = dk_acc[:, LANE:].astype(dkr_ref.dtype)
            dv_ref[...] = dv_acc[...].astype(dv_ref.dtype)

    q_spec = pl.BlockSpec((tq, LANE), lambda h, i: (i, h))
    k_spec = pl.BlockSpec((S, LANE), lambda h, i: (0, h))
    t_shape, s_shape = jax.ShapeDtypeStruct((T, H * LANE), BF16), jax.ShapeDtypeStruct((S, H * LANE), BF16)
    return pl.pallas_call(
        body, name="attn_bwd", grid=(H, nq),
        in_specs=[q_spec, q_spec, k_spec, pl.BlockSpec((S, LANE), lambda h, i: (0, H + h)),
                  pl.BlockSpec((S, LANE), lambda h, i: (0, 0)), q_spec, pl.BlockSpec((1, tq, 1), lambda h, i: (h, i, 0))],
        out_specs=[q_spec, q_spec, k_spec, k_spec, k_spec], out_shape=[t_shape, t_shape, s_shape, s_shape, s_shape],
        scratch_shapes=[pltpu.VMEM((S, 2 * LANE), F32), pltpu.VMEM((S, LANE), F32)],
        compiler_params=_cparams(("parallel", "arbitrary")),
    )(qn, qr, kv, kv, kr, do, lse)


def _rope_tables(T):
    rows = T // GRID_W
    row = jnp.repeat(jnp.arange(rows, dtype=F32), GRID_W)
    col = jnp.tile(jnp.arange(GRID_W, dtype=F32), rows)
    n_freq = QK_ROPE // 4
    inv = ROPE_BASE ** (-jnp.arange(n_freq, dtype=F32) / n_freq)
    ar, ac = row[:, None] * inv, col[:, None] * inv
    cos = jnp.concatenate([jnp.cos(ar), jnp.cos(ar), jnp.cos(ac), jnp.cos(ac)], axis=1)
    sin = jnp.concatenate([-jnp.sin(ar), jnp.sin(ar), -jnp.sin(ac), jnp.sin(ac)], axis=1)
    pad = lambda t: jnp.pad(t, ((0, 0), (0, LANE - QK_ROPE)))
    return pad(cos), pad(sin)


def _dw(a, dy, w, name):
    return _mm(a, dy, "tn", BF16, name, out_slots=w.shape[0] if w.ndim == 3 else None)


def _local_step(x, ctx, tgt, m_lat, m_ctx, p, W, goff, hooks=None):
    T, D = x.shape
    Tc = ctx.shape[0]
    S = T + Tc
    S5W = p["s5_d"].shape[1]
    QR, KVR = p["q_norm"].shape[1], p["kv_norm"].shape[1]
    G, N = p["s5_a_re"].shape[1:]
    P = S5_GROUP
    nch = G // 8
    o_cq, o_ckv, o_kr = S5W, S5W + QR, S5W + QR + KVR
    assert o_cq % QR == 0 and o_ckv % KVR == 0 and o_kr % LANE == 0 and goff % D == 0 and S5W % LANE == 0 and G % 8 == 0
    assert 8 * P == LANE
    row = lambda k, m: m[k:k + 1]
    sh1, sc1, g1, sh2, sc2, g2 = (row(k, m_lat) for k in range(6))
    csh1, csc1 = row(0, m_ctx), row(1, m_ctx)
    n1, n2, nf = p["norm1"], p["norm2"], p["norm_f"]

    (xm_lat,) = _rowmap(_normmod, "norm1_lat", T, [x], [n1, sc1, sh1], [(D, BF16)])
    (xm_ctx,) = _rowmap(_normmod, "norm1_ctx", Tc, [ctx], [n1, csc1, csh1], [(D, BF16)])
    xm_all = jnp.concatenate([xm_lat, xm_ctx], axis=0)

    a_re, a_im = p["s5_a_re"][:, :, None, :], p["s5_a_im"][:, :, None, :]
    ldt = p["s5_log_dt"][:, :, None, None]
    b_re, b_im = p["s5_b_re"].transpose(0, 1, 3, 2), p["s5_b_im"].transpose(0, 1, 3, 2)
    wre, wim, vre, vim, pwr, pwi = _s5_tables(a_re, a_im, ldt, b_re, b_im, p["s5_c_re"], p["s5_c_im"])
    waug = _s5_expand(wre, wim, "s5_expand_b")
    vaug = _s5_expand(vre, vim, "s5_expand_c")
    lanes = lambda t: t.reshape(2, SUB + 1, nch, 8 * N).transpose(0, 2, 1, 3)
    pw_re, pw_im = lanes(pwr), lanes(pwi)
    near = lambda t: t[:, :, 1:]
    far = lambda t: t[:, :, :0:-1]
    pw_c = jnp.concatenate([near(pw_re), near(pw_im)], axis=-1)
    pw_a = jnp.concatenate([far(pw_re), far(pw_im)], axis=-1)
    pwc_c = jnp.concatenate([near(pw_re), -near(pw_im)], axis=-1)
    pwc_a = jnp.concatenate([far(pw_re), -far(pw_im)], axis=-1)

    if hooks:
        W = {**W, **hooks["first_weights"](xm_all, vaug)}
    H = W["w_uq"].shape[1] // (2 * LANE)
    h_all = _mm(xm_all, W["w_in"], "nn", F32, "mm_in")
    y0, hs0 = _s5_fwd(h_all, waug, vaug, pw_c[0], S5W, T, 0, True, "s5_scan_fwd0")
    y1, hs1 = _s5_fwd(h_all, waug, vaug, pw_a[1], S5W, T, 1, False, "s5_scan_fwd1")

    def s5_combine(u, yf, yr, dskip):
        y5 = dskip * u + yf + yr
        return y5, jax.nn.gelu(y5)

    y5, z = _rowmap(s5_combine, "s5_combine", T, [(h_all, S5W, 0), y0, y1], [p["s5_d"]], [(S5W, F32), (S5W, BF16)])

    (qn,) = _rowmap(_rms, "q_norm", T, [(h_all, QR, o_cq // QR)], [p["q_norm"]], [(QR, BF16)])
    (kvn,) = _rowmap(_rms, "kv_norm", S, [(h_all, KVR, o_ckv // KVR)], [p["kv_norm"]], [(KVR, BF16)])
    qraw = _mm(qn, W["w_uq"], "nn", F32, "mm_uq")
    kvraw = _mm(kvn, W["w_ukv"], "nn", BF16, "mm_ukv")
    cos_q, sin_q = _rope_tables(T)
    padl = lambda t: jnp.pad(t[:, :LANE], ((0, Tc), (0, 0)))
    cos_k = padl(cos_q) + jnp.pad(jnp.ones((Tc, LANE), F32), ((T, 0), (0, 0)))
    sin_k = padl(sin_q)
    hn = H * LANE

    def q_post(q, cos, sin):
        return q[:, :hn], _rope(q[:, hn:], jnp.tile(cos, (1, H)), jnp.tile(sin, (1, H)))

    q_nope, q_rope = _rowmap(q_post, "q_rope", T, [qraw, cos_q, sin_q], [], [(hn, BF16), (hn, BF16)])
    (kr,) = _rowmap(_rope, "k_rope", S, [(h_all, LANE, o_kr // LANE), cos_k, sin_k], [], [(LANE, BF16)])
    scale = (QK_NOPE + QK_ROPE) ** -0.5
    o, lse = _attn_fwd(q_nope, q_rope, kvraw, kr, H, scale)
    g1_fwd = g1
    if hooks:
        W = {**W, **hooks["mix_weights"](o)}
        g1_fwd = g1 + hooks["ffn_mid"](o)[:1, :1]

    zz = _mm(z, W["w_glu"], "nn", BF16, "mm_glu")
    br_mla = _mm(o, W["w_mla_o"], "nn", BF16, "mm_mla_o")

    def merge(zz, brm, gs, gm):
        a, b = zz[:, :D], zz[:, D:]
        return jax.nn.sigmoid(gs) * (a * jax.nn.sigmoid(b)) + jax.nn.sigmoid(gm) * brm

    gb = goff // D
    merge_ins = [zz, br_mla, (h_all, D, gb), (h_all, D, gb + 1)]
    (mix,) = _rowmap(merge, "merge", T, merge_ins, [], [(D, BF16)])
    out1 = _mm(mix, W["w_out"], "nn", F32, "mm_out")

    def resid_norm2(x, out1, g1, n2, sc2, sh2):
        x1 = x + g1 * out1
        return x1, _normmod(x1, n2, sc2, sh2)

    x1, hm = _rowmap(resid_norm2, "resid_norm2", T, [x, out1], [g1_fwd, n2, sc2, sh2], [(D, F32), (D, BF16)])

    if hooks:
        W = {**W, **hooks["ffn_weights"](hm)}
    FF = W["w_ffn_out"].shape[0]
    assert FF % LANE == 0
    ab = _mm(hm, W["w_ffn_in"], "nn", BF16, "mm_ffn_in")

    def swiglu_act(a, b):
        return jax.nn.silu(a) * b

    (f,) = _rowmap(swiglu_act, "ffn_act", T, [(ab, FF, 0), (ab, FF, 1)], [], [(FF, BF16)])
    out2 = _mm(f, W["w_ffn_out"], "nn", F32, "mm_ffn_out")

    def loss_rows(x1, out2, g2, nf, tgt):
        y = _rms(x1 + g2 * out2, nf)
        return 0.5 * jnp.sum(jnp.mean(jnp.square(y - tgt), axis=-1))

    def final(x1, out2, tgt, g2, nf):
        val, (dx1, dout2, dg2, dnf) = jax.value_and_grad(loss_rows, argnums=(0, 1, 2, 3))(x1, out2, g2, nf, tgt)
        return dx1, dout2, jnp.full((1, LANE), val, F32), dg2, dnf

    dx2, dout2, loss_acc, dg2, dnf = _rowmap(final, "final_loss", T, [x1, out2, tgt], [g2, nf],
                                             [(D, F32), (D, BF16)], [LANE, D, D])

    gW = {}
    df = _mm(dout2, W["w_ffn_out"], "nt", BF16, "mm_ffn_out_dx")
    gW["w_ffn_out"] = _dw(f, dout2, W["w_ffn_out"], "mm_ffn_out_dw")

    def swiglu_bwd(a, b, df):
        _, vjp = jax.vjp(swiglu_act, a, b)
        da, db = vjp(df)
        return jnp.concatenate([da, db], axis=1)

    (dab,) = _rowmap(swiglu_bwd, "ffn_act_bwd", T, [(ab, FF, 0), (ab, FF, 1), df], [], [(2 * FF, BF16)])
    dhm = _mm(dab, W["w_ffn_in"], "nt", F32, "mm_ffn_in_dx")
    gW["w_ffn_in"] = _dw(hm, dab, W["w_ffn_in"], "mm_ffn_in_dw")
    if hooks:
        token = hooks["send_grads"](FFN, [gW.pop(n) for n in FFN])
        g1 = g1 if token is None else g1 + token[:1, :1]

    def resid_norm2_bwd(x, out1, dx2, dhm, g1, n2, sc2, sh2):
        _, vjp = jax.vjp(resid_norm2, x, out1, g1, n2, sc2, sh2)
        dx, dout1, dg1, dn2, dsc2, dsh2 = vjp((dx2, dhm))
        return dx, dout1, dg1, dn2, dsc2, dsh2

    dx1, dout1, dg1, dn2, dsc2, dsh2 = _rowmap(resid_norm2_bwd, "resid_norm2_bwd", T, [x, out1, dx2, dhm],
                                               [g1, n2, sc2, sh2], [(D, F32), (D, BF16)], [D, D, D, D])

    dmix = _mm(dout1, W["w_out"], "nt", BF16, "mm_out_dx")
    gW["w_out"] = _dw(mix, dout1, W["w_out"], "mm_out_dw")

    def merge_bwd(zz, brm, gs, gm, dmix):
        _, vjp = jax.vjp(merge, zz, brm, gs, gm)
        dzz, dbrm, dgs, dgm = vjp(dmix)
        return dzz, dbrm, jnp.concatenate([dgs, dgm], axis=1)

    dzz, dbrm, dgates = _rowmap(merge_bwd, "merge_bwd", T, merge_ins + [dmix], [],
                                [(2 * D, BF16), (D, BF16), (2 * D, BF16)])
    do = _mm(dbrm, W["w_mla_o"], "nt", BF16, "mm_mla_o_dx")
    gW["w_mla_o"] = _dw(o, dbrm, W["w_mla_o"], "mm_mla_o_dw")
    dz = _mm(dzz, W["w_glu"], "nt", BF16, "mm_glu_dx")
    gW["w_glu"] = _dw(z, dzz, W["w_glu"], "mm_glu_dw")
    d_skip_w = p["s5_d"]
    if hooks:
        token = hooks["send_grads"](MIX, [gW.pop(n) for n in MIX])
        d_skip_w = d_skip_w if token is None else d_skip_w + token[:1, :1]

    def s5_combine_bwd(u, y5, dz, dskip):
        _, vjp = jax.vjp(lambda y: jax.nn.gelu(y), y5)
        (dy5,) = vjp(dz)
        return dy5, jnp.sum(dy5 * u, axis=0, keepdims=True)

    dy5, d_skip = _rowmap(s5_combine_bwd, "s5_combine_bwd", T, [(h_all, S5W, 0), y5, dz], [d_skip_w], [(S5W, F32)], [S5W])

    dq_nope, dq_rope, dk_nope, dkr_heads, dv = _attn_bwd(q_nope, q_rope, kvraw, kr, do, lse, H, scale)

    def q_post_bwd(dqn, dqr, cos, sin):
        return jnp.concatenate([dqn, _rope_bwd(dqr, jnp.tile(cos, (1, H)), jnp.tile(sin, (1, H)))], axis=1)

    (dqraw,) = _rowmap(q_post_bwd, "q_rope_bwd", T, [dq_nope, dq_rope, cos_q, sin_q], [], [(2 * hn, BF16)])
    dkvraw = jnp.concatenate([dk_nope, dv], axis=1)

    def k_rope_bwd(dkh, cos, sin):
        d = dkh[:, :LANE]
        for h in range(1, H):
            d = d + dkh[:, h * LANE:(h + 1) * LANE]
        return _rope_bwd(d, cos, sin)

    (dkr,) = _rowmap(k_rope_bwd, "k_rope_bwd", S, [dkr_heads, cos_k, sin_k], [], [(LANE, BF16)])
    dqn = _mm(dqraw, W["w_uq"], "nt", F32, "mm_uq_dx")
    gW["w_uq"] = _dw(qn, dqraw, W["w_uq"], "mm_uq_dw")
    dkvn = _mm(dkvraw, W["w_ukv"], "nt", F32, "mm_ukv_dx")
    gW["w_ukv"] = _dw(kvn, dkvraw, W["w_ukv"], "mm_ukv_dw")

    def rms_bwd(cx, dn, g):
        _, vjp = jax.vjp(_rms, cx, g)
        return vjp(dn)

    dcq, dq_norm = _rowmap(rms_bwd, "q_norm_bwd", T, [(h_all, QR, o_cq // QR), dqn], [p["q_norm"]], [(QR, BF16)], [QR])
    dckv, dkv_norm = _rowmap(rms_bwd, "kv_norm_bwd", S, [(h_all, KVR, o_ckv // KVR), dkvn], [p["kv_norm"]],
                             [(KVR, BF16)], [KVR])

    dy_all = jnp.concatenate([dy5, jnp.zeros((Tc, S5W), F32)], axis=0)
    du0, dbb0, dc0, da0 = _s5_bwd(dy_all, h_all, hs0, waug, vaug, pwc_a[0], S5W, T, 0, True, "s5_scan_bwd0")
    du1, dbb1, dc1, da1 = _s5_bwd(dy_all, h_all, hs1, waug, vaug, pwc_c[1], S5W, T, 1, False, "s5_scan_bwd1")

    def du_combine(a, b, dy, dskip):
        return a + b + dskip * dy

    (du_all,) = _rowmap(du_combine, "s5_du", S, [du0, du1, dy_all], [p["s5_d"]], [(S5W, BF16)])
    dbb = jnp.einsum("dsgpcgn->dcsgpn", jnp.stack([dbb0, dbb1]).reshape(2, nch, 8, P, 2, 8, N)).reshape(2, 2, G, P, N)
    dcm = jnp.einsum("dscgngp->dcsgpn", jnp.stack([dc0, dc1]).reshape(2, nch, 2, 8, N, 8, P)).reshape(2, 2, G, P, N)
    da = jnp.stack([da0, da1]).reshape(2, nch, 2, 8, N).transpose(0, 2, 1, 3, 4).reshape(2, 2, G, 1, N)
    d_lr, d_li, d_ldt, d_br, d_bi = _s5_param_bwd(a_re, a_im, ldt, b_re, b_im, da[:, 0], da[:, 1], dbb[:, 0], dbb[:, 1])

    lat_only = lambda t: jnp.pad(t, ((0, Tc), (0, 0)))
    dh_all = jnp.concatenate([du_all, lat_only(dcq), dckv, dkr, jnp.zeros((S, goff - o_kr - LANE), BF16), lat_only(dgates)],
                             axis=1)
    dxm = _mm(dh_all, W["w_in"], "nt", F32, "mm_in_dx")
    gW["w_in"] = _dw(xm_all, dh_all, W["w_in"], "mm_in_dw")

    def norm1_bwd(x, dxm, dx1, n1, sc, sh):
        _, vjp = jax.vjp(_normmod, x, n1, sc, sh)
        dx, dn, dsc, dsh = vjp(dxm)
        return dx + dx1, dn, dsc, dsh

    grad_x, dn1_l, dsc1, dsh1 = _rowmap(norm1_bwd, "norm1_lat_bwd", T, [x, dxm, dx1], [n1, sc1, sh1], [(D, F32)], [D, D, D])

    def norm1_ctx_bwd(x, dxm, n1, sc, sh):
        _, vjp = jax.vjp(_normmod, x, n1, sc, sh)
        return vjp(dxm)[1:]

    dn1_c, dcsc1, dcsh1 = _rowmap(norm1_ctx_bwd, "norm1_ctx_bwd", Tc, [ctx, dxm[T:]], [n1, csc1, csh1], [], [D, D, D])

    zero = jnp.zeros((1, D), F32)
    dm_lat = jnp.concatenate([dsh1, dsc1, dg1, dsh2, dsc2, dg2], axis=0)
    dm_ctx = jnp.concatenate([dcsh1, dcsc1, zero, zero, zero, zero], axis=0)
    small = {
        "norm1": dn1_l + dn1_c, "norm2": dn2, "norm_f": dnf, "q_norm": dq_norm, "kv_norm": dkv_norm, "s5_d": d_skip,
        "s5_a_re": d_lr, "s5_a_im": d_li, "s5_log_dt": d_ldt, "s5_b_re": d_br.transpose(0, 1, 3, 2),
        "s5_b_im": d_bi.transpose(0, 1, 3, 2), "s5_c_re": dcm[:, 0], "s5_c_im": -dcm[:, 1],
    }
    return loss_acc[:, :1], grad_x, small, dm_lat, dm_ctx, gW


BIG = ("w_in", "w_uq", "w_ukv", "w_glu", "w_mla_o", "w_out", "w_ffn_in", "w_ffn_out")
FFN = ("w_ffn_in", "w_ffn_out")
MIX = ("w_out", "w_mla_o", "w_glu")
ROW_SHARDED = ("w_out", "w_ffn_out")
RELAID = ("w_in", "w_uq", "w_ukv")
SMALL = ("c_ctx", "b_mod", "norm1", "norm2", "s5_a_re", "s5_a_im", "s5_log_dt", "s5_b_re", "s5_b_im", "s5_c_re",
         "s5_c_im", "s5_d", "q_norm", "kv_norm", "norm_f")
S5_BULK = ("s5_b_re", "s5_b_im", "s5_c_re", "s5_c_im")
WEIGHTS = ("c_ctx", "w_mod", "b_mod", "norm1", "norm2", "w_in", "s5_a_re", "s5_a_im", "s5_log_dt", "s5_b_re", "s5_b_im",
           "s5_c_re", "s5_c_im", "s5_d", "w_glu", "q_norm", "kv_norm", "w_uq", "w_ukv", "w_mla_o", "w_out", "w_ffn_in",
           "w_ffn_out", "norm_f")


def _heads_split(w, heads, first):
    k = w.shape[0]
    w3 = w.reshape(k, heads, -1)
    return jnp.concatenate([w3[:, :, :first].reshape(k, -1), w3[:, :, first:].reshape(k, -1)], axis=1)


def _uq_layout(w, heads):
    k = w.shape[0]
    w3 = w.reshape(k, heads, QK_NOPE + QK_ROPE)
    rope = jnp.pad(w3[:, :, QK_NOPE:], ((0, 0), (0, 0), (0, LANE - QK_ROPE)))
    return jnp.concatenate([w3[:, :, :QK_NOPE].reshape(k, -1), rope.reshape(k, -1)], axis=1)


def _uq_unlayout(w, heads):
    k = w.shape[0]
    nope = w[:, :heads * QK_NOPE].reshape(k, heads, QK_NOPE)
    rope = w[:, heads * QK_NOPE:].reshape(k, heads, LANE)[:, :, :QK_ROPE]
    return jnp.concatenate([nope, rope], axis=2).reshape(k, -1)


def _heads_merge(w, heads, first):
    k = w.shape[0]
    a, b = w[:, :heads * first].reshape(k, heads, first), w[:, heads * first:].reshape(k, heads, -1)
    return jnp.concatenate([a, b], axis=2).reshape(k, -1)


def _cols_full(w8):
    return w8.transpose(1, 0, 2).reshape(w8.shape[1], -1)


def _cols_slots(w):
    return w.reshape(w.shape[0], N_DEV, -1).transpose(1, 0, 2)


def _weight_layout(n, w8):
    if n in ROW_SHARDED:
        return w8.reshape(-1, w8.shape[-1])
    return _cols_full(w8) if (n in RELAID or w8.shape[-1] % LANE) else w8


def _grad_slots(n, g):
    if g.ndim == 3:
        return g
    return g.reshape(N_DEV, g.shape[0] // N_DEV, g.shape[1]) if n in ROW_SHARDED else _cols_slots(g)


def _gate_offset(in_cols, D):
    return -(-(in_cols - 2 * D) // D) * D


def _model_weights(g8, D):
    W = {n: _weight_layout(n, w8) for n, w8 in g8.items()}
    w_in = W["w_in"]
    n_front = w_in.shape[1] - 2 * D
    goff = _gate_offset(w_in.shape[1], D)
    W["w_in"] = jnp.concatenate([w_in[:, :n_front], jnp.zeros((D, goff - n_front), w_in.dtype), w_in[:, n_front:]], axis=1)
    heads = W["w_uq"].shape[1] // (QK_NOPE + QK_ROPE)
    W["w_uq"] = _uq_layout(W["w_uq"], heads)
    W["w_ukv"] = _heads_split(W["w_ukv"], heads, QK_NOPE)
    return W, goff


def kernel(x, c, ctx, c_ctx, w_mod, b_mod, norm1, norm2, w_in, s5_a_re, s5_a_im, s5_log_dt, s5_b_re, s5_b_im, s5_c_re, s5_c_im, s5_d, w_glu, q_norm, kv_norm, w_uq, w_ukv, w_mla_o, w_out, w_ffn_in, w_ffn_out, norm_f, loss_target, m_c_ctx, m_w_mod, m_b_mod, m_norm1, m_norm2, m_w_in, m_s5_a_re, m_s5_a_im, m_s5_log_dt, m_s5_b_re, m_s5_b_im, m_s5_c_re, m_s5_c_im, m_s5_d, m_w_glu, m_q_norm, m_kv_norm, m_w_uq, m_w_ukv, m_w_mla_o, m_w_out, m_w_ffn_in, m_w_ffn_out, m_norm_f, v_c_ctx, v_w_mod, v_b_mod, v_norm1, v_norm2, v_w_in, v_s5_a_re, v_s5_a_im, v_s5_log_dt, v_s5_b_re, v_s5_b_im, v_s5_c_re, v_s5_c_im, v_s5_d, v_w_glu, v_q_norm, v_kv_norm, v_w_uq, v_w_ukv, v_w_mla_o, v_w_out, v_w_ffn_in, v_w_ffn_out, v_norm_f):
    a = dict(locals())
    D = x.shape[-1]
    me = 4 * lax.axis_index("x") + 2 * lax.axis_index("y") + lax.axis_index("c")

    shard = {n: a[n][0] for n in BIG}
    first = [n for n in BIG if n not in FFN + MIX]
    (cg,) = _all_gather([jnp.broadcast_to(c, (8, D))], "ag_c")
    goff = _gate_offset(w_in.shape[-1] * N_DEV, D)

    wm = w_mod[0]
    ncol = wm.shape[1]
    c16 = jnp.concatenate([cg[:, 0, :], c_ctx[None], jnp.zeros((7, D), F32)], axis=0)
    (s16,) = _rowmap(jax.nn.silu, "mod_silu", 16, [c16], [], [(D, BF16)])
    m_cols = _mm(s16, wm, "nn", F32, "mm_mod")
    (mg,) = _all_gather([m_cols], "ag_mod")
    (m16,) = _rowmap(lambda m, b: m + b, "mod_bias", 16, [_cols_full(mg)], [b_mod], [(N_DEV * ncol, F32)])

    first_blocks = [shard[n].astype(BF16) for n in first]
    fst = {}
    fst["sems1"], fst["thru"], first_token = _ag2_start(first_blocks, [_own_slot(b, me) for b in first_blocks], "ag_first_start")

    def first_weights(after_norm, after_tables):
        sems2, thru, _ = _ag2_mid(fst["sems1"], fst["thru"], after_tables, "ag_first_mid")
        lands = _ag2_end(fst["sems1"], sems2, thru, after_norm, "ag_first_end")
        return _model_weights(dict(zip(first, lands)), D)[0]

    mix_blocks = [shard[n].astype(BF16) for n in MIX]
    mix = _xchg_start(mix_blocks, [_own_slot(b, me) for b in mix_blocks], False, "ag_mix_start")
    ffn_blocks = [shard[n].astype(BF16) for n in FFN]
    ffn = {}
    ffn["sems1"], ffn["thru"], ag_token = _ag2_start(ffn_blocks, [_own_slot(b, me) for b in ffn_blocks], "ag_ffn_start")
    m16 = m16 + (first_token[:1, :1] + mix[3][:1, :1] + ag_token[:1, :1])

    def mix_weights(after):
        lands = _xchg_wait(mix[0], mix[1], mix[2], after, False, "ag_mix_wait")
        return {n: _weight_layout(n, w8) for n, w8 in zip(MIX, lands)}

    def ffn_mid(after):
        ffn["sems2"], ffn["thru"], token = _ag2_mid(ffn["sems1"], ffn["thru"], after, "ag_ffn_mid")
        return token

    def ffn_weights(after):
        lands = _ag2_end(ffn["sems1"], ffn["sems2"], ffn["thru"], after, "ag_ffn_end")
        return {n: _weight_layout(n, w8) for n, w8 in zip(FFN, lands)}

    rs_async = {}

    def send_grads(names, gs):
        slots = [_grad_slots(n, g) for n, g in zip(names, gs)]
        lands = [_own_slot(lax.dynamic_index_in_dim(s, me, 0, keepdims=False), me) for s in slots]
        rs_async[names] = _xchg_start(slots, lands, True, "rs_start_" + names[0])
        return rs_async[names][3]

    m_lat = lax.dynamic_slice(m16, (me, 0), (1, 6 * D)).reshape(6, D)
    m_ctx = m16[8].reshape(6, D)

    p = {n: a[n][0] for n in ("norm1", "norm2", "s5_a_re", "s5_a_im", "s5_log_dt", "s5_b_re", "s5_b_im", "s5_c_re",
                              "s5_c_im", "q_norm", "kv_norm")}
    p = {k: (v[None] if v.ndim == 1 else v) for k, v in p.items()}
    p["s5_d"] = s5_d.reshape(1, -1)
    p["norm_f"] = norm_f[None]
    hooks = dict(first_weights=first_weights, mix_weights=mix_weights, ffn_mid=ffn_mid, ffn_weights=ffn_weights,
                 send_grads=send_grads)
    loss_part, grad_x, small, dm_lat, dm_ctx, gW = _local_step(x[0], ctx[0], loss_target[0], m_lat, m_ctx, p, {}, goff, hooks)
    loss = lax.psum(loss_part[0, 0], ("x", "y", "c"))

    dm8 = jnp.concatenate([dm_lat.reshape(1, -1), dm_ctx.reshape(1, -1), jnp.zeros((SUB - 2, 6 * D), F32)], axis=0)
    (dmg,) = _all_gather([dm8], "ag_dmod")
    dm_sum = _sum_slots(dmg, "sum_dmod")
    dM16 = jnp.concatenate([dmg[:, 0, :], dm_sum[1:2], jnp.zeros((7, 6 * D), F32)], axis=0)
    (g_b_mod,) = _rowmap(lambda d: jnp.sum(d, axis=0, keepdims=True), "b_mod_grad", 16, [dM16], [], [], [6 * D])
    dM_loc = lax.dynamic_slice(dM16, (0, me * ncol), (16, ncol))
    g_w_mod = _mm(s16, dM_loc, "tn", F32, "mm_mod_dw")
    ds16_part = _mm(dM_loc, wm, "nt", F32, "mm_mod_dx")

    fine = [n for n in SMALL if n not in ("c_ctx", "b_mod") + S5_BULK]
    small_blocks = [_pack_rows([small[n] for n in fine] + [ds16_part[8:9]], F32), _pack_rows([small[n] for n in S5_BULK], BF16)]
    sm_sems1, sm_thru, sm_token = _ag2_start(small_blocks, [_own_slot(b, me) for b in small_blocks], "ag_small_start")
    grads = {"b_mod": g_b_mod}

    def small_grads(after):
        sems2, thru, token = _ag2_mid(sm_sems1, sm_thru, after, "ag_small_mid")
        sg, sgb = _ag2_end(sm_sems1, sems2, thru, token, "ag_small_end")
        parts = _unpack_rows(_sum_slots(sg, "sum_small"), [small[n].shape for n in fine] + [(1, D)])
        grads.update(zip(fine, parts[:-1]))
        grads.update(zip(S5_BULK, _unpack_rows(_sum_slots(sgb, "sum_small_bulk"), [small[n].shape for n in S5_BULK])))

        def silu_bwd(cc, ds):
            _, vjp = jax.vjp(jax.nn.silu, cc)
            return vjp(ds)[0]

        (grads["c_ctx"],) = _rowmap(silu_bwd, "c_ctx_grad", 1, [c_ctx[None], parts[-1]], [], [(D, F32)])


    gW = dict(gW)
    n_front = w_in.shape[-1] * N_DEV - 2 * D
    gW["w_in"] = jnp.concatenate([gW["w_in"][:, :n_front], gW["w_in"][:, goff:]], axis=1)
    heads = gW["w_uq"].shape[1] // (2 * LANE)
    gW["w_uq"] = _uq_unlayout(gW["w_uq"], heads)
    gW["w_ukv"] = _heads_merge(gW["w_ukv"], heads, QK_NOPE)
    last = [n for n in BIG if n in gW]
    slots = [_grad_slots(n, gW[n]) for n in last]
    from_sibling = _rs_pair(slots, "rs_pair")
    chip_sums = [_add_pair(pp, rr, "rs_add_" + n, sm_token) for n, pp, rr in zip(last, slots, from_sibling)]
    my_chip = 2 * lax.axis_index("x") + lax.axis_index("y")
    lands = [_own_slot(lax.dynamic_index_in_dim(q, my_chip, 0, keepdims=False), my_chip, N_CHIP) for q in chip_sums]
    rs_send, rs_recv, rs_thru, behind = _xchg_start(chip_sums, lands, True, "rs_chips_start")
    for names, (send, recv, thru, _) in rs_async.items():
        for n, g8 in zip(names, _xchg_wait(send, recv, thru, behind, True, "rs_wait_" + names[0])):
            grads[n] = _sum_slots(g8, "rs_sum_" + n)
    grads["w_mod"] = g_w_mod

    out = {}

    def adamw_big(n, after):
        d, nm, nv = _adamw(a[n][0], grads[n], a["m_" + n][0], a["v_" + n][0], "adamw_" + n, after)
        for k, val in (("grad_", grads[n]), ("delta_", d), ("new_m_", nm), ("new_v_", nv)):
            out[k + n] = val.reshape(a[n].shape)
        return nv

    for n in FFN + MIX + ("w_mod",):
        behind = adamw_big(n, behind)
    small_grads(behind)
    packs = [_pack_rows([t[n] for n in SMALL], F32) for t in (
        {n: a[n] for n in SMALL}, {n: grads[n] for n in SMALL}, {n: a["m_" + n] for n in SMALL}, {n: a["v_" + n] for n in SMALL})]
    res = _adamw(*packs, "adamw_small")
    for n, g4 in zip(last, _xchg_wait(rs_send, rs_recv, rs_thru, res[2], True, "rs_chips_wait")):
        grads[n] = _sum_slots(g4, "rs_sum_" + n)
        adamw_big(n, None)
    shapes = [a[n].shape for n in SMALL]
    for k, packed in (("grad_", packs[1]), ("delta_", res[0]), ("new_m_", res[1]), ("new_v_", res[2])):
        for n, val in zip(SMALL, _unpack_rows(packed, shapes)):
            out[k + n] = val
    return (loss, grad_x[None]) + tuple(out[k + n] for k in ("grad_", "delta_", "new_m_", "new_v_") for n in WEIGHTS)
```

```python
import functools
import math

import jax
import jax.numpy as jnp
from jax import lax
from jax.experimental import pallas as pl
from jax.experimental.pallas import tpu as pltpu

F32 = jnp.float32
BF16 = jnp.bfloat16

N_DEV = 8
N_CHIP = 4
EPS = 1e-6
GRID_W = 64
S5_GROUP = 16
QK_NOPE, QK_ROPE, V_DIM = 128, 64, 128
ROPE_BASE = 10000.0
ADAM_LR, ADAM_B1, ADAM_B2, ADAM_EPS, ADAM_WD, ADAM_STEP = 0.001, 0.9, 0.999, 1e-08, 0.01, 10

LANE = 128
SUB = 8
PACK_W = 1024
PACK_ROWS = 32
VMEM_LIMIT = 48 << 20
ROWMAP_TILE_BYTES = 20 << 20
MM_VMEM_BUDGET = 36 << 20
MESH = pl.DeviceIdType.MESH
_NT = (((1,), (1,)), ((), ()))
_TN = (((0,), (0,)), ((), ()))


def _pick(dim, cands):
    for c in cands:
        if dim % c == 0:
            return c
    return dim


def _cparams(sem):
    return pltpu.CompilerParams(dimension_semantics=sem, vmem_limit_bytes=VMEM_LIMIT)


def _mm(a, b, dims, out_dtype, name, out_slots=None):
    a = a.astype(BF16)
    b = b.astype(BF16)
    b3 = b.ndim == 3
    if dims == "nn":
        (M, K), N = a.shape, (b.shape[0] * b.shape[2] if b3 else b.shape[1])
    elif dims == "nt":
        M, N = a.shape[0], b.shape[-2]
        K = b.shape[0] * b.shape[2] if b3 else b.shape[1]
    else:
        (K, M), N = a.shape, b.shape[1]
    unit_n = b.shape[2] if (b3 and dims == "nn") else (N // out_slots if out_slots else N)
    unit_k = b.shape[2] if (b3 and dims == "nt") else K
    osz = jnp.dtype(out_dtype).itemsize
    tm, tn, tk = _mm_tiles(M, unit_n, unit_k, osz, LANE if dims == "tn" else 16)
    nk, npt, kpt = K // tk, unit_n // tn, unit_k // tk
    use_acc = nk > 1 and out_dtype != F32
    if dims == "nn":
        a_spec = pl.BlockSpec((tm, tk), lambda i, j, k: (i, k))
        b_spec = (pl.BlockSpec((None, tk, tn), lambda i, j, k: (j // npt, k, j % npt)) if b3
                  else pl.BlockSpec((tk, tn), lambda i, j, k: (k, j)))
        dn = (((1,), (0,)), ((), ()))
    elif dims == "nt":
        a_spec = pl.BlockSpec((tm, tk), lambda i, j, k: (i, k))
        b_spec = (pl.BlockSpec((None, tn, tk), lambda i, j, k: (k // kpt, j, k % kpt)) if b3
                  else pl.BlockSpec((tn, tk), lambda i, j, k: (j, k)))
        dn = _NT
    else:
        a_spec = pl.BlockSpec((tk, tm), lambda i, j, k: (k, i))
        b_spec = pl.BlockSpec((tk, tn), lambda i, j, k: (k, j))
        dn = _TN
    if out_slots:
        out_spec = pl.BlockSpec((None, tm, tn), lambda i, j, k: (j // npt, i, j % npt))
        out_shape = jax.ShapeDtypeStruct((out_slots, M, unit_n), out_dtype)
    else:
        out_spec = pl.BlockSpec((tm, tn), lambda i, j, k: (i, j))
        out_shape = jax.ShapeDtypeStruct((M, N), out_dtype)

    def body(a_ref, b_ref, o_ref, *scratch):
        part = lax.dot_general(a_ref[...], b_ref[...], dn, preferred_element_type=F32)
        if nk == 1:
            o_ref[...] = part.astype(o_ref.dtype)
            return
        acc_ref = scratch[0] if use_acc else o_ref
        k = pl.program_id(2)

        @pl.when(k == 0)
        def _():
            acc_ref[...] = part

        @pl.when(k > 0)
        def _():
            acc_ref[...] += part

        if use_acc:
            @pl.when(k == nk - 1)
            def _():
                o_ref[...] = acc_ref[...].astype(o_ref.dtype)

    return pl.pallas_call(
        body, name=name, grid=(M // tm, N // tn, nk),
        in_specs=[a_spec, b_spec], out_specs=out_spec, out_shape=out_shape,
        scratch_shapes=[pltpu.VMEM((tm, tn), F32)] if use_acc else [],
        compiler_params=_cparams(("parallel", "parallel", "arbitrary")),
    )(a, b)


def _divisors(n, mult, cap):
    d = [t for t in range(mult, min(n, cap) + 1, mult) if n % t == 0]
    return d[::-1] or [n]


def _mm_tiles(M, unit_n, unit_k, out_itemsize, tm_mult):
    best = None
    for tk in _divisors(unit_k, LANE, 2816):
        for tn in _divisors(unit_n, LANE, 1536):
            for tm in _divisors(M, tm_mult, 1024):
                vmem = 2 * 2 * (tm * tk + tk * tn) + 2 * tm * tn * out_itemsize + 4 * tm * tn * (2 if unit_k > tk else 1)
                if vmem > MM_VMEM_BUDGET:
                    continue
                steps = (M // tm) * (unit_n // tn) * (unit_k // tk)
                key = (steps, -tk, -tn)
                if best is None or key < best[0]:
                    best = (key, (tm, tn, tk))
                break
    return best[1]


def _rowmap(fn, name, M, row_ins, bc_ins, row_outs, acc_outs=(), after=None):
    row_ins = [r if isinstance(r, tuple) else (r, r.shape[1], 0) for r in row_ins]
    row_bytes = sum(w * a.dtype.itemsize for a, w, _ in row_ins) + sum(w * jnp.dtype(d).itemsize for w, d in row_outs)
    widest = max([w for _, w, _ in row_ins] + [w for w, _ in row_outs])
    row_bytes = 2 * row_bytes + 6 * 4 * widest
    tm = _pick(M, [t for t in (512, 256, 128, 64, 32, 16) if t * row_bytes <= ROWMAP_TILE_BYTES] + [16])
    n_in, n_row, n_acc = len(row_ins) + len(bc_ins), len(row_outs), len(acc_outs)

    def body(*refs):
        res = fn(*[r[...].astype(F32) for r in refs[:n_in]])
        res = res if isinstance(res, (tuple, list)) else (res,)
        outs = refs[n_in + (after is not None):]
        for k in range(n_row):
            outs[k][...] = res[k].astype(outs[k].dtype)
        if n_acc:
            @pl.when(pl.program_id(0) == 0)
            def _():
                for k in range(n_acc):
                    outs[n_row + k][...] = jnp.zeros_like(outs[n_row + k])

            for k in range(n_acc):
                outs[n_row + k][...] += res[n_row + k].astype(F32)

    in_specs = [pl.BlockSpec((tm, w), functools.partial(lambda i, blk: (i, blk), blk=blk)) for _, w, blk in row_ins]
    in_specs += [pl.BlockSpec(b.shape, lambda i: (0, 0)) for b in bc_ins]
    in_specs += [pl.BlockSpec(memory_space=pl.ANY)] * (after is not None)
    out_specs = [pl.BlockSpec((tm, w), lambda i: (i, 0)) for w, _ in row_outs]
    out_specs += [pl.BlockSpec((1, w), lambda i: (0, 0)) for w in acc_outs]
    out_shape = [jax.ShapeDtypeStruct((M, w), d) for w, d in row_outs]
    out_shape += [jax.ShapeDtypeStruct((1, w), F32) for w in acc_outs]
    return pl.pallas_call(
        body, name=name, grid=(M // tm,), in_specs=in_specs, out_specs=out_specs, out_shape=out_shape,
        compiler_params=_cparams(("arbitrary",) if n_acc else ("parallel",)),
    )(*[a for a, _, _ in row_ins], *bc_ins, *([] if after is None else [after]))


def _rms(x, g):
    return x * lax.rsqrt(jnp.mean(x * x, axis=-1, keepdims=True) + EPS) * g


def _normmod(x, g, sc, sh):
    return _rms(x, g) * (1.0 + sc) + sh


def _swap16(v):
    w = v.shape[1]
    lane = lax.broadcasted_iota(jnp.int32, v.shape, 1)
    return jnp.where((lane // 16) % 2 == 0, pltpu.roll(v, w - 16, 1), pltpu.roll(v, 16, 1))


def _rope(v, cos, sin_signed):
    return v * cos + _swap16(v) * sin_signed


def _rope_bwd(d, cos, sin_signed):
    return d * cos + _swap16(d * sin_signed)


def _mesh_pos():
    return lax.axis_index("x"), lax.axis_index("y"), lax.axis_index("c")


def _hbm_call(body, name, ins, out_shapes, n_sems):
    any_spec = pl.BlockSpec(memory_space=pl.ANY)
    return pl.pallas_call(
        body, name=name, out_shape=out_shapes, in_specs=[any_spec] * len(ins), out_specs=[any_spec] * len(out_shapes),
        scratch_shapes=[pltpu.SemaphoreType.DMA((n_sems,)), pltpu.SemaphoreType.DMA((n_sems,)),
                        pltpu.SemaphoreType.DMA((len(ins),))],
    )(*ins)


def _all_gather(xs, name, after=None):
    n = len(xs)

    def body(*refs):
        k = n + (after is not None)
        x_refs, out_refs, (send_sems, recv_sems, local_sems) = refs[:n], refs[k:k + n], refs[k + n:]
        x, y, c = _mesh_pos()
        me, sibling = (x, y, c), (x, y, 1 - c)
        chips = [(1 - x, y), (x, 1 - y), (1 - x, 1 - y)]
        locals_, first, passed, arrivals = [], [], [], []
        for a in range(n):
            def slot(px, py, pc, a=a):
                return out_refs[a].at[4 * px + 2 * py + pc]

            def copy(k, block, to, src=None, a=a, slot=slot):
                return pltpu.make_async_remote_copy(
                    src_ref=slot(*block) if src is None else src, dst_ref=slot(*block),
                    send_sem=send_sems.at[7 * a + k], recv_sem=recv_sems.at[7 * a + k], device_id=to, device_id_type=MESH)

            locals_.append(pltpu.make_async_copy(x_refs[a], slot(*me), local_sems.at[a]))
            first.append(copy(0, me, sibling, src=x_refs[a]))
            first += [copy(1 + j, me, (*chip, c), src=x_refs[a]) for j, chip in enumerate(chips)]
            passed.append([copy(4 + j, (*chip, c), sibling) for j, chip in enumerate(chips)])
            arrivals.append([copy(1 + j, (*chip, c), me) for j, chip in enumerate(chips)]
                            + [copy(0, sibling, me)] + [copy(4 + j, (*chip, 1 - c), me) for j, chip in enumerate(chips)])
        for cp in locals_ + first:
            cp.start()
        for j in range(3):
            for a in range(n):
                arrivals[a][j].wait_recv()
                passed[a][j].start()
        for a in range(n):
            for cp in arrivals[a][3:]:
                cp.wait_recv()
        for cp in first + [p for ps in passed for p in ps]:
            cp.wait_send()
        for cp in locals_:
            cp.wait()

    return _hbm_call(body, name, list(xs) + ([] if after is None else [after]),
                     [jax.ShapeDtypeStruct((N_DEV,) + x.shape, x.dtype) for x in xs], 7 * n)


def _rs_pair(ps, name):
    n = len(ps)

    def body(*refs):
        p_refs, out_refs, (send_sems, recv_sems, _) = refs[:n], refs[n:2 * n], refs[2 * n:]
        x, y, c = _mesh_pos()
        sends, recvs = [], []
        for a in range(n):
            for q in range(N_CHIP):
                sem = dict(send_sem=send_sems.at[4 * a + q], recv_sem=recv_sems.at[4 * a + q],
                           device_id=(x, y, 1 - c), device_id_type=MESH)
                sends.append(pltpu.make_async_remote_copy(src_ref=p_refs[a].at[2 * q + 1 - c], dst_ref=out_refs[a].at[q], **sem))
                recvs.append(pltpu.make_async_remote_copy(src_ref=p_refs[a].at[2 * q + c], dst_ref=out_refs[a].at[q], **sem))
        for cp in sends:
            cp.start()
        for cp in recvs:
            cp.wait_recv()
        for cp in sends:
            cp.wait_send()

    return _hbm_call(body, name, ps, [jax.ShapeDtypeStruct((N_CHIP,) + p.shape[1:], p.dtype) for p in ps], 4 * n)


def _xchg_copies(src_refs, land_refs, send_sems, recv_sems, slot_src):
    x, y, c = _mesh_pos()
    sends, recvs = [], []
    for a, (src, land) in enumerate(zip(src_refs, land_refs)):
        chips = land.shape[0] == N_CHIP
        npeer = land.shape[0] - 1
        me = 2 * x + y if chips else 4 * x + 2 * y + c
        for r in range(1, npeer + 1):
            px = 1 - x if r & (2 if chips else 4) else x
            py = 1 - y if r & (1 if chips else 2) else y
            pc = c if chips else (1 - c if r & 1 else c)
            peer = 2 * px + py if chips else 4 * px + 2 * py + pc
            sem = dict(send_sem=send_sems.at[npeer * a + r - 1], recv_sem=recv_sems.at[npeer * a + r - 1],
                       device_id=(px, py, pc), device_id_type=MESH)
            s = src.at[peer] if slot_src else src
            sends.append(pltpu.make_async_remote_copy(src_ref=s, dst_ref=land.at[me], **sem))
            recvs.append(pltpu.make_async_remote_copy(src_ref=s, dst_ref=land.at[peer], **sem))
    return sends, recvs


_HBM = pl.BlockSpec(memory_space=pltpu.HBM)
_SEM = pl.BlockSpec(memory_space=pltpu.SEMAPHORE)
_EFFECT = pltpu.SideEffectType.DATAFLOW_SIDE_EFFECTING


def _xchg_start(srcs, lands, slot_src, name):
    n = len(srcs)

    def body(*refs):
        sends, _ = _xchg_copies(refs[:n], refs[n:2 * n], refs[2 * n], refs[2 * n + 1], slot_src)
        for cp in sends:
            cp.start()
        refs[-1][...] = jnp.zeros_like(refs[-1])

    bufs = list(srcs) + list(lands)
    n_sems = n * (lands[0].shape[0] - 1)
    res = pl.pallas_call(
        body, name=name,
        out_shape=(pltpu.SemaphoreType.DMA((n_sems,)), pltpu.SemaphoreType.DMA((n_sems,)))
        + tuple(pltpu.HBM(b.shape, b.dtype) for b in bufs) + (jax.ShapeDtypeStruct((SUB, LANE), F32),),
        in_specs=(_HBM,) * (2 * n), out_specs=(_SEM, _SEM) + (_HBM,) * (2 * n) + (pl.BlockSpec(memory_space=pltpu.VMEM),),
        input_output_aliases={i: 2 + i for i in range(2 * n)},
        compiler_params=pltpu.CompilerParams(has_side_effects=_EFFECT),
    )(*[pltpu.with_memory_space_constraint(b, pltpu.HBM) for b in bufs])
    return res[0], res[1], res[2:-1], res[-1]


def _xchg_wait(send_sems, recv_sems, thru, after, slot_src, name):
    n = len(thru) // 2

    def body(*refs):
        sends, recvs = _xchg_copies(refs[:n], refs[n:2 * n], refs[2 * n], refs[2 * n + 1], slot_src)
        for cp in sends:
            cp.wait_send()
        for cp in recvs:
            cp.wait_recv()

    res = pl.pallas_call(
        body, name=name, out_shape=tuple(pltpu.HBM(b.shape, b.dtype) for b in thru),
        in_specs=(_HBM,) * (2 * n) + (_SEM, _SEM, pl.BlockSpec(memory_space=pl.ANY)), out_specs=(_HBM,) * (2 * n),
        input_output_aliases={i: i for i in range(2 * n)},
        compiler_params=pltpu.CompilerParams(has_side_effects=_EFFECT),
    )(*thru, send_sems, recv_sems, after)
    return res[n:]


def _ag2_copy(land, sems, k, block, to, src=None):
    slot = land.at[4 * block[0] + 2 * block[1] + block[2]]
    return pltpu.make_async_remote_copy(src_ref=slot if src is None else src, dst_ref=slot, send_sem=sems[0].at[k],
                                        recv_sem=sems[1].at[k], device_id=to, device_id_type=MESH)


def _ag2_start(blocks, lands, name):
    n = len(blocks)

    def body(*refs):
        x, y, c = _mesh_pos()
        for a in range(n):
            sems = (refs[2 * n], refs[2 * n + 1])
            _ag2_copy(refs[n + a], sems, 4 * a, (x, y, c), (x, y, 1 - c), src=refs[a]).start()
            for j, chip in enumerate([(1 - x, y), (x, 1 - y), (1 - x, 1 - y)]):
                _ag2_copy(refs[n + a], sems, 4 * a + 1 + j, (x, y, c), (*chip, c), src=refs[a]).start()
        refs[-1][...] = jnp.zeros_like(refs[-1])

    bufs = list(blocks) + list(lands)
    res = pl.pallas_call(
        body, name=name,
        out_shape=(pltpu.SemaphoreType.DMA((4 * n,)), pltpu.SemaphoreType.DMA((4 * n,)))
        + tuple(pltpu.HBM(b.shape, b.dtype) for b in bufs) + (jax.ShapeDtypeStruct((SUB, LANE), F32),),
        in_specs=(_HBM,) * (2 * n), out_specs=(_SEM, _SEM) + (_HBM,) * (2 * n) + (pl.BlockSpec(memory_space=pltpu.VMEM),),
        input_output_aliases={i: 2 + i for i in range(2 * n)},
        compiler_params=pltpu.CompilerParams(has_side_effects=_EFFECT),
    )(*[pltpu.with_memory_space_constraint(b, pltpu.HBM) for b in bufs])
    return (res[0], res[1]), res[2:-1], res[-1]


def _ag2_mid(sems1, thru, after, name):
    n = len(thru) // 2

    def body(*refs):
        x, y, c = _mesh_pos()
        s1, s2 = (refs[2 * n], refs[2 * n + 1]), (refs[2 * n + 3], refs[2 * n + 4])
        for j, chip in enumerate([(1 - x, y), (x, 1 - y), (1 - x, 1 - y)]):
            for a in range(n):
                _ag2_copy(refs[n + a], s1, 4 * a + 1 + j, (*chip, c), (x, y, c)).wait_recv()
                _ag2_copy(refs[n + a], s2, 3 * a + j, (*chip, c), (x, y, 1 - c)).start()
        refs[-1][...] = jnp.zeros_like(refs[-1])

    res = pl.pallas_call(
        body, name=name,
        out_shape=(pltpu.SemaphoreType.DMA((3 * n,)), pltpu.SemaphoreType.DMA((3 * n,)))
        + tuple(pltpu.HBM(b.shape, b.dtype) for b in thru) + (jax.ShapeDtypeStruct((SUB, LANE), F32),),
        in_specs=(_HBM,) * (2 * n) + (_SEM, _SEM, pl.BlockSpec(memory_space=pl.ANY)),
        out_specs=(_SEM, _SEM) + (_HBM,) * (2 * n) + (pl.BlockSpec(memory_space=pltpu.VMEM),),
        input_output_aliases={i: 2 + i for i in range(2 * n)},
        compiler_params=pltpu.CompilerParams(has_side_effects=_EFFECT),
    )(*thru, *sems1, after)
    return (res[0], res[1]), res[2:-1], res[-1]


def _ag2_end(sems1, sems2, thru, after, name):
    n = len(thru) // 2

    def body(*refs):
        x, y, c = _mesh_pos()
        s1, s2 = (refs[2 * n], refs[2 * n + 1]), (refs[2 * n + 2], refs[2 * n + 3])
        chips = [(1 - x, y), (x, 1 - y), (1 - x, 1 - y)]
        for a in range(n):
            land = refs[n + a]
            _ag2_copy(land, s1, 4 * a, (x, y, c), (x, y, 1 - c), src=refs[a]).wait_send()
            _ag2_copy(land, s1, 4 * a, (x, y, 1 - c), (x, y, c)).wait_recv()
            for j, chip in enumerate(chips):
                _ag2_copy(land, s1, 4 * a + 1 + j, (x, y, c), (*chip, c), src=refs[a]).wait_send()
                _ag2_copy(land, s2, 3 * a + j, (*chip, c), (x, y, 1 - c)).wait_send()
                _ag2_copy(land, s2, 3 * a + j, (*chip, 1 - c), (x, y, c)).wait_recv()

    res = pl.pallas_call(
        body, name=name, out_shape=tuple(pltpu.HBM(b.shape, b.dtype) for b in thru),
        in_specs=(_HBM,) * (2 * n) + (_SEM,) * 4 + (pl.BlockSpec(memory_space=pl.ANY),), out_specs=(_HBM,) * (2 * n),
        input_output_aliases={i: i for i in range(2 * n)},
        compiler_params=pltpu.CompilerParams(has_side_effects=_EFFECT),
    )(*thru, *sems1, *sems2, after)
    return res[n:]


def _own_slot(block, me, slots=N_DEV):
    return lax.dynamic_update_slice(lax.empty((slots,) + block.shape, block.dtype), block[None], (me, 0, 0))


def _add_pair(p, r, name, after):
    _, R, C = p.shape
    tr = _pick(R, (512, 256, 128, 64, 32, 16))

    def body(c_ref, p_ref, r_ref, after_ref, o_ref):
        o_ref[...] = (p_ref[...].astype(F32) + r_ref[...].astype(F32)).astype(o_ref.dtype)

    return pl.pallas_call(
        body, name=name, out_shape=jax.ShapeDtypeStruct((N_CHIP, R, C), p.dtype),
        grid_spec=pltpu.PrefetchScalarGridSpec(
            num_scalar_prefetch=1, grid=(N_CHIP, R // tr),
            in_specs=[pl.BlockSpec((None, None, tr, C), lambda q, i, c_ref: (q, c_ref[0], i, 0)),
                      pl.BlockSpec((None, tr, C), lambda q, i, c_ref: (q, i, 0)), pl.BlockSpec(memory_space=pl.ANY)],
            out_specs=pl.BlockSpec((None, tr, C), lambda q, i, c_ref: (q, i, 0))),
        compiler_params=_cparams(("parallel", "parallel")),
    )(lax.axis_index("c").reshape(1).astype(jnp.int32), p.reshape(N_CHIP, 2, R, C), r, after)


def _sum_slots(g, name):
    ns, R, C = g.shape
    tr = _pick(R, (256, 128, 64, 32, 16))

    def body(g_ref, o_ref):
        acc = g_ref[0].astype(F32)
        for j in range(1, ns):
            acc = acc + g_ref[j].astype(F32)
        o_ref[...] = acc

    return pl.pallas_call(
        body, name=name, grid=(R // tr,),
        in_specs=[pl.BlockSpec((ns, tr, C), lambda i: (0, i, 0))], out_specs=pl.BlockSpec((tr, C), lambda i: (i, 0)),
        out_shape=jax.ShapeDtypeStruct((R, C), F32), compiler_params=_cparams(("parallel",)),
    )(g)


def _pack_rows(arrs, dtype):
    parts = []
    for a in arrs:
        flat = a.reshape(-1).astype(dtype)
        pad = (-flat.shape[0]) % (PACK_W * 16)
        parts.append(jnp.pad(flat, (0, pad)).reshape(-1, PACK_W))
    out = jnp.concatenate(parts, axis=0)
    return jnp.pad(out, ((0, (-out.shape[0]) % PACK_ROWS), (0, 0)))


def _packed_rows(shape):
    n = math.prod(shape)
    return (n + PACK_W * 16 - 1) // (PACK_W * 16) * 16


def _unpack_rows(packed, shapes):
    out, r0 = [], 0
    for s in shapes:
        rows, n = _packed_rows(s), math.prod(s)
        out.append(packed[r0:r0 + rows].reshape(rows * PACK_W)[:n].reshape(s))
        r0 += rows
    return out


def _adamw_math(w, g, m, v):
    m = ADAM_B1 * m + (1.0 - ADAM_B1) * g
    v = ADAM_B2 * v + (1.0 - ADAM_B2) * (g * g)
    m_hat = m / (1.0 - ADAM_B1 ** ADAM_STEP)
    v_hat = v / (1.0 - ADAM_B2 ** ADAM_STEP)
    delta = -ADAM_LR * (m_hat / (jnp.sqrt(v_hat) + ADAM_EPS) + ADAM_WD * w)
    return delta, m, v


def _adamw_slots(w, gs, m, v, name, after=None):
    ns, R, C = gs.shape
    row_bytes = 2 * (ns * C * gs.dtype.itemsize + 7 * C * 4) + 6 * 4 * C
    tr = _pick(R, [t for t in (512, 256, 128, 64, 32, 16) if t * row_bytes <= ROWMAP_TILE_BYTES] + [16])

    def body(w_ref, g_ref, m_ref, v_ref, *rest):
        outs = rest[(after is not None):]
        g = g_ref[0].astype(F32)
        for j in range(1, ns):
            g = g + g_ref[j].astype(F32)
        res = (g,) + _adamw_math(w_ref[...], g, m_ref[...], v_ref[...])
        for o_ref, val in zip(outs, res):
            o_ref[...] = val

    row = pl.BlockSpec((tr, C), lambda i: (i, 0))
    return pl.pallas_call(
        body, name=name, grid=(R // tr,),
        in_specs=[row, pl.BlockSpec((ns, tr, C), lambda i: (0, i, 0)), row, row]
        + [pl.BlockSpec(memory_space=pl.ANY)] * (after is not None),
        out_specs=[row] * 4, out_shape=[jax.ShapeDtypeStruct((R, C), F32)] * 4, compiler_params=_cparams(("parallel",)),
    )(w, gs, m, v, *([] if after is None else [after]))


def _adamw(w, g, m, v, name, after=None):
    R, C = w.shape
    return _rowmap(_adamw_math, name, R, [w, g, m, v], [], [(C, F32)] * 3, after=after)


def _s5_disc_math(lr, li, ldt, br, bi):
    dt = jnp.exp(ldt)
    mag = jnp.exp(lr * dt)
    ab_re, ab_im = mag * jnp.cos(li * dt), mag * jnp.sin(li * dt)
    den = lr * lr + li * li
    nr, ni = ab_re - 1.0, ab_im
    co_re = (nr * lr + ni * li) / den
    co_im = (ni * lr - nr * li) / den
    bb_re = co_re * br - co_im * bi
    bb_im = co_re * bi + co_im * br
    return ab_re, ab_im, bb_re, bb_im


def _s5_tables(a_re, a_im, ldt, b_re, b_im, c_re, c_im):
    _, G, P, N = b_re.shape
    nch = G // 8

    def body(lr_ref, li_ref, ldt_ref, br_ref, bi_ref, cr_ref, ci_ref, wre, wim, vre, vim, pwr, pwi):
        ar, ai, bb_re, bb_im = _s5_disc_math(lr_ref[0], li_ref[0], ldt_ref[0], br_ref[0], bi_ref[0])
        cr, ci = cr_ref[0], ci_ref[0]
        pr, pi = jnp.ones_like(ar), jnp.zeros_like(ar)
        for j in range(SUB + 1):
            pwr[0, j], pwi[0, j] = pr, pi
            if j < SUB:
                tabs = ((wre, bb_re * pr - bb_im * pi), (wim, bb_re * pi + bb_im * pr),
                        (vre, cr * pr - ci * pi), (vim, -(cr * pi + ci * pr)))
                for ref, val in tabs:
                    for s in range(nch):
                        ref[0, s, pl.ds(j * LANE, LANE), :] = val[s * 8:(s + 1) * 8].reshape(LANE, N).astype(BF16)
            pr, pi = pr * ar - pi * ai, pr * ai + pi * ar

    g1n = pl.BlockSpec((1, G, 1, N), lambda d: (d, 0, 0, 0))
    gpn = pl.BlockSpec((1, G, P, N), lambda d: (d, 0, 0, 0))
    tab = pl.BlockSpec((1, nch, SUB * LANE, N), lambda d: (d, 0, 0, 0))
    pw = pl.BlockSpec((1, SUB + 1, G, 1, N), lambda d: (d, 0, 0, 0, 0))
    s_tab = jax.ShapeDtypeStruct((2, nch, SUB * LANE, N), BF16)
    s_pw = jax.ShapeDtypeStruct((2, SUB + 1, G, 1, N), F32)
    return pl.pallas_call(
        body, name="s5_tables", grid=(2,),
        in_specs=[g1n, g1n, pl.BlockSpec((1, G, 1, 1), lambda d: (d, 0, 0, 0)), gpn, gpn, gpn, gpn],
        out_specs=[tab] * 4 + [pw] * 2, out_shape=[s_tab] * 4 + [s_pw] * 2,
        compiler_params=_cparams(("parallel",)),
    )(a_re, a_im, ldt, b_re, b_im, c_re, c_im)


def _s5_expand(t_re, t_im, name):
    _, nch, R, N = t_re.shape
    sw = 8 * N

    def body(re_ref, im_ref, o_ref):
        spread = (lax.broadcasted_iota(jnp.int32, (N, sw), 1) % N == lax.broadcasted_iota(jnp.int32, (N, sw), 0)).astype(BF16)
        row_g = (lax.broadcasted_iota(jnp.int32, (R, sw), 0) % LANE) // S5_GROUP
        keep = row_g == lax.broadcasted_iota(jnp.int32, (R, sw), 1) // N
        for half, ref in enumerate((re_ref, im_ref)):
            t = jnp.dot(ref[0, 0], spread, preferred_element_type=F32)
            o_ref[0, 0, :, pl.ds(half * sw, sw)] = jnp.where(keep, t, 0.0).astype(BF16)

    spec = pl.BlockSpec((1, 1, R, N), lambda d, s: (d, s, 0, 0))
    return pl.pallas_call(
        body, name=name, grid=(2, nch), in_specs=[spec, spec],
        out_specs=pl.BlockSpec((1, 1, R, 2 * sw), lambda d, s: (d, s, 0, 0)),
        out_shape=jax.ShapeDtypeStruct((2, nch, R, 2 * sw), BF16), compiler_params=_cparams(("parallel", "parallel")),
    )(t_re, t_im)


def _s5_param_bwd(a_re, a_im, ldt, b_re, b_im, da_re, da_im, dbb_re, dbb_im):
    _, G, P, N = b_re.shape

    def body(lr_ref, li_ref, ldt_ref, br_ref, bi_ref, dar, dai, dbr, dbi, o_lr, o_li, o_ldt, o_br, o_bi):
        _, vjp = jax.vjp(_s5_disc_math, lr_ref[0], li_ref[0], ldt_ref[0], br_ref[0], bi_ref[0])
        o_lr[0], o_li[0], o_ldt[0], o_br[0], o_bi[0] = vjp((dar[0], dai[0], dbr[0], dbi[0]))

    g1n = pl.BlockSpec((1, G, 1, N), lambda d: (d, 0, 0, 0))
    g11 = pl.BlockSpec((1, G, 1, 1), lambda d: (d, 0, 0, 0))
    gpn = pl.BlockSpec((1, G, P, N), lambda d: (d, 0, 0, 0))
    s_g1n, s_g11, s_gpn = (jax.ShapeDtypeStruct(s, F32) for s in ((2, G, 1, N), (2, G, 1, 1), (2, G, P, N)))
    return pl.pallas_call(
        body, name="s5_param_bwd", grid=(2,),
        in_specs=[g1n, g1n, g11, gpn, gpn, g1n, g1n, gpn, gpn], out_specs=[g1n, g1n, g11, gpn, gpn],
        out_shape=[s_g1n, s_g1n, s_g11, s_gpn, s_gpn], compiler_params=_cparams(("parallel",)),
    )(a_re, a_im, ldt, b_re, b_im, da_re, da_im, dbb_re, dbb_im)


def _tile_local_scan(u, w_ref, back):
    tb, sw2 = u.shape[0], w_ref.shape[1]
    sw, half = sw2 // 2, LANE // 2
    tau = lax.broadcasted_iota(jnp.int32, u.shape, 0) % SUB
    low = lax.broadcasted_iota(jnp.int32, u.shape, 1) < half
    parts = [u]
    for j in range(1, SUB):
        if back:
            parts.append(jnp.where(tau >= j, pltpu.roll(u, j, 0), 0.0))
        else:
            parts.append(jnp.where(tau <= SUB - 1 - j, pltpu.roll(u, tb - j, 0), 0.0))
    out = [None] * 4
    for h in range(2):
        pieces = [jnp.where(low, a, pltpu.roll(b, half, 1)) if h == 0 else jnp.where(low, pltpu.roll(a, half, 1), b)
                  for a, b in zip(parts[0::2], parts[1::2])]
        lhs = jnp.concatenate(pieces, axis=1).astype(BF16)
        rows = jnp.concatenate([w_ref[pl.ds(j * LANE + h * half, half), :] for j in range(SUB)], axis=0)
        for part in range(2):
            cols = rows[:, part * sw + h * (sw // 2):part * sw + (h + 1) * (sw // 2)]
            out[2 * part + h] = jnp.dot(lhs, cols, preferred_element_type=F32)
    return jnp.concatenate(out, axis=1)


def _cmul_add(tile, pw, carry, sw):
    pr, pi, cr, ci = pw[:, :sw], pw[:, sw:], carry[:, :sw], carry[:, sw:]
    return tile + jnp.concatenate([pr * cr - pi * ci, pr * ci + pi * cr], axis=1)


def _tile_scan(buf, base, ntile, pw, carry, sw, causal):
    def step(k, c):
        i = k if causal else ntile - 1 - k
        r = pl.multiple_of(base + i * SUB, SUB)
        tile = _cmul_add(buf[pl.ds(r, SUB), :], pw, c, sw)
        buf[pl.ds(r, SUB), :] = tile
        return tile[SUB - 1:SUB, :] if causal else tile[0:1, :]

    return lax.fori_loop(0, ntile, step, carry)


def _s5_fwd(h_all, waug, vaug, pw, S5W, T, d, causal, name):
    S = h_all.shape[0]
    _, nch, _, sw2 = waug.shape
    sw = sw2 // 2
    tb = _pick(math.gcd(T, S - T), (256, 128, 64, 32, 16))
    ntile, nt, off = tb // SUB, S // tb, T // tb
    rb = (lambda s, t: ((t + off) % nt, s)) if causal else (lambda s, t: (nt - 1 - t, s))

    def body(u_ref, w_ref, v_ref, p_ref, y_ref, h_ref, hblk, carry):
        @pl.when(pl.program_id(1) == 0)
        def _():
            carry[...] = jnp.zeros_like(carry)

        hblk[...] = _tile_local_scan(u_ref[...], w_ref, causal)
        carry[...] = _tile_scan(hblk, 0, ntile, p_ref[...], carry[...], sw, causal)
        hb = hblk[...].astype(BF16)
        h_ref[...] = hb
        y_ref[...] = lax.dot_general(hb, v_ref[...], _NT, preferred_element_type=F32)

    return pl.pallas_call(
        body, name=name, grid=(nch, nt),
        in_specs=[pl.BlockSpec((tb, LANE), rb),
                  pl.BlockSpec((None, None, SUB * LANE, sw2), lambda s, t: (d, s, 0, 0)),
                  pl.BlockSpec((None, None, LANE, sw2), lambda s, t: (d, s, 0, 0)),
                  pl.BlockSpec((None, SUB, sw2), lambda s, t: (s, 0, 0))],
        out_specs=[pl.BlockSpec((tb, LANE), rb), pl.BlockSpec((tb, sw2), rb)],
        out_shape=[jax.ShapeDtypeStruct((S, S5W), F32), jax.ShapeDtypeStruct((S, nch * sw2), BF16)],
        scratch_shapes=[pltpu.VMEM((tb, sw2), F32), pltpu.VMEM((1, sw2), F32)],
        compiler_params=_cparams(("parallel", "arbitrary")),
    )(h_all, waug, vaug, pw)


def _s5_bwd(dy_all, h_all, hs, waug, vaug, pwc, S5W, T, d, causal, name):
    S = h_all.shape[0]
    _, nch, _, sw2 = waug.shape
    sw = sw2 // 2
    tb = _pick(math.gcd(T, S - T), (256, 128, 64, 32, 16))
    ntile, nt, off = tb // SUB, S // tb, T // tb
    rb = (lambda s, t: ((nt - 1 - t + off) % nt, s)) if causal else (lambda s, t: (t, s))
    adj_causal = not causal
    edge = SUB - 1 if adj_causal else SUB + tb
    keep_src, keep_dst = (tb, 0) if adj_causal else (SUB, SUB + tb)

    def body(dy_ref, u_ref, h_ref, w_ref, v_ref, p_ref, du_ref, dbb_ref, dc_ref, da_ref, lam):
        @pl.when(pl.program_id(1) == 0)
        def _():
            lam[pl.ds(0, SUB), :] = jnp.zeros((SUB, sw2), F32)
            lam[pl.ds(SUB + tb, SUB), :] = jnp.zeros((SUB, sw2), F32)
            dbb_ref[...] = jnp.zeros_like(dbb_ref)
            dc_ref[...] = jnp.zeros_like(dc_ref)
            da_ref[...] = jnp.zeros_like(da_ref)

        dy = dy_ref[...]
        lam[pl.ds(SUB, tb), :] = _tile_local_scan(dy, v_ref, adj_causal)
        _tile_scan(lam, SUB, ntile, p_ref[...], lam[pl.ds(edge, 1), :], sw, adj_causal)
        lb = lam[pl.ds(SUB, tb), :].astype(BF16)
        du_ref[...] = lax.dot_general(lb, w_ref[...], _NT, preferred_element_type=F32)
        dbb_ref[...] += lax.dot_general(u_ref[...].astype(BF16), lb, _TN, preferred_element_type=F32)
        dc_ref[...] += lax.dot_general(h_ref[...], dy.astype(BF16), _TN, preferred_element_type=F32)
        h = h_ref[...].astype(F32)
        ln = lam[pl.ds(SUB + 1 if causal else SUB - 1, tb), :]
        hr, hi, lr, li = h[:, :sw], h[:, sw:], ln[:, :sw], ln[:, sw:]
        da_ref[...] += jnp.concatenate([jnp.sum(hr * lr + hi * li, axis=0, keepdims=True),
                                        jnp.sum(hr * li - hi * lr, axis=0, keepdims=True)], axis=1)
        lam[pl.ds(keep_dst, SUB), :] = lam[pl.ds(keep_src, SUB), :]

    fixed = lambda s, t: (s, 0, 0)
    return pl.pallas_call(
        body, name=name, grid=(nch, nt),
        in_specs=[pl.BlockSpec((tb, LANE), rb), pl.BlockSpec((tb, LANE), rb), pl.BlockSpec((tb, sw2), rb),
                  pl.BlockSpec((None, None, LANE, sw2), lambda s, t: (d, s, 0, 0)),
                  pl.BlockSpec((None, None, SUB * LANE, sw2), lambda s, t: (d, s, 0, 0)),
                  pl.BlockSpec((None, SUB, sw2), fixed)],
        out_specs=[pl.BlockSpec((tb, LANE), rb), pl.BlockSpec((None, LANE, sw2), fixed),
                   pl.BlockSpec((None, sw2, LANE), fixed), pl.BlockSpec((None, 1, sw2), fixed)],
        out_shape=[jax.ShapeDtypeStruct((S, S5W), F32), jax.ShapeDtypeStruct((nch, LANE, sw2), F32),
                   jax.ShapeDtypeStruct((nch, sw2, LANE), F32), jax.ShapeDtypeStruct((nch, 1, sw2), F32)],
        scratch_shapes=[pltpu.VMEM((tb + 2 * SUB, sw2), F32)],
        compiler_params=_cparams(("parallel", "arbitrary")),
    )(dy_all, h_all, hs, waug, vaug, pwc)


def _attn_fwd(qn, qr, kv, kr, H, scale):
    T, S = qn.shape[0], kv.shape[0]
    tq = _pick(T, (256, 128, 64, 32, 16))

    def body(qn_ref, qr_ref, kn_ref, v_ref, kr_ref, o_ref, lse_ref):
        q = jnp.concatenate([qn_ref[...], qr_ref[...]], axis=1)
        k = jnp.concatenate([kn_ref[...], kr_ref[...]], axis=1)
        s = lax.dot_general(q, k, _NT, preferred_element_type=F32) * scale
        m = jnp.max(s, axis=1, keepdims=True)
        p = jnp.exp(s - m)
        l = jnp.sum(p, axis=1, keepdims=True)
        o_ref[...] = jnp.dot((p * (1.0 / l)).astype(BF16), v_ref[...], preferred_element_type=F32).astype(o_ref.dtype)
        lse_ref[0] = m + jnp.log(l)

    q_spec = pl.BlockSpec((tq, LANE), lambda h, i: (i, h))
    return pl.pallas_call(
        body, name="attn_fwd", grid=(H, T // tq),
        in_specs=[q_spec, q_spec, pl.BlockSpec((S, LANE), lambda h, i: (0, h)), pl.BlockSpec((S, LANE), lambda h, i: (0, H + h)),
                  pl.BlockSpec((S, LANE), lambda h, i: (0, 0))],
        out_specs=[q_spec, pl.BlockSpec((1, tq, 1), lambda h, i: (h, i, 0))],
        out_shape=[jax.ShapeDtypeStruct((T, H * LANE), BF16), jax.ShapeDtypeStruct((H, T, 1), F32)],
        compiler_params=_cparams(("parallel", "parallel")),
    )(qn, qr, kv, kv, kr)


def _attn_bwd(qn, qr, kv, kr, do, lse, H, scale):
    T, S = qn.shape[0], kv.shape[0]
    tq = _pick(T, (512, 256, 128, 64, 32, 16))
    nq = T // tq

    def body(qn_ref, qr_ref, kn_ref, v_ref, kr_ref, do_ref, lse_ref, dqn_ref, dqr_ref, dkn_ref, dkr_ref, dv_ref, dk_acc, dv_acc):
        i = pl.program_id(1)
        q = jnp.concatenate([qn_ref[...], qr_ref[...]], axis=1)
        k = jnp.concatenate([kn_ref[...], kr_ref[...]], axis=1)
        v, d_o = v_ref[...], do_ref[...]
        s = lax.dot_general(q, k, _NT, preferred_element_type=F32) * scale
        p = jnp.exp(s - lse_ref[0])
        dv_part = lax.dot_general(p.astype(BF16), d_o, _TN, preferred_element_type=F32)
        dp = lax.dot_general(d_o, v, _NT, preferred_element_type=F32)
        ds = (p * (dp - jnp.sum(p * dp, axis=1, keepdims=True)) * scale).astype(BF16)
        dq = jnp.dot(ds, k, preferred_element_type=F32)
        dqn_ref[...] = dq[:, :LANE].astype(dqn_ref.dtype)
        dqr_ref[...] = dq[:, LANE:].astype(dqr_ref.dtype)
        dk_part = lax.dot_general(ds, q, _TN, preferred_element_type=F32)

        @pl.when(i == 0)
        def _():
            dk_acc[...] = dk_part
            dv_acc[...] = dv_part

        @pl.when(i > 0)
        def _():
            dk_acc[...] += dk_part
            dv_acc[...] += dv_part

        @pl.when(i == nq - 1)
        def _():
            dkn_ref[...] = dk_acc[:, :LANE].astype(dkn_ref.dtype)
            dkr_ref[...] = dk_acc[:, LANE:].astype(dkr_ref.dtype)
            dv_ref[...] = dv_acc[...].astype(dv_ref.dtype)

    q_spec = pl.BlockSpec((tq, LANE), lambda h, i: (i, h))
    k_spec = pl.BlockSpec((S, LANE), lambda h, i: (0, h))
    t_shape, s_shape = jax.ShapeDtypeStruct((T, H * LANE), BF16), jax.ShapeDtypeStruct((S, H * LANE), BF16)
    return pl.pallas_call(
        body, name="attn_bwd", grid=(H, nq),
        in_specs=[q_spec, q_spec, k_spec, pl.BlockSpec((S, LANE), lambda h, i: (0, H + h)),
                  pl.BlockSpec((S, LANE), lambda h, i: (0, 0)), q_spec, pl.BlockSpec((1, tq, 1), lambda h, i: (h, i, 0))],
        out_specs=[q_spec, q_spec, k_spec, k_spec, k_spec], out_shape=[t_shape, t_shape, s_shape, s_shape, s_shape],
        scratch_shapes=[pltpu.VMEM((S, 2 * LANE), F32), pltpu.VMEM((S, LANE), F32)],
        compiler_params=_cparams(("parallel", "arbitrary")),
    )(qn, qr, kv, kv, kr, do, lse)


def _rope_tables(T):
    rows = T // GRID_W
    row = jnp.repeat(jnp.arange(rows, dtype=F32), GRID_W)
    col = jnp.tile(jnp.arange(GRID_W, dtype=F32), rows)
    n_freq = QK_ROPE // 4
    inv = ROPE_BASE ** (-jnp.arange(n_freq, dtype=F32) / n_freq)
    ar, ac = row[:, None] * inv, col[:, None] * inv
    cos = jnp.concatenate([jnp.cos(ar), jnp.cos(ar), jnp.cos(ac), jnp.cos(ac)], axis=1)
    sin = jnp.concatenate([-jnp.sin(ar), jnp.sin(ar), -jnp.sin(ac), jnp.sin(ac)], axis=1)
    pad = lambda t: jnp.pad(t, ((0, 0), (0, LANE - QK_ROPE)))
    return pad(cos), pad(sin)


def _dw(a, dy, w, name):
    return _mm(a, dy, "tn", BF16, name, out_slots=w.shape[0] if w.ndim == 3 else None)


def _local_step(x, ctx, tgt, m_lat, m_ctx, p, W, goff, hooks=None):
    T, D = x.shape
    Tc = ctx.shape[0]
    S = T + Tc
    S5W = p["s5_d"].shape[1]
    QR, KVR = p["q_norm"].shape[1], p["kv_norm"].shape[1]
    G, N = p["s5_a_re"].shape[1:]
    P = S5_GROUP
    nch = G // 8
    o_cq, o_ckv, o_kr = S5W, S5W + QR, S5W + QR + KVR
    assert o_cq % QR == 0 and o_ckv % KVR == 0 and o_kr % LANE == 0 and goff % D == 0 and S5W % LANE == 0 and G % 8 == 0
    assert 8 * P == LANE
    row = lambda k, m: m[k:k + 1]
    sh1, sc1, g1, sh2, sc2, g2 = (row(k, m_lat) for k in range(6))
    csh1, csc1 = row(0, m_ctx), row(1, m_ctx)
    n1, n2, nf = p["norm1"], p["norm2"], p["norm_f"]

    (xm_lat,) = _rowmap(_normmod, "norm1_lat", T, [x], [n1, sc1, sh1], [(D, BF16)])
    (xm_ctx,) = _rowmap(_normmod, "norm1_ctx", Tc, [ctx], [n1, csc1, csh1], [(D, BF16)])
    xm_all = jnp.concatenate([xm_lat, xm_ctx], axis=0)

    a_re, a_im = p["s5_a_re"][:, :, None, :], p["s5_a_im"][:, :, None, :]
    ldt = p["s5_log_dt"][:, :, None, None]
    b_re, b_im = p["s5_b_re"].transpose(0, 1, 3, 2), p["s5_b_im"].transpose(0, 1, 3, 2)
    wre, wim, vre, vim, pwr, pwi = _s5_tables(a_re, a_im, ldt, b_re, b_im, p["s5_c_re"], p["s5_c_im"])
    waug = _s5_expand(wre, wim, "s5_expand_b")
    vaug = _s5_expand(vre, vim, "s5_expand_c")
    lanes = lambda t: t.reshape(2, SUB + 1, nch, 8 * N).transpose(0, 2, 1, 3)
    pw_re, pw_im = lanes(pwr), lanes(pwi)
    near = lambda t: t[:, :, 1:]
    far = lambda t: t[:, :, :0:-1]
    pw_c = jnp.concatenate([near(pw_re), near(pw_im)], axis=-1)
    pw_a = jnp.concatenate([far(pw_re), far(pw_im)], axis=-1)
    pwc_c = jnp.concatenate([near(pw_re), -near(pw_im)], axis=-1)
    pwc_a = jnp.concatenate([far(pw_re), -far(pw_im)], axis=-1)

    if hooks:
        W = {**W, **hooks["first_weights"](xm_all, vaug)}
    H = W["w_uq"].shape[1] // (2 * LANE)
    h_all = _mm(xm_all, W["w_in"], "nn", F32, "mm_in")
    y0, hs0 = _s5_fwd(h_all, waug, vaug, pw_c[0], S5W, T, 0, True, "s5_scan_fwd0")
    y1, hs1 = _s5_fwd(h_all, waug, vaug, pw_a[1], S5W, T, 1, False, "s5_scan_fwd1")

    def s5_combine(u, yf, yr, dskip):
        y5 = dskip * u + yf + yr
        return y5, jax.nn.gelu(y5)

    y5, z = _rowmap(s5_combine, "s5_combine", T, [(h_all, S5W, 0), y0, y1], [p["s5_d"]], [(S5W, F32), (S5W, BF16)])

    (qn,) = _rowmap(_rms, "q_norm", T, [(h_all, QR, o_cq // QR)], [p["q_norm"]], [(QR, BF16)])
    (kvn,) = _rowmap(_rms, "kv_norm", S, [(h_all, KVR, o_ckv // KVR)], [p["kv_norm"]], [(KVR, BF16)])
    qraw = _mm(qn, W["w_uq"], "nn", F32, "mm_uq")
    kvraw = _mm(kvn, W["w_ukv"], "nn", BF16, "mm_ukv")
    cos_q, sin_q = _rope_tables(T)
    padl = lambda t: jnp.pad(t[:, :LANE], ((0, Tc), (0, 0)))
    cos_k = padl(cos_q) + jnp.pad(jnp.ones((Tc, LANE), F32), ((T, 0), (0, 0)))
    sin_k = padl(sin_q)
    hn = H * LANE

    def q_post(q, cos, sin):
        return q[:, :hn], _rope(q[:, hn:], jnp.tile(cos, (1, H)), jnp.tile(sin, (1, H)))

    q_nope, q_rope = _rowmap(q_post, "q_rope", T, [qraw, cos_q, sin_q], [], [(hn, BF16), (hn, BF16)])
    (kr,) = _rowmap(_rope, "k_rope", S, [(h_all, LANE, o_kr // LANE), cos_k, sin_k], [], [(LANE, BF16)])
    scale = (QK_NOPE + QK_ROPE) ** -0.5
    o, lse = _attn_fwd(q_nope, q_rope, kvraw, kr, H, scale)
    g1_fwd = g1
    if hooks:
        W = {**W, **hooks["mix_weights"](o)}
        g1_fwd = g1 + hooks["ffn_mid"](o)[:1, :1]

    zz = _mm(z, W["w_glu"], "nn", BF16, "mm_glu")
    br_mla = _mm(o, W["w_mla_o"], "nn", BF16, "mm_mla_o")

    def merge(zz, brm, gs, gm):
        a, b = zz[:, :D], zz[:, D:]
        return jax.nn.sigmoid(gs) * (a * jax.nn.sigmoid(b)) + jax.nn.sigmoid(gm) * brm

    gb = goff // D
    merge_ins = [zz, br_mla, (h_all, D, gb), (h_all, D, gb + 1)]
    (mix,) = _rowmap(merge, "merge", T, merge_ins, [], [(D, BF16)])
    out1 = _mm(mix, W["w_out"], "nn", F32, "mm_out")

    def resid_norm2(x, out1, g1, n2, sc2, sh2):
        x1 = x + g1 * out1
        return x1, _normmod(x1, n2, sc2, sh2)

    x1, hm = _rowmap(resid_norm2, "resid_norm2", T, [x, out1], [g1_fwd, n2, sc2, sh2], [(D, F32), (D, BF16)])

    if hooks:
        W = {**W, **hooks["ffn_weights"](hm)}
    FF = W["w_ffn_out"].shape[0]
    assert FF % LANE == 0
    ab = _mm(hm, W["w_ffn_in"], "nn", BF16, "mm_ffn_in")

    def swiglu_act(a, b):
        return jax.nn.silu(a) * b

    (f,) = _rowmap(swiglu_act, "ffn_act", T, [(ab, FF, 0), (ab, FF, 1)], [], [(FF, BF16)])
    out2 = _mm(f, W["w_ffn_out"], "nn", F32, "mm_ffn_out")

    def loss_rows(x1, out2, g2, nf, tgt):
        y = _rms(x1 + g2 * out2, nf)
        return 0.5 * jnp.sum(jnp.mean(jnp.square(y - tgt), axis=-1))

    def final(x1, out2, tgt, g2, nf):
        val, (dx1, dout2, dg2, dnf) = jax.value_and_grad(loss_rows, argnums=(0, 1, 2, 3))(x1, out2, g2, nf, tgt)
        return dx1, dout2, jnp.full((1, LANE), val, F32), dg2, dnf

    dx2, dout2, loss_acc, dg2, dnf = _rowmap(final, "final_loss", T, [x1, out2, tgt], [g2, nf],
                                             [(D, F32), (D, BF16)], [LANE, D, D])

    gW = {}
    df = _mm(dout2, W["w_ffn_out"], "nt", BF16, "mm_ffn_out_dx")
    gW["w_ffn_out"] = _dw(f, dout2, W["w_ffn_out"], "mm_ffn_out_dw")

    def swiglu_bwd(a, b, df):
        _, vjp = jax.vjp(swiglu_act, a, b)
        da, db = vjp(df)
        return jnp.concatenate([da, db], axis=1)

    (dab,) = _rowmap(swiglu_bwd, "ffn_act_bwd", T, [(ab, FF, 0), (ab, FF, 1), df], [], [(2 * FF, BF16)])
    dhm = _mm(dab, W["w_ffn_in"], "nt", F32, "mm_ffn_in_dx")
    gW["w_ffn_in"] = _dw(hm, dab, W["w_ffn_in"], "mm_ffn_in_dw")
    if hooks:
        token = hooks["send_grads"](FFN, [gW.pop(n) for n in FFN])
        g1 = g1 if token is None else g1 + token[:1, :1]

    def resid_norm2_bwd(x, out1, dx2, dhm, g1, n2, sc2, sh2):
        _, vjp = jax.vjp(resid_norm2, x, out1, g1, n2, sc2, sh2)
        dx, dout1, dg1, dn2, dsc2, dsh2 = vjp((dx2, dhm))
        return dx, dout1, dg1, dn2, dsc2, dsh2

    dx1, dout1, dg1, dn2, dsc2, dsh2 = _rowmap(resid_norm2_bwd, "resid_norm2_bwd", T, [x, out1, dx2, dhm],
                                               [g1, n2, sc2, sh2], [(D, F32), (D, BF16)], [D, D, D, D])

    dmix = _mm(dout1, W["w_out"], "nt", BF16, "mm_out_dx")
    gW["w_out"] = _dw(mix, dout1, W["w_out"], "mm_out_dw")

    def merge_bwd(zz, brm, gs, gm, dmix):
        _, vjp = jax.vjp(merge, zz, brm, gs, gm)
        dzz, dbrm, dgs, dgm = vjp(dmix)
        return dzz, dbrm, jnp.concatenate([dgs, dgm], axis=1)

    dzz, dbrm, dgates = _rowmap(merge_bwd, "merge_bwd", T, merge_ins + [dmix], [],
                                [(2 * D, BF16), (D, BF16), (2 * D, BF16)])
    do = _mm(dbrm, W["w_mla_o"], "nt", BF16, "mm_mla_o_dx")
    gW["w_mla_o"] = _dw(o, dbrm, W["w_mla_o"], "mm_mla_o_dw")
    dz = _mm(dzz, W["w_glu"], "nt", BF16, "mm_glu_dx")
    gW["w_glu"] = _dw(z, dzz, W["w_glu"], "mm_glu_dw")
    d_skip_w = p["s5_d"]
    if hooks:
        token = hooks["send_grads"](MIX, [gW.pop(n) for n in MIX])
        d_skip_w = d_skip_w if token is None else d_skip_w + token[:1, :1]

    def s5_combine_bwd(u, y5, dz, dskip):
        _, vjp = jax.vjp(lambda y: jax.nn.gelu(y), y5)
        (dy5,) = vjp(dz)
        return dy5, jnp.sum(dy5 * u, axis=0, keepdims=True)

    dy5, d_skip = _rowmap(s5_combine_bwd, "s5_combine_bwd", T, [(h_all, S5W, 0), y5, dz], [d_skip_w], [(S5W, F32)], [S5W])

    dq_nope, dq_rope, dk_nope, dkr_heads, dv = _attn_bwd(q_nope, q_rope, kvraw, kr, do, lse, H, scale)

    def q_post_bwd(dqn, dqr, cos, sin):
        return jnp.concatenate([dqn, _rope_bwd(dqr, jnp.tile(cos, (1, H)), jnp.tile(sin, (1, H)))], axis=1)

    (dqraw,) = _rowmap(q_post_bwd, "q_rope_bwd", T, [dq_nope, dq_rope, cos_q, sin_q], [], [(2 * hn, BF16)])
    dkvraw = jnp.concatenate([dk_nope, dv], axis=1)

    def k_rope_bwd(dkh, cos, sin):
        d = dkh[:, :LANE]
        for h in range(1, H):
            d = d + dkh[:, h * LANE:(h + 1) * LANE]
        return _rope_bwd(d, cos, sin)

    (dkr,) = _rowmap(k_rope_bwd, "k_rope_bwd", S, [dkr_heads, cos_k, sin_k], [], [(LANE, BF16)])
    dqn = _mm(dqraw, W["w_uq"], "nt", F32, "mm_uq_dx")
    gW["w_uq"] = _dw(qn, dqraw, W["w_uq"], "mm_uq_dw")
    dkvn = _mm(dkvraw, W["w_ukv"], "nt", F32, "mm_ukv_dx")
    gW["w_ukv"] = _dw(kvn, dkvraw, W["w_ukv"], "mm_ukv_dw")

    def rms_bwd(cx, dn, g):
        _, vjp = jax.vjp(_rms, cx, g)
        return vjp(dn)

    dcq, dq_norm = _rowmap(rms_bwd, "q_norm_bwd", T, [(h_all, QR, o_cq // QR), dqn], [p["q_norm"]], [(QR, BF16)], [QR])
    dckv, dkv_norm = _rowmap(rms_bwd, "kv_norm_bwd", S, [(h_all, KVR, o_ckv // KVR), dkvn], [p["kv_norm"]],
                             [(KVR, BF16)], [KVR])

    dy_all = jnp.concatenate([dy5, jnp.zeros((Tc, S5W), F32)], axis=0)
    du0, dbb0, dc0, da0 = _s5_bwd(dy_all, h_all, hs0, waug, vaug, pwc_a[0], S5W, T, 0, True, "s5_scan_bwd0")
    du1, dbb1, dc1, da1 = _s5_bwd(dy_all, h_all, hs1, waug, vaug, pwc_c[1], S5W, T, 1, False, "s5_scan_bwd1")

    def du_combine(a, b, dy, dskip):
        return a + b + dskip * dy

    (du_all,) = _rowmap(du_combine, "s5_du", S, [du0, du1, dy_all], [p["s5_d"]], [(S5W, BF16)])
    dbb = jnp.einsum("dsgpcgn->dcsgpn", jnp.stack([dbb0, dbb1]).reshape(2, nch, 8, P, 2, 8, N)).reshape(2, 2, G, P, N)
    dcm = jnp.einsum("dscgngp->dcsgpn", jnp.stack([dc0, dc1]).reshape(2, nch, 2, 8, N, 8, P)).reshape(2, 2, G, P, N)
    da = jnp.stack([da0, da1]).reshape(2, nch, 2, 8, N).transpose(0, 2, 1, 3, 4).reshape(2, 2, G, 1, N)
    d_lr, d_li, d_ldt, d_br, d_bi = _s5_param_bwd(a_re, a_im, ldt, b_re, b_im, da[:, 0], da[:, 1], dbb[:, 0], dbb[:, 1])

    lat_only = lambda t: jnp.pad(t, ((0, Tc), (0, 0)))
    dh_all = jnp.concatenate([du_all, lat_only(dcq), dckv, dkr, jnp.zeros((S, goff - o_kr - LANE), BF16), lat_only(dgates)],
                             axis=1)
    dxm = _mm(dh_all, W["w_in"], "nt", F32, "mm_in_dx")
    gW["w_in"] = _dw(xm_all, dh_all, W["w_in"], "mm_in_dw")

    def norm1_bwd(x, dxm, dx1, n1, sc, sh):
        _, vjp = jax.vjp(_normmod, x, n1, sc, sh)
        dx, dn, dsc, dsh = vjp(dxm)
        return dx + dx1, dn, dsc, dsh

    grad_x, dn1_l, dsc1, dsh1 = _rowmap(norm1_bwd, "norm1_lat_bwd", T, [x, dxm, dx1], [n1, sc1, sh1], [(D, F32)], [D, D, D])

    def norm1_ctx_bwd(x, dxm, n1, sc, sh):
        _, vjp = jax.vjp(_normmod, x, n1, sc, sh)
        return vjp(dxm)[1:]

    dn1_c, dcsc1, dcsh1 = _rowmap(norm1_ctx_bwd, "norm1_ctx_bwd", Tc, [ctx, dxm[T:]], [n1, csc1, csh1], [], [D, D, D])

    zero = jnp.zeros((1, D), F32)
    dm_lat = jnp.concatenate([dsh1, dsc1, dg1, dsh2, dsc2, dg2], axis=0)
    dm_ctx = jnp.concatenate([dcsh1, dcsc1, zero, zero, zero, zero], axis=0)
    small = {
        "norm1": dn1_l + dn1_c, "norm2": dn2, "norm_f": dnf, "q_norm": dq_norm, "kv_norm": dkv_norm, "s5_d": d_skip,
        "s5_a_re": d_lr, "s5_a_im": d_li, "s5_log_dt": d_ldt, "s5_b_re": d_br.transpose(0, 1, 3, 2),
        "s5_b_im": d_bi.transpose(0, 1, 3, 2), "s5_c_re": dcm[:, 0], "s5_c_im": -dcm[:, 1],
    }
    return loss_acc[:, :1], grad_x, small, dm_lat, dm_ctx, gW


BIG = ("w_in", "w_uq", "w_ukv", "w_glu", "w_mla_o", "w_out", "w_ffn_in", "w_ffn_out")
FFN = ("w_ffn_in", "w_ffn_out")
MIX = ("w_out", "w_mla_o", "w_glu")
ROW_SHARDED = ("w_out", "w_ffn_out")
RELAID = ("w_in", "w_uq", "w_ukv")
SMALL = ("c_ctx", "b_mod", "norm1", "norm2", "s5_a_re", "s5_a_im", "s5_log_dt", "s5_b_re", "s5_b_im", "s5_c_re",
         "s5_c_im", "s5_d", "q_norm", "kv_norm", "norm_f")
S5_BULK = ("s5_b_re", "s5_b_im", "s5_c_re", "s5_c_im")
WEIGHTS = ("c_ctx", "w_mod", "b_mod", "norm1", "norm2", "w_in", "s5_a_re", "s5_a_im", "s5_log_dt", "s5_b_re", "s5_b_im",
           "s5_c_re", "s5_c_im", "s5_d", "w_glu", "q_norm", "kv_norm", "w_uq", "w_ukv", "w_mla_o", "w_out", "w_ffn_in",
           "w_ffn_out", "norm_f")


def _heads_split(w, heads, first):
    k = w.shape[0]
    w3 = w.reshape(k, heads, -1)
    return jnp.concatenate([w3[:, :, :first].reshape(k, -1), w3[:, :, first:].reshape(k, -1)], axis=1)


def _uq_layout(w, heads):
    k = w.shape[0]
    w3 = w.reshape(k, heads, QK_NOPE + QK_ROPE)
    rope = jnp.pad(w3[:, :, QK_NOPE:], ((0, 0), (0, 0), (0, LANE - QK_ROPE)))
    return jnp.concatenate([w3[:, :, :QK_NOPE].reshape(k, -1), rope.reshape(k, -1)], axis=1)


def _uq_unlayout(w, heads):
    k = w.shape[0]
    nope = w[:, :heads * QK_NOPE].reshape(k, heads, QK_NOPE)
    rope = w[:, heads * QK_NOPE:].reshape(k, heads, LANE)[:, :, :QK_ROPE]
    return jnp.concatenate([nope, rope], axis=2).reshape(k, -1)


def _heads_merge(w, heads, first):
    k = w.shape[0]
    a, b = w[:, :heads * first].reshape(k, heads, first), w[:, heads * first:].reshape(k, heads, -1)
    return jnp.concatenate([a, b], axis=2).reshape(k, -1)


def _cols_full(w8):
    return w8.transpose(1, 0, 2).reshape(w8.shape[1], -1)


def _cols_slots(w):
    return w.reshape(w.shape[0], N_DEV, -1).transpose(1, 0, 2)


def _weight_layout(n, w8):
    if n in ROW_SHARDED:
        return w8.reshape(-1, w8.shape[-1])
    return _cols_full(w8) if (n in RELAID or w8.shape[-1] % LANE) else w8


def _grad_slots(n, g):
    if g.ndim == 3:
        return g
    return g.reshape(N_DEV, g.shape[0] // N_DEV, g.shape[1]) if n in ROW_SHARDED else _cols_slots(g)


def _gate_offset(in_cols, D):
    return -(-(in_cols - 2 * D) // D) * D


def _model_weights(g8, D):
    W = {n: _weight_layout(n, w8) for n, w8 in g8.items()}
    w_in = W["w_in"]
    n_front = w_in.shape[1] - 2 * D
    goff = _gate_offset(w_in.shape[1], D)
    W["w_in"] = jnp.concatenate([w_in[:, :n_front], jnp.zeros((D, goff - n_front), w_in.dtype), w_in[:, n_front:]], axis=1)
    heads = W["w_uq"].shape[1] // (QK_NOPE + QK_ROPE)
    W["w_uq"] = _uq_layout(W["w_uq"], heads)
    W["w_ukv"] = _heads_split(W["w_ukv"], heads, QK_NOPE)
    return W, goff


def kernel(x, c, ctx, c_ctx, w_mod, b_mod, norm1, norm2, w_in, s5_a_re, s5_a_im, s5_log_dt, s5_b_re, s5_b_im, s5_c_re, s5_c_im, s5_d, w_glu, q_norm, kv_norm, w_uq, w_ukv, w_mla_o, w_out, w_ffn_in, w_ffn_out, norm_f, loss_target, m_c_ctx, m_w_mod, m_b_mod, m_norm1, m_norm2, m_w_in, m_s5_a_re, m_s5_a_im, m_s5_log_dt, m_s5_b_re, m_s5_b_im, m_s5_c_re, m_s5_c_im, m_s5_d, m_w_glu, m_q_norm, m_kv_norm, m_w_uq, m_w_ukv, m_w_mla_o, m_w_out, m_w_ffn_in, m_w_ffn_out, m_norm_f, v_c_ctx, v_w_mod, v_b_mod, v_norm1, v_norm2, v_w_in, v_s5_a_re, v_s5_a_im, v_s5_log_dt, v_s5_b_re, v_s5_b_im, v_s5_c_re, v_s5_c_im, v_s5_d, v_w_glu, v_q_norm, v_kv_norm, v_w_uq, v_w_ukv, v_w_mla_o, v_w_out, v_w_ffn_in, v_w_ffn_out, v_norm_f):
    a = dict(locals())
    D = x.shape[-1]
    me = 4 * lax.axis_index("x") + 2 * lax.axis_index("y") + lax.axis_index("c")

    shard = {n: a[n][0] for n in BIG}
    first = [n for n in BIG if n not in FFN + MIX]
    (cg,) = _all_gather([jnp.broadcast_to(c, (8, D))], "ag_c")
    goff = _gate_offset(w_in.shape[-1] * N_DEV, D)

    wm = w_mod[0]
    ncol = wm.shape[1]
    c16 = jnp.concatenate([cg[:, 0, :], c_ctx[None], jnp.zeros((7, D), F32)], axis=0)
    (s16,) = _rowmap(jax.nn.silu, "mod_silu", 16, [c16], [], [(D, BF16)])
    m_cols = _mm(s16, wm, "nn", F32, "mm_mod")
    (mg,) = _all_gather([m_cols], "ag_mod")
    (m16,) = _rowmap(lambda m, b: m + b, "mod_bias", 16, [_cols_full(mg)], [b_mod], [(N_DEV * ncol, F32)])

    first_blocks = [shard[n].astype(BF16) for n in first]
    fst = {}
    fst["sems1"], fst["thru"], first_token = _ag2_start(first_blocks, [_own_slot(b, me) for b in first_blocks], "ag_first_start")

    def first_weights(after_norm, after_tables):
        sems2, thru, _ = _ag2_mid(fst["sems1"], fst["thru"], after_tables, "ag_first_mid")
        lands = _ag2_end(fst["sems1"], sems2, thru, after_norm, "ag_first_end")
        return _model_weights(dict(zip(first, lands)), D)[0]

    mix_blocks = [shard[n].astype(BF16) for n in MIX]
    mix = _xchg_start(mix_blocks, [_own_slot(b, me) for b in mix_blocks], False, "ag_mix_start")
    ffn_blocks = [shard[n].astype(BF16) for n in FFN]
    ffn = {}
    ffn["sems1"], ffn["thru"], ag_token = _ag2_start(ffn_blocks, [_own_slot(b, me) for b in ffn_blocks], "ag_ffn_start")
    m16 = m16 + (first_token[:1, :1] + mix[3][:1, :1] + ag_token[:1, :1])

    def mix_weights(after):
        lands = _xchg_wait(mix[0], mix[1], mix[2], after, False, "ag_mix_wait")
        return {n: _weight_layout(n, w8) for n, w8 in zip(MIX, lands)}

    def ffn_mid(after):
        ffn["sems2"], ffn["thru"], token = _ag2_mid(ffn["sems1"], ffn["thru"], after, "ag_ffn_mid")
        return token

    def ffn_weights(after):
        lands = _ag2_end(ffn["sems1"], ffn["sems2"], ffn["thru"], after, "ag_ffn_end")
        return {n: _weight_layout(n, w8) for n, w8 in zip(FFN, lands)}

    rs_async = {}

    def send_grads(names, gs):
        slots = [_grad_slots(n, g) for n, g in zip(names, gs)]
        lands = [_own_slot(lax.dynamic_index_in_dim(s, me, 0, keepdims=False), me) for s in slots]
        rs_async[names] = _xchg_start(slots, lands, True, "rs_start_" + names[0])
        return rs_async[names][3]

    m_lat = lax.dynamic_slice(m16, (me, 0), (1, 6 * D)).reshape(6, D)
    m_ctx = m16[8].reshape(6, D)

    p = {n: a[n][0] for n in ("norm1", "norm2", "s5_a_re", "s5_a_im", "s5_log_dt", "s5_b_re", "s5_b_im", "s5_c_re",
                              "s5_c_im", "q_norm", "kv_norm")}
    p = {k: (v[None] if v.ndim == 1 else v) for k, v in p.items()}
    p["s5_d"] = s5_d.reshape(1, -1)
    p["norm_f"] = norm_f[None]
    hooks = dict(first_weights=first_weights, mix_weights=mix_weights, ffn_mid=ffn_mid, ffn_weights=ffn_weights,
                 send_grads=send_grads)
    loss_part, grad_x, small, dm_lat, dm_ctx, gW = _local_step(x[0], ctx[0], loss_target[0], m_lat, m_ctx, p, {}, goff, hooks)
    loss = lax.psum(loss_part[0, 0], ("x", "y", "c"))

    gW = dict(gW)
    n_front = w_in.shape[-1] * N_DEV - 2 * D
    gW["w_in"] = jnp.concatenate([gW["w_in"][:, :n_front], gW["w_in"][:, goff:]], axis=1)
    heads = gW["w_uq"].shape[1] // (2 * LANE)
    gW["w_uq"] = _uq_unlayout(gW["w_uq"], heads)
    gW["w_ukv"] = _heads_merge(gW["w_ukv"], heads, QK_NOPE)
    last = [n for n in BIG if n in gW]
    slots = [_grad_slots(n, gW[n]) for n in last]
    from_sibling = _rs_pair(slots, "rs_pair")
    chip_sums = [_add_pair(pp, rr, "rs_add_" + n, grad_x) for n, pp, rr in zip(last, slots, from_sibling)]

    dm8 = jnp.concatenate([dm_lat.reshape(1, -1), dm_ctx.reshape(1, -1), jnp.zeros((SUB - 2, 6 * D), F32)], axis=0)
    (dmg,) = _all_gather([dm8], "ag_dmod", after=chip_sums[0])
    dm_sum = _sum_slots(dmg, "sum_dmod")
    dM16 = jnp.concatenate([dmg[:, 0, :], dm_sum[1:2], jnp.zeros((7, 6 * D), F32)], axis=0)
    (g_b_mod,) = _rowmap(lambda d: jnp.sum(d, axis=0, keepdims=True), "b_mod_grad", 16, [dM16], [], [], [6 * D])
    dM_loc = lax.dynamic_slice(dM16, (0, me * ncol), (16, ncol))
    g_w_mod = _mm(s16, dM_loc, "tn", F32, "mm_mod_dw")
    ds16_part = _mm(dM_loc, wm, "nt", F32, "mm_mod_dx")

    fine = [n for n in SMALL if n not in ("c_ctx", "b_mod") + S5_BULK]
    small_blocks = [_pack_rows([small[n] for n in fine] + [ds16_part[8:9]], F32), _pack_rows([small[n] for n in S5_BULK], BF16)]
    sm_sems1, sm_thru, sm_token = _ag2_start(small_blocks, [_own_slot(b, me) for b in small_blocks], "ag_small_start")
    grads = {"b_mod": g_b_mod}

    def small_grads(after):
        sems2, thru, token = _ag2_mid(sm_sems1, sm_thru, after, "ag_small_mid")
        sg, sgb = _ag2_end(sm_sems1, sems2, thru, token, "ag_small_end")
        parts = _unpack_rows(_sum_slots(sg, "sum_small"), [small[n].shape for n in fine] + [(1, D)])
        grads.update(zip(fine, parts[:-1]))
        grads.update(zip(S5_BULK, _unpack_rows(_sum_slots(sgb, "sum_small_bulk"), [small[n].shape for n in S5_BULK])))

        def silu_bwd(cc, ds):
            _, vjp = jax.vjp(jax.nn.silu, cc)
            return vjp(ds)[0]

        (grads["c_ctx"],) = _rowmap(silu_bwd, "c_ctx_grad", 1, [c_ctx[None], parts[-1]], [], [(D, F32)])


    my_chip = 2 * lax.axis_index("x") + lax.axis_index("y")
    lands = [_own_slot(lax.dynamic_index_in_dim(q, my_chip, 0, keepdims=False) + sm_token[:1, :1].astype(q.dtype), my_chip, N_CHIP)
             for q in chip_sums]
    rs_send, rs_recv, rs_thru, behind = _xchg_start(chip_sums, lands, True, "rs_chips_start")
    partials = {}
    for names, (send, recv, thru, _) in rs_async.items():
        partials.update(zip(names, _xchg_wait(send, recv, thru, behind, True, "rs_wait_" + names[0])))

    out = {}

    def adamw_big(n, after):
        w2, m2, v2 = a[n][0], a["m_" + n][0], a["v_" + n][0]
        if n in partials:
            g, d, nm, nv = _adamw_slots(w2, partials[n], m2, v2, "adamw_" + n, after)
        else:
            g = g_w_mod
            d, nm, nv = _adamw(w2, g, m2, v2, "adamw_" + n, after)
        for k, val in (("grad_", g), ("delta_", d), ("new_m_", nm), ("new_v_", nv)):
            out[k + n] = val.reshape(a[n].shape)
        return nv

    for n in FFN + MIX + ("w_mod",):
        behind = adamw_big(n, behind)
    small_grads(behind)
    packs = [_pack_rows([t[n] for n in SMALL], F32) for t in (
        {n: a[n] for n in SMALL}, {n: grads[n] for n in SMALL}, {n: a["m_" + n] for n in SMALL}, {n: a["v_" + n] for n in SMALL})]
    res = _adamw(*packs, "adamw_small")
    partials.update(zip(last, _xchg_wait(rs_send, rs_recv, rs_thru, res[2], True, "rs_chips_wait")))
    for n in last:
        adamw_big(n, None)
    shapes = [a[n].shape for n in SMALL]
    for k, packed in (("grad_", packs[1]), ("delta_", res[0]), ("new_m_", res[1]), ("new_v_", res[2])):
        for n, val in zip(SMALL, _unpack_rows(packed, shapes)):
            out[k + n] = val
    return (loss, grad_x[None]) + tuple(out[k + n] for k in ("grad_", "delta_", "new_m_", "new_v_") for n in WEIGHTS)
```

```python
import functools
import math

import jax
import jax.numpy as jnp
from jax import lax
from jax.experimental import pallas as pl
from jax.experimental.pallas import tpu as pltpu

F32 = jnp.float32
BF16 = jnp.bfloat16

N_DEV = 8
N_CHIP = 4
EPS = 1e-6
GRID_W = 64
S5_GROUP = 16
QK_NOPE, QK_ROPE, V_DIM = 128, 64, 128
ROPE_BASE = 10000.0
ADAM_LR, ADAM_B1, ADAM_B2, ADAM_EPS, ADAM_WD, ADAM_STEP = 0.001, 0.9, 0.999, 1e-08, 0.01, 10

LANE = 128
SUB = 8
PACK_W = 1024
PACK_ROWS = 32
VMEM_LIMIT = 48 << 20
ROWMAP_TILE_BYTES = 20 << 20
MM_VMEM_BUDGET = 36 << 20
MESH = pl.DeviceIdType.MESH
_NT = (((1,), (1,)), ((), ()))
_TN = (((0,), (0,)), ((), ()))


def _pick(dim, cands):
    for c in cands:
        if dim % c == 0:
            return c
    return dim


def _cparams(sem):
    return pltpu.CompilerParams(dimension_semantics=sem, vmem_limit_bytes=VMEM_LIMIT)


def _mm(a, b, dims, out_dtype, name, out_slots=None):
    a = a.astype(BF16)
    b = b.astype(BF16)
    b3 = b.ndim == 3
    if dims == "nn":
        (M, K), N = a.shape, (b.shape[0] * b.shape[2] if b3 else b.shape[1])
    elif dims == "nt":
        M, N = a.shape[0], b.shape[-2]
        K = b.shape[0] * b.shape[2] if b3 else b.shape[1]
    else:
        (K, M), N = a.shape, b.shape[1]
    unit_n = b.shape[2] if (b3 and dims == "nn") else (N // out_slots if out_slots else N)
    unit_k = b.shape[2] if (b3 and dims == "nt") else K
    osz = jnp.dtype(out_dtype).itemsize
    tm, tn, tk = _mm_tiles(M, unit_n, unit_k, osz, LANE if dims == "tn" else 16)
    nk, npt, kpt = K // tk, unit_n // tn, unit_k // tk
    use_acc = nk > 1 and out_dtype != F32
    if dims == "nn":
        a_spec = pl.BlockSpec((tm, tk), lambda i, j, k: (i, k))
        b_spec = (pl.BlockSpec((None, tk, tn), lambda i, j, k: (j // npt, k, j % npt)) if b3
                  else pl.BlockSpec((tk, tn), lambda i, j, k: (k, j)))
        dn = (((1,), (0,)), ((), ()))
    elif dims == "nt":
        a_spec = pl.BlockSpec((tm, tk), lambda i, j, k: (i, k))
        b_spec = (pl.BlockSpec((None, tn, tk), lambda i, j, k: (k // kpt, j, k % kpt)) if b3
                  else pl.BlockSpec((tn, tk), lambda i, j, k: (j, k)))
        dn = _NT
    else:
        a_spec = pl.BlockSpec((tk, tm), lambda i, j, k: (k, i))
        b_spec = pl.BlockSpec((tk, tn), lambda i, j, k: (k, j))
        dn = _TN
    if out_slots:
        out_spec = pl.BlockSpec((None, tm, tn), lambda i, j, k: (j // npt, i, j % npt))
        out_shape = jax.ShapeDtypeStruct((out_slots, M, unit_n), out_dtype)
    else:
        out_spec = pl.BlockSpec((tm, tn), lambda i, j, k: (i, j))
        out_shape = jax.ShapeDtypeStruct((M, N), out_dtype)

    def body(a_ref, b_ref, o_ref, *scratch):
        part = lax.dot_general(a_ref[...], b_ref[...], dn, preferred_element_type=F32)
        if nk == 1:
            o_ref[...] = part.astype(o_ref.dtype)
            return
        acc_ref = scratch[0] if use_acc else o_ref
        k = pl.program_id(2)

        @pl.when(k == 0)
        def _():
            acc_ref[...] = part

        @pl.when(k > 0)
        def _():
            acc_ref[...] += part

        if use_acc:
            @pl.when(k == nk - 1)
            def _():
                o_ref[...] = acc_ref[...].astype(o_ref.dtype)

    return pl.pallas_call(
        body, name=name, grid=(M // tm, N // tn, nk),
        in_specs=[a_spec, b_spec], out_specs=out_spec, out_shape=out_shape,
        scratch_shapes=[pltpu.VMEM((tm, tn), F32)] if use_acc else [],
        compiler_params=_cparams(("parallel", "parallel", "arbitrary")),
    )(a, b)


def _mm_swiglu(x, w3, act, name):
    x = x.astype(BF16)
    M, K = x.shape
    ns, _, n = w3.shape
    half = ns // 2
    tm = _divisors(M, 16, 256)[0]

    def body(x_ref, wa_ref, wb_ref, a_ref, b_ref, f_ref):
        a = jnp.dot(x_ref[...], wa_ref[...], preferred_element_type=F32)
        b = jnp.dot(x_ref[...], wb_ref[...], preferred_element_type=F32)
        a_ref[...] = a.astype(a_ref.dtype)
        b_ref[...] = b.astype(b_ref.dtype)
        f_ref[...] = act(a, b).astype(f_ref.dtype)

    out = pl.BlockSpec((tm, n), lambda s, i: (i, s))
    return pl.pallas_call(
        body, name=name, grid=(half, M // tm),
        in_specs=[pl.BlockSpec((tm, K), lambda s, i: (i, 0)), pl.BlockSpec((None, K, n), lambda s, i: (s, 0, 0)),
                  pl.BlockSpec((None, K, n), lambda s, i: (s + half, 0, 0))],
        out_specs=[out] * 3, out_shape=[jax.ShapeDtypeStruct((M, half * n), BF16)] * 3,
        compiler_params=_cparams(("parallel", "parallel")),
    )(x, w3, w3)


def _divisors(n, mult, cap):
    d = [t for t in range(mult, min(n, cap) + 1, mult) if n % t == 0]
    return d[::-1] or [n]


def _mm_tiles(M, unit_n, unit_k, out_itemsize, tm_mult):
    best = None
    for tk in _divisors(unit_k, LANE, 2816):
        for tn in _divisors(unit_n, LANE, 1536):
            for tm in _divisors(M, tm_mult, 1024):
                vmem = 2 * 2 * (tm * tk + tk * tn) + 2 * tm * tn * out_itemsize + 4 * tm * tn * (2 if unit_k > tk else 1)
                if vmem > MM_VMEM_BUDGET:
                    continue
                steps = (M // tm) * (unit_n // tn) * (unit_k // tk)
                key = (steps, -tk, -tn)
                if best is None or key < best[0]:
                    best = (key, (tm, tn, tk))
                break
    return best[1]


def _rowmap(fn, name, M, row_ins, bc_ins, row_outs, acc_outs=(), after=None):
    row_ins = [r if isinstance(r, tuple) else (r, r.shape[1], 0) for r in row_ins]
    row_bytes = sum(w * a.dtype.itemsize for a, w, _ in row_ins) + sum(w * jnp.dtype(d).itemsize for w, d in row_outs)
    widest = max([w for _, w, _ in row_ins] + [w for w, _ in row_outs])
    row_bytes = 2 * row_bytes + 6 * 4 * widest
    tm = _pick(M, [t for t in (512, 256, 128, 64, 32, 16) if t * row_bytes <= ROWMAP_TILE_BYTES] + [16])
    n_in, n_row, n_acc = len(row_ins) + len(bc_ins), len(row_outs), len(acc_outs)

    def body(*refs):
        res = fn(*[r[...].astype(F32) for r in refs[:n_in]])
        res = res if isinstance(res, (tuple, list)) else (res,)
        outs = refs[n_in + (after is not None):]
        for k in range(n_row):
            outs[k][...] = res[k].astype(outs[k].dtype)
        if n_acc:
            @pl.when(pl.program_id(0) == 0)
            def _():
                for k in range(n_acc):
                    outs[n_row + k][...] = jnp.zeros_like(outs[n_row + k])

            for k in range(n_acc):
                outs[n_row + k][...] += res[n_row + k].astype(F32)

    in_specs = [pl.BlockSpec((tm, w), functools.partial(lambda i, blk: (i, blk), blk=blk)) for _, w, blk in row_ins]
    in_specs += [pl.BlockSpec(b.shape, lambda i: (0, 0)) for b in bc_ins]
    in_specs += [pl.BlockSpec(memory_space=pl.ANY)] * (after is not None)
    out_specs = [pl.BlockSpec((tm, w), lambda i: (i, 0)) for w, _ in row_outs]
    out_specs += [pl.BlockSpec((1, w), lambda i: (0, 0)) for w in acc_outs]
    out_shape = [jax.ShapeDtypeStruct((M, w), d) for w, d in row_outs]
    out_shape += [jax.ShapeDtypeStruct((1, w), F32) for w in acc_outs]
    return pl.pallas_call(
        body, name=name, grid=(M // tm,), in_specs=in_specs, out_specs=out_specs, out_shape=out_shape,
        compiler_params=_cparams(("arbitrary",) if n_acc else ("parallel",)),
    )(*[a for a, _, _ in row_ins], *bc_ins, *([] if after is None else [after]))


def _rms(x, g):
    return x * lax.rsqrt(jnp.mean(x * x, axis=-1, keepdims=True) + EPS) * g


def _normmod(x, g, sc, sh):
    return _rms(x, g) * (1.0 + sc) + sh


def _swap16(v):
    w = v.shape[1]
    lane = lax.broadcasted_iota(jnp.int32, v.shape, 1)
    return jnp.where((lane // 16) % 2 == 0, pltpu.roll(v, w - 16, 1), pltpu.roll(v, 16, 1))


def _rope(v, cos, sin_signed):
    return v * cos + _swap16(v) * sin_signed


def _rope_bwd(d, cos, sin_signed):
    return d * cos + _swap16(d * sin_signed)


def _mesh_pos():
    return lax.axis_index("x"), lax.axis_index("y"), lax.axis_index("c")


def _hbm_call(body, name, ins, out_shapes, n_sems):
    any_spec = pl.BlockSpec(memory_space=pl.ANY)
    return pl.pallas_call(
        body, name=name, out_shape=out_shapes, in_specs=[any_spec] * len(ins), out_specs=[any_spec] * len(out_shapes),
        scratch_shapes=[pltpu.SemaphoreType.DMA((n_sems,)), pltpu.SemaphoreType.DMA((n_sems,)),
                        pltpu.SemaphoreType.DMA((len(ins),))],
    )(*ins)


def _all_gather(xs, name, after=None):
    n = len(xs)

    def body(*refs):
        k = n + (after is not None)
        x_refs, out_refs, (send_sems, recv_sems, local_sems) = refs[:n], refs[k:k + n], refs[k + n:]
        x, y, c = _mesh_pos()
        me, sibling = (x, y, c), (x, y, 1 - c)
        chips = [(1 - x, y), (x, 1 - y), (1 - x, 1 - y)]
        locals_, first, passed, arrivals = [], [], [], []
        for a in range(n):
            def slot(px, py, pc, a=a):
                return out_refs[a].at[4 * px + 2 * py + pc]

            def copy(k, block, to, src=None, a=a, slot=slot):
                return pltpu.make_async_remote_copy(
                    src_ref=slot(*block) if src is None else src, dst_ref=slot(*block),
                    send_sem=send_sems.at[7 * a + k], recv_sem=recv_sems.at[7 * a + k], device_id=to, device_id_type=MESH)

            locals_.append(pltpu.make_async_copy(x_refs[a], slot(*me), local_sems.at[a]))
            first.append(copy(0, me, sibling, src=x_refs[a]))
            first += [copy(1 + j, me, (*chip, c), src=x_refs[a]) for j, chip in enumerate(chips)]
            passed.append([copy(4 + j, (*chip, c), sibling) for j, chip in enumerate(chips)])
            arrivals.append([copy(1 + j, (*chip, c), me) for j, chip in enumerate(chips)]
                            + [copy(0, sibling, me)] + [copy(4 + j, (*chip, 1 - c), me) for j, chip in enumerate(chips)])
        for cp in locals_ + first:
            cp.start()
        for j in range(3):
            for a in range(n):
                arrivals[a][j].wait_recv()
                passed[a][j].start()
        for a in range(n):
            for cp in arrivals[a][3:]:
                cp.wait_recv()
        for cp in first + [p for ps in passed for p in ps]:
            cp.wait_send()
        for cp in locals_:
            cp.wait()

    return _hbm_call(body, name, list(xs) + ([] if after is None else [after]),
                     [jax.ShapeDtypeStruct((N_DEV,) + x.shape, x.dtype) for x in xs], 7 * n)


def _rs_pair(ps, name):
    n = len(ps)

    def body(*refs):
        p_refs, out_refs, (send_sems, recv_sems, _) = refs[:n], refs[n:2 * n], refs[2 * n:]
        x, y, c = _mesh_pos()
        sends, recvs = [], []
        for a in range(n):
            for q in range(N_CHIP):
                sem = dict(send_sem=send_sems.at[4 * a + q], recv_sem=recv_sems.at[4 * a + q],
                           device_id=(x, y, 1 - c), device_id_type=MESH)
                sends.append(pltpu.make_async_remote_copy(src_ref=p_refs[a].at[2 * q + 1 - c], dst_ref=out_refs[a].at[q], **sem))
                recvs.append(pltpu.make_async_remote_copy(src_ref=p_refs[a].at[2 * q + c], dst_ref=out_refs[a].at[q], **sem))
        for cp in sends:
            cp.start()
        for cp in recvs:
            cp.wait_recv()
        for cp in sends:
            cp.wait_send()

    return _hbm_call(body, name, ps, [jax.ShapeDtypeStruct((N_CHIP,) + p.shape[1:], p.dtype) for p in ps], 4 * n)


def _xchg_copies(src_refs, land_refs, send_sems, recv_sems, slot_src):
    x, y, c = _mesh_pos()
    sends, recvs = [], []
    for a, (src, land) in enumerate(zip(src_refs, land_refs)):
        chips = land.shape[0] == N_CHIP
        npeer = land.shape[0] - 1
        me = 2 * x + y if chips else 4 * x + 2 * y + c
        for r in range(1, npeer + 1):
            px = 1 - x if r & (2 if chips else 4) else x
            py = 1 - y if r & (1 if chips else 2) else y
            pc = c if chips else (1 - c if r & 1 else c)
            peer = 2 * px + py if chips else 4 * px + 2 * py + pc
            sem = dict(send_sem=send_sems.at[npeer * a + r - 1], recv_sem=recv_sems.at[npeer * a + r - 1],
                       device_id=(px, py, pc), device_id_type=MESH)
            s = src.at[peer] if slot_src else src
            sends.append(pltpu.make_async_remote_copy(src_ref=s, dst_ref=land.at[me], **sem))
            recvs.append(pltpu.make_async_remote_copy(src_ref=s, dst_ref=land.at[peer], **sem))
    return sends, recvs


_HBM = pl.BlockSpec(memory_space=pltpu.HBM)
_SEM = pl.BlockSpec(memory_space=pltpu.SEMAPHORE)
_EFFECT = pltpu.SideEffectType.DATAFLOW_SIDE_EFFECTING


def _xchg_start(srcs, lands, slot_src, name):
    n = len(srcs)

    def body(*refs):
        sends, _ = _xchg_copies(refs[:n], refs[n:2 * n], refs[2 * n], refs[2 * n + 1], slot_src)
        for cp in sends:
            cp.start()
        refs[-1][...] = jnp.zeros_like(refs[-1])

    bufs = list(srcs) + list(lands)
    n_sems = n * (lands[0].shape[0] - 1)
    res = pl.pallas_call(
        body, name=name,
        out_shape=(pltpu.SemaphoreType.DMA((n_sems,)), pltpu.SemaphoreType.DMA((n_sems,)))
        + tuple(pltpu.HBM(b.shape, b.dtype) for b in bufs) + (jax.ShapeDtypeStruct((SUB, LANE), F32),),
        in_specs=(_HBM,) * (2 * n), out_specs=(_SEM, _SEM) + (_HBM,) * (2 * n) + (pl.BlockSpec(memory_space=pltpu.VMEM),),
        input_output_aliases={i: 2 + i for i in range(2 * n)},
        compiler_params=pltpu.CompilerParams(has_side_effects=_EFFECT),
    )(*[pltpu.with_memory_space_constraint(b, pltpu.HBM) for b in bufs])
    return res[0], res[1], res[2:-1], res[-1]


def _xchg_wait(send_sems, recv_sems, thru, after, slot_src, name):
    n = len(thru) // 2

    def body(*refs):
        sends, recvs = _xchg_copies(refs[:n], refs[n:2 * n], refs[2 * n], refs[2 * n + 1], slot_src)
        for cp in sends:
            cp.wait_send()
        for cp in recvs:
            cp.wait_recv()

    res = pl.pallas_call(
        body, name=name, out_shape=tuple(pltpu.HBM(b.shape, b.dtype) for b in thru),
        in_specs=(_HBM,) * (2 * n) + (_SEM, _SEM, pl.BlockSpec(memory_space=pl.ANY)), out_specs=(_HBM,) * (2 * n),
        input_output_aliases={i: i for i in range(2 * n)},
        compiler_params=pltpu.CompilerParams(has_side_effects=_EFFECT),
    )(*thru, send_sems, recv_sems, after)
    return res[n:]


def _ag2_copy(land, sems, k, block, to, src=None):
    slot = land.at[4 * block[0] + 2 * block[1] + block[2]]
    return pltpu.make_async_remote_copy(src_ref=slot if src is None else src, dst_ref=slot, send_sem=sems[0].at[k],
                                        recv_sem=sems[1].at[k], device_id=to, device_id_type=MESH)


def _ag2_start(blocks, lands, name):
    n = len(blocks)

    def body(*refs):
        x, y, c = _mesh_pos()
        for a in range(n):
            sems = (refs[2 * n], refs[2 * n + 1])
            _ag2_copy(refs[n + a], sems, 4 * a, (x, y, c), (x, y, 1 - c), src=refs[a]).start()
            for j, chip in enumerate([(1 - x, y), (x, 1 - y), (1 - x, 1 - y)]):
                _ag2_copy(refs[n + a], sems, 4 * a + 1 + j, (x, y, c), (*chip, c), src=refs[a]).start()
        refs[-1][...] = jnp.zeros_like(refs[-1])

    bufs = list(blocks) + list(lands)
    res = pl.pallas_call(
        body, name=name,
        out_shape=(pltpu.SemaphoreType.DMA((4 * n,)), pltpu.SemaphoreType.DMA((4 * n,)))
        + tuple(pltpu.HBM(b.shape, b.dtype) for b in bufs) + (jax.ShapeDtypeStruct((SUB, LANE), F32),),
        in_specs=(_HBM,) * (2 * n), out_specs=(_SEM, _SEM) + (_HBM,) * (2 * n) + (pl.BlockSpec(memory_space=pltpu.VMEM),),
        input_output_aliases={i: 2 + i for i in range(2 * n)},
        compiler_params=pltpu.CompilerParams(has_side_effects=_EFFECT),
    )(*[pltpu.with_memory_space_constraint(b, pltpu.HBM) for b in bufs])
    return (res[0], res[1]), res[2:-1], res[-1]


def _ag2_mid(sems1, thru, after, name):
    n = len(thru) // 2

    def body(*refs):
        x, y, c = _mesh_pos()
        s1, s2 = (refs[2 * n], refs[2 * n + 1]), (refs[2 * n + 3], refs[2 * n + 4])
        for j, chip in enumerate([(1 - x, y), (x, 1 - y), (1 - x, 1 - y)]):
            for a in range(n):
                _ag2_copy(refs[n + a], s1, 4 * a + 1 + j, (*chip, c), (x, y, c)).wait_recv()
                _ag2_copy(refs[n + a], s2, 3 * a + j, (*chip, c), (x, y, 1 - c)).start()
        refs[-1][...] = jnp.zeros_like(refs[-1])

    res = pl.pallas_call(
        body, name=name,
        out_shape=(pltpu.SemaphoreType.DMA((3 * n,)), pltpu.SemaphoreType.DMA((3 * n,)))
        + tuple(pltpu.HBM(b.shape, b.dtype) for b in thru) + (jax.ShapeDtypeStruct((SUB, LANE), F32),),
        in_specs=(_HBM,) * (2 * n) + (_SEM, _SEM, pl.BlockSpec(memory_space=pl.ANY)),
        out_specs=(_SEM, _SEM) + (_HBM,) * (2 * n) + (pl.BlockSpec(memory_space=pltpu.VMEM),),
        input_output_aliases={i: 2 + i for i in range(2 * n)},
        compiler_params=pltpu.CompilerParams(has_side_effects=_EFFECT),
    )(*thru, *sems1, after)
    return (res[0], res[1]), res[2:-1], res[-1]


def _ag2_end(sems1, sems2, thru, after, name):
    n = len(thru) // 2

    def body(*refs):
        x, y, c = _mesh_pos()
        s1, s2 = (refs[2 * n], refs[2 * n + 1]), (refs[2 * n + 2], refs[2 * n + 3])
        chips = [(1 - x, y), (x, 1 - y), (1 - x, 1 - y)]
        for a in range(n):
            land = refs[n + a]
            _ag2_copy(land, s1, 4 * a, (x, y, c), (x, y, 1 - c), src=refs[a]).wait_send()
            _ag2_copy(land, s1, 4 * a, (x, y, 1 - c), (x, y, c)).wait_recv()
            for j, chip in enumerate(chips):
                _ag2_copy(land, s1, 4 * a + 1 + j, (x, y, c), (*chip, c), src=refs[a]).wait_send()
                _ag2_copy(land, s2, 3 * a + j, (*chip, c), (x, y, 1 - c)).wait_send()
                _ag2_copy(land, s2, 3 * a + j, (*chip, 1 - c), (x, y, c)).wait_recv()

    res = pl.pallas_call(
        body, name=name, out_shape=tuple(pltpu.HBM(b.shape, b.dtype) for b in thru),
        in_specs=(_HBM,) * (2 * n) + (_SEM,) * 4 + (pl.BlockSpec(memory_space=pl.ANY),), out_specs=(_HBM,) * (2 * n),
        input_output_aliases={i: i for i in range(2 * n)},
        compiler_params=pltpu.CompilerParams(has_side_effects=_EFFECT),
    )(*thru, *sems1, *sems2, after)
    return res[n:]


def _own_slot(block, me, slots=N_DEV):
    return lax.dynamic_update_slice(lax.empty((slots,) + block.shape, block.dtype), block[None], (me, 0, 0))


def _add_pair(p, r, name, after):
    _, R, C = p.shape
    tr = _pick(R, (512, 256, 128, 64, 32, 16))

    def body(c_ref, p_ref, r_ref, after_ref, o_ref):
        o_ref[...] = (p_ref[...].astype(F32) + r_ref[...].astype(F32)).astype(o_ref.dtype)

    return pl.pallas_call(
        body, name=name, out_shape=jax.ShapeDtypeStruct((N_CHIP, R, C), p.dtype),
        grid_spec=pltpu.PrefetchScalarGridSpec(
            num_scalar_prefetch=1, grid=(N_CHIP, R // tr),
            in_specs=[pl.BlockSpec((None, None, tr, C), lambda q, i, c_ref: (q, c_ref[0], i, 0)),
                      pl.BlockSpec((None, tr, C), lambda q, i, c_ref: (q, i, 0)), pl.BlockSpec(memory_space=pl.ANY)],
            out_specs=pl.BlockSpec((None, tr, C), lambda q, i, c_ref: (q, i, 0))),
        compiler_params=_cparams(("parallel", "parallel")),
    )(lax.axis_index("c").reshape(1).astype(jnp.int32), p.reshape(N_CHIP, 2, R, C), r, after)


def _sum_slots(g, name):
    ns, R, C = g.shape
    tr = _pick(R, (256, 128, 64, 32, 16))

    def body(g_ref, o_ref):
        acc = g_ref[0].astype(F32)
        for j in range(1, ns):
            acc = acc + g_ref[j].astype(F32)
        o_ref[...] = acc

    return pl.pallas_call(
        body, name=name, grid=(R // tr,),
        in_specs=[pl.BlockSpec((ns, tr, C), lambda i: (0, i, 0))], out_specs=pl.BlockSpec((tr, C), lambda i: (i, 0)),
        out_shape=jax.ShapeDtypeStruct((R, C), F32), compiler_params=_cparams(("parallel",)),
    )(g)


def _pack_rows(arrs, dtype):
    parts = []
    for a in arrs:
        flat = a.reshape(-1).astype(dtype)
        pad = (-flat.shape[0]) % (PACK_W * 16)
        parts.append(jnp.pad(flat, (0, pad)).reshape(-1, PACK_W))
    out = jnp.concatenate(parts, axis=0)
    return jnp.pad(out, ((0, (-out.shape[0]) % PACK_ROWS), (0, 0)))


def _packed_rows(shape):
    n = math.prod(shape)
    return (n + PACK_W * 16 - 1) // (PACK_W * 16) * 16


def _unpack_rows(packed, shapes):
    out, r0 = [], 0
    for s in shapes:
        rows, n = _packed_rows(s), math.prod(s)
        out.append(packed[r0:r0 + rows].reshape(rows * PACK_W)[:n].reshape(s))
        r0 += rows
    return out


def _adamw_math(w, g, m, v):
    m = ADAM_B1 * m + (1.0 - ADAM_B1) * g
    v = ADAM_B2 * v + (1.0 - ADAM_B2) * (g * g)
    m_hat = m / (1.0 - ADAM_B1 ** ADAM_STEP)
    v_hat = v / (1.0 - ADAM_B2 ** ADAM_STEP)
    delta = -ADAM_LR * (m_hat / (jnp.sqrt(v_hat) + ADAM_EPS) + ADAM_WD * w)
    return delta, m, v


def _adamw_slots(w, gs, m, v, name, after=None):
    ns, R, C = gs.shape
    row_bytes = 2 * (ns * C * gs.dtype.itemsize + 7 * C * 4) + 6 * 4 * C
    tr = _pick(R, [t for t in (512, 256, 128, 64, 32, 16) if t * row_bytes <= ROWMAP_TILE_BYTES] + [16])

    def body(w_ref, g_ref, m_ref, v_ref, *rest):
        outs = rest[(after is not None):]
        g = g_ref[0].astype(F32)
        for j in range(1, ns):
            g = g + g_ref[j].astype(F32)
        res = (g,) + _adamw_math(w_ref[...], g, m_ref[...], v_ref[...])
        for o_ref, val in zip(outs, res):
            o_ref[...] = val

    row = pl.BlockSpec((tr, C), lambda i: (i, 0))
    return pl.pallas_call(
        body, name=name, grid=(R // tr,),
        in_specs=[row, pl.BlockSpec((ns, tr, C), lambda i: (0, i, 0)), row, row]
        + [pl.BlockSpec(memory_space=pl.ANY)] * (after is not None),
        out_specs=[row] * 4, out_shape=[jax.ShapeDtypeStruct((R, C), F32)] * 4, compiler_params=_cparams(("parallel",)),
    )(w, gs, m, v, *([] if after is None else [after]))


def _adamw(w, g, m, v, name, after=None):
    R, C = w.shape
    return _rowmap(_adamw_math, name, R, [w, g, m, v], [], [(C, F32)] * 3, after=after)


def _s5_disc_math(lr, li, ldt, br, bi):
    dt = jnp.exp(ldt)
    mag = jnp.exp(lr * dt)
    ab_re, ab_im = mag * jnp.cos(li * dt), mag * jnp.sin(li * dt)
    den = lr * lr + li * li
    nr, ni = ab_re - 1.0, ab_im
    co_re = (nr * lr + ni * li) / den
    co_im = (ni * lr - nr * li) / den
    bb_re = co_re * br - co_im * bi
    bb_im = co_re * bi + co_im * br
    return ab_re, ab_im, bb_re, bb_im


def _s5_tables(a_re, a_im, ldt, b_re, b_im, c_re, c_im):
    _, G, P, N = b_re.shape
    nch = G // 8

    def body(lr_ref, li_ref, ldt_ref, br_ref, bi_ref, cr_ref, ci_ref, wre, wim, vre, vim, pwr, pwi):
        ar, ai, bb_re, bb_im = _s5_disc_math(lr_ref[0], li_ref[0], ldt_ref[0], br_ref[0], bi_ref[0])
        cr, ci = cr_ref[0], ci_ref[0]
        pr, pi = jnp.ones_like(ar), jnp.zeros_like(ar)
        for j in range(SUB + 1):
            pwr[0, j], pwi[0, j] = pr, pi
            if j < SUB:
                tabs = ((wre, bb_re * pr - bb_im * pi), (wim, bb_re * pi + bb_im * pr),
                        (vre, cr * pr - ci * pi), (vim, -(cr * pi + ci * pr)))
                for ref, val in tabs:
                    for s in range(nch):
                        ref[0, s, pl.ds(j * LANE, LANE), :] = val[s * 8:(s + 1) * 8].reshape(LANE, N).astype(BF16)
            pr, pi = pr * ar - pi * ai, pr * ai + pi * ar

    g1n = pl.BlockSpec((1, G, 1, N), lambda d: (d, 0, 0, 0))
    gpn = pl.BlockSpec((1, G, P, N), lambda d: (d, 0, 0, 0))
    tab = pl.BlockSpec((1, nch, SUB * LANE, N), lambda d: (d, 0, 0, 0))
    pw = pl.BlockSpec((1, SUB + 1, G, 1, N), lambda d: (d, 0, 0, 0, 0))
    s_tab = jax.ShapeDtypeStruct((2, nch, SUB * LANE, N), BF16)
    s_pw = jax.ShapeDtypeStruct((2, SUB + 1, G, 1, N), F32)
    return pl.pallas_call(
        body, name="s5_tables", grid=(2,),
        in_specs=[g1n, g1n, pl.BlockSpec((1, G, 1, 1), lambda d: (d, 0, 0, 0)), gpn, gpn, gpn, gpn],
        out_specs=[tab] * 4 + [pw] * 2, out_shape=[s_tab] * 4 + [s_pw] * 2,
        compiler_params=_cparams(("parallel",)),
    )(a_re, a_im, ldt, b_re, b_im, c_re, c_im)


def _s5_expand(t_re, t_im, name):
    _, nch, R, N = t_re.shape
    sw = 8 * N

    def body(re_ref, im_ref, o_ref):
        spread = (lax.broadcasted_iota(jnp.int32, (N, sw), 1) % N == lax.broadcasted_iota(jnp.int32, (N, sw), 0)).astype(BF16)
        row_g = (lax.broadcasted_iota(jnp.int32, (R, sw), 0) % LANE) // S5_GROUP
        keep = row_g == lax.broadcasted_iota(jnp.int32, (R, sw), 1) // N
        for half, ref in enumerate((re_ref, im_ref)):
            t = jnp.dot(ref[0, 0], spread, preferred_element_type=F32)
            o_ref[0, 0, :, pl.ds(half * sw, sw)] = jnp.where(keep, t, 0.0).astype(BF16)

    spec = pl.BlockSpec((1, 1, R, N), lambda d, s: (d, s, 0, 0))
    return pl.pallas_call(
        body, name=name, grid=(2, nch), in_specs=[spec, spec],
        out_specs=pl.BlockSpec((1, 1, R, 2 * sw), lambda d, s: (d, s, 0, 0)),
        out_shape=jax.ShapeDtypeStruct((2, nch, R, 2 * sw), BF16), compiler_params=_cparams(("parallel", "parallel")),
    )(t_re, t_im)


def _s5_param_bwd(a_re, a_im, ldt, b_re, b_im, da_re, da_im, dbb_re, dbb_im):
    _, G, P, N = b_re.shape

    def body(lr_ref, li_ref, ldt_ref, br_ref, bi_ref, dar, dai, dbr, dbi, o_lr, o_li, o_ldt, o_br, o_bi):
        _, vjp = jax.vjp(_s5_disc_math, lr_ref[0], li_ref[0], ldt_ref[0], br_ref[0], bi_ref[0])
        o_lr[0], o_li[0], o_ldt[0], o_br[0], o_bi[0] = vjp((dar[0], dai[0], dbr[0], dbi[0]))

    g1n = pl.BlockSpec((1, G, 1, N), lambda d: (d, 0, 0, 0))
    g11 = pl.BlockSpec((1, G, 1, 1), lambda d: (d, 0, 0, 0))
    gpn = pl.BlockSpec((1, G, P, N), lambda d: (d, 0, 0, 0))
    s_g1n, s_g11, s_gpn = (jax.ShapeDtypeStruct(s, F32) for s in ((2, G, 1, N), (2, G, 1, 1), (2, G, P, N)))
    return pl.pallas_call(
        body, name="s5_param_bwd", grid=(2,),
        in_specs=[g1n, g1n, g11, gpn, gpn, g1n, g1n, gpn, gpn], out_specs=[g1n, g1n, g11, gpn, gpn],
        out_shape=[s_g1n, s_g1n, s_g11, s_gpn, s_gpn], compiler_params=_cparams(("parallel",)),
    )(a_re, a_im, ldt, b_re, b_im, da_re, da_im, dbb_re, dbb_im)


def _tile_local_scan(u, w_ref, back):
    tb, sw2 = u.shape[0], w_ref.shape[1]
    sw, half = sw2 // 2, LANE // 2
    tau = lax.broadcasted_iota(jnp.int32, u.shape, 0) % SUB
    low = lax.broadcasted_iota(jnp.int32, u.shape, 1) < half
    parts = [u]
    for j in range(1, SUB):
        if back:
            parts.append(jnp.where(tau >= j, pltpu.roll(u, j, 0), 0.0))
        else:
            parts.append(jnp.where(tau <= SUB - 1 - j, pltpu.roll(u, tb - j, 0), 0.0))
    out = [None] * 4
    for h in range(2):
        pieces = [jnp.where(low, a, pltpu.roll(b, half, 1)) if h == 0 else jnp.where(low, pltpu.roll(a, half, 1), b)
                  for a, b in zip(parts[0::2], parts[1::2])]
        lhs = jnp.concatenate(pieces, axis=1).astype(BF16)
        rows = jnp.concatenate([w_ref[pl.ds(j * LANE + h * half, half), :] for j in range(SUB)], axis=0)
        for part in range(2):
            cols = rows[:, part * sw + h * (sw // 2):part * sw + (h + 1) * (sw // 2)]
            out[2 * part + h] = jnp.dot(lhs, cols, preferred_element_type=F32)
    return jnp.concatenate(out, axis=1)


def _cmul_add(tile, pw, carry, sw):
    pr, pi, cr, ci = pw[:, :sw], pw[:, sw:], carry[:, :sw], carry[:, sw:]
    return tile + jnp.concatenate([pr * cr - pi * ci, pr * ci + pi * cr], axis=1)


def _tile_scan(buf, base, ntile, pw, carry, sw, causal):
    def step(k, c):
        i = k if causal else ntile - 1 - k
        r = pl.multiple_of(base + i * SUB, SUB)
        tile = _cmul_add(buf[pl.ds(r, SUB), :], pw, c, sw)
        buf[pl.ds(r, SUB), :] = tile
        return tile[SUB - 1:SUB, :] if causal else tile[0:1, :]

    return lax.fori_loop(0, ntile, step, carry)


def _s5_fwd(h_all, waug, vaug, pw, S5W, T, d, causal, name):
    S = h_all.shape[0]
    _, nch, _, sw2 = waug.shape
    sw = sw2 // 2
    tb = _pick(math.gcd(T, S - T), (256, 128, 64, 32, 16))
    ntile, nt, off = tb // SUB, S // tb, T // tb
    rb = (lambda s, t: ((t + off) % nt, s)) if causal else (lambda s, t: (nt - 1 - t, s))

    def body(u_ref, w_ref, v_ref, p_ref, y_ref, h_ref, hblk, carry):
        @pl.when(pl.program_id(1) == 0)
        def _():
            carry[...] = jnp.zeros_like(carry)

        hblk[...] = _tile_local_scan(u_ref[...], w_ref, causal)
        carry[...] = _tile_scan(hblk, 0, ntile, p_ref[...], carry[...], sw, causal)
        hb = hblk[...].astype(BF16)
        h_ref[...] = hb
        y_ref[...] = lax.dot_general(hb, v_ref[...], _NT, preferred_element_type=F32)

    return pl.pallas_call(
        body, name=name, grid=(nch, nt),
        in_specs=[pl.BlockSpec((tb, LANE), rb),
                  pl.BlockSpec((None, None, SUB * LANE, sw2), lambda s, t: (d, s, 0, 0)),
                  pl.BlockSpec((None, None, LANE, sw2), lambda s, t: (d, s, 0, 0)),
                  pl.BlockSpec((None, SUB, sw2), lambda s, t: (s, 0, 0))],
        out_specs=[pl.BlockSpec((tb, LANE), rb), pl.BlockSpec((tb, sw2), rb)],
        out_shape=[jax.ShapeDtypeStruct((S, S5W), F32), jax.ShapeDtypeStruct((S, nch * sw2), BF16)],
        scratch_shapes=[pltpu.VMEM((tb, sw2), F32), pltpu.VMEM((1, sw2), F32)],
        compiler_params=_cparams(("parallel", "arbitrary")),
    )(h_all, waug, vaug, pw)


def _s5_bwd(dy_all, h_all, hs, waug, vaug, pwc, S5W, T, d, causal, name):
    S = h_all.shape[0]
    _, nch, _, sw2 = waug.shape
    sw = sw2 // 2
    tb = _pick(math.gcd(T, S - T), (256, 128, 64, 32, 16))
    ntile, nt, off = tb // SUB, S // tb, T // tb
    rb = (lambda s, t: ((nt - 1 - t + off) % nt, s)) if causal else (lambda s, t: (t, s))
    adj_causal = not causal
    edge = SUB - 1 if adj_causal else SUB + tb
    keep_src, keep_dst = (tb, 0) if adj_causal else (SUB, SUB + tb)

    def body(dy_ref, u_ref, h_ref, w_ref, v_ref, p_ref, du_ref, dbb_ref, dc_ref, da_ref, lam):
        @pl.when(pl.program_id(1) == 0)
        def _():
            lam[pl.ds(0, SUB), :] = jnp.zeros((SUB, sw2), F32)
            lam[pl.ds(SUB + tb, SUB), :] = jnp.zeros((SUB, sw2), F32)
            dbb_ref[...] = jnp.zeros_like(dbb_ref)
            dc_ref[...] = jnp.zeros_like(dc_ref)
            da_ref[...] = jnp.zeros_like(da_ref)

        dy = dy_ref[...]
        lam[pl.ds(SUB, tb), :] = _tile_local_scan(dy, v_ref, adj_causal)
        _tile_scan(lam, SUB, ntile, p_ref[...], lam[pl.ds(edge, 1), :], sw, adj_causal)
        lb = lam[pl.ds(SUB, tb), :].astype(BF16)
        du_ref[...] = lax.dot_general(lb, w_ref[...], _NT, preferred_element_type=F32)
        dbb_ref[...] += lax.dot_general(u_ref[...].astype(BF16), lb, _TN, preferred_element_type=F32)
        dc_ref[...] += lax.dot_general(h_ref[...], dy.astype(BF16), _TN, preferred_element_type=F32)
        h = h_ref[...].astype(F32)
        ln = lam[pl.ds(SUB + 1 if causal else SUB - 1, tb), :]
        hr, hi, lr, li = h[:, :sw], h[:, sw:], ln[:, :sw], ln[:, sw:]
        da_ref[...] += jnp.concatenate([jnp.sum(hr * lr + hi * li, axis=0, keepdims=True),
                                        jnp.sum(hr * li - hi * lr, axis=0, keepdims=True)], axis=1)
        lam[pl.ds(keep_dst, SUB), :] = lam[pl.ds(keep_src, SUB), :]

    fixed = lambda s, t: (s, 0, 0)
    return pl.pallas_call(
        body, name=name, grid=(nch, nt),
        in_specs=[pl.BlockSpec((tb, LANE), rb), pl.BlockSpec((tb, LANE), rb), pl.BlockSpec((tb, sw2), rb),
                  pl.BlockSpec((None, None, LANE, sw2), lambda s, t: (d, s, 0, 0)),
                  pl.BlockSpec((None, None, SUB * LANE, sw2), lambda s, t: (d, s, 0, 0)),
                  pl.BlockSpec((None, SUB, sw2), fixed)],
        out_specs=[pl.BlockSpec((tb, LANE), rb), pl.BlockSpec((None, LANE, sw2), fixed),
                   pl.BlockSpec((None, sw2, LANE), fixed), pl.BlockSpec((None, 1, sw2), fixed)],
        out_shape=[jax.ShapeDtypeStruct((S, S5W), F32), jax.ShapeDtypeStruct((nch, LANE, sw2), F32),
                   jax.ShapeDtypeStruct((nch, sw2, LANE), F32), jax.ShapeDtypeStruct((nch, 1, sw2), F32)],
        scratch_shapes=[pltpu.VMEM((tb + 2 * SUB, sw2), F32)],
        compiler_params=_cparams(("parallel", "arbitrary")),
    )(dy_all, h_all, hs, waug, vaug, pwc)


def _attn_fwd(qn, qr, kv, kr, H, scale):
    T, S = qn.shape[0], kv.shape[0]
    tq = _pick(T, (256, 128, 64, 32, 16))

    def body(qn_ref, qr_ref, kn_ref, v_ref, kr_ref, o_ref, lse_ref):
        q = jnp.concatenate([qn_ref[...], qr_ref[...]], axis=1)
        k = jnp.concatenate([kn_ref[...], kr_ref[...]], axis=1)
        s = lax.dot_general(q, k, _NT, preferred_element_type=F32) * scale
        m = jnp.max(s, axis=1, keepdims=True)
        p = jnp.exp(s - m)
        l = jnp.sum(p, axis=1, keepdims=True)
        o_ref[...] = jnp.dot((p * (1.0 / l)).astype(BF16), v_ref[...], preferred_element_type=F32).astype(o_ref.dtype)
        lse_ref[0] = m + jnp.log(l)

    q_spec = pl.BlockSpec((tq, LANE), lambda h, i: (i, h))
    return pl.pallas_call(
        body, name="attn_fwd", grid=(H, T // tq),
        in_specs=[q_spec, q_spec, pl.BlockSpec((S, LANE), lambda h, i: (0, h)), pl.BlockSpec((S, LANE), lambda h, i: (0, H + h)),
                  pl.BlockSpec((S, LANE), lambda h, i: (0, 0))],
        out_specs=[q_spec, pl.BlockSpec((1, tq, 1), lambda h, i: (h, i, 0))],
        out_shape=[jax.ShapeDtypeStruct((T, H * LANE), BF16), jax.ShapeDtypeStruct((H, T, 1), F32)],
        compiler_params=_cparams(("parallel", "parallel")),
    )(qn, qr, kv, kv, kr)


def _attn_bwd(qn, qr, kv, kr, do, lse, H, scale):
    T, S = qn.shape[0], kv.shape[0]
    tq = _pick(T, (512, 256, 128, 64, 32, 16))
    nq = T // tq

    def body(qn_ref, qr_ref, kn_ref, v_ref, kr_ref, do_ref, lse_ref, dqn_ref, dqr_ref, dkn_ref, dkr_ref, dv_ref, dk_acc, dv_acc):
        i = pl.program_id(1)
        q = jnp.concatenate([qn_ref[...], qr_ref[...]], axis=1)
        k = jnp.concatenate([kn_ref[...], kr_ref[...]], axis=1)
        v, d_o = v_ref[...], do_ref[...]
        s = lax.dot_general(q, k, _NT, preferred_element_type=F32) * scale
        p = jnp.exp(s - lse_ref[0])
        dv_part = lax.dot_general(p.astype(BF16), d_o, _TN, preferred_element_type=F32)
        dp = lax.dot_general(d_o, v, _NT, preferred_element_type=F32)
        ds = (p * (dp - jnp.sum(p * dp, axis=1, keepdims=True)) * scale).astype(BF16)
        dq = jnp.dot(ds, k, preferred_element_type=F32)
        dqn_ref[...] = dq[:, :LANE].astype(dqn_ref.dtype)
        dqr_ref[...] = dq[:, LANE:].astype(dqr_ref.dtype)
        dk_part = lax.dot_general(ds, q, _TN, preferred_element_type=F32)

        @pl.when(i == 0)
        def _():
            dk_acc[...] = dk_part
            dv_acc[...] = dv_part

        @pl.when(i > 0)
        def _():
            dk_acc[...] += dk_part
            dv_acc[...] += dv_part

        @pl.when(i == nq - 1)
        def _():
            dkn_ref[...] = dk_acc[:, :LANE].astype(dkn_ref.dtype)
            dkr_ref[...] = dk_acc[:, LANE:].astype(dkr_ref.dtype)
            dv_ref[...] = dv_acc[...].astype(dv_ref.dtype)

    q_spec = pl.BlockSpec((tq, LANE), lambda h, i: (i, h))
    k_spec = pl.BlockSpec((S, LANE), lambda h, i: (0, h))
    t_shape, s_shape = jax.ShapeDtypeStruct((T, H * LANE), BF16), jax.ShapeDtypeStruct((S, H * LANE), BF16)
    return pl.pallas_call(
        body, name="attn_bwd", grid=(H, nq),
        in_specs=[q_spec, q_spec, k_spec, pl.BlockSpec((S, LANE), lambda h, i: (0, H + h)),
                  pl.BlockSpec((S, LANE), lambda h, i: (0, 0)), q_spec, pl.BlockSpec((1, tq, 1), lambda h, i: (h, i, 0))],
        out_specs=[q_spec, q_spec, k_spec, k_spec, k_spec], out_shape=[t_shape, t_shape, s_shape, s_shape, s_shape],
        scratch_shapes=[pltpu.VMEM((S, 2 * LANE), F32), pltpu.VMEM((S, LANE), F32)],
        compiler_params=_cparams(("parallel", "arbitrary")),
    )(qn, qr, kv, kv, kr, do, lse)


def _rope_tables(T):
    rows = T // GRID_W
    row = jnp.repeat(jnp.arange(rows, dtype=F32), GRID_W)
    col = jnp.tile(jnp.arange(GRID_W, dtype=F32), rows)
    n_freq = QK_ROPE // 4
    inv = ROPE_BASE ** (-jnp.arange(n_freq, dtype=F32) / n_freq)
    ar, ac = row[:, None] * inv, col[:, None] * inv
    cos = jnp.concatenate([jnp.cos(ar), jnp.cos(ar), jnp.cos(ac), jnp.cos(ac)], axis=1)
    sin = jnp.concatenate([-jnp.sin(ar), jnp.sin(ar), -jnp.sin(ac), jnp.sin(ac)], axis=1)
    pad = lambda t: jnp.pad(t, ((0, 0), (0, LANE - QK_ROPE)))
    return pad(cos), pad(sin)


def _dw(a, dy, w, name):
    return _mm(a, dy, "tn", BF16, name, out_slots=w.shape[0] if w.ndim == 3 else None)


def _local_step(x, ctx, tgt, m_lat, m_ctx, p, W, goff, hooks=None):
    T, D = x.shape
    Tc = ctx.shape[0]
    S = T + Tc
    S5W = p["s5_d"].shape[1]
    QR, KVR = p["q_norm"].shape[1], p["kv_norm"].shape[1]
    G, N = p["s5_a_re"].shape[1:]
    P = S5_GROUP
    nch = G // 8
    o_cq, o_ckv, o_kr = S5W, S5W + QR, S5W + QR + KVR
    assert o_cq % QR == 0 and o_ckv % KVR == 0 and o_kr % LANE == 0 and goff % D == 0 and S5W % LANE == 0 and G % 8 == 0
    assert 8 * P == LANE
    row = lambda k, m: m[k:k + 1]
    sh1, sc1, g1, sh2, sc2, g2 = (row(k, m_lat) for k in range(6))
    csh1, csc1 = row(0, m_ctx), row(1, m_ctx)
    n1, n2, nf = p["norm1"], p["norm2"], p["norm_f"]

    (xm_lat,) = _rowmap(_normmod, "norm1_lat", T, [x], [n1, sc1, sh1], [(D, BF16)])
    (xm_ctx,) = _rowmap(_normmod, "norm1_ctx", Tc, [ctx], [n1, csc1, csh1], [(D, BF16)])
    xm_all = jnp.concatenate([xm_lat, xm_ctx], axis=0)

    a_re, a_im = p["s5_a_re"][:, :, None, :], p["s5_a_im"][:, :, None, :]
    ldt = p["s5_log_dt"][:, :, None, None]
    b_re, b_im = p["s5_b_re"].transpose(0, 1, 3, 2), p["s5_b_im"].transpose(0, 1, 3, 2)
    wre, wim, vre, vim, pwr, pwi = _s5_tables(a_re, a_im, ldt, b_re, b_im, p["s5_c_re"], p["s5_c_im"])
    waug = _s5_expand(wre, wim, "s5_expand_b")
    vaug = _s5_expand(vre, vim, "s5_expand_c")
    lanes = lambda t: t.reshape(2, SUB + 1, nch, 8 * N).transpose(0, 2, 1, 3)
    pw_re, pw_im = lanes(pwr), lanes(pwi)
    near = lambda t: t[:, :, 1:]
    far = lambda t: t[:, :, :0:-1]
    pw_c = jnp.concatenate([near(pw_re), near(pw_im)], axis=-1)
    pw_a = jnp.concatenate([far(pw_re), far(pw_im)], axis=-1)
    pwc_c = jnp.concatenate([near(pw_re), -near(pw_im)], axis=-1)
    pwc_a = jnp.concatenate([far(pw_re), -far(pw_im)], axis=-1)

    if hooks:
        W = {**W, **hooks["first_weights"](xm_all, vaug)}
    H = W["w_uq"].shape[1] // (2 * LANE)
    h_all = _mm(xm_all, W["w_in"], "nn", F32, "mm_in")
    y0, hs0 = _s5_fwd(h_all, waug, vaug, pw_c[0], S5W, T, 0, True, "s5_scan_fwd0")
    y1, hs1 = _s5_fwd(h_all, waug, vaug, pw_a[1], S5W, T, 1, False, "s5_scan_fwd1")

    def s5_combine(u, yf, yr, dskip):
        y5 = dskip * u + yf + yr
        return y5, jax.nn.gelu(y5)

    y5, z = _rowmap(s5_combine, "s5_combine", T, [(h_all, S5W, 0), y0, y1], [p["s5_d"]], [(S5W, F32), (S5W, BF16)])

    (qn,) = _rowmap(_rms, "q_norm", T, [(h_all, QR, o_cq // QR)], [p["q_norm"]], [(QR, BF16)])
    (kvn,) = _rowmap(_rms, "kv_norm", S, [(h_all, KVR, o_ckv // KVR)], [p["kv_norm"]], [(KVR, BF16)])
    qraw = _mm(qn, W["w_uq"], "nn", F32, "mm_uq")
    kvraw = _mm(kvn, W["w_ukv"], "nn", BF16, "mm_ukv")
    cos_q, sin_q = _rope_tables(T)
    padl = lambda t: jnp.pad(t[:, :LANE], ((0, Tc), (0, 0)))
    cos_k = padl(cos_q) + jnp.pad(jnp.ones((Tc, LANE), F32), ((T, 0), (0, 0)))
    sin_k = padl(sin_q)
    hn = H * LANE

    def q_post(q, cos, sin):
        return q[:, :hn], _rope(q[:, hn:], jnp.tile(cos, (1, H)), jnp.tile(sin, (1, H)))

    q_nope, q_rope = _rowmap(q_post, "q_rope", T, [qraw, cos_q, sin_q], [], [(hn, BF16), (hn, BF16)])
    (kr,) = _rowmap(_rope, "k_rope", S, [(h_all, LANE, o_kr // LANE), cos_k, sin_k], [], [(LANE, BF16)])
    scale = (QK_NOPE + QK_ROPE) ** -0.5
    o, lse = _attn_fwd(q_nope, q_rope, kvraw, kr, H, scale)
    g1_fwd = g1
    if hooks:
        W = {**W, **hooks["mix_weights"](o)}
        g1_fwd = g1 + hooks["ffn_mid"](o)[:1, :1]

    zz = _mm(z, W["w_glu"], "nn", BF16, "mm_glu")
    br_mla = _mm(o, W["w_mla_o"], "nn", BF16, "mm_mla_o")

    def merge(zz, brm, gs, gm):
        a, b = zz[:, :D], zz[:, D:]
        return jax.nn.sigmoid(gs) * (a * jax.nn.sigmoid(b)) + jax.nn.sigmoid(gm) * brm

    gb = goff // D
    merge_ins = [zz, br_mla, (h_all, D, gb), (h_all, D, gb + 1)]
    (mix,) = _rowmap(merge, "merge", T, merge_ins, [], [(D, BF16)])
    out1 = _mm(mix, W["w_out"], "nn", F32, "mm_out")

    def resid_norm2(x, out1, g1, n2, sc2, sh2):
        x1 = x + g1 * out1
        return x1, _normmod(x1, n2, sc2, sh2)

    x1, hm = _rowmap(resid_norm2, "resid_norm2", T, [x, out1], [g1_fwd, n2, sc2, sh2], [(D, F32), (D, BF16)])

    if hooks:
        W = {**W, **hooks["ffn_weights"](hm)}
    FF = W["w_ffn_out"].shape[0]
    assert FF % LANE == 0
    def swiglu_act(a, b):
        return jax.nn.silu(a) * b

    if W["w_ffn_in"].ndim == 3:
        ffn_a, ffn_b, f = _mm_swiglu(hm, W["w_ffn_in"], swiglu_act, "mm_ffn_in")
        ffn_a, ffn_b = (ffn_a, FF, 0), (ffn_b, FF, 0)
    else:
        ab = _mm(hm, W["w_ffn_in"], "nn", BF16, "mm_ffn_in")
        ffn_a, ffn_b = (ab, FF, 0), (ab, FF, 1)
        (f,) = _rowmap(swiglu_act, "ffn_act", T, [ffn_a, ffn_b], [], [(FF, BF16)])
    out2 = _mm(f, W["w_ffn_out"], "nn", F32, "mm_ffn_out")

    def loss_rows(x1, out2, g2, nf, tgt):
        y = _rms(x1 + g2 * out2, nf)
        return 0.5 * jnp.sum(jnp.mean(jnp.square(y - tgt), axis=-1))

    def final(x1, out2, tgt, g2, nf):
        val, (dx1, dout2, dg2, dnf) = jax.value_and_grad(loss_rows, argnums=(0, 1, 2, 3))(x1, out2, g2, nf, tgt)
        return dx1, dout2, jnp.full((1, LANE), val, F32), dg2, dnf

    dx2, dout2, loss_acc, dg2, dnf = _rowmap(final, "final_loss", T, [x1, out2, tgt], [g2, nf],
                                             [(D, F32), (D, BF16)], [LANE, D, D])

    gW = {}
    df = _mm(dout2, W["w_ffn_out"], "nt", BF16, "mm_ffn_out_dx")
    gW["w_ffn_out"] = _dw(f, dout2, W["w_ffn_out"], "mm_ffn_out_dw")

    def swiglu_bwd(a, b, df):
        _, vjp = jax.vjp(swiglu_act, a, b)
        da, db = vjp(df)
        return jnp.concatenate([da, db], axis=1)

    (dab,) = _rowmap(swiglu_bwd, "ffn_act_bwd", T, [ffn_a, ffn_b, df], [], [(2 * FF, BF16)])
    dhm = _mm(dab, W["w_ffn_in"], "nt", F32, "mm_ffn_in_dx")
    gW["w_ffn_in"] = _dw(hm, dab, W["w_ffn_in"], "mm_ffn_in_dw")
    if hooks:
        token = hooks["send_grads"](FFN, [gW.pop(n) for n in FFN])
        g1 = g1 if token is None else g1 + token[:1, :1]

    def resid_norm2_bwd(x, out1, dx2, dhm, g1, n2, sc2, sh2):
        _, vjp = jax.vjp(resid_norm2, x, out1, g1, n2, sc2, sh2)
        dx, dout1, dg1, dn2, dsc2, dsh2 = vjp((dx2, dhm))
        return dx, dout1, dg1, dn2, dsc2, dsh2

    dx1, dout1, dg1, dn2, dsc2, dsh2 = _rowmap(resid_norm2_bwd, "resid_norm2_bwd", T, [x, out1, dx2, dhm],
                                               [g1, n2, sc2, sh2], [(D, F32), (D, BF16)], [D, D, D, D])

    dmix = _mm(dout1, W["w_out"], "nt", BF16, "mm_out_dx")
    gW["w_out"] = _dw(mix, dout1, W["w_out"], "mm_out_dw")

    def merge_bwd(zz, brm, gs, gm, dmix):
        _, vjp = jax.vjp(merge, zz, brm, gs, gm)
        dzz, dbrm, dgs, dgm = vjp(dmix)
        return dzz, dbrm, jnp.concatenate([dgs, dgm], axis=1)

    dzz, dbrm, dgates = _rowmap(merge_bwd, "merge_bwd", T, merge_ins + [dmix], [],
                                [(2 * D, BF16), (D, BF16), (2 * D, BF16)])
    do = _mm(dbrm, W["w_mla_o"], "nt", BF16, "mm_mla_o_dx")
    gW["w_mla_o"] = _dw(o, dbrm, W["w_mla_o"], "mm_mla_o_dw")
    dz = _mm(dzz, W["w_glu"], "nt", BF16, "mm_glu_dx")
    gW["w_glu"] = _dw(z, dzz, W["w_glu"], "mm_glu_dw")
    d_skip_w = p["s5_d"]
    if hooks:
        token = hooks["send_grads"](MIX, [gW.pop(n) for n in MIX])
        d_skip_w = d_skip_w if token is None else d_skip_w + token[:1, :1]

    def s5_combine_bwd(u, y5, dz, dskip):
        _, vjp = jax.vjp(lambda y: jax.nn.gelu(y), y5)
        (dy5,) = vjp(dz)
        return dy5, jnp.sum(dy5 * u, axis=0, keepdims=True)

    dy5, d_skip = _rowmap(s5_combine_bwd, "s5_combine_bwd", T, [(h_all, S5W, 0), y5, dz], [d_skip_w], [(S5W, F32)], [S5W])

    dq_nope, dq_rope, dk_nope, dkr_heads, dv = _attn_bwd(q_nope, q_rope, kvraw, kr, do, lse, H, scale)

    def q_post_bwd(dqn, dqr, cos, sin):
        return jnp.concatenate([dqn, _rope_bwd(dqr, jnp.tile(cos, (1, H)), jnp.tile(sin, (1, H)))], axis=1)

    (dqraw,) = _rowmap(q_post_bwd, "q_rope_bwd", T, [dq_nope, dq_rope, cos_q, sin_q], [], [(2 * hn, BF16)])
    dkvraw = jnp.concatenate([dk_nope, dv], axis=1)

    def k_rope_bwd(dkh, cos, sin):
        d = dkh[:, :LANE]
        for h in range(1, H):
            d = d + dkh[:, h * LANE:(h + 1) * LANE]
        return _rope_bwd(d, cos, sin)

    (dkr,) = _rowmap(k_rope_bwd, "k_rope_bwd", S, [dkr_heads, cos_k, sin_k], [], [(LANE, BF16)])
    dqn = _mm(dqraw, W["w_uq"], "nt", F32, "mm_uq_dx")
    gW["w_uq"] = _dw(qn, dqraw, W["w_uq"], "mm_uq_dw")
    dkvn = _mm(dkvraw, W["w_ukv"], "nt", F32, "mm_ukv_dx")
    gW["w_ukv"] = _dw(kvn, dkvraw, W["w_ukv"], "mm_ukv_dw")

    def rms_bwd(cx, dn, g):
        _, vjp = jax.vjp(_rms, cx, g)
        return vjp(dn)

    dcq, dq_norm = _rowmap(rms_bwd, "q_norm_bwd", T, [(h_all, QR, o_cq // QR), dqn], [p["q_norm"]], [(QR, BF16)], [QR])
    dckv, dkv_norm = _rowmap(rms_bwd, "kv_norm_bwd", S, [(h_all, KVR, o_ckv // KVR), dkvn], [p["kv_norm"]],
                             [(KVR, BF16)], [KVR])

    dy_all = jnp.concatenate([dy5, jnp.zeros((Tc, S5W), F32)], axis=0)
    du0, dbb0, dc0, da0 = _s5_bwd(dy_all, h_all, hs0, waug, vaug, pwc_a[0], S5W, T, 0, True, "s5_scan_bwd0")
    du1, dbb1, dc1, da1 = _s5_bwd(dy_all, h_all, hs1, waug, vaug, pwc_c[1], S5W, T, 1, False, "s5_scan_bwd1")

    def du_combine(a, b, dy, dskip):
        return a + b + dskip * dy

    (du_all,) = _rowmap(du_combine, "s5_du", S, [du0, du1, dy_all], [p["s5_d"]], [(S5W, BF16)])
    dbb = jnp.einsum("dsgpcgn->dcsgpn", jnp.stack([dbb0, dbb1]).reshape(2, nch, 8, P, 2, 8, N)).reshape(2, 2, G, P, N)
    dcm = jnp.einsum("dscgngp->dcsgpn", jnp.stack([dc0, dc1]).reshape(2, nch, 2, 8, N, 8, P)).reshape(2, 2, G, P, N)
    da = jnp.stack([da0, da1]).reshape(2, nch, 2, 8, N).transpose(0, 2, 1, 3, 4).reshape(2, 2, G, 1, N)
    d_lr, d_li, d_ldt, d_br, d_bi = _s5_param_bwd(a_re, a_im, ldt, b_re, b_im, da[:, 0], da[:, 1], dbb[:, 0], dbb[:, 1])

    lat_only = lambda t: jnp.pad(t, ((0, Tc), (0, 0)))
    dh_all = jnp.concatenate([du_all, lat_only(dcq), dckv, dkr, jnp.zeros((S, goff - o_kr - LANE), BF16), lat_only(dgates)],
                             axis=1)
    dxm = _mm(dh_all, W["w_in"], "nt", F32, "mm_in_dx")
    gW["w_in"] = _dw(xm_all, dh_all, W["w_in"], "mm_in_dw")

    def norm1_bwd(x, dxm, dx1, n1, sc, sh):
        _, vjp = jax.vjp(_normmod, x, n1, sc, sh)
        dx, dn, dsc, dsh = vjp(dxm)
        return dx + dx1, dn, dsc, dsh

    grad_x, dn1_l, dsc1, dsh1 = _rowmap(norm1_bwd, "norm1_lat_bwd", T, [x, dxm, dx1], [n1, sc1, sh1], [(D, F32)], [D, D, D])

    def norm1_ctx_bwd(x, dxm, n1, sc, sh):
        _, vjp = jax.vjp(_normmod, x, n1, sc, sh)
        return vjp(dxm)[1:]

    dn1_c, dcsc1, dcsh1 = _rowmap(norm1_ctx_bwd, "norm1_ctx_bwd", Tc, [ctx, dxm[T:]], [n1, csc1, csh1], [], [D, D, D])

    zero = jnp.zeros((1, D), F32)
    dm_lat = jnp.concatenate([dsh1, dsc1, dg1, dsh2, dsc2, dg2], axis=0)
    dm_ctx = jnp.concatenate([dcsh1, dcsc1, zero, zero, zero, zero], axis=0)
    small = {
        "norm1": dn1_l + dn1_c, "norm2": dn2, "norm_f": dnf, "q_norm": dq_norm, "kv_norm": dkv_norm, "s5_d": d_skip,
        "s5_a_re": d_lr, "s5_a_im": d_li, "s5_log_dt": d_ldt, "s5_b_re": d_br.transpose(0, 1, 3, 2),
        "s5_b_im": d_bi.transpose(0, 1, 3, 2), "s5_c_re": dcm[:, 0], "s5_c_im": -dcm[:, 1],
    }
    return loss_acc[:, :1], grad_x, small, dm_lat, dm_ctx, gW


BIG = ("w_in", "w_uq", "w_ukv", "w_glu", "w_mla_o", "w_out", "w_ffn_in", "w_ffn_out")
FFN = ("w_ffn_in", "w_ffn_out")
MIX = ("w_out", "w_mla_o", "w_glu")
ROW_SHARDED = ("w_out", "w_ffn_out")
RELAID = ("w_in", "w_uq", "w_ukv")
SMALL = ("c_ctx", "b_mod", "norm1", "norm2", "s5_a_re", "s5_a_im", "s5_log_dt", "s5_b_re", "s5_b_im", "s5_c_re",
         "s5_c_im", "s5_d", "q_norm", "kv_norm", "norm_f")
S5_BULK = ("s5_b_re", "s5_b_im", "s5_c_re", "s5_c_im")
WEIGHTS = ("c_ctx", "w_mod", "b_mod", "norm1", "norm2", "w_in", "s5_a_re", "s5_a_im", "s5_log_dt", "s5_b_re", "s5_b_im",
           "s5_c_re", "s5_c_im", "s5_d", "w_glu", "q_norm", "kv_norm", "w_uq", "w_ukv", "w_mla_o", "w_out", "w_ffn_in",
           "w_ffn_out", "norm_f")


def _heads_split(w, heads, first):
    k = w.shape[0]
    w3 = w.reshape(k, heads, -1)
    return jnp.concatenate([w3[:, :, :first].reshape(k, -1), w3[:, :, first:].reshape(k, -1)], axis=1)


def _uq_layout(w, heads):
    k = w.shape[0]
    w3 = w.reshape(k, heads, QK_NOPE + QK_ROPE)
    rope = jnp.pad(w3[:, :, QK_NOPE:], ((0, 0), (0, 0), (0, LANE - QK_ROPE)))
    return jnp.concatenate([w3[:, :, :QK_NOPE].reshape(k, -1), rope.reshape(k, -1)], axis=1)


def _uq_unlayout(w, heads):
    k = w.shape[0]
    nope = w[:, :heads * QK_NOPE].reshape(k, heads, QK_NOPE)
    rope = w[:, heads * QK_NOPE:].reshape(k, heads, LANE)[:, :, :QK_ROPE]
    return jnp.concatenate([nope, rope], axis=2).reshape(k, -1)


def _heads_merge(w, heads, first):
    k = w.shape[0]
    a, b = w[:, :heads * first].reshape(k, heads, first), w[:, heads * first:].reshape(k, heads, -1)
    return jnp.concatenate([a, b], axis=2).reshape(k, -1)


def _cols_full(w8):
    return w8.transpose(1, 0, 2).reshape(w8.shape[1], -1)


def _cols_slots(w):
    return w.reshape(w.shape[0], N_DEV, -1).transpose(1, 0, 2)


def _weight_layout(n, w8):
    if n in ROW_SHARDED:
        return w8.reshape(-1, w8.shape[-1])
    return _cols_full(w8) if (n in RELAID or w8.shape[-1] % LANE) else w8


def _grad_slots(n, g):
    if g.ndim == 3:
        return g
    return g.reshape(N_DEV, g.shape[0] // N_DEV, g.shape[1]) if n in ROW_SHARDED else _cols_slots(g)


def _gate_offset(in_cols, D):
    return -(-(in_cols - 2 * D) // D) * D


def _model_weights(g8, D):
    W = {n: _weight_layout(n, w8) for n, w8 in g8.items()}
    w_in = W["w_in"]
    n_front = w_in.shape[1] - 2 * D
    goff = _gate_offset(w_in.shape[1], D)
    W["w_in"] = jnp.concatenate([w_in[:, :n_front], jnp.zeros((D, goff - n_front), w_in.dtype), w_in[:, n_front:]], axis=1)
    heads = W["w_uq"].shape[1] // (QK_NOPE + QK_ROPE)
    W["w_uq"] = _uq_layout(W["w_uq"], heads)
    W["w_ukv"] = _heads_split(W["w_ukv"], heads, QK_NOPE)
    return W, goff


def kernel(x, c, ctx, c_ctx, w_mod, b_mod, norm1, norm2, w_in, s5_a_re, s5_a_im, s5_log_dt, s5_b_re, s5_b_im, s5_c_re, s5_c_im, s5_d, w_glu, q_norm, kv_norm, w_uq, w_ukv, w_mla_o, w_out, w_ffn_in, w_ffn_out, norm_f, loss_target, m_c_ctx, m_w_mod, m_b_mod, m_norm1, m_norm2, m_w_in, m_s5_a_re, m_s5_a_im, m_s5_log_dt, m_s5_b_re, m_s5_b_im, m_s5_c_re, m_s5_c_im, m_s5_d, m_w_glu, m_q_norm, m_kv_norm, m_w_uq, m_w_ukv, m_w_mla_o, m_w_out, m_w_ffn_in, m_w_ffn_out, m_norm_f, v_c_ctx, v_w_mod, v_b_mod, v_norm1, v_norm2, v_w_in, v_s5_a_re, v_s5_a_im, v_s5_log_dt, v_s5_b_re, v_s5_b_im, v_s5_c_re, v_s5_c_im, v_s5_d, v_w_glu, v_q_norm, v_kv_norm, v_w_uq, v_w_ukv, v_w_mla_o, v_w_out, v_w_ffn_in, v_w_ffn_out, v_norm_f):
    a = dict(locals())
    D = x.shape[-1]
    me = 4 * lax.axis_index("x") + 2 * lax.axis_index("y") + lax.axis_index("c")

    shard = {n: a[n][0] for n in BIG}
    first = [n for n in BIG if n not in FFN + MIX]
    (cg,) = _all_gather([jnp.broadcast_to(c, (8, D))], "ag_c")
    goff = _gate_offset(w_in.shape[-1] * N_DEV, D)

    wm = w_mod[0]
    ncol = wm.shape[1]
    c16 = jnp.concatenate([cg[:, 0, :], c_ctx[None], jnp.zeros((7, D), F32)], axis=0)
    (s16,) = _rowmap(jax.nn.silu, "mod_silu", 16, [c16], [], [(D, BF16)])
    m_cols = _mm(s16, wm, "nn", F32, "mm_mod")
    (mg,) = _all_gather([m_cols], "ag_mod")
    (m16,) = _rowmap(lambda m, b: m + b, "mod_bias", 16, [_cols_full(mg)], [b_mod], [(N_DEV * ncol, F32)])

    first_blocks = [shard[n].astype(BF16) for n in first]
    fst = {}
    fst["sems1"], fst["thru"], first_token = _ag2_start(first_blocks, [_own_slot(b, me) for b in first_blocks], "ag_first_start")

    def first_weights(after_norm, after_tables):
        sems2, thru, _ = _ag2_mid(fst["sems1"], fst["thru"], after_tables, "ag_first_mid")
        lands = _ag2_end(fst["sems1"], sems2, thru, after_norm, "ag_first_end")
        return _model_weights(dict(zip(first, lands)), D)[0]

    mix_blocks = [shard[n].astype(BF16) for n in MIX]
    mix = _xchg_start(mix_blocks, [_own_slot(b, me) for b in mix_blocks], False, "ag_mix_start")
    ffn_blocks = [shard[n].astype(BF16) for n in FFN]
    ffn = {}
    ffn["sems1"], ffn["thru"], ag_token = _ag2_start(ffn_blocks, [_own_slot(b, me) for b in ffn_blocks], "ag_ffn_start")
    m16 = m16 + (first_token[:1, :1] + mix[3][:1, :1] + ag_token[:1, :1])

    def mix_weights(after):
        lands = _xchg_wait(mix[0], mix[1], mix[2], after, False, "ag_mix_wait")
        return {n: _weight_layout(n, w8) for n, w8 in zip(MIX, lands)}

    def ffn_mid(after):
        ffn["sems2"], ffn["thru"], token = _ag2_mid(ffn["sems1"], ffn["thru"], after, "ag_ffn_mid")
        return token

    def ffn_weights(after):
        lands = _ag2_end(ffn["sems1"], ffn["sems2"], ffn["thru"], after, "ag_ffn_end")
        return {n: _weight_layout(n, w8) for n, w8 in zip(FFN, lands)}

    rs_async = {}

    def send_grads(names, gs):
        slots = [_grad_slots(n, g) for n, g in zip(names, gs)]
        lands = [_own_slot(lax.dynamic_index_in_dim(s, me, 0, keepdims=False), me) for s in slots]
        rs_async[names] = _xchg_start(slots, lands, True, "rs_start_" + names[0])
        return rs_async[names][3]

    m_lat = lax.dynamic_slice(m16, (me, 0), (1, 6 * D)).reshape(6, D)
    m_ctx = m16[8].reshape(6, D)

    p = {n: a[n][0] for n in ("norm1", "norm2", "s5_a_re", "s5_a_im", "s5_log_dt", "s5_b_re", "s5_b_im", "s5_c_re",
                              "s5_c_im", "q_norm", "kv_norm")}
    p = {k: (v[None] if v.ndim == 1 else v) for k, v in p.items()}
    p["s5_d"] = s5_d.reshape(1, -1)
    p["norm_f"] = norm_f[None]
    hooks = dict(first_weights=first_weights, mix_weights=mix_weights, ffn_mid=ffn_mid, ffn_weights=ffn_weights,
                 send_grads=send_grads)
    loss_part, grad_x, small, dm_lat, dm_ctx, gW = _local_step(x[0], ctx[0], loss_target[0], m_lat, m_ctx, p, {}, goff, hooks)
    loss = lax.psum(loss_part[0, 0], ("x", "y", "c"))

    gW = dict(gW)
    n_front = w_in.shape[-1] * N_DEV - 2 * D
    gW["w_in"] = jnp.concatenate([gW["w_in"][:, :n_front], gW["w_in"][:, goff:]], axis=1)
    heads = gW["w_uq"].shape[1] // (2 * LANE)
    gW["w_uq"] = _uq_unlayout(gW["w_uq"], heads)
    gW["w_ukv"] = _heads_merge(gW["w_ukv"], heads, QK_NOPE)
    last = [n for n in BIG if n in gW]
    slots = [_grad_slots(n, gW[n]) for n in last]
    from_sibling = _rs_pair(slots, "rs_pair")
    chip_sums = [_add_pair(pp, rr, "rs_add_" + n, grad_x) for n, pp, rr in zip(last, slots, from_sibling)]

    dm8 = jnp.concatenate([dm_lat.reshape(1, -1), dm_ctx.reshape(1, -1), jnp.zeros((SUB - 2, 6 * D), F32)], axis=0)
    (dmg,) = _all_gather([dm8], "ag_dmod", after=chip_sums[0])
    dm_sum = _sum_slots(dmg, "sum_dmod")
    dM16 = jnp.concatenate([dmg[:, 0, :], dm_sum[1:2], jnp.zeros((7, 6 * D), F32)], axis=0)
    (g_b_mod,) = _rowmap(lambda d: jnp.sum(d, axis=0, keepdims=True), "b_mod_grad", 16, [dM16], [], [], [6 * D])
    dM_loc = lax.dynamic_slice(dM16, (0, me * ncol), (16, ncol))
    g_w_mod = _mm(s16, dM_loc, "tn", F32, "mm_mod_dw")
    ds16_part = _mm(dM_loc, wm, "nt", F32, "mm_mod_dx")

    fine = [n for n in SMALL if n not in ("c_ctx", "b_mod") + S5_BULK]
    small_blocks = [_pack_rows([small[n] for n in fine] + [ds16_part[8:9]], F32), _pack_rows([small[n] for n in S5_BULK], BF16)]
    sm_sems1, sm_thru, sm_token = _ag2_start(small_blocks, [_own_slot(b, me) for b in small_blocks], "ag_small_start")
    grads = {"b_mod": g_b_mod}

    def small_grads(after):
        sems2, thru, token = _ag2_mid(sm_sems1, sm_thru, after, "ag_small_mid")
        sg, sgb = _ag2_end(sm_sems1, sems2, thru, token, "ag_small_end")
        parts = _unpack_rows(_sum_slots(sg, "sum_small"), [small[n].shape for n in fine] + [(1, D)])
        grads.update(zip(fine, parts[:-1]))
        grads.update(zip(S5_BULK, _unpack_rows(_sum_slots(sgb, "sum_small_bulk"), [small[n].shape for n in S5_BULK])))

        def silu_bwd(cc, ds):
            _, vjp = jax.vjp(jax.nn.silu, cc)
            return vjp(ds)[0]

        (grads["c_ctx"],) = _rowmap(silu_bwd, "c_ctx_grad", 1, [c_ctx[None], parts[-1]], [], [(D, F32)])


    my_chip = 2 * lax.axis_index("x") + lax.axis_index("y")
    lands = [_own_slot(lax.dynamic_index_in_dim(q, my_chip, 0, keepdims=False) + sm_token[:1, :1].astype(q.dtype), my_chip, N_CHIP)
             for q in chip_sums]
    rs_send, rs_recv, rs_thru, behind = _xchg_start(chip_sums, lands, True, "rs_chips_start")
    partials = {}
    for names, (send, recv, thru, _) in rs_async.items():
        partials.update(zip(names, _xchg_wait(send, recv, thru, behind, True, "rs_wait_" + names[0])))

    out = {}

    def adamw_big(n, after):
        w2, m2, v2 = a[n][0], a["m_" + n][0], a["v_" + n][0]
        if n in partials:
            g, d, nm, nv = _adamw_slots(w2, partials[n], m2, v2, "adamw_" + n, after)
        else:
            g = g_w_mod
            d, nm, nv = _adamw(w2, g, m2, v2, "adamw_" + n, after)
        for k, val in (("grad_", g), ("delta_", d), ("new_m_", nm), ("new_v_", nv)):
            out[k + n] = val.reshape(a[n].shape)
        return nv

    for n in FFN + MIX + ("w_mod",):
        behind = adamw_big(n, behind)
    small_grads(behind)
    packs = [_pack_rows([t[n] for n in SMALL], F32) for t in (
        {n: a[n] for n in SMALL}, {n: grads[n] for n in SMALL}, {n: a["m_" + n] for n in SMALL}, {n: a["v_" + n] for n in SMALL})]
    res = _adamw(*packs, "adamw_small")
    partials.update(zip(last, _xchg_wait(rs_send, rs_recv, rs_thru, res[2], True, "rs_chips_wait")))
    for n in last:
        adamw_big(n, None)
    shapes = [a[n].shape for n in SMALL]
    for k, packed in (("grad_", packs[1]), ("delta_", res[0]), ("new_m_", res[1]), ("new_v_", res[2])):
        for n, val in zip(SMALL, _unpack_rows(packed, shapes)):
            out[k + n] = val
    return (loss, grad_x[None]) + tuple(out[k + n] for k in ("grad_", "delta_", "new_m_", "new_v_") for n in WEIGHTS)
```

```python
import functools
import math

import jax
import jax.numpy as jnp
from jax import lax
from jax.experimental import pallas as pl
from jax.experimental.pallas import tpu as pltpu

F32 = jnp.float32
BF16 = jnp.bfloat16

N_DEV = 8
N_CHIP = 4
EPS = 1e-6
GRID_W = 64
S5_GROUP = 16
QK_NOPE, QK_ROPE, V_DIM = 128, 64, 128
ROPE_BASE = 10000.0
ADAM_LR, ADAM_B1, ADAM_B2, ADAM_EPS, ADAM_WD, ADAM_STEP = 0.001, 0.9, 0.999, 1e-08, 0.01, 10

LANE = 128
SUB = 8
PACK_W = 1024
PACK_ROWS = 32
VMEM_LIMIT = 48 << 20
ROWMAP_TILE_BYTES = 20 << 20
MM_VMEM_BUDGET = 36 << 20
MESH = pl.DeviceIdType.MESH
_NT = (((1,), (1,)), ((), ()))
_TN = (((0,), (0,)), ((), ()))


def _pick(dim, cands):
    for c in cands:
        if dim % c == 0:
            return c
    return dim


def _cparams(sem):
    return pltpu.CompilerParams(dimension_semantics=sem, vmem_limit_bytes=VMEM_LIMIT)


def _mm(a, b, dims, out_dtype, name, out_slots=None):
    a = a.astype(BF16)
    b = b.astype(BF16)
    b3 = b.ndim == 3
    if dims == "nn":
        (M, K), N = a.shape, (b.shape[0] * b.shape[2] if b3 else b.shape[1])
    elif dims == "nt":
        M, N = a.shape[0], b.shape[-2]
        K = b.shape[0] * b.shape[2] if b3 else b.shape[1]
    else:
        (K, M), N = a.shape, b.shape[1]
    unit_n = b.shape[2] if (b3 and dims == "nn") else (N // out_slots if out_slots else N)
    unit_k = b.shape[2] if (b3 and dims == "nt") else K
    osz = jnp.dtype(out_dtype).itemsize
    tm, tn, tk = _mm_tiles(M, unit_n, unit_k, osz, LANE if dims == "tn" else 16)
    nk, npt, kpt = K // tk, unit_n // tn, unit_k // tk
    use_acc = nk > 1 and out_dtype != F32
    if dims == "nn":
        a_spec = pl.BlockSpec((tm, tk), lambda i, j, k: (i, k))
        b_spec = (pl.BlockSpec((None, tk, tn), lambda i, j, k: (j // npt, k, j % npt)) if b3
                  else pl.BlockSpec((tk, tn), lambda i, j, k: (k, j)))
        dn = (((1,), (0,)), ((), ()))
    elif dims == "nt":
        a_spec = pl.BlockSpec((tm, tk), lambda i, j, k: (i, k))
        b_spec = (pl.BlockSpec((None, tn, tk), lambda i, j, k: (k // kpt, j, k % kpt)) if b3
                  else pl.BlockSpec((tn, tk), lambda i, j, k: (j, k)))
        dn = _NT
    else:
        a_spec = pl.BlockSpec((tk, tm), lambda i, j, k: (k, i))
        b_spec = pl.BlockSpec((tk, tn), lambda i, j, k: (k, j))
        dn = _TN
    if out_slots:
        out_spec = pl.BlockSpec((None, tm, tn), lambda i, j, k: (j // npt, i, j % npt))
        out_shape = jax.ShapeDtypeStruct((out_slots, M, unit_n), out_dtype)
    else:
        out_spec = pl.BlockSpec((tm, tn), lambda i, j, k: (i, j))
        out_shape = jax.ShapeDtypeStruct((M, N), out_dtype)

    def body(a_ref, b_ref, o_ref, *scratch):
        part = lax.dot_general(a_ref[...], b_ref[...], dn, preferred_element_type=F32)
        if nk == 1:
            o_ref[...] = part.astype(o_ref.dtype)
            return
        acc_ref = scratch[0] if use_acc else o_ref
        k = pl.program_id(2)

        @pl.when(k == 0)
        def _():
            acc_ref[...] = part

        @pl.when(k > 0)
        def _():
            acc_ref[...] += part

        if use_acc:
            @pl.when(k == nk - 1)
            def _():
                o_ref[...] = acc_ref[...].astype(o_ref.dtype)

    return pl.pallas_call(
        body, name=name, grid=(M // tm, N // tn, nk),
        in_specs=[a_spec, b_spec], out_specs=out_spec, out_shape=out_shape,
        scratch_shapes=[pltpu.VMEM((tm, tn), F32)] if use_acc else [],
        compiler_params=_cparams(("parallel", "parallel", "arbitrary")),
    )(a, b)


def _mm_swiglu(x, w3, act, name):
    x = x.astype(BF16)
    M, K = x.shape
    ns, _, n = w3.shape
    half = ns // 2
    tm = _divisors(M, 16, 256)[0]

    def body(x_ref, wa_ref, wb_ref, a_ref, b_ref, f_ref):
        a = jnp.dot(x_ref[...], wa_ref[...], preferred_element_type=F32)
        b = jnp.dot(x_ref[...], wb_ref[...], preferred_element_type=F32)
        a_ref[...] = a.astype(a_ref.dtype)
        b_ref[...] = b.astype(b_ref.dtype)
        f_ref[...] = act(a, b).astype(f_ref.dtype)

    out = pl.BlockSpec((tm, n), lambda s, i: (i, s))
    return pl.pallas_call(
        body, name=name, grid=(half, M // tm),
        in_specs=[pl.BlockSpec((tm, K), lambda s, i: (i, 0)), pl.BlockSpec((None, K, n), lambda s, i: (s, 0, 0)),
                  pl.BlockSpec((None, K, n), lambda s, i: (s + half, 0, 0))],
        out_specs=[out] * 3, out_shape=[jax.ShapeDtypeStruct((M, half * n), BF16)] * 3,
        compiler_params=_cparams(("parallel", "parallel")),
    )(x, w3, w3)


def _divisors(n, mult, cap):
    d = [t for t in range(mult, min(n, cap) + 1, mult) if n % t == 0]
    return d[::-1] or [n]


def _mm_tiles(M, unit_n, unit_k, out_itemsize, tm_mult):
    best = None
    for tk in _divisors(unit_k, LANE, 2816):
        for tn in _divisors(unit_n, LANE, 1536):
            for tm in _divisors(M, tm_mult, 1024):
                vmem = 2 * 2 * (tm * tk + tk * tn) + 2 * tm * tn * out_itemsize + 4 * tm * tn * (2 if unit_k > tk else 1)
                if vmem > MM_VMEM_BUDGET:
                    continue
                steps = (M // tm) * (unit_n // tn) * (unit_k // tk)
                key = (steps, -tk, -tn)
                if best is None or key < best[0]:
                    best = (key, (tm, tn, tk))
                break
    return best[1]


def _rowmap(fn, name, M, row_ins, bc_ins, row_outs, acc_outs=(), after=None):
    row_ins = [r if isinstance(r, tuple) else (r, r.shape[1], 0) for r in row_ins]
    row_bytes = sum(w * a.dtype.itemsize for a, w, _ in row_ins) + sum(w * jnp.dtype(d).itemsize for w, d in row_outs)
    widest = max([w for _, w, _ in row_ins] + [w for w, _ in row_outs])
    row_bytes = 2 * row_bytes + 6 * 4 * widest
    tm = _pick(M, [t for t in (512, 256, 128, 64, 32, 16) if t * row_bytes <= ROWMAP_TILE_BYTES] + [16])
    n_in, n_row, n_acc = len(row_ins) + len(bc_ins), len(row_outs), len(acc_outs)

    def body(*refs):
        res = fn(*[r[...].astype(F32) for r in refs[:n_in]])
        res = res if isinstance(res, (tuple, list)) else (res,)
        outs = refs[n_in + (after is not None):]
        for k in range(n_row):
            outs[k][...] = res[k].astype(outs[k].dtype)
        if n_acc:
            @pl.when(pl.program_id(0) == 0)
            def _():
                for k in range(n_acc):
                    outs[n_row + k][...] = jnp.zeros_like(outs[n_row + k])

            for k in range(n_acc):
                outs[n_row + k][...] += res[n_row + k].astype(F32)

    in_specs = [pl.BlockSpec((tm, w), functools.partial(lambda i, blk: (i, blk), blk=blk)) for _, w, blk in row_ins]
    in_specs += [pl.BlockSpec(b.shape, lambda i: (0, 0)) for b in bc_ins]
    in_specs += [pl.BlockSpec(memory_space=pl.ANY)] * (after is not None)
    out_specs = [pl.BlockSpec((tm, w), lambda i: (i, 0)) for w, _ in row_outs]
    out_specs += [pl.BlockSpec((1, w), lambda i: (0, 0)) for w in acc_outs]
    out_shape = [jax.ShapeDtypeStruct((M, w), d) for w, d in row_outs]
    out_shape += [jax.ShapeDtypeStruct((1, w), F32) for w in acc_outs]
    return pl.pallas_call(
        body, name=name, grid=(M // tm,), in_specs=in_specs, out_specs=out_specs, out_shape=out_shape,
        compiler_params=_cparams(("arbitrary",) if n_acc else ("parallel",)),
    )(*[a for a, _, _ in row_ins], *bc_ins, *([] if after is None else [after]))


def _rms(x, g):
    return x * lax.rsqrt(jnp.mean(x * x, axis=-1, keepdims=True) + EPS) * g


def _normmod(x, g, sc, sh):
    return _rms(x, g) * (1.0 + sc) + sh


def _swap16(v):
    w = v.shape[1]
    lane = lax.broadcasted_iota(jnp.int32, v.shape, 1)
    return jnp.where((lane // 16) % 2 == 0, pltpu.roll(v, w - 16, 1), pltpu.roll(v, 16, 1))


def _rope(v, cos, sin_signed):
    return v * cos + _swap16(v) * sin_signed


def _rope_bwd(d, cos, sin_signed):
    return d * cos + _swap16(d * sin_signed)


def _mesh_pos():
    return lax.axis_index("x"), lax.axis_index("y"), lax.axis_index("c")


def _hbm_call(body, name, ins, out_shapes, n_sems):
    any_spec = pl.BlockSpec(memory_space=pl.ANY)
    return pl.pallas_call(
        body, name=name, out_shape=out_shapes, in_specs=[any_spec] * len(ins), out_specs=[any_spec] * len(out_shapes),
        scratch_shapes=[pltpu.SemaphoreType.DMA((n_sems,)), pltpu.SemaphoreType.DMA((n_sems,)),
                        pltpu.SemaphoreType.DMA((len(ins),))],
    )(*ins)


def _all_gather(xs, name, after=None):
    n = len(xs)

    def body(*refs):
        k = n + (after is not None)
        x_refs, out_refs, (send_sems, recv_sems, local_sems) = refs[:n], refs[k:k + n], refs[k + n:]
        x, y, c = _mesh_pos()
        me, sibling = (x, y, c), (x, y, 1 - c)
        chips = [(1 - x, y), (x, 1 - y), (1 - x, 1 - y)]
        locals_, first, passed, arrivals = [], [], [], []
        for a in range(n):
            def slot(px, py, pc, a=a):
                return out_refs[a].at[4 * px + 2 * py + pc]

            def copy(k, block, to, src=None, a=a, slot=slot):
                return pltpu.make_async_remote_copy(
                    src_ref=slot(*block) if src is None else src, dst_ref=slot(*block),
                    send_sem=send_sems.at[7 * a + k], recv_sem=recv_sems.at[7 * a + k], device_id=to, device_id_type=MESH)

            locals_.append(pltpu.make_async_copy(x_refs[a], slot(*me), local_sems.at[a]))
            first.append(copy(0, me, sibling, src=x_refs[a]))
            first += [copy(1 + j, me, (*chip, c), src=x_refs[a]) for j, chip in enumerate(chips)]
            passed.append([copy(4 + j, (*chip, c), sibling) for j, chip in enumerate(chips)])
            arrivals.append([copy(1 + j, (*chip, c), me) for j, chip in enumerate(chips)]
                            + [copy(0, sibling, me)] + [copy(4 + j, (*chip, 1 - c), me) for j, chip in enumerate(chips)])
        for cp in locals_ + first:
            cp.start()
        for j in range(3):
            for a in range(n):
                arrivals[a][j].wait_recv()
                passed[a][j].start()
        for a in range(n):
            for cp in arrivals[a][3:]:
                cp.wait_recv()
        for cp in first + [p for ps in passed for p in ps]:
            cp.wait_send()
        for cp in locals_:
            cp.wait()

    return _hbm_call(body, name, list(xs) + ([] if after is None else [after]),
                     [jax.ShapeDtypeStruct((N_DEV,) + x.shape, x.dtype) for x in xs], 7 * n)


def _rs_pair(ps, name):
    n = len(ps)

    def body(*refs):
        p_refs, out_refs, (send_sems, recv_sems, _) = refs[:n], refs[n:2 * n], refs[2 * n:]
        x, y, c = _mesh_pos()
        sends, recvs = [], []
        for a in range(n):
            for q in range(N_CHIP):
                sem = dict(send_sem=send_sems.at[4 * a + q], recv_sem=recv_sems.at[4 * a + q],
                           device_id=(x, y, 1 - c), device_id_type=MESH)
                sends.append(pltpu.make_async_remote_copy(src_ref=p_refs[a].at[2 * q + 1 - c], dst_ref=out_refs[a].at[q], **sem))
                recvs.append(pltpu.make_async_remote_copy(src_ref=p_refs[a].at[2 * q + c], dst_ref=out_refs[a].at[q], **sem))
        for cp in sends:
            cp.start()
        for cp in recvs:
            cp.wait_recv()
        for cp in sends:
            cp.wait_send()

    return _hbm_call(body, name, ps, [jax.ShapeDtypeStruct((N_CHIP,) + p.shape[1:], p.dtype) for p in ps], 4 * n)


def _xchg_copies(src_refs, land_refs, send_sems, recv_sems, slot_src):
    x, y, c = _mesh_pos()
    sends, recvs = [], []
    for a, (src, land) in enumerate(zip(src_refs, land_refs)):
        chips = land.shape[0] == N_CHIP
        npeer = land.shape[0] - 1
        me = 2 * x + y if chips else 4 * x + 2 * y + c
        for r in range(1, npeer + 1):
            px = 1 - x if r & (2 if chips else 4) else x
            py = 1 - y if r & (1 if chips else 2) else y
            pc = c if chips else (1 - c if r & 1 else c)
            peer = 2 * px + py if chips else 4 * px + 2 * py + pc
            sem = dict(send_sem=send_sems.at[npeer * a + r - 1], recv_sem=recv_sems.at[npeer * a + r - 1],
                       device_id=(px, py, pc), device_id_type=MESH)
            s = src.at[peer] if slot_src else src
            sends.append(pltpu.make_async_remote_copy(src_ref=s, dst_ref=land.at[me], **sem))
            recvs.append(pltpu.make_async_remote_copy(src_ref=s, dst_ref=land.at[peer], **sem))
    return sends, recvs


_HBM = pl.BlockSpec(memory_space=pltpu.HBM)
_SEM = pl.BlockSpec(memory_space=pltpu.SEMAPHORE)
_EFFECT = pltpu.SideEffectType.DATAFLOW_SIDE_EFFECTING


def _xchg_start(srcs, lands, slot_src, name):
    n = len(srcs)

    def body(*refs):
        sends, _ = _xchg_copies(refs[:n], refs[n:2 * n], refs[2 * n], refs[2 * n + 1], slot_src)
        for cp in sends:
            cp.start()
        refs[-1][...] = jnp.zeros_like(refs[-1])

    bufs = list(srcs) + list(lands)
    n_sems = n * (lands[0].shape[0] - 1)
    res = pl.pallas_call(
        body, name=name,
        out_shape=(pltpu.SemaphoreType.DMA((n_sems,)), pltpu.SemaphoreType.DMA((n_sems,)))
        + tuple(pltpu.HBM(b.shape, b.dtype) for b in bufs) + (jax.ShapeDtypeStruct((SUB, LANE), F32),),
        in_specs=(_HBM,) * (2 * n), out_specs=(_SEM, _SEM) + (_HBM,) * (2 * n) + (pl.BlockSpec(memory_space=pltpu.VMEM),),
        input_output_aliases={i: 2 + i for i in range(2 * n)},
        compiler_params=pltpu.CompilerParams(has_side_effects=_EFFECT),
    )(*[pltpu.with_memory_space_constraint(b, pltpu.HBM) for b in bufs])
    return res[0], res[1], res[2:-1], res[-1]


def _xchg_wait(send_sems, recv_sems, thru, after, slot_src, name):
    n = len(thru) // 2

    def body(*refs):
        sends, recvs = _xchg_copies(refs[:n], refs[n:2 * n], refs[2 * n], refs[2 * n + 1], slot_src)
        for cp in sends:
            cp.wait_send()
        for cp in recvs:
            cp.wait_recv()

    res = pl.pallas_call(
        body, name=name, out_shape=tuple(pltpu.HBM(b.shape, b.dtype) for b in thru),
        in_specs=(_HBM,) * (2 * n) + (_SEM, _SEM, pl.BlockSpec(memory_space=pl.ANY)), out_specs=(_HBM,) * (2 * n),
        input_output_aliases={i: i for i in range(2 * n)},
        compiler_params=pltpu.CompilerParams(has_side_effects=_EFFECT),
    )(*thru, send_sems, recv_sems, after)
    return res[n:]


def _ag2_copy(land, sems, k, block, to, src=None):
    slot = land.at[4 * block[0] + 2 * block[1] + block[2]]
    return pltpu.make_async_remote_copy(src_ref=slot if src is None else src, dst_ref=slot, send_sem=sems[0].at[k],
                                        recv_sem=sems[1].at[k], device_id=to, device_id_type=MESH)


def _ag2_start(blocks, lands, name):
    n = len(blocks)

    def body(*refs):
        x, y, c = _mesh_pos()
        for a in range(n):
            sems = (refs[2 * n], refs[2 * n + 1])
            _ag2_copy(refs[n + a], sems, 4 * a, (x, y, c), (x, y, 1 - c), src=refs[a]).start()
            for j, chip in enumerate([(1 - x, y), (x, 1 - y), (1 - x, 1 - y)]):
                _ag2_copy(refs[n + a], sems, 4 * a + 1 + j, (x, y, c), (*chip, c), src=refs[a]).start()
        refs[-1][...] = jnp.zeros_like(refs[-1])

    bufs = list(blocks) + list(lands)
    res = pl.pallas_call(
        body, name=name,
        out_shape=(pltpu.SemaphoreType.DMA((4 * n,)), pltpu.SemaphoreType.DMA((4 * n,)))
        + tuple(pltpu.HBM(b.shape, b.dtype) for b in bufs) + (jax.ShapeDtypeStruct((SUB, LANE), F32),),
        in_specs=(_HBM,) * (2 * n), out_specs=(_SEM, _SEM) + (_HBM,) * (2 * n) + (pl.BlockSpec(memory_space=pltpu.VMEM),),
        input_output_aliases={i: 2 + i for i in range(2 * n)},
        compiler_params=pltpu.CompilerParams(has_side_effects=_EFFECT),
    )(*[pltpu.with_memory_space_constraint(b, pltpu.HBM) for b in bufs])
    return (res[0], res[1]), res[2:-1], res[-1]


def _ag2_mid(sems1, thru, after, name):
    n = len(thru) // 2

    def body(*refs):
        x, y, c = _mesh_pos()
        s1, s2 = (refs[2 * n], refs[2 * n + 1]), (refs[2 * n + 3], refs[2 * n + 4])
        for j, chip in enumerate([(1 - x, y), (x, 1 - y), (1 - x, 1 - y)]):
            for a in range(n):
                _ag2_copy(refs[n + a], s1, 4 * a + 1 + j, (*chip, c), (x, y, c)).wait_recv()
                _ag2_copy(refs[n + a], s2, 3 * a + j, (*chip, c), (x, y, 1 - c)).start()
        refs[-1][...] = jnp.zeros_like(refs[-1])

    res = pl.pallas_call(
        body, name=name,
        out_shape=(pltpu.SemaphoreType.DMA((3 * n,)), pltpu.SemaphoreType.DMA((3 * n,)))
        + tuple(pltpu.HBM(b.shape, b.dtype) for b in thru) + (jax.ShapeDtypeStruct((SUB, LANE), F32),),
        in_specs=(_HBM,) * (2 * n) + (_SEM, _SEM, pl.BlockSpec(memory_space=pl.ANY)),
        out_specs=(_SEM, _SEM) + (_HBM,) * (2 * n) + (pl.BlockSpec(memory_space=pltpu.VMEM),),
        input_output_aliases={i: 2 + i for i in range(2 * n)},
        compiler_params=pltpu.CompilerParams(has_side_effects=_EFFECT),
    )(*thru, *sems1, after)
    return (res[0], res[1]), res[2:-1], res[-1]


def _ag2_end(sems1, sems2, thru, after, name):
    n = len(thru) // 2

    def body(*refs):
        x, y, c = _mesh_pos()
        s1, s2 = (refs[2 * n], refs[2 * n + 1]), (refs[2 * n + 2], refs[2 * n + 3])
        chips = [(1 - x, y), (x, 1 - y), (1 - x, 1 - y)]
        for a in range(n):
            land = refs[n + a]
            _ag2_copy(land, s1, 4 * a, (x, y, c), (x, y, 1 - c), src=refs[a]).wait_send()
            _ag2_copy(land, s1, 4 * a, (x, y, 1 - c), (x, y, c)).wait_recv()
            for j, chip in enumerate(chips):
                _ag2_copy(land, s1, 4 * a + 1 + j, (x, y, c), (*chip, c), src=refs[a]).wait_send()
                _ag2_copy(land, s2, 3 * a + j, (*chip, c), (x, y, 1 - c)).wait_send()
                _ag2_copy(land, s2, 3 * a + j, (*chip, 1 - c), (x, y, c)).wait_recv()

    res = pl.pallas_call(
        body, name=name, out_shape=tuple(pltpu.HBM(b.shape, b.dtype) for b in thru),
        in_specs=(_HBM,) * (2 * n) + (_SEM,) * 4 + (pl.BlockSpec(memory_space=pl.ANY),), out_specs=(_HBM,) * (2 * n),
        input_output_aliases={i: i for i in range(2 * n)},
        compiler_params=pltpu.CompilerParams(has_side_effects=_EFFECT),
    )(*thru, *sems1, *sems2, after)
    return res[n:]


def _own_slot(block, me, slots=N_DEV):
    return lax.dynamic_update_slice(lax.empty((slots,) + block.shape, block.dtype), block[None], (me, 0, 0))


def _add_pair(p, r, name, after):
    _, R, C = p.shape
    tr = _pick(R, (512, 256, 128, 64, 32, 16))

    def body(c_ref, p_ref, r_ref, after_ref, o_ref):
        o_ref[...] = (p_ref[...].astype(F32) + r_ref[...].astype(F32)).astype(o_ref.dtype)

    return pl.pallas_call(
        body, name=name, out_shape=jax.ShapeDtypeStruct((N_CHIP, R, C), p.dtype),
        grid_spec=pltpu.PrefetchScalarGridSpec(
            num_scalar_prefetch=1, grid=(N_CHIP, R // tr),
            in_specs=[pl.BlockSpec((None, None, tr, C), lambda q, i, c_ref: (q, c_ref[0], i, 0)),
                      pl.BlockSpec((None, tr, C), lambda q, i, c_ref: (q, i, 0)), pl.BlockSpec(memory_space=pl.ANY)],
            out_specs=pl.BlockSpec((None, tr, C), lambda q, i, c_ref: (q, i, 0))),
        compiler_params=_cparams(("parallel", "parallel")),
    )(lax.axis_index("c").reshape(1).astype(jnp.int32), p.reshape(N_CHIP, 2, R, C), r, after)


def _sum_slots(g, name):
    ns, R, C = g.shape
    tr = _pick(R, (256, 128, 64, 32, 16))

    def body(g_ref, o_ref):
        acc = g_ref[0].astype(F32)
        for j in range(1, ns):
            acc = acc + g_ref[j].astype(F32)
        o_ref[...] = acc

    return pl.pallas_call(
        body, name=name, grid=(R // tr,),
        in_specs=[pl.BlockSpec((ns, tr, C), lambda i: (0, i, 0))], out_specs=pl.BlockSpec((tr, C), lambda i: (i, 0)),
        out_shape=jax.ShapeDtypeStruct((R, C), F32), compiler_params=_cparams(("parallel",)),
    )(g)


def _pack_rows(arrs, dtype):
    parts = []
    for a in arrs:
        flat = a.reshape(-1).astype(dtype)
        pad = (-flat.shape[0]) % (PACK_W * 16)
        parts.append(jnp.pad(flat, (0, pad)).reshape(-1, PACK_W))
    out = jnp.concatenate(parts, axis=0)
    return jnp.pad(out, ((0, (-out.shape[0]) % PACK_ROWS), (0, 0)))


def _packed_rows(shape):
    n = math.prod(shape)
    return (n + PACK_W * 16 - 1) // (PACK_W * 16) * 16


def _unpack_rows(packed, shapes):
    out, r0 = [], 0
    for s in shapes:
        rows, n = _packed_rows(s), math.prod(s)
        out.append(packed[r0:r0 + rows].reshape(rows * PACK_W)[:n].reshape(s))
        r0 += rows
    return out


def _adamw_math(w, g, m, v):
    m = ADAM_B1 * m + (1.0 - ADAM_B1) * g
    v = ADAM_B2 * v + (1.0 - ADAM_B2) * (g * g)
    m_hat = m / (1.0 - ADAM_B1 ** ADAM_STEP)
    v_hat = v / (1.0 - ADAM_B2 ** ADAM_STEP)
    delta = -ADAM_LR * (m_hat / (jnp.sqrt(v_hat) + ADAM_EPS) + ADAM_WD * w)
    return delta, m, v


def _adamw_slots(w, gs, m, v, name, after=None):
    ns, R, C = gs.shape
    row_bytes = 2 * (ns * C * gs.dtype.itemsize + 7 * C * 4) + 6 * 4 * C
    tr = _pick(R, [t for t in (512, 256, 128, 64, 32, 16) if t * row_bytes <= ROWMAP_TILE_BYTES] + [16])

    def body(w_ref, g_ref, m_ref, v_ref, *rest):
        outs = rest[(after is not None):]
        g = g_ref[0].astype(F32)
        for j in range(1, ns):
            g = g + g_ref[j].astype(F32)
        res = (g,) + _adamw_math(w_ref[...], g, m_ref[...], v_ref[...])
        for o_ref, val in zip(outs, res):
            o_ref[...] = val

    row = pl.BlockSpec((tr, C), lambda i: (i, 0))
    return pl.pallas_call(
        body, name=name, grid=(R // tr,),
        in_specs=[row, pl.BlockSpec((ns, tr, C), lambda i: (0, i, 0)), row, row]
        + [pl.BlockSpec(memory_space=pl.ANY)] * (after is not None),
        out_specs=[row] * 4, out_shape=[jax.ShapeDtypeStruct((R, C), F32)] * 4, compiler_params=_cparams(("parallel",)),
    )(w, gs, m, v, *([] if after is None else [after]))


def _adamw(w, g, m, v, name, after=None):
    R, C = w.shape
    return _rowmap(_adamw_math, name, R, [w, g, m, v], [], [(C, F32)] * 3, after=after)


def _s5_disc_math(lr, li, ldt, br, bi):
    dt = jnp.exp(ldt)
    mag = jnp.exp(lr * dt)
    ab_re, ab_im = mag * jnp.cos(li * dt), mag * jnp.sin(li * dt)
    den = lr * lr + li * li
    nr, ni = ab_re - 1.0, ab_im
    co_re = (nr * lr + ni * li) / den
    co_im = (ni * lr - nr * li) / den
    bb_re = co_re * br - co_im * bi
    bb_im = co_re * bi + co_im * br
    return ab_re, ab_im, bb_re, bb_im


def _s5_tables(a_re, a_im, ldt, b_re, b_im, c_re, c_im):
    _, G, P, N = b_re.shape
    nch = G // 8

    def body(lr_ref, li_ref, ldt_ref, br_ref, bi_ref, cr_ref, ci_ref, wre, wim, vre, vim, pwr, pwi):
        ar, ai, bb_re, bb_im = _s5_disc_math(lr_ref[0], li_ref[0], ldt_ref[0], br_ref[0], bi_ref[0])
        cr, ci = cr_ref[0], ci_ref[0]
        pr, pi = jnp.ones_like(ar), jnp.zeros_like(ar)
        for j in range(SUB + 1):
            pwr[0, j], pwi[0, j] = pr, pi
            if j < SUB:
                tabs = ((wre, bb_re * pr - bb_im * pi), (wim, bb_re * pi + bb_im * pr),
                        (vre, cr * pr - ci * pi), (vim, -(cr * pi + ci * pr)))
                for ref, val in tabs:
                    for s in range(nch):
                        ref[0, s, pl.ds(j * LANE, LANE), :] = val[s * 8:(s + 1) * 8].reshape(LANE, N).astype(BF16)
            pr, pi = pr * ar - pi * ai, pr * ai + pi * ar

    g1n = pl.BlockSpec((1, G, 1, N), lambda d: (d, 0, 0, 0))
    gpn = pl.BlockSpec((1, G, P, N), lambda d: (d, 0, 0, 0))
    tab = pl.BlockSpec((1, nch, SUB * LANE, N), lambda d: (d, 0, 0, 0))
    pw = pl.BlockSpec((1, SUB + 1, G, 1, N), lambda d: (d, 0, 0, 0, 0))
    s_tab = jax.ShapeDtypeStruct((2, nch, SUB * LANE, N), BF16)
    s_pw = jax.ShapeDtypeStruct((2, SUB + 1, G, 1, N), F32)
    return pl.pallas_call(
        body, name="s5_tables", grid=(2,),
        in_specs=[g1n, g1n, pl.BlockSpec((1, G, 1, 1), lambda d: (d, 0, 0, 0)), gpn, gpn, gpn, gpn],
        out_specs=[tab] * 4 + [pw] * 2, out_shape=[s_tab] * 4 + [s_pw] * 2,
        compiler_params=_cparams(("parallel",)),
    )(a_re, a_im, ldt, b_re, b_im, c_re, c_im)


def _s5_expand(t_re, t_im, name):
    _, nch, R, N = t_re.shape
    sw = 8 * N

    def body(re_ref, im_ref, o_ref):
        spread = (lax.broadcasted_iota(jnp.int32, (N, sw), 1) % N == lax.broadcasted_iota(jnp.int32, (N, sw), 0)).astype(BF16)
        row_g = (lax.broadcasted_iota(jnp.int32, (R, sw), 0) % LANE) // S5_GROUP
        keep = row_g == lax.broadcasted_iota(jnp.int32, (R, sw), 1) // N
        for half, ref in enumerate((re_ref, im_ref)):
            t = jnp.dot(ref[0, 0], spread, preferred_element_type=F32)
            o_ref[0, 0, :, pl.ds(half * sw, sw)] = jnp.where(keep, t, 0.0).astype(BF16)

    spec = pl.BlockSpec((1, 1, R, N), lambda d, s: (d, s, 0, 0))
    return pl.pallas_call(
        body, name=name, grid=(2, nch), in_specs=[spec, spec],
        out_specs=pl.BlockSpec((1, 1, R, 2 * sw), lambda d, s: (d, s, 0, 0)),
        out_shape=jax.ShapeDtypeStruct((2, nch, R, 2 * sw), BF16), compiler_params=_cparams(("parallel", "parallel")),
    )(t_re, t_im)


def _s5_param_bwd(a_re, a_im, ldt, b_re, b_im, da_re, da_im, dbb_re, dbb_im):
    _, G, P, N = b_re.shape

    def body(lr_ref, li_ref, ldt_ref, br_ref, bi_ref, dar, dai, dbr, dbi, o_lr, o_li, o_ldt, o_br, o_bi):
        _, vjp = jax.vjp(_s5_disc_math, lr_ref[0], li_ref[0], ldt_ref[0], br_ref[0], bi_ref[0])
        o_lr[0], o_li[0], o_ldt[0], o_br[0], o_bi[0] = vjp((dar[0], dai[0], dbr[0], dbi[0]))

    g1n = pl.BlockSpec((1, G, 1, N), lambda d: (d, 0, 0, 0))
    g11 = pl.BlockSpec((1, G, 1, 1), lambda d: (d, 0, 0, 0))
    gpn = pl.BlockSpec((1, G, P, N), lambda d: (d, 0, 0, 0))
    s_g1n, s_g11, s_gpn = (jax.ShapeDtypeStruct(s, F32) for s in ((2, G, 1, N), (2, G, 1, 1), (2, G, P, N)))
    return pl.pallas_call(
        body, name="s5_param_bwd", grid=(2,),
        in_specs=[g1n, g1n, g11, gpn, gpn, g1n, g1n, gpn, gpn], out_specs=[g1n, g1n, g11, gpn, gpn],
        out_shape=[s_g1n, s_g1n, s_g11, s_gpn, s_gpn], compiler_params=_cparams(("parallel",)),
    )(a_re, a_im, ldt, b_re, b_im, da_re, da_im, dbb_re, dbb_im)


def _tile_local_scan(u, w_ref, back):
    tb, sw2 = u.shape[0], w_ref.shape[1]
    sw, half = sw2 // 2, LANE // 2
    tau = lax.broadcasted_iota(jnp.int32, u.shape, 0) % SUB
    low = lax.broadcasted_iota(jnp.int32, u.shape, 1) < half
    parts = [u]
    for j in range(1, SUB):
        if back:
            parts.append(jnp.where(tau >= j, pltpu.roll(u, j, 0), 0.0))
        else:
            parts.append(jnp.where(tau <= SUB - 1 - j, pltpu.roll(u, tb - j, 0), 0.0))
    out = [None] * 4
    for h in range(2):
        pieces = [jnp.where(low, a, pltpu.roll(b, half, 1)) if h == 0 else jnp.where(low, pltpu.roll(a, half, 1), b)
                  for a, b in zip(parts[0::2], parts[1::2])]
        lhs = jnp.concatenate(pieces, axis=1).astype(BF16)
        rows = jnp.concatenate([w_ref[pl.ds(j * LANE + h * half, half), :] for j in range(SUB)], axis=0)
        for part in range(2):
            cols = rows[:, part * sw + h * (sw // 2):part * sw + (h + 1) * (sw // 2)]
            out[2 * part + h] = jnp.dot(lhs, cols, preferred_element_type=F32)
    return jnp.concatenate(out, axis=1)


def _cmul_add(tile, pw, carry, sw):
    pr, pi, cr, ci = pw[:, :sw], pw[:, sw:], carry[:, :sw], carry[:, sw:]
    return tile + jnp.concatenate([pr * cr - pi * ci, pr * ci + pi * cr], axis=1)


def _tile_scan(buf, base, ntile, pw, carry, sw, causal):
    def step(k, c):
        i = k if causal else ntile - 1 - k
        r = pl.multiple_of(base + i * SUB, SUB)
        tile = _cmul_add(buf[pl.ds(r, SUB), :], pw, c, sw)
        buf[pl.ds(r, SUB), :] = tile
        return tile[SUB - 1:SUB, :] if causal else tile[0:1, :]

    return lax.fori_loop(0, ntile, step, carry)


def _s5_fwd(h_all, waug, vaug, pw, S5W, T, d, causal, name):
    S = h_all.shape[0]
    _, nch, _, sw2 = waug.shape
    sw = sw2 // 2
    tb = _pick(math.gcd(T, S - T), (256, 128, 64, 32, 16))
    ntile, nt, off = tb // SUB, S // tb, T // tb
    rb = (lambda s, t: ((t + off) % nt, s)) if causal else (lambda s, t: (nt - 1 - t, s))

    def body(u_ref, w_ref, v_ref, p_ref, y_ref, h_ref, hblk, carry):
        @pl.when(pl.program_id(1) == 0)
        def _():
            carry[...] = jnp.zeros_like(carry)

        hblk[...] = _tile_local_scan(u_ref[...].astype(F32), w_ref, causal)
        carry[...] = _tile_scan(hblk, 0, ntile, p_ref[...], carry[...], sw, causal)
        hb = hblk[...].astype(BF16)
        h_ref[...] = hb
        y_ref[...] = lax.dot_general(hb, v_ref[...], _NT, preferred_element_type=F32)

    return pl.pallas_call(
        body, name=name, grid=(nch, nt),
        in_specs=[pl.BlockSpec((tb, LANE), rb),
                  pl.BlockSpec((None, None, SUB * LANE, sw2), lambda s, t: (d, s, 0, 0)),
                  pl.BlockSpec((None, None, LANE, sw2), lambda s, t: (d, s, 0, 0)),
                  pl.BlockSpec((None, SUB, sw2), lambda s, t: (s, 0, 0))],
        out_specs=[pl.BlockSpec((tb, LANE), rb), pl.BlockSpec((tb, sw2), rb)],
        out_shape=[jax.ShapeDtypeStruct((S, S5W), F32), jax.ShapeDtypeStruct((S, nch * sw2), BF16)],
        scratch_shapes=[pltpu.VMEM((tb, sw2), F32), pltpu.VMEM((1, sw2), F32)],
        compiler_params=_cparams(("parallel", "arbitrary")),
    )(h_all, waug, vaug, pw)


def _s5_bwd(dy_all, h_all, hs, waug, vaug, pwc, S5W, T, d, causal, name):
    S = h_all.shape[0]
    _, nch, _, sw2 = waug.shape
    sw = sw2 // 2
    tb = _pick(math.gcd(T, S - T), (256, 128, 64, 32, 16))
    ntile, nt, off = tb // SUB, S // tb, T // tb
    rb = (lambda s, t: ((nt - 1 - t + off) % nt, s)) if causal else (lambda s, t: (t, s))
    adj_causal = not causal
    edge = SUB - 1 if adj_causal else SUB + tb
    keep_src, keep_dst = (tb, 0) if adj_causal else (SUB, SUB + tb)

    def body(dy_ref, u_ref, h_ref, w_ref, v_ref, p_ref, du_ref, dbb_ref, dc_ref, da_ref, lam):
        @pl.when(pl.program_id(1) == 0)
        def _():
            lam[pl.ds(0, SUB), :] = jnp.zeros((SUB, sw2), F32)
            lam[pl.ds(SUB + tb, SUB), :] = jnp.zeros((SUB, sw2), F32)
            dbb_ref[...] = jnp.zeros_like(dbb_ref)
            dc_ref[...] = jnp.zeros_like(dc_ref)
            da_ref[...] = jnp.zeros_like(da_ref)

        dy = dy_ref[...]
        lam[pl.ds(SUB, tb), :] = _tile_local_scan(dy, v_ref, adj_causal)
        _tile_scan(lam, SUB, ntile, p_ref[...], lam[pl.ds(edge, 1), :], sw, adj_causal)
        lb = lam[pl.ds(SUB, tb), :].astype(BF16)
        du_ref[...] = lax.dot_general(lb, w_ref[...], _NT, preferred_element_type=F32)
        dbb_ref[...] += lax.dot_general(u_ref[...].astype(BF16), lb, _TN, preferred_element_type=F32)
        dc_ref[...] += lax.dot_general(h_ref[...], dy.astype(BF16), _TN, preferred_element_type=F32)
        h = h_ref[...].astype(F32)
        ln = lam[pl.ds(SUB + 1 if causal else SUB - 1, tb), :]
        hr, hi, lr, li = h[:, :sw], h[:, sw:], ln[:, :sw], ln[:, sw:]
        da_ref[...] += jnp.concatenate([jnp.sum(hr * lr + hi * li, axis=0, keepdims=True),
                                        jnp.sum(hr * li - hi * lr, axis=0, keepdims=True)], axis=1)
        lam[pl.ds(keep_dst, SUB), :] = lam[pl.ds(keep_src, SUB), :]

    fixed = lambda s, t: (s, 0, 0)
    return pl.pallas_call(
        body, name=name, grid=(nch, nt),
        in_specs=[pl.BlockSpec((tb, LANE), rb), pl.BlockSpec((tb, LANE), rb), pl.BlockSpec((tb, sw2), rb),
                  pl.BlockSpec((None, None, LANE, sw2), lambda s, t: (d, s, 0, 0)),
                  pl.BlockSpec((None, None, SUB * LANE, sw2), lambda s, t: (d, s, 0, 0)),
                  pl.BlockSpec((None, SUB, sw2), fixed)],
        out_specs=[pl.BlockSpec((tb, LANE), rb), pl.BlockSpec((None, LANE, sw2), fixed),
                   pl.BlockSpec((None, sw2, LANE), fixed), pl.BlockSpec((None, 1, sw2), fixed)],
        out_shape=[jax.ShapeDtypeStruct((S, S5W), F32), jax.ShapeDtypeStruct((nch, LANE, sw2), F32),
                   jax.ShapeDtypeStruct((nch, sw2, LANE), F32), jax.ShapeDtypeStruct((nch, 1, sw2), F32)],
        scratch_shapes=[pltpu.VMEM((tb + 2 * SUB, sw2), F32)],
        compiler_params=_cparams(("parallel", "arbitrary")),
    )(dy_all, h_all, hs, waug, vaug, pwc)


def _attn_fwd(qn, qr, kv, kr, H, scale):
    T, S = qn.shape[0], kv.shape[0]
    tq = _pick(T, (256, 128, 64, 32, 16))

    def body(qn_ref, qr_ref, kn_ref, v_ref, kr_ref, o_ref, lse_ref):
        q = jnp.concatenate([qn_ref[...], qr_ref[...]], axis=1)
        k = jnp.concatenate([kn_ref[...], kr_ref[...]], axis=1)
        s = lax.dot_general(q, k, _NT, preferred_element_type=F32) * scale
        m = jnp.max(s, axis=1, keepdims=True)
        p = jnp.exp(s - m)
        l = jnp.sum(p, axis=1, keepdims=True)
        o_ref[...] = jnp.dot((p * (1.0 / l)).astype(BF16), v_ref[...], preferred_element_type=F32).astype(o_ref.dtype)
        lse_ref[0] = m + jnp.log(l)

    q_spec = pl.BlockSpec((tq, LANE), lambda h, i: (i, h))
    return pl.pallas_call(
        body, name="attn_fwd", grid=(H, T // tq),
        in_specs=[q_spec, q_spec, pl.BlockSpec((S, LANE), lambda h, i: (0, h)), pl.BlockSpec((S, LANE), lambda h, i: (0, H + h)),
                  pl.BlockSpec((S, LANE), lambda h, i: (0, 0))],
        out_specs=[q_spec, pl.BlockSpec((1, tq, 1), lambda h, i: (h, i, 0))],
        out_shape=[jax.ShapeDtypeStruct((T, H * LANE), BF16), jax.ShapeDtypeStruct((H, T, 1), F32)],
        compiler_params=_cparams(("parallel", "parallel")),
    )(qn, qr, kv, kv, kr)


def _attn_bwd(qn, qr, kv, kr, do, lse, H, scale):
    T, S = qn.shape[0], kv.shape[0]
    tq = _pick(T, (512, 256, 128, 64, 32, 16))
    nq = T // tq

    def body(qn_ref, qr_ref, kn_ref, v_ref, kr_ref, do_ref, lse_ref, dqn_ref, dqr_ref, dkn_ref, dkr_ref, dv_ref, dk_acc, dv_acc):
        i = pl.program_id(1)
        q = jnp.concatenate([qn_ref[...], qr_ref[...]], axis=1)
        k = jnp.concatenate([kn_ref[...], kr_ref[...]], axis=1)
        v, d_o = v_ref[...], do_ref[...]
        s = lax.dot_general(q, k, _NT, preferred_element_type=F32) * scale
        p = jnp.exp(s - lse_ref[0])
        dv_part = lax.dot_general(p.astype(BF16), d_o, _TN, preferred_element_type=F32)
        dp = lax.dot_general(d_o, v, _NT, preferred_element_type=F32)
        ds = (p * (dp - jnp.sum(p * dp, axis=1, keepdims=True)) * scale).astype(BF16)
        dq = jnp.dot(ds, k, preferred_element_type=F32)
        dqn_ref[...] = dq[:, :LANE].astype(dqn_ref.dtype)
        dqr_ref[...] = dq[:, LANE:].astype(dqr_ref.dtype)
        dk_part = lax.dot_general(ds, q, _TN, preferred_element_type=F32)

        @pl.when(i == 0)
        def _():
            dk_acc[...] = dk_part
            dv_acc[...] = dv_part

        @pl.when(i > 0)
        def _():
            dk_acc[...] += dk_part
            dv_acc[...] += dv_part

        @pl.when(i == nq - 1)
        def _():
            dkn_ref[...] = dk_acc[:, :LANE].astype(dkn_ref.dtype)
            dkr_ref[...] = dk_acc[:, LANE:].astype(dkr_ref.dtype)
            dv_ref[...] = dv_acc[...].astype(dv_ref.dtype)

    q_spec = pl.BlockSpec((tq, LANE), lambda h, i: (i, h))
    k_spec = pl.BlockSpec((S, LANE), lambda h, i: (0, h))
    t_shape, s_shape = jax.ShapeDtypeStruct((T, H * LANE), BF16), jax.ShapeDtypeStruct((S, H * LANE), BF16)
    return pl.pallas_call(
        body, name="attn_bwd", grid=(H, nq),
        in_specs=[q_spec, q_spec, k_spec, pl.BlockSpec((S, LANE), lambda h, i: (0, H + h)),
                  pl.BlockSpec((S, LANE), lambda h, i: (0, 0)), q_spec, pl.BlockSpec((1, tq, 1), lambda h, i: (h, i, 0))],
        out_specs=[q_spec, q_spec, k_spec, k_spec, k_spec], out_shape=[t_shape, t_shape, s_shape, s_shape, s_shape],
        scratch_shapes=[pltpu.VMEM((S, 2 * LANE), F32), pltpu.VMEM((S, LANE), F32)],
        compiler_params=_cparams(("parallel", "arbitrary")),
    )(qn, qr, kv, kv, kr, do, lse)


def _rope_tables(T):
    rows = T // GRID_W
    row = jnp.repeat(jnp.arange(rows, dtype=F32), GRID_W)
    col = jnp.tile(jnp.arange(GRID_W, dtype=F32), rows)
    n_freq = QK_ROPE // 4
    inv = ROPE_BASE ** (-jnp.arange(n_freq, dtype=F32) / n_freq)
    ar, ac = row[:, None] * inv, col[:, None] * inv
    cos = jnp.concatenate([jnp.cos(ar), jnp.cos(ar), jnp.cos(ac), jnp.cos(ac)], axis=1)
    sin = jnp.concatenate([-jnp.sin(ar), jnp.sin(ar), -jnp.sin(ac), jnp.sin(ac)], axis=1)
    pad = lambda t: jnp.pad(t, ((0, 0), (0, LANE - QK_ROPE)))
    return pad(cos), pad(sin)


def _dw(a, dy, w, name):
    return _mm(a, dy, "tn", BF16, name, out_slots=w.shape[0] if w.ndim == 3 else None)


def _local_step(x, ctx, tgt, m_lat, m_ctx, p, W, goff, hooks=None):
    T, D = x.shape
    Tc = ctx.shape[0]
    S = T + Tc
    S5W = p["s5_d"].shape[1]
    QR, KVR = p["q_norm"].shape[1], p["kv_norm"].shape[1]
    G, N = p["s5_a_re"].shape[1:]
    P = S5_GROUP
    nch = G // 8
    o_cq, o_ckv, o_kr = S5W, S5W + QR, S5W + QR + KVR
    assert o_cq % QR == 0 and o_ckv % KVR == 0 and o_kr % LANE == 0 and goff % D == 0 and S5W % LANE == 0 and G % 8 == 0
    assert 8 * P == LANE
    row = lambda k, m: m[k:k + 1]
    sh1, sc1, g1, sh2, sc2, g2 = (row(k, m_lat) for k in range(6))
    csh1, csc1 = row(0, m_ctx), row(1, m_ctx)
    n1, n2, nf = p["norm1"], p["norm2"], p["norm_f"]

    (xm_lat,) = _rowmap(_normmod, "norm1_lat", T, [x], [n1, sc1, sh1], [(D, BF16)])
    (xm_ctx,) = _rowmap(_normmod, "norm1_ctx", Tc, [ctx], [n1, csc1, csh1], [(D, BF16)])
    xm_all = jnp.concatenate([xm_lat, xm_ctx], axis=0)

    a_re, a_im = p["s5_a_re"][:, :, None, :], p["s5_a_im"][:, :, None, :]
    ldt = p["s5_log_dt"][:, :, None, None]
    b_re, b_im = p["s5_b_re"].transpose(0, 1, 3, 2), p["s5_b_im"].transpose(0, 1, 3, 2)
    wre, wim, vre, vim, pwr, pwi = _s5_tables(a_re, a_im, ldt, b_re, b_im, p["s5_c_re"], p["s5_c_im"])
    waug = _s5_expand(wre, wim, "s5_expand_b")
    vaug = _s5_expand(vre, vim, "s5_expand_c")
    lanes = lambda t: t.reshape(2, SUB + 1, nch, 8 * N).transpose(0, 2, 1, 3)
    pw_re, pw_im = lanes(pwr), lanes(pwi)
    near = lambda t: t[:, :, 1:]
    far = lambda t: t[:, :, :0:-1]
    pw_c = jnp.concatenate([near(pw_re), near(pw_im)], axis=-1)
    pw_a = jnp.concatenate([far(pw_re), far(pw_im)], axis=-1)
    pwc_c = jnp.concatenate([near(pw_re), -near(pw_im)], axis=-1)
    pwc_a = jnp.concatenate([far(pw_re), -far(pw_im)], axis=-1)

    if hooks:
        W = {**W, **hooks["first_weights"](xm_all, vaug)}
    H = W["w_uq"].shape[1] // (2 * LANE)
    h_all = _mm(xm_all, W["w_in"], "nn", BF16, "mm_in")
    y0, hs0 = _s5_fwd(h_all, waug, vaug, pw_c[0], S5W, T, 0, True, "s5_scan_fwd0")
    y1, hs1 = _s5_fwd(h_all, waug, vaug, pw_a[1], S5W, T, 1, False, "s5_scan_fwd1")

    def s5_combine(u, yf, yr, dskip):
        y5 = dskip * u + yf + yr
        return y5, jax.nn.gelu(y5)

    y5, z = _rowmap(s5_combine, "s5_combine", T, [(h_all, S5W, 0), y0, y1], [p["s5_d"]], [(S5W, F32), (S5W, BF16)])

    (qn,) = _rowmap(_rms, "q_norm", T, [(h_all, QR, o_cq // QR)], [p["q_norm"]], [(QR, BF16)])
    (kvn,) = _rowmap(_rms, "kv_norm", S, [(h_all, KVR, o_ckv // KVR)], [p["kv_norm"]], [(KVR, BF16)])
    qraw = _mm(qn, W["w_uq"], "nn", F32, "mm_uq")
    kvraw = _mm(kvn, W["w_ukv"], "nn", BF16, "mm_ukv")
    cos_q, sin_q = _rope_tables(T)
    padl = lambda t: jnp.pad(t[:, :LANE], ((0, Tc), (0, 0)))
    cos_k = padl(cos_q) + jnp.pad(jnp.ones((Tc, LANE), F32), ((T, 0), (0, 0)))
    sin_k = padl(sin_q)
    hn = H * LANE

    def q_post(q, cos, sin):
        return q[:, :hn], _rope(q[:, hn:], jnp.tile(cos, (1, H)), jnp.tile(sin, (1, H)))

    q_nope, q_rope = _rowmap(q_post, "q_rope", T, [qraw, cos_q, sin_q], [], [(hn, BF16), (hn, BF16)])
    (kr,) = _rowmap(_rope, "k_rope", S, [(h_all, LANE, o_kr // LANE), cos_k, sin_k], [], [(LANE, BF16)])
    scale = (QK_NOPE + QK_ROPE) ** -0.5
    o, lse = _attn_fwd(q_nope, q_rope, kvraw, kr, H, scale)
    g1_fwd = g1
    if hooks:
        W = {**W, **hooks["mix_weights"](o)}
        g1_fwd = g1 + hooks["ffn_mid"](o)[:1, :1]

    zz = _mm(z, W["w_glu"], "nn", BF16, "mm_glu")
    br_mla = _mm(o, W["w_mla_o"], "nn", BF16, "mm_mla_o")

    def merge(zz, brm, gs, gm):
        a, b = zz[:, :D], zz[:, D:]
        return jax.nn.sigmoid(gs) * (a * jax.nn.sigmoid(b)) + jax.nn.sigmoid(gm) * brm

    gb = goff // D
    merge_ins = [zz, br_mla, (h_all, D, gb), (h_all, D, gb + 1)]
    (mix,) = _rowmap(merge, "merge", T, merge_ins, [], [(D, BF16)])
    out1 = _mm(mix, W["w_out"], "nn", F32, "mm_out")

    def resid_norm2(x, out1, g1, n2, sc2, sh2):
        x1 = x + g1 * out1
        return x1, _normmod(x1, n2, sc2, sh2)

    x1, hm = _rowmap(resid_norm2, "resid_norm2", T, [x, out1], [g1_fwd, n2, sc2, sh2], [(D, F32), (D, BF16)])

    if hooks:
        W = {**W, **hooks["ffn_weights"](hm)}
    FF = W["w_ffn_out"].shape[0]
    assert FF % LANE == 0
    def swiglu_act(a, b):
        return jax.nn.silu(a) * b

    if W["w_ffn_in"].ndim == 3:
        ffn_a, ffn_b, f = _mm_swiglu(hm, W["w_ffn_in"], swiglu_act, "mm_ffn_in")
        ffn_a, ffn_b = (ffn_a, FF, 0), (ffn_b, FF, 0)
    else:
        ab = _mm(hm, W["w_ffn_in"], "nn", BF16, "mm_ffn_in")
        ffn_a, ffn_b = (ab, FF, 0), (ab, FF, 1)
        (f,) = _rowmap(swiglu_act, "ffn_act", T, [ffn_a, ffn_b], [], [(FF, BF16)])
    out2 = _mm(f, W["w_ffn_out"], "nn", F32, "mm_ffn_out")

    def loss_rows(x1, out2, g2, nf, tgt):
        y = _rms(x1 + g2 * out2, nf)
        return 0.5 * jnp.sum(jnp.mean(jnp.square(y - tgt), axis=-1))

    def final(x1, out2, tgt, g2, nf):
        val, (dx1, dout2, dg2, dnf) = jax.value_and_grad(loss_rows, argnums=(0, 1, 2, 3))(x1, out2, g2, nf, tgt)
        return dx1, dout2, jnp.full((1, LANE), val, F32), dg2, dnf

    dx2, dout2, loss_acc, dg2, dnf = _rowmap(final, "final_loss", T, [x1, out2, tgt], [g2, nf],
                                             [(D, F32), (D, BF16)], [LANE, D, D])

    gW = {}
    df = _mm(dout2, W["w_ffn_out"], "nt", BF16, "mm_ffn_out_dx")
    gW["w_ffn_out"] = _dw(f, dout2, W["w_ffn_out"], "mm_ffn_out_dw")

    def swiglu_bwd(a, b, df):
        _, vjp = jax.vjp(swiglu_act, a, b)
        da, db = vjp(df)
        return jnp.concatenate([da, db], axis=1)

    (dab,) = _rowmap(swiglu_bwd, "ffn_act_bwd", T, [ffn_a, ffn_b, df], [], [(2 * FF, BF16)])
    dhm = _mm(dab, W["w_ffn_in"], "nt", F32, "mm_ffn_in_dx")
    gW["w_ffn_in"] = _dw(hm, dab, W["w_ffn_in"], "mm_ffn_in_dw")
    if hooks:
        token = hooks["send_grads"](FFN, [gW.pop(n) for n in FFN])
        g1 = g1 if token is None else g1 + token[:1, :1]

    def resid_norm2_bwd(x, out1, dx2, dhm, g1, n2, sc2, sh2):
        _, vjp = jax.vjp(resid_norm2, x, out1, g1, n2, sc2, sh2)
        dx, dout1, dg1, dn2, dsc2, dsh2 = vjp((dx2, dhm))
        return dx, dout1, dg1, dn2, dsc2, dsh2

    dx1, dout1, dg1, dn2, dsc2, dsh2 = _rowmap(resid_norm2_bwd, "resid_norm2_bwd", T, [x, out1, dx2, dhm],
                                               [g1, n2, sc2, sh2], [(D, F32), (D, BF16)], [D, D, D, D])

    dmix = _mm(dout1, W["w_out"], "nt", BF16, "mm_out_dx")
    gW["w_out"] = _dw(mix, dout1, W["w_out"], "mm_out_dw")

    def merge_bwd(zz, brm, gs, gm, dmix):
        _, vjp = jax.vjp(merge, zz, brm, gs, gm)
        dzz, dbrm, dgs, dgm = vjp(dmix)
        return dzz, dbrm, jnp.concatenate([dgs, dgm], axis=1)

    dzz, dbrm, dgates = _rowmap(merge_bwd, "merge_bwd", T, merge_ins + [dmix], [],
                                [(2 * D, BF16), (D, BF16), (2 * D, BF16)])
    do = _mm(dbrm, W["w_mla_o"], "nt", BF16, "mm_mla_o_dx")
    gW["w_mla_o"] = _dw(o, dbrm, W["w_mla_o"], "mm_mla_o_dw")
    dz = _mm(dzz, W["w_glu"], "nt", BF16, "mm_glu_dx")
    gW["w_glu"] = _dw(z, dzz, W["w_glu"], "mm_glu_dw")
    d_skip_w = p["s5_d"]
    if hooks:
        token = hooks["send_grads"](MIX, [gW.pop(n) for n in MIX])
        d_skip_w = d_skip_w if token is None else d_skip_w + token[:1, :1]

    def s5_combine_bwd(u, y5, dz, dskip):
        _, vjp = jax.vjp(lambda y: jax.nn.gelu(y), y5)
        (dy5,) = vjp(dz)
        return dy5, jnp.sum(dy5 * u, axis=0, keepdims=True)

    dy5, d_skip = _rowmap(s5_combine_bwd, "s5_combine_bwd", T, [(h_all, S5W, 0), y5, dz], [d_skip_w], [(S5W, F32)], [S5W])

    dq_nope, dq_rope, dk_nope, dkr_heads, dv = _attn_bwd(q_nope, q_rope, kvraw, kr, do, lse, H, scale)

    def q_post_bwd(dqn, dqr, cos, sin):
        return jnp.concatenate([dqn, _rope_bwd(dqr, jnp.tile(cos, (1, H)), jnp.tile(sin, (1, H)))], axis=1)

    (dqraw,) = _rowmap(q_post_bwd, "q_rope_bwd", T, [dq_nope, dq_rope, cos_q, sin_q], [], [(2 * hn, BF16)])
    dkvraw = jnp.concatenate([dk_nope, dv], axis=1)

    def k_rope_bwd(dkh, cos, sin):
        d = dkh[:, :LANE]
        for h in range(1, H):
            d = d + dkh[:, h * LANE:(h + 1) * LANE]
        return _rope_bwd(d, cos, sin)

    (dkr,) = _rowmap(k_rope_bwd, "k_rope_bwd", S, [dkr_heads, cos_k, sin_k], [], [(LANE, BF16)])
    dqn = _mm(dqraw, W["w_uq"], "nt", F32, "mm_uq_dx")
    gW["w_uq"] = _dw(qn, dqraw, W["w_uq"], "mm_uq_dw")
    dkvn = _mm(dkvraw, W["w_ukv"], "nt", F32, "mm_ukv_dx")
    gW["w_ukv"] = _dw(kvn, dkvraw, W["w_ukv"], "mm_ukv_dw")

    def rms_bwd(cx, dn, g):
        _, vjp = jax.vjp(_rms, cx, g)
        return vjp(dn)

    dcq, dq_norm = _rowmap(rms_bwd, "q_norm_bwd", T, [(h_all, QR, o_cq // QR), dqn], [p["q_norm"]], [(QR, BF16)], [QR])
    dckv, dkv_norm = _rowmap(rms_bwd, "kv_norm_bwd", S, [(h_all, KVR, o_ckv // KVR), dkvn], [p["kv_norm"]],
                             [(KVR, BF16)], [KVR])

    dy_all = jnp.concatenate([dy5, jnp.zeros((Tc, S5W), F32)], axis=0)
    du0, dbb0, dc0, da0 = _s5_bwd(dy_all, h_all, hs0, waug, vaug, pwc_a[0], S5W, T, 0, True, "s5_scan_bwd0")
    du1, dbb1, dc1, da1 = _s5_bwd(dy_all, h_all, hs1, waug, vaug, pwc_c[1], S5W, T, 1, False, "s5_scan_bwd1")

    def du_combine(a, b, dy, dskip):
        return a + b + dskip * dy

    (du_all,) = _rowmap(du_combine, "s5_du", S, [du0, du1, dy_all], [p["s5_d"]], [(S5W, BF16)])
    dbb = jnp.einsum("dsgpcgn->dcsgpn", jnp.stack([dbb0, dbb1]).reshape(2, nch, 8, P, 2, 8, N)).reshape(2, 2, G, P, N)
    dcm = jnp.einsum("dscgngp->dcsgpn", jnp.stack([dc0, dc1]).reshape(2, nch, 2, 8, N, 8, P)).reshape(2, 2, G, P, N)
    da = jnp.stack([da0, da1]).reshape(2, nch, 2, 8, N).transpose(0, 2, 1, 3, 4).reshape(2, 2, G, 1, N)
    d_lr, d_li, d_ldt, d_br, d_bi = _s5_param_bwd(a_re, a_im, ldt, b_re, b_im, da[:, 0], da[:, 1], dbb[:, 0], dbb[:, 1])

    lat_only = lambda t: jnp.pad(t, ((0, Tc), (0, 0)))
    dh_all = jnp.concatenate([du_all, lat_only(dcq), dckv, dkr, jnp.zeros((S, goff - o_kr - LANE), BF16), lat_only(dgates)],
                             axis=1)
    dxm = _mm(dh_all, W["w_in"], "nt", F32, "mm_in_dx")
    gW["w_in"] = _dw(xm_all, dh_all, W["w_in"], "mm_in_dw")

    def norm1_bwd(x, dxm, dx1, n1, sc, sh):
        _, vjp = jax.vjp(_normmod, x, n1, sc, sh)
        dx, dn, dsc, dsh = vjp(dxm)
        return dx + dx1, dn, dsc, dsh

    grad_x, dn1_l, dsc1, dsh1 = _rowmap(norm1_bwd, "norm1_lat_bwd", T, [x, dxm, dx1], [n1, sc1, sh1], [(D, F32)], [D, D, D])

    def norm1_ctx_bwd(x, dxm, n1, sc, sh):
        _, vjp = jax.vjp(_normmod, x, n1, sc, sh)
        return vjp(dxm)[1:]

    dn1_c, dcsc1, dcsh1 = _rowmap(norm1_ctx_bwd, "norm1_ctx_bwd", Tc, [ctx, dxm[T:]], [n1, csc1, csh1], [], [D, D, D])

    zero = jnp.zeros((1, D), F32)
    dm_lat = jnp.concatenate([dsh1, dsc1, dg1, dsh2, dsc2, dg2], axis=0)
    dm_ctx = jnp.concatenate([dcsh1, dcsc1, zero, zero, zero, zero], axis=0)
    small = {
        "norm1": dn1_l + dn1_c, "norm2": dn2, "norm_f": dnf, "q_norm": dq_norm, "kv_norm": dkv_norm, "s5_d": d_skip,
        "s5_a_re": d_lr, "s5_a_im": d_li, "s5_log_dt": d_ldt, "s5_b_re": d_br.transpose(0, 1, 3, 2),
        "s5_b_im": d_bi.transpose(0, 1, 3, 2), "s5_c_re": dcm[:, 0], "s5_c_im": -dcm[:, 1],
    }
    return loss_acc[:, :1], grad_x, small, dm_lat, dm_ctx, gW


BIG = ("w_in", "w_uq", "w_ukv", "w_glu", "w_mla_o", "w_out", "w_ffn_in", "w_ffn_out")
FFN = ("w_ffn_in", "w_ffn_out")
MIX = ("w_out", "w_mla_o", "w_glu")
ROW_SHARDED = ("w_out", "w_ffn_out")
RELAID = ("w_in", "w_uq", "w_ukv")
SMALL = ("c_ctx", "b_mod", "norm1", "norm2", "s5_a_re", "s5_a_im", "s5_log_dt", "s5_b_re", "s5_b_im", "s5_c_re",
         "s5_c_im", "s5_d", "q_norm", "kv_norm", "norm_f")
S5_BULK = ("s5_b_re", "s5_b_im", "s5_c_re", "s5_c_im")
WEIGHTS = ("c_ctx", "w_mod", "b_mod", "norm1", "norm2", "w_in", "s5_a_re", "s5_a_im", "s5_log_dt", "s5_b_re", "s5_b_im",
           "s5_c_re", "s5_c_im", "s5_d", "w_glu", "q_norm", "kv_norm", "w_uq", "w_ukv", "w_mla_o", "w_out", "w_ffn_in",
           "w_ffn_out", "norm_f")


def _heads_split(w, heads, first):
    k = w.shape[0]
    w3 = w.reshape(k, heads, -1)
    return jnp.concatenate([w3[:, :, :first].reshape(k, -1), w3[:, :, first:].reshape(k, -1)], axis=1)


def _uq_layout(w, heads):
    k = w.shape[0]
    w3 = w.reshape(k, heads, QK_NOPE + QK_ROPE)
    rope = jnp.pad(w3[:, :, QK_NOPE:], ((0, 0), (0, 0), (0, LANE - QK_ROPE)))
    return jnp.concatenate([w3[:, :, :QK_NOPE].reshape(k, -1), rope.reshape(k, -1)], axis=1)


def _uq_unlayout(w, heads):
    k = w.shape[0]
    nope = w[:, :heads * QK_NOPE].reshape(k, heads, QK_NOPE)
    rope = w[:, heads * QK_NOPE:].reshape(k, heads, LANE)[:, :, :QK_ROPE]
    return jnp.concatenate([nope, rope], axis=2).reshape(k, -1)


def _heads_merge(w, heads, first):
    k = w.shape[0]
    a, b = w[:, :heads * first].reshape(k, heads, first), w[:, heads * first:].reshape(k, heads, -1)
    return jnp.concatenate([a, b], axis=2).reshape(k, -1)


def _cols_full(w8):
    return w8.transpose(1, 0, 2).reshape(w8.shape[1], -1)


def _cols_slots(w):
    return w.reshape(w.shape[0], N_DEV, -1).transpose(1, 0, 2)


def _weight_layout(n, w8):
    if n in ROW_SHARDED:
        return w8.reshape(-1, w8.shape[-1])
    return _cols_full(w8) if (n in RELAID or w8.shape[-1] % LANE) else w8


def _grad_slots(n, g):
    if g.ndim == 3:
        return g
    return g.reshape(N_DEV, g.shape[0] // N_DEV, g.shape[1]) if n in ROW_SHARDED else _cols_slots(g)


def _gate_offset(in_cols, D):
    return -(-(in_cols - 2 * D) // D) * D


def _model_weights(g8, D):
    W = {n: _weight_layout(n, w8) for n, w8 in g8.items()}
    w_in = W["w_in"]
    n_front = w_in.shape[1] - 2 * D
    goff = _gate_offset(w_in.shape[1], D)
    W["w_in"] = jnp.concatenate([w_in[:, :n_front], jnp.zeros((D, goff - n_front), w_in.dtype), w_in[:, n_front:]], axis=1)
    heads = W["w_uq"].shape[1] // (QK_NOPE + QK_ROPE)
    W["w_uq"] = _uq_layout(W["w_uq"], heads)
    W["w_ukv"] = _heads_split(W["w_ukv"], heads, QK_NOPE)
    return W, goff


def kernel(x, c, ctx, c_ctx, w_mod, b_mod, norm1, norm2, w_in, s5_a_re, s5_a_im, s5_log_dt, s5_b_re, s5_b_im, s5_c_re, s5_c_im, s5_d, w_glu, q_norm, kv_norm, w_uq, w_ukv, w_mla_o, w_out, w_ffn_in, w_ffn_out, norm_f, loss_target, m_c_ctx, m_w_mod, m_b_mod, m_norm1, m_norm2, m_w_in, m_s5_a_re, m_s5_a_im, m_s5_log_dt, m_s5_b_re, m_s5_b_im, m_s5_c_re, m_s5_c_im, m_s5_d, m_w_glu, m_q_norm, m_kv_norm, m_w_uq, m_w_ukv, m_w_mla_o, m_w_out, m_w_ffn_in, m_w_ffn_out, m_norm_f, v_c_ctx, v_w_mod, v_b_mod, v_norm1, v_norm2, v_w_in, v_s5_a_re, v_s5_a_im, v_s5_log_dt, v_s5_b_re, v_s5_b_im, v_s5_c_re, v_s5_c_im, v_s5_d, v_w_glu, v_q_norm, v_kv_norm, v_w_uq, v_w_ukv, v_w_mla_o, v_w_out, v_w_ffn_in, v_w_ffn_out, v_norm_f):
    a = dict(locals())
    D = x.shape[-1]
    me = 4 * lax.axis_index("x") + 2 * lax.axis_index("y") + lax.axis_index("c")

    shard = {n: a[n][0] for n in BIG}
    first = [n for n in BIG if n not in FFN + MIX]
    (cg,) = _all_gather([jnp.broadcast_to(c, (8, D))], "ag_c")
    goff = _gate_offset(w_in.shape[-1] * N_DEV, D)

    wm = w_mod[0]
    ncol = wm.shape[1]
    c16 = jnp.concatenate([cg[:, 0, :], c_ctx[None], jnp.zeros((7, D), F32)], axis=0)
    (s16,) = _rowmap(jax.nn.silu, "mod_silu", 16, [c16], [], [(D, BF16)])
    m_cols = _mm(s16, wm, "nn", F32, "mm_mod")
    (mg,) = _all_gather([m_cols], "ag_mod")
    (m16,) = _rowmap(lambda m, b: m + b, "mod_bias", 16, [_cols_full(mg)], [b_mod], [(N_DEV * ncol, F32)])

    first_blocks = [shard[n].astype(BF16) for n in first]
    fst = {}
    fst["sems1"], fst["thru"], first_token = _ag2_start(first_blocks, [_own_slot(b, me) for b in first_blocks], "ag_first_start")

    def first_weights(after_norm, after_tables):
        sems2, thru, _ = _ag2_mid(fst["sems1"], fst["thru"], after_tables, "ag_first_mid")
        lands = _ag2_end(fst["sems1"], sems2, thru, after_norm, "ag_first_end")
        return _model_weights(dict(zip(first, lands)), D)[0]

    mix_blocks = [shard[n].astype(BF16) for n in MIX]
    mix = _xchg_start(mix_blocks, [_own_slot(b, me) for b in mix_blocks], False, "ag_mix_start")
    ffn_blocks = [shard[n].astype(BF16) for n in FFN]
    ffn = {}
    ffn["sems1"], ffn["thru"], ag_token = _ag2_start(ffn_blocks, [_own_slot(b, me) for b in ffn_blocks], "ag_ffn_start")
    m16 = m16 + (first_token[:1, :1] + mix[3][:1, :1] + ag_token[:1, :1])

    def mix_weights(after):
        lands = _xchg_wait(mix[0], mix[1], mix[2], after, False, "ag_mix_wait")
        return {n: _weight_layout(n, w8) for n, w8 in zip(MIX, lands)}

    def ffn_mid(after):
        ffn["sems2"], ffn["thru"], token = _ag2_mid(ffn["sems1"], ffn["thru"], after, "ag_ffn_mid")
        return token

    def ffn_weights(after):
        lands = _ag2_end(ffn["sems1"], ffn["sems2"], ffn["thru"], after, "ag_ffn_end")
        return {n: _weight_layout(n, w8) for n, w8 in zip(FFN, lands)}

    rs_async = {}

    def send_grads(names, gs):
        slots = [_grad_slots(n, g) for n, g in zip(names, gs)]
        lands = [_own_slot(lax.dynamic_index_in_dim(s, me, 0, keepdims=False), me) for s in slots]
        rs_async[names] = _xchg_start(slots, lands, True, "rs_start_" + names[0])
        return rs_async[names][3]

    m_lat = lax.dynamic_slice(m16, (me, 0), (1, 6 * D)).reshape(6, D)
    m_ctx = m16[8].reshape(6, D)

    p = {n: a[n][0] for n in ("norm1", "norm2", "s5_a_re", "s5_a_im", "s5_log_dt", "s5_b_re", "s5_b_im", "s5_c_re",
                              "s5_c_im", "q_norm", "kv_norm")}
    p = {k: (v[None] if v.ndim == 1 else v) for k, v in p.items()}
    p["s5_d"] = s5_d.reshape(1, -1)
    p["norm_f"] = norm_f[None]
    hooks = dict(first_weights=first_weights, mix_weights=mix_weights, ffn_mid=ffn_mid, ffn_weights=ffn_weights,
                 send_grads=send_grads)
    loss_part, grad_x, small, dm_lat, dm_ctx, gW = _local_step(x[0], ctx[0], loss_target[0], m_lat, m_ctx, p, {}, goff, hooks)
    loss = lax.psum(loss_part[0, 0], ("x", "y", "c"))

    gW = dict(gW)
    n_front = w_in.shape[-1] * N_DEV - 2 * D
    gW["w_in"] = jnp.concatenate([gW["w_in"][:, :n_front], gW["w_in"][:, goff:]], axis=1)
    heads = gW["w_uq"].shape[1] // (2 * LANE)
    gW["w_uq"] = _uq_unlayout(gW["w_uq"], heads)
    gW["w_ukv"] = _heads_merge(gW["w_ukv"], heads, QK_NOPE)
    last = [n for n in BIG if n in gW]
    slots = [_grad_slots(n, gW[n]) for n in last]
    from_sibling = _rs_pair(slots, "rs_pair")
    chip_sums = [_add_pair(pp, rr, "rs_add_" + n, grad_x) for n, pp, rr in zip(last, slots, from_sibling)]

    dm8 = jnp.concatenate([dm_lat.reshape(1, -1), dm_ctx.reshape(1, -1), jnp.zeros((SUB - 2, 6 * D), F32)], axis=0)
    (dmg,) = _all_gather([dm8], "ag_dmod", after=chip_sums[0])
    dm_sum = _sum_slots(dmg, "sum_dmod")
    dM16 = jnp.concatenate([dmg[:, 0, :], dm_sum[1:2], jnp.zeros((7, 6 * D), F32)], axis=0)
    (g_b_mod,) = _rowmap(lambda d: jnp.sum(d, axis=0, keepdims=True), "b_mod_grad", 16, [dM16], [], [], [6 * D])
    dM_loc = lax.dynamic_slice(dM16, (0, me * ncol), (16, ncol))
    g_w_mod = _mm(s16, dM_loc, "tn", F32, "mm_mod_dw")
    ds16_part = _mm(dM_loc, wm, "nt", F32, "mm_mod_dx")

    fine = [n for n in SMALL if n not in ("c_ctx", "b_mod") + S5_BULK]
    small_blocks = [_pack_rows([small[n] for n in fine] + [ds16_part[8:9]], F32), _pack_rows([small[n] for n in S5_BULK], BF16)]
    sm_sems1, sm_thru, sm_token = _ag2_start(small_blocks, [_own_slot(b, me) for b in small_blocks], "ag_small_start")
    grads = {"b_mod": g_b_mod}

    def small_grads(after):
        sems2, thru, token = _ag2_mid(sm_sems1, sm_thru, after, "ag_small_mid")
        sg, sgb = _ag2_end(sm_sems1, sems2, thru, token, "ag_small_end")
        parts = _unpack_rows(_sum_slots(sg, "sum_small"), [small[n].shape for n in fine] + [(1, D)])
        grads.update(zip(fine, parts[:-1]))
        grads.update(zip(S5_BULK, _unpack_rows(_sum_slots(sgb, "sum_small_bulk"), [small[n].shape for n in S5_BULK])))

        def silu_bwd(cc, ds):
            _, vjp = jax.vjp(jax.nn.silu, cc)
            return vjp(ds)[0]

        (grads["c_ctx"],) = _rowmap(silu_bwd, "c_ctx_grad", 1, [c_ctx[None], parts[-1]], [], [(D, F32)])


    my_chip = 2 * lax.axis_index("x") + lax.axis_index("y")
    lands = [_own_slot(lax.dynamic_index_in_dim(q, my_chip, 0, keepdims=False) + sm_token[:1, :1].astype(q.dtype), my_chip, N_CHIP)
             for q in chip_sums]
    rs_send, rs_recv, rs_thru, behind = _xchg_start(chip_sums, lands, True, "rs_chips_start")
    partials = {}
    for names, (send, recv, thru, _) in rs_async.items():
        partials.update(zip(names, _xchg_wait(send, recv, thru, behind, True, "rs_wait_" + names[0])))

    out = {}

    def adamw_big(n, after):
        w2, m2, v2 = a[n][0], a["m_" + n][0], a["v_" + n][0]
        if n in partials:
            g, d, nm, nv = _adamw_slots(w2, partials[n], m2, v2, "adamw_" + n, after)
        else:
            g = g_w_mod
            d, nm, nv = _adamw(w2, g, m2, v2, "adamw_" + n, after)
        for k, val in (("grad_", g), ("delta_", d), ("new_m_", nm), ("new_v_", nv)):
            out[k + n] = val.reshape(a[n].shape)
        return nv

    for n in FFN + MIX + ("w_mod",):
        behind = adamw_big(n, behind)
    small_grads(behind)
    packs = [_pack_rows([t[n] for n in SMALL], F32) for t in (
        {n: a[n] for n in SMALL}, {n: grads[n] for n in SMALL}, {n: a["m_" + n] for n in SMALL}, {n: a["v_" + n] for n in SMALL})]
    res = _adamw(*packs, "adamw_small")
    partials.update(zip(last, _xchg_wait(rs_send, rs_recv, rs_thru, res[2], True, "rs_chips_wait")))
    for n in last:
        adamw_big(n, None)
    shapes = [a[n].shape for n in SMALL]
    for k, packed in (("grad_", packs[1]), ("delta_", res[0]), ("new_m_", res[1]), ("new_v_", res[2])):
        for n, val in zip(SMALL, _unpack_rows(packed, shapes)):
            out[k + n] = val
    return (loss, grad_x[None]) + tuple(out[k + n] for k in ("grad_", "delta_", "new_m_", "new_v_") for n in WEIGHTS)
```

```python
import functools
import math

import jax
import jax.numpy as jnp
from jax import lax
from jax.experimental import pallas as pl
from jax.experimental.pallas import tpu as pltpu

F32 = jnp.float32
BF16 = jnp.bfloat16

N_DEV = 8
N_CHIP = 4
EPS = 1e-6
GRID_W = 64
S5_GROUP = 16
QK_NOPE, QK_ROPE, V_DIM = 128, 64, 128
ROPE_BASE = 10000.0
ADAM_LR, ADAM_B1, ADAM_B2, ADAM_EPS, ADAM_WD, ADAM_STEP = 0.001, 0.9, 0.999, 1e-08, 0.01, 10

LANE = 128
SUB = 8
PACK_W = 1024
PACK_ROWS = 32
VMEM_LIMIT = 48 << 20
ROWMAP_TILE_BYTES = 28 << 20
MM_VMEM_BUDGET = 36 << 20
MESH = pl.DeviceIdType.MESH
_NT = (((1,), (1,)), ((), ()))
_TN = (((0,), (0,)), ((), ()))


def _pick(dim, cands):
    for c in cands:
        if dim % c == 0:
            return c
    return dim


def _cparams(sem):
    return pltpu.CompilerParams(dimension_semantics=sem, vmem_limit_bytes=VMEM_LIMIT)


def _mm(a, b, dims, out_dtype, name, out_slots=None):
    a = a.astype(BF16)
    b = b.astype(BF16)
    b3 = b.ndim == 3
    if dims == "nn":
        (M, K), N = a.shape, (b.shape[0] * b.shape[2] if b3 else b.shape[1])
    elif dims == "nt":
        M, N = a.shape[0], b.shape[-2]
        K = b.shape[0] * b.shape[2] if b3 else b.shape[1]
    else:
        (K, M), N = a.shape, b.shape[1]
    unit_n = b.shape[2] if (b3 and dims == "nn") else (N // out_slots if out_slots else N)
    unit_k = b.shape[2] if (b3 and dims == "nt") else K
    osz = jnp.dtype(out_dtype).itemsize
    tm, tn, tk = _mm_tiles(M, unit_n, unit_k, osz, LANE if dims == "tn" else 16)
    nk, npt, kpt = K // tk, unit_n // tn, unit_k // tk
    use_acc = nk > 1 and out_dtype != F32
    if dims == "nn":
        a_spec = pl.BlockSpec((tm, tk), lambda i, j, k: (i, k))
        b_spec = (pl.BlockSpec((None, tk, tn), lambda i, j, k: (j // npt, k, j % npt)) if b3
                  else pl.BlockSpec((tk, tn), lambda i, j, k: (k, j)))
        dn = (((1,), (0,)), ((), ()))
    elif dims == "nt":
        a_spec = pl.BlockSpec((tm, tk), lambda i, j, k: (i, k))
        b_spec = (pl.BlockSpec((None, tn, tk), lambda i, j, k: (k // kpt, j, k % kpt)) if b3
                  else pl.BlockSpec((tn, tk), lambda i, j, k: (j, k)))
        dn = _NT
    else:
        a_spec = pl.BlockSpec((tk, tm), lambda i, j, k: (k, i))
        b_spec = pl.BlockSpec((tk, tn), lambda i, j, k: (k, j))
        dn = _TN
    if out_slots:
        out_spec = pl.BlockSpec((None, tm, tn), lambda i, j, k: (j // npt, i, j % npt))
        out_shape = jax.ShapeDtypeStruct((out_slots, M, unit_n), out_dtype)
    else:
        out_spec = pl.BlockSpec((tm, tn), lambda i, j, k: (i, j))
        out_shape = jax.ShapeDtypeStruct((M, N), out_dtype)

    def body(a_ref, b_ref, o_ref, *scratch):
        part = lax.dot_general(a_ref[...], b_ref[...], dn, preferred_element_type=F32)
        if nk == 1:
            o_ref[...] = part.astype(o_ref.dtype)
            return
        acc_ref = scratch[0] if use_acc else o_ref
        k = pl.program_id(2)

        @pl.when(k == 0)
        def _():
            acc_ref[...] = part

        @pl.when(k > 0)
        def _():
            acc_ref[...] += part

        if use_acc:
            @pl.when(k == nk - 1)
            def _():
                o_ref[...] = acc_ref[...].astype(o_ref.dtype)

    return pl.pallas_call(
        body, name=name, grid=(M // tm, N // tn, nk),
        in_specs=[a_spec, b_spec], out_specs=out_spec, out_shape=out_shape,
        scratch_shapes=[pltpu.VMEM((tm, tn), F32)] if use_acc else [],
        compiler_params=_cparams(("parallel", "parallel", "arbitrary")),
    )(a, b)


def _mm_swiglu(x, w3, act, name):
    x = x.astype(BF16)
    M, K = x.shape
    ns, _, n = w3.shape
    half = ns // 2
    tm = _divisors(M, 16, 256)[0]

    def body(x_ref, wa_ref, wb_ref, a_ref, b_ref, f_ref):
        a = jnp.dot(x_ref[...], wa_ref[...], preferred_element_type=F32)
        b = jnp.dot(x_ref[...], wb_ref[...], preferred_element_type=F32)
        a_ref[...] = a.astype(a_ref.dtype)
        b_ref[...] = b.astype(b_ref.dtype)
        f_ref[...] = act(a, b).astype(f_ref.dtype)

    out = pl.BlockSpec((tm, n), lambda s, i: (i, s))
    return pl.pallas_call(
        body, name=name, grid=(half, M // tm),
        in_specs=[pl.BlockSpec((tm, K), lambda s, i: (i, 0)), pl.BlockSpec((None, K, n), lambda s, i: (s, 0, 0)),
                  pl.BlockSpec((None, K, n), lambda s, i: (s + half, 0, 0))],
        out_specs=[out] * 3, out_shape=[jax.ShapeDtypeStruct((M, half * n), BF16)] * 3,
        compiler_params=_cparams(("parallel", "parallel")),
    )(x, w3, w3)


def _divisors(n, mult, cap):
    d = [t for t in range(mult, min(n, cap) + 1, mult) if n % t == 0]
    return d[::-1] or [n]


def _mm_tiles(M, unit_n, unit_k, out_itemsize, tm_mult):
    best = None
    for tk in _divisors(unit_k, LANE, 2816):
        for tn in _divisors(unit_n, LANE, 1536):
            for tm in _divisors(M, tm_mult, 1024):
                vmem = 2 * 2 * (tm * tk + tk * tn) + 2 * tm * tn * out_itemsize + 4 * tm * tn * (2 if unit_k > tk else 1)
                if vmem > MM_VMEM_BUDGET:
                    continue
                steps = (M // tm) * (unit_n // tn) * (unit_k // tk)
                key = (steps, -tk, -tn)
                if best is None or key < best[0]:
                    best = (key, (tm, tn, tk))
                break
    return best[1]


def _rowmap(fn, name, M, row_ins, bc_ins, row_outs, acc_outs=(), after=None):
    row_ins = [r if isinstance(r, tuple) else (r, r.shape[1], 0) for r in row_ins]
    row_bytes = sum(w * a.dtype.itemsize for a, w, _ in row_ins) + sum(w * jnp.dtype(d).itemsize for w, d in row_outs)
    widest = max([w for _, w, _ in row_ins] + [w for w, _ in row_outs])
    row_bytes = 2 * row_bytes + 6 * 4 * widest
    tm = _pick(M, [t for t in (512, 256, 128, 64, 32, 16) if t * row_bytes <= ROWMAP_TILE_BYTES] + [16])
    n_in, n_row, n_acc = len(row_ins) + len(bc_ins), len(row_outs), len(acc_outs)

    def body(*refs):
        res = fn(*[r[...].astype(F32) for r in refs[:n_in]])
        res = res if isinstance(res, (tuple, list)) else (res,)
        outs = refs[n_in + (after is not None):]
        for k in range(n_row):
            outs[k][...] = res[k].astype(outs[k].dtype)
        if n_acc:
            @pl.when(pl.program_id(0) == 0)
            def _():
                for k in range(n_acc):
                    outs[n_row + k][...] = jnp.zeros_like(outs[n_row + k])

            for k in range(n_acc):
                outs[n_row + k][...] += res[n_row + k].astype(F32)

    in_specs = [pl.BlockSpec((tm, w), functools.partial(lambda i, blk: (i, blk), blk=blk)) for _, w, blk in row_ins]
    in_specs += [pl.BlockSpec(b.shape, lambda i: (0, 0)) for b in bc_ins]
    in_specs += [pl.BlockSpec(memory_space=pl.ANY)] * (after is not None)
    out_specs = [pl.BlockSpec((tm, w), lambda i: (i, 0)) for w, _ in row_outs]
    out_specs += [pl.BlockSpec((1, w), lambda i: (0, 0)) for w in acc_outs]
    out_shape = [jax.ShapeDtypeStruct((M, w), d) for w, d in row_outs]
    out_shape += [jax.ShapeDtypeStruct((1, w), F32) for w in acc_outs]
    return pl.pallas_call(
        body, name=name, grid=(M // tm,), in_specs=in_specs, out_specs=out_specs, out_shape=out_shape,
        compiler_params=_cparams(("arbitrary",) if n_acc else ("parallel",)),
    )(*[a for a, _, _ in row_ins], *bc_ins, *([] if after is None else [after]))


def _rms(x, g):
    return x * lax.rsqrt(jnp.mean(x * x, axis=-1, keepdims=True) + EPS) * g


def _normmod(x, g, sc, sh):
    return _rms(x, g) * (1.0 + sc) + sh


def _swap16(v):
    w = v.shape[1]
    lane = lax.broadcasted_iota(jnp.int32, v.shape, 1)
    return jnp.where((lane // 16) % 2 == 0, pltpu.roll(v, w - 16, 1), pltpu.roll(v, 16, 1))


def _rope(v, cos, sin_signed):
    return v * cos + _swap16(v) * sin_signed


def _rope_bwd(d, cos, sin_signed):
    return d * cos + _swap16(d * sin_signed)


def _mesh_pos():
    return lax.axis_index("x"), lax.axis_index("y"), lax.axis_index("c")


def _hbm_call(body, name, ins, out_shapes, n_sems):
    any_spec = pl.BlockSpec(memory_space=pl.ANY)
    return pl.pallas_call(
        body, name=name, out_shape=out_shapes, in_specs=[any_spec] * len(ins), out_specs=[any_spec] * len(out_shapes),
        scratch_shapes=[pltpu.SemaphoreType.DMA((n_sems,)), pltpu.SemaphoreType.DMA((n_sems,)),
                        pltpu.SemaphoreType.DMA((len(ins),))],
    )(*ins)


def _all_gather(xs, name, after=None):
    n = len(xs)

    def body(*refs):
        k = n + (after is not None)
        x_refs, out_refs, (send_sems, recv_sems, local_sems) = refs[:n], refs[k:k + n], refs[k + n:]
        x, y, c = _mesh_pos()
        me, sibling = (x, y, c), (x, y, 1 - c)
        chips = [(1 - x, y), (x, 1 - y), (1 - x, 1 - y)]
        locals_, first, passed, arrivals = [], [], [], []
        for a in range(n):
            def slot(px, py, pc, a=a):
                return out_refs[a].at[4 * px + 2 * py + pc]

            def copy(k, block, to, src=None, a=a, slot=slot):
                return pltpu.make_async_remote_copy(
                    src_ref=slot(*block) if src is None else src, dst_ref=slot(*block),
                    send_sem=send_sems.at[7 * a + k], recv_sem=recv_sems.at[7 * a + k], device_id=to, device_id_type=MESH)

            locals_.append(pltpu.make_async_copy(x_refs[a], slot(*me), local_sems.at[a]))
            first.append(copy(0, me, sibling, src=x_refs[a]))
            first += [copy(1 + j, me, (*chip, c), src=x_refs[a]) for j, chip in enumerate(chips)]
            passed.append([copy(4 + j, (*chip, c), sibling) for j, chip in enumerate(chips)])
            arrivals.append([copy(1 + j, (*chip, c), me) for j, chip in enumerate(chips)]
                            + [copy(0, sibling, me)] + [copy(4 + j, (*chip, 1 - c), me) for j, chip in enumerate(chips)])
        for cp in locals_ + first:
            cp.start()
        for j in range(3):
            for a in range(n):
                arrivals[a][j].wait_recv()
                passed[a][j].start()
        for a in range(n):
            for cp in arrivals[a][3:]:
                cp.wait_recv()
        for cp in first + [p for ps in passed for p in ps]:
            cp.wait_send()
        for cp in locals_:
            cp.wait()

    return _hbm_call(body, name, list(xs) + ([] if after is None else [after]),
                     [jax.ShapeDtypeStruct((N_DEV,) + x.shape, x.dtype) for x in xs], 7 * n)


def _rs_pair(ps, name):
    n = len(ps)

    def body(*refs):
        p_refs, out_refs, (send_sems, recv_sems, _) = refs[:n], refs[n:2 * n], refs[2 * n:]
        x, y, c = _mesh_pos()
        sends, recvs = [], []
        for a in range(n):
            for q in range(N_CHIP):
                sem = dict(send_sem=send_sems.at[4 * a + q], recv_sem=recv_sems.at[4 * a + q],
                           device_id=(x, y, 1 - c), device_id_type=MESH)
                sends.append(pltpu.make_async_remote_copy(src_ref=p_refs[a].at[2 * q + 1 - c], dst_ref=out_refs[a].at[q], **sem))
                recvs.append(pltpu.make_async_remote_copy(src_ref=p_refs[a].at[2 * q + c], dst_ref=out_refs[a].at[q], **sem))
        for cp in sends:
            cp.start()
        for cp in recvs:
            cp.wait_recv()
        for cp in sends:
            cp.wait_send()

    return _hbm_call(body, name, ps, [jax.ShapeDtypeStruct((N_CHIP,) + p.shape[1:], p.dtype) for p in ps], 4 * n)


def _xchg_copies(src_refs, land_refs, send_sems, recv_sems, slot_src):
    x, y, c = _mesh_pos()
    sends, recvs = [], []
    for a, (src, land) in enumerate(zip(src_refs, land_refs)):
        chips = land.shape[0] == N_CHIP
        npeer = land.shape[0] - 1
        me = 2 * x + y if chips else 4 * x + 2 * y + c
        for r in range(1, npeer + 1):
            px = 1 - x if r & (2 if chips else 4) else x
            py = 1 - y if r & (1 if chips else 2) else y
            pc = c if chips else (1 - c if r & 1 else c)
            peer = 2 * px + py if chips else 4 * px + 2 * py + pc
            sem = dict(send_sem=send_sems.at[npeer * a + r - 1], recv_sem=recv_sems.at[npeer * a + r - 1],
                       device_id=(px, py, pc), device_id_type=MESH)
            s = src.at[peer] if slot_src else src
            sends.append(pltpu.make_async_remote_copy(src_ref=s, dst_ref=land.at[me], **sem))
            recvs.append(pltpu.make_async_remote_copy(src_ref=s, dst_ref=land.at[peer], **sem))
    return sends, recvs


_HBM = pl.BlockSpec(memory_space=pltpu.HBM)
_SEM = pl.BlockSpec(memory_space=pltpu.SEMAPHORE)
_EFFECT = pltpu.SideEffectType.DATAFLOW_SIDE_EFFECTING


def _xchg_start(srcs, lands, slot_src, name):
    n = len(srcs)

    def body(*refs):
        sends, _ = _xchg_copies(refs[:n], refs[n:2 * n], refs[2 * n], refs[2 * n + 1], slot_src)
        for cp in sends:
            cp.start()
        refs[-1][...] = jnp.zeros_like(refs[-1])

    bufs = list(srcs) + list(lands)
    n_sems = n * (lands[0].shape[0] - 1)
    res = pl.pallas_call(
        body, name=name,
        out_shape=(pltpu.SemaphoreType.DMA((n_sems,)), pltpu.SemaphoreType.DMA((n_sems,)))
        + tuple(pltpu.HBM(b.shape, b.dtype) for b in bufs) + (jax.ShapeDtypeStruct((SUB, LANE), F32),),
        in_specs=(_HBM,) * (2 * n), out_specs=(_SEM, _SEM) + (_HBM,) * (2 * n) + (pl.BlockSpec(memory_space=pltpu.VMEM),),
        input_output_aliases={i: 2 + i for i in range(2 * n)},
        compiler_params=pltpu.CompilerParams(has_side_effects=_EFFECT),
    )(*[pltpu.with_memory_space_constraint(b, pltpu.HBM) for b in bufs])
    return res[0], res[1], res[2:-1], res[-1]


def _xchg_wait(send_sems, recv_sems, thru, after, slot_src, name):
    n = len(thru) // 2

    def body(*refs):
        sends, recvs = _xchg_copies(refs[:n], refs[n:2 * n], refs[2 * n], refs[2 * n + 1], slot_src)
        for cp in sends:
            cp.wait_send()
        for cp in recvs:
            cp.wait_recv()

    res = pl.pallas_call(
        body, name=name, out_shape=tuple(pltpu.HBM(b.shape, b.dtype) for b in thru),
        in_specs=(_HBM,) * (2 * n) + (_SEM, _SEM, pl.BlockSpec(memory_space=pl.ANY)), out_specs=(_HBM,) * (2 * n),
        input_output_aliases={i: i for i in range(2 * n)},
        compiler_params=pltpu.CompilerParams(has_side_effects=_EFFECT),
    )(*thru, send_sems, recv_sems, after)
    return res[n:]


def _ag2_copy(land, sems, k, block, to, src=None):
    slot = land.at[4 * block[0] + 2 * block[1] + block[2]]
    return pltpu.make_async_remote_copy(src_ref=slot if src is None else src, dst_ref=slot, send_sem=sems[0].at[k],
                                        recv_sem=sems[1].at[k], device_id=to, device_id_type=MESH)


def _ag2_start(blocks, lands, name):
    n = len(blocks)

    def body(*refs):
        x, y, c = _mesh_pos()
        for a in range(n):
            sems = (refs[2 * n], refs[2 * n + 1])
            _ag2_copy(refs[n + a], sems, 4 * a, (x, y, c), (x, y, 1 - c), src=refs[a]).start()
            for j, chip in enumerate([(1 - x, y), (x, 1 - y), (1 - x, 1 - y)]):
                _ag2_copy(refs[n + a], sems, 4 * a + 1 + j, (x, y, c), (*chip, c), src=refs[a]).start()
        refs[-1][...] = jnp.zeros_like(refs[-1])

    bufs = list(blocks) + list(lands)
    res = pl.pallas_call(
        body, name=name,
        out_shape=(pltpu.SemaphoreType.DMA((4 * n,)), pltpu.SemaphoreType.DMA((4 * n,)))
        + tuple(pltpu.HBM(b.shape, b.dtype) for b in bufs) + (jax.ShapeDtypeStruct((SUB, LANE), F32),),
        in_specs=(_HBM,) * (2 * n), out_specs=(_SEM, _SEM) + (_HBM,) * (2 * n) + (pl.BlockSpec(memory_space=pltpu.VMEM),),
        input_output_aliases={i: 2 + i for i in range(2 * n)},
        compiler_params=pltpu.CompilerParams(has_side_effects=_EFFECT),
    )(*[pltpu.with_memory_space_constraint(b, pltpu.HBM) for b in bufs])
    return (res[0], res[1]), res[2:-1], res[-1]


def _ag2_mid(sems1, thru, after, name):
    n = len(thru) // 2

    def body(*refs):
        x, y, c = _mesh_pos()
        s1, s2 = (refs[2 * n], refs[2 * n + 1]), (refs[2 * n + 3], refs[2 * n + 4])
        for j, chip in enumerate([(1 - x, y), (x, 1 - y), (1 - x, 1 - y)]):
            for a in range(n):
                _ag2_copy(refs[n + a], s1, 4 * a + 1 + j, (*chip, c), (x, y, c)).wait_recv()
                _ag2_copy(refs[n + a], s2, 3 * a + j, (*chip, c), (x, y, 1 - c)).start()
        refs[-1][...] = jnp.zeros_like(refs[-1])

    res = pl.pallas_call(
        body, name=name,
        out_shape=(pltpu.SemaphoreType.DMA((3 * n,)), pltpu.SemaphoreType.DMA((3 * n,)))
        + tuple(pltpu.HBM(b.shape, b.dtype) for b in thru) + (jax.ShapeDtypeStruct((SUB, LANE), F32),),
        in_specs=(_HBM,) * (2 * n) + (_SEM, _SEM, pl.BlockSpec(memory_space=pl.ANY)),
        out_specs=(_SEM, _SEM) + (_HBM,) * (2 * n) + (pl.BlockSpec(memory_space=pltpu.VMEM),),
        input_output_aliases={i: 2 + i for i in range(2 * n)},
        compiler_params=pltpu.CompilerParams(has_side_effects=_EFFECT),
    )(*thru, *sems1, after)
    return (res[0], res[1]), res[2:-1], res[-1]


def _ag2_end(sems1, sems2, thru, after, name):
    n = len(thru) // 2

    def body(*refs):
        x, y, c = _mesh_pos()
        s1, s2 = (refs[2 * n], refs[2 * n + 1]), (refs[2 * n + 2], refs[2 * n + 3])
        chips = [(1 - x, y), (x, 1 - y), (1 - x, 1 - y)]
        for a in range(n):
            land = refs[n + a]
            _ag2_copy(land, s1, 4 * a, (x, y, c), (x, y, 1 - c), src=refs[a]).wait_send()
            _ag2_copy(land, s1, 4 * a, (x, y, 1 - c), (x, y, c)).wait_recv()
            for j, chip in enumerate(chips):
                _ag2_copy(land, s1, 4 * a + 1 + j, (x, y, c), (*chip, c), src=refs[a]).wait_send()
                _ag2_copy(land, s2, 3 * a + j, (*chip, c), (x, y, 1 - c)).wait_send()
                _ag2_copy(land, s2, 3 * a + j, (*chip, 1 - c), (x, y, c)).wait_recv()

    res = pl.pallas_call(
        body, name=name, out_shape=tuple(pltpu.HBM(b.shape, b.dtype) for b in thru),
        in_specs=(_HBM,) * (2 * n) + (_SEM,) * 4 + (pl.BlockSpec(memory_space=pl.ANY),), out_specs=(_HBM,) * (2 * n),
        input_output_aliases={i: i for i in range(2 * n)},
        compiler_params=pltpu.CompilerParams(has_side_effects=_EFFECT),
    )(*thru, *sems1, *sems2, after)
    return res[n:]


def _own_slot(block, me, slots=N_DEV):
    return lax.dynamic_update_slice(lax.empty((slots,) + block.shape, block.dtype), block[None], (me, 0, 0))


def _add_pair(p, r, name, after):
    _, R, C = p.shape
    tr = _pick(R, (512, 256, 128, 64, 32, 16))

    def body(c_ref, p_ref, r_ref, after_ref, o_ref):
        o_ref[...] = (p_ref[...].astype(F32) + r_ref[...].astype(F32)).astype(o_ref.dtype)

    return pl.pallas_call(
        body, name=name, out_shape=jax.ShapeDtypeStruct((N_CHIP, R, C), p.dtype),
        grid_spec=pltpu.PrefetchScalarGridSpec(
            num_scalar_prefetch=1, grid=(N_CHIP, R // tr),
            in_specs=[pl.BlockSpec((None, None, tr, C), lambda q, i, c_ref: (q, c_ref[0], i, 0)),
                      pl.BlockSpec((None, tr, C), lambda q, i, c_ref: (q, i, 0)), pl.BlockSpec(memory_space=pl.ANY)],
            out_specs=pl.BlockSpec((None, tr, C), lambda q, i, c_ref: (q, i, 0))),
        compiler_params=_cparams(("parallel", "parallel")),
    )(lax.axis_index("c").reshape(1).astype(jnp.int32), p.reshape(N_CHIP, 2, R, C), r, after)


def _sum_slots(g, name):
    ns, R, C = g.shape
    tr = _pick(R, (256, 128, 64, 32, 16))

    def body(g_ref, o_ref):
        acc = g_ref[0].astype(F32)
        for j in range(1, ns):
            acc = acc + g_ref[j].astype(F32)
        o_ref[...] = acc

    return pl.pallas_call(
        body, name=name, grid=(R // tr,),
        in_specs=[pl.BlockSpec((ns, tr, C), lambda i: (0, i, 0))], out_specs=pl.BlockSpec((tr, C), lambda i: (i, 0)),
        out_shape=jax.ShapeDtypeStruct((R, C), F32), compiler_params=_cparams(("parallel",)),
    )(g)


def _pack_rows(arrs, dtype):
    parts = []
    for a in arrs:
        flat = a.reshape(-1).astype(dtype)
        pad = (-flat.shape[0]) % (PACK_W * 16)
        parts.append(jnp.pad(flat, (0, pad)).reshape(-1, PACK_W))
    out = jnp.concatenate(parts, axis=0)
    return jnp.pad(out, ((0, (-out.shape[0]) % PACK_ROWS), (0, 0)))


def _packed_rows(shape):
    n = math.prod(shape)
    return (n + PACK_W * 16 - 1) // (PACK_W * 16) * 16


def _unpack_rows(packed, shapes):
    out, r0 = [], 0
    for s in shapes:
        rows, n = _packed_rows(s), math.prod(s)
        out.append(packed[r0:r0 + rows].reshape(rows * PACK_W)[:n].reshape(s))
        r0 += rows
    return out


def _adamw_math(w, g, m, v):
    m = ADAM_B1 * m + (1.0 - ADAM_B1) * g
    v = ADAM_B2 * v + (1.0 - ADAM_B2) * (g * g)
    m_hat = m / (1.0 - ADAM_B1 ** ADAM_STEP)
    v_hat = v / (1.0 - ADAM_B2 ** ADAM_STEP)
    delta = -ADAM_LR * (m_hat / (jnp.sqrt(v_hat) + ADAM_EPS) + ADAM_WD * w)
    return delta, m, v


def _adamw_slots(w, gs, m, v, name, after=None):
    ns, R, C = gs.shape
    row_bytes = 2 * (ns * C * gs.dtype.itemsize + 7 * C * 4) + 6 * 4 * C
    tr = _pick(R, [t for t in (512, 256, 128, 64, 32, 16) if t * row_bytes <= ROWMAP_TILE_BYTES] + [16])

    def body(w_ref, g_ref, m_ref, v_ref, *rest):
        outs = rest[(after is not None):]
        g = g_ref[0].astype(F32)
        for j in range(1, ns):
            g = g + g_ref[j].astype(F32)
        res = (g,) + _adamw_math(w_ref[...], g, m_ref[...], v_ref[...])
        for o_ref, val in zip(outs, res):
            o_ref[...] = val

    row = pl.BlockSpec((tr, C), lambda i: (i, 0))
    return pl.pallas_call(
        body, name=name, grid=(R // tr,),
        in_specs=[row, pl.BlockSpec((ns, tr, C), lambda i: (0, i, 0)), row, row]
        + [pl.BlockSpec(memory_space=pl.ANY)] * (after is not None),
        out_specs=[row] * 4, out_shape=[jax.ShapeDtypeStruct((R, C), F32)] * 4, compiler_params=_cparams(("parallel",)),
    )(w, gs, m, v, *([] if after is None else [after]))


def _adamw(w, g, m, v, name, after=None):
    R, C = w.shape
    return _rowmap(_adamw_math, name, R, [w, g, m, v], [], [(C, F32)] * 3, after=after)


def _s5_disc_math(lr, li, ldt, br, bi):
    dt = jnp.exp(ldt)
    mag = jnp.exp(lr * dt)
    ab_re, ab_im = mag * jnp.cos(li * dt), mag * jnp.sin(li * dt)
    den = lr * lr + li * li
    nr, ni = ab_re - 1.0, ab_im
    co_re = (nr * lr + ni * li) / den
    co_im = (ni * lr - nr * li) / den
    bb_re = co_re * br - co_im * bi
    bb_im = co_re * bi + co_im * br
    return ab_re, ab_im, bb_re, bb_im


def _s5_tables(a_re, a_im, ldt, b_re, b_im, c_re, c_im):
    _, G, P, N = b_re.shape
    nch = G // 8

    def body(lr_ref, li_ref, ldt_ref, br_ref, bi_ref, cr_ref, ci_ref, wre, wim, vre, vim, pwr, pwi):
        ar, ai, bb_re, bb_im = _s5_disc_math(lr_ref[0], li_ref[0], ldt_ref[0], br_ref[0], bi_ref[0])
        cr, ci = cr_ref[0], ci_ref[0]
        pr, pi = jnp.ones_like(ar), jnp.zeros_like(ar)
        for j in range(SUB + 1):
            pwr[0, j], pwi[0, j] = pr, pi
            if j < SUB:
                tabs = ((wre, bb_re * pr - bb_im * pi), (wim, bb_re * pi + bb_im * pr),
                        (vre, cr * pr - ci * pi), (vim, -(cr * pi + ci * pr)))
                for ref, val in tabs:
                    for s in range(nch):
                        ref[0, s, pl.ds(j * LANE, LANE), :] = val[s * 8:(s + 1) * 8].reshape(LANE, N).astype(BF16)
            pr, pi = pr * ar - pi * ai, pr * ai + pi * ar

    g1n = pl.BlockSpec((1, G, 1, N), lambda d: (d, 0, 0, 0))
    gpn = pl.BlockSpec((1, G, P, N), lambda d: (d, 0, 0, 0))
    tab = pl.BlockSpec((1, nch, SUB * LANE, N), lambda d: (d, 0, 0, 0))
    pw = pl.BlockSpec((1, SUB + 1, G, 1, N), lambda d: (d, 0, 0, 0, 0))
    s_tab = jax.ShapeDtypeStruct((2, nch, SUB * LANE, N), BF16)
    s_pw = jax.ShapeDtypeStruct((2, SUB + 1, G, 1, N), F32)
    return pl.pallas_call(
        body, name="s5_tables", grid=(2,),
        in_specs=[g1n, g1n, pl.BlockSpec((1, G, 1, 1), lambda d: (d, 0, 0, 0)), gpn, gpn, gpn, gpn],
        out_specs=[tab] * 4 + [pw] * 2, out_shape=[s_tab] * 4 + [s_pw] * 2,
        compiler_params=_cparams(("parallel",)),
    )(a_re, a_im, ldt, b_re, b_im, c_re, c_im)


def _s5_expand(t_re, t_im, name):
    _, nch, R, N = t_re.shape
    sw = 8 * N

    def body(re_ref, im_ref, o_ref):
        spread = (lax.broadcasted_iota(jnp.int32, (N, sw), 1) % N == lax.broadcasted_iota(jnp.int32, (N, sw), 0)).astype(BF16)
        row_g = (lax.broadcasted_iota(jnp.int32, (R, sw), 0) % LANE) // S5_GROUP
        keep = row_g == lax.broadcasted_iota(jnp.int32, (R, sw), 1) // N
        for half, ref in enumerate((re_ref, im_ref)):
            t = jnp.dot(ref[0, 0], spread, preferred_element_type=F32)
            o_ref[0, 0, :, pl.ds(half * sw, sw)] = jnp.where(keep, t, 0.0).astype(BF16)

    spec = pl.BlockSpec((1, 1, R, N), lambda d, s: (d, s, 0, 0))
    return pl.pallas_call(
        body, name=name, grid=(2, nch), in_specs=[spec, spec],
        out_specs=pl.BlockSpec((1, 1, R, 2 * sw), lambda d, s: (d, s, 0, 0)),
        out_shape=jax.ShapeDtypeStruct((2, nch, R, 2 * sw), BF16), compiler_params=_cparams(("parallel", "parallel")),
    )(t_re, t_im)


def _s5_param_bwd(a_re, a_im, ldt, b_re, b_im, da_re, da_im, dbb_re, dbb_im):
    _, G, P, N = b_re.shape

    def body(lr_ref, li_ref, ldt_ref, br_ref, bi_ref, dar, dai, dbr, dbi, o_lr, o_li, o_ldt, o_br, o_bi):
        _, vjp = jax.vjp(_s5_disc_math, lr_ref[0], li_ref[0], ldt_ref[0], br_ref[0], bi_ref[0])
        o_lr[0], o_li[0], o_ldt[0], o_br[0], o_bi[0] = vjp((dar[0], dai[0], dbr[0], dbi[0]))

    g1n = pl.BlockSpec((1, G, 1, N), lambda d: (d, 0, 0, 0))
    g11 = pl.BlockSpec((1, G, 1, 1), lambda d: (d, 0, 0, 0))
    gpn = pl.BlockSpec((1, G, P, N), lambda d: (d, 0, 0, 0))
    s_g1n, s_g11, s_gpn = (jax.ShapeDtypeStruct(s, F32) for s in ((2, G, 1, N), (2, G, 1, 1), (2, G, P, N)))
    return pl.pallas_call(
        body, name="s5_param_bwd", grid=(2,),
        in_specs=[g1n, g1n, g11, gpn, gpn, g1n, g1n, gpn, gpn], out_specs=[g1n, g1n, g11, gpn, gpn],
        out_shape=[s_g1n, s_g1n, s_g11, s_gpn, s_gpn], compiler_params=_cparams(("parallel",)),
    )(a_re, a_im, ldt, b_re, b_im, da_re, da_im, dbb_re, dbb_im)


def _tile_local_scan(u, w_ref, back):
    tb, sw2 = u.shape[0], w_ref.shape[1]
    sw, half = sw2 // 2, LANE // 2
    tau = lax.broadcasted_iota(jnp.int32, u.shape, 0) % SUB
    low = lax.broadcasted_iota(jnp.int32, u.shape, 1) < half
    parts = [u]
    for j in range(1, SUB):
        if back:
            parts.append(jnp.where(tau >= j, pltpu.roll(u, j, 0), 0.0))
        else:
            parts.append(jnp.where(tau <= SUB - 1 - j, pltpu.roll(u, tb - j, 0), 0.0))
    out = [None] * 4
    for h in range(2):
        pieces = [jnp.where(low, a, pltpu.roll(b, half, 1)) if h == 0 else jnp.where(low, pltpu.roll(a, half, 1), b)
                  for a, b in zip(parts[0::2], parts[1::2])]
        lhs = jnp.concatenate(pieces, axis=1).astype(BF16)
        rows = jnp.concatenate([w_ref[pl.ds(j * LANE + h * half, half), :] for j in range(SUB)], axis=0)
        for part in range(2):
            cols = rows[:, part * sw + h * (sw // 2):part * sw + (h + 1) * (sw // 2)]
            out[2 * part + h] = jnp.dot(lhs, cols, preferred_element_type=F32)
    return jnp.concatenate(out, axis=1)


def _cmul_add(tile, pw, carry, sw):
    pr, pi, cr, ci = pw[:, :sw], pw[:, sw:], carry[:, :sw], carry[:, sw:]
    return tile + jnp.concatenate([pr * cr - pi * ci, pr * ci + pi * cr], axis=1)


def _tile_scan(buf, base, ntile, pw, carry, sw, causal):
    def step(k, c):
        i = k if causal else ntile - 1 - k
        r = pl.multiple_of(base + i * SUB, SUB)
        tile = _cmul_add(buf[pl.ds(r, SUB), :], pw, c, sw)
        buf[pl.ds(r, SUB), :] = tile
        return tile[SUB - 1:SUB, :] if causal else tile[0:1, :]

    return lax.fori_loop(0, ntile, step, carry)


def _s5_fwd(h_all, waug, vaug, pw, S5W, T, d, causal, name):
    S = h_all.shape[0]
    _, nch, _, sw2 = waug.shape
    sw = sw2 // 2
    tb = _pick(math.gcd(T, S - T), (256, 128, 64, 32, 16))
    ntile, nt, off = tb // SUB, S // tb, T // tb
    rb = (lambda s, t: ((t + off) % nt, s)) if causal else (lambda s, t: (nt - 1 - t, s))

    def body(u_ref, w_ref, v_ref, p_ref, y_ref, h_ref, hblk, carry):
        @pl.when(pl.program_id(1) == 0)
        def _():
            carry[...] = jnp.zeros_like(carry)

        hblk[...] = _tile_local_scan(u_ref[...], w_ref, causal)
        carry[...] = _tile_scan(hblk, 0, ntile, p_ref[...], carry[...], sw, causal)
        hb = hblk[...].astype(BF16)
        h_ref[...] = hb
        y_ref[...] = lax.dot_general(hb, v_ref[...], _NT, preferred_element_type=F32)

    return pl.pallas_call(
        body, name=name, grid=(nch, nt),
        in_specs=[pl.BlockSpec((tb, LANE), rb),
                  pl.BlockSpec((None, None, SUB * LANE, sw2), lambda s, t: (d, s, 0, 0)),
                  pl.BlockSpec((None, None, LANE, sw2), lambda s, t: (d, s, 0, 0)),
                  pl.BlockSpec((None, SUB, sw2), lambda s, t: (s, 0, 0))],
        out_specs=[pl.BlockSpec((tb, LANE), rb), pl.BlockSpec((tb, sw2), rb)],
        out_shape=[jax.ShapeDtypeStruct((S, S5W), F32), jax.ShapeDtypeStruct((S, nch * sw2), BF16)],
        scratch_shapes=[pltpu.VMEM((tb, sw2), F32), pltpu.VMEM((1, sw2), F32)],
        compiler_params=_cparams(("parallel", "arbitrary")),
    )(h_all, waug, vaug, pw)


def _s5_bwd(dy_all, h_all, hs, waug, vaug, pwc, S5W, T, d, causal, name):
    S = h_all.shape[0]
    _, nch, _, sw2 = waug.shape
    sw = sw2 // 2
    tb = _pick(math.gcd(T, S - T), (256, 128, 64, 32, 16))
    ntile, nt, off = tb // SUB, S // tb, T // tb
    rb = (lambda s, t: ((nt - 1 - t + off) % nt, s)) if causal else (lambda s, t: (t, s))
    adj_causal = not causal
    edge = SUB - 1 if adj_causal else SUB + tb
    keep_src, keep_dst = (tb, 0) if adj_causal else (SUB, SUB + tb)

    def body(dy_ref, u_ref, h_ref, w_ref, v_ref, p_ref, du_ref, dbb_ref, dc_ref, da_ref, lam):
        @pl.when(pl.program_id(1) == 0)
        def _():
            lam[pl.ds(0, SUB), :] = jnp.zeros((SUB, sw2), F32)
            lam[pl.ds(SUB + tb, SUB), :] = jnp.zeros((SUB, sw2), F32)
            dbb_ref[...] = jnp.zeros_like(dbb_ref)
            dc_ref[...] = jnp.zeros_like(dc_ref)
            da_ref[...] = jnp.zeros_like(da_ref)

        dy = dy_ref[...]
        lam[pl.ds(SUB, tb), :] = _tile_local_scan(dy, v_ref, adj_causal)
        _tile_scan(lam, SUB, ntile, p_ref[...], lam[pl.ds(edge, 1), :], sw, adj_causal)
        lb = lam[pl.ds(SUB, tb), :].astype(BF16)
        du_ref[...] = lax.dot_general(lb, w_ref[...], _NT, preferred_element_type=F32)
        dbb_ref[...] += lax.dot_general(u_ref[...].astype(BF16), lb, _TN, preferred_element_type=F32)
        dc_ref[...] += lax.dot_general(h_ref[...], dy.astype(BF16), _TN, preferred_element_type=F32)
        h = h_ref[...].astype(F32)
        ln = lam[pl.ds(SUB + 1 if causal else SUB - 1, tb), :]
        hr, hi, lr, li = h[:, :sw], h[:, sw:], ln[:, :sw], ln[:, sw:]
        da_ref[...] += jnp.concatenate([jnp.sum(hr * lr + hi * li, axis=0, keepdims=True),
                                        jnp.sum(hr * li - hi * lr, axis=0, keepdims=True)], axis=1)
        lam[pl.ds(keep_dst, SUB), :] = lam[pl.ds(keep_src, SUB), :]

    fixed = lambda s, t: (s, 0, 0)
    return pl.pallas_call(
        body, name=name, grid=(nch, nt),
        in_specs=[pl.BlockSpec((tb, LANE), rb), pl.BlockSpec((tb, LANE), rb), pl.BlockSpec((tb, sw2), rb),
                  pl.BlockSpec((None, None, LANE, sw2), lambda s, t: (d, s, 0, 0)),
                  pl.BlockSpec((None, None, SUB * LANE, sw2), lambda s, t: (d, s, 0, 0)),
                  pl.BlockSpec((None, SUB, sw2), fixed)],
        out_specs=[pl.BlockSpec((tb, LANE), rb), pl.BlockSpec((None, LANE, sw2), fixed),
                   pl.BlockSpec((None, sw2, LANE), fixed), pl.BlockSpec((None, 1, sw2), fixed)],
        out_shape=[jax.ShapeDtypeStruct((S, S5W), F32), jax.ShapeDtypeStruct((nch, LANE, sw2), F32),
                   jax.ShapeDtypeStruct((nch, sw2, LANE), F32), jax.ShapeDtypeStruct((nch, 1, sw2), F32)],
        scratch_shapes=[pltpu.VMEM((tb + 2 * SUB, sw2), F32)],
        compiler_params=_cparams(("parallel", "arbitrary")),
    )(dy_all, h_all, hs, waug, vaug, pwc)


def _attn_fwd(qn, qr, kv, kr, H, scale):
    T, S = qn.shape[0], kv.shape[0]
    tq = _pick(T, (256, 128, 64, 32, 16))

    def body(qn_ref, qr_ref, kn_ref, v_ref, kr_ref, o_ref, lse_ref):
        q = jnp.concatenate([qn_ref[...], qr_ref[...]], axis=1)
        k = jnp.concatenate([kn_ref[...], kr_ref[...]], axis=1)
        s = lax.dot_general(q, k, _NT, preferred_element_type=F32) * scale
        m = jnp.max(s, axis=1, keepdims=True)
        p = jnp.exp(s - m)
        l = jnp.sum(p, axis=1, keepdims=True)
        o_ref[...] = jnp.dot((p * (1.0 / l)).astype(BF16), v_ref[...], preferred_element_type=F32).astype(o_ref.dtype)
        lse_ref[0] = m + jnp.log(l)

    q_spec = pl.BlockSpec((tq, LANE), lambda h, i: (i, h))
    return pl.pallas_call(
        body, name="attn_fwd", grid=(H, T // tq),
        in_specs=[q_spec, q_spec, pl.BlockSpec((S, LANE), lambda h, i: (0, h)), pl.BlockSpec((S, LANE), lambda h, i: (0, H + h)),
                  pl.BlockSpec((S, LANE), lambda h, i: (0, 0))],
        out_specs=[q_spec, pl.BlockSpec((1, tq, 1), lambda h, i: (h, i, 0))],
        out_shape=[jax.ShapeDtypeStruct((T, H * LANE), BF16), jax.ShapeDtypeStruct((H, T, 1), F32)],
        compiler_params=_cparams(("parallel", "parallel")),
    )(qn, qr, kv, kv, kr)


def _attn_bwd(qn, qr, kv, kr, do, lse, H, scale):
    T, S = qn.shape[0], kv.shape[0]
    tq = _pick(T, (512, 256, 128, 64, 32, 16))
    nq = T // tq

    def body(qn_ref, qr_ref, kn_ref, v_ref, kr_ref, do_ref, lse_ref, dqn_ref, dqr_ref, dkn_ref, dkr_ref, dv_ref, dk_acc, dv_acc):
        i = pl.program_id(1)
        q = jnp.concatenate([qn_ref[...], qr_ref[...]], axis=1)
        k = jnp.concatenate([kn_ref[...], kr_ref[...]], axis=1)
        v, d_o = v_ref[...], do_ref[...]
        s = lax.dot_general(q, k, _NT, preferred_element_type=F32) * scale
        p = jnp.exp(s - lse_ref[0])
        dv_part = lax.dot_general(p.astype(BF16), d_o, _TN, preferred_element_type=F32)
        dp = lax.dot_general(d_o, v, _NT, preferred_element_type=F32)
        ds = (p * (dp - jnp.sum(p * dp, axis=1, keepdims=True)) * scale).astype(BF16)
        dq = jnp.dot(ds, k, preferred_element_type=F32)
        dqn_ref[...] = dq[:, :LANE].astype(dqn_ref.dtype)
        dqr_ref[...] = dq[:, LANE:].astype(dqr_ref.dtype)
        dk_part = lax.dot_general(ds, q, _TN, preferred_element_type=F32)

        @pl.when(i == 0)
        def _():
            dk_acc[...] = dk_part
            dv_acc[...] = dv_part

        @pl.when(i > 0)
        def _():
            dk_acc[...] += dk_part
            dv_acc[...] += dv_part

        @pl.when(i == nq - 1)
        def _():
            dkn_ref[...] = dk_acc[:, :LANE].astype(dkn_ref.dtype)
            dkr_ref[...] = dk_acc[:, LANE:].astype(dkr_ref.dtype)
            dv_ref[...] = dv_acc[...].astype(dv_ref.dtype)

    q_spec = pl.BlockSpec((tq, LANE), lambda h, i: (i, h))
    k_spec = pl.BlockSpec((S, LANE), lambda h, i: (0, h))
    t_shape, s_shape = jax.ShapeDtypeStruct((T, H * LANE), BF16), jax.ShapeDtypeStruct((S, H * LANE), BF16)
    return pl.pallas_call(
        body, name="attn_bwd", grid=(H, nq),
        in_specs=[q_spec, q_spec, k_spec, pl.BlockSpec((S, LANE), lambda h, i: (0, H + h)),
                  pl.BlockSpec((S, LANE), lambda h, i: (0, 0)), q_spec, pl.BlockSpec((1, tq, 1), lambda h, i: (h, i, 0))],
        out_specs=[q_spec, q_spec, k_spec, k_spec, k_spec], out_shape=[t_shape, t_shape, s_shape, s_shape, s_shape],
        scratch_shapes=[pltpu.VMEM((S, 2 * LANE), F32), pltpu.VMEM((S, LANE), F32)],
        compiler_params=_cparams(("parallel", "arbitrary")),
    )(qn, qr, kv, kv, kr, do, lse)


def _rope_tables(T):
    rows = T // GRID_W
    row = jnp.repeat(jnp.arange(rows, dtype=F32), GRID_W)
    col = jnp.tile(jnp.arange(GRID_W, dtype=F32), rows)
    n_freq = QK_ROPE // 4
    inv = ROPE_BASE ** (-jnp.arange(n_freq, dtype=F32) / n_freq)
    ar, ac = row[:, None] * inv, col[:, None] * inv
    cos = jnp.concatenate([jnp.cos(ar), jnp.cos(ar), jnp.cos(ac), jnp.cos(ac)], axis=1)
    sin = jnp.concatenate([-jnp.sin(ar), jnp.sin(ar), -jnp.sin(ac), jnp.sin(ac)], axis=1)
    pad = lambda t: jnp.pad(t, ((0, 0), (0, LANE - QK_ROPE)))
    return pad(cos), pad(sin)


def _dw(a, dy, w, name):
    return _mm(a, dy, "tn", BF16, name, out_slots=w.shape[0] if w.ndim == 3 else None)


def _local_step(x, ctx, tgt, m_lat, m_ctx, p, W, goff, hooks=None):
    T, D = x.shape
    Tc = ctx.shape[0]
    S = T + Tc
    S5W = p["s5_d"].shape[1]
    QR, KVR = p["q_norm"].shape[1], p["kv_norm"].shape[1]
    G, N = p["s5_a_re"].shape[1:]
    P = S5_GROUP
    nch = G // 8
    o_cq, o_ckv, o_kr = S5W, S5W + QR, S5W + QR + KVR
    assert o_cq % QR == 0 and o_ckv % KVR == 0 and o_kr % LANE == 0 and goff % D == 0 and S5W % LANE == 0 and G % 8 == 0
    assert 8 * P == LANE
    row = lambda k, m: m[k:k + 1]
    sh1, sc1, g1, sh2, sc2, g2 = (row(k, m_lat) for k in range(6))
    csh1, csc1 = row(0, m_ctx), row(1, m_ctx)
    n1, n2, nf = p["norm1"], p["norm2"], p["norm_f"]

    (xm_lat,) = _rowmap(_normmod, "norm1_lat", T, [x], [n1, sc1, sh1], [(D, BF16)])
    (xm_ctx,) = _rowmap(_normmod, "norm1_ctx", Tc, [ctx], [n1, csc1, csh1], [(D, BF16)])
    xm_all = jnp.concatenate([xm_lat, xm_ctx], axis=0)

    a_re, a_im = p["s5_a_re"][:, :, None, :], p["s5_a_im"][:, :, None, :]
    ldt = p["s5_log_dt"][:, :, None, None]
    b_re, b_im = p["s5_b_re"].transpose(0, 1, 3, 2), p["s5_b_im"].transpose(0, 1, 3, 2)
    wre, wim, vre, vim, pwr, pwi = _s5_tables(a_re, a_im, ldt, b_re, b_im, p["s5_c_re"], p["s5_c_im"])
    waug = _s5_expand(wre, wim, "s5_expand_b")
    vaug = _s5_expand(vre, vim, "s5_expand_c")
    lanes = lambda t: t.reshape(2, SUB + 1, nch, 8 * N).transpose(0, 2, 1, 3)
    pw_re, pw_im = lanes(pwr), lanes(pwi)
    near = lambda t: t[:, :, 1:]
    far = lambda t: t[:, :, :0:-1]
    pw_c = jnp.concatenate([near(pw_re), near(pw_im)], axis=-1)
    pw_a = jnp.concatenate([far(pw_re), far(pw_im)], axis=-1)
    pwc_c = jnp.concatenate([near(pw_re), -near(pw_im)], axis=-1)
    pwc_a = jnp.concatenate([far(pw_re), -far(pw_im)], axis=-1)

    if hooks:
        W = {**W, **hooks["first_weights"](xm_all, vaug)}
    H = W["w_uq"].shape[1] // (2 * LANE)
    h_all = _mm(xm_all, W["w_in"], "nn", F32, "mm_in")
    y0, hs0 = _s5_fwd(h_all, waug, vaug, pw_c[0], S5W, T, 0, True, "s5_scan_fwd0")
    y1, hs1 = _s5_fwd(h_all, waug, vaug, pw_a[1], S5W, T, 1, False, "s5_scan_fwd1")

    def s5_combine(u, yf, yr, dskip):
        y5 = dskip * u + yf + yr
        return y5, jax.nn.gelu(y5)

    y5, z = _rowmap(s5_combine, "s5_combine", T, [(h_all, S5W, 0), y0, y1], [p["s5_d"]], [(S5W, F32), (S5W, BF16)])

    (qn,) = _rowmap(_rms, "q_norm", T, [(h_all, QR, o_cq // QR)], [p["q_norm"]], [(QR, BF16)])
    (kvn,) = _rowmap(_rms, "kv_norm", S, [(h_all, KVR, o_ckv // KVR)], [p["kv_norm"]], [(KVR, BF16)])
    qraw = _mm(qn, W["w_uq"], "nn", F32, "mm_uq")
    kvraw = _mm(kvn, W["w_ukv"], "nn", BF16, "mm_ukv")
    cos_q, sin_q = _rope_tables(T)
    padl = lambda t: jnp.pad(t[:, :LANE], ((0, Tc), (0, 0)))
    cos_k = padl(cos_q) + jnp.pad(jnp.ones((Tc, LANE), F32), ((T, 0), (0, 0)))
    sin_k = padl(sin_q)
    hn = H * LANE

    def q_post(q, cos, sin):
        return q[:, :hn], _rope(q[:, hn:], jnp.tile(cos, (1, H)), jnp.tile(sin, (1, H)))

    q_nope, q_rope = _rowmap(q_post, "q_rope", T, [qraw, cos_q, sin_q], [], [(hn, BF16), (hn, BF16)])
    (kr,) = _rowmap(_rope, "k_rope", S, [(h_all, LANE, o_kr // LANE), cos_k, sin_k], [], [(LANE, BF16)])
    scale = (QK_NOPE + QK_ROPE) ** -0.5
    o, lse = _attn_fwd(q_nope, q_rope, kvraw, kr, H, scale)
    g1_fwd = g1
    if hooks:
        W = {**W, **hooks["mix_weights"](o)}
        g1_fwd = g1 + hooks["ffn_mid"](o)[:1, :1]

    zz = _mm(z, W["w_glu"], "nn", BF16, "mm_glu")
    br_mla = _mm(o, W["w_mla_o"], "nn", BF16, "mm_mla_o")

    def merge(zz, brm, gs, gm):
        a, b = zz[:, :D], zz[:, D:]
        return jax.nn.sigmoid(gs) * (a * jax.nn.sigmoid(b)) + jax.nn.sigmoid(gm) * brm

    gb = goff // D
    merge_ins = [zz, br_mla, (h_all, D, gb), (h_all, D, gb + 1)]
    (mix,) = _rowmap(merge, "merge", T, merge_ins, [], [(D, BF16)])
    out1 = _mm(mix, W["w_out"], "nn", F32, "mm_out")

    def resid_norm2(x, out1, g1, n2, sc2, sh2):
        x1 = x + g1 * out1
        return x1, _normmod(x1, n2, sc2, sh2)

    x1, hm = _rowmap(resid_norm2, "resid_norm2", T, [x, out1], [g1_fwd, n2, sc2, sh2], [(D, F32), (D, BF16)])

    if hooks:
        W = {**W, **hooks["ffn_weights"](hm)}
    FF = W["w_ffn_out"].shape[0]
    assert FF % LANE == 0
    def swiglu_act(a, b):
        return jax.nn.silu(a) * b

    if W["w_ffn_in"].ndim == 3:
        ffn_a, ffn_b, f = _mm_swiglu(hm, W["w_ffn_in"], swiglu_act, "mm_ffn_in")
        ffn_a, ffn_b = (ffn_a, FF, 0), (ffn_b, FF, 0)
    else:
        ab = _mm(hm, W["w_ffn_in"], "nn", BF16, "mm_ffn_in")
        ffn_a, ffn_b = (ab, FF, 0), (ab, FF, 1)
        (f,) = _rowmap(swiglu_act, "ffn_act", T, [ffn_a, ffn_b], [], [(FF, BF16)])
    out2 = _mm(f, W["w_ffn_out"], "nn", F32, "mm_ffn_out")

    def loss_rows(x1, out2, g2, nf, tgt):
        y = _rms(x1 + g2 * out2, nf)
        return 0.5 * jnp.sum(jnp.mean(jnp.square(y - tgt), axis=-1))

    def final(x1, out2, tgt, g2, nf):
        val, (dx1, dout2, dg2, dnf) = jax.value_and_grad(loss_rows, argnums=(0, 1, 2, 3))(x1, out2, g2, nf, tgt)
        return dx1, dout2, jnp.full((1, LANE), val, F32), dg2, dnf

    dx2, dout2, loss_acc, dg2, dnf = _rowmap(final, "final_loss", T, [x1, out2, tgt], [g2, nf],
                                             [(D, F32), (D, BF16)], [LANE, D, D])

    gW = {}
    df = _mm(dout2, W["w_ffn_out"], "nt", BF16, "mm_ffn_out_dx")
    gW["w_ffn_out"] = _dw(f, dout2, W["w_ffn_out"], "mm_ffn_out_dw")

    def swiglu_bwd(a, b, df):
        _, vjp = jax.vjp(swiglu_act, a, b)
        da, db = vjp(df)
        return jnp.concatenate([da, db], axis=1)

    (dab,) = _rowmap(swiglu_bwd, "ffn_act_bwd", T, [ffn_a, ffn_b, df], [], [(2 * FF, BF16)])
    dhm = _mm(dab, W["w_ffn_in"], "nt", F32, "mm_ffn_in_dx")
    gW["w_ffn_in"] = _dw(hm, dab, W["w_ffn_in"], "mm_ffn_in_dw")
    if hooks:
        token = hooks["send_grads"](FFN, [gW.pop(n) for n in FFN])
        g1 = g1 if token is None else g1 + token[:1, :1]

    def resid_norm2_bwd(x, out1, dx2, dhm, g1, n2, sc2, sh2):
        _, vjp = jax.vjp(resid_norm2, x, out1, g1, n2, sc2, sh2)
        dx, dout1, dg1, dn2, dsc2, dsh2 = vjp((dx2, dhm))
        return dx, dout1, dg1, dn2, dsc2, dsh2

    dx1, dout1, dg1, dn2, dsc2, dsh2 = _rowmap(resid_norm2_bwd, "resid_norm2_bwd", T, [x, out1, dx2, dhm],
                                               [g1, n2, sc2, sh2], [(D, F32), (D, BF16)], [D, D, D, D])

    dmix = _mm(dout1, W["w_out"], "nt", BF16, "mm_out_dx")
    gW["w_out"] = _dw(mix, dout1, W["w_out"], "mm_out_dw")

    def merge_bwd(zz, brm, gs, gm, dmix):
        _, vjp = jax.vjp(merge, zz, brm, gs, gm)
        dzz, dbrm, dgs, dgm = vjp(dmix)
        return dzz, dbrm, jnp.concatenate([dgs, dgm], axis=1)

    dzz, dbrm, dgates = _rowmap(merge_bwd, "merge_bwd", T, merge_ins + [dmix], [],
                                [(2 * D, BF16), (D, BF16), (2 * D, BF16)])
    do = _mm(dbrm, W["w_mla_o"], "nt", BF16, "mm_mla_o_dx")
    gW["w_mla_o"] = _dw(o, dbrm, W["w_mla_o"], "mm_mla_o_dw")
    dz = _mm(dzz, W["w_glu"], "nt", BF16, "mm_glu_dx")
    gW["w_glu"] = _dw(z, dzz, W["w_glu"], "mm_glu_dw")
    d_skip_w = p["s5_d"]
    if hooks:
        token = hooks["send_grads"](MIX, [gW.pop(n) for n in MIX])
        d_skip_w = d_skip_w if token is None else d_skip_w + token[:1, :1]

    def s5_combine_bwd(u, y5, dz, dskip):
        _, vjp = jax.vjp(lambda y: jax.nn.gelu(y), y5)
        (dy5,) = vjp(dz)
        return dy5, jnp.sum(dy5 * u, axis=0, keepdims=True)

    dy5, d_skip = _rowmap(s5_combine_bwd, "s5_combine_bwd", T, [(h_all, S5W, 0), y5, dz], [d_skip_w], [(S5W, F32)], [S5W])

    dq_nope, dq_rope, dk_nope, dkr_heads, dv = _attn_bwd(q_nope, q_rope, kvraw, kr, do, lse, H, scale)

    def q_post_bwd(dqn, dqr, cos, sin):
        return jnp.concatenate([dqn, _rope_bwd(dqr, jnp.tile(cos, (1, H)), jnp.tile(sin, (1, H)))], axis=1)

    (dqraw,) = _rowmap(q_post_bwd, "q_rope_bwd", T, [dq_nope, dq_rope, cos_q, sin_q], [], [(2 * hn, BF16)])
    dkvraw = jnp.concatenate([dk_nope, dv], axis=1)

    def k_rope_bwd(dkh, cos, sin):
        d = dkh[:, :LANE]
        for h in range(1, H):
            d = d + dkh[:, h * LANE:(h + 1) * LANE]
        return _rope_bwd(d, cos, sin)

    (dkr,) = _rowmap(k_rope_bwd, "k_rope_bwd", S, [dkr_heads, cos_k, sin_k], [], [(LANE, BF16)])
    dqn = _mm(dqraw, W["w_uq"], "nt", F32, "mm_uq_dx")
    gW["w_uq"] = _dw(qn, dqraw, W["w_uq"], "mm_uq_dw")
    dkvn = _mm(dkvraw, W["w_ukv"], "nt", F32, "mm_ukv_dx")
    gW["w_ukv"] = _dw(kvn, dkvraw, W["w_ukv"], "mm_ukv_dw")

    def rms_bwd(cx, dn, g):
        _, vjp = jax.vjp(_rms, cx, g)
        return vjp(dn)

    dcq, dq_norm = _rowmap(rms_bwd, "q_norm_bwd", T, [(h_all, QR, o_cq // QR), dqn], [p["q_norm"]], [(QR, BF16)], [QR])
    dckv, dkv_norm = _rowmap(rms_bwd, "kv_norm_bwd", S, [(h_all, KVR, o_ckv // KVR), dkvn], [p["kv_norm"]],
                             [(KVR, BF16)], [KVR])

    dy_all = jnp.concatenate([dy5, jnp.zeros((Tc, S5W), F32)], axis=0)
    du0, dbb0, dc0, da0 = _s5_bwd(dy_all, h_all, hs0, waug, vaug, pwc_a[0], S5W, T, 0, True, "s5_scan_bwd0")
    du1, dbb1, dc1, da1 = _s5_bwd(dy_all, h_all, hs1, waug, vaug, pwc_c[1], S5W, T, 1, False, "s5_scan_bwd1")

    def du_combine(a, b, dy, dskip):
        return a + b + dskip * dy

    (du_all,) = _rowmap(du_combine, "s5_du", S, [du0, du1, dy_all], [p["s5_d"]], [(S5W, BF16)])
    dbb = jnp.einsum("dsgpcgn->dcsgpn", jnp.stack([dbb0, dbb1]).reshape(2, nch, 8, P, 2, 8, N)).reshape(2, 2, G, P, N)
    dcm = jnp.einsum("dscgngp->dcsgpn", jnp.stack([dc0, dc1]).reshape(2, nch, 2, 8, N, 8, P)).reshape(2, 2, G, P, N)
    da = jnp.stack([da0, da1]).reshape(2, nch, 2, 8, N).transpose(0, 2, 1, 3, 4).reshape(2, 2, G, 1, N)
    d_lr, d_li, d_ldt, d_br, d_bi = _s5_param_bwd(a_re, a_im, ldt, b_re, b_im, da[:, 0], da[:, 1], dbb[:, 0], dbb[:, 1])

    lat_only = lambda t: jnp.pad(t, ((0, Tc), (0, 0)))
    dh_all = jnp.concatenate([du_all, lat_only(dcq), dckv, dkr, jnp.zeros((S, goff - o_kr - LANE), BF16), lat_only(dgates)],
                             axis=1)
    dxm = _mm(dh_all, W["w_in"], "nt", F32, "mm_in_dx")
    gW["w_in"] = _dw(xm_all, dh_all, W["w_in"], "mm_in_dw")

    def norm1_bwd(x, dxm, dx1, n1, sc, sh):
        _, vjp = jax.vjp(_normmod, x, n1, sc, sh)
        dx, dn, dsc, dsh = vjp(dxm)
        return dx + dx1, dn, dsc, dsh

    grad_x, dn1_l, dsc1, dsh1 = _rowmap(norm1_bwd, "norm1_lat_bwd", T, [x, dxm, dx1], [n1, sc1, sh1], [(D, F32)], [D, D, D])

    def norm1_ctx_bwd(x, dxm, n1, sc, sh):
        _, vjp = jax.vjp(_normmod, x, n1, sc, sh)
        return vjp(dxm)[1:]

    dn1_c, dcsc1, dcsh1 = _rowmap(norm1_ctx_bwd, "norm1_ctx_bwd", Tc, [ctx, dxm[T:]], [n1, csc1, csh1], [], [D, D, D])

    zero = jnp.zeros((1, D), F32)
    dm_lat = jnp.concatenate([dsh1, dsc1, dg1, dsh2, dsc2, dg2], axis=0)
    dm_ctx = jnp.concatenate([dcsh1, dcsc1, zero, zero, zero, zero], axis=0)
    small = {
        "norm1": dn1_l + dn1_c, "norm2": dn2, "norm_f": dnf, "q_norm": dq_norm, "kv_norm": dkv_norm, "s5_d": d_skip,
        "s5_a_re": d_lr, "s5_a_im": d_li, "s5_log_dt": d_ldt, "s5_b_re": d_br.transpose(0, 1, 3, 2),
        "s5_b_im": d_bi.transpose(0, 1, 3, 2), "s5_c_re": dcm[:, 0], "s5_c_im": -dcm[:, 1],
    }
    return loss_acc[:, :1], grad_x, small, dm_lat, dm_ctx, gW


BIG = ("w_in", "w_uq", "w_ukv", "w_glu", "w_mla_o", "w_out", "w_ffn_in", "w_ffn_out")
FFN = ("w_ffn_in", "w_ffn_out")
MIX = ("w_out", "w_mla_o", "w_glu")
ROW_SHARDED = ("w_out", "w_ffn_out")
RELAID = ("w_in", "w_uq", "w_ukv")
SMALL = ("c_ctx", "b_mod", "norm1", "norm2", "s5_a_re", "s5_a_im", "s5_log_dt", "s5_b_re", "s5_b_im", "s5_c_re",
         "s5_c_im", "s5_d", "q_norm", "kv_norm", "norm_f")
S5_BULK = ("s5_b_re", "s5_b_im", "s5_c_re", "s5_c_im")
WEIGHTS = ("c_ctx", "w_mod", "b_mod", "norm1", "norm2", "w_in", "s5_a_re", "s5_a_im", "s5_log_dt", "s5_b_re", "s5_b_im",
           "s5_c_re", "s5_c_im", "s5_d", "w_glu", "q_norm", "kv_norm", "w_uq", "w_ukv", "w_mla_o", "w_out", "w_ffn_in",
           "w_ffn_out", "norm_f")


def _heads_split(w, heads, first):
    k = w.shape[0]
    w3 = w.reshape(k, heads, -1)
    return jnp.concatenate([w3[:, :, :first].reshape(k, -1), w3[:, :, first:].reshape(k, -1)], axis=1)


def _uq_layout(w, heads):
    k = w.shape[0]
    w3 = w.reshape(k, heads, QK_NOPE + QK_ROPE)
    rope = jnp.pad(w3[:, :, QK_NOPE:], ((0, 0), (0, 0), (0, LANE - QK_ROPE)))
    return jnp.concatenate([w3[:, :, :QK_NOPE].reshape(k, -1), rope.reshape(k, -1)], axis=1)


def _uq_unlayout(w, heads):
    k = w.shape[0]
    nope = w[:, :heads * QK_NOPE].reshape(k, heads, QK_NOPE)
    rope = w[:, heads * QK_NOPE:].reshape(k, heads, LANE)[:, :, :QK_ROPE]
    return jnp.concatenate([nope, rope], axis=2).reshape(k, -1)


def _heads_merge(w, heads, first):
    k = w.shape[0]
    a, b = w[:, :heads * first].reshape(k, heads, first), w[:, heads * first:].reshape(k, heads, -1)
    return jnp.concatenate([a, b], axis=2).reshape(k, -1)


def _cols_full(w8):
    return w8.transpose(1, 0, 2).reshape(w8.shape[1], -1)


def _cols_slots(w):
    return w.reshape(w.shape[0], N_DEV, -1).transpose(1, 0, 2)


def _weight_layout(n, w8):
    if n in ROW_SHARDED:
        return w8.reshape(-1, w8.shape[-1])
    return _cols_full(w8) if (n in RELAID or w8.shape[-1] % LANE) else w8


def _grad_slots(n, g):
    if g.ndim == 3:
        return g
    return g.reshape(N_DEV, g.shape[0] // N_DEV, g.shape[1]) if n in ROW_SHARDED else _cols_slots(g)


def _gate_offset(in_cols, D):
    return -(-(in_cols - 2 * D) // D) * D


def _model_weights(g8, D):
    W = {n: _weight_layout(n, w8) for n, w8 in g8.items()}
    w_in = W["w_in"]
    n_front = w_in.shape[1] - 2 * D
    goff = _gate_offset(w_in.shape[1], D)
    W["w_in"] = jnp.concatenate([w_in[:, :n_front], jnp.zeros((D, goff - n_front), w_in.dtype), w_in[:, n_front:]], axis=1)
    heads = W["w_uq"].shape[1] // (QK_NOPE + QK_ROPE)
    W["w_uq"] = _uq_layout(W["w_uq"], heads)
    W["w_ukv"] = _heads_split(W["w_ukv"], heads, QK_NOPE)
    return W, goff


def kernel(x, c, ctx, c_ctx, w_mod, b_mod, norm1, norm2, w_in, s5_a_re, s5_a_im, s5_log_dt, s5_b_re, s5_b_im, s5_c_re, s5_c_im, s5_d, w_glu, q_norm, kv_norm, w_uq, w_ukv, w_mla_o, w_out, w_ffn_in, w_ffn_out, norm_f, loss_target, m_c_ctx, m_w_mod, m_b_mod, m_norm1, m_norm2, m_w_in, m_s5_a_re, m_s5_a_im, m_s5_log_dt, m_s5_b_re, m_s5_b_im, m_s5_c_re, m_s5_c_im, m_s5_d, m_w_glu, m_q_norm, m_kv_norm, m_w_uq, m_w_ukv, m_w_mla_o, m_w_out, m_w_ffn_in, m_w_ffn_out, m_norm_f, v_c_ctx, v_w_mod, v_b_mod, v_norm1, v_norm2, v_w_in, v_s5_a_re, v_s5_a_im, v_s5_log_dt, v_s5_b_re, v_s5_b_im, v_s5_c_re, v_s5_c_im, v_s5_d, v_w_glu, v_q_norm, v_kv_norm, v_w_uq, v_w_ukv, v_w_mla_o, v_w_out, v_w_ffn_in, v_w_ffn_out, v_norm_f):
    a = dict(locals())
    D = x.shape[-1]
    me = 4 * lax.axis_index("x") + 2 * lax.axis_index("y") + lax.axis_index("c")

    shard = {n: a[n][0] for n in BIG}
    first = [n for n in BIG if n not in FFN + MIX]
    (cg,) = _all_gather([jnp.broadcast_to(c, (8, D))], "ag_c")
    goff = _gate_offset(w_in.shape[-1] * N_DEV, D)

    wm = w_mod[0]
    ncol = wm.shape[1]
    c16 = jnp.concatenate([cg[:, 0, :], c_ctx[None], jnp.zeros((7, D), F32)], axis=0)
    (s16,) = _rowmap(jax.nn.silu, "mod_silu", 16, [c16], [], [(D, BF16)])
    m_cols = _mm(s16, wm, "nn", F32, "mm_mod")
    (mg,) = _all_gather([m_cols], "ag_mod")
    (m16,) = _rowmap(lambda m, b: m + b, "mod_bias", 16, [_cols_full(mg)], [b_mod], [(N_DEV * ncol, F32)])

    first_blocks = [shard[n].astype(BF16) for n in first]
    fst = {}
    fst["sems1"], fst["thru"], first_token = _ag2_start(first_blocks, [_own_slot(b, me) for b in first_blocks], "ag_first_start")

    def first_weights(after_norm, after_tables):
        sems2, thru, _ = _ag2_mid(fst["sems1"], fst["thru"], after_tables, "ag_first_mid")
        lands = _ag2_end(fst["sems1"], sems2, thru, after_norm, "ag_first_end")
        return _model_weights(dict(zip(first, lands)), D)[0]

    mix_blocks = [shard[n].astype(BF16) for n in MIX]
    mix = _xchg_start(mix_blocks, [_own_slot(b, me) for b in mix_blocks], False, "ag_mix_start")
    ffn_blocks = [shard[n].astype(BF16) for n in FFN]
    ffn = {}
    ffn["sems1"], ffn["thru"], ag_token = _ag2_start(ffn_blocks, [_own_slot(b, me) for b in ffn_blocks], "ag_ffn_start")
    m16 = m16 + (first_token[:1, :1] + mix[3][:1, :1] + ag_token[:1, :1])

    def mix_weights(after):
        lands = _xchg_wait(mix[0], mix[1], mix[2], after, False, "ag_mix_wait")
        return {n: _weight_layout(n, w8) for n, w8 in zip(MIX, lands)}

    def ffn_mid(after):
        ffn["sems2"], ffn["thru"], token = _ag2_mid(ffn["sems1"], ffn["thru"], after, "ag_ffn_mid")
        return token

    def ffn_weights(after):
        lands = _ag2_end(ffn["sems1"], ffn["sems2"], ffn["thru"], after, "ag_ffn_end")
        return {n: _weight_layout(n, w8) for n, w8 in zip(FFN, lands)}

    rs_async = {}

    def send_grads(names, gs):
        slots = [_grad_slots(n, g) for n, g in zip(names, gs)]
        lands = [_own_slot(lax.dynamic_index_in_dim(s, me, 0, keepdims=False), me) for s in slots]
        rs_async[names] = _xchg_start(slots, lands, True, "rs_start_" + names[0])
        return rs_async[names][3]

    m_lat = lax.dynamic_slice(m16, (me, 0), (1, 6 * D)).reshape(6, D)
    m_ctx = m16[8].reshape(6, D)

    p = {n: a[n][0] for n in ("norm1", "norm2", "s5_a_re", "s5_a_im", "s5_log_dt", "s5_b_re", "s5_b_im", "s5_c_re",
                              "s5_c_im", "q_norm", "kv_norm")}
    p = {k: (v[None] if v.ndim == 1 else v) for k, v in p.items()}
    p["s5_d"] = s5_d.reshape(1, -1)
    p["norm_f"] = norm_f[None]
    hooks = dict(first_weights=first_weights, mix_weights=mix_weights, ffn_mid=ffn_mid, ffn_weights=ffn_weights,
                 send_grads=send_grads)
    loss_part, grad_x, small, dm_lat, dm_ctx, gW = _local_step(x[0], ctx[0], loss_target[0], m_lat, m_ctx, p, {}, goff, hooks)
    loss = lax.psum(loss_part[0, 0], ("x", "y", "c"))

    gW = dict(gW)
    n_front = w_in.shape[-1] * N_DEV - 2 * D
    gW["w_in"] = jnp.concatenate([gW["w_in"][:, :n_front], gW["w_in"][:, goff:]], axis=1)
    heads = gW["w_uq"].shape[1] // (2 * LANE)
    gW["w_uq"] = _uq_unlayout(gW["w_uq"], heads)
    gW["w_ukv"] = _heads_merge(gW["w_ukv"], heads, QK_NOPE)
    last = [n for n in BIG if n in gW]
    slots = [_grad_slots(n, gW[n]) for n in last]
    from_sibling = _rs_pair(slots, "rs_pair")
    chip_sums = [_add_pair(pp, rr, "rs_add_" + n, grad_x) for n, pp, rr in zip(last, slots, from_sibling)]

    dm8 = jnp.concatenate([dm_lat.reshape(1, -1), dm_ctx.reshape(1, -1), jnp.zeros((SUB - 2, 6 * D), F32)], axis=0)
    (dmg,) = _all_gather([dm8], "ag_dmod", after=chip_sums[0])
    dm_sum = _sum_slots(dmg, "sum_dmod")
    dM16 = jnp.concatenate([dmg[:, 0, :], dm_sum[1:2], jnp.zeros((7, 6 * D), F32)], axis=0)
    (g_b_mod,) = _rowmap(lambda d: jnp.sum(d, axis=0, keepdims=True), "b_mod_grad", 16, [dM16], [], [], [6 * D])
    dM_loc = lax.dynamic_slice(dM16, (0, me * ncol), (16, ncol))
    g_w_mod = _mm(s16, dM_loc, "tn", F32, "mm_mod_dw")
    ds16_part = _mm(dM_loc, wm, "nt", F32, "mm_mod_dx")

    fine = [n for n in SMALL if n not in ("c_ctx", "b_mod") + S5_BULK]
    small_blocks = [_pack_rows([small[n] for n in fine] + [ds16_part[8:9]], F32), _pack_rows([small[n] for n in S5_BULK], BF16)]
    sm_sems1, sm_thru, sm_token = _ag2_start(small_blocks, [_own_slot(b, me) for b in small_blocks], "ag_small_start")
    grads = {"b_mod": g_b_mod}

    def small_grads(after):
        sems2, thru, token = _ag2_mid(sm_sems1, sm_thru, after, "ag_small_mid")
        sg, sgb = _ag2_end(sm_sems1, sems2, thru, token, "ag_small_end")
        parts = _unpack_rows(_sum_slots(sg, "sum_small"), [small[n].shape for n in fine] + [(1, D)])
        grads.update(zip(fine, parts[:-1]))
        grads.update(zip(S5_BULK, _unpack_rows(_sum_slots(sgb, "sum_small_bulk"), [small[n].shape for n in S5_BULK])))

        def silu_bwd(cc, ds):
            _, vjp = jax.vjp(jax.nn.silu, cc)
            return vjp(ds)[0]

        (grads["c_ctx"],) = _rowmap(silu_bwd, "c_ctx_grad", 1, [c_ctx[None], parts[-1]], [], [(D, F32)])


    my_chip = 2 * lax.axis_index("x") + lax.axis_index("y")
    lands = [_own_slot(lax.dynamic_index_in_dim(q, my_chip, 0, keepdims=False) + sm_token[:1, :1].astype(q.dtype), my_chip, N_CHIP)
             for q in chip_sums]
    rs_send, rs_recv, rs_thru, behind = _xchg_start(chip_sums, lands, True, "rs_chips_start")
    partials = {}
    for names, (send, recv, thru, _) in rs_async.items():
        partials.update(zip(names, _xchg_wait(send, recv, thru, behind, True, "rs_wait_" + names[0])))

    out = {}

    def adamw_big(n, after):
        w2, m2, v2 = a[n][0], a["m_" + n][0], a["v_" + n][0]
        if n in partials:
            g, d, nm, nv = _adamw_slots(w2, partials[n], m2, v2, "adamw_" + n, after)
        else:
            g = g_w_mod
            d, nm, nv = _adamw(w2, g, m2, v2, "adamw_" + n, after)
        for k, val in (("grad_", g), ("delta_", d), ("new_m_", nm), ("new_v_", nv)):
            out[k + n] = val.reshape(a[n].shape)
        return nv

    for n in FFN + MIX + ("w_mod",):
        behind = adamw_big(n, behind)
    small_grads(behind)
    packs = [_pack_rows([t[n] for n in SMALL], F32) for t in (
        {n: a[n] for n in SMALL}, {n: grads[n] for n in SMALL}, {n: a["m_" + n] for n in SMALL}, {n: a["v_" + n] for n in SMALL})]
    res = _adamw(*packs, "adamw_small")
    partials.update(zip(last, _xchg_wait(rs_send, rs_recv, rs_thru, res[2], True, "rs_chips_wait")))
    for n in last:
        adamw_big(n, None)
    shapes = [a[n].shape for n in SMALL]
    for k, packed in (("grad_", packs[1]), ("delta_", res[0]), ("new_m_", res[1]), ("new_v_", res[2])):
        for n, val in zip(SMALL, _unpack_rows(packed, shapes)):
            out[k + n] = val
    return (loss, grad_x[None]) + tuple(out[k + n] for k in ("grad_", "delta_", "new_m_", "new_v_") for n in WEIGHTS)
```

```python
import functools
import math

import jax
import jax.numpy as jnp
from jax import lax
from jax.experimental import pallas as pl
from jax.experimental.pallas import tpu as pltpu

F32 = jnp.float32
BF16 = jnp.bfloat16

N_DEV = 8
N_CHIP = 4
EPS = 1e-6
GRID_W = 64
S5_GROUP = 16
QK_NOPE, QK_ROPE, V_DIM = 128, 64, 128
ROPE_BASE = 10000.0
ADAM_LR, ADAM_B1, ADAM_B2, ADAM_EPS, ADAM_WD, ADAM_STEP = 0.001, 0.9, 0.999, 1e-08, 0.01, 10

LANE = 128
SUB = 8
PACK_W = 1024
PACK_ROWS = 32
VMEM_LIMIT = 48 << 20
ROWMAP_TILE_BYTES = 36 << 20
MM_VMEM_BUDGET = 36 << 20
MESH = pl.DeviceIdType.MESH
_NT = (((1,), (1,)), ((), ()))
_TN = (((0,), (0,)), ((), ()))


def _pick(dim, cands):
    for c in cands:
        if dim % c == 0:
            return c
    return dim


def _cparams(sem):
    return pltpu.CompilerParams(dimension_semantics=sem, vmem_limit_bytes=VMEM_LIMIT)


def _mm(a, b, dims, out_dtype, name, out_slots=None):
    a = a.astype(BF16)
    b = b.astype(BF16)
    b3 = b.ndim == 3
    if dims == "nn":
        (M, K), N = a.shape, (b.shape[0] * b.shape[2] if b3 else b.shape[1])
    elif dims == "nt":
        M, N = a.shape[0], b.shape[-2]
        K = b.shape[0] * b.shape[2] if b3 else b.shape[1]
    else:
        (K, M), N = a.shape, b.shape[1]
    unit_n = b.shape[2] if (b3 and dims == "nn") else (N // out_slots if out_slots else N)
    unit_k = b.shape[2] if (b3 and dims == "nt") else K
    osz = jnp.dtype(out_dtype).itemsize
    tm, tn, tk = _mm_tiles(M, unit_n, unit_k, osz, LANE if dims == "tn" else 16)
    nk, npt, kpt = K // tk, unit_n // tn, unit_k // tk
    use_acc = nk > 1 and out_dtype != F32
    if dims == "nn":
        a_spec = pl.BlockSpec((tm, tk), lambda i, j, k: (i, k))
        b_spec = (pl.BlockSpec((None, tk, tn), lambda i, j, k: (j // npt, k, j % npt)) if b3
                  else pl.BlockSpec((tk, tn), lambda i, j, k: (k, j)))
        dn = (((1,), (0,)), ((), ()))
    elif dims == "nt":
        a_spec = pl.BlockSpec((tm, tk), lambda i, j, k: (i, k))
        b_spec = (pl.BlockSpec((None, tn, tk), lambda i, j, k: (k // kpt, j, k % kpt)) if b3
                  else pl.BlockSpec((tn, tk), lambda i, j, k: (j, k)))
        dn = _NT
    else:
        a_spec = pl.BlockSpec((tk, tm), lambda i, j, k: (k, i))
        b_spec = pl.BlockSpec((tk, tn), lambda i, j, k: (k, j))
        dn = _TN
    if out_slots:
        out_spec = pl.BlockSpec((None, tm, tn), lambda i, j, k: (j // npt, i, j % npt))
        out_shape = jax.ShapeDtypeStruct((out_slots, M, unit_n), out_dtype)
    else:
        out_spec = pl.BlockSpec((tm, tn), lambda i, j, k: (i, j))
        out_shape = jax.ShapeDtypeStruct((M, N), out_dtype)

    def body(a_ref, b_ref, o_ref, *scratch):
        part = lax.dot_general(a_ref[...], b_ref[...], dn, preferred_element_type=F32)
        if nk == 1:
            o_ref[...] = part.astype(o_ref.dtype)
            return
        acc_ref = scratch[0] if use_acc else o_ref
        k = pl.program_id(2)

        @pl.when(k == 0)
        def _():
            acc_ref[...] = part

        @pl.when(k > 0)
        def _():
            acc_ref[...] += part

        if use_acc:
            @pl.when(k == nk - 1)
            def _():
                o_ref[...] = acc_ref[...].astype(o_ref.dtype)

    return pl.pallas_call(
        body, name=name, grid=(M // tm, N // tn, nk),
        in_specs=[a_spec, b_spec], out_specs=out_spec, out_shape=out_shape,
        scratch_shapes=[pltpu.VMEM((tm, tn), F32)] if use_acc else [],
        compiler_params=_cparams(("parallel", "parallel", "arbitrary")),
    )(a, b)


def _mm_swiglu(x, w3, act, name):
    x = x.astype(BF16)
    M, K = x.shape
    ns, _, n = w3.shape
    half = ns // 2
    tm = _divisors(M, 16, 256)[0]

    def body(x_ref, wa_ref, wb_ref, a_ref, b_ref, f_ref):
        a = jnp.dot(x_ref[...], wa_ref[...], preferred_element_type=F32)
        b = jnp.dot(x_ref[...], wb_ref[...], preferred_element_type=F32)
        a_ref[...] = a.astype(a_ref.dtype)
        b_ref[...] = b.astype(b_ref.dtype)
        f_ref[...] = act(a, b).astype(f_ref.dtype)

    out = pl.BlockSpec((tm, n), lambda s, i: (i, s))
    return pl.pallas_call(
        body, name=name, grid=(half, M // tm),
        in_specs=[pl.BlockSpec((tm, K), lambda s, i: (i, 0)), pl.BlockSpec((None, K, n), lambda s, i: (s, 0, 0)),
                  pl.BlockSpec((None, K, n), lambda s, i: (s + half, 0, 0))],
        out_specs=[out] * 3, out_shape=[jax.ShapeDtypeStruct((M, half * n), BF16)] * 3,
        compiler_params=_cparams(("parallel", "parallel")),
    )(x, w3, w3)


def _divisors(n, mult, cap):
    d = [t for t in range(mult, min(n, cap) + 1, mult) if n % t == 0]
    return d[::-1] or [n]


def _mm_tiles(M, unit_n, unit_k, out_itemsize, tm_mult):
    best = None
    for tk in _divisors(unit_k, LANE, 2816):
        for tn in _divisors(unit_n, LANE, 1536):
            for tm in _divisors(M, tm_mult, 1024):
                vmem = 2 * 2 * (tm * tk + tk * tn) + 2 * tm * tn * out_itemsize + 4 * tm * tn * (2 if unit_k > tk else 1)
                if vmem > MM_VMEM_BUDGET:
                    continue
                steps = (M // tm) * (unit_n // tn) * (unit_k // tk)
                key = (steps, -tk, -tn)
                if best is None or key < best[0]:
                    best = (key, (tm, tn, tk))
                break
    return best[1]


def _rowmap(fn, name, M, row_ins, bc_ins, row_outs, acc_outs=(), after=None):
    row_ins = [r if isinstance(r, tuple) else (r, r.shape[1], 0) for r in row_ins]
    row_bytes = sum(w * a.dtype.itemsize for a, w, _ in row_ins) + sum(w * jnp.dtype(d).itemsize for w, d in row_outs)
    widest = max([w for _, w, _ in row_ins] + [w for w, _ in row_outs])
    row_bytes = 2 * row_bytes + 6 * 4 * widest
    tm = _pick(M, [t for t in (512, 256, 128, 64, 32, 16) if t * row_bytes <= ROWMAP_TILE_BYTES] + [16])
    n_in, n_row, n_acc = len(row_ins) + len(bc_ins), len(row_outs), len(acc_outs)

    def body(*refs):
        res = fn(*[r[...].astype(F32) for r in refs[:n_in]])
        res = res if isinstance(res, (tuple, list)) else (res,)
        outs = refs[n_in + (after is not None):]
        for k in range(n_row):
            outs[k][...] = res[k].astype(outs[k].dtype)
        if n_acc:
            @pl.when(pl.program_id(0) == 0)
            def _():
                for k in range(n_acc):
                    outs[n_row + k][...] = jnp.zeros_like(outs[n_row + k])

            for k in range(n_acc):
                outs[n_row + k][...] += res[n_row + k].astype(F32)

    in_specs = [pl.BlockSpec((tm, w), functools.partial(lambda i, blk: (i, blk), blk=blk)) for _, w, blk in row_ins]
    in_specs += [pl.BlockSpec(b.shape, lambda i: (0, 0)) for b in bc_ins]
    in_specs += [pl.BlockSpec(memory_space=pl.ANY)] * (after is not None)
    out_specs = [pl.BlockSpec((tm, w), lambda i: (i, 0)) for w, _ in row_outs]
    out_specs += [pl.BlockSpec((1, w), lambda i: (0, 0)) for w in acc_outs]
    out_shape = [jax.ShapeDtypeStruct((M, w), d) for w, d in row_outs]
    out_shape += [jax.ShapeDtypeStruct((1, w), F32) for w in acc_outs]
    return pl.pallas_call(
        body, name=name, grid=(M // tm,), in_specs=in_specs, out_specs=out_specs, out_shape=out_shape,
        compiler_params=_cparams(("arbitrary",) if n_acc else ("parallel",)),
    )(*[a for a, _, _ in row_ins], *bc_ins, *([] if after is None else [after]))


def _rms(x, g):
    return x * lax.rsqrt(jnp.mean(x * x, axis=-1, keepdims=True) + EPS) * g


def _normmod(x, g, sc, sh):
    return _rms(x, g) * (1.0 + sc) + sh


def _swap16(v):
    w = v.shape[1]
    lane = lax.broadcasted_iota(jnp.int32, v.shape, 1)
    return jnp.where((lane // 16) % 2 == 0, pltpu.roll(v, w - 16, 1), pltpu.roll(v, 16, 1))


def _rope(v, cos, sin_signed):
    return v * cos + _swap16(v) * sin_signed


def _rope_bwd(d, cos, sin_signed):
    return d * cos + _swap16(d * sin_signed)


def _mesh_pos():
    return lax.axis_index("x"), lax.axis_index("y"), lax.axis_index("c")


def _hbm_call(body, name, ins, out_shapes, n_sems):
    any_spec = pl.BlockSpec(memory_space=pl.ANY)
    return pl.pallas_call(
        body, name=name, out_shape=out_shapes, in_specs=[any_spec] * len(ins), out_specs=[any_spec] * len(out_shapes),
        scratch_shapes=[pltpu.SemaphoreType.DMA((n_sems,)), pltpu.SemaphoreType.DMA((n_sems,)),
                        pltpu.SemaphoreType.DMA((len(ins),))],
    )(*ins)


def _all_gather(xs, name, after=None):
    n = len(xs)

    def body(*refs):
        k = n + (after is not None)
        x_refs, out_refs, (send_sems, recv_sems, local_sems) = refs[:n], refs[k:k + n], refs[k + n:]
        x, y, c = _mesh_pos()
        me, sibling = (x, y, c), (x, y, 1 - c)
        chips = [(1 - x, y), (x, 1 - y), (1 - x, 1 - y)]
        locals_, first, passed, arrivals = [], [], [], []
        for a in range(n):
            def slot(px, py, pc, a=a):
                return out_refs[a].at[4 * px + 2 * py + pc]

            def copy(k, block, to, src=None, a=a, slot=slot):
                return pltpu.make_async_remote_copy(
                    src_ref=slot(*block) if src is None else src, dst_ref=slot(*block),
                    send_sem=send_sems.at[7 * a + k], recv_sem=recv_sems.at[7 * a + k], device_id=to, device_id_type=MESH)

            locals_.append(pltpu.make_async_copy(x_refs[a], slot(*me), local_sems.at[a]))
            first.append(copy(0, me, sibling, src=x_refs[a]))
            first += [copy(1 + j, me, (*chip, c), src=x_refs[a]) for j, chip in enumerate(chips)]
            passed.append([copy(4 + j, (*chip, c), sibling) for j, chip in enumerate(chips)])
            arrivals.append([copy(1 + j, (*chip, c), me) for j, chip in enumerate(chips)]
                            + [copy(0, sibling, me)] + [copy(4 + j, (*chip, 1 - c), me) for j, chip in enumerate(chips)])
        for cp in locals_ + first:
            cp.start()
        for j in range(3):
            for a in range(n):
                arrivals[a][j].wait_recv()
                passed[a][j].start()
        for a in range(n):
            for cp in arrivals[a][3:]:
                cp.wait_recv()
        for cp in first + [p for ps in passed for p in ps]:
            cp.wait_send()
        for cp in locals_:
            cp.wait()

    return _hbm_call(body, name, list(xs) + ([] if after is None else [after]),
                     [jax.ShapeDtypeStruct((N_DEV,) + x.shape, x.dtype) for x in xs], 7 * n)


def _rs_pair(ps, name):
    n = len(ps)

    def body(*refs):
        p_refs, out_refs, (send_sems, recv_sems, _) = refs[:n], refs[n:2 * n], refs[2 * n:]
        x, y, c = _mesh_pos()
        sends, recvs = [], []
        for a in range(n):
            for q in range(N_CHIP):
                sem = dict(send_sem=send_sems.at[4 * a + q], recv_sem=recv_sems.at[4 * a + q],
                           device_id=(x, y, 1 - c), device_id_type=MESH)
                sends.append(pltpu.make_async_remote_copy(src_ref=p_refs[a].at[2 * q + 1 - c], dst_ref=out_refs[a].at[q], **sem))
                recvs.append(pltpu.make_async_remote_copy(src_ref=p_refs[a].at[2 * q + c], dst_ref=out_refs[a].at[q], **sem))
        for cp in sends:
            cp.start()
        for cp in recvs:
            cp.wait_recv()
        for cp in sends:
            cp.wait_send()

    return _hbm_call(body, name, ps, [jax.ShapeDtypeStruct((N_CHIP,) + p.shape[1:], p.dtype) for p in ps], 4 * n)


def _xchg_copies(src_refs, land_refs, send_sems, recv_sems, slot_src):
    x, y, c = _mesh_pos()
    sends, recvs = [], []
    for a, (src, land) in enumerate(zip(src_refs, land_refs)):
        chips = land.shape[0] == N_CHIP
        npeer = land.shape[0] - 1
        me = 2 * x + y if chips else 4 * x + 2 * y + c
        for r in range(1, npeer + 1):
            px = 1 - x if r & (2 if chips else 4) else x
            py = 1 - y if r & (1 if chips else 2) else y
            pc = c if chips else (1 - c if r & 1 else c)
            peer = 2 * px + py if chips else 4 * px + 2 * py + pc
            sem = dict(send_sem=send_sems.at[npeer * a + r - 1], recv_sem=recv_sems.at[npeer * a + r - 1],
                       device_id=(px, py, pc), device_id_type=MESH)
            s = src.at[peer] if slot_src else src
            sends.append(pltpu.make_async_remote_copy(src_ref=s, dst_ref=land.at[me], **sem))
            recvs.append(pltpu.make_async_remote_copy(src_ref=s, dst_ref=land.at[peer], **sem))
    return sends, recvs


_HBM = pl.BlockSpec(memory_space=pltpu.HBM)
_SEM = pl.BlockSpec(memory_space=pltpu.SEMAPHORE)
_EFFECT = pltpu.SideEffectType.DATAFLOW_SIDE_EFFECTING


def _xchg_start(srcs, lands, slot_src, name):
    n = len(srcs)

    def body(*refs):
        sends, _ = _xchg_copies(refs[:n], refs[n:2 * n], refs[2 * n], refs[2 * n + 1], slot_src)
        for cp in sends:
            cp.start()
        refs[-1][...] = jnp.zeros_like(refs[-1])

    bufs = list(srcs) + list(lands)
    n_sems = n * (lands[0].shape[0] - 1)
    res = pl.pallas_call(
        body, name=name,
        out_shape=(pltpu.SemaphoreType.DMA((n_sems,)), pltpu.SemaphoreType.DMA((n_sems,)))
        + tuple(pltpu.HBM(b.shape, b.dtype) for b in bufs) + (jax.ShapeDtypeStruct((SUB, LANE), F32),),
        in_specs=(_HBM,) * (2 * n), out_specs=(_SEM, _SEM) + (_HBM,) * (2 * n) + (pl.BlockSpec(memory_space=pltpu.VMEM),),
        input_output_aliases={i: 2 + i for i in range(2 * n)},
        compiler_params=pltpu.CompilerParams(has_side_effects=_EFFECT),
    )(*[pltpu.with_memory_space_constraint(b, pltpu.HBM) for b in bufs])
    return res[0], res[1], res[2:-1], res[-1]


def _xchg_wait(send_sems, recv_sems, thru, after, slot_src, name):
    n = len(thru) // 2

    def body(*refs):
        sends, recvs = _xchg_copies(refs[:n], refs[n:2 * n], refs[2 * n], refs[2 * n + 1], slot_src)
        for cp in sends:
            cp.wait_send()
        for cp in recvs:
            cp.wait_recv()

    res = pl.pallas_call(
        body, name=name, out_shape=tuple(pltpu.HBM(b.shape, b.dtype) for b in thru),
        in_specs=(_HBM,) * (2 * n) + (_SEM, _SEM, pl.BlockSpec(memory_space=pl.ANY)), out_specs=(_HBM,) * (2 * n),
        input_output_aliases={i: i for i in range(2 * n)},
        compiler_params=pltpu.CompilerParams(has_side_effects=_EFFECT),
    )(*thru, send_sems, recv_sems, after)
    return res[n:]


def _ag2_copy(land, sems, k, block, to, src=None):
    slot = land.at[4 * block[0] + 2 * block[1] + block[2]]
    return pltpu.make_async_remote_copy(src_ref=slot if src is None else src, dst_ref=slot, send_sem=sems[0].at[k],
                                        recv_sem=sems[1].at[k], device_id=to, device_id_type=MESH)


def _ag2_start(blocks, lands, name):
    n = len(blocks)

    def body(*refs):
        x, y, c = _mesh_pos()
        for a in range(n):
            sems = (refs[2 * n], refs[2 * n + 1])
            _ag2_copy(refs[n + a], sems, 4 * a, (x, y, c), (x, y, 1 - c), src=refs[a]).start()
            for j, chip in enumerate([(1 - x, y), (x, 1 - y), (1 - x, 1 - y)]):
                _ag2_copy(refs[n + a], sems, 4 * a + 1 + j, (x, y, c), (*chip, c), src=refs[a]).start()
        refs[-1][...] = jnp.zeros_like(refs[-1])

    bufs = list(blocks) + list(lands)
    res = pl.pallas_call(
        body, name=name,
        out_shape=(pltpu.SemaphoreType.DMA((4 * n,)), pltpu.SemaphoreType.DMA((4 * n,)))
        + tuple(pltpu.HBM(b.shape, b.dtype) for b in bufs) + (jax.ShapeDtypeStruct((SUB, LANE), F32),),
        in_specs=(_HBM,) * (2 * n), out_specs=(_SEM, _SEM) + (_HBM,) * (2 * n) + (pl.BlockSpec(memory_space=pltpu.VMEM),),
        input_output_aliases={i: 2 + i for i in range(2 * n)},
        compiler_params=pltpu.CompilerParams(has_side_effects=_EFFECT),
    )(*[pltpu.with_memory_space_constraint(b, pltpu.HBM) for b in bufs])
    return (res[0], res[1]), res[2:-1], res[-1]


def _ag2_mid(sems1, thru, after, name):
    n = len(thru) // 2

    def body(*refs):
        x, y, c = _mesh_pos()
        s1, s2 = (refs[2 * n], refs[2 * n + 1]), (refs[2 * n + 3], refs[2 * n + 4])
        for j, chip in enumerate([(1 - x, y), (x, 1 - y), (1 - x, 1 - y)]):
            for a in range(n):
                _ag2_copy(refs[n + a], s1, 4 * a + 1 + j, (*chip, c), (x, y, c)).wait_recv()
                _ag2_copy(refs[n + a], s2, 3 * a + j, (*chip, c), (x, y, 1 - c)).start()
        refs[-1][...] = jnp.zeros_like(refs[-1])

    res = pl.pallas_call(
        body, name=name,
        out_shape=(pltpu.SemaphoreType.DMA((3 * n,)), pltpu.SemaphoreType.DMA((3 * n,)))
        + tuple(pltpu.HBM(b.shape, b.dtype) for b in thru) + (jax.ShapeDtypeStruct((SUB, LANE), F32),),
        in_specs=(_HBM,) * (2 * n) + (_SEM, _SEM, pl.BlockSpec(memory_space=pl.ANY)),
        out_specs=(_SEM, _SEM) + (_HBM,) * (2 * n) + (pl.BlockSpec(memory_space=pltpu.VMEM),),
        input_output_aliases={i: 2 + i for i in range(2 * n)},
        compiler_params=pltpu.CompilerParams(has_side_effects=_EFFECT),
    )(*thru, *sems1, after)
    return (res[0], res[1]), res[2:-1], res[-1]


def _ag2_end(sems1, sems2, thru, after, name):
    n = len(thru) // 2

    def body(*refs):
        x, y, c = _mesh_pos()
        s1, s2 = (refs[2 * n], refs[2 * n + 1]), (refs[2 * n + 2], refs[2 * n + 3])
        chips = [(1 - x, y), (x, 1 - y), (1 - x, 1 - y)]
        for a in range(n):
            land = refs[n + a]
            _ag2_copy(land, s1, 4 * a, (x, y, c), (x, y, 1 - c), src=refs[a]).wait_send()
            _ag2_copy(land, s1, 4 * a, (x, y, 1 - c), (x, y, c)).wait_recv()
            for j, chip in enumerate(chips):
                _ag2_copy(land, s1, 4 * a + 1 + j, (x, y, c), (*chip, c), src=refs[a]).wait_send()
                _ag2_copy(land, s2, 3 * a + j, (*chip, c), (x, y, 1 - c)).wait_send()
                _ag2_copy(land, s2, 3 * a + j, (*chip, 1 - c), (x, y, c)).wait_recv()

    res = pl.pallas_call(
        body, name=name, out_shape=tuple(pltpu.HBM(b.shape, b.dtype) for b in thru),
        in_specs=(_HBM,) * (2 * n) + (_SEM,) * 4 + (pl.BlockSpec(memory_space=pl.ANY),), out_specs=(_HBM,) * (2 * n),
        input_output_aliases={i: i for i in range(2 * n)},
        compiler_params=pltpu.CompilerParams(has_side_effects=_EFFECT),
    )(*thru, *sems1, *sems2, after)
    return res[n:]


def _own_slot(block, me, slots=N_DEV):
    return lax.dynamic_update_slice(lax.empty((slots,) + block.shape, block.dtype), block[None], (me, 0, 0))


def _add_pair(p, r, name, after):
    _, R, C = p.shape
    tr = _pick(R, (512, 256, 128, 64, 32, 16))

    def body(c_ref, p_ref, r_ref, after_ref, o_ref):
        o_ref[...] = (p_ref[...].astype(F32) + r_ref[...].astype(F32)).astype(o_ref.dtype)

    return pl.pallas_call(
        body, name=name, out_shape=jax.ShapeDtypeStruct((N_CHIP, R, C), p.dtype),
        grid_spec=pltpu.PrefetchScalarGridSpec(
            num_scalar_prefetch=1, grid=(N_CHIP, R // tr),
            in_specs=[pl.BlockSpec((None, None, tr, C), lambda q, i, c_ref: (q, c_ref[0], i, 0)),
                      pl.BlockSpec((None, tr, C), lambda q, i, c_ref: (q, i, 0)), pl.BlockSpec(memory_space=pl.ANY)],
            out_specs=pl.BlockSpec((None, tr, C), lambda q, i, c_ref: (q, i, 0))),
        compiler_params=_cparams(("parallel", "parallel")),
    )(lax.axis_index("c").reshape(1).astype(jnp.int32), p.reshape(N_CHIP, 2, R, C), r, after)


def _sum_slots(g, name):
    ns, R, C = g.shape
    tr = _pick(R, (256, 128, 64, 32, 16))

    def body(g_ref, o_ref):
        acc = g_ref[0].astype(F32)
        for j in range(1, ns):
            acc = acc + g_ref[j].astype(F32)
        o_ref[...] = acc

    return pl.pallas_call(
        body, name=name, grid=(R // tr,),
        in_specs=[pl.BlockSpec((ns, tr, C), lambda i: (0, i, 0))], out_specs=pl.BlockSpec((tr, C), lambda i: (i, 0)),
        out_shape=jax.ShapeDtypeStruct((R, C), F32), compiler_params=_cparams(("parallel",)),
    )(g)


def _pack_rows(arrs, dtype):
    parts = []
    for a in arrs:
        flat = a.reshape(-1).astype(dtype)
        pad = (-flat.shape[0]) % (PACK_W * 16)
        parts.append(jnp.pad(flat, (0, pad)).reshape(-1, PACK_W))
    out = jnp.concatenate(parts, axis=0)
    return jnp.pad(out, ((0, (-out.shape[0]) % PACK_ROWS), (0, 0)))


def _packed_rows(shape):
    n = math.prod(shape)
    return (n + PACK_W * 16 - 1) // (PACK_W * 16) * 16


def _unpack_rows(packed, shapes):
    out, r0 = [], 0
    for s in shapes:
        rows, n = _packed_rows(s), math.prod(s)
        out.append(packed[r0:r0 + rows].reshape(rows * PACK_W)[:n].reshape(s))
        r0 += rows
    return out


def _adamw_math(w, g, m, v):
    m = ADAM_B1 * m + (1.0 - ADAM_B1) * g
    v = ADAM_B2 * v + (1.0 - ADAM_B2) * (g * g)
    m_hat = m / (1.0 - ADAM_B1 ** ADAM_STEP)
    v_hat = v / (1.0 - ADAM_B2 ** ADAM_STEP)
    delta = -ADAM_LR * (m_hat / (jnp.sqrt(v_hat) + ADAM_EPS) + ADAM_WD * w)
    return delta, m, v


def _adamw_slots(w, gs, m, v, name, after=None):
    ns, R, C = gs.shape
    row_bytes = 2 * (ns * C * gs.dtype.itemsize + 7 * C * 4) + 6 * 4 * C
    tr = _pick(R, [t for t in (512, 256, 128, 64, 32, 16) if t * row_bytes <= ROWMAP_TILE_BYTES] + [16])

    def body(w_ref, g_ref, m_ref, v_ref, *rest):
        outs = rest[(after is not None):]
        g = g_ref[0].astype(F32)
        for j in range(1, ns):
            g = g + g_ref[j].astype(F32)
        res = (g,) + _adamw_math(w_ref[...], g, m_ref[...], v_ref[...])
        for o_ref, val in zip(outs, res):
            o_ref[...] = val

    row = pl.BlockSpec((tr, C), lambda i: (i, 0))
    return pl.pallas_call(
        body, name=name, grid=(R // tr,),
        in_specs=[row, pl.BlockSpec((ns, tr, C), lambda i: (0, i, 0)), row, row]
        + [pl.BlockSpec(memory_space=pl.ANY)] * (after is not None),
        out_specs=[row] * 4, out_shape=[jax.ShapeDtypeStruct((R, C), F32)] * 4, compiler_params=_cparams(("parallel",)),
    )(w, gs, m, v, *([] if after is None else [after]))


def _adamw(w, g, m, v, name, after=None):
    R, C = w.shape
    return _rowmap(_adamw_math, name, R, [w, g, m, v], [], [(C, F32)] * 3, after=after)


def _s5_disc_math(lr, li, ldt, br, bi):
    dt = jnp.exp(ldt)
    mag = jnp.exp(lr * dt)
    ab_re, ab_im = mag * jnp.cos(li * dt), mag * jnp.sin(li * dt)
    den = lr * lr + li * li
    nr, ni = ab_re - 1.0, ab_im
    co_re = (nr * lr + ni * li) / den
    co_im = (ni * lr - nr * li) / den
    bb_re = co_re * br - co_im * bi
    bb_im = co_re * bi + co_im * br
    return ab_re, ab_im, bb_re, bb_im


def _s5_tables(a_re, a_im, ldt, b_re, b_im, c_re, c_im):
    _, G, P, N = b_re.shape
    nch = G // 8

    def body(lr_ref, li_ref, ldt_ref, br_ref, bi_ref, cr_ref, ci_ref, wre, wim, vre, vim, pwr, pwi):
        ar, ai, bb_re, bb_im = _s5_disc_math(lr_ref[0], li_ref[0], ldt_ref[0], br_ref[0], bi_ref[0])
        cr, ci = cr_ref[0], ci_ref[0]
        pr, pi = jnp.ones_like(ar), jnp.zeros_like(ar)
        for j in range(SUB + 1):
            pwr[0, j], pwi[0, j] = pr, pi
            if j < SUB:
                tabs = ((wre, bb_re * pr - bb_im * pi), (wim, bb_re * pi + bb_im * pr),
                        (vre, cr * pr - ci * pi), (vim, -(cr * pi + ci * pr)))
                for ref, val in tabs:
                    for s in range(nch):
                        ref[0, s, pl.ds(j * LANE, LANE), :] = val[s * 8:(s + 1) * 8].reshape(LANE, N).astype(BF16)
            pr, pi = pr * ar - pi * ai, pr * ai + pi * ar

    g1n = pl.BlockSpec((1, G, 1, N), lambda d: (d, 0, 0, 0))
    gpn = pl.BlockSpec((1, G, P, N), lambda d: (d, 0, 0, 0))
    tab = pl.BlockSpec((1, nch, SUB * LANE, N), lambda d: (d, 0, 0, 0))
    pw = pl.BlockSpec((1, SUB + 1, G, 1, N), lambda d: (d, 0, 0, 0, 0))
    s_tab = jax.ShapeDtypeStruct((2, nch, SUB * LANE, N), BF16)
    s_pw = jax.ShapeDtypeStruct((2, SUB + 1, G, 1, N), F32)
    return pl.pallas_call(
        body, name="s5_tables", grid=(2,),
        in_specs=[g1n, g1n, pl.BlockSpec((1, G, 1, 1), lambda d: (d, 0, 0, 0)), gpn, gpn, gpn, gpn],
        out_specs=[tab] * 4 + [pw] * 2, out_shape=[s_tab] * 4 + [s_pw] * 2,
        compiler_params=_cparams(("parallel",)),
    )(a_re, a_im, ldt, b_re, b_im, c_re, c_im)


def _s5_expand(t_re, t_im, name):
    _, nch, R, N = t_re.shape
    sw = 8 * N

    def body(re_ref, im_ref, o_ref):
        spread = (lax.broadcasted_iota(jnp.int32, (N, sw), 1) % N == lax.broadcasted_iota(jnp.int32, (N, sw), 0)).astype(BF16)
        row_g = (lax.broadcasted_iota(jnp.int32, (R, sw), 0) % LANE) // S5_GROUP
        keep = row_g == lax.broadcasted_iota(jnp.int32, (R, sw), 1) // N
        for half, ref in enumerate((re_ref, im_ref)):
            t = jnp.dot(ref[0, 0], spread, preferred_element_type=F32)
            o_ref[0, 0, :, pl.ds(half * sw, sw)] = jnp.where(keep, t, 0.0).astype(BF16)

    spec = pl.BlockSpec((1, 1, R, N), lambda d, s: (d, s, 0, 0))
    return pl.pallas_call(
        body, name=name, grid=(2, nch), in_specs=[spec, spec],
        out_specs=pl.BlockSpec((1, 1, R, 2 * sw), lambda d, s: (d, s, 0, 0)),
        out_shape=jax.ShapeDtypeStruct((2, nch, R, 2 * sw), BF16), compiler_params=_cparams(("parallel", "parallel")),
    )(t_re, t_im)


def _s5_param_bwd(a_re, a_im, ldt, b_re, b_im, da_re, da_im, dbb_re, dbb_im):
    _, G, P, N = b_re.shape

    def body(lr_ref, li_ref, ldt_ref, br_ref, bi_ref, dar, dai, dbr, dbi, o_lr, o_li, o_ldt, o_br, o_bi):
        _, vjp = jax.vjp(_s5_disc_math, lr_ref[0], li_ref[0], ldt_ref[0], br_ref[0], bi_ref[0])
        o_lr[0], o_li[0], o_ldt[0], o_br[0], o_bi[0] = vjp((dar[0], dai[0], dbr[0], dbi[0]))

    g1n = pl.BlockSpec((1, G, 1, N), lambda d: (d, 0, 0, 0))
    g11 = pl.BlockSpec((1, G, 1, 1), lambda d: (d, 0, 0, 0))
    gpn = pl.BlockSpec((1, G, P, N), lambda d: (d, 0, 0, 0))
    s_g1n, s_g11, s_gpn = (jax.ShapeDtypeStruct(s, F32) for s in ((2, G, 1, N), (2, G, 1, 1), (2, G, P, N)))
    return pl.pallas_call(
        body, name="s5_param_bwd", grid=(2,),
        in_specs=[g1n, g1n, g11, gpn, gpn, g1n, g1n, gpn, gpn], out_specs=[g1n, g1n, g11, gpn, gpn],
        out_shape=[s_g1n, s_g1n, s_g11, s_gpn, s_gpn], compiler_params=_cparams(("parallel",)),
    )(a_re, a_im, ldt, b_re, b_im, da_re, da_im, dbb_re, dbb_im)


def _tile_local_scan(u, w_ref, back):
    tb, sw2 = u.shape[0], w_ref.shape[1]
    sw, half = sw2 // 2, LANE // 2
    tau = lax.broadcasted_iota(jnp.int32, u.shape, 0) % SUB
    low = lax.broadcasted_iota(jnp.int32, u.shape, 1) < half
    parts = [u]
    for j in range(1, SUB):
        if back:
            parts.append(jnp.where(tau >= j, pltpu.roll(u, j, 0), 0.0))
        else:
            parts.append(jnp.where(tau <= SUB - 1 - j, pltpu.roll(u, tb - j, 0), 0.0))
    out = [None] * 4
    for h in range(2):
        pieces = [jnp.where(low, a, pltpu.roll(b, half, 1)) if h == 0 else jnp.where(low, pltpu.roll(a, half, 1), b)
                  for a, b in zip(parts[0::2], parts[1::2])]
        lhs = jnp.concatenate(pieces, axis=1).astype(BF16)
        rows = jnp.concatenate([w_ref[pl.ds(j * LANE + h * half, half), :] for j in range(SUB)], axis=0)
        for part in range(2):
            cols = rows[:, part * sw + h * (sw // 2):part * sw + (h + 1) * (sw // 2)]
            out[2 * part + h] = jnp.dot(lhs, cols, preferred_element_type=F32)
    return jnp.concatenate(out, axis=1)


def _cmul_add(tile, pw, carry, sw):
    pr, pi, cr, ci = pw[:, :sw], pw[:, sw:], carry[:, :sw], carry[:, sw:]
    return tile + jnp.concatenate([pr * cr - pi * ci, pr * ci + pi * cr], axis=1)


def _tile_scan(buf, base, ntile, pw, carry, sw, causal):
    def step(k, c):
        i = k if causal else ntile - 1 - k
        r = pl.multiple_of(base + i * SUB, SUB)
        tile = _cmul_add(buf[pl.ds(r, SUB), :], pw, c, sw)
        buf[pl.ds(r, SUB), :] = tile
        return tile[SUB - 1:SUB, :] if causal else tile[0:1, :]

    return lax.fori_loop(0, ntile, step, carry)


def _s5_fwd(h_all, waug, vaug, pw, S5W, T, d, causal, name):
    S = h_all.shape[0]
    _, nch, _, sw2 = waug.shape
    sw = sw2 // 2
    tb = _pick(math.gcd(T, S - T), (256, 128, 64, 32, 16))
    ntile, nt, off = tb // SUB, S // tb, T // tb
    rb = (lambda s, t: ((t + off) % nt, s)) if causal else (lambda s, t: (nt - 1 - t, s))

    def body(u_ref, w_ref, v_ref, p_ref, y_ref, h_ref, hblk, carry):
        @pl.when(pl.program_id(1) == 0)
        def _():
            carry[...] = jnp.zeros_like(carry)

        hblk[...] = _tile_local_scan(u_ref[...], w_ref, causal)
        carry[...] = _tile_scan(hblk, 0, ntile, p_ref[...], carry[...], sw, causal)
        hb = hblk[...].astype(BF16)
        h_ref[...] = hb
        y_ref[...] = lax.dot_general(hb, v_ref[...], _NT, preferred_element_type=F32)

    return pl.pallas_call(
        body, name=name, grid=(nch, nt),
        in_specs=[pl.BlockSpec((tb, LANE), rb),
                  pl.BlockSpec((None, None, SUB * LANE, sw2), lambda s, t: (d, s, 0, 0)),
                  pl.BlockSpec((None, None, LANE, sw2), lambda s, t: (d, s, 0, 0)),
                  pl.BlockSpec((None, SUB, sw2), lambda s, t: (s, 0, 0))],
        out_specs=[pl.BlockSpec((tb, LANE), rb), pl.BlockSpec((tb, sw2), rb)],
        out_shape=[jax.ShapeDtypeStruct((S, S5W), F32), jax.ShapeDtypeStruct((S, nch * sw2), BF16)],
        scratch_shapes=[pltpu.VMEM((tb, sw2), F32), pltpu.VMEM((1, sw2), F32)],
        compiler_params=_cparams(("parallel", "arbitrary")),
    )(h_all, waug, vaug, pw)


def _s5_bwd(dy_all, h_all, hs, waug, vaug, pwc, S5W, T, d, causal, name):
    S = h_all.shape[0]
    _, nch, _, sw2 = waug.shape
    sw = sw2 // 2
    tb = _pick(math.gcd(T, S - T), (256, 128, 64, 32, 16))
    ntile, nt, off = tb // SUB, S // tb, T // tb
    rb = (lambda s, t: ((nt - 1 - t + off) % nt, s)) if causal else (lambda s, t: (t, s))
    adj_causal = not causal
    edge = SUB - 1 if adj_causal else SUB + tb
    keep_src, keep_dst = (tb, 0) if adj_causal else (SUB, SUB + tb)

    def body(dy_ref, u_ref, h_ref, w_ref, v_ref, p_ref, du_ref, dbb_ref, dc_ref, da_ref, lam):
        @pl.when(pl.program_id(1) == 0)
        def _():
            lam[pl.ds(0, SUB), :] = jnp.zeros((SUB, sw2), F32)
            lam[pl.ds(SUB + tb, SUB), :] = jnp.zeros((SUB, sw2), F32)
            dbb_ref[...] = jnp.zeros_like(dbb_ref)
            dc_ref[...] = jnp.zeros_like(dc_ref)
            da_ref[...] = jnp.zeros_like(da_ref)

        dy = dy_ref[...]
        lam[pl.ds(SUB, tb), :] = _tile_local_scan(dy, v_ref, adj_causal)
        _tile_scan(lam, SUB, ntile, p_ref[...], lam[pl.ds(edge, 1), :], sw, adj_causal)
        lb = lam[pl.ds(SUB, tb), :].astype(BF16)
        du_ref[...] = lax.dot_general(lb, w_ref[...], _NT, preferred_element_type=F32)
        dbb_ref[...] += lax.dot_general(u_ref[...].astype(BF16), lb, _TN, preferred_element_type=F32)
        dc_ref[...] += lax.dot_general(h_ref[...], dy.astype(BF16), _TN, preferred_element_type=F32)
        h = h_ref[...].astype(F32)
        ln = lam[pl.ds(SUB + 1 if causal else SUB - 1, tb), :]
        hr, hi, lr, li = h[:, :sw], h[:, sw:], ln[:, :sw], ln[:, sw:]
        da_ref[...] += jnp.concatenate([jnp.sum(hr * lr + hi * li, axis=0, keepdims=True),
                                        jnp.sum(hr * li - hi * lr, axis=0, keepdims=True)], axis=1)
        lam[pl.ds(keep_dst, SUB), :] = lam[pl.ds(keep_src, SUB), :]

    fixed = lambda s, t: (s, 0, 0)
    return pl.pallas_call(
        body, name=name, grid=(nch, nt),
        in_specs=[pl.BlockSpec((tb, LANE), rb), pl.BlockSpec((tb, LANE), rb), pl.BlockSpec((tb, sw2), rb),
                  pl.BlockSpec((None, None, LANE, sw2), lambda s, t: (d, s, 0, 0)),
                  pl.BlockSpec((None, None, SUB * LANE, sw2), lambda s, t: (d, s, 0, 0)),
                  pl.BlockSpec((None, SUB, sw2), fixed)],
        out_specs=[pl.BlockSpec((tb, LANE), rb), pl.BlockSpec((None, LANE, sw2), fixed),
                   pl.BlockSpec((None, sw2, LANE), fixed), pl.BlockSpec((None, 1, sw2), fixed)],
        out_shape=[jax.ShapeDtypeStruct((S, S5W), F32), jax.ShapeDtypeStruct((nch, LANE, sw2), F32),
                   jax.ShapeDtypeStruct((nch, sw2, LANE), F32), jax.ShapeDtypeStruct((nch, 1, sw2), F32)],
        scratch_shapes=[pltpu.VMEM((tb + 2 * SUB, sw2), F32)],
        compiler_params=_cparams(("parallel", "arbitrary")),
    )(dy_all, h_all, hs, waug, vaug, pwc)


def _attn_fwd(qn, qr, kv, kr, H, scale):
    T, S = qn.shape[0], kv.shape[0]
    tq = _pick(T, (256, 128, 64, 32, 16))

    def body(qn_ref, qr_ref, kn_ref, v_ref, kr_ref, o_ref, lse_ref):
        q = jnp.concatenate([qn_ref[...], qr_ref[...]], axis=1)
        k = jnp.concatenate([kn_ref[...], kr_ref[...]], axis=1)
        s = lax.dot_general(q, k, _NT, preferred_element_type=F32) * scale
        m = jnp.max(s, axis=1, keepdims=True)
        p = jnp.exp(s - m)
        l = jnp.sum(p, axis=1, keepdims=True)
        o_ref[...] = jnp.dot((p * (1.0 / l)).astype(BF16), v_ref[...], preferred_element_type=F32).astype(o_ref.dtype)
        lse_ref[0] = m + jnp.log(l)

    q_spec = pl.BlockSpec((tq, LANE), lambda h, i: (i, h))
    return pl.pallas_call(
        body, name="attn_fwd", grid=(H, T // tq),
        in_specs=[q_spec, q_spec, pl.BlockSpec((S, LANE), lambda h, i: (0, h)), pl.BlockSpec((S, LANE), lambda h, i: (0, H + h)),
                  pl.BlockSpec((S, LANE), lambda h, i: (0, 0))],
        out_specs=[q_spec, pl.BlockSpec((1, tq, 1), lambda h, i: (h, i, 0))],
        out_shape=[jax.ShapeDtypeStruct((T, H * LANE), BF16), jax.ShapeDtypeStruct((H, T, 1), F32)],
        compiler_params=_cparams(("parallel", "parallel")),
    )(qn, qr, kv, kv, kr)


def _attn_bwd(qn, qr, kv, kr, do, lse, H, scale):
    T, S = qn.shape[0], kv.shape[0]
    tq = _pick(T, (512, 256, 128, 64, 32, 16))
    nq = T // tq

    def body(qn_ref, qr_ref, kn_ref, v_ref, kr_ref, do_ref, lse_ref, dqn_ref, dqr_ref, dkn_ref, dkr_ref, dv_ref, dk_acc, dv_acc):
        i = pl.program_id(1)
        q = jnp.concatenate([qn_ref[...], qr_ref[...]], axis=1)
        k = jnp.concatenate([kn_ref[...], kr_ref[...]], axis=1)
        v, d_o = v_ref[...], do_ref[...]
        s = lax.dot_general(q, k, _NT, preferred_element_type=F32) * scale
        p = jnp.exp(s - lse_ref[0])
        dv_part = lax.dot_general(p.astype(BF16), d_o, _TN, preferred_element_type=F32)
        dp = lax.dot_general(d_o, v, _NT, preferred_element_type=F32)
        ds = (p * (dp - jnp.sum(p * dp, axis=1, keepdims=True)) * scale).astype(BF16)
        dq = jnp.dot(ds, k, preferred_element_type=F32)
        dqn_ref[...] = dq[:, :LANE].astype(dqn_ref.dtype)
        dqr_ref[...] = dq[:, LANE:].astype(dqr_ref.dtype)
        dk_part = lax.dot_general(ds, q, _TN, preferred_element_type=F32)

        @pl.when(i == 0)
        def _():
            dk_acc[...] = dk_part
            dv_acc[...] = dv_part

        @pl.when(i > 0)
        def _():
            dk_acc[...] += dk_part
            dv_acc[...] += dv_part

        @pl.when(i == nq - 1)
        def _():
            dkn_ref[...] = dk_acc[:, :LANE].astype(dkn_ref.dtype)
            dkr_ref[...] = dk_acc[:, LANE:].astype(dkr_ref.dtype)
            dv_ref[...] = dv_acc[...].astype(dv_ref.dtype)

    q_spec = pl.BlockSpec((tq, LANE), lambda h, i: (i, h))
    k_spec = pl.BlockSpec((S, LANE), lambda h, i: (0, h))
    t_shape, s_shape = jax.ShapeDtypeStruct((T, H * LANE), BF16), jax.ShapeDtypeStruct((S, H * LANE), BF16)
    return pl.pallas_call(
        body, name="attn_bwd", grid=(H, nq),
        in_specs=[q_spec, q_spec, k_spec, pl.BlockSpec((S, LANE), lambda h, i: (0, H + h)),
                  pl.BlockSpec((S, LANE), lambda h, i: (0, 0)), q_spec, pl.BlockSpec((1, tq, 1), lambda h, i: (h, i, 0))],
        out_specs=[q_spec, q_spec, k_spec, k_spec, k_spec], out_shape=[t_shape, t_shape, s_shape, s_shape, s_shape],
        scratch_shapes=[pltpu.VMEM((S, 2 * LANE), F32), pltpu.VMEM((S, LANE), F32)],
        compiler_params=_cparams(("parallel", "arbitrary")),
    )(qn, qr, kv, kv, kr, do, lse)


def _rope_tables(T):
    rows = T // GRID_W
    row = jnp.repeat(jnp.arange(rows, dtype=F32), GRID_W)
    col = jnp.tile(jnp.arange(GRID_W, dtype=F32), rows)
    n_freq = QK_ROPE // 4
    inv = ROPE_BASE ** (-jnp.arange(n_freq, dtype=F32) / n_freq)
    ar, ac = row[:, None] * inv, col[:, None] * inv
    cos = jnp.concatenate([jnp.cos(ar), jnp.cos(ar), jnp.cos(ac), jnp.cos(ac)], axis=1)
    sin = jnp.concatenate([-jnp.sin(ar), jnp.sin(ar), -jnp.sin(ac), jnp.sin(ac)], axis=1)
    pad = lambda t: jnp.pad(t, ((0, 0), (0, LANE - QK_ROPE)))
    return pad(cos), pad(sin)


def _dw(a, dy, w, name):
    return _mm(a, dy, "tn", BF16, name, out_slots=w.shape[0] if w.ndim == 3 else None)


def _local_step(x, ctx, tgt, m_lat, m_ctx, p, W, goff, hooks=None):
    T, D = x.shape
    Tc = ctx.shape[0]
    S = T + Tc
    S5W = p["s5_d"].shape[1]
    QR, KVR = p["q_norm"].shape[1], p["kv_norm"].shape[1]
    G, N = p["s5_a_re"].shape[1:]
    P = S5_GROUP
    nch = G // 8
    o_cq, o_ckv, o_kr = S5W, S5W + QR, S5W + QR + KVR
    assert o_cq % QR == 0 and o_ckv % KVR == 0 and o_kr % LANE == 0 and goff % D == 0 and S5W % LANE == 0 and G % 8 == 0
    assert 8 * P == LANE
    row = lambda k, m: m[k:k + 1]
    sh1, sc1, g1, sh2, sc2, g2 = (row(k, m_lat) for k in range(6))
    csh1, csc1 = row(0, m_ctx), row(1, m_ctx)
    n1, n2, nf = p["norm1"], p["norm2"], p["norm_f"]

    (xm_lat,) = _rowmap(_normmod, "norm1_lat", T, [x], [n1, sc1, sh1], [(D, BF16)])
    (xm_ctx,) = _rowmap(_normmod, "norm1_ctx", Tc, [ctx], [n1, csc1, csh1], [(D, BF16)])
    xm_all = jnp.concatenate([xm_lat, xm_ctx], axis=0)

    a_re, a_im = p["s5_a_re"][:, :, None, :], p["s5_a_im"][:, :, None, :]
    ldt = p["s5_log_dt"][:, :, None, None]
    b_re, b_im = p["s5_b_re"].transpose(0, 1, 3, 2), p["s5_b_im"].transpose(0, 1, 3, 2)
    wre, wim, vre, vim, pwr, pwi = _s5_tables(a_re, a_im, ldt, b_re, b_im, p["s5_c_re"], p["s5_c_im"])
    waug = _s5_expand(wre, wim, "s5_expand_b")
    vaug = _s5_expand(vre, vim, "s5_expand_c")
    lanes = lambda t: t.reshape(2, SUB + 1, nch, 8 * N).transpose(0, 2, 1, 3)
    pw_re, pw_im = lanes(pwr), lanes(pwi)
    near = lambda t: t[:, :, 1:]
    far = lambda t: t[:, :, :0:-1]
    pw_c = jnp.concatenate([near(pw_re), near(pw_im)], axis=-1)
    pw_a = jnp.concatenate([far(pw_re), far(pw_im)], axis=-1)
    pwc_c = jnp.concatenate([near(pw_re), -near(pw_im)], axis=-1)
    pwc_a = jnp.concatenate([far(pw_re), -far(pw_im)], axis=-1)

    if hooks:
        W = {**W, **hooks["first_weights"](xm_all, vaug)}
    H = W["w_uq"].shape[1] // (2 * LANE)
    h_all = _mm(xm_all, W["w_in"], "nn", F32, "mm_in")
    y0, hs0 = _s5_fwd(h_all, waug, vaug, pw_c[0], S5W, T, 0, True, "s5_scan_fwd0")
    y1, hs1 = _s5_fwd(h_all, waug, vaug, pw_a[1], S5W, T, 1, False, "s5_scan_fwd1")

    def s5_combine(u, yf, yr, dskip):
        y5 = dskip * u + yf + yr
        return y5, jax.nn.gelu(y5)

    y5, z = _rowmap(s5_combine, "s5_combine", T, [(h_all, S5W, 0), y0, y1], [p["s5_d"]], [(S5W, F32), (S5W, BF16)])

    (qn,) = _rowmap(_rms, "q_norm", T, [(h_all, QR, o_cq // QR)], [p["q_norm"]], [(QR, BF16)])
    (kvn,) = _rowmap(_rms, "kv_norm", S, [(h_all, KVR, o_ckv // KVR)], [p["kv_norm"]], [(KVR, BF16)])
    qraw = _mm(qn, W["w_uq"], "nn", F32, "mm_uq")
    kvraw = _mm(kvn, W["w_ukv"], "nn", BF16, "mm_ukv")
    cos_q, sin_q = _rope_tables(T)
    padl = lambda t: jnp.pad(t[:, :LANE], ((0, Tc), (0, 0)))
    cos_k = padl(cos_q) + jnp.pad(jnp.ones((Tc, LANE), F32), ((T, 0), (0, 0)))
    sin_k = padl(sin_q)
    hn = H * LANE

    def q_post(q, cos, sin):
        return q[:, :hn], _rope(q[:, hn:], jnp.tile(cos, (1, H)), jnp.tile(sin, (1, H)))

    q_nope, q_rope = _rowmap(q_post, "q_rope", T, [qraw, cos_q, sin_q], [], [(hn, BF16), (hn, BF16)])
    (kr,) = _rowmap(_rope, "k_rope", S, [(h_all, LANE, o_kr // LANE), cos_k, sin_k], [], [(LANE, BF16)])
    scale = (QK_NOPE + QK_ROPE) ** -0.5
    o, lse = _attn_fwd(q_nope, q_rope, kvraw, kr, H, scale)
    g1_fwd = g1
    if hooks:
        W = {**W, **hooks["mix_weights"](o)}
        g1_fwd = g1 + hooks["ffn_mid"](o)[:1, :1]

    zz = _mm(z, W["w_glu"], "nn", BF16, "mm_glu")
    br_mla = _mm(o, W["w_mla_o"], "nn", BF16, "mm_mla_o")

    def merge(zz, brm, gs, gm):
        a, b = zz[:, :D], zz[:, D:]
        return jax.nn.sigmoid(gs) * (a * jax.nn.sigmoid(b)) + jax.nn.sigmoid(gm) * brm

    gb = goff // D
    merge_ins = [zz, br_mla, (h_all, D, gb), (h_all, D, gb + 1)]
    (mix,) = _rowmap(merge, "merge", T, merge_ins, [], [(D, BF16)])
    out1 = _mm(mix, W["w_out"], "nn", F32, "mm_out")

    def resid_norm2(x, out1, g1, n2, sc2, sh2):
        x1 = x + g1 * out1
        return x1, _normmod(x1, n2, sc2, sh2)

    x1, hm = _rowmap(resid_norm2, "resid_norm2", T, [x, out1], [g1_fwd, n2, sc2, sh2], [(D, F32), (D, BF16)])

    if hooks:
        W = {**W, **hooks["ffn_weights"](hm)}
    FF = W["w_ffn_out"].shape[0]
    assert FF % LANE == 0
    def swiglu_act(a, b):
        return jax.nn.silu(a) * b

    if W["w_ffn_in"].ndim == 3:
        ffn_a, ffn_b, f = _mm_swiglu(hm, W["w_ffn_in"], swiglu_act, "mm_ffn_in")
        ffn_a, ffn_b = (ffn_a, FF, 0), (ffn_b, FF, 0)
    else:
        ab = _mm(hm, W["w_ffn_in"], "nn", BF16, "mm_ffn_in")
        ffn_a, ffn_b = (ab, FF, 0), (ab, FF, 1)
        (f,) = _rowmap(swiglu_act, "ffn_act", T, [ffn_a, ffn_b], [], [(FF, BF16)])
    out2 = _mm(f, W["w_ffn_out"], "nn", F32, "mm_ffn_out")

    def loss_rows(x1, out2, g2, nf, tgt):
        y = _rms(x1 + g2 * out2, nf)
        return 0.5 * jnp.sum(jnp.mean(jnp.square(y - tgt), axis=-1))

    def final(x1, out2, tgt, g2, nf):
        val, (dx1, dout2, dg2, dnf) = jax.value_and_grad(loss_rows, argnums=(0, 1, 2, 3))(x1, out2, g2, nf, tgt)
        return dx1, dout2, jnp.full((1, LANE), val, F32), dg2, dnf

    dx2, dout2, loss_acc, dg2, dnf = _rowmap(final, "final_loss", T, [x1, out2, tgt], [g2, nf],
                                             [(D, F32), (D, BF16)], [LANE, D, D])

    gW = {}
    df = _mm(dout2, W["w_ffn_out"], "nt", BF16, "mm_ffn_out_dx")
    gW["w_ffn_out"] = _dw(f, dout2, W["w_ffn_out"], "mm_ffn_out_dw")

    def swiglu_bwd(a, b, df):
        _, vjp = jax.vjp(swiglu_act, a, b)
        da, db = vjp(df)
        return jnp.concatenate([da, db], axis=1)

    (dab,) = _rowmap(swiglu_bwd, "ffn_act_bwd", T, [ffn_a, ffn_b, df], [], [(2 * FF, BF16)])
    dhm = _mm(dab, W["w_ffn_in"], "nt", F32, "mm_ffn_in_dx")
    gW["w_ffn_in"] = _dw(hm, dab, W["w_ffn_in"], "mm_ffn_in_dw")
    if hooks:
        token = hooks["send_grads"](FFN, [gW.pop(n) for n in FFN])
        g1 = g1 if token is None else g1 + token[:1, :1]

    def resid_norm2_bwd(x, out1, dx2, dhm, g1, n2, sc2, sh2):
        _, vjp = jax.vjp(resid_norm2, x, out1, g1, n2, sc2, sh2)
        dx, dout1, dg1, dn2, dsc2, dsh2 = vjp((dx2, dhm))
        return dx, dout1, dg1, dn2, dsc2, dsh2

    dx1, dout1, dg1, dn2, dsc2, dsh2 = _rowmap(resid_norm2_bwd, "resid_norm2_bwd", T, [x, out1, dx2, dhm],
                                               [g1, n2, sc2, sh2], [(D, F32), (D, BF16)], [D, D, D, D])

    dmix = _mm(dout1, W["w_out"], "nt", BF16, "mm_out_dx")
    gW["w_out"] = _dw(mix, dout1, W["w_out"], "mm_out_dw")

    def merge_bwd(zz, brm, gs, gm, dmix):
        _, vjp = jax.vjp(merge, zz, brm, gs, gm)
        dzz, dbrm, dgs, dgm = vjp(dmix)
        return dzz, dbrm, jnp.concatenate([dgs, dgm], axis=1)

    dzz, dbrm, dgates = _rowmap(merge_bwd, "merge_bwd", T, merge_ins + [dmix], [],
                                [(2 * D, BF16), (D, BF16), (2 * D, BF16)])
    do = _mm(dbrm, W["w_mla_o"], "nt", BF16, "mm_mla_o_dx")
    gW["w_mla_o"] = _dw(o, dbrm, W["w_mla_o"], "mm_mla_o_dw")
    dz = _mm(dzz, W["w_glu"], "nt", BF16, "mm_glu_dx")
    gW["w_glu"] = _dw(z, dzz, W["w_glu"], "mm_glu_dw")
    d_skip_w = p["s5_d"]
    if hooks:
        token = hooks["send_grads"](MIX, [gW.pop(n) for n in MIX])
        d_skip_w = d_skip_w if token is None else d_skip_w + token[:1, :1]

    def s5_combine_bwd(u, y5, dz, dskip):
        _, vjp = jax.vjp(lambda y: jax.nn.gelu(y), y5)
        (dy5,) = vjp(dz)
        return dy5, jnp.sum(dy5 * u, axis=0, keepdims=True)

    dy5, d_skip = _rowmap(s5_combine_bwd, "s5_combine_bwd", T, [(h_all, S5W, 0), y5, dz], [d_skip_w], [(S5W, F32)], [S5W])

    dq_nope, dq_rope, dk_nope, dkr_heads, dv = _attn_bwd(q_nope, q_rope, kvraw, kr, do, lse, H, scale)

    def q_post_bwd(dqn, dqr, cos, sin):
        return jnp.concatenate([dqn, _rope_bwd(dqr, jnp.tile(cos, (1, H)), jnp.tile(sin, (1, H)))], axis=1)

    (dqraw,) = _rowmap(q_post_bwd, "q_rope_bwd", T, [dq_nope, dq_rope, cos_q, sin_q], [], [(2 * hn, BF16)])
    dkvraw = jnp.concatenate([dk_nope, dv], axis=1)

    def k_rope_bwd(dkh, cos, sin):
        d = dkh[:, :LANE]
        for h in range(1, H):
            d = d + dkh[:, h * LANE:(h + 1) * LANE]
        return _rope_bwd(d, cos, sin)

    (dkr,) = _rowmap(k_rope_bwd, "k_rope_bwd", S, [dkr_heads, cos_k, sin_k], [], [(LANE, BF16)])
    dqn = _mm(dqraw, W["w_uq"], "nt", F32, "mm_uq_dx")
    gW["w_uq"] = _dw(qn, dqraw, W["w_uq"], "mm_uq_dw")
    dkvn = _mm(dkvraw, W["w_ukv"], "nt", F32, "mm_ukv_dx")
    gW["w_ukv"] = _dw(kvn, dkvraw, W["w_ukv"], "mm_ukv_dw")

    def rms_bwd(cx, dn, g):
        _, vjp = jax.vjp(_rms, cx, g)
        return vjp(dn)

    dcq, dq_norm = _rowmap(rms_bwd, "q_norm_bwd", T, [(h_all, QR, o_cq // QR), dqn], [p["q_norm"]], [(QR, BF16)], [QR])
    dckv, dkv_norm = _rowmap(rms_bwd, "kv_norm_bwd", S, [(h_all, KVR, o_ckv // KVR), dkvn], [p["kv_norm"]],
                             [(KVR, BF16)], [KVR])

    dy_all = jnp.concatenate([dy5, jnp.zeros((Tc, S5W), F32)], axis=0)
    du0, dbb0, dc0, da0 = _s5_bwd(dy_all, h_all, hs0, waug, vaug, pwc_a[0], S5W, T, 0, True, "s5_scan_bwd0")
    du1, dbb1, dc1, da1 = _s5_bwd(dy_all, h_all, hs1, waug, vaug, pwc_c[1], S5W, T, 1, False, "s5_scan_bwd1")

    def du_combine(a, b, dy, dskip):
        return a + b + dskip * dy

    (du_all,) = _rowmap(du_combine, "s5_du", S, [du0, du1, dy_all], [p["s5_d"]], [(S5W, BF16)])
    dbb = jnp.einsum("dsgpcgn->dcsgpn", jnp.stack([dbb0, dbb1]).reshape(2, nch, 8, P, 2, 8, N)).reshape(2, 2, G, P, N)
    dcm = jnp.einsum("dscgngp->dcsgpn", jnp.stack([dc0, dc1]).reshape(2, nch, 2, 8, N, 8, P)).reshape(2, 2, G, P, N)
    da = jnp.stack([da0, da1]).reshape(2, nch, 2, 8, N).transpose(0, 2, 1, 3, 4).reshape(2, 2, G, 1, N)
    d_lr, d_li, d_ldt, d_br, d_bi = _s5_param_bwd(a_re, a_im, ldt, b_re, b_im, da[:, 0], da[:, 1], dbb[:, 0], dbb[:, 1])

    lat_only = lambda t: jnp.pad(t, ((0, Tc), (0, 0)))
    dh_all = jnp.concatenate([du_all, lat_only(dcq), dckv, dkr, jnp.zeros((S, goff - o_kr - LANE), BF16), lat_only(dgates)],
                             axis=1)
    dxm = _mm(dh_all, W["w_in"], "nt", F32, "mm_in_dx")
    gW["w_in"] = _dw(xm_all, dh_all, W["w_in"], "mm_in_dw")

    def norm1_bwd(x, dxm, dx1, n1, sc, sh):
        _, vjp = jax.vjp(_normmod, x, n1, sc, sh)
        dx, dn, dsc, dsh = vjp(dxm)
        return dx + dx1, dn, dsc, dsh

    grad_x, dn1_l, dsc1, dsh1 = _rowmap(norm1_bwd, "norm1_lat_bwd", T, [x, dxm, dx1], [n1, sc1, sh1], [(D, F32)], [D, D, D])

    def norm1_ctx_bwd(x, dxm, n1, sc, sh):
        _, vjp = jax.vjp(_normmod, x, n1, sc, sh)
        return vjp(dxm)[1:]

    dn1_c, dcsc1, dcsh1 = _rowmap(norm1_ctx_bwd, "norm1_ctx_bwd", Tc, [ctx, dxm[T:]], [n1, csc1, csh1], [], [D, D, D])

    zero = jnp.zeros((1, D), F32)
    dm_lat = jnp.concatenate([dsh1, dsc1, dg1, dsh2, dsc2, dg2], axis=0)
    dm_ctx = jnp.concatenate([dcsh1, dcsc1, zero, zero, zero, zero], axis=0)
    small = {
        "norm1": dn1_l + dn1_c, "norm2": dn2, "norm_f": dnf, "q_norm": dq_norm, "kv_norm": dkv_norm, "s5_d": d_skip,
        "s5_a_re": d_lr, "s5_a_im": d_li, "s5_log_dt": d_ldt, "s5_b_re": d_br.transpose(0, 1, 3, 2),
        "s5_b_im": d_bi.transpose(0, 1, 3, 2), "s5_c_re": dcm[:, 0], "s5_c_im": -dcm[:, 1],
    }
    return loss_acc[:, :1], grad_x, small, dm_lat, dm_ctx, gW


BIG = ("w_in", "w_uq", "w_ukv", "w_glu", "w_mla_o", "w_out", "w_ffn_in", "w_ffn_out")
FFN = ("w_ffn_in", "w_ffn_out")
MIX = ("w_out", "w_mla_o", "w_glu")
ROW_SHARDED = ("w_out", "w_ffn_out")
RELAID = ("w_in", "w_uq", "w_ukv")
SMALL = ("c_ctx", "b_mod", "norm1", "norm2", "s5_a_re", "s5_a_im", "s5_log_dt", "s5_b_re", "s5_b_im", "s5_c_re",
         "s5_c_im", "s5_d", "q_norm", "kv_norm", "norm_f")
S5_BULK = ("s5_b_re", "s5_b_im", "s5_c_re", "s5_c_im")
WEIGHTS = ("c_ctx", "w_mod", "b_mod", "norm1", "norm2", "w_in", "s5_a_re", "s5_a_im", "s5_log_dt", "s5_b_re", "s5_b_im",
           "s5_c_re", "s5_c_im", "s5_d", "w_glu", "q_norm", "kv_norm", "w_uq", "w_ukv", "w_mla_o", "w_out", "w_ffn_in",
           "w_ffn_out", "norm_f")


def _heads_split(w, heads, first):
    k = w.shape[0]
    w3 = w.reshape(k, heads, -1)
    return jnp.concatenate([w3[:, :, :first].reshape(k, -1), w3[:, :, first:].reshape(k, -1)], axis=1)


def _uq_layout(w, heads):
    k = w.shape[0]
    w3 = w.reshape(k, heads, QK_NOPE + QK_ROPE)
    rope = jnp.pad(w3[:, :, QK_NOPE:], ((0, 0), (0, 0), (0, LANE - QK_ROPE)))
    return jnp.concatenate([w3[:, :, :QK_NOPE].reshape(k, -1), rope.reshape(k, -1)], axis=1)


def _uq_unlayout(w, heads):
    k = w.shape[0]
    nope = w[:, :heads * QK_NOPE].reshape(k, heads, QK_NOPE)
    rope = w[:, heads * QK_NOPE:].reshape(k, heads, LANE)[:, :, :QK_ROPE]
    return jnp.concatenate([nope, rope], axis=2).reshape(k, -1)


def _heads_merge(w, heads, first):
    k = w.shape[0]
    a, b = w[:, :heads * first].reshape(k, heads, first), w[:, heads * first:].reshape(k, heads, -1)
    return jnp.concatenate([a, b], axis=2).reshape(k, -1)


def _cols_full(w8):
    return w8.transpose(1, 0, 2).reshape(w8.shape[1], -1)


def _cols_slots(w):
    return w.reshape(w.shape[0], N_DEV, -1).transpose(1, 0, 2)


def _weight_layout(n, w8):
    if n in ROW_SHARDED:
        return w8.reshape(-1, w8.shape[-1])
    return _cols_full(w8) if (n in RELAID or w8.shape[-1] % LANE) else w8


def _grad_slots(n, g):
    if g.ndim == 3:
        return g
    return g.reshape(N_DEV, g.shape[0] // N_DEV, g.shape[1]) if n in ROW_SHARDED else _cols_slots(g)


def _gate_offset(in_cols, D):
    return -(-(in_cols - 2 * D) // D) * D


def _model_weights(g8, D):
    W = {n: _weight_layout(n, w8) for n, w8 in g8.items()}
    w_in = W["w_in"]
    n_front = w_in.shape[1] - 2 * D
    goff = _gate_offset(w_in.shape[1], D)
    W["w_in"] = jnp.concatenate([w_in[:, :n_front], jnp.zeros((D, goff - n_front), w_in.dtype), w_in[:, n_front:]], axis=1)
    heads = W["w_uq"].shape[1] // (QK_NOPE + QK_ROPE)
    W["w_uq"] = _uq_layout(W["w_uq"], heads)
    W["w_ukv"] = _heads_split(W["w_ukv"], heads, QK_NOPE)
    return W, goff


def kernel(x, c, ctx, c_ctx, w_mod, b_mod, norm1, norm2, w_in, s5_a_re, s5_a_im, s5_log_dt, s5_b_re, s5_b_im, s5_c_re, s5_c_im, s5_d, w_glu, q_norm, kv_norm, w_uq, w_ukv, w_mla_o, w_out, w_ffn_in, w_ffn_out, norm_f, loss_target, m_c_ctx, m_w_mod, m_b_mod, m_norm1, m_norm2, m_w_in, m_s5_a_re, m_s5_a_im, m_s5_log_dt, m_s5_b_re, m_s5_b_im, m_s5_c_re, m_s5_c_im, m_s5_d, m_w_glu, m_q_norm, m_kv_norm, m_w_uq, m_w_ukv, m_w_mla_o, m_w_out, m_w_ffn_in, m_w_ffn_out, m_norm_f, v_c_ctx, v_w_mod, v_b_mod, v_norm1, v_norm2, v_w_in, v_s5_a_re, v_s5_a_im, v_s5_log_dt, v_s5_b_re, v_s5_b_im, v_s5_c_re, v_s5_c_im, v_s5_d, v_w_glu, v_q_norm, v_kv_norm, v_w_uq, v_w_ukv, v_w_mla_o, v_w_out, v_w_ffn_in, v_w_ffn_out, v_norm_f):
    a = dict(locals())
    D = x.shape[-1]
    me = 4 * lax.axis_index("x") + 2 * lax.axis_index("y") + lax.axis_index("c")

    shard = {n: a[n][0] for n in BIG}
    first = [n for n in BIG if n not in FFN + MIX]
    (cg,) = _all_gather([jnp.broadcast_to(c, (8, D))], "ag_c")
    goff = _gate_offset(w_in.shape[-1] * N_DEV, D)

    wm = w_mod[0]
    ncol = wm.shape[1]
    c16 = jnp.concatenate([cg[:, 0, :], c_ctx[None], jnp.zeros((7, D), F32)], axis=0)
    (s16,) = _rowmap(jax.nn.silu, "mod_silu", 16, [c16], [], [(D, BF16)])
    m_cols = _mm(s16, wm, "nn", F32, "mm_mod")
    (mg,) = _all_gather([m_cols], "ag_mod")
    (m16,) = _rowmap(lambda m, b: m + b, "mod_bias", 16, [_cols_full(mg)], [b_mod], [(N_DEV * ncol, F32)])

    first_blocks = [shard[n].astype(BF16) for n in first]
    fst = {}
    fst["sems1"], fst["thru"], first_token = _ag2_start(first_blocks, [_own_slot(b, me) for b in first_blocks], "ag_first_start")

    def first_weights(after_norm, after_tables):
        sems2, thru, _ = _ag2_mid(fst["sems1"], fst["thru"], after_tables, "ag_first_mid")
        lands = _ag2_end(fst["sems1"], sems2, thru, after_norm, "ag_first_end")
        return _model_weights(dict(zip(first, lands)), D)[0]

    mix_blocks = [shard[n].astype(BF16) for n in MIX]
    mix = _xchg_start(mix_blocks, [_own_slot(b, me) for b in mix_blocks], False, "ag_mix_start")
    ffn_blocks = [shard[n].astype(BF16) for n in FFN]
    ffn = {}
    ffn["sems1"], ffn["thru"], ag_token = _ag2_start(ffn_blocks, [_own_slot(b, me) for b in ffn_blocks], "ag_ffn_start")
    m16 = m16 + (first_token[:1, :1] + mix[3][:1, :1] + ag_token[:1, :1])

    def mix_weights(after):
        lands = _xchg_wait(mix[0], mix[1], mix[2], after, False, "ag_mix_wait")
        return {n: _weight_layout(n, w8) for n, w8 in zip(MIX, lands)}

    def ffn_mid(after):
        ffn["sems2"], ffn["thru"], token = _ag2_mid(ffn["sems1"], ffn["thru"], after, "ag_ffn_mid")
        return token

    def ffn_weights(after):
        lands = _ag2_end(ffn["sems1"], ffn["sems2"], ffn["thru"], after, "ag_ffn_end")
        return {n: _weight_layout(n, w8) for n, w8 in zip(FFN, lands)}

    rs_async = {}

    def send_grads(names, gs):
        slots = [_grad_slots(n, g) for n, g in zip(names, gs)]
        lands = [_own_slot(lax.dynamic_index_in_dim(s, me, 0, keepdims=False), me) for s in slots]
        rs_async[names] = _xchg_start(slots, lands, True, "rs_start_" + names[0])
        return rs_async[names][3]

    m_lat = lax.dynamic_slice(m16, (me, 0), (1, 6 * D)).reshape(6, D)
    m_ctx = m16[8].reshape(6, D)

    p = {n: a[n][0] for n in ("norm1", "norm2", "s5_a_re", "s5_a_im", "s5_log_dt", "s5_b_re", "s5_b_im", "s5_c_re",
                              "s5_c_im", "q_norm", "kv_norm")}
    p = {k: (v[None] if v.ndim == 1 else v) for k, v in p.items()}
    p["s5_d"] = s5_d.reshape(1, -1)
    p["norm_f"] = norm_f[None]
    hooks = dict(first_weights=first_weights, mix_weights=mix_weights, ffn_mid=ffn_mid, ffn_weights=ffn_weights,
                 send_grads=send_grads)
    loss_part, grad_x, small, dm_lat, dm_ctx, gW = _local_step(x[0], ctx[0], loss_target[0], m_lat, m_ctx, p, {}, goff, hooks)
    loss = lax.psum(loss_part[0, 0], ("x", "y", "c"))

    gW = dict(gW)
    n_front = w_in.shape[-1] * N_DEV - 2 * D
    gW["w_in"] = jnp.concatenate([gW["w_in"][:, :n_front], gW["w_in"][:, goff:]], axis=1)
    heads = gW["w_uq"].shape[1] // (2 * LANE)
    gW["w_uq"] = _uq_unlayout(gW["w_uq"], heads)
    gW["w_ukv"] = _heads_merge(gW["w_ukv"], heads, QK_NOPE)
    last = [n for n in BIG if n in gW]
    slots = [_grad_slots(n, gW[n]) for n in last]
    from_sibling = _rs_pair(slots, "rs_pair")
    chip_sums = [_add_pair(pp, rr, "rs_add_" + n, grad_x) for n, pp, rr in zip(last, slots, from_sibling)]

    dm8 = jnp.concatenate([dm_lat.reshape(1, -1), dm_ctx.reshape(1, -1), jnp.zeros((SUB - 2, 6 * D), F32)], axis=0)
    (dmg,) = _all_gather([dm8], "ag_dmod", after=chip_sums[0])
    dm_sum = _sum_slots(dmg, "sum_dmod")
    dM16 = jnp.concatenate([dmg[:, 0, :], dm_sum[1:2], jnp.zeros((7, 6 * D), F32)], axis=0)
    (g_b_mod,) = _rowmap(lambda d: jnp.sum(d, axis=0, keepdims=True), "b_mod_grad", 16, [dM16], [], [], [6 * D])
    dM_loc = lax.dynamic_slice(dM16, (0, me * ncol), (16, ncol))
    g_w_mod = _mm(s16, dM_loc, "tn", F32, "mm_mod_dw")
    ds16_part = _mm(dM_loc, wm, "nt", F32, "mm_mod_dx")

    fine = [n for n in SMALL if n not in ("c_ctx", "b_mod") + S5_BULK]
    small_blocks = [_pack_rows([small[n] for n in fine] + [ds16_part[8:9]], F32), _pack_rows([small[n] for n in S5_BULK], BF16)]
    sm_sems1, sm_thru, sm_token = _ag2_start(small_blocks, [_own_slot(b, me) for b in small_blocks], "ag_small_start")
    grads = {"b_mod": g_b_mod}

    def small_grads(after):
        sems2, thru, token = _ag2_mid(sm_sems1, sm_thru, after, "ag_small_mid")
        sg, sgb = _ag2_end(sm_sems1, sems2, thru, token, "ag_small_end")
        parts = _unpack_rows(_sum_slots(sg, "sum_small"), [small[n].shape for n in fine] + [(1, D)])
        grads.update(zip(fine, parts[:-1]))
        grads.update(zip(S5_BULK, _unpack_rows(_sum_slots(sgb, "sum_small_bulk"), [small[n].shape for n in S5_BULK])))

        def silu_bwd(cc, ds):
            _, vjp = jax.vjp(jax.nn.silu, cc)
            return vjp(ds)[0]

        (grads["c_ctx"],) = _rowmap(silu_bwd, "c_ctx_grad", 1, [c_ctx[None], parts[-1]], [], [(D, F32)])


    my_chip = 2 * lax.axis_index("x") + lax.axis_index("y")
    lands = [_own_slot(lax.dynamic_index_in_dim(q, my_chip, 0, keepdims=False) + sm_token[:1, :1].astype(q.dtype), my_chip, N_CHIP)
             for q in chip_sums]
    rs_send, rs_recv, rs_thru, behind = _xchg_start(chip_sums, lands, True, "rs_chips_start")
    partials = {}
    for names, (send, recv, thru, _) in rs_async.items():
        partials.update(zip(names, _xchg_wait(send, recv, thru, behind, True, "rs_wait_" + names[0])))

    out = {}

    def adamw_big(n, after):
        w2, m2, v2 = a[n][0], a["m_" + n][0], a["v_" + n][0]
        if n in partials:
            g, d, nm, nv = _adamw_slots(w2, partials[n], m2, v2, "adamw_" + n, after)
        else:
            g = g_w_mod
            d, nm, nv = _adamw(w2, g, m2, v2, "adamw_" + n, after)
        for k, val in (("grad_", g), ("delta_", d), ("new_m_", nm), ("new_v_", nv)):
            out[k + n] = val.reshape(a[n].shape)
        return nv

    for n in FFN + MIX + ("w_mod",):
        behind = adamw_big(n, behind)
    small_grads(behind)
    packs = [_pack_rows([t[n] for n in SMALL], F32) for t in (
        {n: a[n] for n in SMALL}, {n: grads[n] for n in SMALL}, {n: a["m_" + n] for n in SMALL}, {n: a["v_" + n] for n in SMALL})]
    res = _adamw(*packs, "adamw_small")
    partials.update(zip(last, _xchg_wait(rs_send, rs_recv, rs_thru, res[2], True, "rs_chips_wait")))
    for n in last:
        adamw_big(n, None)
    shapes = [a[n].shape for n in SMALL]
    for k, packed in (("grad_", packs[1]), ("delta_", res[0]), ("new_m_", res[1]), ("new_v_", res[2])):
        for n, val in zip(SMALL, _unpack_rows(packed, shapes)):
            out[k + n] = val
    return (loss, grad_x[None]) + tuple(out[k + n] for k in ("grad_", "delta_", "new_m_", "new_v_") for n in WEIGHTS)
```
